```python
import jax, jax.numpy as jnp
from jax import lax
import numpy as np

D_MODEL = 2048
BATCH = 8
SEQ = 2048
DEPTH = 1

HEAD_DIM = 64
N_Q_HEADS = D_MODEL // HEAD_DIM
N_KV_HEADS = 4
GQA_GROUP = N_Q_HEADS // N_KV_HEADS
WINDOW = 128

GLA_HEADS = 4
GLA_DK = (D_MODEL // 2) // GLA_HEADS
GLA_DV = D_MODEL // GLA_HEADS
GLA_GATE_RANK = 16
GLA_GATE_NORMALIZER = 16.0
GLA_CHUNK = 64

FFN_HIDDEN = ((8 * D_MODEL // 3 + 255) // 256) * 256

RMS_EPS = 1e-6
MASK_VALUE = -1e30

IN_WIDTHS = (
    N_Q_HEADS * HEAD_DIM,
    N_KV_HEADS * HEAD_DIM,
    N_KV_HEADS * HEAD_DIM,
    GLA_HEADS * GLA_DK,
    GLA_HEADS * GLA_DK,
    GLA_HEADS * GLA_DV,
    GLA_GATE_RANK,
    GLA_HEADS * GLA_DV,
    D_MODEL,
    D_MODEL,
)
D_IN = sum(IN_WIDTHS)

kernel_name = "hybrid_swa_sink_gla_swiglu_block"


def _rmsnorm(x, w):
    xf = x.astype(jnp.float32)
    y = xf * lax.rsqrt(jnp.mean(xf * xf, axis=-1, keepdims=True) + RMS_EPS)
    return (y * w.astype(jnp.float32)).astype(x.dtype)


def _split_offsets():
    offs, acc = [], 0
    for w in IN_WIDTHS[:-1]:
        acc += w
        offs.append(acc)
    return offs


def _swa_sink_attention(q, k, v, sinks):
    B, T = q.shape[0], q.shape[1]
    nb = T // WINDOW
    qb = q.reshape(B, nb, WINDOW, N_KV_HEADS, GQA_GROUP, HEAD_DIM)
    kb = k.reshape(B, nb, WINDOW, N_KV_HEADS, HEAD_DIM)
    vb = v.reshape(B, nb, WINDOW, N_KV_HEADS, HEAD_DIM)
    k_prev = jnp.concatenate([jnp.zeros_like(kb[:, :1]), kb[:, :-1]], axis=1)
    v_prev = jnp.concatenate([jnp.zeros_like(vb[:, :1]), vb[:, :-1]], axis=1)
    kk = jnp.concatenate([k_prev, kb], axis=2)
    vv = jnp.concatenate([v_prev, vb], axis=2)
    s = jnp.einsum('bnqhgd,bnkhd->bhgnqk', qb, kk).astype(jnp.float32) * (HEAD_DIM ** -0.5)
    qi = jnp.arange(WINDOW)[:, None]
    ki = jnp.arange(2 * WINDOW)[None, :]
    rel = qi + WINDOW - ki
    band = (rel >= 0) & (rel < WINDOW)
    blk = jnp.arange(nb)[:, None, None]
    mask = band[None] & ((blk > 0) | (ki[None] >= WINDOW))
    s = jnp.where(mask, s, MASK_VALUE)
    sink = sinks.astype(jnp.float32).reshape(N_KV_HEADS, GQA_GROUP)[None, :, :, None, None, None]
    sink = jnp.broadcast_to(sink, s.shape[:-1] + (1,))
    p = jax.nn.softmax(jnp.concatenate([s, sink], axis=-1), axis=-1)[..., :-1]
    o = jnp.einsum('bhgnqk,bnkhd->bnqhgd', p.astype(vv.dtype), vv)
    return o.reshape(B, T, N_Q_HEADS * HEAD_DIM)


def _gla(q, k, v, log_a):
    B, T = q.shape[0], q.shape[1]
    nc = T // GLA_CHUNK

    def chunk(t, d):
        return t.astype(jnp.float32).reshape(B, nc, GLA_CHUNK, GLA_HEADS, d).transpose(0, 3, 1, 2, 4)

    qc = chunk(q, GLA_DK) * (GLA_DK ** -0.5)
    kc = chunk(k, GLA_DK)
    vc = chunk(v, GLA_DV)
    g = jnp.cumsum(chunk(log_a, GLA_DK), axis=3)
    g_last = g[..., -1:, :]
    q_dec = qc * jnp.exp(g)
    k_inv = kc * jnp.exp(-g)
    k_to_end = kc * jnp.exp(g_last - g)
    causal = jnp.tril(jnp.ones((GLA_CHUNK, GLA_CHUNK), dtype=bool))
    att = jnp.where(causal, jnp.einsum('bhnid,bhnjd->bhnij', q_dec, k_inv), 0.0)
    o_intra = jnp.einsum('bhnij,bhnjv->bhniv', att, vc)
    upd = jnp.einsum('bhnjd,bhnjv->bhndv', k_to_end, vc)
    decay = jnp.exp(g_last[..., 0, :])

    def step(state, inp):
        d_c, u_c = inp
        return d_c[..., None] * state + u_c, state

    s0 = jnp.zeros((B, GLA_HEADS, GLA_DK, GLA_DV), jnp.float32)
    _, s_prev = lax.scan(step, s0, (jnp.moveaxis(decay, 2, 0), jnp.moveaxis(upd, 2, 0)))
    s_prev = jnp.moveaxis(s_prev, 0, 2)
    o_inter = jnp.einsum('bhnid,bhndv->bhniv', q_dec, s_prev)
    o = o_intra + o_inter
    return o.transpose(0, 2, 3, 1, 4).reshape(B, T, GLA_HEADS, GLA_DV)


def _fwd_setup_inputs(seed: int = 0) -> dict:
    key = jax.random.key(seed)
    ks = jax.random.split(key, 14)
    f32 = jnp.float32

    def normal(k, shape, scale):
        return jax.random.normal(k, shape, f32) * scale

    return {
        "x": jax.random.normal(ks[0], (BATCH, SEQ, D_MODEL), f32),
        "norm1_w": 1.0 + normal(ks[1], (DEPTH, D_MODEL), 0.02),
        "w_in": normal(ks[2], (DEPTH, D_MODEL, D_IN), D_MODEL ** -0.5),
        "gla_gate_w2": normal(ks[3], (DEPTH, GLA_GATE_RANK, GLA_HEADS * GLA_DK), GLA_GATE_RANK ** -0.5),
        "gla_gate_b": normal(ks[4], (DEPTH, GLA_HEADS * GLA_DK), 0.02),
        "attn_sinks": normal(ks[5], (DEPTH, N_Q_HEADS), 0.5),
        "gla_norm_w": 1.0 + normal(ks[6], (DEPTH, GLA_DV), 0.02),
        "w_out": normal(ks[7], (DEPTH, D_MODEL, D_MODEL), D_MODEL ** -0.5),
        "norm2_w": 1.0 + normal(ks[8], (DEPTH, D_MODEL), 0.02),
        "w_ffn_gate": normal(ks[9], (DEPTH, D_MODEL, FFN_HIDDEN), D_MODEL ** -0.5),
        "w_ffn_up": normal(ks[10], (DEPTH, D_MODEL, FFN_HIDDEN), D_MODEL ** -0.5),
        "w_ffn_down": normal(ks[11], (DEPTH, FFN_HIDDEN, D_MODEL), FFN_HIDDEN ** -0.5),
        "final_norm_w": 1.0 + normal(ks[12], (D_MODEL,), 0.02),
    }


def _fwd_reference(x, norm1_w, w_in, gla_gate_w2, gla_gate_b, attn_sinks, gla_norm_w, w_out,
              norm2_w, w_ffn_gate, w_ffn_up, w_ffn_down, final_norm_w):
    B, T, _ = x.shape
    offs = _split_offsets()
    h = x
    for l in range(DEPTH):
        u = _rmsnorm(h, norm1_w[l])
        proj = u @ w_in[l]
        aq, ak, av, gq, gk, gv, g_lr, g_r, gate_a, gate_b = jnp.split(proj, offs, axis=-1)

        attn_o = _swa_sink_attention(aq, ak, av, attn_sinks[l])

        gate_logit = (g_lr @ gla_gate_w2[l] + gla_gate_b[l]).astype(jnp.float32)
        log_a = jax.nn.log_sigmoid(gate_logit) / GLA_GATE_NORMALIZER
        gla_o = _gla(gq.reshape(B, T, GLA_HEADS, GLA_DK),
                     gk.reshape(B, T, GLA_HEADS, GLA_DK),
                     gv.reshape(B, T, GLA_HEADS, GLA_DV),
                     log_a.reshape(B, T, GLA_HEADS, GLA_DK))
        gla_o = _rmsnorm(gla_o, gla_norm_w[l]).astype(x.dtype)
        gla_o = gla_o.reshape(B, T, GLA_HEADS * GLA_DV) * jax.nn.silu(g_r)

        merged = jax.nn.sigmoid(gate_a) * attn_o + jax.nn.sigmoid(gate_b) * gla_o
        h = h + merged @ w_out[l]

        v2 = _rmsnorm(h, norm2_w[l])
        ff = jax.nn.silu(v2 @ w_ffn_gate[l]) * (v2 @ w_ffn_up[l])
        h = h + ff @ w_ffn_down[l]
    return _rmsnorm(h, final_norm_w)


import jax as _jax
import jax.numpy as _jnp

TWIN_FORMAT = 'train_step'
FWD_PARAMS = ['x', 'norm1_w', 'w_in', 'gla_gate_w2', 'gla_gate_b', 'attn_sinks', 'gla_norm_w', 'w_out', 'norm2_w', 'w_ffn_gate', 'w_ffn_up', 'w_ffn_down', 'final_norm_w']
TWIN_WEIGHTS = ['norm1_w', 'w_in', 'gla_gate_w2', 'gla_gate_b', 'attn_sinks', 'gla_norm_w', 'w_out', 'norm2_w', 'w_ffn_gate', 'w_ffn_up', 'w_ffn_down', 'final_norm_w']
TWIN_DIFF_INPUT = 'x'
TWIN_INPUTS = ['x', 'norm1_w', 'w_in', 'gla_gate_w2', 'gla_gate_b', 'attn_sinks', 'gla_norm_w', 'w_out', 'norm2_w', 'w_ffn_gate', 'w_ffn_up', 'w_ffn_down', 'final_norm_w', 'loss_target', 'm_norm1_w', 'm_w_in', 'm_gla_gate_w2', 'm_gla_gate_b', 'm_attn_sinks', 'm_gla_norm_w', 'm_w_out', 'm_norm2_w', 'm_w_ffn_gate', 'm_w_ffn_up', 'm_w_ffn_down', 'm_final_norm_w', 'v_norm1_w', 'v_w_in', 'v_gla_gate_w2', 'v_gla_gate_b', 'v_attn_sinks', 'v_gla_norm_w', 'v_w_out', 'v_norm2_w', 'v_w_ffn_gate', 'v_w_ffn_up', 'v_w_ffn_down', 'v_final_norm_w']
TWIN_OUTPUTS = ['loss', 'grad_x', 'grad_norm1_w', 'grad_w_in', 'grad_gla_gate_w2', 'grad_gla_gate_b', 'grad_attn_sinks', 'grad_gla_norm_w', 'grad_w_out', 'grad_norm2_w', 'grad_w_ffn_gate', 'grad_w_ffn_up', 'grad_w_ffn_down', 'grad_final_norm_w', 'delta_norm1_w', 'delta_w_in', 'delta_gla_gate_w2', 'delta_gla_gate_b', 'delta_attn_sinks', 'delta_gla_norm_w', 'delta_w_out', 'delta_norm2_w', 'delta_w_ffn_gate', 'delta_w_ffn_up', 'delta_w_ffn_down', 'delta_final_norm_w', 'new_m_norm1_w', 'new_m_w_in', 'new_m_gla_gate_w2', 'new_m_gla_gate_b', 'new_m_attn_sinks', 'new_m_gla_norm_w', 'new_m_w_out', 'new_m_norm2_w', 'new_m_w_ffn_gate', 'new_m_w_ffn_up', 'new_m_w_ffn_down', 'new_m_final_norm_w', 'new_v_norm1_w', 'new_v_w_in', 'new_v_gla_gate_w2', 'new_v_gla_gate_b', 'new_v_attn_sinks', 'new_v_gla_norm_w', 'new_v_w_out', 'new_v_norm2_w', 'new_v_w_ffn_gate', 'new_v_w_ffn_up', 'new_v_w_ffn_down', 'new_v_final_norm_w']
TWIN_LEAF_KINDS = {'loss': 'loss', 'grad_x': 'grad_x', 'grad_norm1_w': 'grad_w', 'grad_w_in': 'grad_w', 'grad_gla_gate_w2': 'grad_w', 'grad_gla_gate_b': 'grad_w', 'grad_attn_sinks': 'grad_w', 'grad_gla_norm_w': 'grad_w', 'grad_w_out': 'grad_w', 'grad_norm2_w': 'grad_w', 'grad_w_ffn_gate': 'grad_w', 'grad_w_ffn_up': 'grad_w', 'grad_w_ffn_down': 'grad_w', 'grad_final_norm_w': 'grad_w', 'delta_norm1_w': 'delta_w', 'delta_w_in': 'delta_w', 'delta_gla_gate_w2': 'delta_w', 'delta_gla_gate_b': 'delta_w', 'delta_attn_sinks': 'delta_w', 'delta_gla_norm_w': 'delta_w', 'delta_w_out': 'delta_w', 'delta_norm2_w': 'delta_w', 'delta_w_ffn_gate': 'delta_w', 'delta_w_ffn_up': 'delta_w', 'delta_w_ffn_down': 'delta_w', 'delta_final_norm_w': 'delta_w', 'new_m_norm1_w': 'new_m', 'new_m_w_in': 'new_m', 'new_m_gla_gate_w2': 'new_m', 'new_m_gla_gate_b': 'new_m', 'new_m_attn_sinks': 'new_m', 'new_m_gla_norm_w': 'new_m', 'new_m_w_out': 'new_m', 'new_m_norm2_w': 'new_m', 'new_m_w_ffn_gate': 'new_m', 'new_m_w_ffn_up': 'new_m', 'new_m_w_ffn_down': 'new_m', 'new_m_final_norm_w': 'new_m', 'new_v_norm1_w': 'new_v', 'new_v_w_in': 'new_v', 'new_v_gla_gate_w2': 'new_v', 'new_v_gla_gate_b': 'new_v', 'new_v_attn_sinks': 'new_v', 'new_v_gla_norm_w': 'new_v', 'new_v_w_out': 'new_v', 'new_v_norm2_w': 'new_v', 'new_v_w_ffn_gate': 'new_v', 'new_v_w_ffn_up': 'new_v', 'new_v_w_ffn_down': 'new_v', 'new_v_final_norm_w': 'new_v'}


def _forward(args):
    return _fwd_reference(*[args[k] for k in FWD_PARAMS])


def _output_shape():
    out = _jax.eval_shape(lambda: _forward(_fwd_setup_inputs(0)))
    return out.shape, out.dtype

N_MICROBATCH = 1
ADAM_LR = 0.001
ADAM_B1 = 0.9
ADAM_B2 = 0.999
ADAM_EPS = 1e-08
ADAM_WD = 0.01
ADAM_STEP = 10
PER_EXAMPLE_BATCH_AXIS = {'x': 0, 'loss_target': 0}
SHARED_INPUTS = []
_WEIGHT_DTYPES = {'norm1_w': _jnp.float32, 'w_in': _jnp.float32, 'gla_gate_w2': _jnp.float32, 'gla_gate_b': _jnp.float32, 'attn_sinks': _jnp.float32, 'gla_norm_w': _jnp.float32, 'w_out': _jnp.float32, 'norm2_w': _jnp.float32, 'w_ffn_gate': _jnp.float32, 'w_ffn_up': _jnp.float32, 'w_ffn_down': _jnp.float32, 'final_norm_w': _jnp.float32}
MOMENT_SCALE = {'norm1_w': 4.668686e-02, 'w_in': 1.854942e-02, 'gla_gate_w2': 3.491953e-03, 'gla_gate_b': 1.426166e-02, 'attn_sinks': 5.757826e-03, 'gla_norm_w': 4.261830e-02, 'w_out': 2.262184e-02, 'norm2_w': 4.443436e-02, 'w_ffn_gate': 1.926292e-02, 'w_ffn_up': 1.865161e-02, 'w_ffn_down': 3.092235e-02, 'final_norm_w': 8.004748e+00}


def _to_microbatches(a, axis):
    t = _jnp.moveaxis(a, axis, 0)
    t = t.reshape((N_MICROBATCH, t.shape[0] // N_MICROBATCH) + t.shape[1:])
    return _jnp.moveaxis(t, 1, axis + 1)


def setup_inputs(seed: int = 0) -> dict:
    inp = _fwd_setup_inputs(seed)
    key = _jax.random.fold_in(_jax.random.key(seed), 7919)
    shape, _ = _output_shape()
    out = dict(inp)
    out["loss_target"] = _jax.random.normal(_jax.random.fold_in(key, 0), shape, _jnp.float32)
    for i, name in enumerate(TWIN_WEIGHTS):
        w = inp[name].astype(_jnp.float32)
        if MOMENT_SCALE is None:
            s = _jnp.sqrt(_jnp.mean(_jnp.square(w)) + 1e-30)
        else:
            s = MOMENT_SCALE[name]
        km, kv = _jax.random.split(_jax.random.fold_in(key, i + 1))
        out[name] = w
        out["m_" + name] = s * _jax.random.normal(km, w.shape, _jnp.float32)
        out["v_" + name] = (s * s) * _jax.random.uniform(kv, w.shape, _jnp.float32, 0.5, 1.5)
    if N_MICROBATCH > 1:
        for name, axis in PER_EXAMPLE_BATCH_AXIS.items():
            out[name] = _to_microbatches(out[name], axis)
    return {'x': out['x'], 'norm1_w': out['norm1_w'], 'w_in': out['w_in'], 'gla_gate_w2': out['gla_gate_w2'], 'gla_gate_b': out['gla_gate_b'], 'attn_sinks': out['attn_sinks'], 'gla_norm_w': out['gla_norm_w'], 'w_out': out['w_out'], 'norm2_w': out['norm2_w'], 'w_ffn_gate': out['w_ffn_gate'], 'w_ffn_up': out['w_ffn_up'], 'w_ffn_down': out['w_ffn_down'], 'final_norm_w': out['final_norm_w'], 'loss_target': out['loss_target'], 'm_norm1_w': out['m_norm1_w'], 'm_w_in': out['m_w_in'], 'm_gla_gate_w2': out['m_gla_gate_w2'], 'm_gla_gate_b': out['m_gla_gate_b'], 'm_attn_sinks': out['m_attn_sinks'], 'm_gla_norm_w': out['m_gla_norm_w'], 'm_w_out': out['m_w_out'], 'm_norm2_w': out['m_norm2_w'], 'm_w_ffn_gate': out['m_w_ffn_gate'], 'm_w_ffn_up': out['m_w_ffn_up'], 'm_w_ffn_down': out['m_w_ffn_down'], 'm_final_norm_w': out['m_final_norm_w'], 'v_norm1_w': out['v_norm1_w'], 'v_w_in': out['v_w_in'], 'v_gla_gate_w2': out['v_gla_gate_w2'], 'v_gla_gate_b': out['v_gla_gate_b'], 'v_attn_sinks': out['v_attn_sinks'], 'v_gla_norm_w': out['v_gla_norm_w'], 'v_w_out': out['v_w_out'], 'v_norm2_w': out['v_norm2_w'], 'v_w_ffn_gate': out['v_w_ffn_gate'], 'v_w_ffn_up': out['v_w_ffn_up'], 'v_w_ffn_down': out['v_w_ffn_down'], 'v_final_norm_w': out['v_final_norm_w']}


def _loss(weights, diff, rest, loss_target):
    with _jax.named_scope("forward"):
        args = {**rest, TWIN_DIFF_INPUT: diff, **{k: w.astype(_WEIGHT_DTYPES[k]) for k, w in weights.items()}}
        y = _forward(args)
    with _jax.named_scope("loss_head"):
        err = _jnp.square(y.astype(_jnp.float32) - loss_target)
        return 0.5 * _jnp.sum(_jnp.mean(err, axis=-1)) if err.ndim else 0.5 * err


def _adamw(w, g, m, v):
    m = ADAM_B1 * m + (1.0 - ADAM_B1) * g
    v = ADAM_B2 * v + (1.0 - ADAM_B2) * _jnp.square(g)
    m_hat = m / (1.0 - ADAM_B1 ** ADAM_STEP)
    v_hat = v / (1.0 - ADAM_B2 ** ADAM_STEP)
    delta = -ADAM_LR * (m_hat / (_jnp.sqrt(v_hat) + ADAM_EPS) + ADAM_WD * w)
    return delta, m, v


def reference(x, norm1_w, w_in, gla_gate_w2, gla_gate_b, attn_sinks, gla_norm_w, w_out, norm2_w, w_ffn_gate, w_ffn_up, w_ffn_down, final_norm_w, loss_target, m_norm1_w, m_w_in, m_gla_gate_w2, m_gla_gate_b, m_attn_sinks, m_gla_norm_w, m_w_out, m_norm2_w, m_w_ffn_gate, m_w_ffn_up, m_w_ffn_down, m_final_norm_w, v_norm1_w, v_w_in, v_gla_gate_w2, v_gla_gate_b, v_attn_sinks, v_gla_norm_w, v_w_out, v_norm2_w, v_w_ffn_gate, v_w_ffn_up, v_w_ffn_down, v_final_norm_w):
    given = dict(x=x, norm1_w=norm1_w, w_in=w_in, gla_gate_w2=gla_gate_w2, gla_gate_b=gla_gate_b, attn_sinks=attn_sinks, gla_norm_w=gla_norm_w, w_out=w_out, norm2_w=norm2_w, w_ffn_gate=w_ffn_gate, w_ffn_up=w_ffn_up, w_ffn_down=w_ffn_down, final_norm_w=final_norm_w, loss_target=loss_target, m_norm1_w=m_norm1_w, m_w_in=m_w_in, m_gla_gate_w2=m_gla_gate_w2, m_gla_gate_b=m_gla_gate_b, m_attn_sinks=m_attn_sinks, m_gla_norm_w=m_gla_norm_w, m_w_out=m_w_out, m_norm2_w=m_norm2_w, m_w_ffn_gate=m_w_ffn_gate, m_w_ffn_up=m_w_ffn_up, m_w_ffn_down=m_w_ffn_down, m_final_norm_w=m_final_norm_w, v_norm1_w=v_norm1_w, v_w_in=v_w_in, v_gla_gate_w2=v_gla_gate_w2, v_gla_gate_b=v_gla_gate_b, v_attn_sinks=v_attn_sinks, v_gla_norm_w=v_gla_norm_w, v_w_out=v_w_out, v_norm2_w=v_norm2_w, v_w_ffn_gate=v_w_ffn_gate, v_w_ffn_up=v_w_ffn_up, v_w_ffn_down=v_w_ffn_down, v_final_norm_w=v_final_norm_w)
    weights = {n: given[n] for n in TWIN_WEIGHTS}
    shared = {n: given[n] for n in SHARED_INPUTS}
    per_example = {n: given[n] for n in ['x']}
    grad_fn = _jax.value_and_grad(_loss, argnums=(0, 1))

    def one_microbatch(ex, loss_target):
        ex = dict(ex)
        diff = ex.pop(TWIN_DIFF_INPUT)
        return grad_fn(weights, diff, {**shared, **ex}, loss_target)

    if N_MICROBATCH == 1:
        loss, (grad_w, grad_x) = one_microbatch(per_example, given["loss_target"])
    else:
        def body(carry, xs):
            loss_sum, grad_sum = carry
            l_k, (gw_k, gx_k) = one_microbatch(xs[0], xs[1])
            with _jax.named_scope("update"):
                return (loss_sum + l_k, _jax.tree.map(_jnp.add, grad_sum, gw_k)), gx_k

        init = (_jnp.zeros((), _jnp.float32), _jax.tree.map(_jnp.zeros_like, weights))
        (loss, grad_w), grad_x = _jax.lax.scan(body, init, (per_example, given["loss_target"]))
    with _jax.named_scope("update"):
        delta_w, new_m, new_v = {}, {}, {}
        for n in TWIN_WEIGHTS:
            delta_w[n], new_m[n], new_v[n] = _adamw(weights[n], grad_w[n], given["m_" + n], given["v_" + n])
    return (loss, grad_x, *[grad_w[n] for n in TWIN_WEIGHTS], *[delta_w[n] for n in TWIN_WEIGHTS],
            *[new_m[n] for n in TWIN_WEIGHTS], *[new_v[n] for n in TWIN_WEIGHTS])
```

```python
import functools

import jax
import jax.numpy as jnp
from jax import lax
from jax.experimental import pallas as pl
from jax.experimental.pallas import tpu as pltpu

F32 = jnp.float32
BF16 = jnp.bfloat16

D = 2048
HEAD_DIM = 64
N_Q = 32
N_KV = 4
GROUP = 8
WINDOW = 128
GLA_H = 4
DK = 256
DV = 512
RANK = 16
CHUNK = 64
FFN = 5632
EPS = 1e-6
MASK_VALUE = -1e30
GATE_NORM = 16.0
LR, B1, B2, ADAM_EPS, WD, STEP = 0.001, 0.9, 0.999, 1e-08, 0.01, 10

MAIN = 12800
O_AQ, O_AK, O_AV, O_GQ, O_GK, O_GV, O_GR, O_GA, O_GB = 0, 2048, 2304, 2560, 3584, 4608, 6656, 8704, 10752
LR_AT = 6656
LRP = 128

VMEM_LIMIT = 56 * 1024 * 1024
MESH = pl.DeviceIdType.MESH
ANY = pl.BlockSpec(memory_space=pl.ANY)


def _params(ngrid):
    return pltpu.CompilerParams(dimension_semantics=("arbitrary",) * ngrid, vmem_limit_bytes=VMEM_LIMIT)


def _pick(n, cands):
    for c in cands:
        if n % c == 0:
            return c
    return n


def _sigmoid(x):
    return 1.0 / (1.0 + jnp.exp(-x))


def _dot(a, b, ca, cb):
    return lax.dot_general(a, b, (((ca,), (cb,)), ((), ())), preferred_element_type=F32)


def _matmul(a, b, *, name, ta=False, tb=False, add=None, out_dtype=F32, bm=None, bj=None, bc=None):
    C, M = a.shape if ta else a.shape[::-1]
    J = b.shape[0] if tb else b.shape[1]
    assert (b.shape[1] if tb else b.shape[0]) == C
    bm = bm or _pick(M, (1024, 512, 256, 128))
    bj = bj or _pick(J, (512, 256, 128))
    bc = bc or (C if C <= 2048 else _pick(C, (1408, 1280)))
    nm, nj, nc = M // bm, J // bj, C // bc
    has_add = add is not None

    def body(*refs):
        a_ref, b_ref = refs[0], refs[1]
        add_ref = refs[2] if has_add else None
        o_ref = refs[3] if has_add else refs[2]
        p = _dot(a_ref[...].astype(BF16), b_ref[...].astype(BF16), 0 if ta else 1, 1 if tb else 0)

        def finish(acc):
            if has_add:
                acc = acc + add_ref[...]
            o_ref[...] = acc.astype(o_ref.dtype)

        if nc == 1:
            finish(p)
        else:
            acc_ref = refs[-1]
            c = pl.program_id(2)

            @pl.when(c == 0)
            def _():
                acc_ref[...] = p

            @pl.when(c > 0)
            def _():
                acc_ref[...] += p

            @pl.when(c == nc - 1)
            def _():
                finish(acc_ref[...])

    a_spec = pl.BlockSpec((bc, bm), lambda m, j, c: (c, m)) if ta else pl.BlockSpec((bm, bc), lambda m, j, c: (m, c))
    b_spec = pl.BlockSpec((bj, bc), lambda m, j, c: (j, c)) if tb else pl.BlockSpec((bc, bj), lambda m, j, c: (c, j))
    o_spec = pl.BlockSpec((bm, bj), lambda m, j, c: (m, j))
    in_specs = [a_spec, b_spec] + ([o_spec] if has_add else [])
    args = (a, b) + ((add,) if has_add else ())
    return pl.pallas_call(
        body, name=name, grid=(nm, nj, nc), in_specs=in_specs, out_specs=o_spec,
        out_shape=jax.ShapeDtypeStruct((M, J), out_dtype),
        scratch_shapes=[pltpu.VMEM((bm, bj), F32)] if nc > 1 else [],
        compiler_params=_params(3),
    )(*args)


def _rows(T):
    return _pick(T, (256, 128))


def _rms_fwd(x, w, *, name):
    T = x.shape[0]
    rb = _rows(T)

    def body(x_ref, w_ref, u_ref):
        xv = x_ref[...]
        r = lax.rsqrt(jnp.mean(xv * xv, axis=-1, keepdims=True) + EPS)
        u_ref[...] = (xv * r * w_ref[...]).astype(BF16)

    return pl.pallas_call(
        body, name=name, grid=(T // rb,),
        in_specs=[pl.BlockSpec((rb, D), lambda i: (i, 0)), pl.BlockSpec((1, D), lambda i: (0, 0))],
        out_specs=pl.BlockSpec((rb, D), lambda i: (i, 0)),
        out_shape=jax.ShapeDtypeStruct((T, D), BF16), compiler_params=_params(1),
    )(x, w)


def _rms_bwd(dy, h, w, resid, *, name):
    T = h.shape[0]
    rb = _rows(T)

    def body(dy_ref, h_ref, w_ref, res_ref, dh_ref, dw_ref):
        @pl.when(pl.program_id(0) == 0)
        def _():
            dw_ref[...] = jnp.zeros_like(dw_ref)

        hv = h_ref[...]
        r = lax.rsqrt(jnp.mean(hv * hv, axis=-1, keepdims=True) + EPS)
        hn = hv * r
        dyv = dy_ref[...]
        dw_ref[...] += jnp.sum(dyv * hn, axis=0, keepdims=True)
        t = dyv * w_ref[...]
        dh_ref[...] = res_ref[...] + r * (t - hn * jnp.mean(t * hn, axis=-1, keepdims=True))

    blk = pl.BlockSpec((rb, D), lambda i: (i, 0))
    vec = pl.BlockSpec((1, D), lambda i: (0, 0))
    return pl.pallas_call(
        body, name=name, grid=(T // rb,), in_specs=[blk, blk, vec, blk], out_specs=[blk, vec],
        out_shape=[jax.ShapeDtypeStruct((T, D), F32), jax.ShapeDtypeStruct((1, D), F32)], compiler_params=_params(1),
    )(dy, h, w, resid)


def _final_loss(h, w, target):
    T = h.shape[0]
    rb = _rows(T)

    def body(h_ref, w_ref, t_ref, loss_ref, dh_ref, dw_ref):
        @pl.when(pl.program_id(0) == 0)
        def _():
            dw_ref[...] = jnp.zeros_like(dw_ref)
            loss_ref[...] = jnp.zeros_like(loss_ref)

        hv = h_ref[...]
        wv = w_ref[...]
        r = lax.rsqrt(jnp.mean(hv * hv, axis=-1, keepdims=True) + EPS)
        hn = hv * r
        e = hn * wv - t_ref[...]
        row = jnp.sum(e * e, axis=-1, keepdims=True) * (0.5 / D)
        loss_ref[...] += jnp.broadcast_to(jnp.sum(row, axis=0, keepdims=True), loss_ref.shape)
        dy = e * (1.0 / D)
        dw_ref[...] += jnp.sum(dy * hn, axis=0, keepdims=True)
        t = dy * wv
        dh_ref[...] = r * (t - hn * jnp.mean(t * hn, axis=-1, keepdims=True))

    blk = pl.BlockSpec((rb, D), lambda i: (i, 0))
    vec = pl.BlockSpec((1, D), lambda i: (0, 0))
    return pl.pallas_call(
        body, name="final_loss", grid=(T // rb,), in_specs=[blk, vec, blk],
        out_specs=[pl.BlockSpec((8, 128), lambda i: (0, 0)), blk, vec],
        out_shape=[jax.ShapeDtypeStruct((8, 128), F32), jax.ShapeDtypeStruct((T, D), F32),
                   jax.ShapeDtypeStruct((1, D), F32)],
        compiler_params=_params(1),
    )(h, w, target)


def _swiglu_fwd(gate, up):
    T = gate.shape[0]
    rb = _rows(T)

    def body(g_ref, u_ref, o_ref):
        g = g_ref[...]
        o_ref[...] = (g * _sigmoid(g) * u_ref[...]).astype(BF16)

    blk = pl.BlockSpec((rb, 512), lambda i, j: (i, j))
    return pl.pallas_call(
        body, name="swiglu_fwd", grid=(T // rb, FFN // 512), in_specs=[blk, blk], out_specs=blk,
        out_shape=jax.ShapeDtypeStruct((T, FFN), BF16), compiler_params=_params(2),
    )(gate, up)


def _swiglu_bwd(dff, gate, up):
    T = gate.shape[0]
    rb = _rows(T)

    def body(d_ref, g_ref, u_ref, dg_ref, du_ref):
        g = g_ref[...]
        d = d_ref[...]
        sg = _sigmoid(g)
        du_ref[...] = (d * g * sg).astype(BF16)
        dg_ref[...] = (d * u_ref[...] * sg * (1.0 + g * (1.0 - sg))).astype(BF16)

    blk = pl.BlockSpec((rb, 512), lambda i, j: (i, j))
    return pl.pallas_call(
        body, name="swiglu_bwd", grid=(T // rb, FFN // 512), in_specs=[blk, blk, blk], out_specs=[blk, blk],
        out_shape=[jax.ShapeDtypeStruct((T, FFN), BF16)] * 2, compiler_params=_params(2),
    )(dff, gate, up)


def _colsum(x, *, name):
    T, W = x.shape
    rb = _rows(T)

    def body(x_ref, o_ref):
        @pl.when(pl.program_id(0) == 0)
        def _():
            o_ref[...] = jnp.zeros_like(o_ref)

        o_ref[...] += jnp.sum(x_ref[...], axis=0, keepdims=True)

    return pl.pallas_call(
        body, name=name, grid=(T // rb,), in_specs=[pl.BlockSpec((rb, W), lambda i: (i, 0))],
        out_specs=pl.BlockSpec((1, W), lambda i: (0, 0)), out_shape=jax.ShapeDtypeStruct((1, W), F32),
        compiler_params=_params(1),
    )(x)


def _merge_fwd(attn_o, gla_raw, proj, gla_norm_w):
    T = attn_o.shape[0]
    rb = _rows(T)

    def body(a_ref, g_ref, gr_ref, ga_ref, gb_ref, w_ref, o_ref):
        graw = g_ref[...]
        r = lax.rsqrt(jnp.mean(graw * graw, axis=-1, keepdims=True) + EPS)
        gr = gr_ref[...]
        go = graw * r * w_ref[...] * (gr * _sigmoid(gr))
        o_ref[...] = (_sigmoid(ga_ref[...]) * a_ref[...] + _sigmoid(gb_ref[...]) * go).astype(BF16)

    def sec(off):
        return pl.BlockSpec((rb, DV), lambda i, h: (i, off // DV + h))

    blk = sec(0)
    return pl.pallas_call(
        body, name="merge_fwd", grid=(T // rb, GLA_H),
        in_specs=[blk, blk, sec(O_GR), sec(O_GA), sec(O_GB), pl.BlockSpec((1, DV), lambda i, h: (0, 0))],
        out_specs=blk, out_shape=jax.ShapeDtypeStruct((T, D), BF16), compiler_params=_params(2),
    )(attn_o, gla_raw, proj, proj, proj, gla_norm_w)


def _merge_bwd(dm, attn_o, gla_raw, proj, gla_norm_w):
    T = attn_o.shape[0]
    rb = _rows(T)

    def body(dm_ref, a_ref, g_ref, gr_ref, ga_ref, gb_ref, w_ref, da_ref, dg_ref, dgr_ref, dga_ref, dgb_ref, dw_ref):
        @pl.when((pl.program_id(0) == 0) & (pl.program_id(1) == 0))
        def _():
            dw_ref[...] = jnp.zeros_like(dw_ref)

        dmv = dm_ref[...]
        av = a_ref[...]
        graw = g_ref[...]
        gr = gr_ref[...]
        wv = w_ref[...]
        sa = _sigmoid(ga_ref[...])
        sb = _sigmoid(gb_ref[...])
        r = lax.rsqrt(jnp.mean(graw * graw, axis=-1, keepdims=True) + EPS)
        gnh = graw * r
        gn = gnh * wv
        sr = _sigmoid(gr)
        sl = gr * sr
        go = gn * sl
        da_ref[...] = dmv * sa
        dga_ref[...] = (dmv * av * sa * (1.0 - sa)).astype(BF16)
        dgo = dmv * sb
        dgb_ref[...] = (dmv * go * sb * (1.0 - sb)).astype(BF16)
        dgr_ref[...] = (dgo * gn * sr * (1.0 + gr * (1.0 - sr))).astype(BF16)
        dgn = dgo * sl
        dw_ref[...] += jnp.sum(dgn * gnh, axis=0, keepdims=True)
        t = dgn * wv
        dg_ref[...] = r * (t - gnh * jnp.mean(t * gnh, axis=-1, keepdims=True))

    def sec(off):
        return pl.BlockSpec((rb, DV), lambda i, h: (i, off // DV + h))

    blk = sec(0)
    vec = pl.BlockSpec((1, DV), lambda i, h: (0, 0))
    return pl.pallas_call(
        body, name="merge_bwd", grid=(T // rb, GLA_H),
        in_specs=[blk, blk, blk, sec(O_GR), sec(O_GA), sec(O_GB), vec],
        out_specs=[blk, blk, blk, blk, blk, vec],
        out_shape=[jax.ShapeDtypeStruct((T, D), F32), jax.ShapeDtypeStruct((T, D), F32),
                   jax.ShapeDtypeStruct((T, D), BF16), jax.ShapeDtypeStruct((T, D), BF16),
                   jax.ShapeDtypeStruct((T, D), BF16), jax.ShapeDtypeStruct((1, DV), F32)],
        compiler_params=_params(2),
    )(dm, attn_o, gla_raw, proj, proj, proj, gla_norm_w)


def _attn_mask(n):
    qi = lax.broadcasted_iota(jnp.int32, (WINDOW, 2 * WINDOW), 0)
    ki = lax.broadcasted_iota(jnp.int32, (WINDOW, 2 * WINDOW), 1)
    rel = qi + WINDOW - ki
    return (rel >= 0) & (rel < WINDOW) & ((n > 0) | (ki >= WINDOW))


def _attn_probs(q, kk, mask, sink):
    s = _dot(q, kk, 1, 1) * (HEAD_DIM ** -0.5)
    s = jnp.where(mask, s, MASK_VALUE)
    m = jnp.maximum(jnp.max(s, axis=-1, keepdims=True), sink)
    e = jnp.exp(s - m)
    es = jnp.exp(sink - m)
    inv = 1.0 / (jnp.sum(e, axis=-1, keepdims=True) + es)
    return e * inv, es * inv


def _attn_specs(nb, rev):
    def at(n):
        return (nb - 1 - n) if rev else n

    q = pl.BlockSpec((WINDOW, D), lambda n: (at(n), 0))
    kc = pl.BlockSpec((WINDOW, 256), lambda n: (at(n), O_AK // 256))
    kp = pl.BlockSpec((WINDOW, 256), lambda n: (jnp.maximum(at(n) - 1, 0), O_AK // 256))
    vc = pl.BlockSpec((WINDOW, 256), lambda n: (at(n), O_AV // 256))
    vp = pl.BlockSpec((WINDOW, 256), lambda n: (jnp.maximum(at(n) - 1, 0), O_AV // 256))
    sk = pl.BlockSpec((1, 128), lambda n: (0, 0))
    return q, kc, kp, vc, vp, sk


def _attn_fwd(proj, sinks):
    T = proj.shape[0]
    nb = T // WINDOW

    def body(q_ref, kc_ref, kp_ref, vc_ref, vp_ref, s_ref, o_ref):
        mask = _attn_mask(pl.program_id(0))
        for h in range(N_KV):
            hs = slice(h * HEAD_DIM, (h + 1) * HEAD_DIM)
            kk = jnp.concatenate([kp_ref[:, hs], kc_ref[:, hs]], axis=0).astype(BF16)
            vv = jnp.concatenate([vp_ref[:, hs], vc_ref[:, hs]], axis=0).astype(BF16)
            for g in range(GROUP):
                hg = h * GROUP + g
                cs = slice(hg * HEAD_DIM, (hg + 1) * HEAD_DIM)
                p, _ = _attn_probs(q_ref[:, cs].astype(BF16), kk, mask, s_ref[:, hg:hg + 1])
                o_ref[:, cs] = _dot(p.astype(BF16), vv, 1, 0)

    q, kc, kp, vc, vp, sk = _attn_specs(nb, False)
    return pl.pallas_call(
        body, name="attn_fwd", grid=(nb,), in_specs=[q, kc, kp, vc, vp, sk], out_specs=q,
        out_shape=jax.ShapeDtypeStruct((T, D), F32), compiler_params=_params(1),
    )(proj, proj, proj, proj, proj, sinks)


def _attn_bwd(proj, sinks, d_o):
    T = proj.shape[0]
    nb = T // WINDOW

    def body(q_ref, kc_ref, kp_ref, vc_ref, vp_ref, s_ref, do_ref, dq_ref, dk_ref, dv_ref, ds_ref, ck_ref, cv_ref):
        i = pl.program_id(0)
        n = nb - 1 - i

        @pl.when(i == 0)
        def _():
            ck_ref[...] = jnp.zeros_like(ck_ref)
            cv_ref[...] = jnp.zeros_like(cv_ref)
            ds_ref[...] = jnp.zeros_like(ds_ref)

        mask = _attn_mask(n)
        for h in range(N_KV):
            hs = slice(h * HEAD_DIM, (h + 1) * HEAD_DIM)
            kk = jnp.concatenate([kp_ref[:, hs], kc_ref[:, hs]], axis=0).astype(BF16)
            vv = jnp.concatenate([vp_ref[:, hs], vc_ref[:, hs]], axis=0).astype(BF16)
            dkk = jnp.zeros((2 * WINDOW, HEAD_DIM), F32)
            dvv = jnp.zeros((2 * WINDOW, HEAD_DIM), F32)
            for g in range(GROUP):
                hg = h * GROUP + g
                cs = slice(hg * HEAD_DIM, (hg + 1) * HEAD_DIM)
                q = q_ref[:, cs].astype(BF16)
                p, ps = _attn_probs(q, kk, mask, s_ref[:, hg:hg + 1])
                do = do_ref[:, cs].astype(BF16)
                dp = _dot(do, vv, 1, 1)
                delta = jnp.sum(p * dp, axis=-1, keepdims=True)
                dsc = (p * (dp - delta) * (HEAD_DIM ** -0.5)).astype(BF16)
                dsink = jnp.sum(-ps * delta, axis=0, keepdims=True)
                ds_ref[hg:hg + 1, :] += jnp.broadcast_to(dsink, (1, 128))
                dq_ref[:, cs] = _dot(dsc, kk, 1, 0).astype(BF16)
                dkk = dkk + _dot(dsc, q, 0, 0)
                dvv = dvv + _dot(p.astype(BF16), do, 0, 0)
            dk_ref[:, hs] = (dkk[WINDOW:] + ck_ref[:, hs]).astype(BF16)
            dv_ref[:, hs] = (dvv[WINDOW:] + cv_ref[:, hs]).astype(BF16)
            ck_ref[:, hs] = dkk[:WINDOW]
            cv_ref[:, hs] = dvv[:WINDOW]

    q, kc, kp, vc, vp, sk = _attn_specs(nb, True)
    kv_out = pl.BlockSpec((WINDOW, 256), lambda n: (nb - 1 - n, 0))
    return pl.pallas_call(
        body, name="attn_bwd", grid=(nb,), in_specs=[q, kc, kp, vc, vp, sk, q],
        out_specs=[q, kv_out, kv_out, pl.BlockSpec((N_Q, 128), lambda n: (0, 0))],
        out_shape=[jax.ShapeDtypeStruct((T, D), BF16), jax.ShapeDtypeStruct((T, 256), BF16),
                   jax.ShapeDtypeStruct((T, 256), BF16), jax.ShapeDtypeStruct((N_Q, 128), F32)],
        scratch_shapes=[pltpu.VMEM((WINDOW, 256), F32), pltpu.VMEM((WINDOW, 256), F32)],
        compiler_params=_params(1),
    )(proj, proj, proj, proj, proj, sinks, d_o)


def _split3(x):
    hi = x.astype(BF16)
    r1 = x - hi.astype(F32)
    mid = r1.astype(BF16)
    lo = (r1 - mid.astype(F32)).astype(BF16)
    return hi, mid, lo


def _tri_sum(tri, x):
    hi, mid, lo = _split3(x)
    return _dot(tri, hi, 1, 0) + _dot(tri, mid, 1, 0) + _dot(tri, lo, 1, 0)


def _gla_chunk(q, k, lr, w2, b, lower):
    logit = _dot(lr.astype(BF16), w2.astype(BF16), 1, 0) + b
    la = (jnp.minimum(logit, 0.0) - jnp.log(1.0 + jnp.exp(-jnp.abs(logit)))) * (1.0 / GATE_NORM)
    g = _tri_sum(lower, la)
    gl = g[CHUNK - 1:CHUNK, :]
    eg = jnp.exp(g)
    qd = q * (DK ** -0.5) * eg
    ki = k * jnp.exp(-g)
    ke = k * jnp.exp(gl - g)
    return logit, g, gl, eg, qd, ki, ke


def _tri(lower):
    r = lax.broadcasted_iota(jnp.int32, (CHUNK, CHUNK), 0)
    c = lax.broadcasted_iota(jnp.int32, (CHUNK, CHUNK), 1)
    return (r >= c) if lower else (r <= c)


def _gla_rows(T):
    return _pick(T, (256, 128, 64))


def _gla_fwd(proj, proj_lr, w2p, gate_b):
    T = proj.shape[0]
    rb = _gla_rows(T)
    per = rb // CHUNK

    def body(q_ref, k_ref, v_ref, lr_ref, w2_ref, b_ref, o_ref, st_ref, s_scr):
        @pl.when(pl.program_id(1) == 0)
        def _():
            s_scr[...] = jnp.zeros_like(s_scr)

        low = _tri(True)
        lower = low.astype(BF16)
        for i in range(per):
            rows = slice(i * CHUNK, (i + 1) * CHUNK)
            _, g, gl, eg, qd, ki, ke = _gla_chunk(q_ref[rows, :], k_ref[rows, :], lr_ref[rows, :], w2_ref[...],
                                                  b_ref[...], lower)
            v = v_ref[rows, :].astype(BF16)
            qdb = qd.astype(BF16)
            att = jnp.where(low, _dot(qdb, ki.astype(BF16), 1, 1), 0.0)
            st = s_scr[...]
            st_ref[0, i] = st
            o_ref[rows, :] = _dot(att.astype(BF16), v, 1, 0) + _dot(qdb, st.astype(BF16), 1, 1)
            s_scr[...] = st * jnp.exp(gl) + _dot(v, ke.astype(BF16), 0, 0)

    return pl.pallas_call(
        body, name="gla_fwd", grid=(GLA_H, T // rb),
        in_specs=[pl.BlockSpec((rb, DK), lambda h, n: (n, O_GQ // DK + h)),
                  pl.BlockSpec((rb, DK), lambda h, n: (n, O_GK // DK + h)),
                  pl.BlockSpec((rb, DV), lambda h, n: (n, O_GV // DV + h)),
                  pl.BlockSpec((rb, LRP), lambda h, n: (n, 0)),
                  pl.BlockSpec((LRP, DK), lambda h, n: (0, h)),
                  pl.BlockSpec((1, DK), lambda h, n: (0, h))],
        out_specs=[pl.BlockSpec((rb, DV), lambda h, n: (n, h)),
                   pl.BlockSpec((1, per, DV, DK), lambda h, n: (h, n, 0, 0))],
        out_shape=[jax.ShapeDtypeStruct((T, GLA_H * DV), F32),
                   jax.ShapeDtypeStruct((GLA_H, T // CHUNK, DV, DK), F32)],
        scratch_shapes=[pltpu.VMEM((DV, DK), F32)], compiler_params=_params(2),
    )(proj, proj, proj, proj_lr, w2p, gate_b)


def _gla_bwd(proj, proj_lr, w2p, gate_b, states, d_o):
    T = proj.shape[0]
    rb = _gla_rows(T)
    per = rb // CHUNK
    nblk = T // rb

    def body(q_ref, k_ref, v_ref, lr_ref, w2_ref, b_ref, st_ref, do_ref, dq_ref, dk_ref, dv_ref, dl_ref, ds_scr):
        @pl.when(pl.program_id(1) == 0)
        def _():
            ds_scr[...] = jnp.zeros_like(ds_scr)

        low = _tri(True)
        lower = low.astype(BF16)
        upper = _tri(False).astype(BF16)
        for i in reversed(range(per)):
            rows = slice(i * CHUNK, (i + 1) * CHUNK)
            logit, g, gl, eg, qd, ki, ke = _gla_chunk(q_ref[rows, :], k_ref[rows, :], lr_ref[rows, :], w2_ref[...],
                                                      b_ref[...], lower)
            v = v_ref[rows, :].astype(BF16)
            do = do_ref[rows, :].astype(BF16)
            qdb, kib, keb = qd.astype(BF16), ki.astype(BF16), ke.astype(BF16)
            att = jnp.where(low, _dot(qdb, kib, 1, 1), 0.0).astype(BF16)
            sp = st_ref[0, i]
            ds = ds_scr[...]
            dsb = ds.astype(BF16)
            datt = jnp.where(low, _dot(do, v, 1, 1), 0.0).astype(BF16)
            dv_ref[rows, :] = (_dot(att, do, 0, 0) + _dot(keb, dsb, 1, 1)).astype(BF16)
            dqd = _dot(datt, kib, 1, 0) + _dot(do, sp.astype(BF16), 1, 0)
            dki = _dot(datt, qdb, 0, 0)
            dke = _dot(v, dsb, 1, 0)
            decay = jnp.exp(gl)
            ds_scr[...] = ds * decay + _dot(do, qdb, 0, 0)
            ddec = jnp.sum(ds * sp, axis=0, keepdims=True)
            dq_ref[rows, :] = (dqd * (DK ** -0.5) * eg).astype(BF16)
            dk_ref[rows, :] = (dki * jnp.exp(-g) + dke * jnp.exp(gl - g)).astype(BF16)
            dke_ke = dke * ke
            dg = dqd * qd - dki * ki - dke_ke
            dgl = jnp.sum(dke_ke, axis=0, keepdims=True) + ddec * decay
            dla = _tri_sum(upper, dg) + dgl
            dl_ref[rows, :] = dla * (1.0 / GATE_NORM) * (1.0 - _sigmoid(logit))

    def rev(n):
        return nblk - 1 - n

    qk_out = pl.BlockSpec((rb, DK), lambda h, n: (rev(n), h))
    return pl.pallas_call(
        body, name="gla_bwd", grid=(GLA_H, nblk),
        in_specs=[pl.BlockSpec((rb, DK), lambda h, n: (rev(n), O_GQ // DK + h)),
                  pl.BlockSpec((rb, DK), lambda h, n: (rev(n), O_GK // DK + h)),
                  pl.BlockSpec((rb, DV), lambda h, n: (rev(n), O_GV // DV + h)),
                  pl.BlockSpec((rb, LRP), lambda h, n: (rev(n), 0)),
                  pl.BlockSpec((LRP, DK), lambda h, n: (0, h)),
                  pl.BlockSpec((1, DK), lambda h, n: (0, h)),
                  pl.BlockSpec((1, per, DV, DK), lambda h, n: (h, rev(n), 0, 0)),
                  pl.BlockSpec((rb, DV), lambda h, n: (rev(n), h))],
        out_specs=[qk_out, qk_out, pl.BlockSpec((rb, DV), lambda h, n: (rev(n), h)), qk_out],
        out_shape=[jax.ShapeDtypeStruct((T, GLA_H * DK), BF16), jax.ShapeDtypeStruct((T, GLA_H * DK), BF16),
                   jax.ShapeDtypeStruct((T, GLA_H * DV), BF16), jax.ShapeDtypeStruct((T, GLA_H * DK), F32)],
        scratch_shapes=[pltpu.VMEM((DV, DK), F32)], compiler_params=_params(2),
    )(proj, proj, proj, proj_lr, w2p, gate_b, states, d_o)


def _local_step(x, target, p):
    u = _rms_fwd(x, p["norm1_w"], name="rms1_fwd")
    proj = _matmul(u, p["w_in_main"], name="mm_proj")
    proj_lr = _matmul(u, p["w_in_lr"], name="mm_proj_lr")
    attn_o = _attn_fwd(proj, p["sinks"])
    gla_raw, states = _gla_fwd(proj, proj_lr, p["w2p"], p["gate_b"])
    merged = _merge_fwd(attn_o, gla_raw, proj, p["gla_norm_w"])
    h1 = _matmul(merged, p["w_out"], add=x, name="mm_out")
    v2 = _rms_fwd(h1, p["norm2_w"], name="rms2_fwd")
    gate = _matmul(v2, p["w_gate"], name="mm_gate")
    up = _matmul(v2, p["w_up"], name="mm_up")
    ff = _swiglu_fwd(gate, up)
    h2 = _matmul(ff, p["w_down"], add=h1, name="mm_down")
    loss, dh2, g_final = _final_loss(h2, p["final_norm_w"], target)

    dff = _matmul(dh2, p["w_down"], tb=True, name="mm_dff")
    g_down = _matmul(ff, dh2, ta=True, out_dtype=BF16, name="mm_gdown")
    dgate, dup = _swiglu_bwd(dff, gate, up)
    g_gate = _matmul(v2, dgate, ta=True, out_dtype=BF16, name="mm_ggate")
    g_up = _matmul(v2, dup, ta=True, out_dtype=BF16, name="mm_gup")
    dv2 = _matmul(dgate, p["w_gate"], tb=True, name="mm_dv2a")
    dv2 = _matmul(dup, p["w_up"], tb=True, add=dv2, name="mm_dv2b")
    dh1, g_norm2 = _rms_bwd(dv2, h1, p["norm2_w"], dh2, name="rms2_bwd")
    dmerged = _matmul(dh1, p["w_out"], tb=True, name="mm_dmerged")
    g_out = _matmul(merged, dh1, ta=True, out_dtype=BF16, name="mm_gout")
    d_attn, d_gla, d_gr, d_ga, d_gb, g_gla_norm = _merge_bwd(dmerged, attn_o, gla_raw, proj, p["gla_norm_w"])
    d_gq, d_gk, d_gv, dlogit = _gla_bwd(proj, proj_lr, p["w2p"], p["gate_b"], states, d_gla)
    d_aq, d_ak, d_av, g_sinks = _attn_bwd(proj, p["sinks"], d_attn)
    g_gate_b = _colsum(dlogit, name="colsum_gate_b")
    g_w2 = _matmul(proj_lr, dlogit, ta=True, name="mm_gw2")
    dproj_lr = _matmul(dlogit, p["w2p"], tb=True, out_dtype=BF16, name="mm_dlr")
    dproj = jnp.concatenate([d_aq, d_ak, d_av, d_gq, d_gk, d_gv, d_gr, d_ga, d_gb], axis=1)
    g_in_main = _matmul(u, dproj, ta=True, out_dtype=BF16, name="mm_gin")
    g_in_lr = _matmul(u, dproj_lr, ta=True, out_dtype=BF16, name="mm_gin_lr")
    du = _matmul(dproj, p["w_in_main"], tb=True, name="mm_du")
    du = _matmul(dproj_lr, p["w_in_lr"], tb=True, add=du, name="mm_du_lr")
    dx, g_norm1 = _rms_bwd(du, x, p["norm1_w"], dh1, name="rms1_bwd")
    grads = dict(norm1_w=g_norm1, w_in_main=g_in_main, w_in_lr=g_in_lr, w2=g_w2[:RANK], gate_b=g_gate_b,
                 sinks=g_sinks[:, 0].reshape(1, N_Q), gla_norm_w=g_gla_norm, w_out=g_out, norm2_w=g_norm2,
                 w_gate=g_gate, w_up=g_up, w_down=g_down, final_norm_w=g_final)
    return loss, dx, grads


def _place():
    x, y, c = lax.axis_index("x"), lax.axis_index("y"), lax.axis_index("c")
    chips = [(1 - x, y), (x, 1 - y), (1 - x, 1 - y)]
    return x, y, c, chips


def _hbm_shape(s, dt):
    return jax.ShapeDtypeStruct(s, dt)


def _gather_shards(shards):
    n = len(shards)

    def body(*refs):
        ins, outs = refs[:n], refs[n:2 * n]
        ici_send, ici_recv, d2d_send, d2d_recv, local_sem = refs[2 * n:]
        x, y, c, chips = _place()
        me = 2 * x + y
        sibling = (x, y, 1 - c)

        def half(w, slot, hc):
            r2 = shards[w].shape[0] // 2
            return outs[w].at[slot, pl.ds(hc * r2, r2), :]

        locals_ = [pltpu.make_async_copy(ins[w], outs[w].at[me], local_sem.at[w]) for w in range(n)]
        for cp in locals_:
            cp.start()
        sends = []
        for w in range(n):
            r2 = shards[w].shape[0] // 2
            for j, chip in enumerate(chips):
                cp = pltpu.make_async_remote_copy(
                    src_ref=ins[w].at[pl.ds(c * r2, r2), :], dst_ref=half(w, me, c),
                    send_sem=ici_send.at[w * 3 + j], recv_sem=ici_recv.at[w * 3 + j],
                    device_id=(*chip, c), device_id_type=MESH)
                cp.start()
                sends.append(cp)
        for w in range(n):
            for j, chip in enumerate(chips):
                slot = 2 * chip[0] + chip[1]
                got = half(w, slot, c)
                pltpu.make_async_remote_copy(
                    src_ref=got, dst_ref=got, send_sem=ici_send.at[w * 3 + j], recv_sem=ici_recv.at[w * 3 + j],
                    device_id=(*chip, c), device_id_type=MESH).wait_recv()
                cp = pltpu.make_async_remote_copy(
                    src_ref=got, dst_ref=got, send_sem=d2d_send.at[w * 3 + j], recv_sem=d2d_recv.at[w * 3 + j],
                    device_id=sibling, device_id_type=MESH)
                cp.start()
                sends.append(cp)
        for w in range(n):
            for j, chip in enumerate(chips):
                slot = 2 * chip[0] + chip[1]
                got = half(w, slot, 1 - c)
                pltpu.make_async_remote_copy(
                    src_ref=got, dst_ref=got, send_sem=d2d_send.at[w * 3 + j], recv_sem=d2d_recv.at[w * 3 + j],
                    device_id=sibling, device_id_type=MESH).wait_recv()
        for cp in sends:
            cp.wait_send()
        for cp in locals_:
            cp.wait()

    return pl.pallas_call(
        body, name="gather_weights", in_specs=[ANY] * n, out_specs=[ANY] * n,
        out_shape=[_hbm_shape((4,) + s.shape, s.dtype) for s in shards],
        scratch_shapes=[pltpu.SemaphoreType.DMA((3 * n,)), pltpu.SemaphoreType.DMA((3 * n,)),
                        pltpu.SemaphoreType.DMA((3 * n,)), pltpu.SemaphoreType.DMA((3 * n,)),
                        pltpu.SemaphoreType.DMA((n,))],
    )(*shards)


def _pair_split(parts):
    n = len(parts)

    def body(*refs):
        ins, owns, gots = refs[:n], refs[n:2 * n], refs[2 * n:3 * n]
        send_sem, recv_sem, local_sem = refs[3 * n:]
        x, y, c, _ = _place()
        sibling = (x, y, 1 - c)
        cps = []
        for w in range(n):
            r2 = parts[w].shape[1] // 2
            lc = pltpu.make_async_copy(ins[w].at[:, pl.ds(c * r2, r2), :], owns[w], local_sem.at[w])
            lc.start()
            rc = pltpu.make_async_remote_copy(
                src_ref=ins[w].at[:, pl.ds((1 - c) * r2, r2), :], dst_ref=gots[w],
                send_sem=send_sem.at[w], recv_sem=recv_sem.at[w], device_id=sibling, device_id_type=MESH)
            rc.start()
            cps.append((lc, rc))
        for lc, rc in cps:
            rc.wait()
            lc.wait()

    halves = [_hbm_shape((4, s.shape[1] // 2, s.shape[2]), s.dtype) for s in parts]
    outs = pl.pallas_call(
        body, name="pair_split", in_specs=[ANY] * n, out_specs=[ANY] * (2 * n), out_shape=halves + halves,
        scratch_shapes=[pltpu.SemaphoreType.DMA((n,)), pltpu.SemaphoreType.DMA((n,)), pltpu.SemaphoreType.DMA((n,))],
    )(*parts)
    return outs[:n], outs[n:]


def _chip_exchange(sums):
    n = len(sums)

    def body(*refs):
        ins, outs = refs[:n], refs[n:2 * n]
        send_sem, recv_sem, local_sem = refs[2 * n:]
        x, y, c, chips = _place()
        me = 2 * x + y
        cps = []
        for w in range(n):
            lc = pltpu.make_async_copy(ins[w].at[me], outs[w].at[me], local_sem.at[w])
            lc.start()
            cps.append(lc)
            for j, chip in enumerate(chips):
                slot = 2 * chip[0] + chip[1]
                rc = pltpu.make_async_remote_copy(
                    src_ref=ins[w].at[slot], dst_ref=outs[w].at[me],
                    send_sem=send_sem.at[w * 3 + j], recv_sem=recv_sem.at[w * 3 + j],
                    device_id=(*chip, c), device_id_type=MESH)
                rc.start()
                cps.append(rc)
        for w in range(n):
            for j, chip in enumerate(chips):
                slot = 2 * chip[0] + chip[1]
                pltpu.make_async_remote_copy(
                    src_ref=ins[w].at[slot], dst_ref=outs[w].at[slot],
                    send_sem=send_sem.at[w * 3 + j], recv_sem=recv_sem.at[w * 3 + j],
                    device_id=(*chip, c), device_id_type=MESH).wait_recv()
        for w in range(n):
            cps[w * 4].wait()
            for j in range(3):
                cps[w * 4 + 1 + j].wait_send()

    return pl.pallas_call(
        body, name="chip_exchange", in_specs=[ANY] * n, out_specs=[ANY] * n,
        out_shape=[_hbm_shape(s.shape, s.dtype) for s in sums],
        scratch_shapes=[pltpu.SemaphoreType.DMA((3 * n,)), pltpu.SemaphoreType.DMA((3 * n,)),
                        pltpu.SemaphoreType.DMA((n,))],
    )(*sums)


def _pair_join(halves):
    n = len(halves)

    def body(*refs):
        ins, outs = refs[:n], refs[n:2 * n]
        send_sem, recv_sem, local_sem = refs[2 * n:]
        x, y, c, _ = _place()
        sibling = (x, y, 1 - c)
        cps = []
        for w in range(n):
            r2 = halves[w].shape[0]
            mine = outs[w].at[pl.ds(c * r2, r2), :]
            lc = pltpu.make_async_copy(ins[w], mine, local_sem.at[w])
            lc.start()
            rc = pltpu.make_async_remote_copy(
                src_ref=ins[w], dst_ref=mine, send_sem=send_sem.at[w], recv_sem=recv_sem.at[w],
                device_id=sibling, device_id_type=MESH)
            rc.start()
            cps.append((lc, rc))
        for w, (lc, rc) in enumerate(cps):
            r2 = halves[w].shape[0]
            theirs = outs[w].at[pl.ds((1 - c) * r2, r2), :]
            pltpu.make_async_remote_copy(
                src_ref=ins[w], dst_ref=theirs, send_sem=send_sem.at[w], recv_sem=recv_sem.at[w],
                device_id=sibling, device_id_type=MESH).wait_recv()
            rc.wait_send()
            lc.wait()

    return pl.pallas_call(
        body, name="pair_join", in_specs=[ANY] * n, out_specs=[ANY] * n,
        out_shape=[_hbm_shape((2 * s.shape[0], s.shape[1]), s.dtype) for s in halves],
        scratch_shapes=[pltpu.SemaphoreType.DMA((n,)), pltpu.SemaphoreType.DMA((n,)), pltpu.SemaphoreType.DMA((n,))],
    )(*halves)


def _sum_small(pack):
    R, C = pack.shape

    def body(in_ref, out_ref, all_ref, send_sem, recv_sem):
        x, y, c, _ = _place()
        me = 4 * x + 2 * y + c
        all_ref[me] = in_ref[...]
        cps = []
        for k in range(1, 8):
            peer = (x ^ (k >> 2), y ^ ((k >> 1) & 1), c ^ (k & 1))
            cp = pltpu.make_async_remote_copy(
                src_ref=in_ref, dst_ref=all_ref.at[me], send_sem=send_sem.at[k - 1], recv_sem=recv_sem.at[k - 1],
                device_id=peer, device_id_type=MESH)
            cp.start()
            cps.append(cp)
        for k in range(1, 8):
            peer = (x ^ (k >> 2), y ^ ((k >> 1) & 1), c ^ (k & 1))
            slot = 4 * peer[0] + 2 * peer[1] + peer[2]
            pltpu.make_async_remote_copy(
                src_ref=in_ref, dst_ref=all_ref.at[slot], send_sem=send_sem.at[k - 1], recv_sem=recv_sem.at[k - 1],
                device_id=peer, device_id_type=MESH).wait_recv()
        for cp in cps:
            cp.wait_send()
        acc = all_ref[0]
        for d in range(1, 8):
            acc = acc + all_ref[d]
        out_ref[...] = acc

    return pl.pallas_call(
        body, name="sum_small", in_specs=[pl.BlockSpec(memory_space=pltpu.VMEM)],
        out_specs=pl.BlockSpec(memory_space=pltpu.VMEM), out_shape=jax.ShapeDtypeStruct((R, C), F32),
        scratch_shapes=[pltpu.VMEM((8, R, C), F32), pltpu.SemaphoreType.DMA((7,)), pltpu.SemaphoreType.DMA((7,))],
    )(pack)


def _add_pairs(own, got):
    _, R2, C = own.shape
    rb = _pick(R2, (256, 128, 64, 32, 16))

    def body(a_ref, b_ref, o_ref):
        o_ref[...] = (a_ref[...].astype(F32) + b_ref[...].astype(F32)).astype(BF16)

    blk = pl.BlockSpec((1, rb, C), lambda s, i: (s, i, 0))
    return pl.pallas_call(
        body, name="add_pairs", grid=(4, R2 // rb), in_specs=[blk, blk], out_specs=blk,
        out_shape=jax.ShapeDtypeStruct(own.shape, BF16), compiler_params=_params(2),
    )(own, got)


def _sum_chips(parts):
    _, R2, C = parts.shape
    rb = _pick(R2, (256, 128, 64, 32, 16))

    def body(p_ref, o_ref):
        acc = p_ref[0].astype(F32)
        for s in range(1, 4):
            acc = acc + p_ref[s].astype(F32)
        o_ref[...] = acc

    return pl.pallas_call(
        body, name="sum_chips", grid=(R2 // rb,), in_specs=[pl.BlockSpec((4, rb, C), lambda i: (0, i, 0))],
        out_specs=pl.BlockSpec((rb, C), lambda i: (i, 0)), out_shape=jax.ShapeDtypeStruct((R2, C), F32),
        compiler_params=_params(1),
    )(parts)


def _adamw(w, g, m, v, *, name):
    R, C = w.shape
    rb = _pick(R, (128, 64, 32, 16, 8))
    c1 = 1.0 / (1.0 - B1 ** STEP)
    c2 = 1.0 / (1.0 - B2 ** STEP)

    def body(w_ref, g_ref, m_ref, v_ref, d_ref, nm_ref, nv_ref):
        gv = g_ref[...]
        m2 = B1 * m_ref[...] + (1.0 - B1) * gv
        v2 = B2 * v_ref[...] + (1.0 - B2) * gv * gv
        nm_ref[...] = m2
        nv_ref[...] = v2
        d_ref[...] = -LR * ((m2 * c1) / (jnp.sqrt(v2 * c2) + ADAM_EPS) + WD * w_ref[...])

    blk = pl.BlockSpec((rb, C), lambda i: (i, 0))
    return pl.pallas_call(
        body, name=name, grid=(R // rb,), in_specs=[blk] * 4, out_specs=[blk] * 3,
        out_shape=[jax.ShapeDtypeStruct((R, C), F32)] * 3, compiler_params=_params(1),
    )(w, g, m, v)


SMALL = (("norm1_w", D), ("norm2_w", D), ("final_norm_w", D), ("gate_b", GLA_H * DK), ("gla_norm_w", DV), ("sinks", N_Q))
PACK_W = 1024


def _pack_small(vals, w2, loss):
    rows = []
    for name, width in SMALL:
        v = vals[name].reshape(-1)
        rows.append(jnp.pad(v, (0, (-width) % PACK_W)).reshape(-1, PACK_W))
    rows.append(w2)
    rows.append(jnp.broadcast_to(loss.reshape(1, 1), (1, PACK_W)))
    pack = jnp.concatenate(rows, axis=0)
    return jnp.pad(pack, ((0, 32 - pack.shape[0]), (0, 0)))


def _unpack_small(pack):
    out, r = {}, 0
    for name, width in SMALL:
        nr = -(-width // PACK_W)
        out[name] = pack[r:r + nr].reshape(-1)[:width]
        r += nr
    out["w2"] = pack[r:r + RANK]
    out["loss"] = pack[r + RANK, 0]
    return out


def kernel(x, norm1_w, w_in, gla_gate_w2, gla_gate_b, attn_sinks, gla_norm_w, w_out, norm2_w, w_ffn_gate, w_ffn_up, w_ffn_down, final_norm_w, loss_target, m_norm1_w, m_w_in, m_gla_gate_w2, m_gla_gate_b, m_attn_sinks, m_gla_norm_w, m_w_out, m_norm2_w, m_w_ffn_gate, m_w_ffn_up, m_w_ffn_down, m_final_norm_w, v_norm1_w, v_w_in, v_gla_gate_w2, v_gla_gate_b, v_attn_sinks, v_gla_norm_w, v_w_out, v_norm2_w, v_w_ffn_gate, v_w_ffn_up, v_w_ffn_down, v_final_norm_w):
    chip = 2 * lax.axis_index("x") + lax.axis_index("y")
    w_in_s, w_out_s, w_gate_s, w_up_s, w_down_s, w2_s = (
        w_in[0], w_out[0], w_ffn_gate[0], w_ffn_up[0], w_ffn_down[0], gla_gate_w2[0])
    CS = w_in_s.shape[1]

    g_in, g_out, g_gate, g_up, g_down, g_w2 = _gather_shards(
        [w_in_s.astype(BF16), w_out_s.astype(BF16), w_gate_s.astype(BF16), w_up_s.astype(BF16),
         w_down_s.astype(BF16), w2_s])
    full_in = jnp.transpose(g_in, (1, 0, 2)).reshape(D, 4 * CS)
    w2_full = jnp.transpose(g_w2, (1, 0, 2)).reshape(RANK, GLA_H * DK)
    p = dict(
        norm1_w=norm1_w, norm2_w=norm2_w, final_norm_w=final_norm_w.reshape(1, D), gate_b=gla_gate_b,
        gla_norm_w=gla_norm_w, sinks=jnp.pad(attn_sinks, ((0, 0), (0, 128 - N_Q))),
        w_in_main=jnp.concatenate([full_in[:, :LR_AT], full_in[:, LR_AT + RANK:]], axis=1),
        w_in_lr=jnp.pad(full_in[:, LR_AT:LR_AT + RANK], ((0, 0), (0, LRP - RANK))),
        w2p=jnp.pad(w2_full, ((0, LRP - RANK), (0, 0))).astype(BF16),
        w_out=g_out.reshape(D, D),
        w_gate=jnp.transpose(g_gate, (1, 0, 2)).reshape(D, FFN),
        w_up=jnp.transpose(g_up, (1, 0, 2)).reshape(D, FFN),
        w_down=g_down.reshape(FFN, D),
    )

    loss_blk, dx, g = _local_step(x[0], loss_target[0], p)

    gin_full = jnp.concatenate([g["w_in_main"][:, :LR_AT], g["w_in_lr"][:, :RANK], g["w_in_main"][:, LR_AT:]], axis=1)
    parts = [
        jnp.transpose(gin_full.reshape(D, 4, CS), (1, 0, 2)),
        g["w_out"].reshape(4, D // 4, D),
        jnp.transpose(g["w_gate"].reshape(D, 4, FFN // 4), (1, 0, 2)),
        jnp.transpose(g["w_up"].reshape(D, 4, FFN // 4), (1, 0, 2)),
        g["w_down"].reshape(4, FFN // 4, D),
    ]
    own, got = _pair_split(parts)
    sums = [_add_pairs(a, b) for a, b in zip(own, got)]
    recv = _chip_exchange(sums)
    halves = [_sum_chips(r) for r in recv]
    big = _pair_join(halves)

    small = _unpack_small(_sum_small(_pack_small(g, g["w2"], loss_blk[0, 0])))
    loss = small["loss"]
    g_w2_mine = lax.dynamic_slice_in_dim(small["w2"], chip * (GLA_H * DK // 4), GLA_H * DK // 4, axis=1)

    grads = dict(
        norm1_w=small["norm1_w"].reshape(1, D), w_in=big[0], gla_gate_w2=g_w2_mine,
        gla_gate_b=small["gate_b"].reshape(1, -1), attn_sinks=small["sinks"].reshape(1, N_Q),
        gla_norm_w=small["gla_norm_w"].reshape(1, DV), w_out=big[1], norm2_w=small["norm2_w"].reshape(1, D),
        w_ffn_gate=big[2], w_ffn_up=big[3], w_ffn_down=big[4], final_norm_w=small["final_norm_w"].reshape(1, D))
    weights = dict(
        norm1_w=(norm1_w, m_norm1_w, v_norm1_w), w_in=(w_in_s, m_w_in[0], v_w_in[0]),
        gla_gate_w2=(w2_s, m_gla_gate_w2[0], v_gla_gate_w2[0]), gla_gate_b=(gla_gate_b, m_gla_gate_b, v_gla_gate_b),
        attn_sinks=(attn_sinks, m_attn_sinks, v_attn_sinks), gla_norm_w=(gla_norm_w, m_gla_norm_w, v_gla_norm_w),
        w_out=(w_out_s, m_w_out[0], v_w_out[0]), norm2_w=(norm2_w, m_norm2_w, v_norm2_w),
        w_ffn_gate=(w_gate_s, m_w_ffn_gate[0], v_w_ffn_gate[0]), w_ffn_up=(w_up_s, m_w_ffn_up[0], v_w_ffn_up[0]),
        w_ffn_down=(w_down_s, m_w_ffn_down[0], v_w_ffn_down[0]),
        final_norm_w=(final_norm_w.reshape(1, D), m_final_norm_w.reshape(1, D), v_final_norm_w.reshape(1, D)))
    names = ["norm1_w", "w_in", "gla_gate_w2", "gla_gate_b", "attn_sinks", "gla_norm_w", "w_out", "norm2_w",
             "w_ffn_gate", "w_ffn_up", "w_ffn_down", "final_norm_w"]
    lead = {"norm1_w": False, "gla_gate_b": False, "attn_sinks": False, "gla_norm_w": False, "norm2_w": False}
    g_out_l, d_out, m_out, v_out = [], [], [], []
    for nm in names:
        w, m, v = weights[nm]
        gr = grads[nm]
        dl, nmn, nvn = _adamw(w, gr, m, v, name="adamw_" + nm)
        if nm == "final_norm_w":
            shape = (D,)
        elif nm in lead:
            shape = w.shape
        else:
            shape = (1,) + w.shape
        g_out_l.append(gr.reshape(shape))
        d_out.append(dl.reshape(shape))
        m_out.append(nmn.reshape(shape))
        v_out.append(nvn.reshape(shape))
    return (loss, dx[None], *g_out_l, *d_out, *m_out, *v_out)
```

```python
import functools

import jax
import jax.numpy as jnp
from jax import lax
from jax.experimental import pallas as pl
from jax.experimental.pallas import tpu as pltpu

F32 = jnp.float32
BF16 = jnp.bfloat16

D = 2048
HEAD_DIM = 64
N_Q = 32
N_KV = 4
GROUP = 8
WINDOW = 128
GLA_H = 4
DK = 256
DV = 512
RANK = 16
CHUNK = 64
FFN = 5632
EPS = 1e-6
MASK_VALUE = -1e30
GATE_NORM = 16.0
LR, B1, B2, ADAM_EPS, WD, STEP = 0.001, 0.9, 0.999, 1e-08, 0.01, 10

MAIN = 12800
O_AQ, O_AK, O_AV, O_GQ, O_GK, O_GV, O_GR, O_GA, O_GB = 0, 2048, 2304, 2560, 3584, 4608, 6656, 8704, 10752
LR_AT = 6656
LRP = 128

VMEM_LIMIT = 56 * 1024 * 1024
MESH = pl.DeviceIdType.MESH
ANY = pl.BlockSpec(memory_space=pl.ANY)


def _params(ngrid):
    return pltpu.CompilerParams(dimension_semantics=("arbitrary",) * ngrid, vmem_limit_bytes=VMEM_LIMIT)


def _pick(n, cands):
    for c in cands:
        if n % c == 0:
            return c
    return n


def _sigmoid(x):
    return 1.0 / (1.0 + jnp.exp(-x))


def _dot(a, b, ca, cb):
    return lax.dot_general(a, b, (((ca,), (cb,)), ((), ())), preferred_element_type=F32)


def _matmul(a, b, *, name, ta=False, tb=False, add=None, out_dtype=F32, bm=None, bj=None, bc=None):
    C, M = a.shape if ta else a.shape[::-1]
    J = b.shape[0] if tb else b.shape[1]
    assert (b.shape[1] if tb else b.shape[0]) == C
    bm = bm or _pick(M, (1024, 512, 256, 128))
    bj = bj or _pick(J, (512, 256, 128))
    bc = bc or (C if C <= 2048 else _pick(C, (1408, 1280)))
    nm, nj, nc = M // bm, J // bj, C // bc
    has_add = add is not None

    def body(*refs):
        a_ref, b_ref = refs[0], refs[1]
        add_ref = refs[2] if has_add else None
        o_ref = refs[3] if has_add else refs[2]
        p = _dot(a_ref[...].astype(BF16), b_ref[...].astype(BF16), 0 if ta else 1, 1 if tb else 0)

        def finish(acc):
            if has_add:
                acc = acc + add_ref[...]
            o_ref[...] = acc.astype(o_ref.dtype)

        if nc == 1:
            finish(p)
        else:
            acc_ref = refs[-1]
            c = pl.program_id(2)

            @pl.when(c == 0)
            def _():
                acc_ref[...] = p

            @pl.when(c > 0)
            def _():
                acc_ref[...] += p

            @pl.when(c == nc - 1)
            def _():
                finish(acc_ref[...])

    a_spec = pl.BlockSpec((bc, bm), lambda m, j, c: (c, m)) if ta else pl.BlockSpec((bm, bc), lambda m, j, c: (m, c))
    b_spec = pl.BlockSpec((bj, bc), lambda m, j, c: (j, c)) if tb else pl.BlockSpec((bc, bj), lambda m, j, c: (c, j))
    o_spec = pl.BlockSpec((bm, bj), lambda m, j, c: (m, j))
    in_specs = [a_spec, b_spec] + ([o_spec] if has_add else [])
    args = (a, b) + ((add,) if has_add else ())
    return pl.pallas_call(
        body, name=name, grid=(nm, nj, nc), in_specs=in_specs, out_specs=o_spec,
        out_shape=jax.ShapeDtypeStruct((M, J), out_dtype),
        scratch_shapes=[pltpu.VMEM((bm, bj), F32)] if nc > 1 else [],
        compiler_params=_params(3),
    )(*args)


def _rows(T):
    return _pick(T, (256, 128))


def _rms_fwd(x, w, *, name):
    T = x.shape[0]
    rb = _rows(T)

    def body(x_ref, w_ref, u_ref):
        xv = x_ref[...]
        r = lax.rsqrt(jnp.mean(xv * xv, axis=-1, keepdims=True) + EPS)
        u_ref[...] = (xv * r * w_ref[...]).astype(BF16)

    return pl.pallas_call(
        body, name=name, grid=(T // rb,),
        in_specs=[pl.BlockSpec((rb, D), lambda i: (i, 0)), pl.BlockSpec((1, D), lambda i: (0, 0))],
        out_specs=pl.BlockSpec((rb, D), lambda i: (i, 0)),
        out_shape=jax.ShapeDtypeStruct((T, D), BF16), compiler_params=_params(1),
    )(x, w)


def _rms_bwd(dy, h, w, resid, *, name):
    T = h.shape[0]
    rb = _rows(T)

    def body(dy_ref, h_ref, w_ref, res_ref, dh_ref, dw_ref):
        @pl.when(pl.program_id(0) == 0)
        def _():
            dw_ref[...] = jnp.zeros_like(dw_ref)

        hv = h_ref[...]
        r = lax.rsqrt(jnp.mean(hv * hv, axis=-1, keepdims=True) + EPS)
        hn = hv * r
        dyv = dy_ref[...]
        dw_ref[...] += jnp.sum(dyv * hn, axis=0, keepdims=True)
        t = dyv * w_ref[...]
        dh_ref[...] = res_ref[...] + r * (t - hn * jnp.mean(t * hn, axis=-1, keepdims=True))

    blk = pl.BlockSpec((rb, D), lambda i: (i, 0))
    vec = pl.BlockSpec((1, D), lambda i: (0, 0))
    return pl.pallas_call(
        body, name=name, grid=(T // rb,), in_specs=[blk, blk, vec, blk], out_specs=[blk, vec],
        out_shape=[jax.ShapeDtypeStruct((T, D), F32), jax.ShapeDtypeStruct((1, D), F32)], compiler_params=_params(1),
    )(dy, h, w, resid)


def _final_loss(h, w, target):
    T = h.shape[0]
    rb = _rows(T)

    def body(h_ref, w_ref, t_ref, loss_ref, dh_ref, dw_ref):
        @pl.when(pl.program_id(0) == 0)
        def _():
            dw_ref[...] = jnp.zeros_like(dw_ref)
            loss_ref[...] = jnp.zeros_like(loss_ref)

        hv = h_ref[...]
        wv = w_ref[...]
        r = lax.rsqrt(jnp.mean(hv * hv, axis=-1, keepdims=True) + EPS)
        hn = hv * r
        e = hn * wv - t_ref[...]
        row = jnp.sum(e * e, axis=-1, keepdims=True) * (0.5 / D)
        loss_ref[...] += jnp.broadcast_to(jnp.sum(row, axis=0, keepdims=True), loss_ref.shape)
        dy = e * (1.0 / D)
        dw_ref[...] += jnp.sum(dy * hn, axis=0, keepdims=True)
        t = dy * wv
        dh_ref[...] = r * (t - hn * jnp.mean(t * hn, axis=-1, keepdims=True))

    blk = pl.BlockSpec((rb, D), lambda i: (i, 0))
    vec = pl.BlockSpec((1, D), lambda i: (0, 0))
    return pl.pallas_call(
        body, name="final_loss", grid=(T // rb,), in_specs=[blk, vec, blk],
        out_specs=[pl.BlockSpec((8, 128), lambda i: (0, 0)), blk, vec],
        out_shape=[jax.ShapeDtypeStruct((8, 128), F32), jax.ShapeDtypeStruct((T, D), F32),
                   jax.ShapeDtypeStruct((1, D), F32)],
        compiler_params=_params(1),
    )(h, w, target)


def _swiglu_fwd(gate, up):
    T = gate.shape[0]
    rb = _rows(T)

    def body(g_ref, u_ref, o_ref):
        g = g_ref[...]
        o_ref[...] = (g * _sigmoid(g) * u_ref[...]).astype(BF16)

    blk = pl.BlockSpec((rb, 512), lambda i, j: (i, j))
    return pl.pallas_call(
        body, name="swiglu_fwd", grid=(T // rb, FFN // 512), in_specs=[blk, blk], out_specs=blk,
        out_shape=jax.ShapeDtypeStruct((T, FFN), BF16), compiler_params=_params(2),
    )(gate, up)


def _swiglu_bwd(dff, gate, up):
    T = gate.shape[0]
    rb = _rows(T)

    def body(d_ref, g_ref, u_ref, dg_ref, du_ref):
        g = g_ref[...]
        d = d_ref[...]
        sg = _sigmoid(g)
        du_ref[...] = (d * g * sg).astype(BF16)
        dg_ref[...] = (d * u_ref[...] * sg * (1.0 + g * (1.0 - sg))).astype(BF16)

    blk = pl.BlockSpec((rb, 512), lambda i, j: (i, j))
    return pl.pallas_call(
        body, name="swiglu_bwd", grid=(T // rb, FFN // 512), in_specs=[blk, blk, blk], out_specs=[blk, blk],
        out_shape=[jax.ShapeDtypeStruct((T, FFN), BF16)] * 2, compiler_params=_params(2),
    )(dff, gate, up)


def _colsum(x, *, name):
    T, W = x.shape
    rb = _rows(T)

    def body(x_ref, o_ref):
        @pl.when(pl.program_id(0) == 0)
        def _():
            o_ref[...] = jnp.zeros_like(o_ref)

        o_ref[...] += jnp.sum(x_ref[...], axis=0, keepdims=True)

    return pl.pallas_call(
        body, name=name, grid=(T // rb,), in_specs=[pl.BlockSpec((rb, W), lambda i: (i, 0))],
        out_specs=pl.BlockSpec((1, W), lambda i: (0, 0)), out_shape=jax.ShapeDtypeStruct((1, W), F32),
        compiler_params=_params(1),
    )(x)


def _merge_fwd(attn_o, gla_raw, proj, gla_norm_w):
    T = attn_o.shape[0]
    rb = _rows(T)

    def body(a_ref, g_ref, gr_ref, ga_ref, gb_ref, w_ref, o_ref):
        graw = g_ref[...]
        r = lax.rsqrt(jnp.mean(graw * graw, axis=-1, keepdims=True) + EPS)
        gr = gr_ref[...]
        go = graw * r * w_ref[...] * (gr * _sigmoid(gr))
        o_ref[...] = (_sigmoid(ga_ref[...]) * a_ref[...] + _sigmoid(gb_ref[...]) * go).astype(BF16)

    def sec(off):
        return pl.BlockSpec((rb, DV), lambda i, h: (i, off // DV + h))

    blk = sec(0)
    return pl.pallas_call(
        body, name="merge_fwd", grid=(T // rb, GLA_H),
        in_specs=[blk, blk, sec(O_GR), sec(O_GA), sec(O_GB), pl.BlockSpec((1, DV), lambda i, h: (0, 0))],
        out_specs=blk, out_shape=jax.ShapeDtypeStruct((T, D), BF16), compiler_params=_params(2),
    )(attn_o, gla_raw, proj, proj, proj, gla_norm_w)


def _merge_bwd(dm, attn_o, gla_raw, proj, gla_norm_w):
    T = attn_o.shape[0]
    rb = _rows(T)

    def body(dm_ref, a_ref, g_ref, gr_ref, ga_ref, gb_ref, w_ref, da_ref, dg_ref, dgr_ref, dga_ref, dgb_ref, dw_ref):
        @pl.when((pl.program_id(0) == 0) & (pl.program_id(1) == 0))
        def _():
            dw_ref[...] = jnp.zeros_like(dw_ref)

        dmv = dm_ref[...]
        av = a_ref[...]
        graw = g_ref[...]
        gr = gr_ref[...]
        wv = w_ref[...]
        sa = _sigmoid(ga_ref[...])
        sb = _sigmoid(gb_ref[...])
        r = lax.rsqrt(jnp.mean(graw * graw, axis=-1, keepdims=True) + EPS)
        gnh = graw * r
        gn = gnh * wv
        sr = _sigmoid(gr)
        sl = gr * sr
        go = gn * sl
        da_ref[...] = dmv * sa
        dga_ref[...] = (dmv * av * sa * (1.0 - sa)).astype(BF16)
        dgo = dmv * sb
        dgb_ref[...] = (dmv * go * sb * (1.0 - sb)).astype(BF16)
        dgr_ref[...] = (dgo * gn * sr * (1.0 + gr * (1.0 - sr))).astype(BF16)
        dgn = dgo * sl
        dw_ref[...] += jnp.sum(dgn * gnh, axis=0, keepdims=True)
        t = dgn * wv
        dg_ref[...] = r * (t - gnh * jnp.mean(t * gnh, axis=-1, keepdims=True))

    def sec(off):
        return pl.BlockSpec((rb, DV), lambda i, h: (i, off // DV + h))

    blk = sec(0)
    vec = pl.BlockSpec((1, DV), lambda i, h: (0, 0))
    return pl.pallas_call(
        body, name="merge_bwd", grid=(T // rb, GLA_H),
        in_specs=[blk, blk, blk, sec(O_GR), sec(O_GA), sec(O_GB), vec],
        out_specs=[blk, blk, blk, blk, blk, vec],
        out_shape=[jax.ShapeDtypeStruct((T, D), F32), jax.ShapeDtypeStruct((T, D), F32),
                   jax.ShapeDtypeStruct((T, D), BF16), jax.ShapeDtypeStruct((T, D), BF16),
                   jax.ShapeDtypeStruct((T, D), BF16), jax.ShapeDtypeStruct((1, DV), F32)],
        compiler_params=_params(2),
    )(dm, attn_o, gla_raw, proj, proj, proj, gla_norm_w)


def _attn_mask(n):
    qi = lax.broadcasted_iota(jnp.int32, (WINDOW, 2 * WINDOW), 0)
    ki = lax.broadcasted_iota(jnp.int32, (WINDOW, 2 * WINDOW), 1)
    rel = qi + WINDOW - ki
    return (rel >= 0) & (rel < WINDOW) & ((n > 0) | (ki >= WINDOW))


def _attn_probs(q, kk, mask, sink):
    s = _dot(q, kk, 1, 1) * (HEAD_DIM ** -0.5)
    s = jnp.where(mask, s, MASK_VALUE)
    m = jnp.maximum(jnp.max(s, axis=-1, keepdims=True), sink)
    e = jnp.exp(s - m)
    es = jnp.exp(sink - m)
    inv = 1.0 / (jnp.sum(e, axis=-1, keepdims=True) + es)
    return e * inv, es * inv


def _attn_specs(nb, rev):
    def at(n):
        return (nb - 1 - n) if rev else n

    q = pl.BlockSpec((WINDOW, D), lambda n: (at(n), 0))
    kc = pl.BlockSpec((WINDOW, 256), lambda n: (at(n), O_AK // 256))
    kp = pl.BlockSpec((WINDOW, 256), lambda n: (jnp.maximum(at(n) - 1, 0), O_AK // 256))
    vc = pl.BlockSpec((WINDOW, 256), lambda n: (at(n), O_AV // 256))
    vp = pl.BlockSpec((WINDOW, 256), lambda n: (jnp.maximum(at(n) - 1, 0), O_AV // 256))
    sk = pl.BlockSpec((1, 128), lambda n: (0, 0))
    return q, kc, kp, vc, vp, sk


def _attn_fwd(proj, sinks):
    T = proj.shape[0]
    nb = T // WINDOW

    def body(q_ref, kc_ref, kp_ref, vc_ref, vp_ref, s_ref, o_ref):
        mask = _attn_mask(pl.program_id(0))
        for h in range(N_KV):
            hs = slice(h * HEAD_DIM, (h + 1) * HEAD_DIM)
            kk = jnp.concatenate([kp_ref[:, hs], kc_ref[:, hs]], axis=0).astype(BF16)
            vv = jnp.concatenate([vp_ref[:, hs], vc_ref[:, hs]], axis=0).astype(BF16)
            for g in range(GROUP):
                hg = h * GROUP + g
                cs = slice(hg * HEAD_DIM, (hg + 1) * HEAD_DIM)
                p, _ = _attn_probs(q_ref[:, cs].astype(BF16), kk, mask, s_ref[:, hg:hg + 1])
                o_ref[:, cs] = _dot(p.astype(BF16), vv, 1, 0)

    q, kc, kp, vc, vp, sk = _attn_specs(nb, False)
    return pl.pallas_call(
        body, name="attn_fwd", grid=(nb,), in_specs=[q, kc, kp, vc, vp, sk], out_specs=q,
        out_shape=jax.ShapeDtypeStruct((T, D), F32), compiler_params=_params(1),
    )(proj, proj, proj, proj, proj, sinks)


def _attn_bwd(proj, sinks, d_o):
    T = proj.shape[0]
    nb = T // WINDOW

    def body(q_ref, kc_ref, kp_ref, vc_ref, vp_ref, s_ref, do_ref, dq_ref, dk_ref, dv_ref, ds_ref, ck_ref, cv_ref):
        i = pl.program_id(0)
        n = nb - 1 - i

        @pl.when(i == 0)
        def _():
            ck_ref[...] = jnp.zeros_like(ck_ref)
            cv_ref[...] = jnp.zeros_like(cv_ref)
            ds_ref[...] = jnp.zeros_like(ds_ref)

        mask = _attn_mask(n)
        for h in range(N_KV):
            hs = slice(h * HEAD_DIM, (h + 1) * HEAD_DIM)
            kk = jnp.concatenate([kp_ref[:, hs], kc_ref[:, hs]], axis=0).astype(BF16)
            vv = jnp.concatenate([vp_ref[:, hs], vc_ref[:, hs]], axis=0).astype(BF16)
            dkk = jnp.zeros((2 * WINDOW, HEAD_DIM), F32)
            dvv = jnp.zeros((2 * WINDOW, HEAD_DIM), F32)
            for g in range(GROUP):
                hg = h * GROUP + g
                cs = slice(hg * HEAD_DIM, (hg + 1) * HEAD_DIM)
                q = q_ref[:, cs].astype(BF16)
                p, ps = _attn_probs(q, kk, mask, s_ref[:, hg:hg + 1])
                do = do_ref[:, cs].astype(BF16)
                dp = _dot(do, vv, 1, 1)
                delta = jnp.sum(p * dp, axis=-1, keepdims=True)
                dsc = (p * (dp - delta) * (HEAD_DIM ** -0.5)).astype(BF16)
                dsink = jnp.sum(-ps * delta, axis=0, keepdims=True)
                ds_ref[hg:hg + 1, :] += jnp.broadcast_to(dsink, (1, 128))
                dq_ref[:, cs] = _dot(dsc, kk, 1, 0).astype(BF16)
                dkk = dkk + _dot(dsc, q, 0, 0)
                dvv = dvv + _dot(p.astype(BF16), do, 0, 0)
            dk_ref[:, hs] = (dkk[WINDOW:] + ck_ref[:, hs]).astype(BF16)
            dv_ref[:, hs] = (dvv[WINDOW:] + cv_ref[:, hs]).astype(BF16)
            ck_ref[:, hs] = dkk[:WINDOW]
            cv_ref[:, hs] = dvv[:WINDOW]

    q, kc, kp, vc, vp, sk = _attn_specs(nb, True)
    kv_out = pl.BlockSpec((WINDOW, 256), lambda n: (nb - 1 - n, 0))
    return pl.pallas_call(
        body, name="attn_bwd", grid=(nb,), in_specs=[q, kc, kp, vc, vp, sk, q],
        out_specs=[q, kv_out, kv_out, pl.BlockSpec((N_Q, 128), lambda n: (0, 0))],
        out_shape=[jax.ShapeDtypeStruct((T, D), BF16), jax.ShapeDtypeStruct((T, 256), BF16),
                   jax.ShapeDtypeStruct((T, 256), BF16), jax.ShapeDtypeStruct((N_Q, 128), F32)],
        scratch_shapes=[pltpu.VMEM((WINDOW, 256), F32), pltpu.VMEM((WINDOW, 256), F32)],
        compiler_params=_params(1),
    )(proj, proj, proj, proj, proj, sinks, d_o)


def _split3(x):
    hi = x.astype(BF16)
    r1 = x - hi.astype(F32)
    mid = r1.astype(BF16)
    lo = (r1 - mid.astype(F32)).astype(BF16)
    return hi, mid, lo


def _tri_sum(tri, x):
    hi, mid, lo = _split3(x)
    return _dot(tri, hi, 1, 0) + _dot(tri, mid, 1, 0) + _dot(tri, lo, 1, 0)


def _gla_chunk(q, k, lr, w2, b, lower):
    logit = _dot(lr.astype(BF16), w2.astype(BF16), 1, 0) + b
    la = (jnp.minimum(logit, 0.0) - jnp.log(1.0 + jnp.exp(-jnp.abs(logit)))) * (1.0 / GATE_NORM)
    g = _tri_sum(lower, la)
    gl = g[CHUNK - 1:CHUNK, :]
    eg = jnp.exp(g)
    qd = q * (DK ** -0.5) * eg
    ki = k * jnp.exp(-g)
    ke = k * jnp.exp(gl - g)
    return logit, g, gl, eg, qd, ki, ke


def _tri(lower):
    r = lax.broadcasted_iota(jnp.int32, (CHUNK, CHUNK), 0)
    c = lax.broadcasted_iota(jnp.int32, (CHUNK, CHUNK), 1)
    return (r >= c) if lower else (r <= c)


def _gla_rows(T):
    return _pick(T, (256, 128, 64))


def _gla_fwd(proj, proj_lr, w2p, gate_b):
    T = proj.shape[0]
    rb = _gla_rows(T)
    per = rb // CHUNK

    def body(q_ref, k_ref, v_ref, lr_ref, w2_ref, b_ref, o_ref, st_ref, s_scr):
        @pl.when(pl.program_id(1) == 0)
        def _():
            s_scr[...] = jnp.zeros_like(s_scr)

        low = _tri(True)
        lower = low.astype(BF16)
        for i in range(per):
            rows = slice(i * CHUNK, (i + 1) * CHUNK)
            _, g, gl, eg, qd, ki, ke = _gla_chunk(q_ref[rows, :], k_ref[rows, :], lr_ref[rows, :], w2_ref[...],
                                                  b_ref[...], lower)
            v = v_ref[rows, :].astype(BF16)
            qdb = qd.astype(BF16)
            att = jnp.where(low, _dot(qdb, ki.astype(BF16), 1, 1), 0.0)
            st = s_scr[...]
            st_ref[0, i] = st
            o_ref[rows, :] = _dot(att.astype(BF16), v, 1, 0) + _dot(qdb, st.astype(BF16), 1, 1)
            s_scr[...] = st * jnp.exp(gl) + _dot(v, ke.astype(BF16), 0, 0)

    return pl.pallas_call(
        body, name="gla_fwd", grid=(GLA_H, T // rb),
        in_specs=[pl.BlockSpec((rb, DK), lambda h, n: (n, O_GQ // DK + h)),
                  pl.BlockSpec((rb, DK), lambda h, n: (n, O_GK // DK + h)),
                  pl.BlockSpec((rb, DV), lambda h, n: (n, O_GV // DV + h)),
                  pl.BlockSpec((rb, LRP), lambda h, n: (n, 0)),
                  pl.BlockSpec((LRP, DK), lambda h, n: (0, h)),
                  pl.BlockSpec((1, DK), lambda h, n: (0, h))],
        out_specs=[pl.BlockSpec((rb, DV), lambda h, n: (n, h)),
                   pl.BlockSpec((1, per, DV, DK), lambda h, n: (h, n, 0, 0))],
        out_shape=[jax.ShapeDtypeStruct((T, GLA_H * DV), F32),
                   jax.ShapeDtypeStruct((GLA_H, T // CHUNK, DV, DK), F32)],
        scratch_shapes=[pltpu.VMEM((DV, DK), F32)], compiler_params=_params(2),
    )(proj, proj, proj, proj_lr, w2p, gate_b)


def _gla_bwd(proj, proj_lr, w2p, gate_b, states, d_o):
    T = proj.shape[0]
    rb = _gla_rows(T)
    per = rb // CHUNK
    nblk = T // rb

    def body(q_ref, k_ref, v_ref, lr_ref, w2_ref, b_ref, st_ref, do_ref, dq_ref, dk_ref, dv_ref, dl_ref, ds_scr):
        @pl.when(pl.program_id(1) == 0)
        def _():
            ds_scr[...] = jnp.zeros_like(ds_scr)

        low = _tri(True)
        lower = low.astype(BF16)
        upper = _tri(False).astype(BF16)
        for i in reversed(range(per)):
            rows = slice(i * CHUNK, (i + 1) * CHUNK)
            logit, g, gl, eg, qd, ki, ke = _gla_chunk(q_ref[rows, :], k_ref[rows, :], lr_ref[rows, :], w2_ref[...],
                                                      b_ref[...], lower)
            v = v_ref[rows, :].astype(BF16)
            do = do_ref[rows, :].astype(BF16)
            qdb, kib, keb = qd.astype(BF16), ki.astype(BF16), ke.astype(BF16)
            att = jnp.where(low, _dot(qdb, kib, 1, 1), 0.0).astype(BF16)
            sp = st_ref[0, i]
            ds = ds_scr[...]
            dsb = ds.astype(BF16)
            datt = jnp.where(low, _dot(do, v, 1, 1), 0.0).astype(BF16)
            dv_ref[rows, :] = (_dot(att, do, 0, 0) + _dot(keb, dsb, 1, 1)).astype(BF16)
            dqd = _dot(datt, kib, 1, 0) + _dot(do, sp.astype(BF16), 1, 0)
            dki = _dot(datt, qdb, 0, 0)
            dke = _dot(v, dsb, 1, 0)
            decay = jnp.exp(gl)
            ds_scr[...] = ds * decay + _dot(do, qdb, 0, 0)
            ddec = jnp.sum(ds * sp, axis=0, keepdims=True)
            dq_ref[rows, :] = (dqd * (DK ** -0.5) * eg).astype(BF16)
            dk_ref[rows, :] = (dki * jnp.exp(-g) + dke * jnp.exp(gl - g)).astype(BF16)
            dke_ke = dke * ke
            dg = dqd * qd - dki * ki - dke_ke
            dgl = jnp.sum(dke_ke, axis=0, keepdims=True) + ddec * decay
            dla = _tri_sum(upper, dg) + dgl
            dl_ref[rows, :] = dla * (1.0 / GATE_NORM) * (1.0 - _sigmoid(logit))

    def rev(n):
        return nblk - 1 - n

    qk_out = pl.BlockSpec((rb, DK), lambda h, n: (rev(n), h))
    return pl.pallas_call(
        body, name="gla_bwd", grid=(GLA_H, nblk),
        in_specs=[pl.BlockSpec((rb, DK), lambda h, n: (rev(n), O_GQ // DK + h)),
                  pl.BlockSpec((rb, DK), lambda h, n: (rev(n), O_GK // DK + h)),
                  pl.BlockSpec((rb, DV), lambda h, n: (rev(n), O_GV // DV + h)),
                  pl.BlockSpec((rb, LRP), lambda h, n: (rev(n), 0)),
                  pl.BlockSpec((LRP, DK), lambda h, n: (0, h)),
                  pl.BlockSpec((1, DK), lambda h, n: (0, h)),
                  pl.BlockSpec((1, per, DV, DK), lambda h, n: (h, rev(n), 0, 0)),
                  pl.BlockSpec((rb, DV), lambda h, n: (rev(n), h))],
        out_specs=[qk_out, qk_out, pl.BlockSpec((rb, DV), lambda h, n: (rev(n), h)), qk_out],
        out_shape=[jax.ShapeDtypeStruct((T, GLA_H * DK), BF16), jax.ShapeDtypeStruct((T, GLA_H * DK), BF16),
                   jax.ShapeDtypeStruct((T, GLA_H * DV), BF16), jax.ShapeDtypeStruct((T, GLA_H * DK), F32)],
        scratch_shapes=[pltpu.VMEM((DV, DK), F32)], compiler_params=_params(2),
    )(proj, proj, proj, proj_lr, w2p, gate_b, states, d_o)


def _local_step(x, target, p):
    u = _rms_fwd(x, p["norm1_w"], name="rms1_fwd")
    proj = _matmul(u, p["w_in_main"], name="mm_proj")
    proj_lr = _matmul(u, p["w_in_lr"], name="mm_proj_lr")
    attn_o = _attn_fwd(proj, p["sinks"])
    gla_raw, states = _gla_fwd(proj, proj_lr, p["w2p"], p["gate_b"])
    merged = _merge_fwd(attn_o, gla_raw, proj, p["gla_norm_w"])
    h1 = _matmul(merged, p["w_out"], add=x, name="mm_out")
    v2 = _rms_fwd(h1, p["norm2_w"], name="rms2_fwd")
    gate = _matmul(v2, p["w_gate"], name="mm_gate")
    up = _matmul(v2, p["w_up"], name="mm_up")
    ff = _swiglu_fwd(gate, up)
    h2 = _matmul(ff, p["w_down"], add=h1, name="mm_down")
    loss, dh2, g_final = _final_loss(h2, p["final_norm_w"], target)

    dff = _matmul(dh2, p["w_down"], tb=True, name="mm_dff")
    g_down = _matmul(ff, dh2, ta=True, out_dtype=BF16, name="mm_gdown")
    dgate, dup = _swiglu_bwd(dff, gate, up)
    g_gate = _matmul(v2, dgate, ta=True, out_dtype=BF16, name="mm_ggate")
    g_up = _matmul(v2, dup, ta=True, out_dtype=BF16, name="mm_gup")
    dv2 = _matmul(dgate, p["w_gate"], tb=True, name="mm_dv2a")
    dv2 = _matmul(dup, p["w_up"], tb=True, add=dv2, name="mm_dv2b")
    dh1, g_norm2 = _rms_bwd(dv2, h1, p["norm2_w"], dh2, name="rms2_bwd")
    dmerged = _matmul(dh1, p["w_out"], tb=True, name="mm_dmerged")
    g_out = _matmul(merged, dh1, ta=True, out_dtype=BF16, name="mm_gout")
    d_attn, d_gla, d_gr, d_ga, d_gb, g_gla_norm = _merge_bwd(dmerged, attn_o, gla_raw, proj, p["gla_norm_w"])
    d_gq, d_gk, d_gv, dlogit = _gla_bwd(proj, proj_lr, p["w2p"], p["gate_b"], states, d_gla)
    d_aq, d_ak, d_av, g_sinks = _attn_bwd(proj, p["sinks"], d_attn)
    g_gate_b = _colsum(dlogit, name="colsum_gate_b")
    g_w2 = _matmul(proj_lr, dlogit, ta=True, name="mm_gw2")
    dproj_lr = _matmul(dlogit, p["w2p"], tb=True, out_dtype=BF16, name="mm_dlr")
    dproj = jnp.concatenate([d_aq, d_ak, d_av, d_gq, d_gk, d_gv, d_gr, d_ga, d_gb], axis=1)
    g_in_main = _matmul(u, dproj, ta=True, out_dtype=BF16, name="mm_gin")
    g_in_lr = _matmul(u, dproj_lr, ta=True, out_dtype=BF16, name="mm_gin_lr")
    du = _matmul(dproj, p["w_in_main"], tb=True, name="mm_du")
    du = _matmul(dproj_lr, p["w_in_lr"], tb=True, add=du, name="mm_du_lr")
    dx, g_norm1 = _rms_bwd(du, x, p["norm1_w"], dh1, name="rms1_bwd")
    grads = dict(norm1_w=g_norm1, w_in_main=g_in_main, w_in_lr=g_in_lr, w2=g_w2[:RANK], gate_b=g_gate_b,
                 sinks=g_sinks[:, 0].reshape(1, N_Q), gla_norm_w=g_gla_norm, w_out=g_out, norm2_w=g_norm2,
                 w_gate=g_gate, w_up=g_up, w_down=g_down, final_norm_w=g_final)
    return loss, dx, grads


def _place():
    x, y, c = lax.axis_index("x"), lax.axis_index("y"), lax.axis_index("c")
    chips = [(1 - x, y), (x, 1 - y), (1 - x, 1 - y)]
    return x, y, c, chips


def _hbm_shape(s, dt):
    return jax.ShapeDtypeStruct(s, dt)


def _gather_shards(shards):
    n = len(shards)

    def body(*refs):
        ins, outs = refs[:n], refs[n:2 * n]
        ici_send, ici_recv, d2d_send, d2d_recv, local_sem = refs[2 * n:]
        x, y, c, chips = _place()
        me = 2 * x + y
        sibling = (x, y, 1 - c)

        def half(w, slot, hc):
            r2 = shards[w].shape[0] // 2
            return outs[w].at[slot, pl.ds(hc * r2, r2), :]

        locals_ = [pltpu.make_async_copy(ins[w], outs[w].at[me], local_sem.at[w]) for w in range(n)]
        for cp in locals_:
            cp.start()
        sends = []
        for w in range(n):
            r2 = shards[w].shape[0] // 2
            for j, chip in enumerate(chips):
                cp = pltpu.make_async_remote_copy(
                    src_ref=ins[w].at[pl.ds(c * r2, r2), :], dst_ref=half(w, me, c),
                    send_sem=ici_send.at[w * 3 + j], recv_sem=ici_recv.at[w * 3 + j],
                    device_id=(*chip, c), device_id_type=MESH)
                cp.start()
                sends.append(cp)
        for w in range(n):
            for j, chip in enumerate(chips):
                slot = 2 * chip[0] + chip[1]
                got = half(w, slot, c)
                pltpu.make_async_remote_copy(
                    src_ref=got, dst_ref=got, send_sem=ici_send.at[w * 3 + j], recv_sem=ici_recv.at[w * 3 + j],
                    device_id=(*chip, c), device_id_type=MESH).wait_recv()
                cp = pltpu.make_async_remote_copy(
                    src_ref=got, dst_ref=got, send_sem=d2d_send.at[w * 3 + j], recv_sem=d2d_recv.at[w * 3 + j],
                    device_id=sibling, device_id_type=MESH)
                cp.start()
                sends.append(cp)
        for w in range(n):
            for j, chip in enumerate(chips):
                slot = 2 * chip[0] + chip[1]
                got = half(w, slot, 1 - c)
                pltpu.make_async_remote_copy(
                    src_ref=got, dst_ref=got, send_sem=d2d_send.at[w * 3 + j], recv_sem=d2d_recv.at[w * 3 + j],
                    device_id=sibling, device_id_type=MESH).wait_recv()
        for cp in sends:
            cp.wait_send()
        for cp in locals_:
            cp.wait()

    return pl.pallas_call(
        body, name="gather_weights", in_specs=[ANY] * n, out_specs=[ANY] * n,
        out_shape=[_hbm_shape((4,) + s.shape, s.dtype) for s in shards],
        scratch_shapes=[pltpu.SemaphoreType.DMA((3 * n,)), pltpu.SemaphoreType.DMA((3 * n,)),
                        pltpu.SemaphoreType.DMA((3 * n,)), pltpu.SemaphoreType.DMA((3 * n,)),
                        pltpu.SemaphoreType.DMA((n,))],
    )(*shards)


def _pair_blocks(R2):
    return _pick(R2, (256, 352, 128, 64, 32, 16))


def _pair_reduce(part, *, name):
    _, R, C = part.shape
    R2 = R // 2
    rb = _pair_blocks(R2)
    nblk = R2 // rb
    steps = [(s, i) for s in range(4) for i in range(nblk)]
    n = len(steps)

    def body(part_ref, sums_ref, own_buf, oth_buf, rcv_buf, out_buf, own_sem, oth_sem, out_sem, send_sem, recv_sem):
        x, y, c, _ = _place()
        sibling = (x, y, 1 - c)

        def fetch(t, half, buf, sem):
            s, i = steps[t]
            return pltpu.make_async_copy(part_ref.at[s, pl.ds(half * R2 + i * rb, rb), :], buf.at[t % 2], sem.at[t % 2])

        def push(t):
            return pltpu.make_async_remote_copy(
                src_ref=oth_buf.at[t % 2], dst_ref=rcv_buf.at[t % 2], send_sem=send_sem.at[t % 2],
                recv_sem=recv_sem.at[t % 2], device_id=sibling, device_id_type=MESH)

        def store(t):
            s, i = steps[t]
            return pltpu.make_async_copy(out_buf.at[t % 2], sums_ref.at[s, pl.ds(i * rb, rb), :], out_sem.at[t % 2])

        fetch(0, c, own_buf, own_sem).start()
        fetch(0, 1 - c, oth_buf, oth_sem).start()
        for t in range(n):
            if t + 1 < n:
                fetch(t + 1, c, own_buf, own_sem).start()
                fetch(t + 1, 1 - c, oth_buf, oth_sem).start()
            fetch(t, 1 - c, oth_buf, oth_sem).wait()
            p = push(t)
            p.start()
            fetch(t, c, own_buf, own_sem).wait()
            p.wait()
            if t >= 2:
                store(t - 2).wait()
            out_buf[t % 2] = (own_buf[t % 2].astype(F32) + rcv_buf[t % 2].astype(F32)).astype(BF16)
            store(t).start()
        for t in range(max(n - 2, 0), n):
            store(t).wait()

    buf = pltpu.VMEM((2, rb, C), BF16)
    sem2 = pltpu.SemaphoreType.DMA((2,))
    return pl.pallas_call(
        body, name=name, in_specs=[ANY], out_specs=ANY, out_shape=_hbm_shape((4, R2, C), BF16),
        scratch_shapes=[buf, buf, buf, buf, sem2, sem2, sem2, sem2, sem2],
        compiler_params=pltpu.CompilerParams(vmem_limit_bytes=VMEM_LIMIT),
    )(part)


def _chip_exchange(sums):
    n = len(sums)

    def body(*refs):
        ins, outs = refs[:n], refs[n:2 * n]
        send_sem, recv_sem, local_sem = refs[2 * n:]
        x, y, c, chips = _place()
        me = 2 * x + y
        cps = []
        for w in range(n):
            lc = pltpu.make_async_copy(ins[w].at[me], outs[w].at[me], local_sem.at[w])
            lc.start()
            cps.append(lc)
            for j, chip in enumerate(chips):
                slot = 2 * chip[0] + chip[1]
                rc = pltpu.make_async_remote_copy(
                    src_ref=ins[w].at[slot], dst_ref=outs[w].at[me],
                    send_sem=send_sem.at[w * 3 + j], recv_sem=recv_sem.at[w * 3 + j],
                    device_id=(*chip, c), device_id_type=MESH)
                rc.start()
                cps.append(rc)
        for w in range(n):
            for j, chip in enumerate(chips):
                slot = 2 * chip[0] + chip[1]
                pltpu.make_async_remote_copy(
                    src_ref=ins[w].at[slot], dst_ref=outs[w].at[slot],
                    send_sem=send_sem.at[w * 3 + j], recv_sem=recv_sem.at[w * 3 + j],
                    device_id=(*chip, c), device_id_type=MESH).wait_recv()
        for w in range(n):
            cps[w * 4].wait()
            for j in range(3):
                cps[w * 4 + 1 + j].wait_send()

    return pl.pallas_call(
        body, name="chip_exchange", in_specs=[ANY] * n, out_specs=[ANY] * n,
        out_shape=[_hbm_shape(s.shape, s.dtype) for s in sums],
        scratch_shapes=[pltpu.SemaphoreType.DMA((3 * n,)), pltpu.SemaphoreType.DMA((3 * n,)),
                        pltpu.SemaphoreType.DMA((n,))],
    )(*sums)


def _sum_join(recv, *, name):
    _, R2, C = recv.shape
    rb = _pair_blocks(R2)
    nblk = R2 // rb

    def body(recv_ref, out_ref, in_buf, acc_buf, in_sem, loc_sem, send_sem, recv_sem):
        x, y, c, _ = _place()
        sibling = (x, y, 1 - c)

        def fetch(t):
            return pltpu.make_async_copy(recv_ref.at[:, pl.ds(t * rb, rb), :], in_buf.at[t % 2], in_sem.at[t % 2])

        def rows(t, half):
            return out_ref.at[pl.ds(half * R2 + t * rb, rb), :]

        def put_local(t):
            return pltpu.make_async_copy(acc_buf.at[t % 2], rows(t, c), loc_sem.at[t])

        def put_remote(t, half):
            return pltpu.make_async_remote_copy(
                src_ref=acc_buf.at[t % 2], dst_ref=rows(t, half), send_sem=send_sem.at[t], recv_sem=recv_sem.at[t],
                device_id=sibling, device_id_type=MESH)

        fetch(0).start()
        for t in range(nblk):
            if t + 1 < nblk:
                fetch(t + 1).start()
            fetch(t).wait()
            if t >= 2:
                put_local(t - 2).wait()
                put_remote(t - 2, c).wait_send()
            acc = in_buf[t % 2, 0].astype(F32)
            for s in range(1, 4):
                acc = acc + in_buf[t % 2, s].astype(F32)
            acc_buf[t % 2] = acc
            put_local(t).start()
            put_remote(t, c).start()
        for t in range(max(nblk - 2, 0), nblk):
            put_local(t).wait()
            put_remote(t, c).wait_send()
        for t in range(nblk):
            put_remote(t, 1 - c).wait_recv()

    semn = pltpu.SemaphoreType.DMA((nblk,))
    return pl.pallas_call(
        body, name=name, in_specs=[ANY], out_specs=ANY, out_shape=_hbm_shape((2 * R2, C), F32),
        scratch_shapes=[pltpu.VMEM((2, 4, rb, C), BF16), pltpu.VMEM((2, rb, C), F32),
                        pltpu.SemaphoreType.DMA((2,)), semn, semn, semn],
        compiler_params=pltpu.CompilerParams(vmem_limit_bytes=VMEM_LIMIT),
    )(recv)


def _sum_small(pack):
    R, C = pack.shape

    def body(in_ref, out_ref, all_ref, send_sem, recv_sem):
        x, y, c, _ = _place()
        me = 4 * x + 2 * y + c
        all_ref[me] = in_ref[...]
        cps = []
        for k in range(1, 8):
            peer = (x ^ (k >> 2), y ^ ((k >> 1) & 1), c ^ (k & 1))
            cp = pltpu.make_async_remote_copy(
                src_ref=in_ref, dst_ref=all_ref.at[me], send_sem=send_sem.at[k - 1], recv_sem=recv_sem.at[k - 1],
                device_id=peer, device_id_type=MESH)
            cp.start()
            cps.append(cp)
        for k in range(1, 8):
            peer = (x ^ (k >> 2), y ^ ((k >> 1) & 1), c ^ (k & 1))
            slot = 4 * peer[0] + 2 * peer[1] + peer[2]
            pltpu.make_async_remote_copy(
                src_ref=in_ref, dst_ref=all_ref.at[slot], send_sem=send_sem.at[k - 1], recv_sem=recv_sem.at[k - 1],
                device_id=peer, device_id_type=MESH).wait_recv()
        for cp in cps:
            cp.wait_send()
        acc = all_ref[0]
        for d in range(1, 8):
            acc = acc + all_ref[d]
        out_ref[...] = acc

    return pl.pallas_call(
        body, name="sum_small", in_specs=[pl.BlockSpec(memory_space=pltpu.VMEM)],
        out_specs=pl.BlockSpec(memory_space=pltpu.VMEM), out_shape=jax.ShapeDtypeStruct((R, C), F32),
        scratch_shapes=[pltpu.VMEM((8, R, C), F32), pltpu.SemaphoreType.DMA((7,)), pltpu.SemaphoreType.DMA((7,))],
    )(pack)


def _adamw(w, g, m, v, *, name):
    R, C = w.shape
    rb = _pick(R, (128, 64, 32, 16, 8))
    c1 = 1.0 / (1.0 - B1 ** STEP)
    c2 = 1.0 / (1.0 - B2 ** STEP)

    def body(w_ref, g_ref, m_ref, v_ref, d_ref, nm_ref, nv_ref):
        gv = g_ref[...]
        m2 = B1 * m_ref[...] + (1.0 - B1) * gv
        v2 = B2 * v_ref[...] + (1.0 - B2) * gv * gv
        nm_ref[...] = m2
        nv_ref[...] = v2
        d_ref[...] = -LR * ((m2 * c1) / (jnp.sqrt(v2 * c2) + ADAM_EPS) + WD * w_ref[...])

    blk = pl.BlockSpec((rb, C), lambda i: (i, 0))
    return pl.pallas_call(
        body, name=name, grid=(R // rb,), in_specs=[blk] * 4, out_specs=[blk] * 3,
        out_shape=[jax.ShapeDtypeStruct((R, C), F32)] * 3, compiler_params=_params(1),
    )(w, g, m, v)


SMALL = (("norm1_w", D), ("norm2_w", D), ("final_norm_w", D), ("gate_b", GLA_H * DK), ("gla_norm_w", DV), ("sinks", N_Q))
PACK_W = 1024


def _pack_small(vals, w2, loss):
    rows = []
    for name, width in SMALL:
        v = vals[name].reshape(-1)
        rows.append(jnp.pad(v, (0, (-width) % PACK_W)).reshape(-1, PACK_W))
    rows.append(w2)
    rows.append(jnp.broadcast_to(loss.reshape(1, 1), (1, PACK_W)))
    pack = jnp.concatenate(rows, axis=0)
    return jnp.pad(pack, ((0, 32 - pack.shape[0]), (0, 0)))


def _unpack_small(pack):
    out, r = {}, 0
    for name, width in SMALL:
        nr = -(-width // PACK_W)
        out[name] = pack[r:r + nr].reshape(-1)[:width]
        r += nr
    out["w2"] = pack[r:r + RANK]
    out["loss"] = pack[r + RANK, 0]
    return out


def kernel(x, norm1_w, w_in, gla_gate_w2, gla_gate_b, attn_sinks, gla_norm_w, w_out, norm2_w, w_ffn_gate, w_ffn_up, w_ffn_down, final_norm_w, loss_target, m_norm1_w, m_w_in, m_gla_gate_w2, m_gla_gate_b, m_attn_sinks, m_gla_norm_w, m_w_out, m_norm2_w, m_w_ffn_gate, m_w_ffn_up, m_w_ffn_down, m_final_norm_w, v_norm1_w, v_w_in, v_gla_gate_w2, v_gla_gate_b, v_attn_sinks, v_gla_norm_w, v_w_out, v_norm2_w, v_w_ffn_gate, v_w_ffn_up, v_w_ffn_down, v_final_norm_w):
    chip = 2 * lax.axis_index("x") + lax.axis_index("y")
    w_in_s, w_out_s, w_gate_s, w_up_s, w_down_s, w2_s = (
        w_in[0], w_out[0], w_ffn_gate[0], w_ffn_up[0], w_ffn_down[0], gla_gate_w2[0])
    CS = w_in_s.shape[1]

    g_in, g_out, g_gate, g_up, g_down, g_w2 = _gather_shards(
        [w_in_s.astype(BF16), w_out_s.astype(BF16), w_gate_s.astype(BF16), w_up_s.astype(BF16),
         w_down_s.astype(BF16), w2_s])
    full_in = jnp.transpose(g_in, (1, 0, 2)).reshape(D, 4 * CS)
    w2_full = jnp.transpose(g_w2, (1, 0, 2)).reshape(RANK, GLA_H * DK)
    p = dict(
        norm1_w=norm1_w, norm2_w=norm2_w, final_norm_w=final_norm_w.reshape(1, D), gate_b=gla_gate_b,
        gla_norm_w=gla_norm_w, sinks=jnp.pad(attn_sinks, ((0, 0), (0, 128 - N_Q))),
        w_in_main=jnp.concatenate([full_in[:, :LR_AT], full_in[:, LR_AT + RANK:]], axis=1),
        w_in_lr=jnp.pad(full_in[:, LR_AT:LR_AT + RANK], ((0, 0), (0, LRP - RANK))),
        w2p=jnp.pad(w2_full, ((0, LRP - RANK), (0, 0))).astype(BF16),
        w_out=g_out.reshape(D, D),
        w_gate=jnp.transpose(g_gate, (1, 0, 2)).reshape(D, FFN),
        w_up=jnp.transpose(g_up, (1, 0, 2)).reshape(D, FFN),
        w_down=g_down.reshape(FFN, D),
    )

    loss_blk, dx, g = _local_step(x[0], loss_target[0], p)

    gin_full = jnp.concatenate([g["w_in_main"][:, :LR_AT], g["w_in_lr"][:, :RANK], g["w_in_main"][:, LR_AT:]], axis=1)
    parts = [
        jnp.transpose(gin_full.reshape(D, 4, CS), (1, 0, 2)),
        g["w_out"].reshape(4, D // 4, D),
        jnp.transpose(g["w_gate"].reshape(D, 4, FFN // 4), (1, 0, 2)),
        jnp.transpose(g["w_up"].reshape(D, 4, FFN // 4), (1, 0, 2)),
        g["w_down"].reshape(4, FFN // 4, D),
    ]
    tags = ["w_in", "w_out", "w_gate", "w_up", "w_down"]
    sums = [_pair_reduce(pt, name="pair_reduce_" + tg) for pt, tg in zip(parts, tags)]
    recv = _chip_exchange(sums)
    big = [_sum_join(r, name="sum_join_" + tg) for r, tg in zip(recv, tags)]

    small = _unpack_small(_sum_small(_pack_small(g, g["w2"], loss_blk[0, 0])))
    loss = small["loss"]
    g_w2_mine = lax.dynamic_slice_in_dim(small["w2"], chip * (GLA_H * DK // 4), GLA_H * DK // 4, axis=1)

    grads = dict(
        norm1_w=small["norm1_w"].reshape(1, D), w_in=big[0], gla_gate_w2=g_w2_mine,
        gla_gate_b=small["gate_b"].reshape(1, -1), attn_sinks=small["sinks"].reshape(1, N_Q),
        gla_norm_w=small["gla_norm_w"].reshape(1, DV), w_out=big[1], norm2_w=small["norm2_w"].reshape(1, D),
        w_ffn_gate=big[2], w_ffn_up=big[3], w_ffn_down=big[4], final_norm_w=small["final_norm_w"].reshape(1, D))
    weights = dict(
        norm1_w=(norm1_w, m_norm1_w, v_norm1_w), w_in=(w_in_s, m_w_in[0], v_w_in[0]),
        gla_gate_w2=(w2_s, m_gla_gate_w2[0], v_gla_gate_w2[0]), gla_gate_b=(gla_gate_b, m_gla_gate_b, v_gla_gate_b),
        attn_sinks=(attn_sinks, m_attn_sinks, v_attn_sinks), gla_norm_w=(gla_norm_w, m_gla_norm_w, v_gla_norm_w),
        w_out=(w_out_s, m_w_out[0], v_w_out[0]), norm2_w=(norm2_w, m_norm2_w, v_norm2_w),
        w_ffn_gate=(w_gate_s, m_w_ffn_gate[0], v_w_ffn_gate[0]), w_ffn_up=(w_up_s, m_w_ffn_up[0], v_w_ffn_up[0]),
        w_ffn_down=(w_down_s, m_w_ffn_down[0], v_w_ffn_down[0]),
        final_norm_w=(final_norm_w.reshape(1, D), m_final_norm_w.reshape(1, D), v_final_norm_w.reshape(1, D)))
    names = ["norm1_w", "w_in", "gla_gate_w2", "gla_gate_b", "attn_sinks", "gla_norm_w", "w_out", "norm2_w",
             "w_ffn_gate", "w_ffn_up", "w_ffn_down", "final_norm_w"]
    lead = {"norm1_w": False, "gla_gate_b": False, "attn_sinks": False, "gla_norm_w": False, "norm2_w": False}
    g_out_l, d_out, m_out, v_out = [], [], [], []
    for nm in names:
        w, m, v = weights[nm]
        gr = grads[nm]
        dl, nmn, nvn = _adamw(w, gr, m, v, name="adamw_" + nm)
        if nm == "final_norm_w":
            shape = (D,)
        elif nm in lead:
            shape = w.shape
        else:
            shape = (1,) + w.shape
        g_out_l.append(gr.reshape(shape))
        d_out.append(dl.reshape(shape))
        m_out.append(nmn.reshape(shape))
        v_out.append(nvn.reshape(shape))
    return (loss, dx[None], *g_out_l, *d_out, *m_out, *v_out)
```

```python
import functools

import jax
import jax.numpy as jnp
from jax import lax
from jax.experimental import pallas as pl
from jax.experimental.pallas import tpu as pltpu
from jax.experimental.pallas import tpu_sc as plsc

F32 = jnp.float32
BF16 = jnp.bfloat16

D = 2048
HEAD_DIM = 64
N_Q = 32
N_KV = 4
GROUP = 8
WINDOW = 128
GLA_H = 4
DK = 256
DV = 512
RANK = 16
CHUNK = 64
FFN = 5632
EPS = 1e-6
MASK_VALUE = -1e30
GATE_NORM = 16.0
LR, B1, B2, ADAM_EPS, WD, STEP = 0.001, 0.9, 0.999, 1e-08, 0.01, 10

MAIN = 12800
O_AQ, O_AK, O_AV, O_GQ, O_GK, O_GV, O_GR, O_GA, O_GB = 0, 2048, 2304, 2560, 3584, 4608, 6656, 8704, 10752
LR_AT = 6656
LRP = 128

VMEM_LIMIT = 56 * 1024 * 1024
MESH = pl.DeviceIdType.MESH
ANY = pl.BlockSpec(memory_space=pl.ANY)


def _params(ngrid):
    return pltpu.CompilerParams(dimension_semantics=("arbitrary",) * ngrid, vmem_limit_bytes=VMEM_LIMIT)


def _pick(n, cands):
    for c in cands:
        if n % c == 0:
            return c
    return n


def _sigmoid(x):
    return 1.0 / (1.0 + jnp.exp(-x))


def _dot(a, b, ca, cb):
    return lax.dot_general(a, b, (((ca,), (cb,)), ((), ())), preferred_element_type=F32)


def _matmul(a, b, *, name, ta=False, tb=False, add=None, out_dtype=F32, bm=None, bj=None, bc=None,
            b_slots=False, out_slots=False):
    C, M = a.shape if ta else a.shape[::-1]
    if b_slots:
        if tb:
            J, bc = b.shape[1], b.shape[2]
            assert b.shape[0] * bc == C
        else:
            bj = b.shape[2]
            J = b.shape[0] * bj
            assert b.shape[1] == C
    else:
        J = b.shape[0] if tb else b.shape[1]
        assert (b.shape[1] if tb else b.shape[0]) == C
    bm = bm or _pick(M, (1024, 512, 256, 128))
    bj = bj or _pick(J, (512, 256, 128))
    bc = bc or (C if C <= 2048 else _pick(C, (1408, 1280)))
    nm, nj, nc = M // bm, J // bj, C // bc
    has_add = add is not None

    def body(*refs):
        a_ref, b_ref = refs[0], refs[1]
        add_ref = refs[2] if has_add else None
        o_ref = refs[3] if has_add else refs[2]
        p = _dot(a_ref[...].astype(BF16), b_ref[...].astype(BF16), 0 if ta else 1, 1 if tb else 0)

        def finish(acc):
            if has_add:
                acc = acc + add_ref[...]
            o_ref[...] = acc.astype(o_ref.dtype)

        if nc == 1:
            finish(p)
        else:
            acc_ref = refs[-1]
            c = pl.program_id(2)

            @pl.when(c == 0)
            def _():
                acc_ref[...] = p

            @pl.when(c > 0)
            def _():
                acc_ref[...] += p

            @pl.when(c == nc - 1)
            def _():
                finish(acc_ref[...])

    a_spec = pl.BlockSpec((bc, bm), lambda m, j, c: (c, m)) if ta else pl.BlockSpec((bm, bc), lambda m, j, c: (m, c))
    if b_slots:
        b_spec = (pl.BlockSpec((None, bj, bc), lambda m, j, c: (c, j, 0)) if tb
                  else pl.BlockSpec((None, bc, bj), lambda m, j, c: (j, c, 0)))
    else:
        b_spec = (pl.BlockSpec((bj, bc), lambda m, j, c: (j, c)) if tb
                  else pl.BlockSpec((bc, bj), lambda m, j, c: (c, j)))
    if out_slots:
        assert not has_add
        o_spec = pl.BlockSpec((None, bm, bj), lambda m, j, c: (j, m, 0))
        out_shape = jax.ShapeDtypeStruct((nj, M, bj), out_dtype)
    else:
        o_spec = pl.BlockSpec((bm, bj), lambda m, j, c: (m, j))
        out_shape = jax.ShapeDtypeStruct((M, J), out_dtype)
    in_specs = [a_spec, b_spec] + ([o_spec] if has_add else [])
    args = (a, b) + ((add,) if has_add else ())
    return pl.pallas_call(
        body, name=name, grid=(nm, nj, nc), in_specs=in_specs, out_specs=o_spec,
        out_shape=out_shape,
        scratch_shapes=[pltpu.VMEM((bm, bj), F32)] if nc > 1 else [],
        compiler_params=_params(3),
    )(*args)


def _rows(T):
    return _pick(T, (256, 128))


def _rms_fwd(x, w, *, name):
    T = x.shape[0]
    rb = _rows(T)

    def body(x_ref, w_ref, u_ref):
        xv = x_ref[...]
        r = lax.rsqrt(jnp.mean(xv * xv, axis=-1, keepdims=True) + EPS)
        u_ref[...] = (xv * r * w_ref[...]).astype(BF16)

    return pl.pallas_call(
        body, name=name, grid=(T // rb,),
        in_specs=[pl.BlockSpec((rb, D), lambda i: (i, 0)), pl.BlockSpec((1, D), lambda i: (0, 0))],
        out_specs=pl.BlockSpec((rb, D), lambda i: (i, 0)),
        out_shape=jax.ShapeDtypeStruct((T, D), BF16), compiler_params=_params(1),
    )(x, w)


def _rms_bwd(dy, h, w, resid, *, name):
    T = h.shape[0]
    rb = _rows(T)

    def body(dy_ref, h_ref, w_ref, res_ref, dh_ref, dw_ref):
        @pl.when(pl.program_id(0) == 0)
        def _():
            dw_ref[...] = jnp.zeros_like(dw_ref)

        hv = h_ref[...]
        r = lax.rsqrt(jnp.mean(hv * hv, axis=-1, keepdims=True) + EPS)
        hn = hv * r
        dyv = dy_ref[...]
        dw_ref[...] += jnp.sum(dyv * hn, axis=0, keepdims=True)
        t = dyv * w_ref[...]
        dh_ref[...] = res_ref[...] + r * (t - hn * jnp.mean(t * hn, axis=-1, keepdims=True))

    blk = pl.BlockSpec((rb, D), lambda i: (i, 0))
    vec = pl.BlockSpec((1, D), lambda i: (0, 0))
    return pl.pallas_call(
        body, name=name, grid=(T // rb,), in_specs=[blk, blk, vec, blk], out_specs=[blk, vec],
        out_shape=[jax.ShapeDtypeStruct((T, D), F32), jax.ShapeDtypeStruct((1, D), F32)], compiler_params=_params(1),
    )(dy, h, w, resid)


def _final_loss(h, w, target):
    T = h.shape[0]
    rb = _rows(T)

    def body(h_ref, w_ref, t_ref, loss_ref, dh_ref, dw_ref):
        @pl.when(pl.program_id(0) == 0)
        def _():
            dw_ref[...] = jnp.zeros_like(dw_ref)
            loss_ref[...] = jnp.zeros_like(loss_ref)

        hv = h_ref[...]
        wv = w_ref[...]
        r = lax.rsqrt(jnp.mean(hv * hv, axis=-1, keepdims=True) + EPS)
        hn = hv * r
        e = hn * wv - t_ref[...]
        row = jnp.sum(e * e, axis=-1, keepdims=True) * (0.5 / D)
        loss_ref[...] += jnp.broadcast_to(jnp.sum(row, axis=0, keepdims=True), loss_ref.shape)
        dy = e * (1.0 / D)
        dw_ref[...] += jnp.sum(dy * hn, axis=0, keepdims=True)
        t = dy * wv
        dh_ref[...] = r * (t - hn * jnp.mean(t * hn, axis=-1, keepdims=True))

    blk = pl.BlockSpec((rb, D), lambda i: (i, 0))
    vec = pl.BlockSpec((1, D), lambda i: (0, 0))
    return pl.pallas_call(
        body, name="final_loss", grid=(T // rb,), in_specs=[blk, vec, blk],
        out_specs=[pl.BlockSpec((8, 128), lambda i: (0, 0)), blk, vec],
        out_shape=[jax.ShapeDtypeStruct((8, 128), F32), jax.ShapeDtypeStruct((T, D), F32),
                   jax.ShapeDtypeStruct((1, D), F32)],
        compiler_params=_params(1),
    )(h, w, target)


def _swiglu_fwd(gate, up):
    T = gate.shape[0]
    rb = _rows(T)

    def body(g_ref, u_ref, o_ref):
        g = g_ref[...]
        o_ref[...] = (g * _sigmoid(g) * u_ref[...]).astype(BF16)

    blk = pl.BlockSpec((rb, 512), lambda i, j: (i, j))
    return pl.pallas_call(
        body, name="swiglu_fwd", grid=(T // rb, FFN // 512), in_specs=[blk, blk], out_specs=blk,
        out_shape=jax.ShapeDtypeStruct((T, FFN), BF16), compiler_params=_params(2),
    )(gate, up)


def _swiglu_bwd(dff, gate, up):
    T = gate.shape[0]
    rb = _rows(T)

    def body(d_ref, g_ref, u_ref, dg_ref, du_ref):
        g = g_ref[...]
        d = d_ref[...]
        sg = _sigmoid(g)
        du_ref[...] = (d * g * sg).astype(BF16)
        dg_ref[...] = (d * u_ref[...] * sg * (1.0 + g * (1.0 - sg))).astype(BF16)

    blk = pl.BlockSpec((rb, 512), lambda i, j: (i, j))
    return pl.pallas_call(
        body, name="swiglu_bwd", grid=(T // rb, FFN // 512), in_specs=[blk, blk, blk], out_specs=[blk, blk],
        out_shape=[jax.ShapeDtypeStruct((T, FFN), BF16)] * 2, compiler_params=_params(2),
    )(dff, gate, up)


def _colsum(x, *, name):
    T, W = x.shape
    rb = _rows(T)

    def body(x_ref, o_ref):
        @pl.when(pl.program_id(0) == 0)
        def _():
            o_ref[...] = jnp.zeros_like(o_ref)

        o_ref[...] += jnp.sum(x_ref[...], axis=0, keepdims=True)

    return pl.pallas_call(
        body, name=name, grid=(T // rb,), in_specs=[pl.BlockSpec((rb, W), lambda i: (i, 0))],
        out_specs=pl.BlockSpec((1, W), lambda i: (0, 0)), out_shape=jax.ShapeDtypeStruct((1, W), F32),
        compiler_params=_params(1),
    )(x)


def _merge_fwd(attn_o, gla_raw, proj, gla_norm_w):
    T = attn_o.shape[0]
    rb = _rows(T)

    def body(a_ref, g_ref, gr_ref, ga_ref, gb_ref, w_ref, o_ref):
        graw = g_ref[...]
        r = lax.rsqrt(jnp.mean(graw * graw, axis=-1, keepdims=True) + EPS)
        gr = gr_ref[...]
        go = graw * r * w_ref[...] * (gr * _sigmoid(gr))
        o_ref[...] = (_sigmoid(ga_ref[...]) * a_ref[...] + _sigmoid(gb_ref[...]) * go).astype(BF16)

    def sec(off):
        return pl.BlockSpec((rb, DV), lambda i, h: (i, off // DV + h))

    blk = sec(0)
    return pl.pallas_call(
        body, name="merge_fwd", grid=(T // rb, GLA_H),
        in_specs=[blk, blk, sec(O_GR), sec(O_GA), sec(O_GB), pl.BlockSpec((1, DV), lambda i, h: (0, 0))],
        out_specs=blk, out_shape=jax.ShapeDtypeStruct((T, D), BF16), compiler_params=_params(2),
    )(attn_o, gla_raw, proj, proj, proj, gla_norm_w)


def _merge_bwd(dm, attn_o, gla_raw, proj, gla_norm_w):
    T = attn_o.shape[0]
    rb = _rows(T)

    def body(dm_ref, a_ref, g_ref, gr_ref, ga_ref, gb_ref, w_ref, da_ref, dg_ref, dgr_ref, dga_ref, dgb_ref, dw_ref):
        @pl.when((pl.program_id(0) == 0) & (pl.program_id(1) == 0))
        def _():
            dw_ref[...] = jnp.zeros_like(dw_ref)

        dmv = dm_ref[...]
        av = a_ref[...]
        graw = g_ref[...]
        gr = gr_ref[...]
        wv = w_ref[...]
        sa = _sigmoid(ga_ref[...])
        sb = _sigmoid(gb_ref[...])
        r = lax.rsqrt(jnp.mean(graw * graw, axis=-1, keepdims=True) + EPS)
        gnh = graw * r
        gn = gnh * wv
        sr = _sigmoid(gr)
        sl = gr * sr
        go = gn * sl
        da_ref[...] = dmv * sa
        dga_ref[...] = (dmv * av * sa * (1.0 - sa)).astype(BF16)
        dgo = dmv * sb
        dgb_ref[...] = (dmv * go * sb * (1.0 - sb)).astype(BF16)
        dgr_ref[...] = (dgo * gn * sr * (1.0 + gr * (1.0 - sr))).astype(BF16)
        dgn = dgo * sl
        dw_ref[...] += jnp.sum(dgn * gnh, axis=0, keepdims=True)
        t = dgn * wv
        dg_ref[...] = r * (t - gnh * jnp.mean(t * gnh, axis=-1, keepdims=True))

    def sec(off):
        return pl.BlockSpec((rb, DV), lambda i, h: (i, off // DV + h))

    blk = sec(0)
    vec = pl.BlockSpec((1, DV), lambda i, h: (0, 0))
    return pl.pallas_call(
        body, name="merge_bwd", grid=(T // rb, GLA_H),
        in_specs=[blk, blk, blk, sec(O_GR), sec(O_GA), sec(O_GB), vec],
        out_specs=[blk, blk, blk, blk, blk, vec],
        out_shape=[jax.ShapeDtypeStruct((T, D), F32), jax.ShapeDtypeStruct((T, D), F32),
                   jax.ShapeDtypeStruct((T, D), BF16), jax.ShapeDtypeStruct((T, D), BF16),
                   jax.ShapeDtypeStruct((T, D), BF16), jax.ShapeDtypeStruct((1, DV), F32)],
        compiler_params=_params(2),
    )(dm, attn_o, gla_raw, proj, proj, proj, gla_norm_w)


def _attn_mask(n):
    qi = lax.broadcasted_iota(jnp.int32, (GROUP * WINDOW, 2 * WINDOW), 0) & (WINDOW - 1)
    ki = lax.broadcasted_iota(jnp.int32, (GROUP * WINDOW, 2 * WINDOW), 1)
    rel = qi + WINDOW - ki
    return (rel >= 0) & (rel < WINDOW) & ((n > 0) | (ki >= WINDOW))


def _stack_heads(ref, h):
    return jnp.concatenate(
        [ref[:, (h * GROUP + g) * HEAD_DIM:(h * GROUP + g + 1) * HEAD_DIM] for g in range(GROUP)], axis=0).astype(BF16)


def _stack_sinks(s_ref, h):
    return jnp.concatenate(
        [jnp.broadcast_to(s_ref[:, h * GROUP + g:h * GROUP + g + 1], (WINDOW, 1)) for g in range(GROUP)], axis=0)


def _attn_probs(q, kk, mask, sink):
    s = _dot(q, kk, 1, 1) * (HEAD_DIM ** -0.5)
    s = jnp.where(mask, s, MASK_VALUE)
    m = jnp.maximum(jnp.max(s, axis=-1, keepdims=True), sink)
    e = jnp.exp(s - m)
    es = jnp.exp(sink - m)
    inv = 1.0 / (jnp.sum(e, axis=-1, keepdims=True) + es)
    return e * inv, es * inv


def _attn_specs(nb, rev):
    def at(n):
        return (nb - 1 - n) if rev else n

    q = pl.BlockSpec((WINDOW, D), lambda n: (at(n), 0))
    kc = pl.BlockSpec((WINDOW, 256), lambda n: (at(n), O_AK // 256))
    kp = pl.BlockSpec((WINDOW, 256), lambda n: (jnp.maximum(at(n) - 1, 0), O_AK // 256))
    vc = pl.BlockSpec((WINDOW, 256), lambda n: (at(n), O_AV // 256))
    vp = pl.BlockSpec((WINDOW, 256), lambda n: (jnp.maximum(at(n) - 1, 0), O_AV // 256))
    sk = pl.BlockSpec((1, 128), lambda n: (0, 0))
    return q, kc, kp, vc, vp, sk


def _attn_fwd(proj, sinks):
    T = proj.shape[0]
    nb = T // WINDOW

    def body(q_ref, kc_ref, kp_ref, vc_ref, vp_ref, s_ref, o_ref):
        mask = _attn_mask(pl.program_id(0))
        for h in range(N_KV):
            hs = slice(h * HEAD_DIM, (h + 1) * HEAD_DIM)
            kk = jnp.concatenate([kp_ref[:, hs], kc_ref[:, hs]], axis=0).astype(BF16)
            vv = jnp.concatenate([vp_ref[:, hs], vc_ref[:, hs]], axis=0).astype(BF16)
            p, _ = _attn_probs(_stack_heads(q_ref, h), kk, mask, _stack_sinks(s_ref, h))
            o = _dot(p.astype(BF16), vv, 1, 0)
            for g in range(GROUP):
                hg = h * GROUP + g
                o_ref[:, hg * HEAD_DIM:(hg + 1) * HEAD_DIM] = o[g * WINDOW:(g + 1) * WINDOW]

    q, kc, kp, vc, vp, sk = _attn_specs(nb, False)
    return pl.pallas_call(
        body, name="attn_fwd", grid=(nb,), in_specs=[q, kc, kp, vc, vp, sk], out_specs=q,
        out_shape=jax.ShapeDtypeStruct((T, D), F32), compiler_params=_params(1),
    )(proj, proj, proj, proj, proj, sinks)


def _attn_bwd(proj, sinks, d_o):
    T = proj.shape[0]
    nb = T // WINDOW

    def body(q_ref, kc_ref, kp_ref, vc_ref, vp_ref, s_ref, do_ref, dq_ref, dk_ref, dv_ref, ds_ref, ck_ref, cv_ref):
        i = pl.program_id(0)
        n = nb - 1 - i

        @pl.when(i == 0)
        def _():
            ck_ref[...] = jnp.zeros_like(ck_ref)
            cv_ref[...] = jnp.zeros_like(cv_ref)
            ds_ref[...] = jnp.zeros_like(ds_ref)

        mask = _attn_mask(n)
        for h in range(N_KV):
            hs = slice(h * HEAD_DIM, (h + 1) * HEAD_DIM)
            kk = jnp.concatenate([kp_ref[:, hs], kc_ref[:, hs]], axis=0).astype(BF16)
            vv = jnp.concatenate([vp_ref[:, hs], vc_ref[:, hs]], axis=0).astype(BF16)
            q = _stack_heads(q_ref, h)
            do = _stack_heads(do_ref, h)
            p, ps = _attn_probs(q, kk, mask, _stack_sinks(s_ref, h))
            dp = _dot(do, vv, 1, 1)
            delta = jnp.sum(p * dp, axis=-1, keepdims=True)
            dsc = (p * (dp - delta) * (HEAD_DIM ** -0.5)).astype(BF16)
            dsink = -ps * delta
            dq = _dot(dsc, kk, 1, 0).astype(BF16)
            for g in range(GROUP):
                hg = h * GROUP + g
                rows = slice(g * WINDOW, (g + 1) * WINDOW)
                ds_ref[hg:hg + 1, :] += jnp.broadcast_to(jnp.sum(dsink[rows], axis=0, keepdims=True), (1, 128))
                dq_ref[:, hg * HEAD_DIM:(hg + 1) * HEAD_DIM] = dq[rows]
            dkk = _dot(dsc, q, 0, 0)
            dvv = _dot(p.astype(BF16), do, 0, 0)
            dk_ref[:, hs] = (dkk[WINDOW:] + ck_ref[:, hs]).astype(BF16)
            dv_ref[:, hs] = (dvv[WINDOW:] + cv_ref[:, hs]).astype(BF16)
            ck_ref[:, hs] = dkk[:WINDOW]
            cv_ref[:, hs] = dvv[:WINDOW]

    q, kc, kp, vc, vp, sk = _attn_specs(nb, True)
    kv_out = pl.BlockSpec((WINDOW, 256), lambda n: (nb - 1 - n, 0))
    return pl.pallas_call(
        body, name="attn_bwd", grid=(nb,), in_specs=[q, kc, kp, vc, vp, sk, q],
        out_specs=[q, kv_out, kv_out, pl.BlockSpec((N_Q, 128), lambda n: (0, 0))],
        out_shape=[jax.ShapeDtypeStruct((T, D), BF16), jax.ShapeDtypeStruct((T, 256), BF16),
                   jax.ShapeDtypeStruct((T, 256), BF16), jax.ShapeDtypeStruct((N_Q, 128), F32)],
        scratch_shapes=[pltpu.VMEM((WINDOW, 256), F32), pltpu.VMEM((WINDOW, 256), F32)],
        compiler_params=_params(1),
    )(proj, proj, proj, proj, proj, sinks, d_o)


def _split3(x):
    hi = x.astype(BF16)
    r1 = x - hi.astype(F32)
    mid = r1.astype(BF16)
    lo = (r1 - mid.astype(F32)).astype(BF16)
    return hi, mid, lo


def _tri_sum(tri, x):
    hi, mid, lo = _split3(x)
    return _dot(tri, hi, 1, 0) + _dot(tri, mid, 1, 0) + _dot(tri, lo, 1, 0)


def _gla_chunk(q, k, lr, w2, b, lower):
    logit = _dot(lr.astype(BF16), w2.astype(BF16), 1, 0) + b
    la = (jnp.minimum(logit, 0.0) - jnp.log(1.0 + jnp.exp(-jnp.abs(logit)))) * (1.0 / GATE_NORM)
    g = _tri_sum(lower, la)
    gl = g[CHUNK - 1:CHUNK, :]
    eg = jnp.exp(g)
    qd = q * (DK ** -0.5) * eg
    ki = k * jnp.exp(-g)
    ke = k * jnp.exp(gl - g)
    return logit, g, gl, eg, qd, ki, ke


def _tri(lower):
    r = lax.broadcasted_iota(jnp.int32, (CHUNK, CHUNK), 0)
    c = lax.broadcasted_iota(jnp.int32, (CHUNK, CHUNK), 1)
    return (r >= c) if lower else (r <= c)


def _gla_rows(T):
    return _pick(T, (256, 128, 64))


def _gla_fwd(proj, proj_lr, w2p, gate_b):
    T = proj.shape[0]
    rb = _gla_rows(T)
    per = rb // CHUNK

    def body(q_ref, k_ref, v_ref, lr_ref, w2_ref, b_ref, o_ref, st_ref, s_scr):
        @pl.when(pl.program_id(1) == 0)
        def _():
            s_scr[...] = jnp.zeros_like(s_scr)

        low = _tri(True)
        lower = low.astype(BF16)
        for i in range(per):
            rows = slice(i * CHUNK, (i + 1) * CHUNK)
            _, g, gl, eg, qd, ki, ke = _gla_chunk(q_ref[rows, :], k_ref[rows, :], lr_ref[rows, :], w2_ref[...],
                                                  b_ref[...], lower)
            v = v_ref[rows, :].astype(BF16)
            qdb = qd.astype(BF16)
            att = jnp.where(low, _dot(qdb, ki.astype(BF16), 1, 1), 0.0)
            st = s_scr[...]
            st_ref[0, i] = st
            o_ref[rows, :] = _dot(att.astype(BF16), v, 1, 0) + _dot(qdb, st.astype(BF16), 1, 1)
            s_scr[...] = st * jnp.exp(gl) + _dot(v, ke.astype(BF16), 0, 0)

    return pl.pallas_call(
        body, name="gla_fwd", grid=(GLA_H, T // rb),
        in_specs=[pl.BlockSpec((rb, DK), lambda h, n: (n, O_GQ // DK + h)),
                  pl.BlockSpec((rb, DK), lambda h, n: (n, O_GK // DK + h)),
                  pl.BlockSpec((rb, DV), lambda h, n: (n, O_GV // DV + h)),
                  pl.BlockSpec((rb, LRP), lambda h, n: (n, 0)),
                  pl.BlockSpec((LRP, DK), lambda h, n: (0, h)),
                  pl.BlockSpec((1, DK), lambda h, n: (0, h))],
        out_specs=[pl.BlockSpec((rb, DV), lambda h, n: (n, h)),
                   pl.BlockSpec((1, per, DV, DK), lambda h, n: (h, n, 0, 0))],
        out_shape=[jax.ShapeDtypeStruct((T, GLA_H * DV), F32),
                   jax.ShapeDtypeStruct((GLA_H, T // CHUNK, DV, DK), F32)],
        scratch_shapes=[pltpu.VMEM((DV, DK), F32)], compiler_params=_params(2),
    )(proj, proj, proj, proj_lr, w2p, gate_b)


def _gla_bwd(proj, proj_lr, w2p, gate_b, states, d_o):
    T = proj.shape[0]
    rb = _gla_rows(T)
    per = rb // CHUNK
    nblk = T // rb

    def body(q_ref, k_ref, v_ref, lr_ref, w2_ref, b_ref, st_ref, do_ref, dq_ref, dk_ref, dv_ref, dl_ref, ds_scr):
        @pl.when(pl.program_id(1) == 0)
        def _():
            ds_scr[...] = jnp.zeros_like(ds_scr)

        low = _tri(True)
        lower = low.astype(BF16)
        upper = _tri(False).astype(BF16)
        for i in reversed(range(per)):
            rows = slice(i * CHUNK, (i + 1) * CHUNK)
            logit, g, gl, eg, qd, ki, ke = _gla_chunk(q_ref[rows, :], k_ref[rows, :], lr_ref[rows, :], w2_ref[...],
                                                      b_ref[...], lower)
            v = v_ref[rows, :].astype(BF16)
            do = do_ref[rows, :].astype(BF16)
            qdb, kib, keb = qd.astype(BF16), ki.astype(BF16), ke.astype(BF16)
            att = jnp.where(low, _dot(qdb, kib, 1, 1), 0.0).astype(BF16)
            sp = st_ref[0, i]
            ds = ds_scr[...]
            dsb = ds.astype(BF16)
            datt = jnp.where(low, _dot(do, v, 1, 1), 0.0).astype(BF16)
            dv_ref[rows, :] = (_dot(att, do, 0, 0) + _dot(keb, dsb, 1, 1)).astype(BF16)
            dqd = _dot(datt, kib, 1, 0) + _dot(do, sp.astype(BF16), 1, 0)
            dki = _dot(datt, qdb, 0, 0)
            dke = _dot(v, dsb, 1, 0)
            decay = jnp.exp(gl)
            ds_scr[...] = ds * decay + _dot(do, qdb, 0, 0)
            ddec = jnp.sum(ds * sp, axis=0, keepdims=True)
            dq_ref[rows, :] = (dqd * (DK ** -0.5) * eg).astype(BF16)
            dk_ref[rows, :] = (dki * jnp.exp(-g) + dke * jnp.exp(gl - g)).astype(BF16)
            dke_ke = dke * ke
            dg = dqd * qd - dki * ki - dke_ke
            dgl = jnp.sum(dke_ke, axis=0, keepdims=True) + ddec * decay
            dla = _tri_sum(upper, dg) + dgl
            dl_ref[rows, :] = dla * (1.0 / GATE_NORM) * (1.0 - _sigmoid(logit))

    def rev(n):
        return nblk - 1 - n

    qk_out = pl.BlockSpec((rb, DK), lambda h, n: (rev(n), h))
    return pl.pallas_call(
        body, name="gla_bwd", grid=(GLA_H, nblk),
        in_specs=[pl.BlockSpec((rb, DK), lambda h, n: (rev(n), O_GQ // DK + h)),
                  pl.BlockSpec((rb, DK), lambda h, n: (rev(n), O_GK // DK + h)),
                  pl.BlockSpec((rb, DV), lambda h, n: (rev(n), O_GV // DV + h)),
                  pl.BlockSpec((rb, LRP), lambda h, n: (rev(n), 0)),
                  pl.BlockSpec((LRP, DK), lambda h, n: (0, h)),
                  pl.BlockSpec((1, DK), lambda h, n: (0, h)),
                  pl.BlockSpec((1, per, DV, DK), lambda h, n: (h, rev(n), 0, 0)),
                  pl.BlockSpec((rb, DV), lambda h, n: (rev(n), h))],
        out_specs=[qk_out, qk_out, pl.BlockSpec((rb, DV), lambda h, n: (rev(n), h)), qk_out],
        out_shape=[jax.ShapeDtypeStruct((T, GLA_H * DK), BF16), jax.ShapeDtypeStruct((T, GLA_H * DK), BF16),
                   jax.ShapeDtypeStruct((T, GLA_H * DV), BF16), jax.ShapeDtypeStruct((T, GLA_H * DK), F32)],
        scratch_shapes=[pltpu.VMEM((DV, DK), F32)], compiler_params=_params(2),
    )(proj, proj, proj, proj_lr, w2p, gate_b, states, d_o)


def _local_step(x, target, p, reduce_part):
    u = _rms_fwd(x, p["norm1_w"], name="rms1_fwd")
    proj = _matmul(u, p["w_in_main"], name="mm_proj")
    proj_lr = _matmul(u, p["w_in_lr"], name="mm_proj_lr")
    attn_o = _attn_fwd(proj, p["sinks"])
    gla_raw, states = _gla_fwd(proj, proj_lr, p["w2p"], p["gate_b"])
    merged = _merge_fwd(attn_o, gla_raw, proj, p["gla_norm_w"])
    h1 = _matmul(merged, p["w_out"], add=x, name="mm_out")
    v2 = _rms_fwd(h1, p["norm2_w"], name="rms2_fwd")
    gate = _matmul(v2, p["w_gate"], b_slots=True, name="mm_gate")
    up = _matmul(v2, p["w_up"], b_slots=True, name="mm_up")
    ff = _swiglu_fwd(gate, up)
    h2 = _matmul(ff, p["w_down"], add=h1, name="mm_down")
    loss, dh2, g_final = _final_loss(h2, p["final_norm_w"], target)

    dff = _matmul(dh2, p["w_down"], tb=True, name="mm_dff")
    reduce_part("w_down", _matmul(ff, dh2, ta=True, out_dtype=BF16, name="mm_gdown").reshape(4, FFN // 4, D))
    dgate, dup = _swiglu_bwd(dff, gate, up)
    reduce_part("w_gate", _matmul(v2, dgate, ta=True, out_dtype=BF16, bj=FFN // 4, out_slots=True, name="mm_ggate"))
    reduce_part("w_up", _matmul(v2, dup, ta=True, out_dtype=BF16, bj=FFN // 4, out_slots=True, name="mm_gup"))
    dv2 = _matmul(dgate, p["w_gate"], tb=True, b_slots=True, name="mm_dv2a")
    dv2 = _matmul(dup, p["w_up"], tb=True, b_slots=True, add=dv2, name="mm_dv2b")
    dh1, g_norm2 = _rms_bwd(dv2, h1, p["norm2_w"], dh2, name="rms2_bwd")
    dmerged = _matmul(dh1, p["w_out"], tb=True, name="mm_dmerged")
    reduce_part("w_out", _matmul(merged, dh1, ta=True, out_dtype=BF16, name="mm_gout").reshape(4, D // 4, D))
    d_attn, d_gla, d_gr, d_ga, d_gb, g_gla_norm = _merge_bwd(dmerged, attn_o, gla_raw, proj, p["gla_norm_w"])
    d_gq, d_gk, d_gv, dlogit = _gla_bwd(proj, proj_lr, p["w2p"], p["gate_b"], states, d_gla)
    d_aq, d_ak, d_av, g_sinks = _attn_bwd(proj, p["sinks"], d_attn)
    g_gate_b = _colsum(dlogit, name="colsum_gate_b")
    g_w2 = _matmul(proj_lr, dlogit, ta=True, name="mm_gw2")
    dproj_lr = _matmul(dlogit, p["w2p"], tb=True, out_dtype=BF16, name="mm_dlr")
    dproj = jnp.concatenate([d_aq, d_ak, d_av, d_gq, d_gk, d_gv, d_gr, d_ga, d_gb], axis=1)
    g_in_main = _matmul(u, dproj, ta=True, out_dtype=BF16, name="mm_gin")
    g_in_lr = _matmul(u, dproj_lr, ta=True, out_dtype=BF16, name="mm_gin_lr")
    du = _matmul(dproj, p["w_in_main"], tb=True, name="mm_du")
    du = _matmul(dproj_lr, p["w_in_lr"], tb=True, add=du, name="mm_du_lr")
    dx, g_norm1 = _rms_bwd(du, x, p["norm1_w"], dh1, name="rms1_bwd")
    grads = dict(norm1_w=g_norm1, w_in_main=g_in_main, w_in_lr=g_in_lr, w2=g_w2[:RANK], gate_b=g_gate_b,
                 sinks=g_sinks[:, 0].reshape(1, N_Q), gla_norm_w=g_gla_norm, norm2_w=g_norm2, final_norm_w=g_final)
    return loss, dx, grads


def _place():
    x, y, c = lax.axis_index("x"), lax.axis_index("y"), lax.axis_index("c")
    chips = [(1 - x, y), (x, 1 - y), (1 - x, 1 - y)]
    return x, y, c, chips


def _hbm_shape(s, dt):
    return jax.ShapeDtypeStruct(s, dt)


def _handshake(peers):
    barrier = pltpu.get_barrier_semaphore()
    for peer in peers:
        pl.semaphore_signal(barrier, inc=1, device_id=peer, device_id_type=MESH)
    pl.semaphore_wait(barrier, len(peers))


def _launch_copies(body, args, out_shape, sems, *, name, collective_id=None):
    if collective_id is None:
        return pl.pallas_call(
            body, name=name, in_specs=[ANY] * len(args), out_specs=[ANY] * len(out_shape), out_shape=out_shape,
            scratch_shapes=sems)(*args)
    return pl.kernel(
        body, name=name, out_type=out_shape, mesh=plsc.ScalarSubcoreMesh(axis_name="sequencer", num_cores=1),
        scratch_types=sems, compiler_params=pltpu.CompilerParams(collective_id=collective_id))(*args)


def _gather_shards(shards, *, name, collective_id=None):
    n = len(shards)

    def body(*refs):
        ins, outs = refs[:n], refs[n:2 * n]
        ici_send, ici_recv, d2d_send, d2d_recv, local_sem = refs[2 * n:]
        x, y, c, chips = _place()
        me = 2 * x + y
        sibling = (x, y, 1 - c)
        if collective_id is not None:
            _handshake([sibling] + [(*chip, c) for chip in chips])

        def half(w, slot, hc):
            r2 = shards[w].shape[0] // 2
            return outs[w].at[slot, pl.ds(hc * r2, r2), :]

        locals_ = [pltpu.make_async_copy(ins[w], outs[w].at[me], local_sem.at[w]) for w in range(n)]
        for cp in locals_:
            cp.start()
        sends = []
        for w in range(n):
            r2 = shards[w].shape[0] // 2
            for j, chip in enumerate(chips):
                cp = pltpu.make_async_remote_copy(
                    src_ref=ins[w].at[pl.ds(c * r2, r2), :], dst_ref=half(w, me, c),
                    send_sem=ici_send.at[w * 3 + j], recv_sem=ici_recv.at[w * 3 + j],
                    device_id=(*chip, c), device_id_type=MESH)
                cp.start()
                sends.append(cp)
        for w in range(n):
            for j, chip in enumerate(chips):
                slot = 2 * chip[0] + chip[1]
                got = half(w, slot, c)
                pltpu.make_async_remote_copy(
                    src_ref=got, dst_ref=got, send_sem=ici_send.at[w * 3 + j], recv_sem=ici_recv.at[w * 3 + j],
                    device_id=(*chip, c), device_id_type=MESH).wait_recv()
                cp = pltpu.make_async_remote_copy(
                    src_ref=got, dst_ref=got, send_sem=d2d_send.at[w * 3 + j], recv_sem=d2d_recv.at[w * 3 + j],
                    device_id=sibling, device_id_type=MESH)
                cp.start()
                sends.append(cp)
        for w in range(n):
            for j, chip in enumerate(chips):
                slot = 2 * chip[0] + chip[1]
                got = half(w, slot, 1 - c)
                pltpu.make_async_remote_copy(
                    src_ref=got, dst_ref=got, send_sem=d2d_send.at[w * 3 + j], recv_sem=d2d_recv.at[w * 3 + j],
                    device_id=sibling, device_id_type=MESH).wait_recv()
        for cp in sends:
            cp.wait_send()
        for cp in locals_:
            cp.wait()

    return _launch_copies(
        body, shards, [_hbm_shape((4,) + s.shape, s.dtype) for s in shards],
        [pltpu.SemaphoreType.DMA((3 * n,)), pltpu.SemaphoreType.DMA((3 * n,)), pltpu.SemaphoreType.DMA((3 * n,)),
         pltpu.SemaphoreType.DMA((3 * n,)), pltpu.SemaphoreType.DMA((n,))],
        name=name, collective_id=collective_id)


def _pair_blocks(R2):
    return _pick(R2, (256, 352, 128, 64, 32, 16))


def _pair_reduce(part, *, name):
    _, R, C = part.shape
    R2 = R // 2
    rb = _pair_blocks(R2)
    nblk = R2 // rb
    steps = [(s, i) for s in range(4) for i in range(nblk)]
    n = len(steps)

    def body(part_ref, sums_ref, own_buf, oth_buf, rcv_buf, out_buf, own_sem, oth_sem, out_sem, send_sem, recv_sem):
        x, y, c, _ = _place()
        sibling = (x, y, 1 - c)

        def fetch(t, half, buf, sem):
            s, i = steps[t]
            return pltpu.make_async_copy(part_ref.at[s, pl.ds(half * R2 + i * rb, rb), :], buf.at[t % 2], sem.at[t % 2])

        def push(t):
            return pltpu.make_async_remote_copy(
                src_ref=oth_buf.at[t % 2], dst_ref=rcv_buf.at[t % 2], send_sem=send_sem.at[t % 2],
                recv_sem=recv_sem.at[t % 2], device_id=sibling, device_id_type=MESH)

        def store(t):
            s, i = steps[t]
            return pltpu.make_async_copy(out_buf.at[t % 2], sums_ref.at[s, pl.ds(i * rb, rb), :], out_sem.at[t % 2])

        fetch(0, c, own_buf, own_sem).start()
        fetch(0, 1 - c, oth_buf, oth_sem).start()
        for t in range(n):
            if t + 1 < n:
                fetch(t + 1, c, own_buf, own_sem).start()
                fetch(t + 1, 1 - c, oth_buf, oth_sem).start()
            fetch(t, 1 - c, oth_buf, oth_sem).wait()
            p = push(t)
            p.start()
            fetch(t, c, own_buf, own_sem).wait()
            p.wait()
            if t >= 2:
                store(t - 2).wait()
            out_buf[t % 2] = (own_buf[t % 2].astype(F32) + rcv_buf[t % 2].astype(F32)).astype(BF16)
            store(t).start()
        for t in range(max(n - 2, 0), n):
            store(t).wait()

    buf = pltpu.VMEM((2, rb, C), BF16)
    sem2 = pltpu.SemaphoreType.DMA((2,))
    return pl.pallas_call(
        body, name=name, in_specs=[ANY], out_specs=ANY, out_shape=_hbm_shape((4, R2, C), BF16),
        scratch_shapes=[buf, buf, buf, buf, sem2, sem2, sem2, sem2, sem2],
        compiler_params=pltpu.CompilerParams(vmem_limit_bytes=VMEM_LIMIT),
    )(part)


def _chip_exchange(sums, *, name, collective_id=None):
    n = len(sums)

    def body(*refs):
        ins, outs = refs[:n], refs[n:2 * n]
        send_sem, recv_sem, local_sem = refs[2 * n:]
        x, y, c, chips = _place()
        me = 2 * x + y
        if collective_id is not None:
            _handshake([(*chip, c) for chip in chips])
        cps = []
        for w in range(n):
            lc = pltpu.make_async_copy(ins[w].at[me], outs[w].at[me], local_sem.at[w])
            lc.start()
            cps.append(lc)
            for j, chip in enumerate(chips):
                slot = 2 * chip[0] + chip[1]
                rc = pltpu.make_async_remote_copy(
                    src_ref=ins[w].at[slot], dst_ref=outs[w].at[me],
                    send_sem=send_sem.at[w * 3 + j], recv_sem=recv_sem.at[w * 3 + j],
                    device_id=(*chip, c), device_id_type=MESH)
                rc.start()
                cps.append(rc)
        for w in range(n):
            for j, chip in enumerate(chips):
                slot = 2 * chip[0] + chip[1]
                pltpu.make_async_remote_copy(
                    src_ref=ins[w].at[slot], dst_ref=outs[w].at[slot],
                    send_sem=send_sem.at[w * 3 + j], recv_sem=recv_sem.at[w * 3 + j],
                    device_id=(*chip, c), device_id_type=MESH).wait_recv()
        for w in range(n):
            cps[w * 4].wait()
            for j in range(3):
                cps[w * 4 + 1 + j].wait_send()

    return _launch_copies(
        body, sums, [_hbm_shape(s.shape, s.dtype) for s in sums],
        [pltpu.SemaphoreType.DMA((3 * n,)), pltpu.SemaphoreType.DMA((3 * n,)), pltpu.SemaphoreType.DMA((n,))],
        name=name, collective_id=collective_id)


def _sum_join(recv, *, name):
    _, R2, C = recv.shape
    rb = _pair_blocks(R2)
    nblk = R2 // rb

    def body(recv_ref, out_ref, in_buf, acc_buf, in_sem, loc_sem, send_sem, recv_sem):
        x, y, c, _ = _place()
        sibling = (x, y, 1 - c)

        def fetch(t):
            return pltpu.make_async_copy(recv_ref.at[:, pl.ds(t * rb, rb), :], in_buf.at[t % 2], in_sem.at[t % 2])

        def rows(t, half):
            return out_ref.at[pl.ds(half * R2 + t * rb, rb), :]

        def put_local(t):
            return pltpu.make_async_copy(acc_buf.at[t % 2], rows(t, c), loc_sem.at[t])

        def put_remote(t, half):
            return pltpu.make_async_remote_copy(
                src_ref=acc_buf.at[t % 2], dst_ref=rows(t, half), send_sem=send_sem.at[t], recv_sem=recv_sem.at[t],
                device_id=sibling, device_id_type=MESH)

        fetch(0).start()
        for t in range(nblk):
            if t + 1 < nblk:
                fetch(t + 1).start()
            fetch(t).wait()
            if t >= 2:
                put_local(t - 2).wait()
                put_remote(t - 2, c).wait_send()
            acc = in_buf[t % 2, 0].astype(F32)
            for s in range(1, 4):
                acc = acc + in_buf[t % 2, s].astype(F32)
            acc_buf[t % 2] = acc
            put_local(t).start()
            put_remote(t, c).start()
        for t in range(max(nblk - 2, 0), nblk):
            put_local(t).wait()
            put_remote(t, c).wait_send()
        for t in range(nblk):
            put_remote(t, 1 - c).wait_recv()

    semn = pltpu.SemaphoreType.DMA((nblk,))
    return pl.pallas_call(
        body, name=name, in_specs=[ANY], out_specs=ANY, out_shape=_hbm_shape((2 * R2, C), F32),
        scratch_shapes=[pltpu.VMEM((2, 4, rb, C), BF16), pltpu.VMEM((2, rb, C), F32),
                        pltpu.SemaphoreType.DMA((2,)), semn, semn, semn],
        compiler_params=pltpu.CompilerParams(vmem_limit_bytes=VMEM_LIMIT),
    )(recv)


def _sum_small(pack):
    R, C = pack.shape

    def body(in_ref, out_ref, all_ref, send_sem, recv_sem):
        x, y, c, _ = _place()
        me = 4 * x + 2 * y + c
        all_ref[me] = in_ref[...]
        cps = []
        for k in range(1, 8):
            peer = (x ^ (k >> 2), y ^ ((k >> 1) & 1), c ^ (k & 1))
            cp = pltpu.make_async_remote_copy(
                src_ref=in_ref, dst_ref=all_ref.at[me], send_sem=send_sem.at[k - 1], recv_sem=recv_sem.at[k - 1],
                device_id=peer, device_id_type=MESH)
            cp.start()
            cps.append(cp)
        for k in range(1, 8):
            peer = (x ^ (k >> 2), y ^ ((k >> 1) & 1), c ^ (k & 1))
            slot = 4 * peer[0] + 2 * peer[1] + peer[2]
            pltpu.make_async_remote_copy(
                src_ref=in_ref, dst_ref=all_ref.at[slot], send_sem=send_sem.at[k - 1], recv_sem=recv_sem.at[k - 1],
                device_id=peer, device_id_type=MESH).wait_recv()
        for cp in cps:
            cp.wait_send()
        acc = all_ref[0]
        for d in range(1, 8):
            acc = acc + all_ref[d]
        out_ref[...] = acc

    return pl.pallas_call(
        body, name="sum_small", in_specs=[pl.BlockSpec(memory_space=pltpu.VMEM)],
        out_specs=pl.BlockSpec(memory_space=pltpu.VMEM), out_shape=jax.ShapeDtypeStruct((R, C), F32),
        scratch_shapes=[pltpu.VMEM((8, R, C), F32), pltpu.SemaphoreType.DMA((7,)), pltpu.SemaphoreType.DMA((7,))],
    )(pack)


def _adamw(w, g, m, v, *, name):
    R, C = w.shape
    rb = _pick(R, (128, 64, 32, 16, 8))
    c1 = 1.0 / (1.0 - B1 ** STEP)
    c2 = 1.0 / (1.0 - B2 ** STEP)

    def body(w_ref, g_ref, m_ref, v_ref, d_ref, nm_ref, nv_ref):
        gv = g_ref[...]
        m2 = B1 * m_ref[...] + (1.0 - B1) * gv
        v2 = B2 * v_ref[...] + (1.0 - B2) * gv * gv
        nm_ref[...] = m2
        nv_ref[...] = v2
        d_ref[...] = -LR * ((m2 * c1) / (jnp.sqrt(v2 * c2) + ADAM_EPS) + WD * w_ref[...])

    blk = pl.BlockSpec((rb, C), lambda i: (i, 0))
    return pl.pallas_call(
        body, name=name, grid=(R // rb,), in_specs=[blk] * 4, out_specs=[blk] * 3,
        out_shape=[jax.ShapeDtypeStruct((R, C), F32)] * 3, compiler_params=_params(1),
    )(w, g, m, v)


SMALL = (("norm1_w", D), ("norm2_w", D), ("final_norm_w", D), ("gate_b", GLA_H * DK), ("gla_norm_w", DV), ("sinks", N_Q))
PACK_W = 1024


def _pack_small(vals, w2, loss):
    rows = []
    for name, width in SMALL:
        v = vals[name].reshape(-1)
        rows.append(jnp.pad(v, (0, (-width) % PACK_W)).reshape(-1, PACK_W))
    rows.append(w2)
    rows.append(jnp.broadcast_to(loss.reshape(1, 1), (1, PACK_W)))
    pack = jnp.concatenate(rows, axis=0)
    return jnp.pad(pack, ((0, 32 - pack.shape[0]), (0, 0)))


def _unpack_small(pack):
    out, r = {}, 0
    for name, width in SMALL:
        nr = -(-width // PACK_W)
        out[name] = pack[r:r + nr].reshape(-1)[:width]
        r += nr
    out["w2"] = pack[r:r + RANK]
    out["loss"] = pack[r + RANK, 0]
    return out


def kernel(x, norm1_w, w_in, gla_gate_w2, gla_gate_b, attn_sinks, gla_norm_w, w_out, norm2_w, w_ffn_gate, w_ffn_up, w_ffn_down, final_norm_w, loss_target, m_norm1_w, m_w_in, m_gla_gate_w2, m_gla_gate_b, m_attn_sinks, m_gla_norm_w, m_w_out, m_norm2_w, m_w_ffn_gate, m_w_ffn_up, m_w_ffn_down, m_final_norm_w, v_norm1_w, v_w_in, v_gla_gate_w2, v_gla_gate_b, v_attn_sinks, v_gla_norm_w, v_w_out, v_norm2_w, v_w_ffn_gate, v_w_ffn_up, v_w_ffn_down, v_final_norm_w):
    chip = 2 * lax.axis_index("x") + lax.axis_index("y")
    w_in_s, w_out_s, w_gate_s, w_up_s, w_down_s, w2_s = (
        w_in[0], w_out[0], w_ffn_gate[0], w_ffn_up[0], w_ffn_down[0], gla_gate_w2[0])
    CS = w_in_s.shape[1]

    g_in, g_w2 = _gather_shards([w_in_s.astype(BF16), w2_s], name="gather_in")
    g_out, g_gate, g_up, g_down = _gather_shards(
        [w_out_s.astype(BF16), w_gate_s.astype(BF16), w_up_s.astype(BF16), w_down_s.astype(BF16)],
        name="gather_rest", collective_id=1)
    full_in = jnp.transpose(g_in, (1, 0, 2)).reshape(D, 4 * CS)
    w2_full = jnp.transpose(g_w2, (1, 0, 2)).reshape(RANK, GLA_H * DK)
    p = dict(
        norm1_w=norm1_w, norm2_w=norm2_w, final_norm_w=final_norm_w.reshape(1, D), gate_b=gla_gate_b,
        gla_norm_w=gla_norm_w, sinks=jnp.pad(attn_sinks, ((0, 0), (0, 128 - N_Q))),
        w_in_main=jnp.concatenate([full_in[:, :LR_AT], full_in[:, LR_AT + RANK:]], axis=1),
        w_in_lr=jnp.pad(full_in[:, LR_AT:LR_AT + RANK], ((0, 0), (0, LRP - RANK))),
        w2p=jnp.pad(w2_full, ((0, LRP - RANK), (0, 0))).astype(BF16),
        w_out=g_out.reshape(D, D),
        w_gate=g_gate, w_up=g_up, w_down=g_down.reshape(FFN, D),
    )

    tags = ["w_in", "w_out", "w_gate", "w_up", "w_down"]
    recv = {}

    def reduce_part(tag, part):
        sums = _pair_reduce(part, name="pair_reduce_" + tag)
        recv[tag], = _chip_exchange([sums], name="chip_exchange_" + tag, collective_id=2 + tags.index(tag))

    loss_blk, dx, g = _local_step(x[0], loss_target[0], p, reduce_part)
    gin_full = jnp.concatenate([g["w_in_main"][:, :LR_AT], g["w_in_lr"][:, :RANK], g["w_in_main"][:, LR_AT:]], axis=1)
    reduce_part("w_in", jnp.transpose(gin_full.reshape(D, 4, CS), (1, 0, 2)))
    big = [_sum_join(recv[tg], name="sum_join_" + tg) for tg in tags]

    small = _unpack_small(_sum_small(_pack_small(g, g["w2"], loss_blk[0, 0])))
    loss = small["loss"]
    g_w2_mine = lax.dynamic_slice_in_dim(small["w2"], chip * (GLA_H * DK // 4), GLA_H * DK // 4, axis=1)

    grads = dict(
        norm1_w=small["norm1_w"].reshape(1, D), w_in=big[0], gla_gate_w2=g_w2_mine,
        gla_gate_b=small["gate_b"].reshape(1, -1), attn_sinks=small["sinks"].reshape(1, N_Q),
        gla_norm_w=small["gla_norm_w"].reshape(1, DV), w_out=big[1], norm2_w=small["norm2_w"].reshape(1, D),
        w_ffn_gate=big[2], w_ffn_up=big[3], w_ffn_down=big[4], final_norm_w=small["final_norm_w"].reshape(1, D))
    weights = dict(
        norm1_w=(norm1_w, m_norm1_w, v_norm1_w), w_in=(w_in_s, m_w_in[0], v_w_in[0]),
        gla_gate_w2=(w2_s, m_gla_gate_w2[0], v_gla_gate_w2[0]), gla_gate_b=(gla_gate_b, m_gla_gate_b, v_gla_gate_b),
        attn_sinks=(attn_sinks, m_attn_sinks, v_attn_sinks), gla_norm_w=(gla_norm_w, m_gla_norm_w, v_gla_norm_w),
        w_out=(w_out_s, m_w_out[0], v_w_out[0]), norm2_w=(norm2_w, m_norm2_w, v_norm2_w),
        w_ffn_gate=(w_gate_s, m_w_ffn_gate[0], v_w_ffn_gate[0]), w_ffn_up=(w_up_s, m_w_ffn_up[0], v_w_ffn_up[0]),
        w_ffn_down=(w_down_s, m_w_ffn_down[0], v_w_ffn_down[0]),
        final_norm_w=(final_norm_w.reshape(1, D), m_final_norm_w.reshape(1, D), v_final_norm_w.reshape(1, D)))
    names = ["norm1_w", "w_in", "gla_gate_w2", "gla_gate_b", "attn_sinks", "gla_norm_w", "w_out", "norm2_w",
             "w_ffn_gate", "w_ffn_up", "w_ffn_down", "final_norm_w"]
    lead = {"norm1_w": False, "gla_gate_b": False, "attn_sinks": False, "gla_norm_w": False, "norm2_w": False}
    g_out_l, d_out, m_out, v_out = [], [], [], []
    for nm in names:
        w, m, v = weights[nm]
        gr = grads[nm]
        dl, nmn, nvn = _adamw(w, gr, m, v, name="adamw_" + nm)
        if nm == "final_norm_w":
            shape = (D,)
        elif nm in lead:
            shape = w.shape
        else:
            shape = (1,) + w.shape
        g_out_l.append(gr.reshape(shape))
        d_out.append(dl.reshape(shape))
        m_out.append(nmn.reshape(shape))
        v_out.append(nvn.reshape(shape))
    return (loss, dx[None], *g_out_l, *d_out, *m_out, *v_out)
```

```python
import functools

import jax
import jax.numpy as jnp
from jax import lax
from jax.experimental import pallas as pl
from jax.experimental.pallas import tpu as pltpu
from jax.experimental.pallas import tpu_sc as plsc

F32 = jnp.float32
BF16 = jnp.bfloat16

D = 2048
HEAD_DIM = 64
N_Q = 32
N_KV = 4
GROUP = 8
WINDOW = 128
GLA_H = 4
DK = 256
DV = 512
RANK = 16
CHUNK = 64
FFN = 5632
EPS = 1e-6
MASK_VALUE = -1e30
GATE_NORM = 16.0
LR, B1, B2, ADAM_EPS, WD, STEP = 0.001, 0.9, 0.999, 1e-08, 0.01, 10

MAIN = 12800
O_MERGE, O_GLA, O_ATTN = 0, 6144, 10240
W_MERGE, W_GLA, W_ATTN = 3 * DV, 2 * DK + DV, D + 2 * N_KV * HEAD_DIM
LR_AT = 6656
LRP = 128


def _main_pieces():
    o_gq, o_gk, o_gv, o_gr, o_ga, o_gb = 2560, 3584, 4608, LR_AT + RANK, LR_AT + RANK + D, LR_AT + RANK + 2 * D
    pieces = []
    for h in range(GLA_H):
        pieces += [(o_gr + DV * h, DV), (o_ga + DV * h, DV), (o_gb + DV * h, DV)]
    for h in range(GLA_H):
        pieces += [(o_gq + DK * h, DK), (o_gk + DK * h, DK), (o_gv + DV * h, DV)]
    pieces.append((0, W_ATTN))
    return pieces


def _to_main(full):
    main = jnp.concatenate([full[..., a:a + w] for a, w in _main_pieces()], axis=-1)
    return main, full[..., LR_AT:LR_AT + RANK]


def _from_main(main, lr):
    at, where = 0, []
    for a, w in _main_pieces():
        where.append((a, main[..., at:at + w]))
        at += w
    where.append((LR_AT, lr))
    return jnp.concatenate([piece for _, piece in sorted(where, key=lambda t: t[0])], axis=-1)

VMEM_LIMIT = 56 * 1024 * 1024
MESH = pl.DeviceIdType.MESH
ANY = pl.BlockSpec(memory_space=pl.ANY)


def _params(ngrid):
    return pltpu.CompilerParams(dimension_semantics=("arbitrary",) * ngrid, vmem_limit_bytes=VMEM_LIMIT)


def _pick(n, cands):
    for c in cands:
        if n % c == 0:
            return c
    return n


def _sigmoid(x):
    return 1.0 / (1.0 + jnp.exp(-x))


def _dot(a, b, ca, cb):
    return lax.dot_general(a, b, (((ca,), (cb,)), ((), ())), preferred_element_type=F32)


def _matmul(a, b, *, name, ta=False, tb=False, add=None, out_dtype=F32, bm=None, bj=None, bc=None,
            b_slots=False, out_slots=False):
    C, M = a.shape if ta else a.shape[::-1]
    if b_slots:
        if tb:
            J, bc = b.shape[1], b.shape[2]
            assert b.shape[0] * bc == C
        else:
            bj = b.shape[2]
            J = b.shape[0] * bj
            assert b.shape[1] == C
    else:
        J = b.shape[0] if tb else b.shape[1]
        assert (b.shape[1] if tb else b.shape[0]) == C
    bm = bm or _pick(M, (1024, 512, 256, 128) if C <= FFN else (512, 256, 128))
    bj = bj or _pick(J, (512, 256, 128))
    bc = bc or (C if C <= FFN else _pick(C, (6400,)))
    nm, nj, nc = M // bm, J // bj, C // bc
    has_add = add is not None

    def body(*refs):
        a_ref, b_ref = refs[0], refs[1]
        add_ref = refs[2] if has_add else None
        o_ref = refs[3] if has_add else refs[2]
        p = _dot(a_ref[...].astype(BF16), b_ref[...].astype(BF16), 0 if ta else 1, 1 if tb else 0)

        def finish(acc):
            if has_add:
                acc = acc + add_ref[...]
            o_ref[...] = acc.astype(o_ref.dtype)

        if nc == 1:
            finish(p)
        else:
            acc_ref = refs[-1]
            c = pl.program_id(2)

            @pl.when(c == 0)
            def _():
                acc_ref[...] = p

            @pl.when(c > 0)
            def _():
                acc_ref[...] += p

            @pl.when(c == nc - 1)
            def _():
                finish(acc_ref[...])

    a_spec = pl.BlockSpec((bc, bm), lambda m, j, c: (c, m)) if ta else pl.BlockSpec((bm, bc), lambda m, j, c: (m, c))
    if b_slots:
        b_spec = (pl.BlockSpec((None, bj, bc), lambda m, j, c: (c, j, 0)) if tb
                  else pl.BlockSpec((None, bc, bj), lambda m, j, c: (j, c, 0)))
    else:
        b_spec = (pl.BlockSpec((bj, bc), lambda m, j, c: (j, c)) if tb
                  else pl.BlockSpec((bc, bj), lambda m, j, c: (c, j)))
    if out_slots:
        assert not has_add
        o_spec = pl.BlockSpec((None, bm, bj), lambda m, j, c: (j, m, 0))
        out_shape = jax.ShapeDtypeStruct((nj, M, bj), out_dtype)
    else:
        o_spec = pl.BlockSpec((bm, bj), lambda m, j, c: (m, j))
        out_shape = jax.ShapeDtypeStruct((M, J), out_dtype)
    in_specs = [a_spec, b_spec] + ([o_spec] if has_add else [])
    args = (a, b) + ((add,) if has_add else ())
    return pl.pallas_call(
        body, name=name, grid=(nm, nj, nc), in_specs=in_specs, out_specs=o_spec,
        out_shape=out_shape,
        scratch_shapes=[pltpu.VMEM((bm, bj), F32)] if nc > 1 else [],
        compiler_params=_params(3),
    )(*args)


def _rows(T):
    return _pick(T, (256, 128))


def _rms_fwd(x, w, *, name):
    T = x.shape[0]
    rb = _rows(T)

    def body(x_ref, w_ref, u_ref):
        xv = x_ref[...]
        r = lax.rsqrt(jnp.mean(xv * xv, axis=-1, keepdims=True) + EPS)
        u_ref[...] = (xv * r * w_ref[...]).astype(BF16)

    return pl.pallas_call(
        body, name=name, grid=(T // rb,),
        in_specs=[pl.BlockSpec((rb, D), lambda i: (i, 0)), pl.BlockSpec((1, D), lambda i: (0, 0))],
        out_specs=pl.BlockSpec((rb, D), lambda i: (i, 0)),
        out_shape=jax.ShapeDtypeStruct((T, D), BF16), compiler_params=_params(1),
    )(x, w)


def _rms_bwd(dy, h, w, resid, *, name):
    T = h.shape[0]
    rb = _rows(T)

    def body(dy_ref, h_ref, w_ref, res_ref, dh_ref, dw_ref):
        @pl.when(pl.program_id(0) == 0)
        def _():
            dw_ref[...] = jnp.zeros_like(dw_ref)

        hv = h_ref[...]
        r = lax.rsqrt(jnp.mean(hv * hv, axis=-1, keepdims=True) + EPS)
        hn = hv * r
        dyv = dy_ref[...]
        dw_ref[...] += jnp.sum(dyv * hn, axis=0, keepdims=True)
        t = dyv * w_ref[...]
        dh_ref[...] = res_ref[...] + r * (t - hn * jnp.mean(t * hn, axis=-1, keepdims=True))

    blk = pl.BlockSpec((rb, D), lambda i: (i, 0))
    vec = pl.BlockSpec((1, D), lambda i: (0, 0))
    return pl.pallas_call(
        body, name=name, grid=(T // rb,), in_specs=[blk, blk, vec, blk], out_specs=[blk, vec],
        out_shape=[jax.ShapeDtypeStruct((T, D), F32), jax.ShapeDtypeStruct((1, D), F32)], compiler_params=_params(1),
    )(dy, h, w, resid)


def _final_loss(h, w, target):
    T = h.shape[0]
    rb = _rows(T)

    def body(h_ref, w_ref, t_ref, loss_ref, dh_ref, dw_ref):
        @pl.when(pl.program_id(0) == 0)
        def _():
            dw_ref[...] = jnp.zeros_like(dw_ref)
            loss_ref[...] = jnp.zeros_like(loss_ref)

        hv = h_ref[...]
        wv = w_ref[...]
        r = lax.rsqrt(jnp.mean(hv * hv, axis=-1, keepdims=True) + EPS)
        hn = hv * r
        e = hn * wv - t_ref[...]
        row = jnp.sum(e * e, axis=-1, keepdims=True) * (0.5 / D)
        loss_ref[...] += jnp.broadcast_to(jnp.sum(row, axis=0, keepdims=True), loss_ref.shape)
        dy = e * (1.0 / D)
        dw_ref[...] += jnp.sum(dy * hn, axis=0, keepdims=True)
        t = dy * wv
        dh_ref[...] = r * (t - hn * jnp.mean(t * hn, axis=-1, keepdims=True))

    blk = pl.BlockSpec((rb, D), lambda i: (i, 0))
    vec = pl.BlockSpec((1, D), lambda i: (0, 0))
    return pl.pallas_call(
        body, name="final_loss", grid=(T // rb,), in_specs=[blk, vec, blk],
        out_specs=[pl.BlockSpec((8, 128), lambda i: (0, 0)), blk, vec],
        out_shape=[jax.ShapeDtypeStruct((8, 128), F32), jax.ShapeDtypeStruct((T, D), F32),
                   jax.ShapeDtypeStruct((1, D), F32)],
        compiler_params=_params(1),
    )(h, w, target)


def _swiglu_fwd(gate, up):
    T = gate.shape[0]
    rb = _rows(T)

    def body(g_ref, u_ref, o_ref):
        g = g_ref[...]
        o_ref[...] = (g * _sigmoid(g) * u_ref[...]).astype(BF16)

    blk = pl.BlockSpec((rb, 512), lambda i, j: (i, j))
    return pl.pallas_call(
        body, name="swiglu_fwd", grid=(T // rb, FFN // 512), in_specs=[blk, blk], out_specs=blk,
        out_shape=jax.ShapeDtypeStruct((T, FFN), BF16), compiler_params=_params(2),
    )(gate, up)


def _swiglu_bwd(dff, gate, up):
    T = gate.shape[0]
    rb = _rows(T)

    def body(d_ref, g_ref, u_ref, dg_ref, du_ref):
        g = g_ref[...]
        d = d_ref[...]
        sg = _sigmoid(g)
        du_ref[...] = (d * g * sg).astype(BF16)
        dg_ref[...] = (d * u_ref[...] * sg * (1.0 + g * (1.0 - sg))).astype(BF16)

    blk = pl.BlockSpec((rb, 512), lambda i, j: (i, j))
    return pl.pallas_call(
        body, name="swiglu_bwd", grid=(T // rb, FFN // 512), in_specs=[blk, blk, blk], out_specs=[blk, blk],
        out_shape=[jax.ShapeDtypeStruct((T, FFN), BF16)] * 2, compiler_params=_params(2),
    )(dff, gate, up)


def _colsum(x, *, name):
    T, W = x.shape
    rb = _rows(T)

    def body(x_ref, o_ref):
        @pl.when(pl.program_id(0) == 0)
        def _():
            o_ref[...] = jnp.zeros_like(o_ref)

        o_ref[...] += jnp.sum(x_ref[...], axis=0, keepdims=True)

    return pl.pallas_call(
        body, name=name, grid=(T // rb,), in_specs=[pl.BlockSpec((rb, W), lambda i: (i, 0))],
        out_specs=pl.BlockSpec((1, W), lambda i: (0, 0)), out_shape=jax.ShapeDtypeStruct((1, W), F32),
        compiler_params=_params(1),
    )(x)


def _merge_fwd(attn_o, gla_raw, proj, gla_norm_w):
    T = attn_o.shape[0]
    rb = _rows(T)

    def body(a_ref, g_ref, gr_ref, ga_ref, gb_ref, w_ref, o_ref):
        graw = g_ref[...]
        r = lax.rsqrt(jnp.mean(graw * graw, axis=-1, keepdims=True) + EPS)
        gr = gr_ref[...]
        go = graw * r * w_ref[...] * (gr * _sigmoid(gr))
        o_ref[...] = (_sigmoid(ga_ref[...]) * a_ref[...] + _sigmoid(gb_ref[...]) * go).astype(BF16)

    def sec(k):
        return pl.BlockSpec((rb, DV), lambda i, h: (i, O_MERGE // DV + 3 * h + k))

    blk = pl.BlockSpec((rb, DV), lambda i, h: (i, h))
    return pl.pallas_call(
        body, name="merge_fwd", grid=(T // rb, GLA_H),
        in_specs=[blk, blk, sec(0), sec(1), sec(2), pl.BlockSpec((1, DV), lambda i, h: (0, 0))],
        out_specs=blk, out_shape=jax.ShapeDtypeStruct((T, D), BF16), compiler_params=_params(2),
    )(attn_o, gla_raw, proj, proj, proj, gla_norm_w)


def _merge_bwd(dm, attn_o, gla_raw, proj, gla_norm_w):
    T = attn_o.shape[0]
    rb = _rows(T)

    def body(dm_ref, a_ref, g_ref, gr_ref, ga_ref, gb_ref, w_ref, da_ref, dg_ref, dp_ref, dw_ref):
        @pl.when((pl.program_id(0) == 0) & (pl.program_id(1) == 0))
        def _():
            dw_ref[...] = jnp.zeros_like(dw_ref)

        dmv = dm_ref[...]
        av = a_ref[...]
        graw = g_ref[...]
        gr = gr_ref[...]
        wv = w_ref[...]
        sa = _sigmoid(ga_ref[...])
        sb = _sigmoid(gb_ref[...])
        r = lax.rsqrt(jnp.mean(graw * graw, axis=-1, keepdims=True) + EPS)
        gnh = graw * r
        gn = gnh * wv
        sr = _sigmoid(gr)
        sl = gr * sr
        go = gn * sl
        da_ref[...] = dmv * sa
        dgo = dmv * sb
        dp_ref[:, 0:DV] = (dgo * gn * sr * (1.0 + gr * (1.0 - sr))).astype(BF16)
        dp_ref[:, DV:2 * DV] = (dmv * av * sa * (1.0 - sa)).astype(BF16)
        dp_ref[:, 2 * DV:3 * DV] = (dmv * go * sb * (1.0 - sb)).astype(BF16)
        dgn = dgo * sl
        dw_ref[...] += jnp.sum(dgn * gnh, axis=0, keepdims=True)
        t = dgn * wv
        dg_ref[...] = r * (t - gnh * jnp.mean(t * gnh, axis=-1, keepdims=True))

    def sec(k):
        return pl.BlockSpec((rb, DV), lambda i, h: (i, O_MERGE // DV + 3 * h + k))

    blk = pl.BlockSpec((rb, DV), lambda i, h: (i, h))
    vec = pl.BlockSpec((1, DV), lambda i, h: (0, 0))
    return pl.pallas_call(
        body, name="merge_bwd", grid=(T // rb, GLA_H),
        in_specs=[blk, blk, blk, sec(0), sec(1), sec(2), vec],
        out_specs=[blk, blk, pl.BlockSpec((rb, W_MERGE), lambda i, h: (i, O_MERGE // W_MERGE + h)), vec],
        out_shape=[jax.ShapeDtypeStruct((T, D), F32), jax.ShapeDtypeStruct((T, D), F32),
                   jax.ShapeDtypeStruct((T, MAIN), BF16), jax.ShapeDtypeStruct((1, DV), F32)],
        compiler_params=_params(2),
    )(dm, attn_o, gla_raw, proj, proj, proj, gla_norm_w)


def _attn_mask(n):
    qi = lax.broadcasted_iota(jnp.int32, (GROUP * WINDOW, 2 * WINDOW), 0) & (WINDOW - 1)
    ki = lax.broadcasted_iota(jnp.int32, (GROUP * WINDOW, 2 * WINDOW), 1)
    rel = qi + WINDOW - ki
    return (rel >= 0) & (rel < WINDOW) & ((n > 0) | (ki >= WINDOW))


def _stack_heads(ref, h):
    return jnp.concatenate(
        [ref[:, (h * GROUP + g) * HEAD_DIM:(h * GROUP + g + 1) * HEAD_DIM] for g in range(GROUP)], axis=0).astype(BF16)


def _stack_sinks(s_ref, h):
    return jnp.concatenate(
        [jnp.broadcast_to(s_ref[:, h * GROUP + g:h * GROUP + g + 1], (WINDOW, 1)) for g in range(GROUP)], axis=0)


def _attn_probs(q, kk, mask, sink):
    s = _dot(q, kk, 1, 1) * (HEAD_DIM ** -0.5)
    s = jnp.where(mask, s, MASK_VALUE)
    m = jnp.maximum(jnp.max(s, axis=-1, keepdims=True), sink)
    e = jnp.exp(s - m)
    es = jnp.exp(sink - m)
    inv = 1.0 / (jnp.sum(e, axis=-1, keepdims=True) + es)
    return e * inv, es * inv


def _attn_specs(nb, rev):
    def at(n):
        return (nb - 1 - n) if rev else n

    kcol, vcol = (O_ATTN + D) // 256, (O_ATTN + D) // 256 + 1
    q = pl.BlockSpec((WINDOW, D), lambda n: (at(n), O_ATTN // D))
    kc = pl.BlockSpec((WINDOW, 256), lambda n: (at(n), kcol))
    kp = pl.BlockSpec((WINDOW, 256), lambda n: (jnp.maximum(at(n) - 1, 0), kcol))
    vc = pl.BlockSpec((WINDOW, 256), lambda n: (at(n), vcol))
    vp = pl.BlockSpec((WINDOW, 256), lambda n: (jnp.maximum(at(n) - 1, 0), vcol))
    sk = pl.BlockSpec((1, 128), lambda n: (0, 0))
    o = pl.BlockSpec((WINDOW, D), lambda n: (at(n), 0))
    return q, kc, kp, vc, vp, sk, o


def _attn_fwd(proj, sinks):
    T = proj.shape[0]
    nb = T // WINDOW

    def body(q_ref, kc_ref, kp_ref, vc_ref, vp_ref, s_ref, o_ref):
        mask = _attn_mask(pl.program_id(0))
        for h in range(N_KV):
            hs = slice(h * HEAD_DIM, (h + 1) * HEAD_DIM)
            kk = jnp.concatenate([kp_ref[:, hs], kc_ref[:, hs]], axis=0).astype(BF16)
            vv = jnp.concatenate([vp_ref[:, hs], vc_ref[:, hs]], axis=0).astype(BF16)
            p, _ = _attn_probs(_stack_heads(q_ref, h), kk, mask, _stack_sinks(s_ref, h))
            o = _dot(p.astype(BF16), vv, 1, 0)
            for g in range(GROUP):
                hg = h * GROUP + g
                o_ref[:, hg * HEAD_DIM:(hg + 1) * HEAD_DIM] = o[g * WINDOW:(g + 1) * WINDOW]

    q, kc, kp, vc, vp, sk, o = _attn_specs(nb, False)
    return pl.pallas_call(
        body, name="attn_fwd", grid=(nb,), in_specs=[q, kc, kp, vc, vp, sk], out_specs=o,
        out_shape=jax.ShapeDtypeStruct((T, D), F32), compiler_params=_params(1),
    )(proj, proj, proj, proj, proj, sinks)


def _attn_bwd(proj, sinks, d_o, dproj):
    T = proj.shape[0]
    nb = T // WINDOW
    kat, vat = D, D + N_KV * HEAD_DIM

    def body(q_ref, kc_ref, kp_ref, vc_ref, vp_ref, s_ref, do_ref, _, dp_ref, ds_ref, ck_ref, cv_ref):
        i = pl.program_id(0)
        n = nb - 1 - i

        @pl.when(i == 0)
        def _():
            ck_ref[...] = jnp.zeros_like(ck_ref)
            cv_ref[...] = jnp.zeros_like(cv_ref)
            ds_ref[...] = jnp.zeros_like(ds_ref)

        mask = _attn_mask(n)
        for h in range(N_KV):
            hs = slice(h * HEAD_DIM, (h + 1) * HEAD_DIM)
            kk = jnp.concatenate([kp_ref[:, hs], kc_ref[:, hs]], axis=0).astype(BF16)
            vv = jnp.concatenate([vp_ref[:, hs], vc_ref[:, hs]], axis=0).astype(BF16)
            q = _stack_heads(q_ref, h)
            do = _stack_heads(do_ref, h)
            p, ps = _attn_probs(q, kk, mask, _stack_sinks(s_ref, h))
            dp = _dot(do, vv, 1, 1)
            delta = jnp.sum(p * dp, axis=-1, keepdims=True)
            dsc = (p * (dp - delta) * (HEAD_DIM ** -0.5)).astype(BF16)
            dsink = -ps * delta
            dq = _dot(dsc, kk, 1, 0).astype(BF16)
            for g in range(GROUP):
                hg = h * GROUP + g
                rows = slice(g * WINDOW, (g + 1) * WINDOW)
                ds_ref[hg:hg + 1, :] += jnp.broadcast_to(jnp.sum(dsink[rows], axis=0, keepdims=True), (1, 128))
                dp_ref[:, hg * HEAD_DIM:(hg + 1) * HEAD_DIM] = dq[rows]
            dkk = _dot(dsc, q, 0, 0)
            dvv = _dot(p.astype(BF16), do, 0, 0)
            dp_ref[:, kat + h * HEAD_DIM:kat + (h + 1) * HEAD_DIM] = (dkk[WINDOW:] + ck_ref[:, hs]).astype(BF16)
            dp_ref[:, vat + h * HEAD_DIM:vat + (h + 1) * HEAD_DIM] = (dvv[WINDOW:] + cv_ref[:, hs]).astype(BF16)
            ck_ref[:, hs] = dkk[:WINDOW]
            cv_ref[:, hs] = dvv[:WINDOW]

    q, kc, kp, vc, vp, sk, o = _attn_specs(nb, True)
    return pl.pallas_call(
        body, name="attn_bwd", grid=(nb,), in_specs=[q, kc, kp, vc, vp, sk, o, ANY],
        out_specs=[pl.BlockSpec((WINDOW, W_ATTN), lambda n: (nb - 1 - n, O_ATTN // W_ATTN)),
                   pl.BlockSpec((N_Q, 128), lambda n: (0, 0))],
        out_shape=[jax.ShapeDtypeStruct((T, MAIN), BF16), jax.ShapeDtypeStruct((N_Q, 128), F32)],
        scratch_shapes=[pltpu.VMEM((WINDOW, 256), F32), pltpu.VMEM((WINDOW, 256), F32)],
        input_output_aliases={7: 0}, compiler_params=_params(1),
    )(proj, proj, proj, proj, proj, sinks, d_o, dproj)


def _split3(x):
    hi = x.astype(BF16)
    r1 = x - hi.astype(F32)
    mid = r1.astype(BF16)
    lo = (r1 - mid.astype(F32)).astype(BF16)
    return hi, mid, lo


def _tri_sum(tri, x):
    hi, mid, lo = _split3(x)
    return _dot(tri, hi, 1, 0) + _dot(tri, mid, 1, 0) + _dot(tri, lo, 1, 0)


def _gla_chunk(q, k, lr, w2, b, lower):
    logit = _dot(lr.astype(BF16), w2.astype(BF16), 1, 0) + b
    la = (jnp.minimum(logit, 0.0) - jnp.log(1.0 + jnp.exp(-jnp.abs(logit)))) * (1.0 / GATE_NORM)
    g = _tri_sum(lower, la)
    gl = g[CHUNK - 1:CHUNK, :]
    eg = jnp.exp(g)
    qd = q * (DK ** -0.5) * eg
    ki = k * jnp.exp(-g)
    ke = k * jnp.exp(gl - g)
    return logit, g, gl, eg, qd, ki, ke


def _tri(lower):
    r = lax.broadcasted_iota(jnp.int32, (CHUNK, CHUNK), 0)
    c = lax.broadcasted_iota(jnp.int32, (CHUNK, CHUNK), 1)
    return (r >= c) if lower else (r <= c)


def _gla_rows(T):
    return _pick(T, (256, 128, 64))


def _gla_fwd(proj, proj_lr, w2p, gate_b):
    T = proj.shape[0]
    rb = _gla_rows(T)
    per = rb // CHUNK

    def body(q_ref, k_ref, v_ref, lr_ref, w2_ref, b_ref, o_ref, st_ref, s_scr):
        @pl.when(pl.program_id(1) == 0)
        def _():
            s_scr[...] = jnp.zeros_like(s_scr)

        low = _tri(True)
        lower = low.astype(BF16)
        for i in range(per):
            rows = slice(i * CHUNK, (i + 1) * CHUNK)
            _, g, gl, eg, qd, ki, ke = _gla_chunk(q_ref[rows, :], k_ref[rows, :], lr_ref[rows, :], w2_ref[...],
                                                  b_ref[...], lower)
            v = v_ref[rows, :].astype(BF16)
            qdb = qd.astype(BF16)
            att = jnp.where(low, _dot(qdb, ki.astype(BF16), 1, 1), 0.0)
            st = s_scr[...]
            st_ref[0, i] = st
            o_ref[rows, :] = _dot(att.astype(BF16), v, 1, 0) + _dot(qdb, st.astype(BF16), 1, 1)
            s_scr[...] = st * jnp.exp(gl) + _dot(v, ke.astype(BF16), 0, 0)

    return pl.pallas_call(
        body, name="gla_fwd", grid=(GLA_H, T // rb),
        in_specs=[pl.BlockSpec((rb, DK), lambda h, n: (n, (O_GLA + W_GLA * h) // DK)),
                  pl.BlockSpec((rb, DK), lambda h, n: (n, (O_GLA + W_GLA * h) // DK + 1)),
                  pl.BlockSpec((rb, DV), lambda h, n: (n, (O_GLA + W_GLA * h) // DV + 1)),
                  pl.BlockSpec((rb, LRP), lambda h, n: (n, 0)),
                  pl.BlockSpec((LRP, DK), lambda h, n: (0, h)),
                  pl.BlockSpec((1, DK), lambda h, n: (0, h))],
        out_specs=[pl.BlockSpec((rb, DV), lambda h, n: (n, h)),
                   pl.BlockSpec((1, per, DV, DK), lambda h, n: (h, n, 0, 0))],
        out_shape=[jax.ShapeDtypeStruct((T, GLA_H * DV), F32),
                   jax.ShapeDtypeStruct((GLA_H, T // CHUNK, DV, DK), F32)],
        scratch_shapes=[pltpu.VMEM((DV, DK), F32)], compiler_params=_params(2),
    )(proj, proj, proj, proj_lr, w2p, gate_b)


def _gla_bwd(proj, proj_lr, w2p, gate_b, states, d_o, dproj):
    T = proj.shape[0]
    rb = _gla_rows(T)
    per = rb // CHUNK
    nblk = T // rb

    def body(q_ref, k_ref, v_ref, lr_ref, w2_ref, b_ref, st_ref, do_ref, _, dp_ref, dl_ref, ds_scr):
        @pl.when(pl.program_id(1) == 0)
        def _():
            ds_scr[...] = jnp.zeros_like(ds_scr)

        low = _tri(True)
        lower = low.astype(BF16)
        upper = _tri(False).astype(BF16)
        for i in reversed(range(per)):
            rows = slice(i * CHUNK, (i + 1) * CHUNK)
            logit, g, gl, eg, qd, ki, ke = _gla_chunk(q_ref[rows, :], k_ref[rows, :], lr_ref[rows, :], w2_ref[...],
                                                      b_ref[...], lower)
            v = v_ref[rows, :].astype(BF16)
            do = do_ref[rows, :].astype(BF16)
            qdb, kib, keb = qd.astype(BF16), ki.astype(BF16), ke.astype(BF16)
            att = jnp.where(low, _dot(qdb, kib, 1, 1), 0.0).astype(BF16)
            sp = st_ref[0, i]
            ds = ds_scr[...]
            dsb = ds.astype(BF16)
            datt = jnp.where(low, _dot(do, v, 1, 1), 0.0).astype(BF16)
            dp_ref[rows, 2 * DK:] = (_dot(att, do, 0, 0) + _dot(keb, dsb, 1, 1)).astype(BF16)
            dqd = _dot(datt, kib, 1, 0) + _dot(do, sp.astype(BF16), 1, 0)
            dki = _dot(datt, qdb, 0, 0)
            dke = _dot(v, dsb, 1, 0)
            decay = jnp.exp(gl)
            ds_scr[...] = ds * decay + _dot(do, qdb, 0, 0)
            ddec = jnp.sum(ds * sp, axis=0, keepdims=True)
            dp_ref[rows, 0:DK] = (dqd * (DK ** -0.5) * eg).astype(BF16)
            dp_ref[rows, DK:2 * DK] = (dki * jnp.exp(-g) + dke * jnp.exp(gl - g)).astype(BF16)
            dke_ke = dke * ke
            dg = dqd * qd - dki * ki - dke_ke
            dgl = jnp.sum(dke_ke, axis=0, keepdims=True) + ddec * decay
            dla = _tri_sum(upper, dg) + dgl
            dl_ref[rows, :] = dla * (1.0 / GATE_NORM) * (1.0 - _sigmoid(logit))

    def rev(n):
        return nblk - 1 - n

    return pl.pallas_call(
        body, name="gla_bwd", grid=(GLA_H, nblk),
        in_specs=[pl.BlockSpec((rb, DK), lambda h, n: (rev(n), (O_GLA + W_GLA * h) // DK)),
                  pl.BlockSpec((rb, DK), lambda h, n: (rev(n), (O_GLA + W_GLA * h) // DK + 1)),
                  pl.BlockSpec((rb, DV), lambda h, n: (rev(n), (O_GLA + W_GLA * h) // DV + 1)),
                  pl.BlockSpec((rb, LRP), lambda h, n: (rev(n), 0)),
                  pl.BlockSpec((LRP, DK), lambda h, n: (0, h)),
                  pl.BlockSpec((1, DK), lambda h, n: (0, h)),
                  pl.BlockSpec((1, per, DV, DK), lambda h, n: (h, rev(n), 0, 0)),
                  pl.BlockSpec((rb, DV), lambda h, n: (rev(n), h)),
                  ANY],
        out_specs=[pl.BlockSpec((rb, W_GLA), lambda h, n: (rev(n), O_GLA // W_GLA + h)),
                   pl.BlockSpec((rb, DK), lambda h, n: (rev(n), h))],
        out_shape=[jax.ShapeDtypeStruct((T, MAIN), BF16), jax.ShapeDtypeStruct((T, GLA_H * DK), F32)],
        scratch_shapes=[pltpu.VMEM((DV, DK), F32)], input_output_aliases={8: 0}, compiler_params=_params(2),
    )(proj, proj, proj, proj_lr, w2p, gate_b, states, d_o, dproj)


def _local_step(x, target, p, reduce_part):
    u = _rms_fwd(x, p["norm1_w"], name="rms1_fwd")
    proj = _matmul(u, p["w_in_main"], name="mm_proj")
    proj_lr = _matmul(u, p["w_in_lr"], name="mm_proj_lr")
    attn_o = _attn_fwd(proj, p["sinks"])
    gla_raw, states = _gla_fwd(proj, proj_lr, p["w2p"], p["gate_b"])
    merged = _merge_fwd(attn_o, gla_raw, proj, p["gla_norm_w"])
    h1 = _matmul(merged, p["w_out"], add=x, name="mm_out")
    v2 = _rms_fwd(h1, p["norm2_w"], name="rms2_fwd")
    gate = _matmul(v2, p["w_gate"], b_slots=True, name="mm_gate")
    up = _matmul(v2, p["w_up"], b_slots=True, name="mm_up")
    ff = _swiglu_fwd(gate, up)
    h2 = _matmul(ff, p["w_down"], add=h1, name="mm_down")
    loss, dh2, g_final = _final_loss(h2, p["final_norm_w"], target)

    dff = _matmul(dh2, p["w_down"], tb=True, name="mm_dff")
    reduce_part("w_down", _matmul(ff, dh2, ta=True, out_dtype=BF16, name="mm_gdown").reshape(4, FFN // 4, D))
    dgate, dup = _swiglu_bwd(dff, gate, up)
    reduce_part("w_gate", _matmul(v2, dgate, ta=True, out_dtype=BF16, bj=FFN // 4, out_slots=True, name="mm_ggate"))
    reduce_part("w_up", _matmul(v2, dup, ta=True, out_dtype=BF16, bj=FFN // 4, out_slots=True, name="mm_gup"))
    dv2 = _matmul(dgate, p["w_gate"], tb=True, b_slots=True, name="mm_dv2a")
    dv2 = _matmul(dup, p["w_up"], tb=True, b_slots=True, add=dv2, name="mm_dv2b")
    dh1, g_norm2 = _rms_bwd(dv2, h1, p["norm2_w"], dh2, name="rms2_bwd")
    dmerged = _matmul(dh1, p["w_out"], tb=True, name="mm_dmerged")
    reduce_part("w_out", _matmul(merged, dh1, ta=True, out_dtype=BF16, name="mm_gout").reshape(4, D // 4, D))
    d_attn, d_gla, dproj, g_gla_norm = _merge_bwd(dmerged, attn_o, gla_raw, proj, p["gla_norm_w"])
    dproj, dlogit = _gla_bwd(proj, proj_lr, p["w2p"], p["gate_b"], states, d_gla, dproj)
    dproj, g_sinks = _attn_bwd(proj, p["sinks"], d_attn, dproj)
    g_gate_b = _colsum(dlogit, name="colsum_gate_b")
    g_w2 = _matmul(proj_lr, dlogit, ta=True, name="mm_gw2")
    dproj_lr = _matmul(dlogit, p["w2p"], tb=True, out_dtype=BF16, name="mm_dlr")
    g_in_main = _matmul(u, dproj, ta=True, out_dtype=BF16, name="mm_gin")
    g_in_lr = _matmul(u, dproj_lr, ta=True, out_dtype=BF16, name="mm_gin_lr")
    du = _matmul(dproj, p["w_in_main"], tb=True, name="mm_du")
    du = _matmul(dproj_lr, p["w_in_lr"], tb=True, add=du, name="mm_du_lr")
    dx, g_norm1 = _rms_bwd(du, x, p["norm1_w"], dh1, name="rms1_bwd")
    grads = dict(norm1_w=g_norm1, w_in_main=g_in_main, w_in_lr=g_in_lr, w2=g_w2[:RANK], gate_b=g_gate_b,
                 sinks=g_sinks[:, 0].reshape(1, N_Q), gla_norm_w=g_gla_norm, norm2_w=g_norm2, final_norm_w=g_final)
    return loss, dx, grads


def _place():
    x, y, c = lax.axis_index("x"), lax.axis_index("y"), lax.axis_index("c")
    chips = [(1 - x, y), (x, 1 - y), (1 - x, 1 - y)]
    return x, y, c, chips


def _hbm_shape(s, dt):
    return jax.ShapeDtypeStruct(s, dt)


def _handshake(peers):
    barrier = pltpu.get_barrier_semaphore()
    for peer in peers:
        pl.semaphore_signal(barrier, inc=1, device_id=peer, device_id_type=MESH)
    pl.semaphore_wait(barrier, len(peers))


def _launch_copies(body, args, out_shape, sems, *, name, collective_id=None):
    if collective_id is None:
        return pl.pallas_call(
            body, name=name, in_specs=[ANY] * len(args), out_specs=[ANY] * len(out_shape), out_shape=out_shape,
            scratch_shapes=sems)(*args)
    return pl.kernel(
        body, name=name, out_type=out_shape, mesh=plsc.ScalarSubcoreMesh(axis_name="sequencer", num_cores=1),
        scratch_types=sems, compiler_params=pltpu.CompilerParams(collective_id=collective_id))(*args)


def _gather_shards(shards, *, name, collective_id=None):
    n = len(shards)

    def body(*refs):
        ins, outs = refs[:n], refs[n:2 * n]
        ici_send, ici_recv, d2d_send, d2d_recv, local_sem = refs[2 * n:]
        x, y, c, chips = _place()
        me = 2 * x + y
        sibling = (x, y, 1 - c)
        if collective_id is not None:
            _handshake([sibling] + [(*chip, c) for chip in chips])

        def half(w, slot, hc):
            r2 = shards[w].shape[0] // 2
            return outs[w].at[slot, pl.ds(hc * r2, r2), :]

        locals_ = [pltpu.make_async_copy(ins[w], outs[w].at[me], local_sem.at[w]) for w in range(n)]
        for cp in locals_:
            cp.start()
        sends = []
        for w in range(n):
            r2 = shards[w].shape[0] // 2
            for j, chip in enumerate(chips):
                cp = pltpu.make_async_remote_copy(
                    src_ref=ins[w].at[pl.ds(c * r2, r2), :], dst_ref=half(w, me, c),
                    send_sem=ici_send.at[w * 3 + j], recv_sem=ici_recv.at[w * 3 + j],
                    device_id=(*chip, c), device_id_type=MESH)
                cp.start()
                sends.append(cp)
        for w in range(n):
            for j, chip in enumerate(chips):
                slot = 2 * chip[0] + chip[1]
                got = half(w, slot, c)
                pltpu.make_async_remote_copy(
                    src_ref=got, dst_ref=got, send_sem=ici_send.at[w * 3 + j], recv_sem=ici_recv.at[w * 3 + j],
                    device_id=(*chip, c), device_id_type=MESH).wait_recv()
                cp = pltpu.make_async_remote_copy(
                    src_ref=got, dst_ref=got, send_sem=d2d_send.at[w * 3 + j], recv_sem=d2d_recv.at[w * 3 + j],
                    device_id=sibling, device_id_type=MESH)
                cp.start()
                sends.append(cp)
        for w in range(n):
            for j, chip in enumerate(chips):
                slot = 2 * chip[0] + chip[1]
                got = half(w, slot, 1 - c)
                pltpu.make_async_remote_copy(
                    src_ref=got, dst_ref=got, send_sem=d2d_send.at[w * 3 + j], recv_sem=d2d_recv.at[w * 3 + j],
                    device_id=sibling, device_id_type=MESH).wait_recv()
        for cp in sends:
            cp.wait_send()
        for cp in locals_:
            cp.wait()

    return _launch_copies(
        body, shards, [_hbm_shape((4,) + s.shape, s.dtype) for s in shards],
        [pltpu.SemaphoreType.DMA((3 * n,)), pltpu.SemaphoreType.DMA((3 * n,)), pltpu.SemaphoreType.DMA((3 * n,)),
         pltpu.SemaphoreType.DMA((3 * n,)), pltpu.SemaphoreType.DMA((n,))],
        name=name, collective_id=collective_id)


def _pair_blocks(R2):
    return _pick(R2, (256, 352, 128, 64, 32, 16))


def _pair_reduce(part, *, name):
    _, R, C = part.shape
    R2 = R // 2
    rb = _pair_blocks(R2)
    nblk = R2 // rb
    steps = [(s, i) for s in range(4) for i in range(nblk)]
    n = len(steps)

    def body(part_ref, sums_ref, own_buf, oth_buf, rcv_buf, out_buf, own_sem, oth_sem, out_sem, send_sem, recv_sem):
        x, y, c, _ = _place()
        sibling = (x, y, 1 - c)

        def fetch(t, half, buf, sem):
            s, i = steps[t]
            return pltpu.make_async_copy(part_ref.at[s, pl.ds(half * R2 + i * rb, rb), :], buf.at[t % 2], sem.at[t % 2])

        def push(t):
            return pltpu.make_async_remote_copy(
                src_ref=oth_buf.at[t % 2], dst_ref=rcv_buf.at[t % 2], send_sem=send_sem.at[t % 2],
                recv_sem=recv_sem.at[t % 2], device_id=sibling, device_id_type=MESH)

        def store(t):
            s, i = steps[t]
            return pltpu.make_async_copy(out_buf.at[t % 2], sums_ref.at[s, pl.ds(i * rb, rb), :], out_sem.at[t % 2])

        fetch(0, c, own_buf, own_sem).start()
        fetch(0, 1 - c, oth_buf, oth_sem).start()
        for t in range(n):
            if t + 1 < n:
                fetch(t + 1, c, own_buf, own_sem).start()
                fetch(t + 1, 1 - c, oth_buf, oth_sem).start()
            fetch(t, 1 - c, oth_buf, oth_sem).wait()
            p = push(t)
            p.start()
            fetch(t, c, own_buf, own_sem).wait()
            p.wait()
            if t >= 2:
                store(t - 2).wait()
            out_buf[t % 2] = (own_buf[t % 2].astype(F32) + rcv_buf[t % 2].astype(F32)).astype(BF16)
            store(t).start()
        for t in range(max(n - 2, 0), n):
            store(t).wait()

    buf = pltpu.VMEM((2, rb, C), BF16)
    sem2 = pltpu.SemaphoreType.DMA((2,))
    return pl.pallas_call(
        body, name=name, in_specs=[ANY], out_specs=ANY, out_shape=_hbm_shape((4, R2, C), BF16),
        scratch_shapes=[buf, buf, buf, buf, sem2, sem2, sem2, sem2, sem2],
        compiler_params=pltpu.CompilerParams(vmem_limit_bytes=VMEM_LIMIT),
    )(part)


def _chip_exchange(sums, *, name, collective_id=None):
    n = len(sums)

    def body(*refs):
        ins, outs = refs[:n], refs[n:2 * n]
        send_sem, recv_sem, local_sem = refs[2 * n:]
        x, y, c, chips = _place()
        me = 2 * x + y
        if collective_id is not None:
            _handshake([(*chip, c) for chip in chips])
        cps = []
        for w in range(n):
            lc = pltpu.make_async_copy(ins[w].at[me], outs[w].at[me], local_sem.at[w])
            lc.start()
            cps.append(lc)
            for j, chip in enumerate(chips):
                slot = 2 * chip[0] + chip[1]
                rc = pltpu.make_async_remote_copy(
                    src_ref=ins[w].at[slot], dst_ref=outs[w].at[me],
                    send_sem=send_sem.at[w * 3 + j], recv_sem=recv_sem.at[w * 3 + j],
                    device_id=(*chip, c), device_id_type=MESH)
                rc.start()
                cps.append(rc)
        for w in range(n):
            for j, chip in enumerate(chips):
                slot = 2 * chip[0] + chip[1]
                pltpu.make_async_remote_copy(
                    src_ref=ins[w].at[slot], dst_ref=outs[w].at[slot],
                    send_sem=send_sem.at[w * 3 + j], recv_sem=recv_sem.at[w * 3 + j],
                    device_id=(*chip, c), device_id_type=MESH).wait_recv()
        for w in range(n):
            cps[w * 4].wait()
            for j in range(3):
                cps[w * 4 + 1 + j].wait_send()

    return _launch_copies(
        body, sums, [_hbm_shape(s.shape, s.dtype) for s in sums],
        [pltpu.SemaphoreType.DMA((3 * n,)), pltpu.SemaphoreType.DMA((3 * n,)), pltpu.SemaphoreType.DMA((n,))],
        name=name, collective_id=collective_id)


def _sum_join(recv, *, name):
    _, R2, C = recv.shape
    rb = _pair_blocks(R2)
    nblk = R2 // rb

    def body(recv_ref, out_ref, in_buf, acc_buf, in_sem, loc_sem, send_sem, recv_sem):
        x, y, c, _ = _place()
        sibling = (x, y, 1 - c)

        def fetch(t):
            return pltpu.make_async_copy(recv_ref.at[:, pl.ds(t * rb, rb), :], in_buf.at[t % 2], in_sem.at[t % 2])

        def rows(t, half):
            return out_ref.at[pl.ds(half * R2 + t * rb, rb), :]

        def put_local(t):
            return pltpu.make_async_copy(acc_buf.at[t % 2], rows(t, c), loc_sem.at[t])

        def put_remote(t, half):
            return pltpu.make_async_remote_copy(
                src_ref=acc_buf.at[t % 2], dst_ref=rows(t, half), send_sem=send_sem.at[t], recv_sem=recv_sem.at[t],
                device_id=sibling, device_id_type=MESH)

        fetch(0).start()
        for t in range(nblk):
            if t + 1 < nblk:
                fetch(t + 1).start()
            fetch(t).wait()
            if t >= 2:
                put_local(t - 2).wait()
                put_remote(t - 2, c).wait_send()
            acc = in_buf[t % 2, 0].astype(F32)
            for s in range(1, 4):
                acc = acc + in_buf[t % 2, s].astype(F32)
            acc_buf[t % 2] = acc
            put_local(t).start()
            put_remote(t, c).start()
        for t in range(max(nblk - 2, 0), nblk):
            put_local(t).wait()
            put_remote(t, c).wait_send()
        for t in range(nblk):
            put_remote(t, 1 - c).wait_recv()

    semn = pltpu.SemaphoreType.DMA((nblk,))
    return pl.pallas_call(
        body, name=name, in_specs=[ANY], out_specs=ANY, out_shape=_hbm_shape((2 * R2, C), F32),
        scratch_shapes=[pltpu.VMEM((2, 4, rb, C), BF16), pltpu.VMEM((2, rb, C), F32),
                        pltpu.SemaphoreType.DMA((2,)), semn, semn, semn],
        compiler_params=pltpu.CompilerParams(vmem_limit_bytes=VMEM_LIMIT),
    )(recv)


def _sum_small(pack):
    R, C = pack.shape

    def body(in_ref, out_ref, all_ref, send_sem, recv_sem):
        x, y, c, _ = _place()
        me = 4 * x + 2 * y + c
        all_ref[me] = in_ref[...]
        cps = []
        for k in range(1, 8):
            peer = (x ^ (k >> 2), y ^ ((k >> 1) & 1), c ^ (k & 1))
            cp = pltpu.make_async_remote_copy(
                src_ref=in_ref, dst_ref=all_ref.at[me], send_sem=send_sem.at[k - 1], recv_sem=recv_sem.at[k - 1],
                device_id=peer, device_id_type=MESH)
            cp.start()
            cps.append(cp)
        for k in range(1, 8):
            peer = (x ^ (k >> 2), y ^ ((k >> 1) & 1), c ^ (k & 1))
            slot = 4 * peer[0] + 2 * peer[1] + peer[2]
            pltpu.make_async_remote_copy(
                src_ref=in_ref, dst_ref=all_ref.at[slot], send_sem=send_sem.at[k - 1], recv_sem=recv_sem.at[k - 1],
                device_id=peer, device_id_type=MESH).wait_recv()
        for cp in cps:
            cp.wait_send()
        acc = all_ref[0]
        for d in range(1, 8):
            acc = acc + all_ref[d]
        out_ref[...] = acc

    return pl.pallas_call(
        body, name="sum_small", in_specs=[pl.BlockSpec(memory_space=pltpu.VMEM)],
        out_specs=pl.BlockSpec(memory_space=pltpu.VMEM), out_shape=jax.ShapeDtypeStruct((R, C), F32),
        scratch_shapes=[pltpu.VMEM((8, R, C), F32), pltpu.SemaphoreType.DMA((7,)), pltpu.SemaphoreType.DMA((7,))],
    )(pack)


def _adamw(w, g, m, v, *, name):
    R, C = w.shape
    rb = _pick(R, (128, 64, 32, 16, 8))
    c1 = 1.0 / (1.0 - B1 ** STEP)
    c2 = 1.0 / (1.0 - B2 ** STEP)

    def body(w_ref, g_ref, m_ref, v_ref, d_ref, nm_ref, nv_ref):
        gv = g_ref[...]
        m2 = B1 * m_ref[...] + (1.0 - B1) * gv
        v2 = B2 * v_ref[...] + (1.0 - B2) * gv * gv
        nm_ref[...] = m2
        nv_ref[...] = v2
        d_ref[...] = -LR * ((m2 * c1) / (jnp.sqrt(v2 * c2) + ADAM_EPS) + WD * w_ref[...])

    blk = pl.BlockSpec((rb, C), lambda i: (i, 0))
    return pl.pallas_call(
        body, name=name, grid=(R // rb,), in_specs=[blk] * 4, out_specs=[blk] * 3,
        out_shape=[jax.ShapeDtypeStruct((R, C), F32)] * 3, compiler_params=_params(1),
    )(w, g, m, v)


SMALL = (("norm1_w", D), ("norm2_w", D), ("final_norm_w", D), ("gate_b", GLA_H * DK), ("gla_norm_w", DV), ("sinks", N_Q))
PACK_W = 1024


def _pack_small(vals, w2, loss):
    rows = []
    for name, width in SMALL:
        v = vals[name].reshape(-1)
        rows.append(jnp.pad(v, (0, (-width) % PACK_W)).reshape(-1, PACK_W))
    rows.append(w2)
    rows.append(jnp.broadcast_to(loss.reshape(1, 1), (1, PACK_W)))
    pack = jnp.concatenate(rows, axis=0)
    return jnp.pad(pack, ((0, 32 - pack.shape[0]), (0, 0)))


def _unpack_small(pack):
    out, r = {}, 0
    for name, width in SMALL:
        nr = -(-width // PACK_W)
        out[name] = pack[r:r + nr].reshape(-1)[:width]
        r += nr
    out["w2"] = pack[r:r + RANK]
    out["loss"] = pack[r + RANK, 0]
    return out


def kernel(x, norm1_w, w_in, gla_gate_w2, gla_gate_b, attn_sinks, gla_norm_w, w_out, norm2_w, w_ffn_gate, w_ffn_up, w_ffn_down, final_norm_w, loss_target, m_norm1_w, m_w_in, m_gla_gate_w2, m_gla_gate_b, m_attn_sinks, m_gla_norm_w, m_w_out, m_norm2_w, m_w_ffn_gate, m_w_ffn_up, m_w_ffn_down, m_final_norm_w, v_norm1_w, v_w_in, v_gla_gate_w2, v_gla_gate_b, v_attn_sinks, v_gla_norm_w, v_w_out, v_norm2_w, v_w_ffn_gate, v_w_ffn_up, v_w_ffn_down, v_final_norm_w):
    chip = 2 * lax.axis_index("x") + lax.axis_index("y")
    w_in_s, w_out_s, w_gate_s, w_up_s, w_down_s, w2_s = (
        w_in[0], w_out[0], w_ffn_gate[0], w_ffn_up[0], w_ffn_down[0], gla_gate_w2[0])
    CS = w_in_s.shape[1]

    g_in, g_w2 = _gather_shards([w_in_s.astype(BF16), w2_s], name="gather_in")
    g_out, g_gate, g_up, g_down = _gather_shards(
        [w_out_s.astype(BF16), w_gate_s.astype(BF16), w_up_s.astype(BF16), w_down_s.astype(BF16)],
        name="gather_rest", collective_id=1)
    w_main, w_lr = _to_main(jnp.transpose(g_in, (1, 0, 2)).reshape(D, 4 * CS))
    w2_full = jnp.transpose(g_w2, (1, 0, 2)).reshape(RANK, GLA_H * DK)
    p = dict(
        norm1_w=norm1_w, norm2_w=norm2_w, final_norm_w=final_norm_w.reshape(1, D), gate_b=gla_gate_b,
        gla_norm_w=gla_norm_w, sinks=jnp.pad(attn_sinks, ((0, 0), (0, 128 - N_Q))),
        w_in_main=w_main, w_in_lr=jnp.pad(w_lr, ((0, 0), (0, LRP - RANK))),
        w2p=jnp.pad(w2_full, ((0, LRP - RANK), (0, 0))).astype(BF16),
        w_out=g_out.reshape(D, D),
        w_gate=g_gate, w_up=g_up, w_down=g_down.reshape(FFN, D),
    )

    tags = ["w_in", "w_out", "w_gate", "w_up", "w_down"]
    recv = {}

    def reduce_part(tag, part):
        sums = _pair_reduce(part, name="pair_reduce_" + tag)
        recv[tag], = _chip_exchange([sums], name="chip_exchange_" + tag, collective_id=2 + tags.index(tag))

    loss_blk, dx, g = _local_step(x[0], loss_target[0], p, reduce_part)
    gin_full = _from_main(g["w_in_main"], g["w_in_lr"][:, :RANK])
    reduce_part("w_in", jnp.transpose(gin_full.reshape(D, 4, CS), (1, 0, 2)))
    big = [_sum_join(recv[tg], name="sum_join_" + tg) for tg in tags]

    small = _unpack_small(_sum_small(_pack_small(g, g["w2"], loss_blk[0, 0])))
    loss = small["loss"]
    g_w2_mine = lax.dynamic_slice_in_dim(small["w2"], chip * (GLA_H * DK // 4), GLA_H * DK // 4, axis=1)

    grads = dict(
        norm1_w=small["norm1_w"].reshape(1, D), w_in=big[0], gla_gate_w2=g_w2_mine,
        gla_gate_b=small["gate_b"].reshape(1, -1), attn_sinks=small["sinks"].reshape(1, N_Q),
        gla_norm_w=small["gla_norm_w"].reshape(1, DV), w_out=big[1], norm2_w=small["norm2_w"].reshape(1, D),
        w_ffn_gate=big[2], w_ffn_up=big[3], w_ffn_down=big[4], final_norm_w=small["final_norm_w"].reshape(1, D))
    weights = dict(
        norm1_w=(norm1_w, m_norm1_w, v_norm1_w), w_in=(w_in_s, m_w_in[0], v_w_in[0]),
        gla_gate_w2=(w2_s, m_gla_gate_w2[0], v_gla_gate_w2[0]), gla_gate_b=(gla_gate_b, m_gla_gate_b, v_gla_gate_b),
        attn_sinks=(attn_sinks, m_attn_sinks, v_attn_sinks), gla_norm_w=(gla_norm_w, m_gla_norm_w, v_gla_norm_w),
        w_out=(w_out_s, m_w_out[0], v_w_out[0]), norm2_w=(norm2_w, m_norm2_w, v_norm2_w),
        w_ffn_gate=(w_gate_s, m_w_ffn_gate[0], v_w_ffn_gate[0]), w_ffn_up=(w_up_s, m_w_ffn_up[0], v_w_ffn_up[0]),
        w_ffn_down=(w_down_s, m_w_ffn_down[0], v_w_ffn_down[0]),
        final_norm_w=(final_norm_w.reshape(1, D), m_final_norm_w.reshape(1, D), v_final_norm_w.reshape(1, D)))
    names = ["norm1_w", "w_in", "gla_gate_w2", "gla_gate_b", "attn_sinks", "gla_norm_w", "w_out", "norm2_w",
             "w_ffn_gate", "w_ffn_up", "w_ffn_down", "final_norm_w"]
    lead = {"norm1_w": False, "gla_gate_b": False, "attn_sinks": False, "gla_norm_w": False, "norm2_w": False}
    g_out_l, d_out, m_out, v_out = [], [], [], []
    for nm in names:
        w, m, v = weights[nm]
        gr = grads[nm]
        dl, nmn, nvn = _adamw(w, gr, m, v, name="adamw_" + nm)
        if nm == "final_norm_w":
            shape = (D,)
        elif nm in lead:
            shape = w.shape
        else:
            shape = (1,) + w.shape
        g_out_l.append(gr.reshape(shape))
        d_out.append(dl.reshape(shape))
        m_out.append(nmn.reshape(shape))
        v_out.append(nvn.reshape(shape))
    return (loss, dx[None], *g_out_l, *d_out, *m_out, *v_out)
```

```python
import functools

import jax
import jax.numpy as jnp
from jax import lax
from jax.experimental import pallas as pl
from jax.experimental.pallas import tpu as pltpu
from jax.experimental.pallas import tpu_sc as plsc

F32 = jnp.float32
BF16 = jnp.bfloat16

D = 2048
HEAD_DIM = 64
N_Q = 32
N_KV = 4
GROUP = 8
WINDOW = 128
GLA_H = 4
DK = 256
DV = 512
RANK = 16
CHUNK = 64
FFN = 5632
EPS = 1e-6
MASK_VALUE = -1e30
GATE_NORM = 16.0
LR, B1, B2, ADAM_EPS, WD, STEP = 0.001, 0.9, 0.999, 1e-08, 0.01, 10

MAIN = 12800
O_MERGE, O_GLA, O_ATTN = 0, 6144, 10240
W_MERGE, W_GLA, W_ATTN = 3 * DV, 2 * DK + DV, D + 2 * N_KV * HEAD_DIM
LR_AT = 6656
LRP = 128


def _main_pieces():
    o_gq, o_gk, o_gv, o_gr, o_ga, o_gb = 2560, 3584, 4608, LR_AT + RANK, LR_AT + RANK + D, LR_AT + RANK + 2 * D
    pieces = []
    for h in range(GLA_H):
        pieces += [(o_gr + DV * h, DV), (o_ga + DV * h, DV), (o_gb + DV * h, DV)]
    for h in range(GLA_H):
        pieces += [(o_gq + DK * h, DK), (o_gk + DK * h, DK), (o_gv + DV * h, DV)]
    pieces.append((0, W_ATTN))
    return pieces


def _to_main(full):
    main = jnp.concatenate([full[..., a:a + w] for a, w in _main_pieces()], axis=-1)
    return main, full[..., LR_AT:LR_AT + RANK]


def _from_main(main, lr):
    at, where = 0, []
    for a, w in _main_pieces():
        where.append((a, main[..., at:at + w]))
        at += w
    where.append((LR_AT, lr))
    return jnp.concatenate([piece for _, piece in sorted(where, key=lambda t: t[0])], axis=-1)

VMEM_LIMIT = 56 * 1024 * 1024
MESH = pl.DeviceIdType.MESH
ANY = pl.BlockSpec(memory_space=pl.ANY)


def _params(ngrid):
    return pltpu.CompilerParams(dimension_semantics=("arbitrary",) * ngrid, vmem_limit_bytes=VMEM_LIMIT)


def _pick(n, cands):
    for c in cands:
        if n % c == 0:
            return c
    return n


def _sigmoid(x):
    return 1.0 / (1.0 + jnp.exp(-x))


def _dot(a, b, ca, cb):
    return lax.dot_general(a, b, (((ca,), (cb,)), ((), ())), preferred_element_type=F32)


def _matmul(a, b, *, name, ta=False, tb=False, add=None, out_dtype=F32, bm=None, bj=None, bc=None,
            b_slots=False, out_slots=False):
    C, M = a.shape if ta else a.shape[::-1]
    if b_slots:
        if tb:
            J, bc = b.shape[1], b.shape[2]
            assert b.shape[0] * bc == C
        else:
            bj = b.shape[2]
            J = b.shape[0] * bj
            assert b.shape[1] == C
    else:
        J = b.shape[0] if tb else b.shape[1]
        assert (b.shape[1] if tb else b.shape[0]) == C
    bm = bm or _pick(M, (1024, 512, 256, 128) if C <= FFN else (512, 256, 128))
    bj = bj or _pick(J, (1280, 1024, 512, 256, 128) if C <= D else (512, 256, 128))
    bc = bc or (C if C <= FFN else _pick(C, (6400,)))
    nm, nj, nc = M // bm, J // bj, C // bc
    has_add = add is not None

    def body(*refs):
        a_ref, b_ref = refs[0], refs[1]
        add_ref = refs[2] if has_add else None
        o_ref = refs[3] if has_add else refs[2]
        p = _dot(a_ref[...].astype(BF16), b_ref[...].astype(BF16), 0 if ta else 1, 1 if tb else 0)

        def finish(acc):
            if has_add:
                acc = acc + add_ref[...]
            o_ref[...] = acc.astype(o_ref.dtype)

        if nc == 1:
            finish(p)
        else:
            acc_ref = refs[-1]
            c = pl.program_id(2)

            @pl.when(c == 0)
            def _():
                acc_ref[...] = p

            @pl.when(c > 0)
            def _():
                acc_ref[...] += p

            @pl.when(c == nc - 1)
            def _():
                finish(acc_ref[...])

    a_spec = pl.BlockSpec((bc, bm), lambda m, j, c: (c, m)) if ta else pl.BlockSpec((bm, bc), lambda m, j, c: (m, c))
    if b_slots:
        b_spec = (pl.BlockSpec((None, bj, bc), lambda m, j, c: (c, j, 0)) if tb
                  else pl.BlockSpec((None, bc, bj), lambda m, j, c: (j, c, 0)))
    else:
        b_spec = (pl.BlockSpec((bj, bc), lambda m, j, c: (j, c)) if tb
                  else pl.BlockSpec((bc, bj), lambda m, j, c: (c, j)))
    if out_slots:
        assert not has_add
        o_spec = pl.BlockSpec((None, bm, bj), lambda m, j, c: (j, m, 0))
        out_shape = jax.ShapeDtypeStruct((nj, M, bj), out_dtype)
    else:
        o_spec = pl.BlockSpec((bm, bj), lambda m, j, c: (m, j))
        out_shape = jax.ShapeDtypeStruct((M, J), out_dtype)
    in_specs = [a_spec, b_spec] + ([o_spec] if has_add else [])
    args = (a, b) + ((add,) if has_add else ())
    return pl.pallas_call(
        body, name=name, grid=(nm, nj, nc), in_specs=in_specs, out_specs=o_spec,
        out_shape=out_shape,
        scratch_shapes=[pltpu.VMEM((bm, bj), F32)] if nc > 1 else [],
        compiler_params=_params(3),
    )(*args)


def _rows(T):
    return _pick(T, (256, 128))


def _rms_fwd(x, w, *, name):
    T = x.shape[0]
    rb = _rows(T)

    def body(x_ref, w_ref, u_ref, ut_ref):
        xv = x_ref[...]
        r = lax.rsqrt(jnp.mean(xv * xv, axis=-1, keepdims=True) + EPS)
        u = xv * r * w_ref[...]
        u_ref[...] = u.astype(BF16)
        ut_ref[...] = u.T.astype(BF16)

    return pl.pallas_call(
        body, name=name, grid=(T // rb,),
        in_specs=[pl.BlockSpec((rb, D), lambda i: (i, 0)), pl.BlockSpec((1, D), lambda i: (0, 0))],
        out_specs=[pl.BlockSpec((rb, D), lambda i: (i, 0)), pl.BlockSpec((D, rb), lambda i: (0, i))],
        out_shape=[jax.ShapeDtypeStruct((T, D), BF16), jax.ShapeDtypeStruct((D, T), BF16)], compiler_params=_params(1),
    )(x, w)


def _rms_bwd(dy, h, w, resid, *, name):
    T = h.shape[0]
    rb = _rows(T)

    def body(dy_ref, h_ref, w_ref, res_ref, dh_ref, dhb_ref, dw_ref):
        @pl.when(pl.program_id(0) == 0)
        def _():
            dw_ref[...] = jnp.zeros_like(dw_ref)

        hv = h_ref[...]
        r = lax.rsqrt(jnp.mean(hv * hv, axis=-1, keepdims=True) + EPS)
        hn = hv * r
        dyv = dy_ref[...]
        dw_ref[...] += jnp.sum(dyv * hn, axis=0, keepdims=True)
        t = dyv * w_ref[...]
        dh = res_ref[...] + r * (t - hn * jnp.mean(t * hn, axis=-1, keepdims=True))
        dh_ref[...] = dh
        dhb_ref[...] = dh.astype(BF16)

    blk = pl.BlockSpec((rb, D), lambda i: (i, 0))
    vec = pl.BlockSpec((1, D), lambda i: (0, 0))
    return pl.pallas_call(
        body, name=name, grid=(T // rb,), in_specs=[blk, blk, vec, blk], out_specs=[blk, blk, vec],
        out_shape=[jax.ShapeDtypeStruct((T, D), F32), jax.ShapeDtypeStruct((T, D), BF16),
                   jax.ShapeDtypeStruct((1, D), F32)],
        compiler_params=_params(1),
    )(dy, h, w, resid)


def _final_loss(h, w, target):
    T = h.shape[0]
    rb = _rows(T)

    def body(h_ref, w_ref, t_ref, loss_ref, dh_ref, dhb_ref, dw_ref):
        @pl.when(pl.program_id(0) == 0)
        def _():
            dw_ref[...] = jnp.zeros_like(dw_ref)
            loss_ref[...] = jnp.zeros_like(loss_ref)

        hv = h_ref[...]
        wv = w_ref[...]
        r = lax.rsqrt(jnp.mean(hv * hv, axis=-1, keepdims=True) + EPS)
        hn = hv * r
        e = hn * wv - t_ref[...]
        row = jnp.sum(e * e, axis=-1, keepdims=True) * (0.5 / D)
        loss_ref[...] += jnp.broadcast_to(jnp.sum(row, axis=0, keepdims=True), loss_ref.shape)
        dy = e * (1.0 / D)
        dw_ref[...] += jnp.sum(dy * hn, axis=0, keepdims=True)
        t = dy * wv
        dh = r * (t - hn * jnp.mean(t * hn, axis=-1, keepdims=True))
        dh_ref[...] = dh
        dhb_ref[...] = dh.astype(BF16)

    blk = pl.BlockSpec((rb, D), lambda i: (i, 0))
    vec = pl.BlockSpec((1, D), lambda i: (0, 0))
    return pl.pallas_call(
        body, name="final_loss", grid=(T // rb,), in_specs=[blk, vec, blk],
        out_specs=[pl.BlockSpec((8, 128), lambda i: (0, 0)), blk, blk, vec],
        out_shape=[jax.ShapeDtypeStruct((8, 128), F32), jax.ShapeDtypeStruct((T, D), F32),
                   jax.ShapeDtypeStruct((T, D), BF16), jax.ShapeDtypeStruct((1, D), F32)],
        compiler_params=_params(1),
    )(h, w, target)


def _swiglu_fwd(gate, up):
    T = gate.shape[0]
    rb = _rows(T)

    def body(g_ref, u_ref, o_ref, ot_ref):
        g = g_ref[...]
        ff = g * _sigmoid(g) * u_ref[...]
        o_ref[...] = ff.astype(BF16)
        ot_ref[...] = ff.T.astype(BF16)

    blk = pl.BlockSpec((rb, 512), lambda i, j: (i, j))
    return pl.pallas_call(
        body, name="swiglu_fwd", grid=(T // rb, FFN // 512), in_specs=[blk, blk],
        out_specs=[blk, pl.BlockSpec((512, rb), lambda i, j: (j, i))],
        out_shape=[jax.ShapeDtypeStruct((T, FFN), BF16), jax.ShapeDtypeStruct((FFN, T), BF16)],
        compiler_params=_params(2),
    )(gate, up)


def _swiglu_bwd(dff, gate, up):
    T = gate.shape[0]
    rb = _rows(T)

    def body(d_ref, g_ref, u_ref, dg_ref, du_ref):
        g = g_ref[...]
        d = d_ref[...]
        sg = _sigmoid(g)
        du_ref[...] = (d * g * sg).astype(BF16)
        dg_ref[...] = (d * u_ref[...] * sg * (1.0 + g * (1.0 - sg))).astype(BF16)

    blk = pl.BlockSpec((rb, 512), lambda i, j: (i, j))
    return pl.pallas_call(
        body, name="swiglu_bwd", grid=(T // rb, FFN // 512), in_specs=[blk, blk, blk], out_specs=[blk, blk],
        out_shape=[jax.ShapeDtypeStruct((T, FFN), BF16)] * 2, compiler_params=_params(2),
    )(dff, gate, up)


def _colsum(x, *, name):
    T, W = x.shape
    rb = _rows(T)

    def body(x_ref, o_ref):
        @pl.when(pl.program_id(0) == 0)
        def _():
            o_ref[...] = jnp.zeros_like(o_ref)

        o_ref[...] += jnp.sum(x_ref[...], axis=0, keepdims=True)

    return pl.pallas_call(
        body, name=name, grid=(T // rb,), in_specs=[pl.BlockSpec((rb, W), lambda i: (i, 0))],
        out_specs=pl.BlockSpec((1, W), lambda i: (0, 0)), out_shape=jax.ShapeDtypeStruct((1, W), F32),
        compiler_params=_params(1),
    )(x)


def _merge_fwd(attn_o, gla_raw, proj, gla_norm_w):
    T = attn_o.shape[0]
    rb = _rows(T)

    def body(a_ref, g_ref, gr_ref, ga_ref, gb_ref, w_ref, o_ref, ot_ref):
        graw = g_ref[...]
        r = lax.rsqrt(jnp.mean(graw * graw, axis=-1, keepdims=True) + EPS)
        gr = gr_ref[...]
        go = graw * r * w_ref[...] * (gr * _sigmoid(gr))
        merged = _sigmoid(ga_ref[...]) * a_ref[...] + _sigmoid(gb_ref[...]) * go
        o_ref[...] = merged.astype(BF16)
        ot_ref[...] = merged.T.astype(BF16)

    def sec(k):
        return pl.BlockSpec((rb, DV), lambda i, h: (i, O_MERGE // DV + 3 * h + k))

    blk = pl.BlockSpec((rb, DV), lambda i, h: (i, h))
    return pl.pallas_call(
        body, name="merge_fwd", grid=(T // rb, GLA_H),
        in_specs=[blk, blk, sec(0), sec(1), sec(2), pl.BlockSpec((1, DV), lambda i, h: (0, 0))],
        out_specs=[blk, pl.BlockSpec((DV, rb), lambda i, h: (h, i))],
        out_shape=[jax.ShapeDtypeStruct((T, D), BF16), jax.ShapeDtypeStruct((D, T), BF16)], compiler_params=_params(2),
    )(attn_o, gla_raw, proj, proj, proj, gla_norm_w)


def _merge_bwd(dm, attn_o, gla_raw, proj, gla_norm_w):
    T = attn_o.shape[0]
    rb = _rows(T)

    def body(dm_ref, a_ref, g_ref, gr_ref, ga_ref, gb_ref, w_ref, da_ref, dg_ref, dp_ref, dw_ref):
        @pl.when((pl.program_id(0) == 0) & (pl.program_id(1) == 0))
        def _():
            dw_ref[...] = jnp.zeros_like(dw_ref)

        dmv = dm_ref[...]
        av = a_ref[...]
        graw = g_ref[...]
        gr = gr_ref[...]
        wv = w_ref[...]
        sa = _sigmoid(ga_ref[...])
        sb = _sigmoid(gb_ref[...])
        r = lax.rsqrt(jnp.mean(graw * graw, axis=-1, keepdims=True) + EPS)
        gnh = graw * r
        gn = gnh * wv
        sr = _sigmoid(gr)
        sl = gr * sr
        go = gn * sl
        da_ref[...] = dmv * sa
        dgo = dmv * sb
        dp_ref[:, 0:DV] = (dgo * gn * sr * (1.0 + gr * (1.0 - sr))).astype(BF16)
        dp_ref[:, DV:2 * DV] = (dmv * av * sa * (1.0 - sa)).astype(BF16)
        dp_ref[:, 2 * DV:3 * DV] = (dmv * go * sb * (1.0 - sb)).astype(BF16)
        dgn = dgo * sl
        dw_ref[...] += jnp.sum(dgn * gnh, axis=0, keepdims=True)
        t = dgn * wv
        dg_ref[...] = r * (t - gnh * jnp.mean(t * gnh, axis=-1, keepdims=True))

    def sec(k):
        return pl.BlockSpec((rb, DV), lambda i, h: (i, O_MERGE // DV + 3 * h + k))

    blk = pl.BlockSpec((rb, DV), lambda i, h: (i, h))
    vec = pl.BlockSpec((1, DV), lambda i, h: (0, 0))
    return pl.pallas_call(
        body, name="merge_bwd", grid=(T // rb, GLA_H),
        in_specs=[blk, blk, blk, sec(0), sec(1), sec(2), vec],
        out_specs=[blk, blk, pl.BlockSpec((rb, W_MERGE), lambda i, h: (i, O_MERGE // W_MERGE + h)), vec],
        out_shape=[jax.ShapeDtypeStruct((T, D), F32), jax.ShapeDtypeStruct((T, D), F32),
                   jax.ShapeDtypeStruct((T, MAIN), BF16), jax.ShapeDtypeStruct((1, DV), F32)],
        compiler_params=_params(2),
    )(dm, attn_o, gla_raw, proj, proj, proj, gla_norm_w)


def _attn_mask(n):
    qi = lax.broadcasted_iota(jnp.int32, (GROUP * WINDOW, 2 * WINDOW), 0) & (WINDOW - 1)
    ki = lax.broadcasted_iota(jnp.int32, (GROUP * WINDOW, 2 * WINDOW), 1)
    rel = qi + WINDOW - ki
    return (rel >= 0) & (rel < WINDOW) & ((n > 0) | (ki >= WINDOW))


def _stack_heads(ref, h):
    return jnp.concatenate(
        [ref[:, (h * GROUP + g) * HEAD_DIM:(h * GROUP + g + 1) * HEAD_DIM] for g in range(GROUP)], axis=0).astype(BF16)


def _stack_sinks(s_ref, h):
    return jnp.concatenate(
        [jnp.broadcast_to(s_ref[:, h * GROUP + g:h * GROUP + g + 1], (WINDOW, 1)) for g in range(GROUP)], axis=0)


def _attn_probs(q, kk, mask, sink):
    s = _dot(q, kk, 1, 1) * (HEAD_DIM ** -0.5)
    s = jnp.where(mask, s, MASK_VALUE)
    m = jnp.maximum(jnp.max(s, axis=-1, keepdims=True), sink)
    e = jnp.exp(s - m)
    es = jnp.exp(sink - m)
    inv = 1.0 / (jnp.sum(e, axis=-1, keepdims=True) + es)
    return e * inv, es * inv


def _attn_specs(nb, rev):
    def at(n):
        return (nb - 1 - n) if rev else n

    kcol, vcol = (O_ATTN + D) // 256, (O_ATTN + D) // 256 + 1
    q = pl.BlockSpec((WINDOW, D), lambda n: (at(n), O_ATTN // D))
    kc = pl.BlockSpec((WINDOW, 256), lambda n: (at(n), kcol))
    kp = pl.BlockSpec((WINDOW, 256), lambda n: (jnp.maximum(at(n) - 1, 0), kcol))
    vc = pl.BlockSpec((WINDOW, 256), lambda n: (at(n), vcol))
    vp = pl.BlockSpec((WINDOW, 256), lambda n: (jnp.maximum(at(n) - 1, 0), vcol))
    sk = pl.BlockSpec((1, 128), lambda n: (0, 0))
    o = pl.BlockSpec((WINDOW, D), lambda n: (at(n), 0))
    return q, kc, kp, vc, vp, sk, o


def _attn_fwd(proj, sinks):
    T = proj.shape[0]
    nb = T // WINDOW

    def body(q_ref, kc_ref, kp_ref, vc_ref, vp_ref, s_ref, o_ref):
        mask = _attn_mask(pl.program_id(0))
        for h in range(N_KV):
            hs = slice(h * HEAD_DIM, (h + 1) * HEAD_DIM)
            kk = jnp.concatenate([kp_ref[:, hs], kc_ref[:, hs]], axis=0).astype(BF16)
            vv = jnp.concatenate([vp_ref[:, hs], vc_ref[:, hs]], axis=0).astype(BF16)
            p, _ = _attn_probs(_stack_heads(q_ref, h), kk, mask, _stack_sinks(s_ref, h))
            o = _dot(p.astype(BF16), vv, 1, 0)
            for g in range(GROUP):
                hg = h * GROUP + g
                o_ref[:, hg * HEAD_DIM:(hg + 1) * HEAD_DIM] = o[g * WINDOW:(g + 1) * WINDOW]

    q, kc, kp, vc, vp, sk, o = _attn_specs(nb, False)
    return pl.pallas_call(
        body, name="attn_fwd", grid=(nb,), in_specs=[q, kc, kp, vc, vp, sk], out_specs=o,
        out_shape=jax.ShapeDtypeStruct((T, D), F32), compiler_params=_params(1),
    )(proj, proj, proj, proj, proj, sinks)


def _attn_bwd(proj, sinks, d_o, dproj):
    T = proj.shape[0]
    nb = T // WINDOW
    kat, vat = D, D + N_KV * HEAD_DIM

    def body(q_ref, kc_ref, kp_ref, vc_ref, vp_ref, s_ref, do_ref, _, dp_ref, ds_ref, ck_ref, cv_ref):
        i = pl.program_id(0)
        n = nb - 1 - i

        @pl.when(i == 0)
        def _():
            ck_ref[...] = jnp.zeros_like(ck_ref)
            cv_ref[...] = jnp.zeros_like(cv_ref)
            ds_ref[...] = jnp.zeros_like(ds_ref)

        mask = _attn_mask(n)
        for h in range(N_KV):
            hs = slice(h * HEAD_DIM, (h + 1) * HEAD_DIM)
            kk = jnp.concatenate([kp_ref[:, hs], kc_ref[:, hs]], axis=0).astype(BF16)
            vv = jnp.concatenate([vp_ref[:, hs], vc_ref[:, hs]], axis=0).astype(BF16)
            q = _stack_heads(q_ref, h)
            do = _stack_heads(do_ref, h)
            p, ps = _attn_probs(q, kk, mask, _stack_sinks(s_ref, h))
            dp = _dot(do, vv, 1, 1)
            delta = jnp.sum(p * dp, axis=-1, keepdims=True)
            dsc = (p * (dp - delta) * (HEAD_DIM ** -0.5)).astype(BF16)
            dsink = -ps * delta
            dq = _dot(dsc, kk, 1, 0).astype(BF16)
            for g in range(GROUP):
                hg = h * GROUP + g
                rows = slice(g * WINDOW, (g + 1) * WINDOW)
                ds_ref[hg:hg + 1, :] += jnp.broadcast_to(jnp.sum(dsink[rows], axis=0, keepdims=True), (1, 128))
                dp_ref[:, hg * HEAD_DIM:(hg + 1) * HEAD_DIM] = dq[rows]
            dkk = _dot(dsc, q, 0, 0)
            dvv = _dot(p.astype(BF16), do, 0, 0)
            dp_ref[:, kat + h * HEAD_DIM:kat + (h + 1) * HEAD_DIM] = (dkk[WINDOW:] + ck_ref[:, hs]).astype(BF16)
            dp_ref[:, vat + h * HEAD_DIM:vat + (h + 1) * HEAD_DIM] = (dvv[WINDOW:] + cv_ref[:, hs]).astype(BF16)
            ck_ref[:, hs] = dkk[:WINDOW]
            cv_ref[:, hs] = dvv[:WINDOW]

    q, kc, kp, vc, vp, sk, o = _attn_specs(nb, True)
    return pl.pallas_call(
        body, name="attn_bwd", grid=(nb,), in_specs=[q, kc, kp, vc, vp, sk, o, ANY],
        out_specs=[pl.BlockSpec((WINDOW, W_ATTN), lambda n: (nb - 1 - n, O_ATTN // W_ATTN)),
                   pl.BlockSpec((N_Q, 128), lambda n: (0, 0))],
        out_shape=[jax.ShapeDtypeStruct((T, MAIN), BF16), jax.ShapeDtypeStruct((N_Q, 128), F32)],
        scratch_shapes=[pltpu.VMEM((WINDOW, 256), F32), pltpu.VMEM((WINDOW, 256), F32)],
        input_output_aliases={7: 0}, compiler_params=_params(1),
    )(proj, proj, proj, proj, proj, sinks, d_o, dproj)


def _split3(x):
    hi = x.astype(BF16)
    r1 = x - hi.astype(F32)
    mid = r1.astype(BF16)
    lo = (r1 - mid.astype(F32)).astype(BF16)
    return hi, mid, lo


def _tri_sum(tri, x):
    hi, mid, lo = _split3(x)
    return _dot(tri, hi, 1, 0) + _dot(tri, mid, 1, 0) + _dot(tri, lo, 1, 0)


def _gla_chunk(q, k, lr, w2, b, lower):
    logit = _dot(lr.astype(BF16), w2.astype(BF16), 1, 0) + b
    la = (jnp.minimum(logit, 0.0) - jnp.log(1.0 + jnp.exp(-jnp.abs(logit)))) * (1.0 / GATE_NORM)
    g = _tri_sum(lower, la)
    gl = g[CHUNK - 1:CHUNK, :]
    eg = jnp.exp(g)
    qd = q * (DK ** -0.5) * eg
    ki = k * jnp.exp(-g)
    ke = k * jnp.exp(gl - g)
    return logit, g, gl, eg, qd, ki, ke


def _tri(lower):
    r = lax.broadcasted_iota(jnp.int32, (CHUNK, CHUNK), 0)
    c = lax.broadcasted_iota(jnp.int32, (CHUNK, CHUNK), 1)
    return (r >= c) if lower else (r <= c)


def _gla_rows(T):
    return _pick(T, (256, 128, 64))


def _gla_fwd(proj, proj_lr, w2p, gate_b):
    T = proj.shape[0]
    rb = _gla_rows(T)
    per = rb // CHUNK

    def body(q_ref, k_ref, v_ref, lr_ref, w2_ref, b_ref, o_ref, st_ref, s_scr):
        @pl.when(pl.program_id(1) == 0)
        def _():
            s_scr[...] = jnp.zeros_like(s_scr)

        low = _tri(True)
        lower = low.astype(BF16)
        for i in range(per):
            rows = slice(i * CHUNK, (i + 1) * CHUNK)
            _, g, gl, eg, qd, ki, ke = _gla_chunk(q_ref[rows, :], k_ref[rows, :], lr_ref[rows, :], w2_ref[...],
                                                  b_ref[...], lower)
            v = v_ref[rows, :].astype(BF16)
            qdb = qd.astype(BF16)
            att = jnp.where(low, _dot(qdb, ki.astype(BF16), 1, 1), 0.0)
            st = s_scr[...]
            st_ref[0, i] = st
            o_ref[rows, :] = _dot(att.astype(BF16), v, 1, 0) + _dot(qdb, st.astype(BF16), 1, 1)
            s_scr[...] = st * jnp.exp(gl) + _dot(v, ke.astype(BF16), 0, 0)

    return pl.pallas_call(
        body, name="gla_fwd", grid=(GLA_H, T // rb),
        in_specs=[pl.BlockSpec((rb, DK), lambda h, n: (n, (O_GLA + W_GLA * h) // DK)),
                  pl.BlockSpec((rb, DK), lambda h, n: (n, (O_GLA + W_GLA * h) // DK + 1)),
                  pl.BlockSpec((rb, DV), lambda h, n: (n, (O_GLA + W_GLA * h) // DV + 1)),
                  pl.BlockSpec((rb, LRP), lambda h, n: (n, 0)),
                  pl.BlockSpec((LRP, DK), lambda h, n: (0, h)),
                  pl.BlockSpec((1, DK), lambda h, n: (0, h))],
        out_specs=[pl.BlockSpec((rb, DV), lambda h, n: (n, h)),
                   pl.BlockSpec((1, per, DV, DK), lambda h, n: (h, n, 0, 0))],
        out_shape=[jax.ShapeDtypeStruct((T, GLA_H * DV), F32),
                   jax.ShapeDtypeStruct((GLA_H, T // CHUNK, DV, DK), F32)],
        scratch_shapes=[pltpu.VMEM((DV, DK), F32)], compiler_params=_params(2),
    )(proj, proj, proj, proj_lr, w2p, gate_b)


def _gla_bwd(proj, proj_lr, w2p, gate_b, states, d_o, dproj):
    T = proj.shape[0]
    rb = _gla_rows(T)
    per = rb // CHUNK
    nblk = T // rb

    def body(q_ref, k_ref, v_ref, lr_ref, w2_ref, b_ref, st_ref, do_ref, _, dp_ref, dl_ref, ds_scr):
        @pl.when(pl.program_id(1) == 0)
        def _():
            ds_scr[...] = jnp.zeros_like(ds_scr)

        low = _tri(True)
        lower = low.astype(BF16)
        upper = _tri(False).astype(BF16)
        for i in reversed(range(per)):
            rows = slice(i * CHUNK, (i + 1) * CHUNK)
            logit, g, gl, eg, qd, ki, ke = _gla_chunk(q_ref[rows, :], k_ref[rows, :], lr_ref[rows, :], w2_ref[...],
                                                      b_ref[...], lower)
            v = v_ref[rows, :].astype(BF16)
            do = do_ref[rows, :].astype(BF16)
            qdb, kib, keb = qd.astype(BF16), ki.astype(BF16), ke.astype(BF16)
            att = jnp.where(low, _dot(qdb, kib, 1, 1), 0.0).astype(BF16)
            sp = st_ref[0, i]
            ds = ds_scr[...]
            dsb = ds.astype(BF16)
            datt = jnp.where(low, _dot(do, v, 1, 1), 0.0).astype(BF16)
            dp_ref[rows, 2 * DK:] = (_dot(att, do, 0, 0) + _dot(keb, dsb, 1, 1)).astype(BF16)
            dqd = _dot(datt, kib, 1, 0) + _dot(do, sp.astype(BF16), 1, 0)
            dki = _dot(datt, qdb, 0, 0)
            dke = _dot(v, dsb, 1, 0)
            decay = jnp.exp(gl)
            ds_scr[...] = ds * decay + _dot(do, qdb, 0, 0)
            ddec = jnp.sum(ds * sp, axis=0, keepdims=True)
            dp_ref[rows, 0:DK] = (dqd * (DK ** -0.5) * eg).astype(BF16)
            dp_ref[rows, DK:2 * DK] = (dki * jnp.exp(-g) + dke * jnp.exp(gl - g)).astype(BF16)
            dke_ke = dke * ke
            dg = dqd * qd - dki * ki - dke_ke
            dgl = jnp.sum(dke_ke, axis=0, keepdims=True) + ddec * decay
            dla = _tri_sum(upper, dg) + dgl
            dl_ref[rows, :] = dla * (1.0 / GATE_NORM) * (1.0 - _sigmoid(logit))

    def rev(n):
        return nblk - 1 - n

    return pl.pallas_call(
        body, name="gla_bwd", grid=(GLA_H, nblk),
        in_specs=[pl.BlockSpec((rb, DK), lambda h, n: (rev(n), (O_GLA + W_GLA * h) // DK)),
                  pl.BlockSpec((rb, DK), lambda h, n: (rev(n), (O_GLA + W_GLA * h) // DK + 1)),
                  pl.BlockSpec((rb, DV), lambda h, n: (rev(n), (O_GLA + W_GLA * h) // DV + 1)),
                  pl.BlockSpec((rb, LRP), lambda h, n: (rev(n), 0)),
                  pl.BlockSpec((LRP, DK), lambda h, n: (0, h)),
                  pl.BlockSpec((1, DK), lambda h, n: (0, h)),
                  pl.BlockSpec((1, per, DV, DK), lambda h, n: (h, rev(n), 0, 0)),
                  pl.BlockSpec((rb, DV), lambda h, n: (rev(n), h)),
                  ANY],
        out_specs=[pl.BlockSpec((rb, W_GLA), lambda h, n: (rev(n), O_GLA // W_GLA + h)),
                   pl.BlockSpec((rb, DK), lambda h, n: (rev(n), h))],
        out_shape=[jax.ShapeDtypeStruct((T, MAIN), BF16), jax.ShapeDtypeStruct((T, GLA_H * DK), F32)],
        scratch_shapes=[pltpu.VMEM((DV, DK), F32)], input_output_aliases={8: 0}, compiler_params=_params(2),
    )(proj, proj, proj, proj_lr, w2p, gate_b, states, d_o, dproj)


def _local_step(x, target, p, reduce_part):
    u, u_t = _rms_fwd(x, p["norm1_w"], name="rms1_fwd")
    proj = _matmul(u, p["w_in_main"], name="mm_proj")
    proj_lr = _matmul(u, p["w_in_lr"], name="mm_proj_lr")
    attn_o = _attn_fwd(proj, p["sinks"])
    gla_raw, states = _gla_fwd(proj, proj_lr, p["w2p"], p["gate_b"])
    merged, merged_t = _merge_fwd(attn_o, gla_raw, proj, p["gla_norm_w"])
    h1 = _matmul(merged, p["w_out"], add=x, name="mm_out")
    v2, v2_t = _rms_fwd(h1, p["norm2_w"], name="rms2_fwd")
    gate = _matmul(v2, p["w_gate"], b_slots=True, name="mm_gate")
    up = _matmul(v2, p["w_up"], b_slots=True, name="mm_up")
    ff, ff_t = _swiglu_fwd(gate, up)
    h2 = _matmul(ff, p["w_down"], add=h1, name="mm_down")
    loss, dh2, dh2_b, g_final = _final_loss(h2, p["final_norm_w"], target)

    dff = _matmul(dh2_b, p["w_down"], tb=True, name="mm_dff")
    reduce_part("w_down", _matmul(ff_t, dh2_b, out_dtype=BF16, name="mm_gdown").reshape(4, FFN // 4, D))
    dgate, dup = _swiglu_bwd(dff, gate, up)
    reduce_part("w_gate", _matmul(v2_t, dgate, out_dtype=BF16, bj=FFN // 4, out_slots=True, name="mm_ggate"))
    reduce_part("w_up", _matmul(v2_t, dup, out_dtype=BF16, bj=FFN // 4, out_slots=True, name="mm_gup"))
    dv2 = _matmul(dgate, p["w_gate"], tb=True, b_slots=True, name="mm_dv2a")
    dv2 = _matmul(dup, p["w_up"], tb=True, b_slots=True, add=dv2, name="mm_dv2b")
    dh1, dh1_b, g_norm2 = _rms_bwd(dv2, h1, p["norm2_w"], dh2, name="rms2_bwd")
    dmerged = _matmul(dh1_b, p["w_out"], tb=True, name="mm_dmerged")
    reduce_part("w_out", _matmul(merged_t, dh1_b, out_dtype=BF16, name="mm_gout").reshape(4, D // 4, D))
    d_attn, d_gla, dproj, g_gla_norm = _merge_bwd(dmerged, attn_o, gla_raw, proj, p["gla_norm_w"])
    dproj, dlogit = _gla_bwd(proj, proj_lr, p["w2p"], p["gate_b"], states, d_gla, dproj)
    dproj, g_sinks = _attn_bwd(proj, p["sinks"], d_attn, dproj)
    g_gate_b = _colsum(dlogit, name="colsum_gate_b")
    g_w2 = _matmul(proj_lr, dlogit, ta=True, name="mm_gw2")
    dproj_lr = _matmul(dlogit, p["w2p"], tb=True, out_dtype=BF16, name="mm_dlr")
    g_in_main = _matmul(u_t, dproj, out_dtype=BF16, name="mm_gin")
    g_in_lr = _matmul(u_t, dproj_lr, out_dtype=BF16, name="mm_gin_lr")
    du = _matmul(dproj, p["w_in_main"], tb=True, name="mm_du")
    du = _matmul(dproj_lr, p["w_in_lr"], tb=True, add=du, name="mm_du_lr")
    dx, _, g_norm1 = _rms_bwd(du, x, p["norm1_w"], dh1, name="rms1_bwd")
    grads = dict(norm1_w=g_norm1, w_in_main=g_in_main, w_in_lr=g_in_lr, w2=g_w2[:RANK], gate_b=g_gate_b,
                 sinks=g_sinks[:, 0].reshape(1, N_Q), gla_norm_w=g_gla_norm, norm2_w=g_norm2, final_norm_w=g_final)
    return loss, dx, grads


def _place():
    x, y, c = lax.axis_index("x"), lax.axis_index("y"), lax.axis_index("c")
    chips = [(1 - x, y), (x, 1 - y), (1 - x, 1 - y)]
    return x, y, c, chips


def _hbm_shape(s, dt):
    return jax.ShapeDtypeStruct(s, dt)


def _handshake(peers):
    barrier = pltpu.get_barrier_semaphore()
    for peer in peers:
        pl.semaphore_signal(barrier, inc=1, device_id=peer, device_id_type=MESH)
    pl.semaphore_wait(barrier, len(peers))


def _launch_copies(body, args, out_shape, sems, *, name, collective_id=None):
    if collective_id is None:
        return pl.pallas_call(
            body, name=name, in_specs=[ANY] * len(args), out_specs=[ANY] * len(out_shape), out_shape=out_shape,
            scratch_shapes=sems)(*args)
    return pl.kernel(
        body, name=name, out_type=out_shape, mesh=plsc.ScalarSubcoreMesh(axis_name="sequencer", num_cores=1),
        scratch_types=sems, compiler_params=pltpu.CompilerParams(collective_id=collective_id))(*args)


def _gather_shards(shards, *, name, collective_id=None):
    n = len(shards)

    def body(*refs):
        ins, outs = refs[:n], refs[n:2 * n]
        ici_send, ici_recv, d2d_send, d2d_recv, local_sem = refs[2 * n:]
        x, y, c, chips = _place()
        me = 2 * x + y
        sibling = (x, y, 1 - c)
        if collective_id is not None:
            _handshake([sibling] + [(*chip, c) for chip in chips])

        def half(w, slot, hc):
            r2 = shards[w].shape[0] // 2
            return outs[w].at[slot, pl.ds(hc * r2, r2), :]

        locals_ = [pltpu.make_async_copy(ins[w], outs[w].at[me], local_sem.at[w]) for w in range(n)]
        for cp in locals_:
            cp.start()
        sends = []
        for w in range(n):
            r2 = shards[w].shape[0] // 2
            for j, chip in enumerate(chips):
                cp = pltpu.make_async_remote_copy(
                    src_ref=ins[w].at[pl.ds(c * r2, r2), :], dst_ref=half(w, me, c),
                    send_sem=ici_send.at[w * 3 + j], recv_sem=ici_recv.at[w * 3 + j],
                    device_id=(*chip, c), device_id_type=MESH)
                cp.start()
                sends.append(cp)
        for w in range(n):
            for j, chip in enumerate(chips):
                slot = 2 * chip[0] + chip[1]
                got = half(w, slot, c)
                pltpu.make_async_remote_copy(
                    src_ref=got, dst_ref=got, send_sem=ici_send.at[w * 3 + j], recv_sem=ici_recv.at[w * 3 + j],
                    device_id=(*chip, c), device_id_type=MESH).wait_recv()
                cp = pltpu.make_async_remote_copy(
                    src_ref=got, dst_ref=got, send_sem=d2d_send.at[w * 3 + j], recv_sem=d2d_recv.at[w * 3 + j],
                    device_id=sibling, device_id_type=MESH)
                cp.start()
                sends.append(cp)
        for w in range(n):
            for j, chip in enumerate(chips):
                slot = 2 * chip[0] + chip[1]
                got = half(w, slot, 1 - c)
                pltpu.make_async_remote_copy(
                    src_ref=got, dst_ref=got, send_sem=d2d_send.at[w * 3 + j], recv_sem=d2d_recv.at[w * 3 + j],
                    device_id=sibling, device_id_type=MESH).wait_recv()
        for cp in sends:
            cp.wait_send()
        for cp in locals_:
            cp.wait()

    return _launch_copies(
        body, shards, [_hbm_shape((4,) + s.shape, s.dtype) for s in shards],
        [pltpu.SemaphoreType.DMA((3 * n,)), pltpu.SemaphoreType.DMA((3 * n,)), pltpu.SemaphoreType.DMA((3 * n,)),
         pltpu.SemaphoreType.DMA((3 * n,)), pltpu.SemaphoreType.DMA((n,))],
        name=name, collective_id=collective_id)


def _pair_blocks(R2):
    return _pick(R2, (256, 352, 128, 64, 32, 16))


def _pair_reduce(part, *, name):
    _, R, C = part.shape
    R2 = R // 2
    rb = _pair_blocks(R2)
    nblk = R2 // rb
    steps = [(s, i) for s in range(4) for i in range(nblk)]
    n = len(steps)

    def body(part_ref, sums_ref, own_buf, oth_buf, rcv_buf, out_buf, own_sem, oth_sem, out_sem, send_sem, recv_sem):
        x, y, c, _ = _place()
        sibling = (x, y, 1 - c)

        def fetch(t, half, buf, sem):
            s, i = steps[t]
            return pltpu.make_async_copy(part_ref.at[s, pl.ds(half * R2 + i * rb, rb), :], buf.at[t % 2], sem.at[t % 2])

        def push(t):
            return pltpu.make_async_remote_copy(
                src_ref=oth_buf.at[t % 2], dst_ref=rcv_buf.at[t % 2], send_sem=send_sem.at[t % 2],
                recv_sem=recv_sem.at[t % 2], device_id=sibling, device_id_type=MESH)

        def store(t):
            s, i = steps[t]
            return pltpu.make_async_copy(out_buf.at[t % 2], sums_ref.at[s, pl.ds(i * rb, rb), :], out_sem.at[t % 2])

        fetch(0, c, own_buf, own_sem).start()
        fetch(0, 1 - c, oth_buf, oth_sem).start()
        for t in range(n):
            if t + 1 < n:
                fetch(t + 1, c, own_buf, own_sem).start()
                fetch(t + 1, 1 - c, oth_buf, oth_sem).start()
            fetch(t, 1 - c, oth_buf, oth_sem).wait()
            p = push(t)
            p.start()
            fetch(t, c, own_buf, own_sem).wait()
            p.wait()
            if t >= 2:
                store(t - 2).wait()
            out_buf[t % 2] = (own_buf[t % 2].astype(F32) + rcv_buf[t % 2].astype(F32)).astype(BF16)
            store(t).start()
        for t in range(max(n - 2, 0), n):
            store(t).wait()

    buf = pltpu.VMEM((2, rb, C), BF16)
    sem2 = pltpu.SemaphoreType.DMA((2,))
    return pl.pallas_call(
        body, name=name, in_specs=[ANY], out_specs=ANY, out_shape=_hbm_shape((4, R2, C), BF16),
        scratch_shapes=[buf, buf, buf, buf, sem2, sem2, sem2, sem2, sem2],
        compiler_params=pltpu.CompilerParams(vmem_limit_bytes=VMEM_LIMIT),
    )(part)


def _chip_exchange(sums, *, name, collective_id=None):
    n = len(sums)

    def body(*refs):
        ins, outs = refs[:n], refs[n:2 * n]
        send_sem, recv_sem, local_sem = refs[2 * n:]
        x, y, c, chips = _place()
        me = 2 * x + y
        if collective_id is not None:
            _handshake([(*chip, c) for chip in chips])
        cps = []
        for w in range(n):
            lc = pltpu.make_async_copy(ins[w].at[me], outs[w].at[me], local_sem.at[w])
            lc.start()
            cps.append(lc)
            for j, chip in enumerate(chips):
                slot = 2 * chip[0] + chip[1]
                rc = pltpu.make_async_remote_copy(
                    src_ref=ins[w].at[slot], dst_ref=outs[w].at[me],
                    send_sem=send_sem.at[w * 3 + j], recv_sem=recv_sem.at[w * 3 + j],
                    device_id=(*chip, c), device_id_type=MESH)
                rc.start()
                cps.append(rc)
        for w in range(n):
            for j, chip in enumerate(chips):
                slot = 2 * chip[0] + chip[1]
                pltpu.make_async_remote_copy(
                    src_ref=ins[w].at[slot], dst_ref=outs[w].at[slot],
                    send_sem=send_sem.at[w * 3 + j], recv_sem=recv_sem.at[w * 3 + j],
                    device_id=(*chip, c), device_id_type=MESH).wait_recv()
        for w in range(n):
            cps[w * 4].wait()
            for j in range(3):
                cps[w * 4 + 1 + j].wait_send()

    return _launch_copies(
        body, sums, [_hbm_shape(s.shape, s.dtype) for s in sums],
        [pltpu.SemaphoreType.DMA((3 * n,)), pltpu.SemaphoreType.DMA((3 * n,)), pltpu.SemaphoreType.DMA((n,))],
        name=name, collective_id=collective_id)


def _sum_join(recv, *, name):
    _, R2, C = recv.shape
    rb = _pair_blocks(R2)
    nblk = R2 // rb

    def body(recv_ref, out_ref, in_buf, acc_buf, in_sem, loc_sem, send_sem, recv_sem):
        x, y, c, _ = _place()
        sibling = (x, y, 1 - c)

        def fetch(t):
            return pltpu.make_async_copy(recv_ref.at[:, pl.ds(t * rb, rb), :], in_buf.at[t % 2], in_sem.at[t % 2])

        def rows(t, half):
            return out_ref.at[pl.ds(half * R2 + t * rb, rb), :]

        def put_local(t):
            return pltpu.make_async_copy(acc_buf.at[t % 2], rows(t, c), loc_sem.at[t])

        def put_remote(t, half):
            return pltpu.make_async_remote_copy(
                src_ref=acc_buf.at[t % 2], dst_ref=rows(t, half), send_sem=send_sem.at[t], recv_sem=recv_sem.at[t],
                device_id=sibling, device_id_type=MESH)

        fetch(0).start()
        for t in range(nblk):
            if t + 1 < nblk:
                fetch(t + 1).start()
            fetch(t).wait()
            if t >= 2:
                put_local(t - 2).wait()
                put_remote(t - 2, c).wait_send()
            acc = in_buf[t % 2, 0].astype(F32)
            for s in range(1, 4):
                acc = acc + in_buf[t % 2, s].astype(F32)
            acc_buf[t % 2] = acc
            put_local(t).start()
            put_remote(t, c).start()
        for t in range(max(nblk - 2, 0), nblk):
            put_local(t).wait()
            put_remote(t, c).wait_send()
        for t in range(nblk):
            put_remote(t, 1 - c).wait_recv()

    semn = pltpu.SemaphoreType.DMA((nblk,))
    return pl.pallas_call(
        body, name=name, in_specs=[ANY], out_specs=ANY, out_shape=_hbm_shape((2 * R2, C), F32),
        scratch_shapes=[pltpu.VMEM((2, 4, rb, C), BF16), pltpu.VMEM((2, rb, C), F32),
                        pltpu.SemaphoreType.DMA((2,)), semn, semn, semn],
        compiler_params=pltpu.CompilerParams(vmem_limit_bytes=VMEM_LIMIT),
    )(recv)


def _sum_small(pack):
    R, C = pack.shape

    def body(in_ref, out_ref, all_ref, send_sem, recv_sem):
        x, y, c, _ = _place()
        me = 4 * x + 2 * y + c
        all_ref[me] = in_ref[...]
        cps = []
        for k in range(1, 8):
            peer = (x ^ (k >> 2), y ^ ((k >> 1) & 1), c ^ (k & 1))
            cp = pltpu.make_async_remote_copy(
                src_ref=in_ref, dst_ref=all_ref.at[me], send_sem=send_sem.at[k - 1], recv_sem=recv_sem.at[k - 1],
                device_id=peer, device_id_type=MESH)
            cp.start()
            cps.append(cp)
        for k in range(1, 8):
            peer = (x ^ (k >> 2), y ^ ((k >> 1) & 1), c ^ (k & 1))
            slot = 4 * peer[0] + 2 * peer[1] + peer[2]
            pltpu.make_async_remote_copy(
                src_ref=in_ref, dst_ref=all_ref.at[slot], send_sem=send_sem.at[k - 1], recv_sem=recv_sem.at[k - 1],
                device_id=peer, device_id_type=MESH).wait_recv()
        for cp in cps:
            cp.wait_send()
        acc = all_ref[0]
        for d in range(1, 8):
            acc = acc + all_ref[d]
        out_ref[...] = acc

    return pl.pallas_call(
        body, name="sum_small", in_specs=[pl.BlockSpec(memory_space=pltpu.VMEM)],
        out_specs=pl.BlockSpec(memory_space=pltpu.VMEM), out_shape=jax.ShapeDtypeStruct((R, C), F32),
        scratch_shapes=[pltpu.VMEM((8, R, C), F32), pltpu.SemaphoreType.DMA((7,)), pltpu.SemaphoreType.DMA((7,))],
    )(pack)


ADAMW_BLOCK_BYTES = 2 * 1024 * 1024


def _adamw_block(R, C):
    padded = -(-C // 128) * 128
    rows = [rb for rb in range(8, R + 1, 8) if R % rb == 0 and rb * padded * 4 <= ADAMW_BLOCK_BYTES]
    if rows or R * padded * 4 <= ADAMW_BLOCK_BYTES:
        return (max(rows) if rows else R), C
    cols = [cb for cb in range(128, C + 1, 128) if C % cb == 0 and R * cb * 4 <= ADAMW_BLOCK_BYTES]
    return R, max(cols)


def _adamw(w, g, m, v, *, name):
    R, C = w.shape
    rb, cb = _adamw_block(R, C)
    c1 = 1.0 / (1.0 - B1 ** STEP)
    c2 = 1.0 / (1.0 - B2 ** STEP)

    def body(w_ref, g_ref, m_ref, v_ref, d_ref, nm_ref, nv_ref):
        gv = g_ref[...]
        m2 = B1 * m_ref[...] + (1.0 - B1) * gv
        v2 = B2 * v_ref[...] + (1.0 - B2) * gv * gv
        nm_ref[...] = m2
        nv_ref[...] = v2
        d_ref[...] = -LR * ((m2 * c1) / (jnp.sqrt(v2 * c2) + ADAM_EPS) + WD * w_ref[...])

    blk = pl.BlockSpec((rb, cb), lambda i, j: (i, j))
    return pl.pallas_call(
        body, name=name, grid=(R // rb, C // cb), in_specs=[blk] * 4, out_specs=[blk] * 3,
        out_shape=[jax.ShapeDtypeStruct((R, C), F32)] * 3, compiler_params=_params(2),
    )(w, g, m, v)


SMALL = (("norm1_w", D), ("norm2_w", D), ("final_norm_w", D), ("gate_b", GLA_H * DK), ("gla_norm_w", DV), ("sinks", N_Q))
PACK_W = 1024


def _pack_small(vals, w2, loss):
    rows = []
    for name, width in SMALL:
        v = vals[name].reshape(-1)
        rows.append(jnp.pad(v, (0, (-width) % PACK_W)).reshape(-1, PACK_W))
    rows.append(w2)
    rows.append(jnp.broadcast_to(loss.reshape(1, 1), (1, PACK_W)))
    pack = jnp.concatenate(rows, axis=0)
    return jnp.pad(pack, ((0, 32 - pack.shape[0]), (0, 0)))


def _unpack_small(pack):
    out, r = {}, 0
    for name, width in SMALL:
        nr = -(-width // PACK_W)
        out[name] = pack[r:r + nr].reshape(-1)[:width]
        r += nr
    out["w2"] = pack[r:r + RANK]
    out["loss"] = pack[r + RANK, 0]
    return out


def kernel(x, norm1_w, w_in, gla_gate_w2, gla_gate_b, attn_sinks, gla_norm_w, w_out, norm2_w, w_ffn_gate, w_ffn_up, w_ffn_down, final_norm_w, loss_target, m_norm1_w, m_w_in, m_gla_gate_w2, m_gla_gate_b, m_attn_sinks, m_gla_norm_w, m_w_out, m_norm2_w, m_w_ffn_gate, m_w_ffn_up, m_w_ffn_down, m_final_norm_w, v_norm1_w, v_w_in, v_gla_gate_w2, v_gla_gate_b, v_attn_sinks, v_gla_norm_w, v_w_out, v_norm2_w, v_w_ffn_gate, v_w_ffn_up, v_w_ffn_down, v_final_norm_w):
    chip = 2 * lax.axis_index("x") + lax.axis_index("y")
    w_in_s, w_out_s, w_gate_s, w_up_s, w_down_s, w2_s = (
        w_in[0], w_out[0], w_ffn_gate[0], w_ffn_up[0], w_ffn_down[0], gla_gate_w2[0])
    CS = w_in_s.shape[1]

    g_in, g_w2 = _gather_shards([w_in_s.astype(BF16), w2_s], name="gather_in")
    g_out, g_gate, g_up, g_down = _gather_shards(
        [w_out_s.astype(BF16), w_gate_s.astype(BF16), w_up_s.astype(BF16), w_down_s.astype(BF16)],
        name="gather_rest", collective_id=1)
    w_main, w_lr = _to_main(jnp.transpose(g_in, (1, 0, 2)).reshape(D, 4 * CS))
    w2_full = jnp.transpose(g_w2, (1, 0, 2)).reshape(RANK, GLA_H * DK)
    p = dict(
        norm1_w=norm1_w, norm2_w=norm2_w, final_norm_w=final_norm_w.reshape(1, D), gate_b=gla_gate_b,
        gla_norm_w=gla_norm_w, sinks=jnp.pad(attn_sinks, ((0, 0), (0, 128 - N_Q))),
        w_in_main=w_main, w_in_lr=jnp.pad(w_lr, ((0, 0), (0, LRP - RANK))),
        w2p=jnp.pad(w2_full, ((0, LRP - RANK), (0, 0))).astype(BF16),
        w_out=g_out.reshape(D, D),
        w_gate=g_gate, w_up=g_up, w_down=g_down.reshape(FFN, D),
    )

    tags = ["w_in", "w_out", "w_gate", "w_up", "w_down"]
    recv = {}

    def reduce_part(tag, part):
        sums = _pair_reduce(part, name="pair_reduce_" + tag)
        recv[tag], = _chip_exchange([sums], name="chip_exchange_" + tag, collective_id=2 + tags.index(tag))

    loss_blk, dx, g = _local_step(x[0], loss_target[0], p, reduce_part)
    gin_full = _from_main(g["w_in_main"], g["w_in_lr"][:, :RANK])
    reduce_part("w_in", jnp.transpose(gin_full.reshape(D, 4, CS), (1, 0, 2)))
    big = [_sum_join(recv[tg], name="sum_join_" + tg) for tg in tags]

    small = _unpack_small(_sum_small(_pack_small(g, g["w2"], loss_blk[0, 0])))
    loss = small["loss"]
    g_w2_mine = lax.dynamic_slice_in_dim(small["w2"], chip * (GLA_H * DK // 4), GLA_H * DK // 4, axis=1)

    grads = dict(
        norm1_w=small["norm1_w"].reshape(1, D), w_in=big[0], gla_gate_w2=g_w2_mine,
        gla_gate_b=small["gate_b"].reshape(1, -1), attn_sinks=small["sinks"].reshape(1, N_Q),
        gla_norm_w=small["gla_norm_w"].reshape(1, DV), w_out=big[1], norm2_w=small["norm2_w"].reshape(1, D),
        w_ffn_gate=big[2], w_ffn_up=big[3], w_ffn_down=big[4], final_norm_w=small["final_norm_w"].reshape(1, D))
    def lin(a):
        return jnp.transpose(a.reshape(D, CS))

    def unlin(a):
        return jnp.transpose(a)

    grads["w_in"] = lin(big[0])
    weights = dict(
        norm1_w=(norm1_w, m_norm1_w, v_norm1_w), w_in=(lin(w_in), lin(m_w_in), lin(v_w_in)),
        gla_gate_w2=(w2_s, m_gla_gate_w2[0], v_gla_gate_w2[0]), gla_gate_b=(gla_gate_b, m_gla_gate_b, v_gla_gate_b),
        attn_sinks=(attn_sinks, m_attn_sinks, v_attn_sinks), gla_norm_w=(gla_norm_w, m_gla_norm_w, v_gla_norm_w),
        w_out=(w_out_s, m_w_out[0], v_w_out[0]), norm2_w=(norm2_w, m_norm2_w, v_norm2_w),
        w_ffn_gate=(w_gate_s, m_w_ffn_gate[0], v_w_ffn_gate[0]), w_ffn_up=(w_up_s, m_w_ffn_up[0], v_w_ffn_up[0]),
        w_ffn_down=(w_down_s, m_w_ffn_down[0], v_w_ffn_down[0]),
        final_norm_w=(final_norm_w.reshape(1, D), m_final_norm_w.reshape(1, D), v_final_norm_w.reshape(1, D)))
    names = ["norm1_w", "w_in", "gla_gate_w2", "gla_gate_b", "attn_sinks", "gla_norm_w", "w_out", "norm2_w",
             "w_ffn_gate", "w_ffn_up", "w_ffn_down", "final_norm_w"]
    lead = {"norm1_w": False, "gla_gate_b": False, "attn_sinks": False, "gla_norm_w": False, "norm2_w": False}
    g_out_l, d_out, m_out, v_out = [], [], [], []
    for nm in names:
        w, m, v = weights[nm]
        gr = grads[nm]
        dl, nmn, nvn = _adamw(w, gr, m, v, name="adamw_" + nm)
        if nm == "w_in":
            gr, dl, nmn, nvn, w = big[0], unlin(dl), unlin(nmn), unlin(nvn), w_in_s
        if nm == "final_norm_w":
            shape = (D,)
        elif nm in lead:
            shape = w.shape
        else:
            shape = (1,) + w.shape
        g_out_l.append(gr.reshape(shape))
        d_out.append(dl.reshape(shape))
        m_out.append(nmn.reshape(shape))
        v_out.append(nvn.reshape(shape))
    return (loss, dx[None], *g_out_l, *d_out, *m_out, *v_out)
```

```python
import functools

import jax
import jax.numpy as jnp
from jax import lax
from jax.experimental import pallas as pl
from jax.experimental.pallas import tpu as pltpu
from jax.experimental.pallas import tpu_sc as plsc

F32 = jnp.float32
BF16 = jnp.bfloat16

D = 2048
HEAD_DIM = 64
N_Q = 32
N_KV = 4
GROUP = 8
WINDOW = 128
GLA_H = 4
DK = 256
DV = 512
RANK = 16
CHUNK = 64
FFN = 5632
EPS = 1e-6
MASK_VALUE = -1e30
GATE_NORM = 16.0
LR, B1, B2, ADAM_EPS, WD, STEP = 0.001, 0.9, 0.999, 1e-08, 0.01, 10

MAIN = 12800
O_MERGE, O_GLA, O_ATTN = 0, 6144, 10240
W_MERGE, W_GLA, W_ATTN = 3 * DV, 2 * DK + DV, D + 2 * N_KV * HEAD_DIM
LR_AT = 6656
LRP = 128


def _main_pieces():
    o_gq, o_gk, o_gv, o_gr, o_ga, o_gb = 2560, 3584, 4608, LR_AT + RANK, LR_AT + RANK + D, LR_AT + RANK + 2 * D
    pieces = []
    for h in range(GLA_H):
        pieces += [(o_gr + DV * h, DV), (o_ga + DV * h, DV), (o_gb + DV * h, DV)]
    for h in range(GLA_H):
        pieces += [(o_gq + DK * h, DK), (o_gk + DK * h, DK), (o_gv + DV * h, DV)]
    pieces.append((0, W_ATTN))
    return pieces


SHARD_COLS = 3204

VMEM_LIMIT = 56 * 1024 * 1024
MESH = pl.DeviceIdType.MESH
ANY = pl.BlockSpec(memory_space=pl.ANY)


def _params(ngrid):
    return pltpu.CompilerParams(dimension_semantics=("arbitrary",) * ngrid, vmem_limit_bytes=VMEM_LIMIT)


def _shard_segments(lo, hi):
    segs = []
    while lo < hi:
        s = lo // SHARD_COLS
        e = min(hi, (s + 1) * SHARD_COLS)
        segs.append((s, lo - s * SHARD_COLS, e - s * SHARD_COLS))
        lo = e
    return segs


def _lanes(pieces):
    return pieces[0] if len(pieces) == 1 else jnp.concatenate(pieces, axis=1)


def _w_in_to_main(shards):
    rb = 256

    def body(g_ref, main_ref, lr_ref):
        at = 0
        for a, w in _main_pieces():
            main_ref[:, at:at + w] = _lanes([g_ref[s, :, lo:hi] for s, lo, hi in _shard_segments(a, a + w)])
            at += w
        lr = [g_ref[s, :, lo:hi] for s, lo, hi in _shard_segments(LR_AT, LR_AT + RANK)]
        lr_ref[...] = _lanes(lr + [jnp.zeros((rb, LRP - RANK), lr_ref.dtype)])

    return pl.pallas_call(
        body, name="w_in_to_main", grid=(D // rb,),
        in_specs=[pl.BlockSpec((4, rb, SHARD_COLS), lambda i: (0, i, 0))],
        out_specs=[pl.BlockSpec((rb, MAIN), lambda i: (i, 0)), pl.BlockSpec((rb, LRP), lambda i: (i, 0))],
        out_shape=[jax.ShapeDtypeStruct((D, MAIN), shards.dtype), jax.ShapeDtypeStruct((D, LRP), shards.dtype)],
        compiler_params=_params(1),
    )(shards)


def _main_to_shards(g_main, g_lr):
    rb = 256
    at, sources = 0, [(LR_AT, RANK, 1, 0)]
    for a, w in _main_pieces():
        sources.append((a, w, 0, at))
        at += w
    sources.sort()

    def body(main_ref, lr_ref, out_ref):
        refs = (main_ref, lr_ref)
        for s in range(4):
            lo, hi = s * SHARD_COLS, (s + 1) * SHARD_COLS
            pieces = []
            for a, w, which, src in sources:
                b, e = max(a, lo), min(a + w, hi)
                if b < e:
                    pieces.append(refs[which][:, src + b - a:src + e - a])
            out_ref[s] = _lanes(pieces)

    return pl.pallas_call(
        body, name="main_to_shards", grid=(D // rb,),
        in_specs=[pl.BlockSpec((rb, MAIN), lambda i: (i, 0)), pl.BlockSpec((rb, LRP), lambda i: (i, 0))],
        out_specs=pl.BlockSpec((4, rb, SHARD_COLS), lambda i: (0, i, 0)),
        out_shape=jax.ShapeDtypeStruct((4, D, SHARD_COLS), g_main.dtype), compiler_params=_params(1),
    )(g_main, g_lr)


def _pick(n, cands):
    for c in cands:
        if n % c == 0:
            return c
    return n


def _sigmoid(x):
    return 1.0 / (1.0 + jnp.exp(-x))


def _dot(a, b, ca, cb):
    return lax.dot_general(a, b, (((ca,), (cb,)), ((), ())), preferred_element_type=F32)


def _matmul(a, b, *, name, ta=False, tb=False, add=None, out_dtype=F32, bm=None, bj=None, bc=None,
            b_slots=False, out_slots=False):
    C, M = a.shape if ta else a.shape[::-1]
    if b_slots:
        if tb:
            J, bc = b.shape[1], b.shape[2]
            assert b.shape[0] * bc == C
        else:
            bj = b.shape[2]
            J = b.shape[0] * bj
            assert b.shape[1] == C
    else:
        J = b.shape[0] if tb else b.shape[1]
        assert (b.shape[1] if tb else b.shape[0]) == C
    bm = bm or _pick(M, (1024, 512, 256, 128) if C <= FFN else (512, 256, 128))
    bj = bj or _pick(J, (1280, 1024, 512, 256, 128) if C <= D else (512, 256, 128))
    bc = bc or (C if C <= FFN else _pick(C, (6400,)))
    nm, nj, nc = M // bm, J // bj, C // bc
    has_add = add is not None

    def body(*refs):
        a_ref, b_ref = refs[0], refs[1]
        add_ref = refs[2] if has_add else None
        o_ref = refs[3] if has_add else refs[2]
        p = _dot(a_ref[...].astype(BF16), b_ref[...].astype(BF16), 0 if ta else 1, 1 if tb else 0)

        def finish(acc):
            if has_add:
                acc = acc + add_ref[...]
            o_ref[...] = acc.astype(o_ref.dtype)

        if nc == 1:
            finish(p)
        else:
            acc_ref = refs[-1]
            c = pl.program_id(2)

            @pl.when(c == 0)
            def _():
                acc_ref[...] = p

            @pl.when(c > 0)
            def _():
                acc_ref[...] += p

            @pl.when(c == nc - 1)
            def _():
                finish(acc_ref[...])

    a_spec = pl.BlockSpec((bc, bm), lambda m, j, c: (c, m)) if ta else pl.BlockSpec((bm, bc), lambda m, j, c: (m, c))
    if b_slots:
        b_spec = (pl.BlockSpec((None, bj, bc), lambda m, j, c: (c, j, 0)) if tb
                  else pl.BlockSpec((None, bc, bj), lambda m, j, c: (j, c, 0)))
    else:
        b_spec = (pl.BlockSpec((bj, bc), lambda m, j, c: (j, c)) if tb
                  else pl.BlockSpec((bc, bj), lambda m, j, c: (c, j)))
    if out_slots:
        assert not has_add
        o_spec = pl.BlockSpec((None, bm, bj), lambda m, j, c: (j, m, 0))
        out_shape = jax.ShapeDtypeStruct((nj, M, bj), out_dtype)
    else:
        o_spec = pl.BlockSpec((bm, bj), lambda m, j, c: (m, j))
        out_shape = jax.ShapeDtypeStruct((M, J), out_dtype)
    in_specs = [a_spec, b_spec] + ([o_spec] if has_add else [])
    args = (a, b) + ((add,) if has_add else ())
    return pl.pallas_call(
        body, name=name, grid=(nm, nj, nc), in_specs=in_specs, out_specs=o_spec,
        out_shape=out_shape,
        scratch_shapes=[pltpu.VMEM((bm, bj), F32)] if nc > 1 else [],
        compiler_params=_params(3),
    )(*args)


def _rows(T):
    return _pick(T, (256, 128))


def _rms_fwd(x, w, *, name):
    T = x.shape[0]
    rb = _rows(T)

    def body(x_ref, w_ref, u_ref, ut_ref):
        xv = x_ref[...]
        r = lax.rsqrt(jnp.mean(xv * xv, axis=-1, keepdims=True) + EPS)
        u = xv * r * w_ref[...]
        u_ref[...] = u.astype(BF16)
        ut_ref[...] = u.T.astype(BF16)

    return pl.pallas_call(
        body, name=name, grid=(T // rb,),
        in_specs=[pl.BlockSpec((rb, D), lambda i: (i, 0)), pl.BlockSpec((1, D), lambda i: (0, 0))],
        out_specs=[pl.BlockSpec((rb, D), lambda i: (i, 0)), pl.BlockSpec((D, rb), lambda i: (0, i))],
        out_shape=[jax.ShapeDtypeStruct((T, D), BF16), jax.ShapeDtypeStruct((D, T), BF16)], compiler_params=_params(1),
    )(x, w)


def _rms_bwd(dy, h, w, resid, *, name):
    T = h.shape[0]
    rb = _rows(T)

    def body(dy_ref, h_ref, w_ref, res_ref, dh_ref, dhb_ref, dw_ref):
        @pl.when(pl.program_id(0) == 0)
        def _():
            dw_ref[...] = jnp.zeros_like(dw_ref)

        hv = h_ref[...]
        r = lax.rsqrt(jnp.mean(hv * hv, axis=-1, keepdims=True) + EPS)
        hn = hv * r
        dyv = dy_ref[...]
        dw_ref[...] += jnp.sum(dyv * hn, axis=0, keepdims=True)
        t = dyv * w_ref[...]
        dh = res_ref[...] + r * (t - hn * jnp.mean(t * hn, axis=-1, keepdims=True))
        dh_ref[...] = dh
        dhb_ref[...] = dh.astype(BF16)

    blk = pl.BlockSpec((rb, D), lambda i: (i, 0))
    vec = pl.BlockSpec((1, D), lambda i: (0, 0))
    return pl.pallas_call(
        body, name=name, grid=(T // rb,), in_specs=[blk, blk, vec, blk], out_specs=[blk, blk, vec],
        out_shape=[jax.ShapeDtypeStruct((T, D), F32), jax.ShapeDtypeStruct((T, D), BF16),
                   jax.ShapeDtypeStruct((1, D), F32)],
        compiler_params=_params(1),
    )(dy, h, w, resid)


def _final_loss(h, w, target):
    T = h.shape[0]
    rb = _rows(T)

    def body(h_ref, w_ref, t_ref, loss_ref, dh_ref, dhb_ref, dw_ref):
        @pl.when(pl.program_id(0) == 0)
        def _():
            dw_ref[...] = jnp.zeros_like(dw_ref)
            loss_ref[...] = jnp.zeros_like(loss_ref)

        hv = h_ref[...]
        wv = w_ref[...]
        r = lax.rsqrt(jnp.mean(hv * hv, axis=-1, keepdims=True) + EPS)
        hn = hv * r
        e = hn * wv - t_ref[...]
        row = jnp.sum(e * e, axis=-1, keepdims=True) * (0.5 / D)
        loss_ref[...] += jnp.broadcast_to(jnp.sum(row, axis=0, keepdims=True), loss_ref.shape)
        dy = e * (1.0 / D)
        dw_ref[...] += jnp.sum(dy * hn, axis=0, keepdims=True)
        t = dy * wv
        dh = r * (t - hn * jnp.mean(t * hn, axis=-1, keepdims=True))
        dh_ref[...] = dh
        dhb_ref[...] = dh.astype(BF16)

    blk = pl.BlockSpec((rb, D), lambda i: (i, 0))
    vec = pl.BlockSpec((1, D), lambda i: (0, 0))
    return pl.pallas_call(
        body, name="final_loss", grid=(T // rb,), in_specs=[blk, vec, blk],
        out_specs=[pl.BlockSpec((8, 128), lambda i: (0, 0)), blk, blk, vec],
        out_shape=[jax.ShapeDtypeStruct((8, 128), F32), jax.ShapeDtypeStruct((T, D), F32),
                   jax.ShapeDtypeStruct((T, D), BF16), jax.ShapeDtypeStruct((1, D), F32)],
        compiler_params=_params(1),
    )(h, w, target)


def _swiglu_fwd(gate, up):
    T = gate.shape[0]
    rb = _rows(T)

    def body(g_ref, u_ref, o_ref, ot_ref):
        g = g_ref[...]
        ff = g * _sigmoid(g) * u_ref[...]
        o_ref[...] = ff.astype(BF16)
        ot_ref[...] = ff.T.astype(BF16)

    blk = pl.BlockSpec((rb, 512), lambda i, j: (i, j))
    return pl.pallas_call(
        body, name="swiglu_fwd", grid=(T // rb, FFN // 512), in_specs=[blk, blk],
        out_specs=[blk, pl.BlockSpec((512, rb), lambda i, j: (j, i))],
        out_shape=[jax.ShapeDtypeStruct((T, FFN), BF16), jax.ShapeDtypeStruct((FFN, T), BF16)],
        compiler_params=_params(2),
    )(gate, up)


def _swiglu_bwd(dff, gate, up):
    T = gate.shape[0]
    rb = _rows(T)

    def body(d_ref, g_ref, u_ref, dg_ref, du_ref):
        g = g_ref[...]
        d = d_ref[...]
        sg = _sigmoid(g)
        du_ref[...] = (d * g * sg).astype(BF16)
        dg_ref[...] = (d * u_ref[...] * sg * (1.0 + g * (1.0 - sg))).astype(BF16)

    blk = pl.BlockSpec((rb, 512), lambda i, j: (i, j))
    return pl.pallas_call(
        body, name="swiglu_bwd", grid=(T // rb, FFN // 512), in_specs=[blk, blk, blk], out_specs=[blk, blk],
        out_shape=[jax.ShapeDtypeStruct((T, FFN), BF16)] * 2, compiler_params=_params(2),
    )(dff, gate, up)


def _colsum(x, *, name):
    T, W = x.shape
    rb = _rows(T)

    def body(x_ref, o_ref):
        @pl.when(pl.program_id(0) == 0)
        def _():
            o_ref[...] = jnp.zeros_like(o_ref)

        o_ref[...] += jnp.sum(x_ref[...], axis=0, keepdims=True)

    return pl.pallas_call(
        body, name=name, grid=(T // rb,), in_specs=[pl.BlockSpec((rb, W), lambda i: (i, 0))],
        out_specs=pl.BlockSpec((1, W), lambda i: (0, 0)), out_shape=jax.ShapeDtypeStruct((1, W), F32),
        compiler_params=_params(1),
    )(x)


def _merge_fwd(attn_o, gla_raw, proj, gla_norm_w):
    T = attn_o.shape[0]
    rb = _rows(T)

    def body(a_ref, g_ref, gr_ref, ga_ref, gb_ref, w_ref, o_ref, ot_ref):
        graw = g_ref[...]
        r = lax.rsqrt(jnp.mean(graw * graw, axis=-1, keepdims=True) + EPS)
        gr = gr_ref[...]
        go = graw * r * w_ref[...] * (gr * _sigmoid(gr))
        merged = _sigmoid(ga_ref[...]) * a_ref[...] + _sigmoid(gb_ref[...]) * go
        o_ref[...] = merged.astype(BF16)
        ot_ref[...] = merged.T.astype(BF16)

    def sec(k):
        return pl.BlockSpec((rb, DV), lambda i, h: (i, O_MERGE // DV + 3 * h + k))

    blk = pl.BlockSpec((rb, DV), lambda i, h: (i, h))
    return pl.pallas_call(
        body, name="merge_fwd", grid=(T // rb, GLA_H),
        in_specs=[blk, blk, sec(0), sec(1), sec(2), pl.BlockSpec((1, DV), lambda i, h: (0, 0))],
        out_specs=[blk, pl.BlockSpec((DV, rb), lambda i, h: (h, i))],
        out_shape=[jax.ShapeDtypeStruct((T, D), BF16), jax.ShapeDtypeStruct((D, T), BF16)], compiler_params=_params(2),
    )(attn_o, gla_raw, proj, proj, proj, gla_norm_w)


def _merge_bwd(dm, attn_o, gla_raw, proj, gla_norm_w):
    T = attn_o.shape[0]
    rb = _rows(T)

    def body(dm_ref, a_ref, g_ref, gr_ref, ga_ref, gb_ref, w_ref, da_ref, dg_ref, dp_ref, dw_ref):
        @pl.when((pl.program_id(0) == 0) & (pl.program_id(1) == 0))
        def _():
            dw_ref[...] = jnp.zeros_like(dw_ref)

        dmv = dm_ref[...]
        av = a_ref[...]
        graw = g_ref[...]
        gr = gr_ref[...]
        wv = w_ref[...]
        sa = _sigmoid(ga_ref[...])
        sb = _sigmoid(gb_ref[...])
        r = lax.rsqrt(jnp.mean(graw * graw, axis=-1, keepdims=True) + EPS)
        gnh = graw * r
        gn = gnh * wv
        sr = _sigmoid(gr)
        sl = gr * sr
        go = gn * sl
        da_ref[...] = dmv * sa
        dgo = dmv * sb
        dp_ref[:, 0:DV] = (dgo * gn * sr * (1.0 + gr * (1.0 - sr))).astype(BF16)
        dp_ref[:, DV:2 * DV] = (dmv * av * sa * (1.0 - sa)).astype(BF16)
        dp_ref[:, 2 * DV:3 * DV] = (dmv * go * sb * (1.0 - sb)).astype(BF16)
        dgn = dgo * sl
        dw_ref[...] += jnp.sum(dgn * gnh, axis=0, keepdims=True)
        t = dgn * wv
        dg_ref[...] = r * (t - gnh * jnp.mean(t * gnh, axis=-1, keepdims=True))

    def sec(k):
        return pl.BlockSpec((rb, DV), lambda i, h: (i, O_MERGE // DV + 3 * h + k))

    blk = pl.BlockSpec((rb, DV), lambda i, h: (i, h))
    vec = pl.BlockSpec((1, DV), lambda i, h: (0, 0))
    return pl.pallas_call(
        body, name="merge_bwd", grid=(T // rb, GLA_H),
        in_specs=[blk, blk, blk, sec(0), sec(1), sec(2), vec],
        out_specs=[blk, blk, pl.BlockSpec((rb, W_MERGE), lambda i, h: (i, O_MERGE // W_MERGE + h)), vec],
        out_shape=[jax.ShapeDtypeStruct((T, D), F32), jax.ShapeDtypeStruct((T, D), F32),
                   jax.ShapeDtypeStruct((T, MAIN), BF16), jax.ShapeDtypeStruct((1, DV), F32)],
        compiler_params=_params(2),
    )(dm, attn_o, gla_raw, proj, proj, proj, gla_norm_w)


def _attn_mask(n):
    qi = lax.broadcasted_iota(jnp.int32, (GROUP * WINDOW, 2 * WINDOW), 0) & (WINDOW - 1)
    ki = lax.broadcasted_iota(jnp.int32, (GROUP * WINDOW, 2 * WINDOW), 1)
    rel = qi + WINDOW - ki
    return (rel >= 0) & (rel < WINDOW) & ((n > 0) | (ki >= WINDOW))


def _stack_heads(ref, h):
    return jnp.concatenate(
        [ref[:, (h * GROUP + g) * HEAD_DIM:(h * GROUP + g + 1) * HEAD_DIM] for g in range(GROUP)], axis=0).astype(BF16)


def _stack_sinks(s_ref, h):
    return jnp.concatenate(
        [jnp.broadcast_to(s_ref[:, h * GROUP + g:h * GROUP + g + 1], (WINDOW, 1)) for g in range(GROUP)], axis=0)


def _attn_probs(q, kk, mask, sink):
    s = _dot(q, kk, 1, 1) * (HEAD_DIM ** -0.5)
    s = jnp.where(mask, s, MASK_VALUE)
    m = jnp.maximum(jnp.max(s, axis=-1, keepdims=True), sink)
    e = jnp.exp(s - m)
    es = jnp.exp(sink - m)
    inv = 1.0 / (jnp.sum(e, axis=-1, keepdims=True) + es)
    return e * inv, es * inv


def _attn_specs(nb, rev):
    def at(n):
        return (nb - 1 - n) if rev else n

    kcol, vcol = (O_ATTN + D) // 256, (O_ATTN + D) // 256 + 1
    q = pl.BlockSpec((WINDOW, D), lambda n: (at(n), O_ATTN // D))
    kc = pl.BlockSpec((WINDOW, 256), lambda n: (at(n), kcol))
    kp = pl.BlockSpec((WINDOW, 256), lambda n: (jnp.maximum(at(n) - 1, 0), kcol))
    vc = pl.BlockSpec((WINDOW, 256), lambda n: (at(n), vcol))
    vp = pl.BlockSpec((WINDOW, 256), lambda n: (jnp.maximum(at(n) - 1, 0), vcol))
    sk = pl.BlockSpec((1, 128), lambda n: (0, 0))
    o = pl.BlockSpec((WINDOW, D), lambda n: (at(n), 0))
    return q, kc, kp, vc, vp, sk, o


def _attn_fwd(proj, sinks):
    T = proj.shape[0]
    nb = T // WINDOW

    def body(q_ref, kc_ref, kp_ref, vc_ref, vp_ref, s_ref, o_ref):
        mask = _attn_mask(pl.program_id(0))
        for h in range(N_KV):
            hs = slice(h * HEAD_DIM, (h + 1) * HEAD_DIM)
            kk = jnp.concatenate([kp_ref[:, hs], kc_ref[:, hs]], axis=0).astype(BF16)
            vv = jnp.concatenate([vp_ref[:, hs], vc_ref[:, hs]], axis=0).astype(BF16)
            p, _ = _attn_probs(_stack_heads(q_ref, h), kk, mask, _stack_sinks(s_ref, h))
            o = _dot(p.astype(BF16), vv, 1, 0)
            for g in range(GROUP):
                hg = h * GROUP + g
                o_ref[:, hg * HEAD_DIM:(hg + 1) * HEAD_DIM] = o[g * WINDOW:(g + 1) * WINDOW]

    q, kc, kp, vc, vp, sk, o = _attn_specs(nb, False)
    return pl.pallas_call(
        body, name="attn_fwd", grid=(nb,), in_specs=[q, kc, kp, vc, vp, sk], out_specs=o,
        out_shape=jax.ShapeDtypeStruct((T, D), F32), compiler_params=_params(1),
    )(proj, proj, proj, proj, proj, sinks)


def _attn_bwd(proj, sinks, d_o, dproj):
    T = proj.shape[0]
    nb = T // WINDOW
    kat, vat = D, D + N_KV * HEAD_DIM

    def body(q_ref, kc_ref, kp_ref, vc_ref, vp_ref, s_ref, do_ref, _, dp_ref, ds_ref, ck_ref, cv_ref):
        i = pl.program_id(0)
        n = nb - 1 - i

        @pl.when(i == 0)
        def _():
            ck_ref[...] = jnp.zeros_like(ck_ref)
            cv_ref[...] = jnp.zeros_like(cv_ref)
            ds_ref[...] = jnp.zeros_like(ds_ref)

        mask = _attn_mask(n)
        for h in range(N_KV):
            hs = slice(h * HEAD_DIM, (h + 1) * HEAD_DIM)
            kk = jnp.concatenate([kp_ref[:, hs], kc_ref[:, hs]], axis=0).astype(BF16)
            vv = jnp.concatenate([vp_ref[:, hs], vc_ref[:, hs]], axis=0).astype(BF16)
            q = _stack_heads(q_ref, h)
            do = _stack_heads(do_ref, h)
            p, ps = _attn_probs(q, kk, mask, _stack_sinks(s_ref, h))
            dp = _dot(do, vv, 1, 1)
            delta = jnp.sum(p * dp, axis=-1, keepdims=True)
            dsc = (p * (dp - delta) * (HEAD_DIM ** -0.5)).astype(BF16)
            dsink = -ps * delta
            dq = _dot(dsc, kk, 1, 0).astype(BF16)
            for g in range(GROUP):
                hg = h * GROUP + g
                rows = slice(g * WINDOW, (g + 1) * WINDOW)
                ds_ref[hg:hg + 1, :] += jnp.broadcast_to(jnp.sum(dsink[rows], axis=0, keepdims=True), (1, 128))
                dp_ref[:, hg * HEAD_DIM:(hg + 1) * HEAD_DIM] = dq[rows]
            dkk = _dot(dsc, q, 0, 0)
            dvv = _dot(p.astype(BF16), do, 0, 0)
            dp_ref[:, kat + h * HEAD_DIM:kat + (h + 1) * HEAD_DIM] = (dkk[WINDOW:] + ck_ref[:, hs]).astype(BF16)
            dp_ref[:, vat + h * HEAD_DIM:vat + (h + 1) * HEAD_DIM] = (dvv[WINDOW:] + cv_ref[:, hs]).astype(BF16)
            ck_ref[:, hs] = dkk[:WINDOW]
            cv_ref[:, hs] = dvv[:WINDOW]

    q, kc, kp, vc, vp, sk, o = _attn_specs(nb, True)
    return pl.pallas_call(
        body, name="attn_bwd", grid=(nb,), in_specs=[q, kc, kp, vc, vp, sk, o, ANY],
        out_specs=[pl.BlockSpec((WINDOW, W_ATTN), lambda n: (nb - 1 - n, O_ATTN // W_ATTN)),
                   pl.BlockSpec((N_Q, 128), lambda n: (0, 0))],
        out_shape=[jax.ShapeDtypeStruct((T, MAIN), BF16), jax.ShapeDtypeStruct((N_Q, 128), F32)],
        scratch_shapes=[pltpu.VMEM((WINDOW, 256), F32), pltpu.VMEM((WINDOW, 256), F32)],
        input_output_aliases={7: 0}, compiler_params=_params(1),
    )(proj, proj, proj, proj, proj, sinks, d_o, dproj)


def _split3(x):
    hi = x.astype(BF16)
    r1 = x - hi.astype(F32)
    mid = r1.astype(BF16)
    lo = (r1 - mid.astype(F32)).astype(BF16)
    return hi, mid, lo


def _tri_sum(tri, x):
    hi, mid, lo = _split3(x)
    return _dot(tri, hi, 1, 0) + _dot(tri, mid, 1, 0) + _dot(tri, lo, 1, 0)


def _gla_chunk(q, k, lr, w2, b, lower):
    logit = _dot(lr.astype(BF16), w2.astype(BF16), 1, 0) + b
    la = (jnp.minimum(logit, 0.0) - jnp.log(1.0 + jnp.exp(-jnp.abs(logit)))) * (1.0 / GATE_NORM)
    g = _tri_sum(lower, la)
    gl = g[CHUNK - 1:CHUNK, :]
    eg = jnp.exp(g)
    qd = q * (DK ** -0.5) * eg
    ki = k * jnp.exp(-g)
    ke = k * jnp.exp(gl - g)
    return logit, g, gl, eg, qd, ki, ke


def _tri(lower):
    r = lax.broadcasted_iota(jnp.int32, (CHUNK, CHUNK), 0)
    c = lax.broadcasted_iota(jnp.int32, (CHUNK, CHUNK), 1)
    return (r >= c) if lower else (r <= c)


def _gla_rows(T):
    return _pick(T, (256, 128, 64))


def _gla_fwd(proj, proj_lr, w2p, gate_b):
    T = proj.shape[0]
    rb = _gla_rows(T)
    per = rb // CHUNK

    def body(q_ref, k_ref, v_ref, lr_ref, w2_ref, b_ref, o_ref, st_ref, s_scr):
        @pl.when(pl.program_id(1) == 0)
        def _():
            s_scr[...] = jnp.zeros_like(s_scr)

        low = _tri(True)
        lower = low.astype(BF16)
        for i in range(per):
            rows = slice(i * CHUNK, (i + 1) * CHUNK)
            _, g, gl, eg, qd, ki, ke = _gla_chunk(q_ref[rows, :], k_ref[rows, :], lr_ref[rows, :], w2_ref[...],
                                                  b_ref[...], lower)
            v = v_ref[rows, :].astype(BF16)
            qdb = qd.astype(BF16)
            att = jnp.where(low, _dot(qdb, ki.astype(BF16), 1, 1), 0.0)
            st = s_scr[...]
            st_ref[0, i] = st
            o_ref[rows, :] = _dot(att.astype(BF16), v, 1, 0) + _dot(qdb, st.astype(BF16), 1, 1)
            s_scr[...] = st * jnp.exp(gl) + _dot(v, ke.astype(BF16), 0, 0)

    return pl.pallas_call(
        body, name="gla_fwd", grid=(GLA_H, T // rb),
        in_specs=[pl.BlockSpec((rb, DK), lambda h, n: (n, (O_GLA + W_GLA * h) // DK)),
                  pl.BlockSpec((rb, DK), lambda h, n: (n, (O_GLA + W_GLA * h) // DK + 1)),
                  pl.BlockSpec((rb, DV), lambda h, n: (n, (O_GLA + W_GLA * h) // DV + 1)),
                  pl.BlockSpec((rb, LRP), lambda h, n: (n, 0)),
                  pl.BlockSpec((LRP, DK), lambda h, n: (0, h)),
                  pl.BlockSpec((1, DK), lambda h, n: (0, h))],
        out_specs=[pl.BlockSpec((rb, DV), lambda h, n: (n, h)),
                   pl.BlockSpec((1, per, DV, DK), lambda h, n: (h, n, 0, 0))],
        out_shape=[jax.ShapeDtypeStruct((T, GLA_H * DV), F32),
                   jax.ShapeDtypeStruct((GLA_H, T // CHUNK, DV, DK), F32)],
        scratch_shapes=[pltpu.VMEM((DV, DK), F32)], compiler_params=_params(2),
    )(proj, proj, proj, proj_lr, w2p, gate_b)


def _gla_bwd(proj, proj_lr, w2p, gate_b, states, d_o, dproj):
    T = proj.shape[0]
    rb = _gla_rows(T)
    per = rb // CHUNK
    nblk = T // rb

    def body(q_ref, k_ref, v_ref, lr_ref, w2_ref, b_ref, st_ref, do_ref, _, dp_ref, dl_ref, ds_scr):
        @pl.when(pl.program_id(1) == 0)
        def _():
            ds_scr[...] = jnp.zeros_like(ds_scr)

        low = _tri(True)
        lower = low.astype(BF16)
        upper = _tri(False).astype(BF16)
        for i in reversed(range(per)):
            rows = slice(i * CHUNK, (i + 1) * CHUNK)
            logit, g, gl, eg, qd, ki, ke = _gla_chunk(q_ref[rows, :], k_ref[rows, :], lr_ref[rows, :], w2_ref[...],
                                                      b_ref[...], lower)
            v = v_ref[rows, :].astype(BF16)
            do = do_ref[rows, :].astype(BF16)
            qdb, kib, keb = qd.astype(BF16), ki.astype(BF16), ke.astype(BF16)
            att = jnp.where(low, _dot(qdb, kib, 1, 1), 0.0).astype(BF16)
            sp = st_ref[0, i]
            ds = ds_scr[...]
            dsb = ds.astype(BF16)
            datt = jnp.where(low, _dot(do, v, 1, 1), 0.0).astype(BF16)
            dp_ref[rows, 2 * DK:] = (_dot(att, do, 0, 0) + _dot(keb, dsb, 1, 1)).astype(BF16)
            dqd = _dot(datt, kib, 1, 0) + _dot(do, sp.astype(BF16), 1, 0)
            dki = _dot(datt, qdb, 0, 0)
            dke = _dot(v, dsb, 1, 0)
            decay = jnp.exp(gl)
            ds_scr[...] = ds * decay + _dot(do, qdb, 0, 0)
            ddec = jnp.sum(ds * sp, axis=0, keepdims=True)
            dp_ref[rows, 0:DK] = (dqd * (DK ** -0.5) * eg).astype(BF16)
            dp_ref[rows, DK:2 * DK] = (dki * jnp.exp(-g) + dke * jnp.exp(gl - g)).astype(BF16)
            dke_ke = dke * ke
            dg = dqd * qd - dki * ki - dke_ke
            dgl = jnp.sum(dke_ke, axis=0, keepdims=True) + ddec * decay
            dla = _tri_sum(upper, dg) + dgl
            dl_ref[rows, :] = dla * (1.0 / GATE_NORM) * (1.0 - _sigmoid(logit))

    def rev(n):
        return nblk - 1 - n

    return pl.pallas_call(
        body, name="gla_bwd", grid=(GLA_H, nblk),
        in_specs=[pl.BlockSpec((rb, DK), lambda h, n: (rev(n), (O_GLA + W_GLA * h) // DK)),
                  pl.BlockSpec((rb, DK), lambda h, n: (rev(n), (O_GLA + W_GLA * h) // DK + 1)),
                  pl.BlockSpec((rb, DV), lambda h, n: (rev(n), (O_GLA + W_GLA * h) // DV + 1)),
                  pl.BlockSpec((rb, LRP), lambda h, n: (rev(n), 0)),
                  pl.BlockSpec((LRP, DK), lambda h, n: (0, h)),
                  pl.BlockSpec((1, DK), lambda h, n: (0, h)),
                  pl.BlockSpec((1, per, DV, DK), lambda h, n: (h, rev(n), 0, 0)),
                  pl.BlockSpec((rb, DV), lambda h, n: (rev(n), h)),
                  ANY],
        out_specs=[pl.BlockSpec((rb, W_GLA), lambda h, n: (rev(n), O_GLA // W_GLA + h)),
                   pl.BlockSpec((rb, DK), lambda h, n: (rev(n), h))],
        out_shape=[jax.ShapeDtypeStruct((T, MAIN), BF16), jax.ShapeDtypeStruct((T, GLA_H * DK), F32)],
        scratch_shapes=[pltpu.VMEM((DV, DK), F32)], input_output_aliases={8: 0}, compiler_params=_params(2),
    )(proj, proj, proj, proj_lr, w2p, gate_b, states, d_o, dproj)


def _local_step(x, target, p, reduce_part):
    u, u_t = _rms_fwd(x, p["norm1_w"], name="rms1_fwd")
    proj = _matmul(u, p["w_in_main"], name="mm_proj")
    proj_lr = _matmul(u, p["w_in_lr"], name="mm_proj_lr")
    attn_o = _attn_fwd(proj, p["sinks"])
    gla_raw, states = _gla_fwd(proj, proj_lr, p["w2p"], p["gate_b"])
    merged, merged_t = _merge_fwd(attn_o, gla_raw, proj, p["gla_norm_w"])
    h1 = _matmul(merged, p["w_out"], add=x, name="mm_out")
    v2, v2_t = _rms_fwd(h1, p["norm2_w"], name="rms2_fwd")
    gate = _matmul(v2, p["w_gate"], b_slots=True, name="mm_gate")
    up = _matmul(v2, p["w_up"], b_slots=True, name="mm_up")
    ff, ff_t = _swiglu_fwd(gate, up)
    h2 = _matmul(ff, p["w_down"], add=h1, name="mm_down")
    loss, dh2, dh2_b, g_final = _final_loss(h2, p["final_norm_w"], target)

    dff = _matmul(dh2_b, p["w_down"], tb=True, name="mm_dff")
    reduce_part("w_down", _matmul(ff_t, dh2_b, out_dtype=BF16, name="mm_gdown").reshape(4, FFN // 4, D))
    dgate, dup = _swiglu_bwd(dff, gate, up)
    reduce_part("w_gate", _matmul(v2_t, dgate, out_dtype=BF16, bj=FFN // 4, out_slots=True, name="mm_ggate"))
    reduce_part("w_up", _matmul(v2_t, dup, out_dtype=BF16, bj=FFN // 4, out_slots=True, name="mm_gup"))
    dv2 = _matmul(dgate, p["w_gate"], tb=True, b_slots=True, name="mm_dv2a")
    dv2 = _matmul(dup, p["w_up"], tb=True, b_slots=True, add=dv2, name="mm_dv2b")
    dh1, dh1_b, g_norm2 = _rms_bwd(dv2, h1, p["norm2_w"], dh2, name="rms2_bwd")
    dmerged = _matmul(dh1_b, p["w_out"], tb=True, name="mm_dmerged")
    reduce_part("w_out", _matmul(merged_t, dh1_b, out_dtype=BF16, name="mm_gout").reshape(4, D // 4, D))
    d_attn, d_gla, dproj, g_gla_norm = _merge_bwd(dmerged, attn_o, gla_raw, proj, p["gla_norm_w"])
    dproj, dlogit = _gla_bwd(proj, proj_lr, p["w2p"], p["gate_b"], states, d_gla, dproj)
    dproj, g_sinks = _attn_bwd(proj, p["sinks"], d_attn, dproj)
    g_gate_b = _colsum(dlogit, name="colsum_gate_b")
    g_w2 = _matmul(proj_lr, dlogit, ta=True, name="mm_gw2")
    dproj_lr = _matmul(dlogit, p["w2p"], tb=True, out_dtype=BF16, name="mm_dlr")
    g_in_main = _matmul(u_t, dproj, out_dtype=BF16, name="mm_gin")
    g_in_lr = _matmul(u_t, dproj_lr, out_dtype=BF16, name="mm_gin_lr")
    du = _matmul(dproj, p["w_in_main"], tb=True, name="mm_du")
    du = _matmul(dproj_lr, p["w_in_lr"], tb=True, add=du, name="mm_du_lr")
    dx, _, g_norm1 = _rms_bwd(du, x, p["norm1_w"], dh1, name="rms1_bwd")
    grads = dict(norm1_w=g_norm1, w_in_main=g_in_main, w_in_lr=g_in_lr, w2=g_w2[:RANK], gate_b=g_gate_b,
                 sinks=g_sinks[:, 0].reshape(1, N_Q), gla_norm_w=g_gla_norm, norm2_w=g_norm2, final_norm_w=g_final)
    return loss, dx, grads


def _place():
    x, y, c = lax.axis_index("x"), lax.axis_index("y"), lax.axis_index("c")
    chips = [(1 - x, y), (x, 1 - y), (1 - x, 1 - y)]
    return x, y, c, chips


def _hbm_shape(s, dt):
    return jax.ShapeDtypeStruct(s, dt)


def _handshake(peers):
    barrier = pltpu.get_barrier_semaphore()
    for peer in peers:
        pl.semaphore_signal(barrier, inc=1, device_id=peer, device_id_type=MESH)
    pl.semaphore_wait(barrier, len(peers))


def _launch_copies(body, args, out_shape, sems, *, name, collective_id=None):
    if collective_id is None:
        return pl.pallas_call(
            body, name=name, in_specs=[ANY] * len(args), out_specs=[ANY] * len(out_shape), out_shape=out_shape,
            scratch_shapes=sems)(*args)
    return pl.kernel(
        body, name=name, out_type=out_shape, mesh=plsc.ScalarSubcoreMesh(axis_name="sequencer", num_cores=1),
        scratch_types=sems, compiler_params=pltpu.CompilerParams(collective_id=collective_id))(*args)


def _gather_shards(shards, *, name, collective_id=None):
    n = len(shards)

    def body(*refs):
        ins, outs = refs[:n], refs[n:2 * n]
        ici_send, ici_recv, d2d_send, d2d_recv, local_sem = refs[2 * n:]
        x, y, c, chips = _place()
        me = 2 * x + y
        sibling = (x, y, 1 - c)
        if collective_id is not None:
            _handshake([sibling] + [(*chip, c) for chip in chips])

        def half(w, slot, hc):
            r2 = shards[w].shape[0] // 2
            return outs[w].at[slot, pl.ds(hc * r2, r2), :]

        locals_ = [pltpu.make_async_copy(ins[w], outs[w].at[me], local_sem.at[w]) for w in range(n)]
        for cp in locals_:
            cp.start()
        sends = []
        for w in range(n):
            r2 = shards[w].shape[0] // 2
            for j, chip in enumerate(chips):
                cp = pltpu.make_async_remote_copy(
                    src_ref=ins[w].at[pl.ds(c * r2, r2), :], dst_ref=half(w, me, c),
                    send_sem=ici_send.at[w * 3 + j], recv_sem=ici_recv.at[w * 3 + j],
                    device_id=(*chip, c), device_id_type=MESH)
                cp.start()
                sends.append(cp)
        for w in range(n):
            for j, chip in enumerate(chips):
                slot = 2 * chip[0] + chip[1]
                got = half(w, slot, c)
                pltpu.make_async_remote_copy(
                    src_ref=got, dst_ref=got, send_sem=ici_send.at[w * 3 + j], recv_sem=ici_recv.at[w * 3 + j],
                    device_id=(*chip, c), device_id_type=MESH).wait_recv()
                cp = pltpu.make_async_remote_copy(
                    src_ref=got, dst_ref=got, send_sem=d2d_send.at[w * 3 + j], recv_sem=d2d_recv.at[w * 3 + j],
                    device_id=sibling, device_id_type=MESH)
                cp.start()
                sends.append(cp)
        for w in range(n):
            for j, chip in enumerate(chips):
                slot = 2 * chip[0] + chip[1]
                got = half(w, slot, 1 - c)
                pltpu.make_async_remote_copy(
                    src_ref=got, dst_ref=got, send_sem=d2d_send.at[w * 3 + j], recv_sem=d2d_recv.at[w * 3 + j],
                    device_id=sibling, device_id_type=MESH).wait_recv()
        for cp in sends:
            cp.wait_send()
        for cp in locals_:
            cp.wait()

    return _launch_copies(
        body, shards, [_hbm_shape((4,) + s.shape, s.dtype) for s in shards],
        [pltpu.SemaphoreType.DMA((3 * n,)), pltpu.SemaphoreType.DMA((3 * n,)), pltpu.SemaphoreType.DMA((3 * n,)),
         pltpu.SemaphoreType.DMA((3 * n,)), pltpu.SemaphoreType.DMA((n,))],
        name=name, collective_id=collective_id)


def _gather_staged(w, small):
    R, C = w.shape
    R2 = R // 2
    rb = 256
    per = R2 // rb
    r2 = small.shape[0] // 2

    def body(w_ref, s_ref, out_ref, outs_ref, stage, ici_send, ici_recv, fwd_send, fwd_recv, in_sem, put_sem,
             push_send, push_recv):
        x, y, c, chips = _place()
        me = 2 * x + y
        sibling = (x, y, 1 - c)
        slots = [2 * chip[0] + chip[1] for chip in chips]

        def ici(j, src, dst, k):
            return pltpu.make_async_remote_copy(
                src_ref=src, dst_ref=dst, send_sem=ici_send.at[k], recv_sem=ici_recv.at[k],
                device_id=(*chips[j], c), device_id_type=MESH)

        sends = []
        for j in range(3):
            sends.append(ici(j, w_ref.at[pl.ds(c * R2, R2), :], out_ref.at[me, pl.ds(c * R2, R2), :], j))
            sends.append(ici(j, s_ref.at[pl.ds(c * r2, r2), :], outs_ref.at[me, pl.ds(c * r2, r2), :], 3 + j))
        for cp in sends:
            cp.start()
        mine = pltpu.make_async_copy(s_ref, outs_ref.at[me], put_sem.at[2])
        mine.start()

        def own(k, to_vmem):
            rows = pl.ds(k * rb, rb)
            if to_vmem:
                return pltpu.make_async_copy(w_ref.at[rows, :], stage.at[k % 2], in_sem.at[k % 2])
            return pltpu.make_async_copy(stage.at[k % 2], out_ref.at[me, rows, :], put_sem.at[k % 2])

        for k in range(R // rb):
            if k >= 2:
                own(k - 2, False).wait()
            own(k, True).start()
            own(k, True).wait()
            own(k, False).start()
        for k in range(R // rb - 2, R // rb):
            own(k, False).wait()

        def block(j, k, hc):
            return out_ref.at[slots[j], pl.ds(hc * R2 + k * rb, rb), :]

        def push(t, hc):
            j, k = divmod(t, per)
            return pltpu.make_async_remote_copy(
                src_ref=stage.at[t % 2], dst_ref=block(j, k, hc), send_sem=push_send.at[t % 2],
                recv_sem=push_recv.at[t], device_id=sibling, device_id_type=MESH)

        passed = []
        for j in range(3):
            ici(j, w_ref.at[pl.ds(c * R2, R2), :], out_ref.at[slots[j], pl.ds(c * R2, R2), :], j).wait_recv()
            for k in range(per):
                t = j * per + k
                if t >= 2:
                    push(t - 2, c).wait_send()
                fetch = pltpu.make_async_copy(block(j, k, c), stage.at[t % 2], in_sem.at[t % 2])
                fetch.start()
                fetch.wait()
                push(t, c).start()
            got = outs_ref.at[slots[j], pl.ds(c * r2, r2), :]
            ici(j, got, got, 3 + j).wait_recv()
            fw = pltpu.make_async_remote_copy(
                src_ref=got, dst_ref=got, send_sem=fwd_send.at[j], recv_sem=fwd_recv.at[j],
                device_id=sibling, device_id_type=MESH)
            fw.start()
            passed.append(fw)
        for t in range(3 * per - 2, 3 * per):
            push(t, c).wait_send()
        for t in range(3 * per):
            push(t, 1 - c).wait_recv()
        for j in range(3):
            theirs = outs_ref.at[slots[j], pl.ds((1 - c) * r2, r2), :]
            pltpu.make_async_remote_copy(
                src_ref=theirs, dst_ref=theirs, send_sem=fwd_send.at[j], recv_sem=fwd_recv.at[j],
                device_id=sibling, device_id_type=MESH).wait_recv()
        for cp in sends + passed:
            cp.wait_send()
        mine.wait()

    dma = pltpu.SemaphoreType.DMA
    return pl.pallas_call(
        body, name="gather_in", in_specs=[ANY, ANY], out_specs=[ANY, ANY],
        out_shape=[_hbm_shape((4, R, C), w.dtype), _hbm_shape((4,) + small.shape, small.dtype)],
        scratch_shapes=[pltpu.VMEM((2, rb, C), w.dtype), dma((6,)), dma((6,)), dma((3,)), dma((3,)), dma((2,)),
                        dma((3,)), dma((2,)), dma((3 * per,))],
        compiler_params=pltpu.CompilerParams(vmem_limit_bytes=VMEM_LIMIT),
    )(w, small)


def _pair_blocks(R2):
    return _pick(R2, (256, 352, 128, 64, 32, 16))


def _pair_reduce(part, *, name):
    _, R, C = part.shape
    R2 = R // 2
    rb = _pair_blocks(R2)
    nblk = R2 // rb
    steps = [(s, i) for s in range(4) for i in range(nblk)]
    n = len(steps)

    def body(part_ref, sums_ref, own_buf, oth_buf, rcv_buf, out_buf, own_sem, oth_sem, out_sem, send_sem, recv_sem):
        x, y, c, _ = _place()
        sibling = (x, y, 1 - c)

        def fetch(t, half, buf, sem):
            s, i = steps[t]
            return pltpu.make_async_copy(part_ref.at[s, pl.ds(half * R2 + i * rb, rb), :], buf.at[t % 2], sem.at[t % 2])

        def push(t):
            return pltpu.make_async_remote_copy(
                src_ref=oth_buf.at[t % 2], dst_ref=rcv_buf.at[t % 2], send_sem=send_sem.at[t % 2],
                recv_sem=recv_sem.at[t % 2], device_id=sibling, device_id_type=MESH)

        def store(t):
            s, i = steps[t]
            return pltpu.make_async_copy(out_buf.at[t % 2], sums_ref.at[s, pl.ds(i * rb, rb), :], out_sem.at[t % 2])

        fetch(0, c, own_buf, own_sem).start()
        fetch(0, 1 - c, oth_buf, oth_sem).start()
        for t in range(n):
            if t + 1 < n:
                fetch(t + 1, c, own_buf, own_sem).start()
                fetch(t + 1, 1 - c, oth_buf, oth_sem).start()
            fetch(t, 1 - c, oth_buf, oth_sem).wait()
            p = push(t)
            p.start()
            fetch(t, c, own_buf, own_sem).wait()
            p.wait()
            if t >= 2:
                store(t - 2).wait()
            out_buf[t % 2] = (own_buf[t % 2].astype(F32) + rcv_buf[t % 2].astype(F32)).astype(BF16)
            store(t).start()
        for t in range(max(n - 2, 0), n):
            store(t).wait()

    buf = pltpu.VMEM((2, rb, C), BF16)
    sem2 = pltpu.SemaphoreType.DMA((2,))
    return pl.pallas_call(
        body, name=name, in_specs=[ANY], out_specs=ANY, out_shape=_hbm_shape((4, R2, C), BF16),
        scratch_shapes=[buf, buf, buf, buf, sem2, sem2, sem2, sem2, sem2],
        compiler_params=pltpu.CompilerParams(vmem_limit_bytes=VMEM_LIMIT),
    )(part)


def _chip_exchange(sums, *, name, collective_id=None):
    n = len(sums)

    def body(*refs):
        ins, outs = refs[:n], refs[n:2 * n]
        send_sem, recv_sem, local_sem = refs[2 * n:]
        x, y, c, chips = _place()
        me = 2 * x + y
        if collective_id is not None:
            _handshake([(*chip, c) for chip in chips])
        cps = []
        for w in range(n):
            lc = pltpu.make_async_copy(ins[w].at[me], outs[w].at[me], local_sem.at[w])
            lc.start()
            cps.append(lc)
            for j, chip in enumerate(chips):
                slot = 2 * chip[0] + chip[1]
                rc = pltpu.make_async_remote_copy(
                    src_ref=ins[w].at[slot], dst_ref=outs[w].at[me],
                    send_sem=send_sem.at[w * 3 + j], recv_sem=recv_sem.at[w * 3 + j],
                    device_id=(*chip, c), device_id_type=MESH)
                rc.start()
                cps.append(rc)
        for w in range(n):
            for j, chip in enumerate(chips):
                slot = 2 * chip[0] + chip[1]
                pltpu.make_async_remote_copy(
                    src_ref=ins[w].at[slot], dst_ref=outs[w].at[slot],
                    send_sem=send_sem.at[w * 3 + j], recv_sem=recv_sem.at[w * 3 + j],
                    device_id=(*chip, c), device_id_type=MESH).wait_recv()
        for w in range(n):
            cps[w * 4].wait()
            for j in range(3):
                cps[w * 4 + 1 + j].wait_send()

    return _launch_copies(
        body, sums, [_hbm_shape(s.shape, s.dtype) for s in sums],
        [pltpu.SemaphoreType.DMA((3 * n,)), pltpu.SemaphoreType.DMA((3 * n,)), pltpu.SemaphoreType.DMA((n,))],
        name=name, collective_id=collective_id)


def _sum_join(recv, *, name):
    _, R2, C = recv.shape
    rb = _pair_blocks(R2)
    nblk = R2 // rb

    def body(recv_ref, out_ref, in_buf, acc_buf, in_sem, loc_sem, send_sem, recv_sem):
        x, y, c, _ = _place()
        sibling = (x, y, 1 - c)

        def fetch(t):
            return pltpu.make_async_copy(recv_ref.at[:, pl.ds(t * rb, rb), :], in_buf.at[t % 2], in_sem.at[t % 2])

        def rows(t, half):
            return out_ref.at[pl.ds(half * R2 + t * rb, rb), :]

        def put_local(t):
            return pltpu.make_async_copy(acc_buf.at[t % 2], rows(t, c), loc_sem.at[t])

        def put_remote(t, half):
            return pltpu.make_async_remote_copy(
                src_ref=acc_buf.at[t % 2], dst_ref=rows(t, half), send_sem=send_sem.at[t], recv_sem=recv_sem.at[t],
                device_id=sibling, device_id_type=MESH)

        fetch(0).start()
        for t in range(nblk):
            if t + 1 < nblk:
                fetch(t + 1).start()
            fetch(t).wait()
            if t >= 2:
                put_local(t - 2).wait()
                put_remote(t - 2, c).wait_send()
            acc = in_buf[t % 2, 0].astype(F32)
            for s in range(1, 4):
                acc = acc + in_buf[t % 2, s].astype(F32)
            acc_buf[t % 2] = acc
            put_local(t).start()
            put_remote(t, c).start()
        for t in range(max(nblk - 2, 0), nblk):
            put_local(t).wait()
            put_remote(t, c).wait_send()
        for t in range(nblk):
            put_remote(t, 1 - c).wait_recv()

    semn = pltpu.SemaphoreType.DMA((nblk,))
    return pl.pallas_call(
        body, name=name, in_specs=[ANY], out_specs=ANY, out_shape=_hbm_shape((2 * R2, C), F32),
        scratch_shapes=[pltpu.VMEM((2, 4, rb, C), BF16), pltpu.VMEM((2, rb, C), F32),
                        pltpu.SemaphoreType.DMA((2,)), semn, semn, semn],
        compiler_params=pltpu.CompilerParams(vmem_limit_bytes=VMEM_LIMIT),
    )(recv)


def _sum_small(pack):
    R, C = pack.shape

    def body(in_ref, out_ref, all_ref, send_sem, recv_sem):
        x, y, c, _ = _place()
        me = 4 * x + 2 * y + c
        all_ref[me] = in_ref[...]
        cps = []
        for k in range(1, 8):
            peer = (x ^ (k >> 2), y ^ ((k >> 1) & 1), c ^ (k & 1))
            cp = pltpu.make_async_remote_copy(
                src_ref=in_ref, dst_ref=all_ref.at[me], send_sem=send_sem.at[k - 1], recv_sem=recv_sem.at[k - 1],
                device_id=peer, device_id_type=MESH)
            cp.start()
            cps.append(cp)
        for k in range(1, 8):
            peer = (x ^ (k >> 2), y ^ ((k >> 1) & 1), c ^ (k & 1))
            slot = 4 * peer[0] + 2 * peer[1] + peer[2]
            pltpu.make_async_remote_copy(
                src_ref=in_ref, dst_ref=all_ref.at[slot], send_sem=send_sem.at[k - 1], recv_sem=recv_sem.at[k - 1],
                device_id=peer, device_id_type=MESH).wait_recv()
        for cp in cps:
            cp.wait_send()
        acc = all_ref[0]
        for d in range(1, 8):
            acc = acc + all_ref[d]
        out_ref[...] = acc

    return pl.pallas_call(
        body, name="sum_small", in_specs=[pl.BlockSpec(memory_space=pltpu.VMEM)],
        out_specs=pl.BlockSpec(memory_space=pltpu.VMEM), out_shape=jax.ShapeDtypeStruct((R, C), F32),
        scratch_shapes=[pltpu.VMEM((8, R, C), F32), pltpu.SemaphoreType.DMA((7,)), pltpu.SemaphoreType.DMA((7,))],
    )(pack)


ADAMW_BLOCK_BYTES = 2 * 1024 * 1024


def _adamw_block(R, C):
    padded = -(-C // 128) * 128
    rows = [rb for rb in range(8, R + 1, 8) if R % rb == 0 and rb * padded * 4 <= ADAMW_BLOCK_BYTES]
    if rows or R * padded * 4 <= ADAMW_BLOCK_BYTES:
        return (max(rows) if rows else R), C
    cols = [cb for cb in range(128, C + 1, 128) if C % cb == 0 and R * cb * 4 <= ADAMW_BLOCK_BYTES]
    return R, max(cols)


def _adamw(w, g, m, v, *, name):
    R, C = w.shape
    rb, cb = _adamw_block(R, C)
    c1 = 1.0 / (1.0 - B1 ** STEP)
    c2 = 1.0 / (1.0 - B2 ** STEP)

    def body(w_ref, g_ref, m_ref, v_ref, d_ref, nm_ref, nv_ref):
        gv = g_ref[...]
        m2 = B1 * m_ref[...] + (1.0 - B1) * gv
        v2 = B2 * v_ref[...] + (1.0 - B2) * gv * gv
        nm_ref[...] = m2
        nv_ref[...] = v2
        d_ref[...] = -LR * ((m2 * c1) / (jnp.sqrt(v2 * c2) + ADAM_EPS) + WD * w_ref[...])

    blk = pl.BlockSpec((rb, cb), lambda i, j: (i, j))
    return pl.pallas_call(
        body, name=name, grid=(R // rb, C // cb), in_specs=[blk] * 4, out_specs=[blk] * 3,
        out_shape=[jax.ShapeDtypeStruct((R, C), F32)] * 3, compiler_params=_params(2),
    )(w, g, m, v)


SMALL = (("norm1_w", D), ("norm2_w", D), ("final_norm_w", D), ("gate_b", GLA_H * DK), ("gla_norm_w", DV), ("sinks", N_Q))
PACK_W = 1024


def _pack_small(vals, w2, loss):
    rows = []
    for name, width in SMALL:
        v = vals[name].reshape(-1)
        rows.append(jnp.pad(v, (0, (-width) % PACK_W)).reshape(-1, PACK_W))
    rows.append(w2)
    rows.append(jnp.broadcast_to(loss.reshape(1, 1), (1, PACK_W)))
    pack = jnp.concatenate(rows, axis=0)
    return jnp.pad(pack, ((0, 32 - pack.shape[0]), (0, 0)))


def _unpack_small(pack):
    out, r = {}, 0
    for name, width in SMALL:
        nr = -(-width // PACK_W)
        out[name] = pack[r:r + nr].reshape(-1)[:width]
        r += nr
    out["w2"] = pack[r:r + RANK]
    out["loss"] = pack[r + RANK, 0]
    return out


def kernel(x, norm1_w, w_in, gla_gate_w2, gla_gate_b, attn_sinks, gla_norm_w, w_out, norm2_w, w_ffn_gate, w_ffn_up, w_ffn_down, final_norm_w, loss_target, m_norm1_w, m_w_in, m_gla_gate_w2, m_gla_gate_b, m_attn_sinks, m_gla_norm_w, m_w_out, m_norm2_w, m_w_ffn_gate, m_w_ffn_up, m_w_ffn_down, m_final_norm_w, v_norm1_w, v_w_in, v_gla_gate_w2, v_gla_gate_b, v_attn_sinks, v_gla_norm_w, v_w_out, v_norm2_w, v_w_ffn_gate, v_w_ffn_up, v_w_ffn_down, v_final_norm_w):
    chip = 2 * lax.axis_index("x") + lax.axis_index("y")
    w_in_s, w_out_s, w_gate_s, w_up_s, w_down_s, w2_s = (
        w_in[0], w_out[0], w_ffn_gate[0], w_ffn_up[0], w_ffn_down[0], gla_gate_w2[0])
    CS = SHARD_COLS

    g_in, g_w2 = _gather_staged(w_in_s.astype(BF16), w2_s)
    g_out, g_gate, g_up, g_down = _gather_shards(
        [w_out_s.astype(BF16), w_gate_s.astype(BF16), w_up_s.astype(BF16), w_down_s.astype(BF16)],
        name="gather_rest", collective_id=1)
    w_main, w_lr = _w_in_to_main(g_in)
    w2_full = jnp.transpose(g_w2, (1, 0, 2)).reshape(RANK, GLA_H * DK)
    p = dict(
        norm1_w=norm1_w, norm2_w=norm2_w, final_norm_w=final_norm_w.reshape(1, D), gate_b=gla_gate_b,
        gla_norm_w=gla_norm_w, sinks=jnp.pad(attn_sinks, ((0, 0), (0, 128 - N_Q))),
        w_in_main=w_main, w_in_lr=w_lr,
        w2p=jnp.pad(w2_full, ((0, LRP - RANK), (0, 0))).astype(BF16),
        w_out=g_out.reshape(D, D),
        w_gate=g_gate, w_up=g_up, w_down=g_down.reshape(FFN, D),
    )

    tags = ["w_in", "w_out", "w_gate", "w_up", "w_down"]
    recv = {}

    def reduce_part(tag, part):
        sums = _pair_reduce(part, name="pair_reduce_" + tag)
        recv[tag], = _chip_exchange([sums], name="chip_exchange_" + tag, collective_id=2 + tags.index(tag))

    loss_blk, dx, g = _local_step(x[0], loss_target[0], p, reduce_part)
    reduce_part("w_in", _main_to_shards(g["w_in_main"], g["w_in_lr"]))
    big = [_sum_join(recv[tg], name="sum_join_" + tg) for tg in tags]

    small = _unpack_small(_sum_small(_pack_small(g, g["w2"], loss_blk[0, 0])))
    loss = small["loss"]
    g_w2_mine = lax.dynamic_slice_in_dim(small["w2"], chip * (GLA_H * DK // 4), GLA_H * DK // 4, axis=1)

    grads = dict(
        norm1_w=small["norm1_w"].reshape(1, D), w_in=big[0], gla_gate_w2=g_w2_mine,
        gla_gate_b=small["gate_b"].reshape(1, -1), attn_sinks=small["sinks"].reshape(1, N_Q),
        gla_norm_w=small["gla_norm_w"].reshape(1, DV), w_out=big[1], norm2_w=small["norm2_w"].reshape(1, D),
        w_ffn_gate=big[2], w_ffn_up=big[3], w_ffn_down=big[4], final_norm_w=small["final_norm_w"].reshape(1, D))
    def lin(a):
        return jnp.transpose(a.reshape(D, CS))

    def unlin(a):
        return jnp.transpose(a)

    grads["w_in"] = lin(big[0])
    weights = dict(
        norm1_w=(norm1_w, m_norm1_w, v_norm1_w), w_in=(lin(w_in), lin(m_w_in), lin(v_w_in)),
        gla_gate_w2=(w2_s, m_gla_gate_w2[0], v_gla_gate_w2[0]), gla_gate_b=(gla_gate_b, m_gla_gate_b, v_gla_gate_b),
        attn_sinks=(attn_sinks, m_attn_sinks, v_attn_sinks), gla_norm_w=(gla_norm_w, m_gla_norm_w, v_gla_norm_w),
        w_out=(w_out_s, m_w_out[0], v_w_out[0]), norm2_w=(norm2_w, m_norm2_w, v_norm2_w),
        w_ffn_gate=(w_gate_s, m_w_ffn_gate[0], v_w_ffn_gate[0]), w_ffn_up=(w_up_s, m_w_ffn_up[0], v_w_ffn_up[0]),
        w_ffn_down=(w_down_s, m_w_ffn_down[0], v_w_ffn_down[0]),
        final_norm_w=(final_norm_w.reshape(1, D), m_final_norm_w.reshape(1, D), v_final_norm_w.reshape(1, D)))
    names = ["norm1_w", "w_in", "gla_gate_w2", "gla_gate_b", "attn_sinks", "gla_norm_w", "w_out", "norm2_w",
             "w_ffn_gate", "w_ffn_up", "w_ffn_down", "final_norm_w"]
    lead = {"norm1_w": False, "gla_gate_b": False, "attn_sinks": False, "gla_norm_w": False, "norm2_w": False}
    g_out_l, d_out, m_out, v_out = [], [], [], []
    for nm in names:
        w, m, v = weights[nm]
        gr = grads[nm]
        dl, nmn, nvn = _adamw(w, gr, m, v, name="adamw_" + nm)
        if nm == "w_in":
            gr, dl, nmn, nvn, w = big[0], unlin(dl), unlin(nmn), unlin(nvn), w_in_s
        if nm == "final_norm_w":
            shape = (D,)
        elif nm in lead:
            shape = w.shape
        else:
            shape = (1,) + w.shape
        g_out_l.append(gr.reshape(shape))
        d_out.append(dl.reshape(shape))
        m_out.append(nmn.reshape(shape))
        v_out.append(nvn.reshape(shape))
    return (loss, dx[None], *g_out_l, *d_out, *m_out, *v_out)
```

```python
import functools

import jax
import jax.numpy as jnp
from jax import lax
from jax.experimental import pallas as pl
from jax.experimental.pallas import tpu as pltpu
from jax.experimental.pallas import tpu_sc as plsc

F32 = jnp.float32
BF16 = jnp.bfloat16

D = 2048
HEAD_DIM = 64
N_Q = 32
N_KV = 4
GROUP = 8
WINDOW = 128
GLA_H = 4
DK = 256
DV = 512
RANK = 16
CHUNK = 64
FFN = 5632
EPS = 1e-6
MASK_VALUE = -1e30
GATE_NORM = 16.0
LR, B1, B2, ADAM_EPS, WD, STEP = 0.001, 0.9, 0.999, 1e-08, 0.01, 10

MAIN = 12800
O_MERGE, O_GLA, O_ATTN = 0, 6144, 10240
W_MERGE, W_GLA, W_ATTN = 3 * DV, 2 * DK + DV, D + 2 * N_KV * HEAD_DIM
LR_AT = 6656
LRP = 128


def _main_pieces():
    o_gq, o_gk, o_gv, o_gr, o_ga, o_gb = 2560, 3584, 4608, LR_AT + RANK, LR_AT + RANK + D, LR_AT + RANK + 2 * D
    pieces = []
    for h in range(GLA_H):
        pieces += [(o_gr + DV * h, DV), (o_ga + DV * h, DV), (o_gb + DV * h, DV)]
    for h in range(GLA_H):
        pieces += [(o_gq + DK * h, DK), (o_gk + DK * h, DK), (o_gv + DV * h, DV)]
    pieces.append((0, W_ATTN))
    return pieces


SHARD_COLS = 3204

VMEM_LIMIT = 56 * 1024 * 1024
MESH = pl.DeviceIdType.MESH
ANY = pl.BlockSpec(memory_space=pl.ANY)


def _params(ngrid):
    return pltpu.CompilerParams(dimension_semantics=("arbitrary",) * ngrid, vmem_limit_bytes=VMEM_LIMIT)


def _shard_segments(lo, hi):
    segs = []
    while lo < hi:
        s = lo // SHARD_COLS
        e = min(hi, (s + 1) * SHARD_COLS)
        segs.append((s, lo - s * SHARD_COLS, e - s * SHARD_COLS))
        lo = e
    return segs


def _lanes(pieces):
    return pieces[0] if len(pieces) == 1 else jnp.concatenate(pieces, axis=1)


def _w_in_to_main(shards):
    rb = 256

    def body(g_ref, main_ref, lr_ref):
        at = 0
        for a, w in _main_pieces():
            main_ref[:, at:at + w] = _lanes([g_ref[s, :, lo:hi] for s, lo, hi in _shard_segments(a, a + w)])
            at += w
        lr = [g_ref[s, :, lo:hi] for s, lo, hi in _shard_segments(LR_AT, LR_AT + RANK)]
        lr_ref[...] = _lanes(lr + [jnp.zeros((rb, LRP - RANK), lr_ref.dtype)])

    return pl.pallas_call(
        body, name="w_in_to_main", grid=(D // rb,),
        in_specs=[pl.BlockSpec((4, rb, SHARD_COLS), lambda i: (0, i, 0))],
        out_specs=[pl.BlockSpec((rb, MAIN), lambda i: (i, 0)), pl.BlockSpec((rb, LRP), lambda i: (i, 0))],
        out_shape=[jax.ShapeDtypeStruct((D, MAIN), shards.dtype), jax.ShapeDtypeStruct((D, LRP), shards.dtype)],
        compiler_params=_params(1),
    )(shards)


def _main_to_shards(g_main, g_lr):
    rb = 256
    at, sources = 0, [(LR_AT, RANK, 1, 0)]
    for a, w in _main_pieces():
        sources.append((a, w, 0, at))
        at += w
    sources.sort()

    def body(main_ref, lr_ref, out_ref):
        refs = (main_ref, lr_ref)
        for s in range(4):
            lo, hi = s * SHARD_COLS, (s + 1) * SHARD_COLS
            pieces = []
            for a, w, which, src in sources:
                b, e = max(a, lo), min(a + w, hi)
                if b < e:
                    pieces.append(refs[which][:, src + b - a:src + e - a])
            out_ref[s] = _lanes(pieces)

    return pl.pallas_call(
        body, name="main_to_shards", grid=(D // rb,),
        in_specs=[pl.BlockSpec((rb, MAIN), lambda i: (i, 0)), pl.BlockSpec((rb, LRP), lambda i: (i, 0))],
        out_specs=pl.BlockSpec((4, rb, SHARD_COLS), lambda i: (0, i, 0)),
        out_shape=jax.ShapeDtypeStruct((4, D, SHARD_COLS), g_main.dtype), compiler_params=_params(1),
    )(g_main, g_lr)


def _pick(n, cands):
    for c in cands:
        if n % c == 0:
            return c
    return n


def _sigmoid(x):
    return 1.0 / (1.0 + jnp.exp(-x))


def _dot(a, b, ca, cb):
    return lax.dot_general(a, b, (((ca,), (cb,)), ((), ())), preferred_element_type=F32)


def _matmul(a, b, *, name, ta=False, tb=False, add=None, out_dtype=F32, bm=None, bj=None, bc=None,
            b_slots=False, out_slots=False):
    C, M = a.shape if ta else a.shape[::-1]
    if b_slots:
        if tb:
            J, bc = b.shape[1], b.shape[2]
            assert b.shape[0] * bc == C
        else:
            bj = b.shape[2]
            J = b.shape[0] * bj
            assert b.shape[1] == C
    else:
        J = b.shape[0] if tb else b.shape[1]
        assert (b.shape[1] if tb else b.shape[0]) == C
    bm = bm or _pick(M, (1024, 512, 256, 128) if C <= FFN else (512, 256, 128))
    bj = bj or _pick(J, (1280, 1024, 512, 256, 128) if C <= D else (512, 256, 128))
    bc = bc or (C if C <= FFN else _pick(C, (6400,)))
    nm, nj, nc = M // bm, J // bj, C // bc
    has_add = add is not None

    def body(*refs):
        a_ref, b_ref = refs[0], refs[1]
        add_ref = refs[2] if has_add else None
        o_ref = refs[3] if has_add else refs[2]
        p = _dot(a_ref[...].astype(BF16), b_ref[...].astype(BF16), 0 if ta else 1, 1 if tb else 0)

        def finish(acc):
            if has_add:
                acc = acc + add_ref[...]
            o_ref[...] = acc.astype(o_ref.dtype)

        if nc == 1:
            finish(p)
        else:
            acc_ref = refs[-1]
            c = pl.program_id(2)

            @pl.when(c == 0)
            def _():
                acc_ref[...] = p

            @pl.when(c > 0)
            def _():
                acc_ref[...] += p

            @pl.when(c == nc - 1)
            def _():
                finish(acc_ref[...])

    a_spec = pl.BlockSpec((bc, bm), lambda m, j, c: (c, m)) if ta else pl.BlockSpec((bm, bc), lambda m, j, c: (m, c))
    if b_slots:
        b_spec = (pl.BlockSpec((None, bj, bc), lambda m, j, c: (c, j, 0)) if tb
                  else pl.BlockSpec((None, bc, bj), lambda m, j, c: (j, c, 0)))
    else:
        b_spec = (pl.BlockSpec((bj, bc), lambda m, j, c: (j, c)) if tb
                  else pl.BlockSpec((bc, bj), lambda m, j, c: (c, j)))
    if out_slots:
        assert not has_add
        o_spec = pl.BlockSpec((None, bm, bj), lambda m, j, c: (j, m, 0))
        out_shape = jax.ShapeDtypeStruct((nj, M, bj), out_dtype)
    else:
        o_spec = pl.BlockSpec((bm, bj), lambda m, j, c: (m, j))
        out_shape = jax.ShapeDtypeStruct((M, J), out_dtype)
    in_specs = [a_spec, b_spec] + ([o_spec] if has_add else [])
    args = (a, b) + ((add,) if has_add else ())
    return pl.pallas_call(
        body, name=name, grid=(nm, nj, nc), in_specs=in_specs, out_specs=o_spec,
        out_shape=out_shape,
        scratch_shapes=[pltpu.VMEM((bm, bj), F32)] if nc > 1 else [],
        compiler_params=_params(3),
    )(*args)


def _rows(T):
    return _pick(T, (256, 128))


def _rms_fwd(x, w, *, name):
    T = x.shape[0]
    rb = _rows(T)

    def body(x_ref, w_ref, u_ref, ut_ref):
        xv = x_ref[...]
        r = lax.rsqrt(jnp.mean(xv * xv, axis=-1, keepdims=True) + EPS)
        u = xv * r * w_ref[...]
        u_ref[...] = u.astype(BF16)
        ut_ref[...] = u.T.astype(BF16)

    return pl.pallas_call(
        body, name=name, grid=(T // rb,),
        in_specs=[pl.BlockSpec((rb, D), lambda i: (i, 0)), pl.BlockSpec((1, D), lambda i: (0, 0))],
        out_specs=[pl.BlockSpec((rb, D), lambda i: (i, 0)), pl.BlockSpec((D, rb), lambda i: (0, i))],
        out_shape=[jax.ShapeDtypeStruct((T, D), BF16), jax.ShapeDtypeStruct((D, T), BF16)], compiler_params=_params(1),
    )(x, w)


def _rms_bwd(dy, h, w, resid, *, name):
    T = h.shape[0]
    rb = _rows(T)

    def body(dy_ref, h_ref, w_ref, res_ref, dh_ref, dhb_ref, dw_ref):
        @pl.when(pl.program_id(0) == 0)
        def _():
            dw_ref[...] = jnp.zeros_like(dw_ref)

        hv = h_ref[...]
        r = lax.rsqrt(jnp.mean(hv * hv, axis=-1, keepdims=True) + EPS)
        hn = hv * r
        dyv = dy_ref[...]
        dw_ref[...] += jnp.sum(dyv * hn, axis=0, keepdims=True)
        t = dyv * w_ref[...]
        dh = res_ref[...] + r * (t - hn * jnp.mean(t * hn, axis=-1, keepdims=True))
        dh_ref[...] = dh
        dhb_ref[...] = dh.astype(BF16)

    blk = pl.BlockSpec((rb, D), lambda i: (i, 0))
    vec = pl.BlockSpec((1, D), lambda i: (0, 0))
    return pl.pallas_call(
        body, name=name, grid=(T // rb,), in_specs=[blk, blk, vec, blk], out_specs=[blk, blk, vec],
        out_shape=[jax.ShapeDtypeStruct((T, D), F32), jax.ShapeDtypeStruct((T, D), BF16),
                   jax.ShapeDtypeStruct((1, D), F32)],
        compiler_params=_params(1),
    )(dy, h, w, resid)


def _final_loss(h, w, target):
    T = h.shape[0]
    rb = _rows(T)

    def body(h_ref, w_ref, t_ref, loss_ref, dh_ref, dhb_ref, dw_ref):
        @pl.when(pl.program_id(0) == 0)
        def _():
            dw_ref[...] = jnp.zeros_like(dw_ref)
            loss_ref[...] = jnp.zeros_like(loss_ref)

        hv = h_ref[...]
        wv = w_ref[...]
        r = lax.rsqrt(jnp.mean(hv * hv, axis=-1, keepdims=True) + EPS)
        hn = hv * r
        e = hn * wv - t_ref[...]
        row = jnp.sum(e * e, axis=-1, keepdims=True) * (0.5 / D)
        loss_ref[...] += jnp.broadcast_to(jnp.sum(row, axis=0, keepdims=True), loss_ref.shape)
        dy = e * (1.0 / D)
        dw_ref[...] += jnp.sum(dy * hn, axis=0, keepdims=True)
        t = dy * wv
        dh = r * (t - hn * jnp.mean(t * hn, axis=-1, keepdims=True))
        dh_ref[...] = dh
        dhb_ref[...] = dh.astype(BF16)

    blk = pl.BlockSpec((rb, D), lambda i: (i, 0))
    vec = pl.BlockSpec((1, D), lambda i: (0, 0))
    return pl.pallas_call(
        body, name="final_loss", grid=(T // rb,), in_specs=[blk, vec, blk],
        out_specs=[pl.BlockSpec((8, 128), lambda i: (0, 0)), blk, blk, vec],
        out_shape=[jax.ShapeDtypeStruct((8, 128), F32), jax.ShapeDtypeStruct((T, D), F32),
                   jax.ShapeDtypeStruct((T, D), BF16), jax.ShapeDtypeStruct((1, D), F32)],
        compiler_params=_params(1),
    )(h, w, target)


def _swiglu_fwd(gate, up):
    T = gate.shape[0]
    rb = _rows(T)

    def body(g_ref, u_ref, o_ref, ot_ref):
        g = g_ref[...]
        ff = g * _sigmoid(g) * u_ref[...]
        o_ref[...] = ff.astype(BF16)
        ot_ref[...] = ff.T.astype(BF16)

    blk = pl.BlockSpec((rb, 512), lambda i, j: (i, j))
    return pl.pallas_call(
        body, name="swiglu_fwd", grid=(T // rb, FFN // 512), in_specs=[blk, blk],
        out_specs=[blk, pl.BlockSpec((512, rb), lambda i, j: (j, i))],
        out_shape=[jax.ShapeDtypeStruct((T, FFN), BF16), jax.ShapeDtypeStruct((FFN, T), BF16)],
        compiler_params=_params(2),
    )(gate, up)


def _swiglu_bwd(dff, gate, up):
    T = gate.shape[0]
    rb = _rows(T)

    def body(d_ref, g_ref, u_ref, dg_ref, du_ref):
        g = g_ref[...]
        d = d_ref[...]
        sg = _sigmoid(g)
        du_ref[...] = (d * g * sg).astype(BF16)
        dg_ref[...] = (d * u_ref[...] * sg * (1.0 + g * (1.0 - sg))).astype(BF16)

    blk = pl.BlockSpec((rb, 512), lambda i, j: (i, j))
    return pl.pallas_call(
        body, name="swiglu_bwd", grid=(T // rb, FFN // 512), in_specs=[blk, blk, blk], out_specs=[blk, blk],
        out_shape=[jax.ShapeDtypeStruct((T, FFN), BF16)] * 2, compiler_params=_params(2),
    )(dff, gate, up)


def _colsum(x, *, name):
    T, W = x.shape
    rb = _rows(T)

    def body(x_ref, o_ref):
        @pl.when(pl.program_id(0) == 0)
        def _():
            o_ref[...] = jnp.zeros_like(o_ref)

        o_ref[...] += jnp.sum(x_ref[...], axis=0, keepdims=True)

    return pl.pallas_call(
        body, name=name, grid=(T // rb,), in_specs=[pl.BlockSpec((rb, W), lambda i: (i, 0))],
        out_specs=pl.BlockSpec((1, W), lambda i: (0, 0)), out_shape=jax.ShapeDtypeStruct((1, W), F32),
        compiler_params=_params(1),
    )(x)


def _merge_fwd(attn_o, gla_raw, proj, gla_norm_w):
    T = attn_o.shape[0]
    rb = _rows(T)

    def body(a_ref, g_ref, gr_ref, ga_ref, gb_ref, w_ref, o_ref, ot_ref):
        graw = g_ref[...]
        r = lax.rsqrt(jnp.mean(graw * graw, axis=-1, keepdims=True) + EPS)
        gr = gr_ref[...]
        go = graw * r * w_ref[...] * (gr * _sigmoid(gr))
        merged = _sigmoid(ga_ref[...]) * a_ref[...] + _sigmoid(gb_ref[...]) * go
        o_ref[...] = merged.astype(BF16)
        ot_ref[...] = merged.T.astype(BF16)

    def sec(k):
        return pl.BlockSpec((rb, DV), lambda i, h: (i, O_MERGE // DV + 3 * h + k))

    blk = pl.BlockSpec((rb, DV), lambda i, h: (i, h))
    return pl.pallas_call(
        body, name="merge_fwd", grid=(T // rb, GLA_H),
        in_specs=[blk, blk, sec(0), sec(1), sec(2), pl.BlockSpec((1, DV), lambda i, h: (0, 0))],
        out_specs=[blk, pl.BlockSpec((DV, rb), lambda i, h: (h, i))],
        out_shape=[jax.ShapeDtypeStruct((T, D), BF16), jax.ShapeDtypeStruct((D, T), BF16)], compiler_params=_params(2),
    )(attn_o, gla_raw, proj, proj, proj, gla_norm_w)


def _merge_bwd(dm, attn_o, gla_raw, proj, gla_norm_w):
    T = attn_o.shape[0]
    rb = _rows(T)

    def body(dm_ref, a_ref, g_ref, gr_ref, ga_ref, gb_ref, w_ref, da_ref, dg_ref, dp_ref, dw_ref):
        @pl.when((pl.program_id(0) == 0) & (pl.program_id(1) == 0))
        def _():
            dw_ref[...] = jnp.zeros_like(dw_ref)

        dmv = dm_ref[...]
        av = a_ref[...]
        graw = g_ref[...]
        gr = gr_ref[...]
        wv = w_ref[...]
        sa = _sigmoid(ga_ref[...])
        sb = _sigmoid(gb_ref[...])
        r = lax.rsqrt(jnp.mean(graw * graw, axis=-1, keepdims=True) + EPS)
        gnh = graw * r
        gn = gnh * wv
        sr = _sigmoid(gr)
        sl = gr * sr
        go = gn * sl
        da_ref[...] = dmv * sa
        dgo = dmv * sb
        dp_ref[:, 0:DV] = (dgo * gn * sr * (1.0 + gr * (1.0 - sr))).astype(BF16)
        dp_ref[:, DV:2 * DV] = (dmv * av * sa * (1.0 - sa)).astype(BF16)
        dp_ref[:, 2 * DV:3 * DV] = (dmv * go * sb * (1.0 - sb)).astype(BF16)
        dgn = dgo * sl
        dw_ref[...] += jnp.sum(dgn * gnh, axis=0, keepdims=True)
        t = dgn * wv
        dg_ref[...] = r * (t - gnh * jnp.mean(t * gnh, axis=-1, keepdims=True))

    def sec(k):
        return pl.BlockSpec((rb, DV), lambda i, h: (i, O_MERGE // DV + 3 * h + k))

    blk = pl.BlockSpec((rb, DV), lambda i, h: (i, h))
    vec = pl.BlockSpec((1, DV), lambda i, h: (0, 0))
    return pl.pallas_call(
        body, name="merge_bwd", grid=(T // rb, GLA_H),
        in_specs=[blk, blk, blk, sec(0), sec(1), sec(2), vec],
        out_specs=[blk, blk, pl.BlockSpec((rb, W_MERGE), lambda i, h: (i, O_MERGE // W_MERGE + h)), vec],
        out_shape=[jax.ShapeDtypeStruct((T, D), F32), jax.ShapeDtypeStruct((T, D), F32),
                   jax.ShapeDtypeStruct((T, MAIN), BF16), jax.ShapeDtypeStruct((1, DV), F32)],
        compiler_params=_params(2),
    )(dm, attn_o, gla_raw, proj, proj, proj, gla_norm_w)


def _attn_mask(n):
    qi = lax.broadcasted_iota(jnp.int32, (GROUP * WINDOW, 2 * WINDOW), 0) & (WINDOW - 1)
    ki = lax.broadcasted_iota(jnp.int32, (GROUP * WINDOW, 2 * WINDOW), 1)
    rel = qi + WINDOW - ki
    return (rel >= 0) & (rel < WINDOW) & ((n > 0) | (ki >= WINDOW))


def _stack_heads(ref, h):
    return jnp.concatenate(
        [ref[:, (h * GROUP + g) * HEAD_DIM:(h * GROUP + g + 1) * HEAD_DIM] for g in range(GROUP)], axis=0).astype(BF16)


def _stack_sinks(s_ref, h):
    return jnp.concatenate(
        [jnp.broadcast_to(s_ref[:, h * GROUP + g:h * GROUP + g + 1], (WINDOW, 1)) for g in range(GROUP)], axis=0)


def _attn_probs(q, kk, mask, sink):
    s = _dot(q, kk, 1, 1) * (HEAD_DIM ** -0.5)
    s = jnp.where(mask, s, MASK_VALUE)
    m = jnp.maximum(jnp.max(s, axis=-1, keepdims=True), sink)
    e = jnp.exp(s - m)
    es = jnp.exp(sink - m)
    inv = 1.0 / (jnp.sum(e, axis=-1, keepdims=True) + es)
    return e * inv, es * inv


def _attn_specs(nb, rev):
    def at(n):
        return (nb - 1 - n) if rev else n

    kcol, vcol = (O_ATTN + D) // 256, (O_ATTN + D) // 256 + 1
    q = pl.BlockSpec((WINDOW, D), lambda n: (at(n), O_ATTN // D))
    kc = pl.BlockSpec((WINDOW, 256), lambda n: (at(n), kcol))
    kp = pl.BlockSpec((WINDOW, 256), lambda n: (jnp.maximum(at(n) - 1, 0), kcol))
    vc = pl.BlockSpec((WINDOW, 256), lambda n: (at(n), vcol))
    vp = pl.BlockSpec((WINDOW, 256), lambda n: (jnp.maximum(at(n) - 1, 0), vcol))
    sk = pl.BlockSpec((1, 128), lambda n: (0, 0))
    o = pl.BlockSpec((WINDOW, D), lambda n: (at(n), 0))
    return q, kc, kp, vc, vp, sk, o


def _attn_fwd(proj, sinks):
    T = proj.shape[0]
    nb = T // WINDOW

    def body(q_ref, kc_ref, kp_ref, vc_ref, vp_ref, s_ref, o_ref):
        mask = _attn_mask(pl.program_id(0))
        for h in range(N_KV):
            hs = slice(h * HEAD_DIM, (h + 1) * HEAD_DIM)
            kk = jnp.concatenate([kp_ref[:, hs], kc_ref[:, hs]], axis=0).astype(BF16)
            vv = jnp.concatenate([vp_ref[:, hs], vc_ref[:, hs]], axis=0).astype(BF16)
            p, _ = _attn_probs(_stack_heads(q_ref, h), kk, mask, _stack_sinks(s_ref, h))
            o = _dot(p.astype(BF16), vv, 1, 0)
            for g in range(GROUP):
                hg = h * GROUP + g
                o_ref[:, hg * HEAD_DIM:(hg + 1) * HEAD_DIM] = o[g * WINDOW:(g + 1) * WINDOW]

    q, kc, kp, vc, vp, sk, o = _attn_specs(nb, False)
    return pl.pallas_call(
        body, name="attn_fwd", grid=(nb,), in_specs=[q, kc, kp, vc, vp, sk], out_specs=o,
        out_shape=jax.ShapeDtypeStruct((T, D), F32), compiler_params=_params(1),
    )(proj, proj, proj, proj, proj, sinks)


def _attn_bwd(proj, sinks, d_o, dproj):
    T = proj.shape[0]
    nb = T // WINDOW
    kat, vat = D, D + N_KV * HEAD_DIM

    def body(q_ref, kc_ref, kp_ref, vc_ref, vp_ref, s_ref, do_ref, _, dp_ref, ds_ref, ck_ref, cv_ref):
        i = pl.program_id(0)
        n = nb - 1 - i

        @pl.when(i == 0)
        def _():
            ck_ref[...] = jnp.zeros_like(ck_ref)
            cv_ref[...] = jnp.zeros_like(cv_ref)
            ds_ref[...] = jnp.zeros_like(ds_ref)

        mask = _attn_mask(n)
        for h in range(N_KV):
            hs = slice(h * HEAD_DIM, (h + 1) * HEAD_DIM)
            kk = jnp.concatenate([kp_ref[:, hs], kc_ref[:, hs]], axis=0).astype(BF16)
            vv = jnp.concatenate([vp_ref[:, hs], vc_ref[:, hs]], axis=0).astype(BF16)
            q = _stack_heads(q_ref, h)
            do = _stack_heads(do_ref, h)
            p, ps = _attn_probs(q, kk, mask, _stack_sinks(s_ref, h))
            dp = _dot(do, vv, 1, 1)
            delta = jnp.sum(p * dp, axis=-1, keepdims=True)
            dsc = (p * (dp - delta) * (HEAD_DIM ** -0.5)).astype(BF16)
            dsink = -ps * delta
            dq = _dot(dsc, kk, 1, 0).astype(BF16)
            for g in range(GROUP):
                hg = h * GROUP + g
                rows = slice(g * WINDOW, (g + 1) * WINDOW)
                ds_ref[hg:hg + 1, :] += jnp.broadcast_to(jnp.sum(dsink[rows], axis=0, keepdims=True), (1, 128))
                dp_ref[:, hg * HEAD_DIM:(hg + 1) * HEAD_DIM] = dq[rows]
            dkk = _dot(dsc, q, 0, 0)
            dvv = _dot(p.astype(BF16), do, 0, 0)
            dp_ref[:, kat + h * HEAD_DIM:kat + (h + 1) * HEAD_DIM] = (dkk[WINDOW:] + ck_ref[:, hs]).astype(BF16)
            dp_ref[:, vat + h * HEAD_DIM:vat + (h + 1) * HEAD_DIM] = (dvv[WINDOW:] + cv_ref[:, hs]).astype(BF16)
            ck_ref[:, hs] = dkk[:WINDOW]
            cv_ref[:, hs] = dvv[:WINDOW]

    q, kc, kp, vc, vp, sk, o = _attn_specs(nb, True)
    return pl.pallas_call(
        body, name="attn_bwd", grid=(nb,), in_specs=[q, kc, kp, vc, vp, sk, o, ANY],
        out_specs=[pl.BlockSpec((WINDOW, W_ATTN), lambda n: (nb - 1 - n, O_ATTN // W_ATTN)),
                   pl.BlockSpec((N_Q, 128), lambda n: (0, 0))],
        out_shape=[jax.ShapeDtypeStruct((T, MAIN), BF16), jax.ShapeDtypeStruct((N_Q, 128), F32)],
        scratch_shapes=[pltpu.VMEM((WINDOW, 256), F32), pltpu.VMEM((WINDOW, 256), F32)],
        input_output_aliases={7: 0}, compiler_params=_params(1),
    )(proj, proj, proj, proj, proj, sinks, d_o, dproj)


def _split3(x):
    hi = x.astype(BF16)
    r1 = x - hi.astype(F32)
    mid = r1.astype(BF16)
    lo = (r1 - mid.astype(F32)).astype(BF16)
    return hi, mid, lo


def _tri_sum(tri, x):
    hi, mid, lo = _split3(x)
    return _dot(tri, hi, 1, 0) + _dot(tri, mid, 1, 0) + _dot(tri, lo, 1, 0)


def _gla_chunk(q, k, lr, w2, b, lower):
    logit = _dot(lr.astype(BF16), w2.astype(BF16), 1, 0) + b
    la = (jnp.minimum(logit, 0.0) - jnp.log(1.0 + jnp.exp(-jnp.abs(logit)))) * (1.0 / GATE_NORM)
    g = _tri_sum(lower, la)
    gl = g[CHUNK - 1:CHUNK, :]
    eg = jnp.exp(g)
    qd = q * (DK ** -0.5) * eg
    ki = k * jnp.exp(-g)
    ke = k * jnp.exp(gl - g)
    return logit, g, gl, eg, qd, ki, ke


def _tri(lower):
    r = lax.broadcasted_iota(jnp.int32, (CHUNK, CHUNK), 0)
    c = lax.broadcasted_iota(jnp.int32, (CHUNK, CHUNK), 1)
    return (r >= c) if lower else (r <= c)


def _gla_rows(T):
    return _pick(T, (256, 128, 64))


def _gla_fwd(proj, proj_lr, w2p, gate_b):
    T = proj.shape[0]
    rb = _gla_rows(T)
    per = rb // CHUNK

    def body(q_ref, k_ref, v_ref, lr_ref, w2_ref, b_ref, o_ref, st_ref, s_scr):
        @pl.when(pl.program_id(1) == 0)
        def _():
            s_scr[...] = jnp.zeros_like(s_scr)

        low = _tri(True)
        lower = low.astype(BF16)
        for i in range(per):
            rows = slice(i * CHUNK, (i + 1) * CHUNK)
            _, g, gl, eg, qd, ki, ke = _gla_chunk(q_ref[rows, :], k_ref[rows, :], lr_ref[rows, :], w2_ref[...],
                                                  b_ref[...], lower)
            v = v_ref[rows, :].astype(BF16)
            qdb = qd.astype(BF16)
            att = jnp.where(low, _dot(qdb, ki.astype(BF16), 1, 1), 0.0)
            st = s_scr[...]
            st_ref[0, i] = st
            o_ref[rows, :] = _dot(att.astype(BF16), v, 1, 0) + _dot(qdb, st.astype(BF16), 1, 1)
            s_scr[...] = st * jnp.exp(gl) + _dot(v, ke.astype(BF16), 0, 0)

    return pl.pallas_call(
        body, name="gla_fwd", grid=(GLA_H, T // rb),
        in_specs=[pl.BlockSpec((rb, DK), lambda h, n: (n, (O_GLA + W_GLA * h) // DK)),
                  pl.BlockSpec((rb, DK), lambda h, n: (n, (O_GLA + W_GLA * h) // DK + 1)),
                  pl.BlockSpec((rb, DV), lambda h, n: (n, (O_GLA + W_GLA * h) // DV + 1)),
                  pl.BlockSpec((rb, LRP), lambda h, n: (n, 0)),
                  pl.BlockSpec((LRP, DK), lambda h, n: (0, h)),
                  pl.BlockSpec((1, DK), lambda h, n: (0, h))],
        out_specs=[pl.BlockSpec((rb, DV), lambda h, n: (n, h)),
                   pl.BlockSpec((1, per, DV, DK), lambda h, n: (h, n, 0, 0))],
        out_shape=[jax.ShapeDtypeStruct((T, GLA_H * DV), F32),
                   jax.ShapeDtypeStruct((GLA_H, T // CHUNK, DV, DK), F32)],
        scratch_shapes=[pltpu.VMEM((DV, DK), F32)], compiler_params=_params(2),
    )(proj, proj, proj, proj_lr, w2p, gate_b)


def _gla_bwd(proj, proj_lr, w2p, gate_b, states, d_o, dproj):
    T = proj.shape[0]
    rb = _gla_rows(T)
    per = rb // CHUNK
    nblk = T // rb

    def body(q_ref, k_ref, v_ref, lr_ref, w2_ref, b_ref, st_ref, do_ref, _, dp_ref, dl_ref, ds_scr):
        @pl.when(pl.program_id(1) == 0)
        def _():
            ds_scr[...] = jnp.zeros_like(ds_scr)

        low = _tri(True)
        lower = low.astype(BF16)
        upper = _tri(False).astype(BF16)
        for i in reversed(range(per)):
            rows = slice(i * CHUNK, (i + 1) * CHUNK)
            logit, g, gl, eg, qd, ki, ke = _gla_chunk(q_ref[rows, :], k_ref[rows, :], lr_ref[rows, :], w2_ref[...],
                                                      b_ref[...], lower)
            v = v_ref[rows, :].astype(BF16)
            do = do_ref[rows, :].astype(BF16)
            qdb, kib, keb = qd.astype(BF16), ki.astype(BF16), ke.astype(BF16)
            att = jnp.where(low, _dot(qdb, kib, 1, 1), 0.0).astype(BF16)
            sp = st_ref[0, i]
            ds = ds_scr[...]
            dsb = ds.astype(BF16)
            datt = jnp.where(low, _dot(do, v, 1, 1), 0.0).astype(BF16)
            dp_ref[rows, 2 * DK:] = (_dot(att, do, 0, 0) + _dot(keb, dsb, 1, 1)).astype(BF16)
            dqd = _dot(datt, kib, 1, 0) + _dot(do, sp.astype(BF16), 1, 0)
            dki = _dot(datt, qdb, 0, 0)
            dke = _dot(v, dsb, 1, 0)
            decay = jnp.exp(gl)
            ds_scr[...] = ds * decay + _dot(do, qdb, 0, 0)
            ddec = jnp.sum(ds * sp, axis=0, keepdims=True)
            dp_ref[rows, 0:DK] = (dqd * (DK ** -0.5) * eg).astype(BF16)
            dp_ref[rows, DK:2 * DK] = (dki * jnp.exp(-g) + dke * jnp.exp(gl - g)).astype(BF16)
            dke_ke = dke * ke
            dg = dqd * qd - dki * ki - dke_ke
            dgl = jnp.sum(dke_ke, axis=0, keepdims=True) + ddec * decay
            dla = _tri_sum(upper, dg) + dgl
            dl_ref[rows, :] = dla * (1.0 / GATE_NORM) * (1.0 - _sigmoid(logit))

    def rev(n):
        return nblk - 1 - n

    return pl.pallas_call(
        body, name="gla_bwd", grid=(GLA_H, nblk),
        in_specs=[pl.BlockSpec((rb, DK), lambda h, n: (rev(n), (O_GLA + W_GLA * h) // DK)),
                  pl.BlockSpec((rb, DK), lambda h, n: (rev(n), (O_GLA + W_GLA * h) // DK + 1)),
                  pl.BlockSpec((rb, DV), lambda h, n: (rev(n), (O_GLA + W_GLA * h) // DV + 1)),
                  pl.BlockSpec((rb, LRP), lambda h, n: (rev(n), 0)),
                  pl.BlockSpec((LRP, DK), lambda h, n: (0, h)),
                  pl.BlockSpec((1, DK), lambda h, n: (0, h)),
                  pl.BlockSpec((1, per, DV, DK), lambda h, n: (h, rev(n), 0, 0)),
                  pl.BlockSpec((rb, DV), lambda h, n: (rev(n), h)),
                  ANY],
        out_specs=[pl.BlockSpec((rb, W_GLA), lambda h, n: (rev(n), O_GLA // W_GLA + h)),
                   pl.BlockSpec((rb, DK), lambda h, n: (rev(n), h))],
        out_shape=[jax.ShapeDtypeStruct((T, MAIN), BF16), jax.ShapeDtypeStruct((T, GLA_H * DK), F32)],
        scratch_shapes=[pltpu.VMEM((DV, DK), F32)], input_output_aliases={8: 0}, compiler_params=_params(2),
    )(proj, proj, proj, proj_lr, w2p, gate_b, states, d_o, dproj)


def _local_step(x, target, p, reduce_part):
    u, u_t = _rms_fwd(x, p["norm1_w"], name="rms1_fwd")
    proj = _matmul(u, p["w_in_main"], name="mm_proj")
    proj_lr = _matmul(u, p["w_in_lr"], name="mm_proj_lr")
    attn_o = _attn_fwd(proj, p["sinks"])
    gla_raw, states = _gla_fwd(proj, proj_lr, p["w2p"], p["gate_b"])
    merged, merged_t = _merge_fwd(attn_o, gla_raw, proj, p["gla_norm_w"])
    h1 = _matmul(merged, p["w_out"], add=x, name="mm_out")
    v2, v2_t = _rms_fwd(h1, p["norm2_w"], name="rms2_fwd")
    gate = _matmul(v2, p["w_gate"], b_slots=True, name="mm_gate")
    up = _matmul(v2, p["w_up"], b_slots=True, name="mm_up")
    ff, ff_t = _swiglu_fwd(gate, up)
    h2 = _matmul(ff, p["w_down"], add=h1, name="mm_down")
    loss, dh2, dh2_b, g_final = _final_loss(h2, p["final_norm_w"], target)

    dff = _matmul(dh2_b, p["w_down"], tb=True, name="mm_dff")
    reduce_part("w_down", _matmul(ff_t, dh2_b, out_dtype=BF16, name="mm_gdown").reshape(4, FFN // 4, D))
    dgate, dup = _swiglu_bwd(dff, gate, up)
    reduce_part("w_gate", _matmul(v2_t, dgate, out_dtype=BF16, bj=FFN // 4, out_slots=True, name="mm_ggate"))
    reduce_part("w_up", _matmul(v2_t, dup, out_dtype=BF16, bj=FFN // 4, out_slots=True, name="mm_gup"))
    dv2 = _matmul(dgate, p["w_gate"], tb=True, b_slots=True, name="mm_dv2a")
    dv2 = _matmul(dup, p["w_up"], tb=True, b_slots=True, add=dv2, name="mm_dv2b")
    dh1, dh1_b, g_norm2 = _rms_bwd(dv2, h1, p["norm2_w"], dh2, name="rms2_bwd")
    dmerged = _matmul(dh1_b, p["w_out"], tb=True, name="mm_dmerged")
    reduce_part("w_out", _matmul(merged_t, dh1_b, out_dtype=BF16, name="mm_gout").reshape(4, D // 4, D))
    d_attn, d_gla, dproj, g_gla_norm = _merge_bwd(dmerged, attn_o, gla_raw, proj, p["gla_norm_w"])
    dproj, dlogit = _gla_bwd(proj, proj_lr, p["w2p"], p["gate_b"], states, d_gla, dproj)
    dproj, g_sinks = _attn_bwd(proj, p["sinks"], d_attn, dproj)
    g_gate_b = _colsum(dlogit, name="colsum_gate_b")
    g_w2 = _matmul(proj_lr, dlogit, ta=True, name="mm_gw2")
    dproj_lr = _matmul(dlogit, p["w2p"], tb=True, out_dtype=BF16, name="mm_dlr")
    g_in_main = _matmul(u_t, dproj, out_dtype=BF16, name="mm_gin")
    g_in_lr = _matmul(u_t, dproj_lr, out_dtype=BF16, name="mm_gin_lr")
    du = _matmul(dproj, p["w_in_main"], tb=True, name="mm_du")
    du = _matmul(dproj_lr, p["w_in_lr"], tb=True, add=du, name="mm_du_lr")
    dx, _, g_norm1 = _rms_bwd(du, x, p["norm1_w"], dh1, name="rms1_bwd")
    grads = dict(norm1_w=g_norm1, w_in_main=g_in_main, w_in_lr=g_in_lr, w2=g_w2[:RANK], gate_b=g_gate_b,
                 sinks=g_sinks[:, 0].reshape(1, N_Q), gla_norm_w=g_gla_norm, norm2_w=g_norm2, final_norm_w=g_final)
    return loss, dx, grads


def _place():
    x, y, c = lax.axis_index("x"), lax.axis_index("y"), lax.axis_index("c")
    chips = [(1 - x, y), (x, 1 - y), (1 - x, 1 - y)]
    return x, y, c, chips


def _hbm_shape(s, dt):
    return jax.ShapeDtypeStruct(s, dt)


def _handshake(peers):
    barrier = pltpu.get_barrier_semaphore()
    for peer in peers:
        pl.semaphore_signal(barrier, inc=1, device_id=peer, device_id_type=MESH)
    pl.semaphore_wait(barrier, len(peers))


def _launch_copies(body, args, out_shape, sems, *, name, collective_id=None):
    if collective_id is None:
        return pl.pallas_call(
            body, name=name, in_specs=[ANY] * len(args), out_specs=[ANY] * len(out_shape), out_shape=out_shape,
            scratch_shapes=sems)(*args)
    return pl.kernel(
        body, name=name, out_type=out_shape, mesh=plsc.ScalarSubcoreMesh(axis_name="sequencer", num_cores=1),
        scratch_types=sems, compiler_params=pltpu.CompilerParams(collective_id=collective_id))(*args)


def _gather_shards(shards, *, name, collective_id=None, after=()):
    n = len(shards)
    first_out = n + len(after)

    def body(*refs):
        ins, outs = refs[:n], refs[first_out:first_out + n]
        ici_send, ici_recv, d2d_send, d2d_recv, local_sem = refs[first_out + n:]
        x, y, c, chips = _place()
        me = 2 * x + y
        sibling = (x, y, 1 - c)
        if collective_id is not None:
            _handshake([sibling] + [(*chip, c) for chip in chips])

        def half(w, slot, hc):
            r2 = shards[w].shape[0] // 2
            return outs[w].at[slot, pl.ds(hc * r2, r2), :]

        locals_ = [pltpu.make_async_copy(ins[w], outs[w].at[me], local_sem.at[w]) for w in range(n)]
        for cp in locals_:
            cp.start()
        sends = []
        for w in range(n):
            r2 = shards[w].shape[0] // 2
            for j, chip in enumerate(chips):
                cp = pltpu.make_async_remote_copy(
                    src_ref=ins[w].at[pl.ds(c * r2, r2), :], dst_ref=half(w, me, c),
                    send_sem=ici_send.at[w * 3 + j], recv_sem=ici_recv.at[w * 3 + j],
                    device_id=(*chip, c), device_id_type=MESH)
                cp.start()
                sends.append(cp)
        for w in range(n):
            for j, chip in enumerate(chips):
                slot = 2 * chip[0] + chip[1]
                got = half(w, slot, c)
                pltpu.make_async_remote_copy(
                    src_ref=got, dst_ref=got, send_sem=ici_send.at[w * 3 + j], recv_sem=ici_recv.at[w * 3 + j],
                    device_id=(*chip, c), device_id_type=MESH).wait_recv()
                cp = pltpu.make_async_remote_copy(
                    src_ref=got, dst_ref=got, send_sem=d2d_send.at[w * 3 + j], recv_sem=d2d_recv.at[w * 3 + j],
                    device_id=sibling, device_id_type=MESH)
                cp.start()
                sends.append(cp)
        for w in range(n):
            for j, chip in enumerate(chips):
                slot = 2 * chip[0] + chip[1]
                got = half(w, slot, 1 - c)
                pltpu.make_async_remote_copy(
                    src_ref=got, dst_ref=got, send_sem=d2d_send.at[w * 3 + j], recv_sem=d2d_recv.at[w * 3 + j],
                    device_id=sibling, device_id_type=MESH).wait_recv()
        for cp in sends:
            cp.wait_send()
        for cp in locals_:
            cp.wait()

    return _launch_copies(
        body, list(shards) + list(after), [_hbm_shape((4,) + s.shape, s.dtype) for s in shards],
        [pltpu.SemaphoreType.DMA((3 * n,)), pltpu.SemaphoreType.DMA((3 * n,)), pltpu.SemaphoreType.DMA((3 * n,)),
         pltpu.SemaphoreType.DMA((3 * n,)), pltpu.SemaphoreType.DMA((n,))],
        name=name, collective_id=collective_id)


def _gather_staged(w, small):
    R, C = w.shape
    R2 = R // 2
    rb = 256
    per = R2 // rb
    r2 = small.shape[0] // 2

    def body(w_ref, s_ref, out_ref, outs_ref, stage, ici_send, ici_recv, fwd_send, fwd_recv, in_sem, put_sem,
             push_send, push_recv):
        x, y, c, chips = _place()
        me = 2 * x + y
        sibling = (x, y, 1 - c)
        slots = [2 * chip[0] + chip[1] for chip in chips]

        def ici(j, src, dst, k):
            return pltpu.make_async_remote_copy(
                src_ref=src, dst_ref=dst, send_sem=ici_send.at[k], recv_sem=ici_recv.at[k],
                device_id=(*chips[j], c), device_id_type=MESH)

        sends = []
        for j in range(3):
            sends.append(ici(j, w_ref.at[pl.ds(c * R2, R2), :], out_ref.at[me, pl.ds(c * R2, R2), :], j))
            sends.append(ici(j, s_ref.at[pl.ds(c * r2, r2), :], outs_ref.at[me, pl.ds(c * r2, r2), :], 3 + j))
        for cp in sends:
            cp.start()
        mine = pltpu.make_async_copy(s_ref, outs_ref.at[me], put_sem.at[2])
        mine.start()

        def own(k, to_vmem):
            rows = pl.ds(k * rb, rb)
            if to_vmem:
                return pltpu.make_async_copy(w_ref.at[rows, :], stage.at[k % 2], in_sem.at[k % 2])
            return pltpu.make_async_copy(stage.at[k % 2], out_ref.at[me, rows, :], put_sem.at[k % 2])

        for k in range(R // rb):
            if k >= 2:
                own(k - 2, False).wait()
            own(k, True).start()
            own(k, True).wait()
            own(k, False).start()
        for k in range(R // rb - 2, R // rb):
            own(k, False).wait()

        def block(j, k, hc):
            return out_ref.at[slots[j], pl.ds(hc * R2 + k * rb, rb), :]

        def push(t, hc):
            j, k = divmod(t, per)
            return pltpu.make_async_remote_copy(
                src_ref=stage.at[t % 2], dst_ref=block(j, k, hc), send_sem=push_send.at[t % 2],
                recv_sem=push_recv.at[t], device_id=sibling, device_id_type=MESH)

        passed = []
        for j in range(3):
            ici(j, w_ref.at[pl.ds(c * R2, R2), :], out_ref.at[slots[j], pl.ds(c * R2, R2), :], j).wait_recv()
            for k in range(per):
                t = j * per + k
                if t >= 2:
                    push(t - 2, c).wait_send()
                fetch = pltpu.make_async_copy(block(j, k, c), stage.at[t % 2], in_sem.at[t % 2])
                fetch.start()
                fetch.wait()
                push(t, c).start()
            got = outs_ref.at[slots[j], pl.ds(c * r2, r2), :]
            ici(j, got, got, 3 + j).wait_recv()
            fw = pltpu.make_async_remote_copy(
                src_ref=got, dst_ref=got, send_sem=fwd_send.at[j], recv_sem=fwd_recv.at[j],
                device_id=sibling, device_id_type=MESH)
            fw.start()
            passed.append(fw)
        for t in range(3 * per - 2, 3 * per):
            push(t, c).wait_send()
        for t in range(3 * per):
            push(t, 1 - c).wait_recv()
        for j in range(3):
            theirs = outs_ref.at[slots[j], pl.ds((1 - c) * r2, r2), :]
            pltpu.make_async_remote_copy(
                src_ref=theirs, dst_ref=theirs, send_sem=fwd_send.at[j], recv_sem=fwd_recv.at[j],
                device_id=sibling, device_id_type=MESH).wait_recv()
        for cp in sends + passed:
            cp.wait_send()
        mine.wait()

    dma = pltpu.SemaphoreType.DMA
    return pl.pallas_call(
        body, name="gather_in", in_specs=[ANY, ANY], out_specs=[ANY, ANY],
        out_shape=[_hbm_shape((4, R, C), w.dtype), _hbm_shape((4,) + small.shape, small.dtype)],
        scratch_shapes=[pltpu.VMEM((2, rb, C), w.dtype), dma((6,)), dma((6,)), dma((3,)), dma((3,)), dma((2,)),
                        dma((3,)), dma((2,)), dma((3 * per,))],
        compiler_params=pltpu.CompilerParams(vmem_limit_bytes=VMEM_LIMIT),
    )(w, small)


def _pair_blocks(R2):
    return _pick(R2, (256, 352, 128, 64, 32, 16))


def _pair_reduce(part, *, name):
    _, R, C = part.shape
    R2 = R // 2
    rb = _pair_blocks(R2)
    nblk = R2 // rb
    steps = [(s, i) for s in range(4) for i in range(nblk)]
    n = len(steps)

    def body(part_ref, sums_ref, own_buf, oth_buf, rcv_buf, out_buf, own_sem, oth_sem, out_sem, send_sem, recv_sem):
        x, y, c, _ = _place()
        sibling = (x, y, 1 - c)

        def fetch(t, half, buf, sem):
            s, i = steps[t]
            return pltpu.make_async_copy(part_ref.at[s, pl.ds(half * R2 + i * rb, rb), :], buf.at[t % 2], sem.at[t % 2])

        def push(t):
            return pltpu.make_async_remote_copy(
                src_ref=oth_buf.at[t % 2], dst_ref=rcv_buf.at[t % 2], send_sem=send_sem.at[t % 2],
                recv_sem=recv_sem.at[t % 2], device_id=sibling, device_id_type=MESH)

        def store(t):
            s, i = steps[t]
            return pltpu.make_async_copy(out_buf.at[t % 2], sums_ref.at[s, pl.ds(i * rb, rb), :], out_sem.at[t % 2])

        fetch(0, c, own_buf, own_sem).start()
        fetch(0, 1 - c, oth_buf, oth_sem).start()
        for t in range(n):
            if t + 1 < n:
                fetch(t + 1, c, own_buf, own_sem).start()
                fetch(t + 1, 1 - c, oth_buf, oth_sem).start()
            fetch(t, 1 - c, oth_buf, oth_sem).wait()
            p = push(t)
            p.start()
            fetch(t, c, own_buf, own_sem).wait()
            p.wait()
            if t >= 2:
                store(t - 2).wait()
            out_buf[t % 2] = (own_buf[t % 2].astype(F32) + rcv_buf[t % 2].astype(F32)).astype(BF16)
            store(t).start()
        for t in range(max(n - 2, 0), n):
            store(t).wait()

    buf = pltpu.VMEM((2, rb, C), BF16)
    sem2 = pltpu.SemaphoreType.DMA((2,))
    return pl.pallas_call(
        body, name=name, in_specs=[ANY], out_specs=ANY, out_shape=_hbm_shape((4, R2, C), BF16),
        scratch_shapes=[buf, buf, buf, buf, sem2, sem2, sem2, sem2, sem2],
        compiler_params=pltpu.CompilerParams(vmem_limit_bytes=VMEM_LIMIT),
    )(part)


def _chip_exchange(sums, *, name, collective_id=None):
    n = len(sums)

    def body(*refs):
        ins, outs = refs[:n], refs[n:2 * n]
        send_sem, recv_sem, local_sem = refs[2 * n:]
        x, y, c, chips = _place()
        me = 2 * x + y
        if collective_id is not None:
            _handshake([(*chip, c) for chip in chips])
        cps = []
        for w in range(n):
            lc = pltpu.make_async_copy(ins[w].at[me], outs[w].at[me], local_sem.at[w])
            lc.start()
            cps.append(lc)
            for j, chip in enumerate(chips):
                slot = 2 * chip[0] + chip[1]
                rc = pltpu.make_async_remote_copy(
                    src_ref=ins[w].at[slot], dst_ref=outs[w].at[me],
                    send_sem=send_sem.at[w * 3 + j], recv_sem=recv_sem.at[w * 3 + j],
                    device_id=(*chip, c), device_id_type=MESH)
                rc.start()
                cps.append(rc)
        for w in range(n):
            for j, chip in enumerate(chips):
                slot = 2 * chip[0] + chip[1]
                pltpu.make_async_remote_copy(
                    src_ref=ins[w].at[slot], dst_ref=outs[w].at[slot],
                    send_sem=send_sem.at[w * 3 + j], recv_sem=recv_sem.at[w * 3 + j],
                    device_id=(*chip, c), device_id_type=MESH).wait_recv()
        for w in range(n):
            cps[w * 4].wait()
            for j in range(3):
                cps[w * 4 + 1 + j].wait_send()

    return _launch_copies(
        body, sums, [_hbm_shape(s.shape, s.dtype) for s in sums],
        [pltpu.SemaphoreType.DMA((3 * n,)), pltpu.SemaphoreType.DMA((3 * n,)), pltpu.SemaphoreType.DMA((n,))],
        name=name, collective_id=collective_id)


def _sum_join(recv, *, name):
    _, R2, C = recv.shape
    rb = _pair_blocks(R2)
    nblk = R2 // rb

    def body(recv_ref, out_ref, in_buf, acc_buf, in_sem, loc_sem, send_sem, recv_sem):
        x, y, c, _ = _place()
        sibling = (x, y, 1 - c)

        def fetch(t):
            return pltpu.make_async_copy(recv_ref.at[:, pl.ds(t * rb, rb), :], in_buf.at[t % 2], in_sem.at[t % 2])

        def rows(t, half):
            return out_ref.at[pl.ds(half * R2 + t * rb, rb), :]

        def put_local(t):
            return pltpu.make_async_copy(acc_buf.at[t % 2], rows(t, c), loc_sem.at[t])

        def put_remote(t, half):
            return pltpu.make_async_remote_copy(
                src_ref=acc_buf.at[t % 2], dst_ref=rows(t, half), send_sem=send_sem.at[t], recv_sem=recv_sem.at[t],
                device_id=sibling, device_id_type=MESH)

        fetch(0).start()
        for t in range(nblk):
            if t + 1 < nblk:
                fetch(t + 1).start()
            fetch(t).wait()
            if t >= 2:
                put_local(t - 2).wait()
                put_remote(t - 2, c).wait_send()
            acc = in_buf[t % 2, 0].astype(F32)
            for s in range(1, 4):
                acc = acc + in_buf[t % 2, s].astype(F32)
            acc_buf[t % 2] = acc
            put_local(t).start()
            put_remote(t, c).start()
        for t in range(max(nblk - 2, 0), nblk):
            put_local(t).wait()
            put_remote(t, c).wait_send()
        for t in range(nblk):
            put_remote(t, 1 - c).wait_recv()

    semn = pltpu.SemaphoreType.DMA((nblk,))
    return pl.pallas_call(
        body, name=name, in_specs=[ANY], out_specs=ANY, out_shape=_hbm_shape((2 * R2, C), F32),
        scratch_shapes=[pltpu.VMEM((2, 4, rb, C), BF16), pltpu.VMEM((2, rb, C), F32),
                        pltpu.SemaphoreType.DMA((2,)), semn, semn, semn],
        compiler_params=pltpu.CompilerParams(vmem_limit_bytes=VMEM_LIMIT),
    )(recv)


def _sum_small(pack):
    R, C = pack.shape

    def body(in_ref, out_ref, all_ref, send_sem, recv_sem):
        x, y, c, _ = _place()
        me = 4 * x + 2 * y + c
        all_ref[me] = in_ref[...]
        cps = []
        for k in range(1, 8):
            peer = (x ^ (k >> 2), y ^ ((k >> 1) & 1), c ^ (k & 1))
            cp = pltpu.make_async_remote_copy(
                src_ref=in_ref, dst_ref=all_ref.at[me], send_sem=send_sem.at[k - 1], recv_sem=recv_sem.at[k - 1],
                device_id=peer, device_id_type=MESH)
            cp.start()
            cps.append(cp)
        for k in range(1, 8):
            peer = (x ^ (k >> 2), y ^ ((k >> 1) & 1), c ^ (k & 1))
            slot = 4 * peer[0] + 2 * peer[1] + peer[2]
            pltpu.make_async_remote_copy(
                src_ref=in_ref, dst_ref=all_ref.at[slot], send_sem=send_sem.at[k - 1], recv_sem=recv_sem.at[k - 1],
                device_id=peer, device_id_type=MESH).wait_recv()
        for cp in cps:
            cp.wait_send()
        acc = all_ref[0]
        for d in range(1, 8):
            acc = acc + all_ref[d]
        out_ref[...] = acc

    return pl.pallas_call(
        body, name="sum_small", in_specs=[pl.BlockSpec(memory_space=pltpu.VMEM)],
        out_specs=pl.BlockSpec(memory_space=pltpu.VMEM), out_shape=jax.ShapeDtypeStruct((R, C), F32),
        scratch_shapes=[pltpu.VMEM((8, R, C), F32), pltpu.SemaphoreType.DMA((7,)), pltpu.SemaphoreType.DMA((7,))],
    )(pack)


ADAMW_BLOCK_BYTES = 2 * 1024 * 1024


def _adamw_block(R, C):
    padded = -(-C // 128) * 128
    rows = [rb for rb in range(8, R + 1, 8) if R % rb == 0 and rb * padded * 4 <= ADAMW_BLOCK_BYTES]
    if rows or R * padded * 4 <= ADAMW_BLOCK_BYTES:
        return (max(rows) if rows else R), C
    cols = [cb for cb in range(128, C + 1, 128) if C % cb == 0 and R * cb * 4 <= ADAMW_BLOCK_BYTES]
    return R, max(cols)


def _adamw(w, g, m, v, *, name):
    R, C = w.shape
    rb, cb = _adamw_block(R, C)
    c1 = 1.0 / (1.0 - B1 ** STEP)
    c2 = 1.0 / (1.0 - B2 ** STEP)

    def body(w_ref, g_ref, m_ref, v_ref, d_ref, nm_ref, nv_ref):
        gv = g_ref[...]
        m2 = B1 * m_ref[...] + (1.0 - B1) * gv
        v2 = B2 * v_ref[...] + (1.0 - B2) * gv * gv
        nm_ref[...] = m2
        nv_ref[...] = v2
        d_ref[...] = -LR * ((m2 * c1) / (jnp.sqrt(v2 * c2) + ADAM_EPS) + WD * w_ref[...])

    blk = pl.BlockSpec((rb, cb), lambda i, j: (i, j))
    return pl.pallas_call(
        body, name=name, grid=(R // rb, C // cb), in_specs=[blk] * 4, out_specs=[blk] * 3,
        out_shape=[jax.ShapeDtypeStruct((R, C), F32)] * 3, compiler_params=_params(2),
    )(w, g, m, v)


SMALL = (("norm1_w", D), ("norm2_w", D), ("final_norm_w", D), ("gate_b", GLA_H * DK), ("gla_norm_w", DV), ("sinks", N_Q))
PACK_W = 1024


def _pack_small(vals, w2, loss):
    rows = []
    for name, width in SMALL:
        v = vals[name].reshape(-1)
        rows.append(jnp.pad(v, (0, (-width) % PACK_W)).reshape(-1, PACK_W))
    rows.append(w2)
    rows.append(jnp.broadcast_to(loss.reshape(1, 1), (1, PACK_W)))
    pack = jnp.concatenate(rows, axis=0)
    return jnp.pad(pack, ((0, 32 - pack.shape[0]), (0, 0)))


def _unpack_small(pack):
    out, r = {}, 0
    for name, width in SMALL:
        nr = -(-width // PACK_W)
        out[name] = pack[r:r + nr].reshape(-1)[:width]
        r += nr
    out["w2"] = pack[r:r + RANK]
    out["loss"] = pack[r + RANK, 0]
    return out


def kernel(x, norm1_w, w_in, gla_gate_w2, gla_gate_b, attn_sinks, gla_norm_w, w_out, norm2_w, w_ffn_gate, w_ffn_up, w_ffn_down, final_norm_w, loss_target, m_norm1_w, m_w_in, m_gla_gate_w2, m_gla_gate_b, m_attn_sinks, m_gla_norm_w, m_w_out, m_norm2_w, m_w_ffn_gate, m_w_ffn_up, m_w_ffn_down, m_final_norm_w, v_norm1_w, v_w_in, v_gla_gate_w2, v_gla_gate_b, v_attn_sinks, v_gla_norm_w, v_w_out, v_norm2_w, v_w_ffn_gate, v_w_ffn_up, v_w_ffn_down, v_final_norm_w):
    chip = 2 * lax.axis_index("x") + lax.axis_index("y")
    w_in_s, w_out_s, w_gate_s, w_up_s, w_down_s, w2_s = (
        w_in[0], w_out[0], w_ffn_gate[0], w_ffn_up[0], w_ffn_down[0], gla_gate_w2[0])
    CS = SHARD_COLS

    g_in, g_w2 = _gather_staged(w_in_s.astype(BF16), w2_s)
    g_out, = _gather_shards([w_out_s.astype(BF16)], name="gather_out", collective_id=1, after=[g_w2])
    g_gate, g_up = _gather_shards([w_gate_s.astype(BF16), w_up_s.astype(BF16)], name="gather_gate_up",
                                  collective_id=7, after=[g_w2])
    g_down, = _gather_shards([w_down_s.astype(BF16)], name="gather_down", collective_id=8, after=[g_w2])
    w_main, w_lr = _w_in_to_main(g_in)
    w2_full = jnp.transpose(g_w2, (1, 0, 2)).reshape(RANK, GLA_H * DK)
    p = dict(
        norm1_w=norm1_w, norm2_w=norm2_w, final_norm_w=final_norm_w.reshape(1, D), gate_b=gla_gate_b,
        gla_norm_w=gla_norm_w, sinks=jnp.pad(attn_sinks, ((0, 0), (0, 128 - N_Q))),
        w_in_main=w_main, w_in_lr=w_lr,
        w2p=jnp.pad(w2_full, ((0, LRP - RANK), (0, 0))).astype(BF16),
        w_out=g_out.reshape(D, D),
        w_gate=g_gate, w_up=g_up, w_down=g_down.reshape(FFN, D),
    )

    tags = ["w_in", "w_out", "w_gate", "w_up", "w_down"]
    recv = {}

    def reduce_part(tag, part):
        sums = _pair_reduce(part, name="pair_reduce_" + tag)
        recv[tag], = _chip_exchange([sums], name="chip_exchange_" + tag, collective_id=2 + tags.index(tag))

    loss_blk, dx, g = _local_step(x[0], loss_target[0], p, reduce_part)
    reduce_part("w_in", _main_to_shards(g["w_in_main"], g["w_in_lr"]))
    big = [_sum_join(recv[tg], name="sum_join_" + tg) for tg in tags]

    small = _unpack_small(_sum_small(_pack_small(g, g["w2"], loss_blk[0, 0])))
    loss = small["loss"]
    g_w2_mine = lax.dynamic_slice_in_dim(small["w2"], chip * (GLA_H * DK // 4), GLA_H * DK // 4, axis=1)

    grads = dict(
        norm1_w=small["norm1_w"].reshape(1, D), w_in=big[0], gla_gate_w2=g_w2_mine,
        gla_gate_b=small["gate_b"].reshape(1, -1), attn_sinks=small["sinks"].reshape(1, N_Q),
        gla_norm_w=small["gla_norm_w"].reshape(1, DV), w_out=big[1], norm2_w=small["norm2_w"].reshape(1, D),
        w_ffn_gate=big[2], w_ffn_up=big[3], w_ffn_down=big[4], final_norm_w=small["final_norm_w"].reshape(1, D))
    def lin(a):
        return jnp.transpose(a.reshape(D, CS))

    def unlin(a):
        return jnp.transpose(a)

    grads["w_in"] = lin(big[0])
    weights = dict(
        norm1_w=(norm1_w, m_norm1_w, v_norm1_w), w_in=(lin(w_in), lin(m_w_in), lin(v_w_in)),
        gla_gate_w2=(w2_s, m_gla_gate_w2[0], v_gla_gate_w2[0]), gla_gate_b=(gla_gate_b, m_gla_gate_b, v_gla_gate_b),
        attn_sinks=(attn_sinks, m_attn_sinks, v_attn_sinks), gla_norm_w=(gla_norm_w, m_gla_norm_w, v_gla_norm_w),
        w_out=(w_out_s, m_w_out[0], v_w_out[0]), norm2_w=(norm2_w, m_norm2_w, v_norm2_w),
        w_ffn_gate=(w_gate_s, m_w_ffn_gate[0], v_w_ffn_gate[0]), w_ffn_up=(w_up_s, m_w_ffn_up[0], v_w_ffn_up[0]),
        w_ffn_down=(w_down_s, m_w_ffn_down[0], v_w_ffn_down[0]),
        final_norm_w=(final_norm_w.reshape(1, D), m_final_norm_w.reshape(1, D), v_final_norm_w.reshape(1, D)))
    names = ["norm1_w", "w_in", "gla_gate_w2", "gla_gate_b", "attn_sinks", "gla_norm_w", "w_out", "norm2_w",
             "w_ffn_gate", "w_ffn_up", "w_ffn_down", "final_norm_w"]
    lead = {"norm1_w": False, "gla_gate_b": False, "attn_sinks": False, "gla_norm_w": False, "norm2_w": False}
    g_out_l, d_out, m_out, v_out = [], [], [], []
    for nm in names:
        w, m, v = weights[nm]
        gr = grads[nm]
        dl, nmn, nvn = _adamw(w, gr, m, v, name="adamw_" + nm)
        if nm == "w_in":
            gr, dl, nmn, nvn, w = big[0], unlin(dl), unlin(nmn), unlin(nvn), w_in_s
        if nm == "final_norm_w":
            shape = (D,)
        elif nm in lead:
            shape = w.shape
        else:
            shape = (1,) + w.shape
        g_out_l.append(gr.reshape(shape))
        d_out.append(dl.reshape(shape))
        m_out.append(nmn.reshape(shape))
        v_out.append(nvn.reshape(shape))
    return (loss, dx[None], *g_out_l, *d_out, *m_out, *v_out)
```

```python
import functools

import jax
import jax.numpy as jnp
from jax import lax
from jax.experimental import pallas as pl
from jax.experimental.pallas import tpu as pltpu
from jax.experimental.pallas import tpu_sc as plsc

F32 = jnp.float32
BF16 = jnp.bfloat16

D = 2048
HEAD_DIM = 64
N_Q = 32
N_KV = 4
GROUP = 8
WINDOW = 128
GLA_H = 4
DK = 256
DV = 512
RANK = 16
CHUNK = 64
FFN = 5632
EPS = 1e-6
MASK_VALUE = -1e30
GATE_NORM = 16.0
LR, B1, B2, ADAM_EPS, WD, STEP = 0.001, 0.9, 0.999, 1e-08, 0.01, 10

MAIN = 12800
O_MERGE, O_GLA, O_ATTN = 0, 6144, 10240
W_MERGE, W_GLA, W_ATTN = 3 * DV, 2 * DK + DV, D + 2 * N_KV * HEAD_DIM
LR_AT = 6656
LRP = 128


def _main_pieces():
    o_gq, o_gk, o_gv, o_gr, o_ga, o_gb = 2560, 3584, 4608, LR_AT + RANK, LR_AT + RANK + D, LR_AT + RANK + 2 * D
    pieces = []
    for h in range(GLA_H):
        pieces += [(o_gr + DV * h, DV), (o_ga + DV * h, DV), (o_gb + DV * h, DV)]
    for h in range(GLA_H):
        pieces += [(o_gq + DK * h, DK), (o_gk + DK * h, DK), (o_gv + DV * h, DV)]
    pieces.append((0, W_ATTN))
    return pieces


SHARD_COLS = 3204

VMEM_LIMIT = 56 * 1024 * 1024
MESH = pl.DeviceIdType.MESH
ANY = pl.BlockSpec(memory_space=pl.ANY)


def _params(ngrid):
    return pltpu.CompilerParams(dimension_semantics=("arbitrary",) * ngrid, vmem_limit_bytes=VMEM_LIMIT)


def _shard_segments(lo, hi):
    segs = []
    while lo < hi:
        s = lo // SHARD_COLS
        e = min(hi, (s + 1) * SHARD_COLS)
        segs.append((s, lo - s * SHARD_COLS, e - s * SHARD_COLS))
        lo = e
    return segs


def _lanes(pieces):
    return pieces[0] if len(pieces) == 1 else jnp.concatenate(pieces, axis=1)


def _w_in_to_main(shards):
    rb = 256

    def body(g_ref, main_ref, lr_ref):
        at = 0
        for a, w in _main_pieces():
            main_ref[:, at:at + w] = _lanes([g_ref[s, :, lo:hi] for s, lo, hi in _shard_segments(a, a + w)])
            at += w
        lr = [g_ref[s, :, lo:hi] for s, lo, hi in _shard_segments(LR_AT, LR_AT + RANK)]
        lr_ref[...] = _lanes(lr + [jnp.zeros((rb, LRP - RANK), lr_ref.dtype)])

    return pl.pallas_call(
        body, name="w_in_to_main", grid=(D // rb,),
        in_specs=[pl.BlockSpec((4, rb, SHARD_COLS), lambda i: (0, i, 0))],
        out_specs=[pl.BlockSpec((rb, MAIN), lambda i: (i, 0)), pl.BlockSpec((rb, LRP), lambda i: (i, 0))],
        out_shape=[jax.ShapeDtypeStruct((D, MAIN), shards.dtype), jax.ShapeDtypeStruct((D, LRP), shards.dtype)],
        compiler_params=_params(1),
    )(shards)


def _main_to_shards(g_main, g_lr):
    rb = 256
    at, sources = 0, [(LR_AT, RANK, 1, 0)]
    for a, w in _main_pieces():
        sources.append((a, w, 0, at))
        at += w
    sources.sort()

    def body(main_ref, lr_ref, out_ref):
        refs = (main_ref, lr_ref)
        for s in range(4):
            lo, hi = s * SHARD_COLS, (s + 1) * SHARD_COLS
            pieces = []
            for a, w, which, src in sources:
                b, e = max(a, lo), min(a + w, hi)
                if b < e:
                    pieces.append(refs[which][:, src + b - a:src + e - a])
            out_ref[s] = _lanes(pieces)

    return pl.pallas_call(
        body, name="main_to_shards", grid=(D // rb,),
        in_specs=[pl.BlockSpec((rb, MAIN), lambda i: (i, 0)), pl.BlockSpec((rb, LRP), lambda i: (i, 0))],
        out_specs=pl.BlockSpec((4, rb, SHARD_COLS), lambda i: (0, i, 0)),
        out_shape=jax.ShapeDtypeStruct((4, D, SHARD_COLS), g_main.dtype), compiler_params=_params(1),
    )(g_main, g_lr)


def _pick(n, cands):
    for c in cands:
        if n % c == 0:
            return c
    return n


def _sigmoid(x):
    return 1.0 / (1.0 + jnp.exp(-x))


def _dot(a, b, ca, cb):
    return lax.dot_general(a, b, (((ca,), (cb,)), ((), ())), preferred_element_type=F32)


def _matmul(a, b, *, name, ta=False, tb=False, add=None, out_dtype=F32, bm=None, bj=None, bc=None,
            b_slots=False, out_slots=False):
    C, M = a.shape if ta else a.shape[::-1]
    if b_slots:
        if tb:
            J, bc = b.shape[1], b.shape[2]
            assert b.shape[0] * bc == C
        else:
            bj = b.shape[2]
            J = b.shape[0] * bj
            assert b.shape[1] == C
    else:
        J = b.shape[0] if tb else b.shape[1]
        assert (b.shape[1] if tb else b.shape[0]) == C
    bm = bm or _pick(M, (1024, 512, 256, 128) if C <= FFN else (512, 256, 128))
    bj = bj or _pick(J, (1280, 1024, 512, 256, 128) if C <= D else (512, 256, 128))
    bc = bc or (C if C <= FFN else _pick(C, (6400,)))
    nm, nj, nc = M // bm, J // bj, C // bc
    has_add = add is not None

    def body(*refs):
        a_ref, b_ref = refs[0], refs[1]
        add_ref = refs[2] if has_add else None
        o_ref = refs[3] if has_add else refs[2]
        p = _dot(a_ref[...].astype(BF16), b_ref[...].astype(BF16), 0 if ta else 1, 1 if tb else 0)

        def finish(acc):
            if has_add:
                acc = acc + add_ref[...]
            o_ref[...] = acc.astype(o_ref.dtype)

        if nc == 1:
            finish(p)
        else:
            acc_ref = refs[-1]
            c = pl.program_id(2)

            @pl.when(c == 0)
            def _():
                acc_ref[...] = p

            @pl.when(c > 0)
            def _():
                acc_ref[...] += p

            @pl.when(c == nc - 1)
            def _():
                finish(acc_ref[...])

    a_spec = pl.BlockSpec((bc, bm), lambda m, j, c: (c, m)) if ta else pl.BlockSpec((bm, bc), lambda m, j, c: (m, c))
    if b_slots:
        b_spec = (pl.BlockSpec((None, bj, bc), lambda m, j, c: (c, j, 0)) if tb
                  else pl.BlockSpec((None, bc, bj), lambda m, j, c: (j, c, 0)))
    else:
        b_spec = (pl.BlockSpec((bj, bc), lambda m, j, c: (j, c)) if tb
                  else pl.BlockSpec((bc, bj), lambda m, j, c: (c, j)))
    if out_slots:
        assert not has_add
        o_spec = pl.BlockSpec((None, bm, bj), lambda m, j, c: (j, m, 0))
        out_shape = jax.ShapeDtypeStruct((nj, M, bj), out_dtype)
    else:
        o_spec = pl.BlockSpec((bm, bj), lambda m, j, c: (m, j))
        out_shape = jax.ShapeDtypeStruct((M, J), out_dtype)
    in_specs = [a_spec, b_spec] + ([o_spec] if has_add else [])
    args = (a, b) + ((add,) if has_add else ())
    return pl.pallas_call(
        body, name=name, grid=(nm, nj, nc), in_specs=in_specs, out_specs=o_spec,
        out_shape=out_shape,
        scratch_shapes=[pltpu.VMEM((bm, bj), F32)] if nc > 1 else [],
        compiler_params=_params(3),
    )(*args)


def _rows(T):
    return _pick(T, (256, 128))


def _rms_fwd(x, w, *, name):
    T = x.shape[0]
    rb = _rows(T)

    def body(x_ref, w_ref, u_ref, ut_ref):
        xv = x_ref[...]
        r = lax.rsqrt(jnp.mean(xv * xv, axis=-1, keepdims=True) + EPS)
        u = xv * r * w_ref[...]
        u_ref[...] = u.astype(BF16)
        ut_ref[...] = u.T.astype(BF16)

    return pl.pallas_call(
        body, name=name, grid=(T // rb,),
        in_specs=[pl.BlockSpec((rb, D), lambda i: (i, 0)), pl.BlockSpec((1, D), lambda i: (0, 0))],
        out_specs=[pl.BlockSpec((rb, D), lambda i: (i, 0)), pl.BlockSpec((D, rb), lambda i: (0, i))],
        out_shape=[jax.ShapeDtypeStruct((T, D), BF16), jax.ShapeDtypeStruct((D, T), BF16)], compiler_params=_params(1),
    )(x, w)


def _rms_bwd(dy, h, w, resid, *, name):
    T = h.shape[0]
    rb = _rows(T)

    def body(dy_ref, h_ref, w_ref, res_ref, dh_ref, dhb_ref, dw_ref):
        @pl.when(pl.program_id(0) == 0)
        def _():
            dw_ref[...] = jnp.zeros_like(dw_ref)

        hv = h_ref[...]
        r = lax.rsqrt(jnp.mean(hv * hv, axis=-1, keepdims=True) + EPS)
        hn = hv * r
        dyv = dy_ref[...]
        dw_ref[...] += jnp.sum(dyv * hn, axis=0, keepdims=True)
        t = dyv * w_ref[...]
        dh = res_ref[...] + r * (t - hn * jnp.mean(t * hn, axis=-1, keepdims=True))
        dh_ref[...] = dh
        dhb_ref[...] = dh.astype(BF16)

    blk = pl.BlockSpec((rb, D), lambda i: (i, 0))
    vec = pl.BlockSpec((1, D), lambda i: (0, 0))
    return pl.pallas_call(
        body, name=name, grid=(T // rb,), in_specs=[blk, blk, vec, blk], out_specs=[blk, blk, vec],
        out_shape=[jax.ShapeDtypeStruct((T, D), F32), jax.ShapeDtypeStruct((T, D), BF16),
                   jax.ShapeDtypeStruct((1, D), F32)],
        compiler_params=_params(1),
    )(dy, h, w, resid)


def _final_loss(h, w, target):
    T = h.shape[0]
    rb = _rows(T)

    def body(h_ref, w_ref, t_ref, loss_ref, dh_ref, dhb_ref, dw_ref):
        @pl.when(pl.program_id(0) == 0)
        def _():
            dw_ref[...] = jnp.zeros_like(dw_ref)
            loss_ref[...] = jnp.zeros_like(loss_ref)

        hv = h_ref[...]
        wv = w_ref[...]
        r = lax.rsqrt(jnp.mean(hv * hv, axis=-1, keepdims=True) + EPS)
        hn = hv * r
        e = hn * wv - t_ref[...]
        row = jnp.sum(e * e, axis=-1, keepdims=True) * (0.5 / D)
        loss_ref[...] += jnp.broadcast_to(jnp.sum(row, axis=0, keepdims=True), loss_ref.shape)
        dy = e * (1.0 / D)
        dw_ref[...] += jnp.sum(dy * hn, axis=0, keepdims=True)
        t = dy * wv
        dh = r * (t - hn * jnp.mean(t * hn, axis=-1, keepdims=True))
        dh_ref[...] = dh
        dhb_ref[...] = dh.astype(BF16)

    blk = pl.BlockSpec((rb, D), lambda i: (i, 0))
    vec = pl.BlockSpec((1, D), lambda i: (0, 0))
    return pl.pallas_call(
        body, name="final_loss", grid=(T // rb,), in_specs=[blk, vec, blk],
        out_specs=[pl.BlockSpec((8, 128), lambda i: (0, 0)), blk, blk, vec],
        out_shape=[jax.ShapeDtypeStruct((8, 128), F32), jax.ShapeDtypeStruct((T, D), F32),
                   jax.ShapeDtypeStruct((T, D), BF16), jax.ShapeDtypeStruct((1, D), F32)],
        compiler_params=_params(1),
    )(h, w, target)


PANEL = FFN // 4


def _ffn_fwd(v2, w_gate, w_up):
    T = v2.shape[0]
    bm = _pick(T, (512, 256))

    def body(a_ref, bg_ref, bu_ref, gate_ref, up_ref, ff_ref, fft_ref):
        a = a_ref[...]
        g = _dot(a, bg_ref[...], 1, 0)
        u = _dot(a, bu_ref[...], 1, 0)
        gate_ref[...] = g
        up_ref[...] = u
        ff = g * _sigmoid(g) * u
        ff_ref[...] = ff.astype(BF16)
        fft_ref[...] = ff.T.astype(BF16)

    panel = pl.BlockSpec((None, D, PANEL), lambda j, m: (j, 0, 0))
    out = pl.BlockSpec((bm, PANEL), lambda j, m: (m, j))
    return pl.pallas_call(
        body, name="ffn_fwd", grid=(4, T // bm),
        in_specs=[pl.BlockSpec((bm, D), lambda j, m: (m, 0)), panel, panel],
        out_specs=[out, out, out, pl.BlockSpec((PANEL, bm), lambda j, m: (j, m))],
        out_shape=[jax.ShapeDtypeStruct((T, FFN), F32), jax.ShapeDtypeStruct((T, FFN), F32),
                   jax.ShapeDtypeStruct((T, FFN), BF16), jax.ShapeDtypeStruct((FFN, T), BF16)],
        compiler_params=_params(2),
    )(v2, w_gate, w_up)


def _ffn_bwd_hidden(dh2, w_down, gate, up):
    T = dh2.shape[0]
    bm = _pick(T, (512, 256))

    def body(a_ref, b_ref, g_ref, u_ref, dg_ref, du_ref):
        d = _dot(a_ref[...], b_ref[...], 1, 1)
        g = g_ref[...]
        sg = _sigmoid(g)
        du_ref[...] = (d * g * sg).astype(BF16)
        dg_ref[...] = (d * u_ref[...] * sg * (1.0 + g * (1.0 - sg))).astype(BF16)

    blk = pl.BlockSpec((bm, PANEL), lambda j, m: (m, j))
    return pl.pallas_call(
        body, name="ffn_bwd_hidden", grid=(4, T // bm),
        in_specs=[pl.BlockSpec((bm, D), lambda j, m: (m, 0)), pl.BlockSpec((PANEL, D), lambda j, m: (j, 0)), blk, blk],
        out_specs=[blk, blk], out_shape=[jax.ShapeDtypeStruct((T, FFN), BF16)] * 2, compiler_params=_params(2),
    )(dh2, w_down, gate, up)


def _ffn_bwd_input(dgate, dup, w_gate, w_up):
    T = dgate.shape[0]
    bm = _pick(T, (1024, 512, 256))
    bj = 512

    def body(ag_ref, au_ref, bg_ref, bu_ref, o_ref, acc_ref):
        c = pl.program_id(2)
        p = _dot(ag_ref[...], bg_ref[...], 1, 1) + _dot(au_ref[...], bu_ref[...], 1, 1)

        @pl.when(c == 0)
        def _():
            acc_ref[...] = p

        @pl.when(c > 0)
        def _():
            acc_ref[...] += p

        @pl.when(c == 3)
        def _():
            o_ref[...] = acc_ref[...]

    a = pl.BlockSpec((bm, PANEL), lambda m, j, c: (m, c))
    b = pl.BlockSpec((None, bj, PANEL), lambda m, j, c: (c, j, 0))
    return pl.pallas_call(
        body, name="ffn_bwd_input", grid=(T // bm, D // bj, 4), in_specs=[a, a, b, b],
        out_specs=pl.BlockSpec((bm, bj), lambda m, j, c: (m, j)), out_shape=jax.ShapeDtypeStruct((T, D), F32),
        scratch_shapes=[pltpu.VMEM((bm, bj), F32)], compiler_params=_params(3),
    )(dgate, dup, w_gate, w_up)


def _colsum(x, *, name):
    T, W = x.shape
    rb = _rows(T)

    def body(x_ref, o_ref):
        @pl.when(pl.program_id(0) == 0)
        def _():
            o_ref[...] = jnp.zeros_like(o_ref)

        o_ref[...] += jnp.sum(x_ref[...], axis=0, keepdims=True)

    return pl.pallas_call(
        body, name=name, grid=(T // rb,), in_specs=[pl.BlockSpec((rb, W), lambda i: (i, 0))],
        out_specs=pl.BlockSpec((1, W), lambda i: (0, 0)), out_shape=jax.ShapeDtypeStruct((1, W), F32),
        compiler_params=_params(1),
    )(x)


def _merge_fwd(attn_o, gla_raw, proj, gla_norm_w):
    T = attn_o.shape[0]
    rb = _rows(T)

    def body(a_ref, g_ref, gr_ref, ga_ref, gb_ref, w_ref, o_ref, ot_ref):
        graw = g_ref[...]
        r = lax.rsqrt(jnp.mean(graw * graw, axis=-1, keepdims=True) + EPS)
        gr = gr_ref[...]
        go = graw * r * w_ref[...] * (gr * _sigmoid(gr))
        merged = _sigmoid(ga_ref[...]) * a_ref[...] + _sigmoid(gb_ref[...]) * go
        o_ref[...] = merged.astype(BF16)
        ot_ref[...] = merged.T.astype(BF16)

    def sec(k):
        return pl.BlockSpec((rb, DV), lambda i, h: (i, O_MERGE // DV + 3 * h + k))

    blk = pl.BlockSpec((rb, DV), lambda i, h: (i, h))
    return pl.pallas_call(
        body, name="merge_fwd", grid=(T // rb, GLA_H),
        in_specs=[blk, blk, sec(0), sec(1), sec(2), pl.BlockSpec((1, DV), lambda i, h: (0, 0))],
        out_specs=[blk, pl.BlockSpec((DV, rb), lambda i, h: (h, i))],
        out_shape=[jax.ShapeDtypeStruct((T, D), BF16), jax.ShapeDtypeStruct((D, T), BF16)], compiler_params=_params(2),
    )(attn_o, gla_raw, proj, proj, proj, gla_norm_w)


def _merge_bwd(dm, attn_o, gla_raw, proj, gla_norm_w):
    T = attn_o.shape[0]
    rb = _rows(T)

    def body(dm_ref, a_ref, g_ref, gr_ref, ga_ref, gb_ref, w_ref, da_ref, dg_ref, dp_ref, dw_ref):
        @pl.when((pl.program_id(0) == 0) & (pl.program_id(1) == 0))
        def _():
            dw_ref[...] = jnp.zeros_like(dw_ref)

        dmv = dm_ref[...]
        av = a_ref[...]
        graw = g_ref[...]
        gr = gr_ref[...]
        wv = w_ref[...]
        sa = _sigmoid(ga_ref[...])
        sb = _sigmoid(gb_ref[...])
        r = lax.rsqrt(jnp.mean(graw * graw, axis=-1, keepdims=True) + EPS)
        gnh = graw * r
        gn = gnh * wv
        sr = _sigmoid(gr)
        sl = gr * sr
        go = gn * sl
        da_ref[...] = dmv * sa
        dgo = dmv * sb
        dp_ref[:, 0:DV] = (dgo * gn * sr * (1.0 + gr * (1.0 - sr))).astype(BF16)
        dp_ref[:, DV:2 * DV] = (dmv * av * sa * (1.0 - sa)).astype(BF16)
        dp_ref[:, 2 * DV:3 * DV] = (dmv * go * sb * (1.0 - sb)).astype(BF16)
        dgn = dgo * sl
        dw_ref[...] += jnp.sum(dgn * gnh, axis=0, keepdims=True)
        t = dgn * wv
        dg_ref[...] = r * (t - gnh * jnp.mean(t * gnh, axis=-1, keepdims=True))

    def sec(k):
        return pl.BlockSpec((rb, DV), lambda i, h: (i, O_MERGE // DV + 3 * h + k))

    blk = pl.BlockSpec((rb, DV), lambda i, h: (i, h))
    vec = pl.BlockSpec((1, DV), lambda i, h: (0, 0))
    return pl.pallas_call(
        body, name="merge_bwd", grid=(T // rb, GLA_H),
        in_specs=[blk, blk, blk, sec(0), sec(1), sec(2), vec],
        out_specs=[blk, blk, pl.BlockSpec((rb, W_MERGE), lambda i, h: (i, O_MERGE // W_MERGE + h)), vec],
        out_shape=[jax.ShapeDtypeStruct((T, D), F32), jax.ShapeDtypeStruct((T, D), F32),
                   jax.ShapeDtypeStruct((T, MAIN), BF16), jax.ShapeDtypeStruct((1, DV), F32)],
        compiler_params=_params(2),
    )(dm, attn_o, gla_raw, proj, proj, proj, gla_norm_w)


def _attn_mask(n):
    qi = lax.broadcasted_iota(jnp.int32, (GROUP * WINDOW, 2 * WINDOW), 0) & (WINDOW - 1)
    ki = lax.broadcasted_iota(jnp.int32, (GROUP * WINDOW, 2 * WINDOW), 1)
    rel = qi + WINDOW - ki
    return (rel >= 0) & (rel < WINDOW) & ((n > 0) | (ki >= WINDOW))


def _stack_heads(ref, h):
    return jnp.concatenate(
        [ref[:, (h * GROUP + g) * HEAD_DIM:(h * GROUP + g + 1) * HEAD_DIM] for g in range(GROUP)], axis=0).astype(BF16)


def _stack_sinks(s_ref, h):
    return jnp.concatenate(
        [jnp.broadcast_to(s_ref[:, h * GROUP + g:h * GROUP + g + 1], (WINDOW, 1)) for g in range(GROUP)], axis=0)


def _attn_probs(q, kk, mask, sink):
    s = _dot(q, kk, 1, 1) * (HEAD_DIM ** -0.5)
    s = jnp.where(mask, s, MASK_VALUE)
    m = jnp.maximum(jnp.max(s, axis=-1, keepdims=True), sink)
    e = jnp.exp(s - m)
    es = jnp.exp(sink - m)
    inv = 1.0 / (jnp.sum(e, axis=-1, keepdims=True) + es)
    return e * inv, es * inv


def _attn_specs(nb, rev):
    def at(n):
        return (nb - 1 - n) if rev else n

    kcol, vcol = (O_ATTN + D) // 256, (O_ATTN + D) // 256 + 1
    q = pl.BlockSpec((WINDOW, D), lambda n: (at(n), O_ATTN // D))
    kc = pl.BlockSpec((WINDOW, 256), lambda n: (at(n), kcol))
    kp = pl.BlockSpec((WINDOW, 256), lambda n: (jnp.maximum(at(n) - 1, 0), kcol))
    vc = pl.BlockSpec((WINDOW, 256), lambda n: (at(n), vcol))
    vp = pl.BlockSpec((WINDOW, 256), lambda n: (jnp.maximum(at(n) - 1, 0), vcol))
    sk = pl.BlockSpec((1, 128), lambda n: (0, 0))
    o = pl.BlockSpec((WINDOW, D), lambda n: (at(n), 0))
    return q, kc, kp, vc, vp, sk, o


def _attn_fwd(proj, sinks):
    T = proj.shape[0]
    nb = T // WINDOW

    def body(q_ref, kc_ref, kp_ref, vc_ref, vp_ref, s_ref, o_ref):
        mask = _attn_mask(pl.program_id(0))
        for h in range(N_KV):
            hs = slice(h * HEAD_DIM, (h + 1) * HEAD_DIM)
            kk = jnp.concatenate([kp_ref[:, hs], kc_ref[:, hs]], axis=0).astype(BF16)
            vv = jnp.concatenate([vp_ref[:, hs], vc_ref[:, hs]], axis=0).astype(BF16)
            p, _ = _attn_probs(_stack_heads(q_ref, h), kk, mask, _stack_sinks(s_ref, h))
            o = _dot(p.astype(BF16), vv, 1, 0)
            for g in range(GROUP):
                hg = h * GROUP + g
                o_ref[:, hg * HEAD_DIM:(hg + 1) * HEAD_DIM] = o[g * WINDOW:(g + 1) * WINDOW]

    q, kc, kp, vc, vp, sk, o = _attn_specs(nb, False)
    return pl.pallas_call(
        body, name="attn_fwd", grid=(nb,), in_specs=[q, kc, kp, vc, vp, sk], out_specs=o,
        out_shape=jax.ShapeDtypeStruct((T, D), F32), compiler_params=_params(1),
    )(proj, proj, proj, proj, proj, sinks)


def _attn_bwd(proj, sinks, d_o, dproj):
    T = proj.shape[0]
    nb = T // WINDOW
    kat, vat = D, D + N_KV * HEAD_DIM

    def body(q_ref, kc_ref, kp_ref, vc_ref, vp_ref, s_ref, do_ref, _, dp_ref, ds_ref, ck_ref, cv_ref):
        i = pl.program_id(0)
        n = nb - 1 - i

        @pl.when(i == 0)
        def _():
            ck_ref[...] = jnp.zeros_like(ck_ref)
            cv_ref[...] = jnp.zeros_like(cv_ref)
            ds_ref[...] = jnp.zeros_like(ds_ref)

        mask = _attn_mask(n)
        for h in range(N_KV):
            hs = slice(h * HEAD_DIM, (h + 1) * HEAD_DIM)
            kk = jnp.concatenate([kp_ref[:, hs], kc_ref[:, hs]], axis=0).astype(BF16)
            vv = jnp.concatenate([vp_ref[:, hs], vc_ref[:, hs]], axis=0).astype(BF16)
            q = _stack_heads(q_ref, h)
            do = _stack_heads(do_ref, h)
            p, ps = _attn_probs(q, kk, mask, _stack_sinks(s_ref, h))
            dp = _dot(do, vv, 1, 1)
            delta = jnp.sum(p * dp, axis=-1, keepdims=True)
            dsc = (p * (dp - delta) * (HEAD_DIM ** -0.5)).astype(BF16)
            dsink = -ps * delta
            dq = _dot(dsc, kk, 1, 0).astype(BF16)
            for g in range(GROUP):
                hg = h * GROUP + g
                rows = slice(g * WINDOW, (g + 1) * WINDOW)
                ds_ref[hg:hg + 1, :] += jnp.broadcast_to(jnp.sum(dsink[rows], axis=0, keepdims=True), (1, 128))
                dp_ref[:, hg * HEAD_DIM:(hg + 1) * HEAD_DIM] = dq[rows]
            dkk = _dot(dsc, q, 0, 0)
            dvv = _dot(p.astype(BF16), do, 0, 0)
            dp_ref[:, kat + h * HEAD_DIM:kat + (h + 1) * HEAD_DIM] = (dkk[WINDOW:] + ck_ref[:, hs]).astype(BF16)
            dp_ref[:, vat + h * HEAD_DIM:vat + (h + 1) * HEAD_DIM] = (dvv[WINDOW:] + cv_ref[:, hs]).astype(BF16)
            ck_ref[:, hs] = dkk[:WINDOW]
            cv_ref[:, hs] = dvv[:WINDOW]

    q, kc, kp, vc, vp, sk, o = _attn_specs(nb, True)
    return pl.pallas_call(
        body, name="attn_bwd", grid=(nb,), in_specs=[q, kc, kp, vc, vp, sk, o, ANY],
        out_specs=[pl.BlockSpec((WINDOW, W_ATTN), lambda n: (nb - 1 - n, O_ATTN // W_ATTN)),
                   pl.BlockSpec((N_Q, 128), lambda n: (0, 0))],
        out_shape=[jax.ShapeDtypeStruct((T, MAIN), BF16), jax.ShapeDtypeStruct((N_Q, 128), F32)],
        scratch_shapes=[pltpu.VMEM((WINDOW, 256), F32), pltpu.VMEM((WINDOW, 256), F32)],
        input_output_aliases={7: 0}, compiler_params=_params(1),
    )(proj, proj, proj, proj, proj, sinks, d_o, dproj)


def _split3(x):
    hi = x.astype(BF16)
    r1 = x - hi.astype(F32)
    mid = r1.astype(BF16)
    lo = (r1 - mid.astype(F32)).astype(BF16)
    return hi, mid, lo


def _tri_sum(tri, x):
    hi, mid, lo = _split3(x)
    return _dot(tri, hi, 1, 0) + _dot(tri, mid, 1, 0) + _dot(tri, lo, 1, 0)


def _gla_chunk(q, k, lr, w2, b, lower):
    logit = _dot(lr.astype(BF16), w2.astype(BF16), 1, 0) + b
    la = (jnp.minimum(logit, 0.0) - jnp.log(1.0 + jnp.exp(-jnp.abs(logit)))) * (1.0 / GATE_NORM)
    g = _tri_sum(lower, la)
    gl = g[CHUNK - 1:CHUNK, :]
    eg = jnp.exp(g)
    qd = q * (DK ** -0.5) * eg
    ki = k * jnp.exp(-g)
    ke = k * jnp.exp(gl - g)
    return logit, g, gl, eg, qd, ki, ke


def _tri(lower):
    r = lax.broadcasted_iota(jnp.int32, (CHUNK, CHUNK), 0)
    c = lax.broadcasted_iota(jnp.int32, (CHUNK, CHUNK), 1)
    return (r >= c) if lower else (r <= c)


def _gla_rows(T):
    return _pick(T, (256, 128, 64))


def _gla_fwd(proj, proj_lr, w2p, gate_b):
    T = proj.shape[0]
    rb = _gla_rows(T)
    per = rb // CHUNK

    def body(q_ref, k_ref, v_ref, lr_ref, w2_ref, b_ref, o_ref, st_ref, s_scr):
        @pl.when(pl.program_id(1) == 0)
        def _():
            s_scr[...] = jnp.zeros_like(s_scr)

        low = _tri(True)
        lower = low.astype(BF16)
        for i in range(per):
            rows = slice(i * CHUNK, (i + 1) * CHUNK)
            _, g, gl, eg, qd, ki, ke = _gla_chunk(q_ref[rows, :], k_ref[rows, :], lr_ref[rows, :], w2_ref[...],
                                                  b_ref[...], lower)
            v = v_ref[rows, :].astype(BF16)
            qdb = qd.astype(BF16)
            att = jnp.where(low, _dot(qdb, ki.astype(BF16), 1, 1), 0.0)
            st = s_scr[...]
            st_ref[0, i] = st
            o_ref[rows, :] = _dot(att.astype(BF16), v, 1, 0) + _dot(qdb, st.astype(BF16), 1, 1)
            s_scr[...] = st * jnp.exp(gl) + _dot(v, ke.astype(BF16), 0, 0)

    return pl.pallas_call(
        body, name="gla_fwd", grid=(GLA_H, T // rb),
        in_specs=[pl.BlockSpec((rb, DK), lambda h, n: (n, (O_GLA + W_GLA * h) // DK)),
                  pl.BlockSpec((rb, DK), lambda h, n: (n, (O_GLA + W_GLA * h) // DK + 1)),
                  pl.BlockSpec((rb, DV), lambda h, n: (n, (O_GLA + W_GLA * h) // DV + 1)),
                  pl.BlockSpec((rb, LRP), lambda h, n: (n, 0)),
                  pl.BlockSpec((LRP, DK), lambda h, n: (0, h)),
                  pl.BlockSpec((1, DK), lambda h, n: (0, h))],
        out_specs=[pl.BlockSpec((rb, DV), lambda h, n: (n, h)),
                   pl.BlockSpec((1, per, DV, DK), lambda h, n: (h, n, 0, 0))],
        out_shape=[jax.ShapeDtypeStruct((T, GLA_H * DV), F32),
                   jax.ShapeDtypeStruct((GLA_H, T // CHUNK, DV, DK), F32)],
        scratch_shapes=[pltpu.VMEM((DV, DK), F32)], compiler_params=_params(2),
    )(proj, proj, proj, proj_lr, w2p, gate_b)


def _gla_bwd(proj, proj_lr, w2p, gate_b, states, d_o, dproj):
    T = proj.shape[0]
    rb = _gla_rows(T)
    per = rb // CHUNK
    nblk = T // rb

    def body(q_ref, k_ref, v_ref, lr_ref, w2_ref, b_ref, st_ref, do_ref, _, dp_ref, dl_ref, ds_scr):
        @pl.when(pl.program_id(1) == 0)
        def _():
            ds_scr[...] = jnp.zeros_like(ds_scr)

        low = _tri(True)
        lower = low.astype(BF16)
        upper = _tri(False).astype(BF16)
        for i in reversed(range(per)):
            rows = slice(i * CHUNK, (i + 1) * CHUNK)
            logit, g, gl, eg, qd, ki, ke = _gla_chunk(q_ref[rows, :], k_ref[rows, :], lr_ref[rows, :], w2_ref[...],
                                                      b_ref[...], lower)
            v = v_ref[rows, :].astype(BF16)
            do = do_ref[rows, :].astype(BF16)
            qdb, kib, keb = qd.astype(BF16), ki.astype(BF16), ke.astype(BF16)
            att = jnp.where(low, _dot(qdb, kib, 1, 1), 0.0).astype(BF16)
            sp = st_ref[0, i]
            ds = ds_scr[...]
            dsb = ds.astype(BF16)
            datt = jnp.where(low, _dot(do, v, 1, 1), 0.0).astype(BF16)
            dp_ref[rows, 2 * DK:] = (_dot(att, do, 0, 0) + _dot(keb, dsb, 1, 1)).astype(BF16)
            dqd = _dot(datt, kib, 1, 0) + _dot(do, sp.astype(BF16), 1, 0)
            dki = _dot(datt, qdb, 0, 0)
            dke = _dot(v, dsb, 1, 0)
            decay = jnp.exp(gl)
            ds_scr[...] = ds * decay + _dot(do, qdb, 0, 0)
            ddec = jnp.sum(ds * sp, axis=0, keepdims=True)
            dp_ref[rows, 0:DK] = (dqd * (DK ** -0.5) * eg).astype(BF16)
            dp_ref[rows, DK:2 * DK] = (dki * jnp.exp(-g) + dke * jnp.exp(gl - g)).astype(BF16)
            dke_ke = dke * ke
            dg = dqd * qd - dki * ki - dke_ke
            dgl = jnp.sum(dke_ke, axis=0, keepdims=True) + ddec * decay
            dla = _tri_sum(upper, dg) + dgl
            dl_ref[rows, :] = dla * (1.0 / GATE_NORM) * (1.0 - _sigmoid(logit))

    def rev(n):
        return nblk - 1 - n

    return pl.pallas_call(
        body, name="gla_bwd", grid=(GLA_H, nblk),
        in_specs=[pl.BlockSpec((rb, DK), lambda h, n: (rev(n), (O_GLA + W_GLA * h) // DK)),
                  pl.BlockSpec((rb, DK), lambda h, n: (rev(n), (O_GLA + W_GLA * h) // DK + 1)),
                  pl.BlockSpec((rb, DV), lambda h, n: (rev(n), (O_GLA + W_GLA * h) // DV + 1)),
                  pl.BlockSpec((rb, LRP), lambda h, n: (rev(n), 0)),
                  pl.BlockSpec((LRP, DK), lambda h, n: (0, h)),
                  pl.BlockSpec((1, DK), lambda h, n: (0, h)),
                  pl.BlockSpec((1, per, DV, DK), lambda h, n: (h, rev(n), 0, 0)),
                  pl.BlockSpec((rb, DV), lambda h, n: (rev(n), h)),
                  ANY],
        out_specs=[pl.BlockSpec((rb, W_GLA), lambda h, n: (rev(n), O_GLA // W_GLA + h)),
                   pl.BlockSpec((rb, DK), lambda h, n: (rev(n), h))],
        out_shape=[jax.ShapeDtypeStruct((T, MAIN), BF16), jax.ShapeDtypeStruct((T, GLA_H * DK), F32)],
        scratch_shapes=[pltpu.VMEM((DV, DK), F32)], input_output_aliases={8: 0}, compiler_params=_params(2),
    )(proj, proj, proj, proj_lr, w2p, gate_b, states, d_o, dproj)


def _local_step(x, target, p, reduce_part):
    u, u_t = _rms_fwd(x, p["norm1_w"], name="rms1_fwd")
    proj = _matmul(u, p["w_in_main"], name="mm_proj")
    proj_lr = _matmul(u, p["w_in_lr"], name="mm_proj_lr")
    attn_o = _attn_fwd(proj, p["sinks"])
    gla_raw, states = _gla_fwd(proj, proj_lr, p["w2p"], p["gate_b"])
    merged, merged_t = _merge_fwd(attn_o, gla_raw, proj, p["gla_norm_w"])
    h1 = _matmul(merged, p["w_out"], add=x, name="mm_out")
    v2, v2_t = _rms_fwd(h1, p["norm2_w"], name="rms2_fwd")
    gate, up, ff, ff_t = _ffn_fwd(v2, p["w_gate"], p["w_up"])
    h2 = _matmul(ff, p["w_down"], add=h1, name="mm_down")
    loss, dh2, dh2_b, g_final = _final_loss(h2, p["final_norm_w"], target)

    dgate, dup = _ffn_bwd_hidden(dh2_b, p["w_down"], gate, up)
    reduce_part("w_down", _matmul(ff_t, dh2_b, out_dtype=BF16, name="mm_gdown").reshape(4, FFN // 4, D))
    reduce_part("w_gate", _matmul(v2_t, dgate, out_dtype=BF16, bj=PANEL, out_slots=True, name="mm_ggate"))
    reduce_part("w_up", _matmul(v2_t, dup, out_dtype=BF16, bj=PANEL, out_slots=True, name="mm_gup"))
    dv2 = _ffn_bwd_input(dgate, dup, p["w_gate"], p["w_up"])
    dh1, dh1_b, g_norm2 = _rms_bwd(dv2, h1, p["norm2_w"], dh2, name="rms2_bwd")
    dmerged = _matmul(dh1_b, p["w_out"], tb=True, name="mm_dmerged")
    reduce_part("w_out", _matmul(merged_t, dh1_b, out_dtype=BF16, name="mm_gout").reshape(4, D // 4, D))
    d_attn, d_gla, dproj, g_gla_norm = _merge_bwd(dmerged, attn_o, gla_raw, proj, p["gla_norm_w"])
    dproj, dlogit = _gla_bwd(proj, proj_lr, p["w2p"], p["gate_b"], states, d_gla, dproj)
    dproj, g_sinks = _attn_bwd(proj, p["sinks"], d_attn, dproj)
    g_gate_b = _colsum(dlogit, name="colsum_gate_b")
    g_w2 = _matmul(proj_lr, dlogit, ta=True, name="mm_gw2")
    dproj_lr = _matmul(dlogit, p["w2p"], tb=True, out_dtype=BF16, name="mm_dlr")
    g_in_main = _matmul(u_t, dproj, out_dtype=BF16, name="mm_gin")
    g_in_lr = _matmul(u_t, dproj_lr, out_dtype=BF16, name="mm_gin_lr")
    du = _matmul(dproj, p["w_in_main"], tb=True, name="mm_du")
    du = _matmul(dproj_lr, p["w_in_lr"], tb=True, add=du, name="mm_du_lr")
    dx, _, g_norm1 = _rms_bwd(du, x, p["norm1_w"], dh1, name="rms1_bwd")
    grads = dict(norm1_w=g_norm1, w_in_main=g_in_main, w_in_lr=g_in_lr, w2=g_w2[:RANK], gate_b=g_gate_b,
                 sinks=g_sinks[:, 0].reshape(1, N_Q), gla_norm_w=g_gla_norm, norm2_w=g_norm2, final_norm_w=g_final)
    return loss, dx, grads


def _place():
    x, y, c = lax.axis_index("x"), lax.axis_index("y"), lax.axis_index("c")
    chips = [(1 - x, y), (x, 1 - y), (1 - x, 1 - y)]
    return x, y, c, chips


def _hbm_shape(s, dt):
    return jax.ShapeDtypeStruct(s, dt)


def _handshake(peers):
    barrier = pltpu.get_barrier_semaphore()
    for peer in peers:
        pl.semaphore_signal(barrier, inc=1, device_id=peer, device_id_type=MESH)
    pl.semaphore_wait(barrier, len(peers))


def _launch_copies(body, args, out_shape, sems, *, name, collective_id=None):
    if collective_id is None:
        return pl.pallas_call(
            body, name=name, in_specs=[ANY] * len(args), out_specs=[ANY] * len(out_shape), out_shape=out_shape,
            scratch_shapes=sems)(*args)
    return pl.kernel(
        body, name=name, out_type=out_shape, mesh=plsc.ScalarSubcoreMesh(axis_name="sequencer", num_cores=1),
        scratch_types=sems, compiler_params=pltpu.CompilerParams(collective_id=collective_id))(*args)


def _gather_shards(shards, *, name, collective_id=None, after=()):
    n = len(shards)
    first_out = n + len(after)

    def body(*refs):
        ins, outs = refs[:n], refs[first_out:first_out + n]
        ici_send, ici_recv, d2d_send, d2d_recv, local_sem = refs[first_out + n:]
        x, y, c, chips = _place()
        me = 2 * x + y
        sibling = (x, y, 1 - c)
        if collective_id is not None:
            _handshake([sibling] + [(*chip, c) for chip in chips])

        def half(w, slot, hc):
            r2 = shards[w].shape[0] // 2
            return outs[w].at[slot, pl.ds(hc * r2, r2), :]

        locals_ = [pltpu.make_async_copy(ins[w], outs[w].at[me], local_sem.at[w]) for w in range(n)]
        for cp in locals_:
            cp.start()
        sends = []
        for w in range(n):
            r2 = shards[w].shape[0] // 2
            for j, chip in enumerate(chips):
                cp = pltpu.make_async_remote_copy(
                    src_ref=ins[w].at[pl.ds(c * r2, r2), :], dst_ref=half(w, me, c),
                    send_sem=ici_send.at[w * 3 + j], recv_sem=ici_recv.at[w * 3 + j],
                    device_id=(*chip, c), device_id_type=MESH)
                cp.start()
                sends.append(cp)
        for w in range(n):
            for j, chip in enumerate(chips):
                slot = 2 * chip[0] + chip[1]
                got = half(w, slot, c)
                pltpu.make_async_remote_copy(
                    src_ref=got, dst_ref=got, send_sem=ici_send.at[w * 3 + j], recv_sem=ici_recv.at[w * 3 + j],
                    device_id=(*chip, c), device_id_type=MESH).wait_recv()
                cp = pltpu.make_async_remote_copy(
                    src_ref=got, dst_ref=got, send_sem=d2d_send.at[w * 3 + j], recv_sem=d2d_recv.at[w * 3 + j],
                    device_id=sibling, device_id_type=MESH)
                cp.start()
                sends.append(cp)
        for w in range(n):
            for j, chip in enumerate(chips):
                slot = 2 * chip[0] + chip[1]
                got = half(w, slot, 1 - c)
                pltpu.make_async_remote_copy(
                    src_ref=got, dst_ref=got, send_sem=d2d_send.at[w * 3 + j], recv_sem=d2d_recv.at[w * 3 + j],
                    device_id=sibling, device_id_type=MESH).wait_recv()
        for cp in sends:
            cp.wait_send()
        for cp in locals_:
            cp.wait()

    return _launch_copies(
        body, list(shards) + list(after), [_hbm_shape((4,) + s.shape, s.dtype) for s in shards],
        [pltpu.SemaphoreType.DMA((3 * n,)), pltpu.SemaphoreType.DMA((3 * n,)), pltpu.SemaphoreType.DMA((3 * n,)),
         pltpu.SemaphoreType.DMA((3 * n,)), pltpu.SemaphoreType.DMA((n,))],
        name=name, collective_id=collective_id)


def _gather_staged(w, small):
    R, C = w.shape
    R2 = R // 2
    rb = 256
    per = R2 // rb
    r2 = small.shape[0] // 2

    def body(w_ref, s_ref, out_ref, outs_ref, stage, ici_send, ici_recv, fwd_send, fwd_recv, in_sem, put_sem,
             push_send, push_recv):
        x, y, c, chips = _place()
        me = 2 * x + y
        sibling = (x, y, 1 - c)
        slots = [2 * chip[0] + chip[1] for chip in chips]

        def ici(j, src, dst, k):
            return pltpu.make_async_remote_copy(
                src_ref=src, dst_ref=dst, send_sem=ici_send.at[k], recv_sem=ici_recv.at[k],
                device_id=(*chips[j], c), device_id_type=MESH)

        sends = []
        for j in range(3):
            sends.append(ici(j, w_ref.at[pl.ds(c * R2, R2), :], out_ref.at[me, pl.ds(c * R2, R2), :], j))
            sends.append(ici(j, s_ref.at[pl.ds(c * r2, r2), :], outs_ref.at[me, pl.ds(c * r2, r2), :], 3 + j))
        for cp in sends:
            cp.start()
        mine = pltpu.make_async_copy(s_ref, outs_ref.at[me], put_sem.at[2])
        mine.start()

        def own(k, to_vmem):
            rows = pl.ds(k * rb, rb)
            if to_vmem:
                return pltpu.make_async_copy(w_ref.at[rows, :], stage.at[k % 2], in_sem.at[k % 2])
            return pltpu.make_async_copy(stage.at[k % 2], out_ref.at[me, rows, :], put_sem.at[k % 2])

        for k in range(R // rb):
            if k >= 2:
                own(k - 2, False).wait()
            own(k, True).start()
            own(k, True).wait()
            own(k, False).start()
        for k in range(R // rb - 2, R // rb):
            own(k, False).wait()

        def block(j, k, hc):
            return out_ref.at[slots[j], pl.ds(hc * R2 + k * rb, rb), :]

        def push(t, hc):
            j, k = divmod(t, per)
            return pltpu.make_async_remote_copy(
                src_ref=stage.at[t % 2], dst_ref=block(j, k, hc), send_sem=push_send.at[t % 2],
                recv_sem=push_recv.at[t], device_id=sibling, device_id_type=MESH)

        passed = []
        for j in range(3):
            ici(j, w_ref.at[pl.ds(c * R2, R2), :], out_ref.at[slots[j], pl.ds(c * R2, R2), :], j).wait_recv()
            for k in range(per):
                t = j * per + k
                if t >= 2:
                    push(t - 2, c).wait_send()
                fetch = pltpu.make_async_copy(block(j, k, c), stage.at[t % 2], in_sem.at[t % 2])
                fetch.start()
                fetch.wait()
                push(t, c).start()
            got = outs_ref.at[slots[j], pl.ds(c * r2, r2), :]
            ici(j, got, got, 3 + j).wait_recv()
            fw = pltpu.make_async_remote_copy(
                src_ref=got, dst_ref=got, send_sem=fwd_send.at[j], recv_sem=fwd_recv.at[j],
                device_id=sibling, device_id_type=MESH)
            fw.start()
            passed.append(fw)
        for t in range(3 * per - 2, 3 * per):
            push(t, c).wait_send()
        for t in range(3 * per):
            push(t, 1 - c).wait_recv()
        for j in range(3):
            theirs = outs_ref.at[slots[j], pl.ds((1 - c) * r2, r2), :]
            pltpu.make_async_remote_copy(
                src_ref=theirs, dst_ref=theirs, send_sem=fwd_send.at[j], recv_sem=fwd_recv.at[j],
                device_id=sibling, device_id_type=MESH).wait_recv()
        for cp in sends + passed:
            cp.wait_send()
        mine.wait()

    dma = pltpu.SemaphoreType.DMA
    return pl.pallas_call(
        body, name="gather_in", in_specs=[ANY, ANY], out_specs=[ANY, ANY],
        out_shape=[_hbm_shape((4, R, C), w.dtype), _hbm_shape((4,) + small.shape, small.dtype)],
        scratch_shapes=[pltpu.VMEM((2, rb, C), w.dtype), dma((6,)), dma((6,)), dma((3,)), dma((3,)), dma((2,)),
                        dma((3,)), dma((2,)), dma((3 * per,))],
        compiler_params=pltpu.CompilerParams(vmem_limit_bytes=VMEM_LIMIT),
    )(w, small)


def _pair_blocks(R2):
    return _pick(R2, (256, 352, 128, 64, 32, 16))


def _pair_reduce(part, *, name):
    _, R, C = part.shape
    R2 = R // 2
    rb = _pair_blocks(R2)
    nblk = R2 // rb
    steps = [(s, i) for s in range(4) for i in range(nblk)]
    n = len(steps)

    def body(part_ref, sums_ref, own_buf, oth_buf, rcv_buf, out_buf, own_sem, oth_sem, out_sem, send_sem, recv_sem):
        x, y, c, _ = _place()
        sibling = (x, y, 1 - c)

        def fetch(t, half, buf, sem):
            s, i = steps[t]
            return pltpu.make_async_copy(part_ref.at[s, pl.ds(half * R2 + i * rb, rb), :], buf.at[t % 2], sem.at[t % 2])

        def push(t):
            return pltpu.make_async_remote_copy(
                src_ref=oth_buf.at[t % 2], dst_ref=rcv_buf.at[t % 2], send_sem=send_sem.at[t % 2],
                recv_sem=recv_sem.at[t % 2], device_id=sibling, device_id_type=MESH)

        def store(t):
            s, i = steps[t]
            return pltpu.make_async_copy(out_buf.at[t % 2], sums_ref.at[s, pl.ds(i * rb, rb), :], out_sem.at[t % 2])

        fetch(0, c, own_buf, own_sem).start()
        fetch(0, 1 - c, oth_buf, oth_sem).start()
        for t in range(n):
            if t + 1 < n:
                fetch(t + 1, c, own_buf, own_sem).start()
                fetch(t + 1, 1 - c, oth_buf, oth_sem).start()
            fetch(t, 1 - c, oth_buf, oth_sem).wait()
            p = push(t)
            p.start()
            fetch(t, c, own_buf, own_sem).wait()
            p.wait()
            if t >= 2:
                store(t - 2).wait()
            out_buf[t % 2] = (own_buf[t % 2].astype(F32) + rcv_buf[t % 2].astype(F32)).astype(BF16)
            store(t).start()
        for t in range(max(n - 2, 0), n):
            store(t).wait()

    buf = pltpu.VMEM((2, rb, C), BF16)
    sem2 = pltpu.SemaphoreType.DMA((2,))
    return pl.pallas_call(
        body, name=name, in_specs=[ANY], out_specs=ANY, out_shape=_hbm_shape((4, R2, C), BF16),
        scratch_shapes=[buf, buf, buf, buf, sem2, sem2, sem2, sem2, sem2],
        compiler_params=pltpu.CompilerParams(vmem_limit_bytes=VMEM_LIMIT),
    )(part)


def _chip_exchange(sums, *, name, collective_id=None):
    n = len(sums)

    def body(*refs):
        ins, outs = refs[:n], refs[n:2 * n]
        send_sem, recv_sem, local_sem = refs[2 * n:]
        x, y, c, chips = _place()
        me = 2 * x + y
        if collective_id is not None:
            _handshake([(*chip, c) for chip in chips])
        cps = []
        for w in range(n):
            lc = pltpu.make_async_copy(ins[w].at[me], outs[w].at[me], local_sem.at[w])
            lc.start()
            cps.append(lc)
            for j, chip in enumerate(chips):
                slot = 2 * chip[0] + chip[1]
                rc = pltpu.make_async_remote_copy(
                    src_ref=ins[w].at[slot], dst_ref=outs[w].at[me],
                    send_sem=send_sem.at[w * 3 + j], recv_sem=recv_sem.at[w * 3 + j],
                    device_id=(*chip, c), device_id_type=MESH)
                rc.start()
                cps.append(rc)
        for w in range(n):
            for j, chip in enumerate(chips):
                slot = 2 * chip[0] + chip[1]
                pltpu.make_async_remote_copy(
                    src_ref=ins[w].at[slot], dst_ref=outs[w].at[slot],
                    send_sem=send_sem.at[w * 3 + j], recv_sem=recv_sem.at[w * 3 + j],
                    device_id=(*chip, c), device_id_type=MESH).wait_recv()
        for w in range(n):
            cps[w * 4].wait()
            for j in range(3):
                cps[w * 4 + 1 + j].wait_send()

    return _launch_copies(
        body, sums, [_hbm_shape(s.shape, s.dtype) for s in sums],
        [pltpu.SemaphoreType.DMA((3 * n,)), pltpu.SemaphoreType.DMA((3 * n,)), pltpu.SemaphoreType.DMA((n,))],
        name=name, collective_id=collective_id)


def _sum_join(recv, *, name):
    _, R2, C = recv.shape
    rb = _pair_blocks(R2)
    nblk = R2 // rb

    def body(recv_ref, out_ref, in_buf, acc_buf, in_sem, loc_sem, send_sem, recv_sem):
        x, y, c, _ = _place()
        sibling = (x, y, 1 - c)

        def fetch(t):
            return pltpu.make_async_copy(recv_ref.at[:, pl.ds(t * rb, rb), :], in_buf.at[t % 2], in_sem.at[t % 2])

        def rows(t, half):
            return out_ref.at[pl.ds(half * R2 + t * rb, rb), :]

        def put_local(t):
            return pltpu.make_async_copy(acc_buf.at[t % 2], rows(t, c), loc_sem.at[t])

        def put_remote(t, half):
            return pltpu.make_async_remote_copy(
                src_ref=acc_buf.at[t % 2], dst_ref=rows(t, half), send_sem=send_sem.at[t], recv_sem=recv_sem.at[t],
                device_id=sibling, device_id_type=MESH)

        fetch(0).start()
        for t in range(nblk):
            if t + 1 < nblk:
                fetch(t + 1).start()
            fetch(t).wait()
            if t >= 2:
                put_local(t - 2).wait()
                put_remote(t - 2, c).wait_send()
            acc = in_buf[t % 2, 0].astype(F32)
            for s in range(1, 4):
                acc = acc + in_buf[t % 2, s].astype(F32)
            acc_buf[t % 2] = acc
            put_local(t).start()
            put_remote(t, c).start()
        for t in range(max(nblk - 2, 0), nblk):
            put_local(t).wait()
            put_remote(t, c).wait_send()
        for t in range(nblk):
            put_remote(t, 1 - c).wait_recv()

    semn = pltpu.SemaphoreType.DMA((nblk,))
    return pl.pallas_call(
        body, name=name, in_specs=[ANY], out_specs=ANY, out_shape=_hbm_shape((2 * R2, C), F32),
        scratch_shapes=[pltpu.VMEM((2, 4, rb, C), BF16), pltpu.VMEM((2, rb, C), F32),
                        pltpu.SemaphoreType.DMA((2,)), semn, semn, semn],
        compiler_params=pltpu.CompilerParams(vmem_limit_bytes=VMEM_LIMIT),
    )(recv)


def _sum_small(pack):
    R, C = pack.shape

    def body(in_ref, out_ref, all_ref, send_sem, recv_sem):
        x, y, c, _ = _place()
        me = 4 * x + 2 * y + c
        all_ref[me] = in_ref[...]
        cps = []
        for k in range(1, 8):
            peer = (x ^ (k >> 2), y ^ ((k >> 1) & 1), c ^ (k & 1))
            cp = pltpu.make_async_remote_copy(
                src_ref=in_ref, dst_ref=all_ref.at[me], send_sem=send_sem.at[k - 1], recv_sem=recv_sem.at[k - 1],
                device_id=peer, device_id_type=MESH)
            cp.start()
            cps.append(cp)
        for k in range(1, 8):
            peer = (x ^ (k >> 2), y ^ ((k >> 1) & 1), c ^ (k & 1))
            slot = 4 * peer[0] + 2 * peer[1] + peer[2]
            pltpu.make_async_remote_copy(
                src_ref=in_ref, dst_ref=all_ref.at[slot], send_sem=send_sem.at[k - 1], recv_sem=recv_sem.at[k - 1],
                device_id=peer, device_id_type=MESH).wait_recv()
        for cp in cps:
            cp.wait_send()
        acc = all_ref[0]
        for d in range(1, 8):
            acc = acc + all_ref[d]
        out_ref[...] = acc

    return pl.pallas_call(
        body, name="sum_small", in_specs=[pl.BlockSpec(memory_space=pltpu.VMEM)],
        out_specs=pl.BlockSpec(memory_space=pltpu.VMEM), out_shape=jax.ShapeDtypeStruct((R, C), F32),
        scratch_shapes=[pltpu.VMEM((8, R, C), F32), pltpu.SemaphoreType.DMA((7,)), pltpu.SemaphoreType.DMA((7,))],
    )(pack)


ADAMW_BLOCK_BYTES = 2 * 1024 * 1024


def _adamw_block(R, C):
    padded = -(-C // 128) * 128
    rows = [rb for rb in range(8, R + 1, 8) if R % rb == 0 and rb * padded * 4 <= ADAMW_BLOCK_BYTES]
    if rows or R * padded * 4 <= ADAMW_BLOCK_BYTES:
        return (max(rows) if rows else R), C
    cols = [cb for cb in range(128, C + 1, 128) if C % cb == 0 and R * cb * 4 <= ADAMW_BLOCK_BYTES]
    return R, max(cols)


def _adamw(w, g, m, v, *, name):
    R, C = w.shape
    rb, cb = _adamw_block(R, C)
    c1 = 1.0 / (1.0 - B1 ** STEP)
    c2 = 1.0 / (1.0 - B2 ** STEP)

    def body(w_ref, g_ref, m_ref, v_ref, d_ref, nm_ref, nv_ref):
        gv = g_ref[...]
        m2 = B1 * m_ref[...] + (1.0 - B1) * gv
        v2 = B2 * v_ref[...] + (1.0 - B2) * gv * gv
        nm_ref[...] = m2
        nv_ref[...] = v2
        d_ref[...] = -LR * ((m2 * c1) / (jnp.sqrt(v2 * c2) + ADAM_EPS) + WD * w_ref[...])

    blk = pl.BlockSpec((rb, cb), lambda i, j: (i, j))
    return pl.pallas_call(
        body, name=name, grid=(R // rb, C // cb), in_specs=[blk] * 4, out_specs=[blk] * 3,
        out_shape=[jax.ShapeDtypeStruct((R, C), F32)] * 3, compiler_params=_params(2),
    )(w, g, m, v)


SMALL = (("norm1_w", D), ("norm2_w", D), ("final_norm_w", D), ("gate_b", GLA_H * DK), ("gla_norm_w", DV), ("sinks", N_Q))
PACK_W = 1024


def _pack_small(vals, w2, loss):
    rows = []
    for name, width in SMALL:
        v = vals[name].reshape(-1)
        rows.append(jnp.pad(v, (0, (-width) % PACK_W)).reshape(-1, PACK_W))
    rows.append(w2)
    rows.append(jnp.broadcast_to(loss.reshape(1, 1), (1, PACK_W)))
    pack = jnp.concatenate(rows, axis=0)
    return jnp.pad(pack, ((0, 32 - pack.shape[0]), (0, 0)))


def _unpack_small(pack):
    out, r = {}, 0
    for name, width in SMALL:
        nr = -(-width // PACK_W)
        out[name] = pack[r:r + nr].reshape(-1)[:width]
        r += nr
    out["w2"] = pack[r:r + RANK]
    out["loss"] = pack[r + RANK, 0]
    return out


def kernel(x, norm1_w, w_in, gla_gate_w2, gla_gate_b, attn_sinks, gla_norm_w, w_out, norm2_w, w_ffn_gate, w_ffn_up, w_ffn_down, final_norm_w, loss_target, m_norm1_w, m_w_in, m_gla_gate_w2, m_gla_gate_b, m_attn_sinks, m_gla_norm_w, m_w_out, m_norm2_w, m_w_ffn_gate, m_w_ffn_up, m_w_ffn_down, m_final_norm_w, v_norm1_w, v_w_in, v_gla_gate_w2, v_gla_gate_b, v_attn_sinks, v_gla_norm_w, v_w_out, v_norm2_w, v_w_ffn_gate, v_w_ffn_up, v_w_ffn_down, v_final_norm_w):
    chip = 2 * lax.axis_index("x") + lax.axis_index("y")
    w_in_s, w_out_s, w_gate_s, w_up_s, w_down_s, w2_s = (
        w_in[0], w_out[0], w_ffn_gate[0], w_ffn_up[0], w_ffn_down[0], gla_gate_w2[0])
    CS = SHARD_COLS

    g_in, g_w2 = _gather_staged(w_in_s.astype(BF16), w2_s)
    g_out, = _gather_shards([w_out_s.astype(BF16)], name="gather_out", collective_id=1, after=[g_w2])
    g_gate, g_up = _gather_shards([w_gate_s.astype(BF16), w_up_s.astype(BF16)], name="gather_gate_up",
                                  collective_id=7, after=[g_w2])
    g_down, = _gather_shards([w_down_s.astype(BF16)], name="gather_down", collective_id=8, after=[g_w2])
    w_main, w_lr = _w_in_to_main(g_in)
    w2_full = jnp.transpose(g_w2, (1, 0, 2)).reshape(RANK, GLA_H * DK)
    p = dict(
        norm1_w=norm1_w, norm2_w=norm2_w, final_norm_w=final_norm_w.reshape(1, D), gate_b=gla_gate_b,
        gla_norm_w=gla_norm_w, sinks=jnp.pad(attn_sinks, ((0, 0), (0, 128 - N_Q))),
        w_in_main=w_main, w_in_lr=w_lr,
        w2p=jnp.pad(w2_full, ((0, LRP - RANK), (0, 0))).astype(BF16),
        w_out=g_out.reshape(D, D),
        w_gate=g_gate, w_up=g_up, w_down=g_down.reshape(FFN, D),
    )

    tags = ["w_in", "w_out", "w_gate", "w_up", "w_down"]
    recv = {}

    def reduce_part(tag, part):
        sums = _pair_reduce(part, name="pair_reduce_" + tag)
        recv[tag], = _chip_exchange([sums], name="chip_exchange_" + tag, collective_id=2 + tags.index(tag))

    loss_blk, dx, g = _local_step(x[0], loss_target[0], p, reduce_part)
    reduce_part("w_in", _main_to_shards(g["w_in_main"], g["w_in_lr"]))
    big = [_sum_join(recv[tg], name="sum_join_" + tg) for tg in tags]

    small = _unpack_small(_sum_small(_pack_small(g, g["w2"], loss_blk[0, 0])))
    loss = small["loss"]
    g_w2_mine = lax.dynamic_slice_in_dim(small["w2"], chip * (GLA_H * DK // 4), GLA_H * DK // 4, axis=1)

    grads = dict(
        norm1_w=small["norm1_w"].reshape(1, D), w_in=big[0], gla_gate_w2=g_w2_mine,
        gla_gate_b=small["gate_b"].reshape(1, -1), attn_sinks=small["sinks"].reshape(1, N_Q),
        gla_norm_w=small["gla_norm_w"].reshape(1, DV), w_out=big[1], norm2_w=small["norm2_w"].reshape(1, D),
        w_ffn_gate=big[2], w_ffn_up=big[3], w_ffn_down=big[4], final_norm_w=small["final_norm_w"].reshape(1, D))
    def lin(a):
        return jnp.transpose(a.reshape(D, CS))

    def unlin(a):
        return jnp.transpose(a)

    grads["w_in"] = lin(big[0])
    weights = dict(
        norm1_w=(norm1_w, m_norm1_w, v_norm1_w), w_in=(lin(w_in), lin(m_w_in), lin(v_w_in)),
        gla_gate_w2=(w2_s, m_gla_gate_w2[0], v_gla_gate_w2[0]), gla_gate_b=(gla_gate_b, m_gla_gate_b, v_gla_gate_b),
        attn_sinks=(attn_sinks, m_attn_sinks, v_attn_sinks), gla_norm_w=(gla_norm_w, m_gla_norm_w, v_gla_norm_w),
        w_out=(w_out_s, m_w_out[0], v_w_out[0]), norm2_w=(norm2_w, m_norm2_w, v_norm2_w),
        w_ffn_gate=(w_gate_s, m_w_ffn_gate[0], v_w_ffn_gate[0]), w_ffn_up=(w_up_s, m_w_ffn_up[0], v_w_ffn_up[0]),
        w_ffn_down=(w_down_s, m_w_ffn_down[0], v_w_ffn_down[0]),
        final_norm_w=(final_norm_w.reshape(1, D), m_final_norm_w.reshape(1, D), v_final_norm_w.reshape(1, D)))
    names = ["norm1_w", "w_in", "gla_gate_w2", "gla_gate_b", "attn_sinks", "gla_norm_w", "w_out", "norm2_w",
             "w_ffn_gate", "w_ffn_up", "w_ffn_down", "final_norm_w"]
    lead = {"norm1_w": False, "gla_gate_b": False, "attn_sinks": False, "gla_norm_w": False, "norm2_w": False}
    g_out_l, d_out, m_out, v_out = [], [], [], []
    for nm in names:
        w, m, v = weights[nm]
        gr = grads[nm]
        dl, nmn, nvn = _adamw(w, gr, m, v, name="adamw_" + nm)
        if nm == "w_in":
            gr, dl, nmn, nvn, w = big[0], unlin(dl), unlin(nmn), unlin(nvn), w_in_s
        if nm == "final_norm_w":
            shape = (D,)
        elif nm in lead:
            shape = w.shape
        else:
            shape = (1,) + w.shape
        g_out_l.append(gr.reshape(shape))
        d_out.append(dl.reshape(shape))
        m_out.append(nmn.reshape(shape))
        v_out.append(nvn.reshape(shape))
    return (loss, dx[None], *g_out_l, *d_out, *m_out, *v_out)
```

```python
import functools

import jax
import jax.numpy as jnp
from jax import lax
from jax.experimental import pallas as pl
from jax.experimental.pallas import tpu as pltpu
from jax.experimental.pallas import tpu_sc as plsc

F32 = jnp.float32
BF16 = jnp.bfloat16

D = 2048
HEAD_DIM = 64
N_Q = 32
N_KV = 4
GROUP = 8
WINDOW = 128
GLA_H = 4
DK = 256
DV = 512
RANK = 16
CHUNK = 64
FFN = 5632
EPS = 1e-6
MASK_VALUE = -1e30
GATE_NORM = 16.0
LR, B1, B2, ADAM_EPS, WD, STEP = 0.001, 0.9, 0.999, 1e-08, 0.01, 10

MAIN = 12800
O_MERGE, O_GLA, O_ATTN = 0, 6144, 10240
W_MERGE, W_GLA, W_ATTN = 3 * DV, 2 * DK + DV, D + 2 * N_KV * HEAD_DIM
LR_AT = 6656
LRP = 128


def _main_pieces():
    o_gq, o_gk, o_gv, o_gr, o_ga, o_gb = 2560, 3584, 4608, LR_AT + RANK, LR_AT + RANK + D, LR_AT + RANK + 2 * D
    pieces = []
    for h in range(GLA_H):
        pieces += [(o_gr + DV * h, DV), (o_ga + DV * h, DV), (o_gb + DV * h, DV)]
    for h in range(GLA_H):
        pieces += [(o_gq + DK * h, DK), (o_gk + DK * h, DK), (o_gv + DV * h, DV)]
    pieces.append((0, W_ATTN))
    return pieces


SHARD_COLS = 3204

VMEM_LIMIT = 56 * 1024 * 1024
MESH = pl.DeviceIdType.MESH
ANY = pl.BlockSpec(memory_space=pl.ANY)


def _params(ngrid):
    return pltpu.CompilerParams(dimension_semantics=("arbitrary",) * ngrid, vmem_limit_bytes=VMEM_LIMIT)


def _shard_segments(lo, hi):
    segs = []
    while lo < hi:
        s = lo // SHARD_COLS
        e = min(hi, (s + 1) * SHARD_COLS)
        segs.append((s, lo - s * SHARD_COLS, e - s * SHARD_COLS))
        lo = e
    return segs


def _lanes(pieces):
    return pieces[0] if len(pieces) == 1 else jnp.concatenate(pieces, axis=1)


def _w_in_to_main(shards):
    rb = 256

    def body(g_ref, main_ref, lr_ref):
        at = 0
        for a, w in _main_pieces():
            main_ref[:, at:at + w] = _lanes([g_ref[s, :, lo:hi] for s, lo, hi in _shard_segments(a, a + w)])
            at += w
        lr = [g_ref[s, :, lo:hi] for s, lo, hi in _shard_segments(LR_AT, LR_AT + RANK)]
        lr_ref[...] = _lanes(lr + [jnp.zeros((rb, LRP - RANK), lr_ref.dtype)])

    return pl.pallas_call(
        body, name="w_in_to_main", grid=(D // rb,),
        in_specs=[pl.BlockSpec((4, rb, SHARD_COLS), lambda i: (0, i, 0))],
        out_specs=[pl.BlockSpec((rb, MAIN), lambda i: (i, 0)), pl.BlockSpec((rb, LRP), lambda i: (i, 0))],
        out_shape=[jax.ShapeDtypeStruct((D, MAIN), shards.dtype), jax.ShapeDtypeStruct((D, LRP), shards.dtype)],
        compiler_params=_params(1),
    )(shards)


def _main_to_shards(g_main, g_lr):
    rb = 256
    at, sources = 0, [(LR_AT, RANK, 1, 0)]
    for a, w in _main_pieces():
        sources.append((a, w, 0, at))
        at += w
    sources.sort()

    def body(main_ref, lr_ref, out_ref):
        refs = (main_ref, lr_ref)
        for s in range(4):
            lo, hi = s * SHARD_COLS, (s + 1) * SHARD_COLS
            pieces = []
            for a, w, which, src in sources:
                b, e = max(a, lo), min(a + w, hi)
                if b < e:
                    pieces.append(refs[which][:, src + b - a:src + e - a])
            out_ref[s] = _lanes(pieces)

    return pl.pallas_call(
        body, name="main_to_shards", grid=(D // rb,),
        in_specs=[pl.BlockSpec((rb, MAIN), lambda i: (i, 0)), pl.BlockSpec((rb, LRP), lambda i: (i, 0))],
        out_specs=pl.BlockSpec((4, rb, SHARD_COLS), lambda i: (0, i, 0)),
        out_shape=jax.ShapeDtypeStruct((4, D, SHARD_COLS), g_main.dtype), compiler_params=_params(1),
    )(g_main, g_lr)


def _pick(n, cands):
    for c in cands:
        if n % c == 0:
            return c
    return n


def _sigmoid(x):
    return 1.0 / (1.0 + jnp.exp(-x))


def _dot(a, b, ca, cb):
    return lax.dot_general(a, b, (((ca,), (cb,)), ((), ())), preferred_element_type=F32)


def _matmul(a, b, *, name, ta=False, tb=False, add=None, out_dtype=F32, bm=None, bj=None, bc=None,
            b_slots=False, out_slots=False):
    C, M = a.shape if ta else a.shape[::-1]
    if b_slots:
        if tb:
            J, bc = b.shape[1], b.shape[2]
            assert b.shape[0] * bc == C
        else:
            bj = b.shape[2]
            J = b.shape[0] * bj
            assert b.shape[1] == C
    else:
        J = b.shape[0] if tb else b.shape[1]
        assert (b.shape[1] if tb else b.shape[0]) == C
    bm = bm or _pick(M, (1024, 512, 256, 128) if C <= FFN else (512, 256, 128))
    bj = bj or _pick(J, (1280, 1024, 512, 256, 128) if C <= D else (512, 256, 128))
    bc = bc or (C if C <= FFN else _pick(C, (6400,)))
    nm, nj, nc = M // bm, J // bj, C // bc
    has_add = add is not None

    def body(*refs):
        a_ref, b_ref = refs[0], refs[1]
        add_ref = refs[2] if has_add else None
        o_ref = refs[3] if has_add else refs[2]
        p = _dot(a_ref[...].astype(BF16), b_ref[...].astype(BF16), 0 if ta else 1, 1 if tb else 0)

        def finish(acc):
            if has_add:
                acc = acc + add_ref[...]
            o_ref[...] = acc.astype(o_ref.dtype)

        if nc == 1:
            finish(p)
        else:
            acc_ref = refs[-1]
            c = pl.program_id(2)

            @pl.when(c == 0)
            def _():
                acc_ref[...] = p

            @pl.when(c > 0)
            def _():
                acc_ref[...] += p

            @pl.when(c == nc - 1)
            def _():
                finish(acc_ref[...])

    a_spec = pl.BlockSpec((bc, bm), lambda m, j, c: (c, m)) if ta else pl.BlockSpec((bm, bc), lambda m, j, c: (m, c))
    if b_slots:
        b_spec = (pl.BlockSpec((None, bj, bc), lambda m, j, c: (c, j, 0)) if tb
                  else pl.BlockSpec((None, bc, bj), lambda m, j, c: (j, c, 0)))
    else:
        b_spec = (pl.BlockSpec((bj, bc), lambda m, j, c: (j, c)) if tb
                  else pl.BlockSpec((bc, bj), lambda m, j, c: (c, j)))
    if out_slots:
        assert not has_add
        o_spec = pl.BlockSpec((None, bm, bj), lambda m, j, c: (j, m, 0))
        out_shape = jax.ShapeDtypeStruct((nj, M, bj), out_dtype)
    else:
        o_spec = pl.BlockSpec((bm, bj), lambda m, j, c: (m, j))
        out_shape = jax.ShapeDtypeStruct((M, J), out_dtype)
    in_specs = [a_spec, b_spec] + ([o_spec] if has_add else [])
    args = (a, b) + ((add,) if has_add else ())
    return pl.pallas_call(
        body, name=name, grid=(nm, nj, nc), in_specs=in_specs, out_specs=o_spec,
        out_shape=out_shape,
        scratch_shapes=[pltpu.VMEM((bm, bj), F32)] if nc > 1 else [],
        compiler_params=_params(3),
    )(*args)


def _rows(T):
    return _pick(T, (256, 128))


def _rms_fwd(x, w, *, name):
    T = x.shape[0]
    rb = _rows(T)

    def body(x_ref, w_ref, u_ref, ut_ref):
        xv = x_ref[...]
        r = lax.rsqrt(jnp.mean(xv * xv, axis=-1, keepdims=True) + EPS)
        u = xv * r * w_ref[...]
        u_ref[...] = u.astype(BF16)
        ut_ref[...] = u.T.astype(BF16)

    return pl.pallas_call(
        body, name=name, grid=(T // rb,),
        in_specs=[pl.BlockSpec((rb, D), lambda i: (i, 0)), pl.BlockSpec((1, D), lambda i: (0, 0))],
        out_specs=[pl.BlockSpec((rb, D), lambda i: (i, 0)), pl.BlockSpec((D, rb), lambda i: (0, i))],
        out_shape=[jax.ShapeDtypeStruct((T, D), BF16), jax.ShapeDtypeStruct((D, T), BF16)], compiler_params=_params(1),
    )(x, w)


def _rms_bwd(dy, h, w, resid, *, name):
    T = h.shape[0]
    rb = _rows(T)

    def body(dy_ref, h_ref, w_ref, res_ref, dh_ref, dhb_ref, dw_ref):
        @pl.when(pl.program_id(0) == 0)
        def _():
            dw_ref[...] = jnp.zeros_like(dw_ref)

        hv = h_ref[...]
        r = lax.rsqrt(jnp.mean(hv * hv, axis=-1, keepdims=True) + EPS)
        hn = hv * r
        dyv = dy_ref[...]
        dw_ref[...] += jnp.sum(dyv * hn, axis=0, keepdims=True)
        t = dyv * w_ref[...]
        dh = res_ref[...] + r * (t - hn * jnp.mean(t * hn, axis=-1, keepdims=True))
        dh_ref[...] = dh
        dhb_ref[...] = dh.astype(BF16)

    blk = pl.BlockSpec((rb, D), lambda i: (i, 0))
    vec = pl.BlockSpec((1, D), lambda i: (0, 0))
    return pl.pallas_call(
        body, name=name, grid=(T // rb,), in_specs=[blk, blk, vec, blk], out_specs=[blk, blk, vec],
        out_shape=[jax.ShapeDtypeStruct((T, D), F32), jax.ShapeDtypeStruct((T, D), BF16),
                   jax.ShapeDtypeStruct((1, D), F32)],
        compiler_params=_params(1),
    )(dy, h, w, resid)


def _final_loss(h, w, target):
    T = h.shape[0]
    rb = _rows(T)

    def body(h_ref, w_ref, t_ref, loss_ref, dh_ref, dhb_ref, dw_ref):
        @pl.when(pl.program_id(0) == 0)
        def _():
            dw_ref[...] = jnp.zeros_like(dw_ref)
            loss_ref[...] = jnp.zeros_like(loss_ref)

        hv = h_ref[...]
        wv = w_ref[...]
        r = lax.rsqrt(jnp.mean(hv * hv, axis=-1, keepdims=True) + EPS)
        hn = hv * r
        e = hn * wv - t_ref[...]
        row = jnp.sum(e * e, axis=-1, keepdims=True) * (0.5 / D)
        loss_ref[...] += jnp.broadcast_to(jnp.sum(row, axis=0, keepdims=True), loss_ref.shape)
        dy = e * (1.0 / D)
        dw_ref[...] += jnp.sum(dy * hn, axis=0, keepdims=True)
        t = dy * wv
        dh = r * (t - hn * jnp.mean(t * hn, axis=-1, keepdims=True))
        dh_ref[...] = dh
        dhb_ref[...] = dh.astype(BF16)

    blk = pl.BlockSpec((rb, D), lambda i: (i, 0))
    vec = pl.BlockSpec((1, D), lambda i: (0, 0))
    return pl.pallas_call(
        body, name="final_loss", grid=(T // rb,), in_specs=[blk, vec, blk],
        out_specs=[pl.BlockSpec((8, 128), lambda i: (0, 0)), blk, blk, vec],
        out_shape=[jax.ShapeDtypeStruct((8, 128), F32), jax.ShapeDtypeStruct((T, D), F32),
                   jax.ShapeDtypeStruct((T, D), BF16), jax.ShapeDtypeStruct((1, D), F32)],
        compiler_params=_params(1),
    )(h, w, target)


PANEL = FFN // 4


def _ffn_fwd(v2, w_gate, w_up):
    T = v2.shape[0]
    bm = _pick(T, (512, 256))

    def body(a_ref, bg_ref, bu_ref, gate_ref, up_ref, ff_ref, fft_ref):
        a = a_ref[...]
        g = _dot(a, bg_ref[...], 1, 0)
        u = _dot(a, bu_ref[...], 1, 0)
        gate_ref[...] = g
        up_ref[...] = u
        ff = g * _sigmoid(g) * u
        ff_ref[...] = ff.astype(BF16)
        fft_ref[...] = ff.T.astype(BF16)

    panel = pl.BlockSpec((None, D, PANEL), lambda j, m: (j, 0, 0))
    out = pl.BlockSpec((bm, PANEL), lambda j, m: (m, j))
    return pl.pallas_call(
        body, name="ffn_fwd", grid=(4, T // bm),
        in_specs=[pl.BlockSpec((bm, D), lambda j, m: (m, 0)), panel, panel],
        out_specs=[out, out, out, pl.BlockSpec((PANEL, bm), lambda j, m: (j, m))],
        out_shape=[jax.ShapeDtypeStruct((T, FFN), F32), jax.ShapeDtypeStruct((T, FFN), F32),
                   jax.ShapeDtypeStruct((T, FFN), BF16), jax.ShapeDtypeStruct((FFN, T), BF16)],
        compiler_params=_params(2),
    )(v2, w_gate, w_up)


def _ffn_bwd_hidden(dh2, w_down, gate, up):
    T = dh2.shape[0]
    bm = _pick(T, (512, 256))

    def body(a_ref, b_ref, g_ref, u_ref, dg_ref, du_ref):
        d = _dot(a_ref[...], b_ref[...], 1, 1)
        g = g_ref[...]
        sg = _sigmoid(g)
        du_ref[...] = (d * g * sg).astype(BF16)
        dg_ref[...] = (d * u_ref[...] * sg * (1.0 + g * (1.0 - sg))).astype(BF16)

    blk = pl.BlockSpec((bm, PANEL), lambda j, m: (m, j))
    return pl.pallas_call(
        body, name="ffn_bwd_hidden", grid=(4, T // bm),
        in_specs=[pl.BlockSpec((bm, D), lambda j, m: (m, 0)), pl.BlockSpec((PANEL, D), lambda j, m: (j, 0)), blk, blk],
        out_specs=[blk, blk], out_shape=[jax.ShapeDtypeStruct((T, FFN), BF16)] * 2, compiler_params=_params(2),
    )(dh2, w_down, gate, up)


def _ffn_bwd_input(dgate, dup, w_gate, w_up):
    T = dgate.shape[0]
    bm = _pick(T, (1024, 512, 256))
    bj = 1024

    def body(ag_ref, au_ref, bg_ref, bu_ref, o_ref, acc_ref):
        c = pl.program_id(2)
        p = _dot(ag_ref[...], bg_ref[...], 1, 1) + _dot(au_ref[...], bu_ref[...], 1, 1)

        @pl.when(c == 0)
        def _():
            acc_ref[...] = p

        @pl.when(c > 0)
        def _():
            acc_ref[...] += p

        @pl.when(c == 3)
        def _():
            o_ref[...] = acc_ref[...]

    a = pl.BlockSpec((bm, PANEL), lambda m, j, c: (m, c))
    b = pl.BlockSpec((None, bj, PANEL), lambda m, j, c: (c, j, 0))
    return pl.pallas_call(
        body, name="ffn_bwd_input", grid=(T // bm, D // bj, 4), in_specs=[a, a, b, b],
        out_specs=pl.BlockSpec((bm, bj), lambda m, j, c: (m, j)), out_shape=jax.ShapeDtypeStruct((T, D), F32),
        scratch_shapes=[pltpu.VMEM((bm, bj), F32)], compiler_params=_params(3),
    )(dgate, dup, w_gate, w_up)


def _colsum(x, *, name):
    T, W = x.shape
    rb = _rows(T)

    def body(x_ref, o_ref):
        @pl.when(pl.program_id(0) == 0)
        def _():
            o_ref[...] = jnp.zeros_like(o_ref)

        o_ref[...] += jnp.sum(x_ref[...], axis=0, keepdims=True)

    return pl.pallas_call(
        body, name=name, grid=(T // rb,), in_specs=[pl.BlockSpec((rb, W), lambda i: (i, 0))],
        out_specs=pl.BlockSpec((1, W), lambda i: (0, 0)), out_shape=jax.ShapeDtypeStruct((1, W), F32),
        compiler_params=_params(1),
    )(x)


def _merge_fwd(attn_o, gla_raw, proj, gla_norm_w):
    T = attn_o.shape[0]
    rb = _rows(T)

    def body(a_ref, g_ref, gr_ref, ga_ref, gb_ref, w_ref, o_ref, ot_ref):
        graw = g_ref[...]
        r = lax.rsqrt(jnp.mean(graw * graw, axis=-1, keepdims=True) + EPS)
        gr = gr_ref[...]
        go = graw * r * w_ref[...] * (gr * _sigmoid(gr))
        merged = _sigmoid(ga_ref[...]) * a_ref[...] + _sigmoid(gb_ref[...]) * go
        o_ref[...] = merged.astype(BF16)
        ot_ref[...] = merged.T.astype(BF16)

    def sec(k):
        return pl.BlockSpec((rb, DV), lambda i, h: (i, O_MERGE // DV + 3 * h + k))

    blk = pl.BlockSpec((rb, DV), lambda i, h: (i, h))
    return pl.pallas_call(
        body, name="merge_fwd", grid=(T // rb, GLA_H),
        in_specs=[blk, blk, sec(0), sec(1), sec(2), pl.BlockSpec((1, DV), lambda i, h: (0, 0))],
        out_specs=[blk, pl.BlockSpec((DV, rb), lambda i, h: (h, i))],
        out_shape=[jax.ShapeDtypeStruct((T, D), BF16), jax.ShapeDtypeStruct((D, T), BF16)], compiler_params=_params(2),
    )(attn_o, gla_raw, proj, proj, proj, gla_norm_w)


def _merge_bwd(dm, attn_o, gla_raw, proj, gla_norm_w):
    T = attn_o.shape[0]
    rb = _rows(T)

    def body(dm_ref, a_ref, g_ref, gr_ref, ga_ref, gb_ref, w_ref, da_ref, dg_ref, dp_ref, dw_ref):
        @pl.when((pl.program_id(0) == 0) & (pl.program_id(1) == 0))
        def _():
            dw_ref[...] = jnp.zeros_like(dw_ref)

        dmv = dm_ref[...]
        av = a_ref[...]
        graw = g_ref[...]
        gr = gr_ref[...]
        wv = w_ref[...]
        sa = _sigmoid(ga_ref[...])
        sb = _sigmoid(gb_ref[...])
        r = lax.rsqrt(jnp.mean(graw * graw, axis=-1, keepdims=True) + EPS)
        gnh = graw * r
        gn = gnh * wv
        sr = _sigmoid(gr)
        sl = gr * sr
        go = gn * sl
        da_ref[...] = dmv * sa
        dgo = dmv * sb
        dp_ref[:, 0:DV] = (dgo * gn * sr * (1.0 + gr * (1.0 - sr))).astype(BF16)
        dp_ref[:, DV:2 * DV] = (dmv * av * sa * (1.0 - sa)).astype(BF16)
        dp_ref[:, 2 * DV:3 * DV] = (dmv * go * sb * (1.0 - sb)).astype(BF16)
        dgn = dgo * sl
        dw_ref[...] += jnp.sum(dgn * gnh, axis=0, keepdims=True)
        t = dgn * wv
        dg_ref[...] = r * (t - gnh * jnp.mean(t * gnh, axis=-1, keepdims=True))

    def sec(k):
        return pl.BlockSpec((rb, DV), lambda i, h: (i, O_MERGE // DV + 3 * h + k))

    blk = pl.BlockSpec((rb, DV), lambda i, h: (i, h))
    vec = pl.BlockSpec((1, DV), lambda i, h: (0, 0))
    return pl.pallas_call(
        body, name="merge_bwd", grid=(T // rb, GLA_H),
        in_specs=[blk, blk, blk, sec(0), sec(1), sec(2), vec],
        out_specs=[blk, blk, pl.BlockSpec((rb, W_MERGE), lambda i, h: (i, O_MERGE // W_MERGE + h)), vec],
        out_shape=[jax.ShapeDtypeStruct((T, D), F32), jax.ShapeDtypeStruct((T, D), F32),
                   jax.ShapeDtypeStruct((T, MAIN), BF16), jax.ShapeDtypeStruct((1, DV), F32)],
        compiler_params=_params(2),
    )(dm, attn_o, gla_raw, proj, proj, proj, gla_norm_w)


def _attn_band(n):
    qi = lax.broadcasted_iota(jnp.int32, (GROUP * WINDOW, WINDOW), 0) & (WINDOW - 1)
    kj = lax.broadcasted_iota(jnp.int32, (GROUP * WINDOW, WINDOW), 1)
    cur = kj <= qi
    return cur, cur | (n > 0)


def _stack_heads(ref, h, scale=1.0):
    rows = jnp.concatenate(
        [ref[:, (h * GROUP + g) * HEAD_DIM:(h * GROUP + g + 1) * HEAD_DIM] for g in range(GROUP)], axis=0)
    return (rows * scale).astype(BF16) if scale != 1.0 else rows.astype(BF16)


def _stack_sinks(s_ref, h):
    return jnp.concatenate(
        [jnp.broadcast_to(s_ref[:, h * GROUP + g:h * GROUP + g + 1], (WINDOW, 1)) for g in range(GROUP)], axis=0)


def _attn_probs(qs, kp, kc, band, sink):
    cur, live = band
    s = jnp.where(cur, _dot(qs, kc, 1, 1), _dot(qs, kp, 1, 1))
    s = jnp.where(live, s, MASK_VALUE)
    m = jnp.maximum(jnp.max(s, axis=-1, keepdims=True), sink)
    e = jnp.exp(s - m)
    es = jnp.exp(sink - m)
    inv = 1.0 / (jnp.sum(e, axis=-1, keepdims=True) + es)
    return e * inv, es * inv


def _unfold(cur, x):
    mine = jnp.where(cur, x, 0.0)
    return mine.astype(BF16), (x - mine).astype(BF16)


def _attn_specs(nb, rev):
    def at(n):
        return (nb - 1 - n) if rev else n

    kcol, vcol = (O_ATTN + D) // 256, (O_ATTN + D) // 256 + 1
    q = pl.BlockSpec((WINDOW, D), lambda n: (at(n), O_ATTN // D))
    kc = pl.BlockSpec((WINDOW, 256), lambda n: (at(n), kcol))
    kp = pl.BlockSpec((WINDOW, 256), lambda n: (jnp.maximum(at(n) - 1, 0), kcol))
    vc = pl.BlockSpec((WINDOW, 256), lambda n: (at(n), vcol))
    vp = pl.BlockSpec((WINDOW, 256), lambda n: (jnp.maximum(at(n) - 1, 0), vcol))
    sk = pl.BlockSpec((1, 128), lambda n: (0, 0))
    o = pl.BlockSpec((WINDOW, D), lambda n: (at(n), 0))
    return q, kc, kp, vc, vp, sk, o


def _attn_fwd(proj, sinks):
    T = proj.shape[0]
    nb = T // WINDOW

    def body(q_ref, kc_ref, kp_ref, vc_ref, vp_ref, s_ref, o_ref):
        band = _attn_band(pl.program_id(0))
        for h in range(N_KV):
            hs = slice(h * HEAD_DIM, (h + 1) * HEAD_DIM)
            p, _ = _attn_probs(_stack_heads(q_ref, h, HEAD_DIM ** -0.5), kp_ref[:, hs].astype(BF16),
                               kc_ref[:, hs].astype(BF16), band, _stack_sinks(s_ref, h))
            p_cur, p_prev = _unfold(band[0], p)
            o = _dot(p_cur, vc_ref[:, hs].astype(BF16), 1, 0) + _dot(p_prev, vp_ref[:, hs].astype(BF16), 1, 0)
            for g in range(GROUP):
                hg = h * GROUP + g
                o_ref[:, hg * HEAD_DIM:(hg + 1) * HEAD_DIM] = o[g * WINDOW:(g + 1) * WINDOW]

    q, kc, kp, vc, vp, sk, o = _attn_specs(nb, False)
    return pl.pallas_call(
        body, name="attn_fwd", grid=(nb,), in_specs=[q, kc, kp, vc, vp, sk], out_specs=o,
        out_shape=jax.ShapeDtypeStruct((T, D), F32), compiler_params=_params(1),
    )(proj, proj, proj, proj, proj, sinks)


def _attn_bwd(proj, sinks, d_o, dproj):
    T = proj.shape[0]
    nb = T // WINDOW
    kat, vat = D, D + N_KV * HEAD_DIM

    def body(q_ref, kc_ref, kp_ref, vc_ref, vp_ref, s_ref, do_ref, _, dp_ref, ds_ref, ck_ref, cv_ref):
        i = pl.program_id(0)
        n = nb - 1 - i

        @pl.when(i == 0)
        def _():
            ck_ref[...] = jnp.zeros_like(ck_ref)
            cv_ref[...] = jnp.zeros_like(cv_ref)
            ds_ref[...] = jnp.zeros_like(ds_ref)

        band = _attn_band(n)
        for h in range(N_KV):
            hs = slice(h * HEAD_DIM, (h + 1) * HEAD_DIM)
            kp, kc = kp_ref[:, hs].astype(BF16), kc_ref[:, hs].astype(BF16)
            vp, vc = vp_ref[:, hs].astype(BF16), vc_ref[:, hs].astype(BF16)
            qs = _stack_heads(q_ref, h, HEAD_DIM ** -0.5)
            do = _stack_heads(do_ref, h)
            p, ps = _attn_probs(qs, kp, kc, band, _stack_sinks(s_ref, h))
            dp = jnp.where(band[0], _dot(do, vc, 1, 1), _dot(do, vp, 1, 1))
            delta = jnp.sum(p * dp, axis=-1, keepdims=True)
            ds_cur, ds_prev = _unfold(band[0], p * (dp - delta))
            p_cur, p_prev = _unfold(band[0], p)
            dsink = -ps * delta
            dq = ((_dot(ds_cur, kc, 1, 0) + _dot(ds_prev, kp, 1, 0)) * (HEAD_DIM ** -0.5)).astype(BF16)
            for g in range(GROUP):
                hg = h * GROUP + g
                rows = slice(g * WINDOW, (g + 1) * WINDOW)
                ds_ref[hg:hg + 1, :] += jnp.broadcast_to(jnp.sum(dsink[rows], axis=0, keepdims=True), (1, 128))
                dp_ref[:, hg * HEAD_DIM:(hg + 1) * HEAD_DIM] = dq[rows]
            dp_ref[:, kat + h * HEAD_DIM:kat + (h + 1) * HEAD_DIM] = (_dot(ds_cur, qs, 0, 0) + ck_ref[:, hs]).astype(BF16)
            dp_ref[:, vat + h * HEAD_DIM:vat + (h + 1) * HEAD_DIM] = (_dot(p_cur, do, 0, 0) + cv_ref[:, hs]).astype(BF16)
            ck_ref[:, hs] = _dot(ds_prev, qs, 0, 0)
            cv_ref[:, hs] = _dot(p_prev, do, 0, 0)

    q, kc, kp, vc, vp, sk, o = _attn_specs(nb, True)
    return pl.pallas_call(
        body, name="attn_bwd", grid=(nb,), in_specs=[q, kc, kp, vc, vp, sk, o, ANY],
        out_specs=[pl.BlockSpec((WINDOW, W_ATTN), lambda n: (nb - 1 - n, O_ATTN // W_ATTN)),
                   pl.BlockSpec((N_Q, 128), lambda n: (0, 0))],
        out_shape=[jax.ShapeDtypeStruct((T, MAIN), BF16), jax.ShapeDtypeStruct((N_Q, 128), F32)],
        scratch_shapes=[pltpu.VMEM((WINDOW, 256), F32), pltpu.VMEM((WINDOW, 256), F32)],
        input_output_aliases={7: 0}, compiler_params=_params(1),
    )(proj, proj, proj, proj, proj, sinks, d_o, dproj)


def _split3(x):
    hi = x.astype(BF16)
    r1 = x - hi.astype(F32)
    mid = r1.astype(BF16)
    lo = (r1 - mid.astype(F32)).astype(BF16)
    return hi, mid, lo


def _tri_sum(tri, x):
    hi, mid, lo = _split3(x)
    return _dot(tri, hi, 1, 0) + _dot(tri, mid, 1, 0) + _dot(tri, lo, 1, 0)


def _gla_chunk(q, k, lr, w2, b, lower):
    logit = _dot(lr.astype(BF16), w2.astype(BF16), 1, 0) + b
    la = (jnp.minimum(logit, 0.0) - jnp.log(1.0 + jnp.exp(-jnp.abs(logit)))) * (1.0 / GATE_NORM)
    g = _tri_sum(lower, la)
    gl = g[CHUNK - 1:CHUNK, :]
    eg = jnp.exp(g)
    qd = q * (DK ** -0.5) * eg
    ki = k * jnp.exp(-g)
    ke = k * jnp.exp(gl - g)
    return logit, g, gl, eg, qd, ki, ke


def _tri(lower):
    r = lax.broadcasted_iota(jnp.int32, (CHUNK, CHUNK), 0)
    c = lax.broadcasted_iota(jnp.int32, (CHUNK, CHUNK), 1)
    return (r >= c) if lower else (r <= c)


def _gla_rows(T):
    return _pick(T, (256, 128, 64))


def _gla_fwd(proj, proj_lr, w2p, gate_b):
    T = proj.shape[0]
    rb = _gla_rows(T)
    per = rb // CHUNK

    def body(q_ref, k_ref, v_ref, lr_ref, w2_ref, b_ref, o_ref, st_ref, s_scr):
        @pl.when(pl.program_id(1) == 0)
        def _():
            s_scr[...] = jnp.zeros_like(s_scr)

        low = _tri(True)
        lower = low.astype(BF16)
        for i in range(per):
            rows = slice(i * CHUNK, (i + 1) * CHUNK)
            _, g, gl, eg, qd, ki, ke = _gla_chunk(q_ref[rows, :], k_ref[rows, :], lr_ref[rows, :], w2_ref[...],
                                                  b_ref[...], lower)
            v = v_ref[rows, :].astype(BF16)
            qdb = qd.astype(BF16)
            att = jnp.where(low, _dot(qdb, ki.astype(BF16), 1, 1), 0.0)
            st = s_scr[...]
            st_ref[0, i] = st
            o_ref[rows, :] = _dot(att.astype(BF16), v, 1, 0) + _dot(qdb, st.astype(BF16), 1, 1)
            s_scr[...] = st * jnp.exp(gl) + _dot(v, ke.astype(BF16), 0, 0)

    return pl.pallas_call(
        body, name="gla_fwd", grid=(GLA_H, T // rb),
        in_specs=[pl.BlockSpec((rb, DK), lambda h, n: (n, (O_GLA + W_GLA * h) // DK)),
                  pl.BlockSpec((rb, DK), lambda h, n: (n, (O_GLA + W_GLA * h) // DK + 1)),
                  pl.BlockSpec((rb, DV), lambda h, n: (n, (O_GLA + W_GLA * h) // DV + 1)),
                  pl.BlockSpec((rb, LRP), lambda h, n: (n, 0)),
                  pl.BlockSpec((LRP, DK), lambda h, n: (0, h)),
                  pl.BlockSpec((1, DK), lambda h, n: (0, h))],
        out_specs=[pl.BlockSpec((rb, DV), lambda h, n: (n, h)),
                   pl.BlockSpec((1, per, DV, DK), lambda h, n: (h, n, 0, 0))],
        out_shape=[jax.ShapeDtypeStruct((T, GLA_H * DV), F32),
                   jax.ShapeDtypeStruct((GLA_H, T // CHUNK, DV, DK), F32)],
        scratch_shapes=[pltpu.VMEM((DV, DK), F32)], compiler_params=_params(2),
    )(proj, proj, proj, proj_lr, w2p, gate_b)


def _gla_bwd(proj, proj_lr, w2p, gate_b, states, d_o, dproj):
    T = proj.shape[0]
    rb = _gla_rows(T)
    per = rb // CHUNK
    nblk = T // rb

    def body(q_ref, k_ref, v_ref, lr_ref, w2_ref, b_ref, st_ref, do_ref, _, dp_ref, dl_ref, ds_scr):
        @pl.when(pl.program_id(1) == 0)
        def _():
            ds_scr[...] = jnp.zeros_like(ds_scr)

        low = _tri(True)
        lower = low.astype(BF16)
        upper = _tri(False).astype(BF16)
        for i in reversed(range(per)):
            rows = slice(i * CHUNK, (i + 1) * CHUNK)
            logit, g, gl, eg, qd, ki, ke = _gla_chunk(q_ref[rows, :], k_ref[rows, :], lr_ref[rows, :], w2_ref[...],
                                                      b_ref[...], lower)
            v = v_ref[rows, :].astype(BF16)
            do = do_ref[rows, :].astype(BF16)
            qdb, kib, keb = qd.astype(BF16), ki.astype(BF16), ke.astype(BF16)
            att = jnp.where(low, _dot(qdb, kib, 1, 1), 0.0).astype(BF16)
            sp = st_ref[0, i]
            ds = ds_scr[...]
            dsb = ds.astype(BF16)
            datt = jnp.where(low, _dot(do, v, 1, 1), 0.0).astype(BF16)
            dp_ref[rows, 2 * DK:] = (_dot(att, do, 0, 0) + _dot(keb, dsb, 1, 1)).astype(BF16)
            dqd = _dot(datt, kib, 1, 0) + _dot(do, sp.astype(BF16), 1, 0)
            dki = _dot(datt, qdb, 0, 0)
            dke = _dot(v, dsb, 1, 0)
            decay = jnp.exp(gl)
            ds_scr[...] = ds * decay + _dot(do, qdb, 0, 0)
            ddec = jnp.sum(ds * sp, axis=0, keepdims=True)
            dp_ref[rows, 0:DK] = (dqd * (DK ** -0.5) * eg).astype(BF16)
            dp_ref[rows, DK:2 * DK] = (dki * jnp.exp(-g) + dke * jnp.exp(gl - g)).astype(BF16)
            dke_ke = dke * ke
            dg = dqd * qd - dki * ki - dke_ke
            dgl = jnp.sum(dke_ke, axis=0, keepdims=True) + ddec * decay
            dla = _tri_sum(upper, dg) + dgl
            dl_ref[rows, :] = dla * (1.0 / GATE_NORM) * (1.0 - _sigmoid(logit))

    def rev(n):
        return nblk - 1 - n

    return pl.pallas_call(
        body, name="gla_bwd", grid=(GLA_H, nblk),
        in_specs=[pl.BlockSpec((rb, DK), lambda h, n: (rev(n), (O_GLA + W_GLA * h) // DK)),
                  pl.BlockSpec((rb, DK), lambda h, n: (rev(n), (O_GLA + W_GLA * h) // DK + 1)),
                  pl.BlockSpec((rb, DV), lambda h, n: (rev(n), (O_GLA + W_GLA * h) // DV + 1)),
                  pl.BlockSpec((rb, LRP), lambda h, n: (rev(n), 0)),
                  pl.BlockSpec((LRP, DK), lambda h, n: (0, h)),
                  pl.BlockSpec((1, DK), lambda h, n: (0, h)),
                  pl.BlockSpec((1, per, DV, DK), lambda h, n: (h, rev(n), 0, 0)),
                  pl.BlockSpec((rb, DV), lambda h, n: (rev(n), h)),
                  ANY],
        out_specs=[pl.BlockSpec((rb, W_GLA), lambda h, n: (rev(n), O_GLA // W_GLA + h)),
                   pl.BlockSpec((rb, DK), lambda h, n: (rev(n), h))],
        out_shape=[jax.ShapeDtypeStruct((T, MAIN), BF16), jax.ShapeDtypeStruct((T, GLA_H * DK), F32)],
        scratch_shapes=[pltpu.VMEM((DV, DK), F32)], input_output_aliases={8: 0}, compiler_params=_params(2),
    )(proj, proj, proj, proj_lr, w2p, gate_b, states, d_o, dproj)


def _local_step(x, target, p, reduce_part):
    u, u_t = _rms_fwd(x, p["norm1_w"], name="rms1_fwd")
    proj = _matmul(u, p["w_in_main"], name="mm_proj")
    proj_lr = _matmul(u, p["w_in_lr"], name="mm_proj_lr")
    attn_o = _attn_fwd(proj, p["sinks"])
    gla_raw, states = _gla_fwd(proj, proj_lr, p["w2p"], p["gate_b"])
    merged, merged_t = _merge_fwd(attn_o, gla_raw, proj, p["gla_norm_w"])
    h1 = _matmul(merged, p["w_out"], add=x, name="mm_out")
    v2, v2_t = _rms_fwd(h1, p["norm2_w"], name="rms2_fwd")
    gate, up, ff, ff_t = _ffn_fwd(v2, p["w_gate"], p["w_up"])
    h2 = _matmul(ff, p["w_down"], add=h1, name="mm_down")
    loss, dh2, dh2_b, g_final = _final_loss(h2, p["final_norm_w"], target)

    dgate, dup = _ffn_bwd_hidden(dh2_b, p["w_down"], gate, up)
    reduce_part("w_down", _matmul(ff_t, dh2_b, out_dtype=BF16, bj=D, name="mm_gdown").reshape(4, FFN // 4, D))
    reduce_part("w_gate", _matmul(v2_t, dgate, out_dtype=BF16, bj=PANEL, out_slots=True, name="mm_ggate"))
    reduce_part("w_up", _matmul(v2_t, dup, out_dtype=BF16, bj=PANEL, out_slots=True, name="mm_gup"))
    dv2 = _ffn_bwd_input(dgate, dup, p["w_gate"], p["w_up"])
    dh1, dh1_b, g_norm2 = _rms_bwd(dv2, h1, p["norm2_w"], dh2, name="rms2_bwd")
    dmerged = _matmul(dh1_b, p["w_out"], tb=True, bj=D, name="mm_dmerged")
    reduce_part("w_out", _matmul(merged_t, dh1_b, out_dtype=BF16, bj=D, name="mm_gout").reshape(4, D // 4, D))
    d_attn, d_gla, dproj, g_gla_norm = _merge_bwd(dmerged, attn_o, gla_raw, proj, p["gla_norm_w"])
    dproj, dlogit = _gla_bwd(proj, proj_lr, p["w2p"], p["gate_b"], states, d_gla, dproj)
    dproj, g_sinks = _attn_bwd(proj, p["sinks"], d_attn, dproj)
    g_gate_b = _colsum(dlogit, name="colsum_gate_b")
    g_w2 = _matmul(proj_lr, dlogit, ta=True, name="mm_gw2")
    dproj_lr = _matmul(dlogit, p["w2p"], tb=True, out_dtype=BF16, name="mm_dlr")
    g_in_main = _matmul(u_t, dproj, out_dtype=BF16, name="mm_gin")
    g_in_lr = _matmul(u_t, dproj_lr, out_dtype=BF16, name="mm_gin_lr")
    du = _matmul(dproj, p["w_in_main"], tb=True, bm=_pick(x.shape[0], (1024, 256)), bj=1024, bc=3200, name="mm_du")
    du = _matmul(dproj_lr, p["w_in_lr"], tb=True, add=du, name="mm_du_lr")
    dx, _, g_norm1 = _rms_bwd(du, x, p["norm1_w"], dh1, name="rms1_bwd")
    grads = dict(norm1_w=g_norm1, w_in_main=g_in_main, w_in_lr=g_in_lr, w2=g_w2[:RANK], gate_b=g_gate_b,
                 sinks=g_sinks[:, 0].reshape(1, N_Q), gla_norm_w=g_gla_norm, norm2_w=g_norm2, final_norm_w=g_final)
    return loss, dx, grads


def _place():
    x, y, c = lax.axis_index("x"), lax.axis_index("y"), lax.axis_index("c")
    chips = [(1 - x, y), (x, 1 - y), (1 - x, 1 - y)]
    return x, y, c, chips


def _hbm_shape(s, dt):
    return jax.ShapeDtypeStruct(s, dt)


def _handshake(peers):
    barrier = pltpu.get_barrier_semaphore()
    for peer in peers:
        pl.semaphore_signal(barrier, inc=1, device_id=peer, device_id_type=MESH)
    pl.semaphore_wait(barrier, len(peers))


def _launch_copies(body, args, out_shape, sems, *, name, collective_id=None):
    if collective_id is None:
        return pl.pallas_call(
            body, name=name, in_specs=[ANY] * len(args), out_specs=[ANY] * len(out_shape), out_shape=out_shape,
            scratch_shapes=sems)(*args)
    return pl.kernel(
        body, name=name, out_type=out_shape, mesh=plsc.ScalarSubcoreMesh(axis_name="sequencer", num_cores=1),
        scratch_types=sems, compiler_params=pltpu.CompilerParams(collective_id=collective_id))(*args)


def _gather_shards(shards, *, name, collective_id=None, after=()):
    n = len(shards)
    first_out = n + len(after)

    def body(*refs):
        ins, outs = refs[:n], refs[first_out:first_out + n]
        ici_send, ici_recv, d2d_send, d2d_recv, local_sem = refs[first_out + n:]
        x, y, c, chips = _place()
        me = 2 * x + y
        sibling = (x, y, 1 - c)
        if collective_id is not None:
            _handshake([sibling] + [(*chip, c) for chip in chips])

        def half(w, slot, hc):
            r2 = shards[w].shape[0] // 2
            return outs[w].at[slot, pl.ds(hc * r2, r2), :]

        locals_ = [pltpu.make_async_copy(ins[w], outs[w].at[me], local_sem.at[w]) for w in range(n)]
        for cp in locals_:
            cp.start()
        sends = []
        for w in range(n):
            r2 = shards[w].shape[0] // 2
            for j, chip in enumerate(chips):
                cp = pltpu.make_async_remote_copy(
                    src_ref=ins[w].at[pl.ds(c * r2, r2), :], dst_ref=half(w, me, c),
                    send_sem=ici_send.at[w * 3 + j], recv_sem=ici_recv.at[w * 3 + j],
                    device_id=(*chip, c), device_id_type=MESH)
                cp.start()
                sends.append(cp)
        for w in range(n):
            for j, chip in enumerate(chips):
                slot = 2 * chip[0] + chip[1]
                got = half(w, slot, c)
                pltpu.make_async_remote_copy(
                    src_ref=got, dst_ref=got, send_sem=ici_send.at[w * 3 + j], recv_sem=ici_recv.at[w * 3 + j],
                    device_id=(*chip, c), device_id_type=MESH).wait_recv()
                cp = pltpu.make_async_remote_copy(
                    src_ref=got, dst_ref=got, send_sem=d2d_send.at[w * 3 + j], recv_sem=d2d_recv.at[w * 3 + j],
                    device_id=sibling, device_id_type=MESH)
                cp.start()
                sends.append(cp)
        for w in range(n):
            for j, chip in enumerate(chips):
                slot = 2 * chip[0] + chip[1]
                got = half(w, slot, 1 - c)
                pltpu.make_async_remote_copy(
                    src_ref=got, dst_ref=got, send_sem=d2d_send.at[w * 3 + j], recv_sem=d2d_recv.at[w * 3 + j],
                    device_id=sibling, device_id_type=MESH).wait_recv()
        for cp in sends:
            cp.wait_send()
        for cp in locals_:
            cp.wait()

    return _launch_copies(
        body, list(shards) + list(after), [_hbm_shape((4,) + s.shape, s.dtype) for s in shards],
        [pltpu.SemaphoreType.DMA((3 * n,)), pltpu.SemaphoreType.DMA((3 * n,)), pltpu.SemaphoreType.DMA((3 * n,)),
         pltpu.SemaphoreType.DMA((3 * n,)), pltpu.SemaphoreType.DMA((n,))],
        name=name, collective_id=collective_id)


def _gather_staged(w, small):
    R, C = w.shape
    R2 = R // 2
    rb = 256
    per = R2 // rb
    r2 = small.shape[0] // 2

    def body(w_ref, s_ref, out_ref, outs_ref, stage, ici_send, ici_recv, fwd_send, fwd_recv, in_sem, put_sem,
             push_send, push_recv):
        x, y, c, chips = _place()
        me = 2 * x + y
        sibling = (x, y, 1 - c)
        slots = [2 * chip[0] + chip[1] for chip in chips]

        def ici(j, src, dst, k):
            return pltpu.make_async_remote_copy(
                src_ref=src, dst_ref=dst, send_sem=ici_send.at[k], recv_sem=ici_recv.at[k],
                device_id=(*chips[j], c), device_id_type=MESH)

        sends = []
        for j in range(3):
            sends.append(ici(j, w_ref.at[pl.ds(c * R2, R2), :], out_ref.at[me, pl.ds(c * R2, R2), :], j))
            sends.append(ici(j, s_ref.at[pl.ds(c * r2, r2), :], outs_ref.at[me, pl.ds(c * r2, r2), :], 3 + j))
        for cp in sends:
            cp.start()
        mine = pltpu.make_async_copy(s_ref, outs_ref.at[me], put_sem.at[2])
        mine.start()

        def own(k, to_vmem):
            rows = pl.ds(k * rb, rb)
            if to_vmem:
                return pltpu.make_async_copy(w_ref.at[rows, :], stage.at[k % 2], in_sem.at[k % 2])
            return pltpu.make_async_copy(stage.at[k % 2], out_ref.at[me, rows, :], put_sem.at[k % 2])

        for k in range(R // rb):
            if k >= 2:
                own(k - 2, False).wait()
            own(k, True).start()
            own(k, True).wait()
            own(k, False).start()
        for k in range(R // rb - 2, R // rb):
            own(k, False).wait()

        def block(j, k, hc):
            return out_ref.at[slots[j], pl.ds(hc * R2 + k * rb, rb), :]

        def push(t, hc):
            j, k = divmod(t, per)
            return pltpu.make_async_remote_copy(
                src_ref=stage.at[t % 2], dst_ref=block(j, k, hc), send_sem=push_send.at[t % 2],
                recv_sem=push_recv.at[t], device_id=sibling, device_id_type=MESH)

        passed = []
        for j in range(3):
            ici(j, w_ref.at[pl.ds(c * R2, R2), :], out_ref.at[slots[j], pl.ds(c * R2, R2), :], j).wait_recv()
            for k in range(per):
                t = j * per + k
                if t >= 2:
                    push(t - 2, c).wait_send()
                fetch = pltpu.make_async_copy(block(j, k, c), stage.at[t % 2], in_sem.at[t % 2])
                fetch.start()
                fetch.wait()
                push(t, c).start()
            got = outs_ref.at[slots[j], pl.ds(c * r2, r2), :]
            ici(j, got, got, 3 + j).wait_recv()
            fw = pltpu.make_async_remote_copy(
                src_ref=got, dst_ref=got, send_sem=fwd_send.at[j], recv_sem=fwd_recv.at[j],
                device_id=sibling, device_id_type=MESH)
            fw.start()
            passed.append(fw)
        for t in range(3 * per - 2, 3 * per):
            push(t, c).wait_send()
        for t in range(3 * per):
            push(t, 1 - c).wait_recv()
        for j in range(3):
            theirs = outs_ref.at[slots[j], pl.ds((1 - c) * r2, r2), :]
            pltpu.make_async_remote_copy(
                src_ref=theirs, dst_ref=theirs, send_sem=fwd_send.at[j], recv_sem=fwd_recv.at[j],
                device_id=sibling, device_id_type=MESH).wait_recv()
        for cp in sends + passed:
            cp.wait_send()
        mine.wait()

    dma = pltpu.SemaphoreType.DMA
    return pl.pallas_call(
        body, name="gather_in", in_specs=[ANY, ANY], out_specs=[ANY, ANY],
        out_shape=[_hbm_shape((4, R, C), w.dtype), _hbm_shape((4,) + small.shape, small.dtype)],
        scratch_shapes=[pltpu.VMEM((2, rb, C), w.dtype), dma((6,)), dma((6,)), dma((3,)), dma((3,)), dma((2,)),
                        dma((3,)), dma((2,)), dma((3 * per,))],
        compiler_params=pltpu.CompilerParams(vmem_limit_bytes=VMEM_LIMIT),
    )(w, small)


def _pair_blocks(R2):
    return _pick(R2, (256, 352, 128, 64, 32, 16))


def _pair_reduce(part, *, name):
    _, R, C = part.shape
    R2 = R // 2
    rb = _pair_blocks(R2)
    nblk = R2 // rb
    steps = [(s, i) for s in range(4) for i in range(nblk)]
    n = len(steps)

    def body(part_ref, sums_ref, own_buf, oth_buf, rcv_buf, out_buf, own_sem, oth_sem, out_sem, send_sem, recv_sem):
        x, y, c, _ = _place()
        sibling = (x, y, 1 - c)

        def fetch(t, half, buf, sem):
            s, i = steps[t]
            return pltpu.make_async_copy(part_ref.at[s, pl.ds(half * R2 + i * rb, rb), :], buf.at[t % 2], sem.at[t % 2])

        def push(t):
            return pltpu.make_async_remote_copy(
                src_ref=oth_buf.at[t % 2], dst_ref=rcv_buf.at[t % 2], send_sem=send_sem.at[t % 2],
                recv_sem=recv_sem.at[t % 2], device_id=sibling, device_id_type=MESH)

        def store(t):
            s, i = steps[t]
            return pltpu.make_async_copy(out_buf.at[t % 2], sums_ref.at[s, pl.ds(i * rb, rb), :], out_sem.at[t % 2])

        fetch(0, c, own_buf, own_sem).start()
        fetch(0, 1 - c, oth_buf, oth_sem).start()
        for t in range(n):
            if t + 1 < n:
                fetch(t + 1, c, own_buf, own_sem).start()
                fetch(t + 1, 1 - c, oth_buf, oth_sem).start()
            fetch(t, 1 - c, oth_buf, oth_sem).wait()
            p = push(t)
            p.start()
            fetch(t, c, own_buf, own_sem).wait()
            p.wait()
            if t >= 2:
                store(t - 2).wait()
            out_buf[t % 2] = (own_buf[t % 2].astype(F32) + rcv_buf[t % 2].astype(F32)).astype(BF16)
            store(t).start()
        for t in range(max(n - 2, 0), n):
            store(t).wait()

    buf = pltpu.VMEM((2, rb, C), BF16)
    sem2 = pltpu.SemaphoreType.DMA((2,))
    return pl.pallas_call(
        body, name=name, in_specs=[ANY], out_specs=ANY, out_shape=_hbm_shape((4, R2, C), BF16),
        scratch_shapes=[buf, buf, buf, buf, sem2, sem2, sem2, sem2, sem2],
        compiler_params=pltpu.CompilerParams(vmem_limit_bytes=VMEM_LIMIT),
    )(part)


def _chip_exchange(sums, *, name, collective_id=None):
    n = len(sums)

    def body(*refs):
        ins, outs = refs[:n], refs[n:2 * n]
        send_sem, recv_sem, local_sem = refs[2 * n:]
        x, y, c, chips = _place()
        me = 2 * x + y
        if collective_id is not None:
            _handshake([(*chip, c) for chip in chips])
        cps = []
        for w in range(n):
            lc = pltpu.make_async_copy(ins[w].at[me], outs[w].at[me], local_sem.at[w])
            lc.start()
            cps.append(lc)
            for j, chip in enumerate(chips):
                slot = 2 * chip[0] + chip[1]
                rc = pltpu.make_async_remote_copy(
                    src_ref=ins[w].at[slot], dst_ref=outs[w].at[me],
                    send_sem=send_sem.at[w * 3 + j], recv_sem=recv_sem.at[w * 3 + j],
                    device_id=(*chip, c), device_id_type=MESH)
                rc.start()
                cps.append(rc)
        for w in range(n):
            for j, chip in enumerate(chips):
                slot = 2 * chip[0] + chip[1]
                pltpu.make_async_remote_copy(
                    src_ref=ins[w].at[slot], dst_ref=outs[w].at[slot],
                    send_sem=send_sem.at[w * 3 + j], recv_sem=recv_sem.at[w * 3 + j],
                    device_id=(*chip, c), device_id_type=MESH).wait_recv()
        for w in range(n):
            cps[w * 4].wait()
            for j in range(3):
                cps[w * 4 + 1 + j].wait_send()

    return _launch_copies(
        body, sums, [_hbm_shape(s.shape, s.dtype) for s in sums],
        [pltpu.SemaphoreType.DMA((3 * n,)), pltpu.SemaphoreType.DMA((3 * n,)), pltpu.SemaphoreType.DMA((n,))],
        name=name, collective_id=collective_id)


def _sum_join(recv, *, name):
    _, R2, C = recv.shape
    rb = _pair_blocks(R2)
    nblk = R2 // rb

    def body(recv_ref, out_ref, in_buf, acc_buf, in_sem, loc_sem, send_sem, recv_sem):
        x, y, c, _ = _place()
        sibling = (x, y, 1 - c)

        def fetch(t):
            return pltpu.make_async_copy(recv_ref.at[:, pl.ds(t * rb, rb), :], in_buf.at[t % 2], in_sem.at[t % 2])

        def rows(t, half):
            return out_ref.at[pl.ds(half * R2 + t * rb, rb), :]

        def put_local(t):
            return pltpu.make_async_copy(acc_buf.at[t % 2], rows(t, c), loc_sem.at[t])

        def put_remote(t, half):
            return pltpu.make_async_remote_copy(
                src_ref=acc_buf.at[t % 2], dst_ref=rows(t, half), send_sem=send_sem.at[t], recv_sem=recv_sem.at[t],
                device_id=sibling, device_id_type=MESH)

        fetch(0).start()
        for t in range(nblk):
            if t + 1 < nblk:
                fetch(t + 1).start()
            fetch(t).wait()
            if t >= 2:
                put_local(t - 2).wait()
                put_remote(t - 2, c).wait_send()
            acc = in_buf[t % 2, 0].astype(F32)
            for s in range(1, 4):
                acc = acc + in_buf[t % 2, s].astype(F32)
            acc_buf[t % 2] = acc
            put_local(t).start()
            put_remote(t, c).start()
        for t in range(max(nblk - 2, 0), nblk):
            put_local(t).wait()
            put_remote(t, c).wait_send()
        for t in range(nblk):
            put_remote(t, 1 - c).wait_recv()

    semn = pltpu.SemaphoreType.DMA((nblk,))
    return pl.pallas_call(
        body, name=name, in_specs=[ANY], out_specs=ANY, out_shape=_hbm_shape((2 * R2, C), F32),
        scratch_shapes=[pltpu.VMEM((2, 4, rb, C), BF16), pltpu.VMEM((2, rb, C), F32),
                        pltpu.SemaphoreType.DMA((2,)), semn, semn, semn],
        compiler_params=pltpu.CompilerParams(vmem_limit_bytes=VMEM_LIMIT),
    )(recv)


def _sum_small(pack):
    R, C = pack.shape

    def body(in_ref, out_ref, all_ref, send_sem, recv_sem):
        x, y, c, _ = _place()
        me = 4 * x + 2 * y + c
        all_ref[me] = in_ref[...]
        cps = []
        for k in range(1, 8):
            peer = (x ^ (k >> 2), y ^ ((k >> 1) & 1), c ^ (k & 1))
            cp = pltpu.make_async_remote_copy(
                src_ref=in_ref, dst_ref=all_ref.at[me], send_sem=send_sem.at[k - 1], recv_sem=recv_sem.at[k - 1],
                device_id=peer, device_id_type=MESH)
            cp.start()
            cps.append(cp)
        for k in range(1, 8):
            peer = (x ^ (k >> 2), y ^ ((k >> 1) & 1), c ^ (k & 1))
            slot = 4 * peer[0] + 2 * peer[1] + peer[2]
            pltpu.make_async_remote_copy(
                src_ref=in_ref, dst_ref=all_ref.at[slot], send_sem=send_sem.at[k - 1], recv_sem=recv_sem.at[k - 1],
                device_id=peer, device_id_type=MESH).wait_recv()
        for cp in cps:
            cp.wait_send()
        acc = all_ref[0]
        for d in range(1, 8):
            acc = acc + all_ref[d]
        out_ref[...] = acc

    return pl.pallas_call(
        body, name="sum_small", in_specs=[pl.BlockSpec(memory_space=pltpu.VMEM)],
        out_specs=pl.BlockSpec(memory_space=pltpu.VMEM), out_shape=jax.ShapeDtypeStruct((R, C), F32),
        scratch_shapes=[pltpu.VMEM((8, R, C), F32), pltpu.SemaphoreType.DMA((7,)), pltpu.SemaphoreType.DMA((7,))],
    )(pack)


ADAMW_BLOCK_BYTES = 2 * 1024 * 1024


def _adamw_block(R, C):
    padded = -(-C // 128) * 128
    rows = [rb for rb in range(8, R + 1, 8) if R % rb == 0 and rb * padded * 4 <= ADAMW_BLOCK_BYTES]
    if rows or R * padded * 4 <= ADAMW_BLOCK_BYTES:
        return (max(rows) if rows else R), C
    cols = [cb for cb in range(128, C + 1, 128) if C % cb == 0 and R * cb * 4 <= ADAMW_BLOCK_BYTES]
    return R, max(cols)


def _adamw(w, g, m, v, *, name):
    R, C = w.shape
    rb, cb = _adamw_block(R, C)
    c1 = 1.0 / (1.0 - B1 ** STEP)
    c2 = 1.0 / (1.0 - B2 ** STEP)

    def body(w_ref, g_ref, m_ref, v_ref, d_ref, nm_ref, nv_ref):
        gv = g_ref[...]
        m2 = B1 * m_ref[...] + (1.0 - B1) * gv
        v2 = B2 * v_ref[...] + (1.0 - B2) * gv * gv
        nm_ref[...] = m2
        nv_ref[...] = v2
        d_ref[...] = -LR * ((m2 * c1) / (jnp.sqrt(v2 * c2) + ADAM_EPS) + WD * w_ref[...])

    blk = pl.BlockSpec((rb, cb), lambda i, j: (i, j))
    return pl.pallas_call(
        body, name=name, grid=(R // rb, C // cb), in_specs=[blk] * 4, out_specs=[blk] * 3,
        out_shape=[jax.ShapeDtypeStruct((R, C), F32)] * 3, compiler_params=_params(2),
    )(w, g, m, v)


SMALL = (("norm1_w", D), ("norm2_w", D), ("final_norm_w", D), ("gate_b", GLA_H * DK), ("gla_norm_w", DV), ("sinks", N_Q))
PACK_W = 1024


def _pack_small(vals, w2, loss):
    rows = []
    for name, width in SMALL:
        v = vals[name].reshape(-1)
        rows.append(jnp.pad(v, (0, (-width) % PACK_W)).reshape(-1, PACK_W))
    rows.append(w2)
    rows.append(jnp.broadcast_to(loss.reshape(1, 1), (1, PACK_W)))
    pack = jnp.concatenate(rows, axis=0)
    return jnp.pad(pack, ((0, 32 - pack.shape[0]), (0, 0)))


def _unpack_small(pack):
    out, r = {}, 0
    for name, width in SMALL:
        nr = -(-width // PACK_W)
        out[name] = pack[r:r + nr].reshape(-1)[:width]
        r += nr
    out["w2"] = pack[r:r + RANK]
    out["loss"] = pack[r + RANK, 0]
    return out


def kernel(x, norm1_w, w_in, gla_gate_w2, gla_gate_b, attn_sinks, gla_norm_w, w_out, norm2_w, w_ffn_gate, w_ffn_up, w_ffn_down, final_norm_w, loss_target, m_norm1_w, m_w_in, m_gla_gate_w2, m_gla_gate_b, m_attn_sinks, m_gla_norm_w, m_w_out, m_norm2_w, m_w_ffn_gate, m_w_ffn_up, m_w_ffn_down, m_final_norm_w, v_norm1_w, v_w_in, v_gla_gate_w2, v_gla_gate_b, v_attn_sinks, v_gla_norm_w, v_w_out, v_norm2_w, v_w_ffn_gate, v_w_ffn_up, v_w_ffn_down, v_final_norm_w):
    chip = 2 * lax.axis_index("x") + lax.axis_index("y")
    w_in_s, w_out_s, w_gate_s, w_up_s, w_down_s, w2_s = (
        w_in[0], w_out[0], w_ffn_gate[0], w_ffn_up[0], w_ffn_down[0], gla_gate_w2[0])
    CS = SHARD_COLS

    g_in, g_w2 = _gather_staged(w_in_s.astype(BF16), w2_s)
    g_out, = _gather_shards([w_out_s.astype(BF16)], name="gather_out", collective_id=1, after=[g_w2])
    g_gate, g_up = _gather_shards([w_gate_s.astype(BF16), w_up_s.astype(BF16)], name="gather_gate_up",
                                  collective_id=7, after=[g_w2])
    g_down, = _gather_shards([w_down_s.astype(BF16)], name="gather_down", collective_id=8, after=[g_w2])
    w_main, w_lr = _w_in_to_main(g_in)
    w2_full = jnp.transpose(g_w2, (1, 0, 2)).reshape(RANK, GLA_H * DK)
    p = dict(
        norm1_w=norm1_w, norm2_w=norm2_w, final_norm_w=final_norm_w.reshape(1, D), gate_b=gla_gate_b,
        gla_norm_w=gla_norm_w, sinks=jnp.pad(attn_sinks, ((0, 0), (0, 128 - N_Q))),
        w_in_main=w_main, w_in_lr=w_lr,
        w2p=jnp.pad(w2_full, ((0, LRP - RANK), (0, 0))).astype(BF16),
        w_out=g_out.reshape(D, D),
        w_gate=g_gate, w_up=g_up, w_down=g_down.reshape(FFN, D),
    )

    tags = ["w_in", "w_out", "w_gate", "w_up", "w_down"]
    recv = {}

    def reduce_part(tag, part):
        sums = _pair_reduce(part, name="pair_reduce_" + tag)
        recv[tag], = _chip_exchange([sums], name="chip_exchange_" + tag, collective_id=2 + tags.index(tag))

    loss_blk, dx, g = _local_step(x[0], loss_target[0], p, reduce_part)
    reduce_part("w_in", _main_to_shards(g["w_in_main"], g["w_in_lr"]))
    big = [_sum_join(recv[tg], name="sum_join_" + tg) for tg in tags]

    small = _unpack_small(_sum_small(_pack_small(g, g["w2"], loss_blk[0, 0])))
    loss = small["loss"]
    g_w2_mine = lax.dynamic_slice_in_dim(small["w2"], chip * (GLA_H * DK // 4), GLA_H * DK // 4, axis=1)

    grads = dict(
        norm1_w=small["norm1_w"].reshape(1, D), w_in=big[0], gla_gate_w2=g_w2_mine,
        gla_gate_b=small["gate_b"].reshape(1, -1), attn_sinks=small["sinks"].reshape(1, N_Q),
        gla_norm_w=small["gla_norm_w"].reshape(1, DV), w_out=big[1], norm2_w=small["norm2_w"].reshape(1, D),
        w_ffn_gate=big[2], w_ffn_up=big[3], w_ffn_down=big[4], final_norm_w=small["final_norm_w"].reshape(1, D))
    def lin(a):
        return jnp.transpose(a.reshape(D, CS))

    def unlin(a):
        return jnp.transpose(a)

    grads["w_in"] = lin(big[0])
    weights = dict(
        norm1_w=(norm1_w, m_norm1_w, v_norm1_w), w_in=(lin(w_in), lin(m_w_in), lin(v_w_in)),
        gla_gate_w2=(w2_s, m_gla_gate_w2[0], v_gla_gate_w2[0]), gla_gate_b=(gla_gate_b, m_gla_gate_b, v_gla_gate_b),
        attn_sinks=(attn_sinks, m_attn_sinks, v_attn_sinks), gla_norm_w=(gla_norm_w, m_gla_norm_w, v_gla_norm_w),
        w_out=(w_out_s, m_w_out[0], v_w_out[0]), norm2_w=(norm2_w, m_norm2_w, v_norm2_w),
        w_ffn_gate=(w_gate_s, m_w_ffn_gate[0], v_w_ffn_gate[0]), w_ffn_up=(w_up_s, m_w_ffn_up[0], v_w_ffn_up[0]),
        w_ffn_down=(w_down_s, m_w_ffn_down[0], v_w_ffn_down[0]),
        final_norm_w=(final_norm_w.reshape(1, D), m_final_norm_w.reshape(1, D), v_final_norm_w.reshape(1, D)))
    names = ["norm1_w", "w_in", "gla_gate_w2", "gla_gate_b", "attn_sinks", "gla_norm_w", "w_out", "norm2_w",
             "w_ffn_gate", "w_ffn_up", "w_ffn_down", "final_norm_w"]
    lead = {"norm1_w": False, "gla_gate_b": False, "attn_sinks": False, "gla_norm_w": False, "norm2_w": False}
    g_out_l, d_out, m_out, v_out = [], [], [], []
    for nm in names:
        w, m, v = weights[nm]
        gr = grads[nm]
        dl, nmn, nvn = _adamw(w, gr, m, v, name="adamw_" + nm)
        if nm == "w_in":
            gr, dl, nmn, nvn, w = big[0], unlin(dl), unlin(nmn), unlin(nvn), w_in_s
        if nm == "final_norm_w":
            shape = (D,)
        elif nm in lead:
            shape = w.shape
        else:
            shape = (1,) + w.shape
        g_out_l.append(gr.reshape(shape))
        d_out.append(dl.reshape(shape))
        m_out.append(nmn.reshape(shape))
        v_out.append(nvn.reshape(shape))
    return (loss, dx[None], *g_out_l, *d_out, *m_out, *v_out)
```

```python
import functools

import jax
import jax.numpy as jnp
from jax import lax
from jax.experimental import pallas as pl
from jax.experimental.pallas import tpu as pltpu
from jax.experimental.pallas import tpu_sc as plsc

F32 = jnp.float32
BF16 = jnp.bfloat16

D = 2048
HEAD_DIM = 64
N_Q = 32
N_KV = 4
GROUP = 8
WINDOW = 128
GLA_H = 4
DK = 256
DV = 512
RANK = 16
CHUNK = 64
FFN = 5632
EPS = 1e-6
MASK_VALUE = -1e30
GATE_NORM = 16.0
LR, B1, B2, ADAM_EPS, WD, STEP = 0.001, 0.9, 0.999, 1e-08, 0.01, 10

MAIN = 12800
O_MERGE, O_GLA, O_ATTN = 0, 6144, 10240
W_MERGE, W_GLA, W_ATTN = 3 * DV, 2 * DK + DV, D + 2 * N_KV * HEAD_DIM
LR_AT = 6656
LRP = 128


def _main_pieces():
    o_gq, o_gk, o_gv, o_gr, o_ga, o_gb = 2560, 3584, 4608, LR_AT + RANK, LR_AT + RANK + D, LR_AT + RANK + 2 * D
    pieces = []
    for h in range(GLA_H):
        pieces += [(o_gr + DV * h, DV), (o_ga + DV * h, DV), (o_gb + DV * h, DV)]
    for h in range(GLA_H):
        pieces += [(o_gq + DK * h, DK), (o_gk + DK * h, DK), (o_gv + DV * h, DV)]
    pieces.append((0, W_ATTN))
    return pieces


SHARD_COLS = 3204

VMEM_LIMIT = 56 * 1024 * 1024
MESH = pl.DeviceIdType.MESH
ANY = pl.BlockSpec(memory_space=pl.ANY)


def _params(ngrid):
    return pltpu.CompilerParams(dimension_semantics=("arbitrary",) * ngrid, vmem_limit_bytes=VMEM_LIMIT)


def _shard_segments(lo, hi):
    segs = []
    while lo < hi:
        s = lo // SHARD_COLS
        e = min(hi, (s + 1) * SHARD_COLS)
        segs.append((s, lo - s * SHARD_COLS, e - s * SHARD_COLS))
        lo = e
    return segs


def _lanes(pieces):
    return pieces[0] if len(pieces) == 1 else jnp.concatenate(pieces, axis=1)


def _w_in_to_main(shards):
    rb = 256

    def body(g_ref, main_ref, lr_ref):
        at = 0
        for a, w in _main_pieces():
            main_ref[:, at:at + w] = _lanes([g_ref[s, :, lo:hi] for s, lo, hi in _shard_segments(a, a + w)])
            at += w
        lr = [g_ref[s, :, lo:hi] for s, lo, hi in _shard_segments(LR_AT, LR_AT + RANK)]
        lr_ref[...] = _lanes(lr + [jnp.zeros((rb, LRP - RANK), lr_ref.dtype)])

    return pl.pallas_call(
        body, name="w_in_to_main", grid=(D // rb,),
        in_specs=[pl.BlockSpec((4, rb, SHARD_COLS), lambda i: (0, i, 0))],
        out_specs=[pl.BlockSpec((rb, MAIN), lambda i: (i, 0)), pl.BlockSpec((rb, LRP), lambda i: (i, 0))],
        out_shape=[jax.ShapeDtypeStruct((D, MAIN), shards.dtype), jax.ShapeDtypeStruct((D, LRP), shards.dtype)],
        compiler_params=_params(1),
    )(shards)


def _main_to_shards(g_main, g_lr):
    rb = 256
    at, sources = 0, [(LR_AT, RANK, 1, 0)]
    for a, w in _main_pieces():
        sources.append((a, w, 0, at))
        at += w
    sources.sort()

    def body(main_ref, lr_ref, out_ref):
        refs = (main_ref, lr_ref)
        for s in range(4):
            lo, hi = s * SHARD_COLS, (s + 1) * SHARD_COLS
            pieces = []
            for a, w, which, src in sources:
                b, e = max(a, lo), min(a + w, hi)
                if b < e:
                    pieces.append(refs[which][:, src + b - a:src + e - a])
            out_ref[s] = _lanes(pieces)

    return pl.pallas_call(
        body, name="main_to_shards", grid=(D // rb,),
        in_specs=[pl.BlockSpec((rb, MAIN), lambda i: (i, 0)), pl.BlockSpec((rb, LRP), lambda i: (i, 0))],
        out_specs=pl.BlockSpec((4, rb, SHARD_COLS), lambda i: (0, i, 0)),
        out_shape=jax.ShapeDtypeStruct((4, D, SHARD_COLS), g_main.dtype), compiler_params=_params(1),
    )(g_main, g_lr)


def _pick(n, cands):
    for c in cands:
        if n % c == 0:
            return c
    return n


def _sigmoid(x):
    return 1.0 / (1.0 + jnp.exp(-x))


def _dot(a, b, ca, cb):
    return lax.dot_general(a, b, (((ca,), (cb,)), ((), ())), preferred_element_type=F32)


def _matmul(a, b, *, name, ta=False, tb=False, add=None, out_dtype=F32, bm=None, bj=None, bc=None,
            b_slots=False, out_slots=False):
    C, M = a.shape if ta else a.shape[::-1]
    if b_slots:
        if tb:
            J, bc = b.shape[1], b.shape[2]
            assert b.shape[0] * bc == C
        else:
            bj = b.shape[2]
            J = b.shape[0] * bj
            assert b.shape[1] == C
    else:
        J = b.shape[0] if tb else b.shape[1]
        assert (b.shape[1] if tb else b.shape[0]) == C
    bm = bm or _pick(M, (1024, 512, 256, 128) if C <= FFN else (512, 256, 128))
    bj = bj or _pick(J, (1280, 1024, 512, 256, 128) if C <= D else (512, 256, 128))
    bc = bc or (C if C <= FFN else _pick(C, (6400,)))
    nm, nj, nc = M // bm, J // bj, C // bc
    has_add = add is not None

    def body(*refs):
        a_ref, b_ref = refs[0], refs[1]
        add_ref = refs[2] if has_add else None
        o_ref = refs[3] if has_add else refs[2]
        p = _dot(a_ref[...].astype(BF16), b_ref[...].astype(BF16), 0 if ta else 1, 1 if tb else 0)

        def finish(acc):
            if has_add:
                acc = acc + add_ref[...]
            o_ref[...] = acc.astype(o_ref.dtype)

        if nc == 1:
            finish(p)
        else:
            acc_ref = refs[-1]
            c = pl.program_id(2)

            @pl.when(c == 0)
            def _():
                acc_ref[...] = p

            @pl.when(c > 0)
            def _():
                acc_ref[...] += p

            @pl.when(c == nc - 1)
            def _():
                finish(acc_ref[...])

    a_spec = pl.BlockSpec((bc, bm), lambda m, j, c: (c, m)) if ta else pl.BlockSpec((bm, bc), lambda m, j, c: (m, c))
    if b_slots:
        b_spec = (pl.BlockSpec((None, bj, bc), lambda m, j, c: (c, j, 0)) if tb
                  else pl.BlockSpec((None, bc, bj), lambda m, j, c: (j, c, 0)))
    else:
        b_spec = (pl.BlockSpec((bj, bc), lambda m, j, c: (j, c)) if tb
                  else pl.BlockSpec((bc, bj), lambda m, j, c: (c, j)))
    if out_slots:
        assert not has_add
        o_spec = pl.BlockSpec((None, bm, bj), lambda m, j, c: (j, m, 0))
        out_shape = jax.ShapeDtypeStruct((nj, M, bj), out_dtype)
    else:
        o_spec = pl.BlockSpec((bm, bj), lambda m, j, c: (m, j))
        out_shape = jax.ShapeDtypeStruct((M, J), out_dtype)
    in_specs = [a_spec, b_spec] + ([o_spec] if has_add else [])
    args = (a, b) + ((add,) if has_add else ())
    return pl.pallas_call(
        body, name=name, grid=(nm, nj, nc), in_specs=in_specs, out_specs=o_spec,
        out_shape=out_shape,
        scratch_shapes=[pltpu.VMEM((bm, bj), F32)] if nc > 1 else [],
        compiler_params=_params(3),
    )(*args)


def _rows(T):
    return _pick(T, (256, 128))


def _rms_fwd(x, w, *, name):
    T = x.shape[0]
    rb = _rows(T)

    def body(x_ref, w_ref, u_ref, ut_ref):
        xv = x_ref[...]
        r = lax.rsqrt(jnp.mean(xv * xv, axis=-1, keepdims=True) + EPS)
        u = xv * r * w_ref[...]
        u_ref[...] = u.astype(BF16)
        ut_ref[...] = u.T.astype(BF16)

    return pl.pallas_call(
        body, name=name, grid=(T // rb,),
        in_specs=[pl.BlockSpec((rb, D), lambda i: (i, 0)), pl.BlockSpec((1, D), lambda i: (0, 0))],
        out_specs=[pl.BlockSpec((rb, D), lambda i: (i, 0)), pl.BlockSpec((D, rb), lambda i: (0, i))],
        out_shape=[jax.ShapeDtypeStruct((T, D), BF16), jax.ShapeDtypeStruct((D, T), BF16)], compiler_params=_params(1),
    )(x, w)


def _rms_bwd(dy, h, w, resid, *, name):
    T = h.shape[0]
    rb = _rows(T)

    def body(dy_ref, h_ref, w_ref, res_ref, dh_ref, dhb_ref, dw_ref):
        @pl.when(pl.program_id(0) == 0)
        def _():
            dw_ref[...] = jnp.zeros_like(dw_ref)

        hv = h_ref[...]
        r = lax.rsqrt(jnp.mean(hv * hv, axis=-1, keepdims=True) + EPS)
        hn = hv * r
        dyv = dy_ref[...]
        dw_ref[...] += jnp.sum(dyv * hn, axis=0, keepdims=True)
        t = dyv * w_ref[...]
        dh = res_ref[...] + r * (t - hn * jnp.mean(t * hn, axis=-1, keepdims=True))
        dh_ref[...] = dh
        dhb_ref[...] = dh.astype(BF16)

    blk = pl.BlockSpec((rb, D), lambda i: (i, 0))
    vec = pl.BlockSpec((1, D), lambda i: (0, 0))
    return pl.pallas_call(
        body, name=name, grid=(T // rb,), in_specs=[blk, blk, vec, blk], out_specs=[blk, blk, vec],
        out_shape=[jax.ShapeDtypeStruct((T, D), F32), jax.ShapeDtypeStruct((T, D), BF16),
                   jax.ShapeDtypeStruct((1, D), F32)],
        compiler_params=_params(1),
    )(dy, h, w, resid)


def _final_loss(h, w, target):
    T = h.shape[0]
    rb = _rows(T)

    def body(h_ref, w_ref, t_ref, loss_ref, dh_ref, dhb_ref, dw_ref):
        @pl.when(pl.program_id(0) == 0)
        def _():
            dw_ref[...] = jnp.zeros_like(dw_ref)
            loss_ref[...] = jnp.zeros_like(loss_ref)

        hv = h_ref[...]
        wv = w_ref[...]
        r = lax.rsqrt(jnp.mean(hv * hv, axis=-1, keepdims=True) + EPS)
        hn = hv * r
        e = hn * wv - t_ref[...]
        row = jnp.sum(e * e, axis=-1, keepdims=True) * (0.5 / D)
        loss_ref[...] += jnp.broadcast_to(jnp.sum(row, axis=0, keepdims=True), loss_ref.shape)
        dy = e * (1.0 / D)
        dw_ref[...] += jnp.sum(dy * hn, axis=0, keepdims=True)
        t = dy * wv
        dh = r * (t - hn * jnp.mean(t * hn, axis=-1, keepdims=True))
        dh_ref[...] = dh
        dhb_ref[...] = dh.astype(BF16)

    blk = pl.BlockSpec((rb, D), lambda i: (i, 0))
    vec = pl.BlockSpec((1, D), lambda i: (0, 0))
    return pl.pallas_call(
        body, name="final_loss", grid=(T // rb,), in_specs=[blk, vec, blk],
        out_specs=[pl.BlockSpec((8, 128), lambda i: (0, 0)), blk, blk, vec],
        out_shape=[jax.ShapeDtypeStruct((8, 128), F32), jax.ShapeDtypeStruct((T, D), F32),
                   jax.ShapeDtypeStruct((T, D), BF16), jax.ShapeDtypeStruct((1, D), F32)],
        compiler_params=_params(1),
    )(h, w, target)


PANEL = FFN // 4


def _ffn_fwd(v2, w_gate, w_up):
    T = v2.shape[0]
    bm = _pick(T, (512, 256))

    def body(a_ref, bg_ref, bu_ref, gate_ref, up_ref, ff_ref, fft_ref):
        a = a_ref[...]
        g = _dot(a, bg_ref[...], 1, 0)
        u = _dot(a, bu_ref[...], 1, 0)
        gate_ref[...] = g
        up_ref[...] = u
        ff = g * _sigmoid(g) * u
        ff_ref[...] = ff.astype(BF16)
        fft_ref[...] = ff.T.astype(BF16)

    panel = pl.BlockSpec((None, D, PANEL), lambda j, m: (j, 0, 0))
    out = pl.BlockSpec((bm, PANEL), lambda j, m: (m, j))
    return pl.pallas_call(
        body, name="ffn_fwd", grid=(4, T // bm),
        in_specs=[pl.BlockSpec((bm, D), lambda j, m: (m, 0)), panel, panel],
        out_specs=[out, out, out, pl.BlockSpec((PANEL, bm), lambda j, m: (j, m))],
        out_shape=[jax.ShapeDtypeStruct((T, FFN), F32), jax.ShapeDtypeStruct((T, FFN), F32),
                   jax.ShapeDtypeStruct((T, FFN), BF16), jax.ShapeDtypeStruct((FFN, T), BF16)],
        compiler_params=_params(2),
    )(v2, w_gate, w_up)


def _ffn_bwd_hidden(dh2, w_down, gate, up):
    T = dh2.shape[0]
    bm = _pick(T, (512, 256))

    def body(a_ref, b_ref, g_ref, u_ref, dg_ref, du_ref):
        d = _dot(a_ref[...], b_ref[...], 1, 1)
        g = g_ref[...]
        sg = _sigmoid(g)
        du_ref[...] = (d * g * sg).astype(BF16)
        dg_ref[...] = (d * u_ref[...] * sg * (1.0 + g * (1.0 - sg))).astype(BF16)

    blk = pl.BlockSpec((bm, PANEL), lambda j, m: (m, j))
    return pl.pallas_call(
        body, name="ffn_bwd_hidden", grid=(4, T // bm),
        in_specs=[pl.BlockSpec((bm, D), lambda j, m: (m, 0)), pl.BlockSpec((PANEL, D), lambda j, m: (j, 0)), blk, blk],
        out_specs=[blk, blk], out_shape=[jax.ShapeDtypeStruct((T, FFN), BF16)] * 2, compiler_params=_params(2),
    )(dh2, w_down, gate, up)


def _ffn_bwd_input(dgate, dup, w_gate, w_up):
    T = dgate.shape[0]
    bm = _pick(T, (1024, 512, 256))
    bj = 1024

    def body(ag_ref, au_ref, bg_ref, bu_ref, o_ref, acc_ref):
        c = pl.program_id(2)
        p = _dot(ag_ref[...], bg_ref[...], 1, 1) + _dot(au_ref[...], bu_ref[...], 1, 1)

        @pl.when(c == 0)
        def _():
            acc_ref[...] = p

        @pl.when(c > 0)
        def _():
            acc_ref[...] += p

        @pl.when(c == 3)
        def _():
            o_ref[...] = acc_ref[...]

    a = pl.BlockSpec((bm, PANEL), lambda m, j, c: (m, c))
    b = pl.BlockSpec((None, bj, PANEL), lambda m, j, c: (c, j, 0))
    return pl.pallas_call(
        body, name="ffn_bwd_input", grid=(T // bm, D // bj, 4), in_specs=[a, a, b, b],
        out_specs=pl.BlockSpec((bm, bj), lambda m, j, c: (m, j)), out_shape=jax.ShapeDtypeStruct((T, D), F32),
        scratch_shapes=[pltpu.VMEM((bm, bj), F32)], compiler_params=_params(3),
    )(dgate, dup, w_gate, w_up)


def _colsum(x, *, name):
    T, W = x.shape
    rb = _rows(T)

    def body(x_ref, o_ref):
        @pl.when(pl.program_id(0) == 0)
        def _():
            o_ref[...] = jnp.zeros_like(o_ref)

        o_ref[...] += jnp.sum(x_ref[...], axis=0, keepdims=True)

    return pl.pallas_call(
        body, name=name, grid=(T // rb,), in_specs=[pl.BlockSpec((rb, W), lambda i: (i, 0))],
        out_specs=pl.BlockSpec((1, W), lambda i: (0, 0)), out_shape=jax.ShapeDtypeStruct((1, W), F32),
        compiler_params=_params(1),
    )(x)


def _merge_fwd(attn_o, gla_raw, proj, gla_norm_w):
    T = attn_o.shape[0]
    rb = _rows(T)

    def body(a_ref, g_ref, gr_ref, ga_ref, gb_ref, w_ref, o_ref, ot_ref):
        graw = g_ref[...]
        r = lax.rsqrt(jnp.mean(graw * graw, axis=-1, keepdims=True) + EPS)
        gr = gr_ref[...]
        go = graw * r * w_ref[...] * (gr * _sigmoid(gr))
        merged = _sigmoid(ga_ref[...]) * a_ref[...] + _sigmoid(gb_ref[...]) * go
        o_ref[...] = merged.astype(BF16)
        ot_ref[...] = merged.T.astype(BF16)

    def sec(k):
        return pl.BlockSpec((rb, DV), lambda i, h: (i, O_MERGE // DV + 3 * h + k))

    blk = pl.BlockSpec((rb, DV), lambda i, h: (i, h))
    return pl.pallas_call(
        body, name="merge_fwd", grid=(T // rb, GLA_H),
        in_specs=[blk, blk, sec(0), sec(1), sec(2), pl.BlockSpec((1, DV), lambda i, h: (0, 0))],
        out_specs=[blk, pl.BlockSpec((DV, rb), lambda i, h: (h, i))],
        out_shape=[jax.ShapeDtypeStruct((T, D), BF16), jax.ShapeDtypeStruct((D, T), BF16)], compiler_params=_params(2),
    )(attn_o, gla_raw, proj, proj, proj, gla_norm_w)


def _merge_bwd(dm, attn_o, gla_raw, proj, gla_norm_w):
    T = attn_o.shape[0]
    rb = _rows(T)

    def body(dm_ref, a_ref, g_ref, gr_ref, ga_ref, gb_ref, w_ref, da_ref, dg_ref, dp_ref, dw_ref):
        @pl.when((pl.program_id(0) == 0) & (pl.program_id(1) == 0))
        def _():
            dw_ref[...] = jnp.zeros_like(dw_ref)

        dmv = dm_ref[...]
        av = a_ref[...]
        graw = g_ref[...]
        gr = gr_ref[...]
        wv = w_ref[...]
        sa = _sigmoid(ga_ref[...])
        sb = _sigmoid(gb_ref[...])
        r = lax.rsqrt(jnp.mean(graw * graw, axis=-1, keepdims=True) + EPS)
        gnh = graw * r
        gn = gnh * wv
        sr = _sigmoid(gr)
        sl = gr * sr
        go = gn * sl
        da_ref[...] = dmv * sa
        dgo = dmv * sb
        dp_ref[:, 0:DV] = (dgo * gn * sr * (1.0 + gr * (1.0 - sr))).astype(BF16)
        dp_ref[:, DV:2 * DV] = (dmv * av * sa * (1.0 - sa)).astype(BF16)
        dp_ref[:, 2 * DV:3 * DV] = (dmv * go * sb * (1.0 - sb)).astype(BF16)
        dgn = dgo * sl
        dw_ref[...] += jnp.sum(dgn * gnh, axis=0, keepdims=True)
        t = dgn * wv
        dg_ref[...] = r * (t - gnh * jnp.mean(t * gnh, axis=-1, keepdims=True))

    def sec(k):
        return pl.BlockSpec((rb, DV), lambda i, h: (i, O_MERGE // DV + 3 * h + k))

    blk = pl.BlockSpec((rb, DV), lambda i, h: (i, h))
    vec = pl.BlockSpec((1, DV), lambda i, h: (0, 0))
    return pl.pallas_call(
        body, name="merge_bwd", grid=(T // rb, GLA_H),
        in_specs=[blk, blk, blk, sec(0), sec(1), sec(2), vec],
        out_specs=[blk, blk, pl.BlockSpec((rb, W_MERGE), lambda i, h: (i, O_MERGE // W_MERGE + h)), vec],
        out_shape=[jax.ShapeDtypeStruct((T, D), F32), jax.ShapeDtypeStruct((T, D), F32),
                   jax.ShapeDtypeStruct((T, MAIN), BF16), jax.ShapeDtypeStruct((1, DV), F32)],
        compiler_params=_params(2),
    )(dm, attn_o, gla_raw, proj, proj, proj, gla_norm_w)


def _attn_band(n):
    qi = lax.broadcasted_iota(jnp.int32, (GROUP * WINDOW, WINDOW), 0) & (WINDOW - 1)
    kj = lax.broadcasted_iota(jnp.int32, (GROUP * WINDOW, WINDOW), 1)
    cur = kj <= qi
    return cur, cur | (n > 0)


def _stack_heads(ref, h, scale=1.0):
    rows = jnp.concatenate(
        [ref[:, (h * GROUP + g) * HEAD_DIM:(h * GROUP + g + 1) * HEAD_DIM] for g in range(GROUP)], axis=0)
    return (rows * scale).astype(BF16) if scale != 1.0 else rows.astype(BF16)


def _stack_sinks(s_ref, h):
    return jnp.concatenate(
        [jnp.broadcast_to(s_ref[:, h * GROUP + g:h * GROUP + g + 1], (WINDOW, 1)) for g in range(GROUP)], axis=0)


def _attn_probs(qs, kp, kc, band, sink):
    cur, live = band
    s = jnp.where(cur, _dot(qs, kc, 1, 1), _dot(qs, kp, 1, 1))
    s = jnp.where(live, s, MASK_VALUE)
    m = jnp.maximum(jnp.max(s, axis=-1, keepdims=True), sink)
    e = jnp.exp(s - m)
    es = jnp.exp(sink - m)
    inv = 1.0 / (jnp.sum(e, axis=-1, keepdims=True) + es)
    return e * inv, es * inv


def _unfold(cur, x):
    mine = jnp.where(cur, x, 0.0)
    return mine.astype(BF16), (x - mine).astype(BF16)


def _attn_specs(nb, rev):
    def at(n):
        return (nb - 1 - n) if rev else n

    kcol, vcol = (O_ATTN + D) // 256, (O_ATTN + D) // 256 + 1
    q = pl.BlockSpec((WINDOW, D), lambda n: (at(n), O_ATTN // D))
    kc = pl.BlockSpec((WINDOW, 256), lambda n: (at(n), kcol))
    kp = pl.BlockSpec((WINDOW, 256), lambda n: (jnp.maximum(at(n) - 1, 0), kcol))
    vc = pl.BlockSpec((WINDOW, 256), lambda n: (at(n), vcol))
    vp = pl.BlockSpec((WINDOW, 256), lambda n: (jnp.maximum(at(n) - 1, 0), vcol))
    sk = pl.BlockSpec((1, 128), lambda n: (0, 0))
    o = pl.BlockSpec((WINDOW, D), lambda n: (at(n), 0))
    return q, kc, kp, vc, vp, sk, o


def _attn_fwd(proj, sinks):
    T = proj.shape[0]
    nb = T // WINDOW

    def body(q_ref, kc_ref, kp_ref, vc_ref, vp_ref, s_ref, o_ref):
        band = _attn_band(pl.program_id(0))
        for h in range(N_KV):
            hs = slice(h * HEAD_DIM, (h + 1) * HEAD_DIM)
            p, _ = _attn_probs(_stack_heads(q_ref, h, HEAD_DIM ** -0.5), kp_ref[:, hs].astype(BF16),
                               kc_ref[:, hs].astype(BF16), band, _stack_sinks(s_ref, h))
            p_cur, p_prev = _unfold(band[0], p)
            o = _dot(p_cur, vc_ref[:, hs].astype(BF16), 1, 0) + _dot(p_prev, vp_ref[:, hs].astype(BF16), 1, 0)
            for g in range(GROUP):
                hg = h * GROUP + g
                o_ref[:, hg * HEAD_DIM:(hg + 1) * HEAD_DIM] = o[g * WINDOW:(g + 1) * WINDOW]

    q, kc, kp, vc, vp, sk, o = _attn_specs(nb, False)
    return pl.pallas_call(
        body, name="attn_fwd", grid=(nb,), in_specs=[q, kc, kp, vc, vp, sk], out_specs=o,
        out_shape=jax.ShapeDtypeStruct((T, D), F32), compiler_params=_params(1),
    )(proj, proj, proj, proj, proj, sinks)


def _attn_bwd(proj, sinks, d_o, dproj):
    T = proj.shape[0]
    nb = T // WINDOW
    kat, vat = D, D + N_KV * HEAD_DIM

    def body(q_ref, kc_ref, kp_ref, vc_ref, vp_ref, s_ref, do_ref, _, dp_ref, ds_ref, ck_ref, cv_ref):
        i = pl.program_id(0)
        n = nb - 1 - i

        @pl.when(i == 0)
        def _():
            ck_ref[...] = jnp.zeros_like(ck_ref)
            cv_ref[...] = jnp.zeros_like(cv_ref)
            ds_ref[...] = jnp.zeros_like(ds_ref)

        band = _attn_band(n)
        for h in range(N_KV):
            hs = slice(h * HEAD_DIM, (h + 1) * HEAD_DIM)
            kp, kc = kp_ref[:, hs].astype(BF16), kc_ref[:, hs].astype(BF16)
            vp, vc = vp_ref[:, hs].astype(BF16), vc_ref[:, hs].astype(BF16)
            qs = _stack_heads(q_ref, h, HEAD_DIM ** -0.5)
            do = _stack_heads(do_ref, h)
            p, ps = _attn_probs(qs, kp, kc, band, _stack_sinks(s_ref, h))
            dp = jnp.where(band[0], _dot(do, vc, 1, 1), _dot(do, vp, 1, 1))
            delta = jnp.sum(p * dp, axis=-1, keepdims=True)
            ds_cur, ds_prev = _unfold(band[0], p * (dp - delta))
            p_cur, p_prev = _unfold(band[0], p)
            dsink = -ps * delta
            dq = ((_dot(ds_cur, kc, 1, 0) + _dot(ds_prev, kp, 1, 0)) * (HEAD_DIM ** -0.5)).astype(BF16)
            for g in range(GROUP):
                hg = h * GROUP + g
                rows = slice(g * WINDOW, (g + 1) * WINDOW)
                ds_ref[hg:hg + 1, :] += jnp.broadcast_to(jnp.sum(dsink[rows], axis=0, keepdims=True), (1, 128))
                dp_ref[:, hg * HEAD_DIM:(hg + 1) * HEAD_DIM] = dq[rows]
            dp_ref[:, kat + h * HEAD_DIM:kat + (h + 1) * HEAD_DIM] = (_dot(ds_cur, qs, 0, 0) + ck_ref[:, hs]).astype(BF16)
            dp_ref[:, vat + h * HEAD_DIM:vat + (h + 1) * HEAD_DIM] = (_dot(p_cur, do, 0, 0) + cv_ref[:, hs]).astype(BF16)
            ck_ref[:, hs] = _dot(ds_prev, qs, 0, 0)
            cv_ref[:, hs] = _dot(p_prev, do, 0, 0)

    q, kc, kp, vc, vp, sk, o = _attn_specs(nb, True)
    return pl.pallas_call(
        body, name="attn_bwd", grid=(nb,), in_specs=[q, kc, kp, vc, vp, sk, o, ANY],
        out_specs=[pl.BlockSpec((WINDOW, W_ATTN), lambda n: (nb - 1 - n, O_ATTN // W_ATTN)),
                   pl.BlockSpec((N_Q, 128), lambda n: (0, 0))],
        out_shape=[jax.ShapeDtypeStruct((T, MAIN), BF16), jax.ShapeDtypeStruct((N_Q, 128), F32)],
        scratch_shapes=[pltpu.VMEM((WINDOW, 256), F32), pltpu.VMEM((WINDOW, 256), F32)],
        input_output_aliases={7: 0}, compiler_params=_params(1),
    )(proj, proj, proj, proj, proj, sinks, d_o, dproj)


def _split3(x):
    hi = x.astype(BF16)
    r1 = x - hi.astype(F32)
    mid = r1.astype(BF16)
    lo = (r1 - mid.astype(F32)).astype(BF16)
    return hi, mid, lo


def _tri_sum(tri, x):
    hi, mid, lo = _split3(x)
    return _dot(tri, hi, 1, 0) + _dot(tri, mid, 1, 0) + _dot(tri, lo, 1, 0)


def _gla_chunk(q, k, lr, w2, b, lower):
    logit = _dot(lr.astype(BF16), w2.astype(BF16), 1, 0) + b
    la = (jnp.minimum(logit, 0.0) - jnp.log(1.0 + jnp.exp(-jnp.abs(logit)))) * (1.0 / GATE_NORM)
    g = _tri_sum(lower, la)
    gl = g[CHUNK - 1:CHUNK, :]
    eg = jnp.exp(g)
    qd = q * (DK ** -0.5) * eg
    ki = k * jnp.exp(-g)
    ke = k * jnp.exp(gl - g)
    return logit, g, gl, eg, qd, ki, ke


def _tri(lower):
    r = lax.broadcasted_iota(jnp.int32, (CHUNK, CHUNK), 0)
    c = lax.broadcasted_iota(jnp.int32, (CHUNK, CHUNK), 1)
    return (r >= c) if lower else (r <= c)


def _gla_rows(T):
    return _pick(T, (256, 128, 64))


def _gla_fwd(proj, proj_lr, w2p, gate_b):
    T = proj.shape[0]
    rb = _gla_rows(T)
    per = rb // CHUNK

    def body(q_ref, k_ref, v_ref, lr_ref, w2_ref, b_ref, o_ref, st_ref, s_scr):
        @pl.when(pl.program_id(1) == 0)
        def _():
            s_scr[...] = jnp.zeros_like(s_scr)

        low = _tri(True)
        lower = low.astype(BF16)
        for i in range(per):
            rows = slice(i * CHUNK, (i + 1) * CHUNK)
            _, g, gl, eg, qd, ki, ke = _gla_chunk(q_ref[rows, :], k_ref[rows, :], lr_ref[rows, :], w2_ref[...],
                                                  b_ref[...], lower)
            v = v_ref[rows, :].astype(BF16)
            qdb = qd.astype(BF16)
            att = jnp.where(low, _dot(qdb, ki.astype(BF16), 1, 1), 0.0)
            st = s_scr[...]
            st_ref[0, i] = st
            o_ref[rows, :] = _dot(att.astype(BF16), v, 1, 0) + _dot(qdb, st.astype(BF16), 1, 1)
            s_scr[...] = st * jnp.exp(gl) + _dot(v, ke.astype(BF16), 0, 0)

    return pl.pallas_call(
        body, name="gla_fwd", grid=(GLA_H, T // rb),
        in_specs=[pl.BlockSpec((rb, DK), lambda h, n: (n, (O_GLA + W_GLA * h) // DK)),
                  pl.BlockSpec((rb, DK), lambda h, n: (n, (O_GLA + W_GLA * h) // DK + 1)),
                  pl.BlockSpec((rb, DV), lambda h, n: (n, (O_GLA + W_GLA * h) // DV + 1)),
                  pl.BlockSpec((rb, LRP), lambda h, n: (n, 0)),
                  pl.BlockSpec((LRP, DK), lambda h, n: (0, h)),
                  pl.BlockSpec((1, DK), lambda h, n: (0, h))],
        out_specs=[pl.BlockSpec((rb, DV), lambda h, n: (n, h)),
                   pl.BlockSpec((1, per, DV, DK), lambda h, n: (h, n, 0, 0))],
        out_shape=[jax.ShapeDtypeStruct((T, GLA_H * DV), F32),
                   jax.ShapeDtypeStruct((GLA_H, T // CHUNK, DV, DK), F32)],
        scratch_shapes=[pltpu.VMEM((DV, DK), F32)], compiler_params=_params(2),
    )(proj, proj, proj, proj_lr, w2p, gate_b)


def _gla_bwd(proj, proj_lr, w2p, gate_b, states, d_o, dproj):
    T = proj.shape[0]
    rb = _gla_rows(T)
    per = rb // CHUNK
    nblk = T // rb

    def body(q_ref, k_ref, v_ref, lr_ref, w2_ref, b_ref, st_ref, do_ref, _, dp_ref, dl_ref, ds_scr):
        @pl.when(pl.program_id(1) == 0)
        def _():
            ds_scr[...] = jnp.zeros_like(ds_scr)

        low = _tri(True)
        lower = low.astype(BF16)
        upper = _tri(False).astype(BF16)
        for i in reversed(range(per)):
            rows = slice(i * CHUNK, (i + 1) * CHUNK)
            logit, g, gl, eg, qd, ki, ke = _gla_chunk(q_ref[rows, :], k_ref[rows, :], lr_ref[rows, :], w2_ref[...],
                                                      b_ref[...], lower)
            v = v_ref[rows, :].astype(BF16)
            do = do_ref[rows, :].astype(BF16)
            qdb, kib, keb = qd.astype(BF16), ki.astype(BF16), ke.astype(BF16)
            att = jnp.where(low, _dot(qdb, kib, 1, 1), 0.0).astype(BF16)
            sp = st_ref[0, i]
            ds = ds_scr[...]
            dsb = ds.astype(BF16)
            datt = jnp.where(low, _dot(do, v, 1, 1), 0.0).astype(BF16)
            dp_ref[rows, 2 * DK:] = (_dot(att, do, 0, 0) + _dot(keb, dsb, 1, 1)).astype(BF16)
            dqd = _dot(datt, kib, 1, 0) + _dot(do, sp.astype(BF16), 1, 0)
            dki = _dot(datt, qdb, 0, 0)
            dke = _dot(v, dsb, 1, 0)
            decay = jnp.exp(gl)
            ds_scr[...] = ds * decay + _dot(do, qdb, 0, 0)
            ddec = jnp.sum(ds * sp, axis=0, keepdims=True)
            dp_ref[rows, 0:DK] = (dqd * (DK ** -0.5) * eg).astype(BF16)
            dp_ref[rows, DK:2 * DK] = (dki * jnp.exp(-g) + dke * jnp.exp(gl - g)).astype(BF16)
            dke_ke = dke * ke
            dg = dqd * qd - dki * ki - dke_ke
            dgl = jnp.sum(dke_ke, axis=0, keepdims=True) + ddec * decay
            dla = _tri_sum(upper, dg) + dgl
            dl_ref[rows, :] = dla * (1.0 / GATE_NORM) * (1.0 - _sigmoid(logit))

    def rev(n):
        return nblk - 1 - n

    return pl.pallas_call(
        body, name="gla_bwd", grid=(GLA_H, nblk),
        in_specs=[pl.BlockSpec((rb, DK), lambda h, n: (rev(n), (O_GLA + W_GLA * h) // DK)),
                  pl.BlockSpec((rb, DK), lambda h, n: (rev(n), (O_GLA + W_GLA * h) // DK + 1)),
                  pl.BlockSpec((rb, DV), lambda h, n: (rev(n), (O_GLA + W_GLA * h) // DV + 1)),
                  pl.BlockSpec((rb, LRP), lambda h, n: (rev(n), 0)),
                  pl.BlockSpec((LRP, DK), lambda h, n: (0, h)),
                  pl.BlockSpec((1, DK), lambda h, n: (0, h)),
                  pl.BlockSpec((1, per, DV, DK), lambda h, n: (h, rev(n), 0, 0)),
                  pl.BlockSpec((rb, DV), lambda h, n: (rev(n), h)),
                  ANY],
        out_specs=[pl.BlockSpec((rb, W_GLA), lambda h, n: (rev(n), O_GLA // W_GLA + h)),
                   pl.BlockSpec((rb, DK), lambda h, n: (rev(n), h))],
        out_shape=[jax.ShapeDtypeStruct((T, MAIN), BF16), jax.ShapeDtypeStruct((T, GLA_H * DK), F32)],
        scratch_shapes=[pltpu.VMEM((DV, DK), F32)], input_output_aliases={8: 0}, compiler_params=_params(2),
    )(proj, proj, proj, proj_lr, w2p, gate_b, states, d_o, dproj)


def _local_step(x, target, p, reduce_part):
    u, u_t = _rms_fwd(x, p["norm1_w"], name="rms1_fwd")
    proj = _matmul(u, p["w_in_main"], name="mm_proj")
    proj_lr = _matmul(u, p["w_in_lr"], name="mm_proj_lr")
    attn_o = _attn_fwd(proj, p["sinks"])
    gla_raw, states = _gla_fwd(proj, proj_lr, p["w2p"], p["gate_b"])
    merged, merged_t = _merge_fwd(attn_o, gla_raw, proj, p["gla_norm_w"])
    h1 = _matmul(merged, p["w_out"], add=x, name="mm_out")
    v2, v2_t = _rms_fwd(h1, p["norm2_w"], name="rms2_fwd")
    gate, up, ff, ff_t = _ffn_fwd(v2, p["w_gate"], p["w_up"])
    h2 = _matmul(ff, p["w_down"], add=h1, name="mm_down")
    loss, dh2, dh2_b, g_final = _final_loss(h2, p["final_norm_w"], target)

    dgate, dup = _ffn_bwd_hidden(dh2_b, p["w_down"], gate, up)
    reduce_part("w_down", _matmul(ff_t, dh2_b, out_dtype=BF16, bj=D, name="mm_gdown").reshape(4, FFN // 4, D))
    reduce_part("w_gate", _matmul(v2_t, dgate, out_dtype=BF16, bj=PANEL, out_slots=True, name="mm_ggate"))
    reduce_part("w_up", _matmul(v2_t, dup, out_dtype=BF16, bj=PANEL, out_slots=True, name="mm_gup"))
    dv2 = _ffn_bwd_input(dgate, dup, p["w_gate"], p["w_up"])
    dh1, dh1_b, g_norm2 = _rms_bwd(dv2, h1, p["norm2_w"], dh2, name="rms2_bwd")
    dmerged = _matmul(dh1_b, p["w_out"], tb=True, bj=D, name="mm_dmerged")
    reduce_part("w_out", _matmul(merged_t, dh1_b, out_dtype=BF16, bj=D, name="mm_gout").reshape(4, D // 4, D))
    d_attn, d_gla, dproj, g_gla_norm = _merge_bwd(dmerged, attn_o, gla_raw, proj, p["gla_norm_w"])
    dproj, dlogit = _gla_bwd(proj, proj_lr, p["w2p"], p["gate_b"], states, d_gla, dproj)
    dproj, g_sinks = _attn_bwd(proj, p["sinks"], d_attn, dproj)
    g_gate_b = _colsum(dlogit, name="colsum_gate_b")
    g_w2 = _matmul(proj_lr, dlogit, ta=True, name="mm_gw2")
    dproj_lr = _matmul(dlogit, p["w2p"], tb=True, out_dtype=BF16, name="mm_dlr")
    g_in_main = _matmul(u_t, dproj, out_dtype=BF16, name="mm_gin")
    g_in_lr = _matmul(u_t, dproj_lr, out_dtype=BF16, name="mm_gin_lr")
    du = _matmul(dproj, p["w_in_main"], tb=True, bm=_pick(x.shape[0], (1024, 256)), bj=1024, bc=3200, name="mm_du")
    du = _matmul(dproj_lr, p["w_in_lr"], tb=True, add=du, name="mm_du_lr")
    dx, _, g_norm1 = _rms_bwd(du, x, p["norm1_w"], dh1, name="rms1_bwd")
    grads = dict(norm1_w=g_norm1, w_in_main=g_in_main, w_in_lr=g_in_lr, w2=g_w2[:RANK], gate_b=g_gate_b,
                 sinks=g_sinks[:, 0].reshape(1, N_Q), gla_norm_w=g_gla_norm, norm2_w=g_norm2, final_norm_w=g_final)
    return loss, dx, grads


def _place():
    x, y, c = lax.axis_index("x"), lax.axis_index("y"), lax.axis_index("c")
    chips = [(1 - x, y), (x, 1 - y), (1 - x, 1 - y)]
    return x, y, c, chips


def _hbm_shape(s, dt):
    return jax.ShapeDtypeStruct(s, dt)


ID_SIBLING = 9
ID_GATHER_IN = 10


def _handshake(peers):
    barrier = pltpu.get_barrier_semaphore()
    for peer in peers:
        pl.semaphore_signal(barrier, inc=1, device_id=peer, device_id_type=MESH)
    pl.semaphore_wait(barrier, len(peers))


def _launch_copies(body, args, out_shape, sems, *, name, collective_id=None):
    if collective_id is None:
        return pl.pallas_call(
            body, name=name, in_specs=[ANY] * len(args), out_specs=[ANY] * len(out_shape), out_shape=out_shape,
            scratch_shapes=sems)(*args)
    return pl.kernel(
        body, name=name, out_type=out_shape, mesh=plsc.ScalarSubcoreMesh(axis_name="sequencer", num_cores=1),
        scratch_types=sems, compiler_params=pltpu.CompilerParams(collective_id=collective_id))(*args)


def _gather_shards(shards, *, name, collective_id=None, after=()):
    n = len(shards)
    first_out = n + len(after)

    def body(*refs):
        ins, outs = refs[:n], refs[first_out:first_out + n]
        ici_send, ici_recv, d2d_send, d2d_recv, local_sem = refs[first_out + n:]
        x, y, c, chips = _place()
        me = 2 * x + y
        sibling = (x, y, 1 - c)
        if collective_id is not None:
            _handshake([sibling] + [(*chip, c) for chip in chips])

        def half(w, slot, hc):
            r2 = shards[w].shape[0] // 2
            return outs[w].at[slot, pl.ds(hc * r2, r2), :]

        locals_ = [pltpu.make_async_copy(ins[w], outs[w].at[me], local_sem.at[w]) for w in range(n)]
        for cp in locals_:
            cp.start()
        sends = []
        for w in range(n):
            r2 = shards[w].shape[0] // 2
            for j, chip in enumerate(chips):
                cp = pltpu.make_async_remote_copy(
                    src_ref=ins[w].at[pl.ds(c * r2, r2), :], dst_ref=half(w, me, c),
                    send_sem=ici_send.at[w * 3 + j], recv_sem=ici_recv.at[w * 3 + j],
                    device_id=(*chip, c), device_id_type=MESH)
                cp.start()
                sends.append(cp)
        for w in range(n):
            for j, chip in enumerate(chips):
                slot = 2 * chip[0] + chip[1]
                got = half(w, slot, c)
                pltpu.make_async_remote_copy(
                    src_ref=got, dst_ref=got, send_sem=ici_send.at[w * 3 + j], recv_sem=ici_recv.at[w * 3 + j],
                    device_id=(*chip, c), device_id_type=MESH).wait_recv()
                cp = pltpu.make_async_remote_copy(
                    src_ref=got, dst_ref=got, send_sem=d2d_send.at[w * 3 + j], recv_sem=d2d_recv.at[w * 3 + j],
                    device_id=sibling, device_id_type=MESH)
                cp.start()
                sends.append(cp)
        for w in range(n):
            for j, chip in enumerate(chips):
                slot = 2 * chip[0] + chip[1]
                got = half(w, slot, 1 - c)
                pltpu.make_async_remote_copy(
                    src_ref=got, dst_ref=got, send_sem=d2d_send.at[w * 3 + j], recv_sem=d2d_recv.at[w * 3 + j],
                    device_id=sibling, device_id_type=MESH).wait_recv()
        for cp in sends:
            cp.wait_send()
        for cp in locals_:
            cp.wait()

    return _launch_copies(
        body, list(shards) + list(after), [_hbm_shape((4,) + s.shape, s.dtype) for s in shards],
        [pltpu.SemaphoreType.DMA((3 * n,)), pltpu.SemaphoreType.DMA((3 * n,)), pltpu.SemaphoreType.DMA((3 * n,)),
         pltpu.SemaphoreType.DMA((3 * n,)), pltpu.SemaphoreType.DMA((n,))],
        name=name, collective_id=collective_id)


def _gather_staged(w, small):
    R, C = w.shape
    R2 = R // 2
    rb = 256
    per = R2 // rb
    r2 = small.shape[0] // 2

    def body(w_ref, s_ref, out_ref, outs_ref, stage, ici_send, ici_recv, fwd_send, fwd_recv, in_sem, put_sem,
             push_send, push_recv):
        x, y, c, chips = _place()
        me = 2 * x + y
        sibling = (x, y, 1 - c)
        slots = [2 * chip[0] + chip[1] for chip in chips]
        _handshake([sibling] + [(*chip, c) for chip in chips])

        def ici(j, src, dst, k):
            return pltpu.make_async_remote_copy(
                src_ref=src, dst_ref=dst, send_sem=ici_send.at[k], recv_sem=ici_recv.at[k],
                device_id=(*chips[j], c), device_id_type=MESH)

        sends = []
        for j in range(3):
            sends.append(ici(j, w_ref.at[pl.ds(c * R2, R2), :], out_ref.at[me, pl.ds(c * R2, R2), :], j))
            sends.append(ici(j, s_ref.at[pl.ds(c * r2, r2), :], outs_ref.at[me, pl.ds(c * r2, r2), :], 3 + j))
        for cp in sends:
            cp.start()
        mine = pltpu.make_async_copy(s_ref, outs_ref.at[me], put_sem.at[2])
        mine.start()

        def own(k, to_vmem):
            rows = pl.ds(k * rb, rb)
            if to_vmem:
                return pltpu.make_async_copy(w_ref.at[rows, :], stage.at[k % 2], in_sem.at[k % 2])
            return pltpu.make_async_copy(stage.at[k % 2], out_ref.at[me, rows, :], put_sem.at[k % 2])

        for k in range(R // rb):
            if k >= 2:
                own(k - 2, False).wait()
            own(k, True).start()
            own(k, True).wait()
            own(k, False).start()
        for k in range(R // rb - 2, R // rb):
            own(k, False).wait()

        def block(j, k, hc):
            return out_ref.at[slots[j], pl.ds(hc * R2 + k * rb, rb), :]

        def push(t, hc):
            j, k = divmod(t, per)
            return pltpu.make_async_remote_copy(
                src_ref=stage.at[t % 2], dst_ref=block(j, k, hc), send_sem=push_send.at[t % 2],
                recv_sem=push_recv.at[t], device_id=sibling, device_id_type=MESH)

        passed = []
        for j in range(3):
            ici(j, w_ref.at[pl.ds(c * R2, R2), :], out_ref.at[slots[j], pl.ds(c * R2, R2), :], j).wait_recv()
            for k in range(per):
                t = j * per + k
                if t >= 2:
                    push(t - 2, c).wait_send()
                fetch = pltpu.make_async_copy(block(j, k, c), stage.at[t % 2], in_sem.at[t % 2])
                fetch.start()
                fetch.wait()
                push(t, c).start()
            got = outs_ref.at[slots[j], pl.ds(c * r2, r2), :]
            ici(j, got, got, 3 + j).wait_recv()
            fw = pltpu.make_async_remote_copy(
                src_ref=got, dst_ref=got, send_sem=fwd_send.at[j], recv_sem=fwd_recv.at[j],
                device_id=sibling, device_id_type=MESH)
            fw.start()
            passed.append(fw)
        for t in range(3 * per - 2, 3 * per):
            push(t, c).wait_send()
        for t in range(3 * per):
            push(t, 1 - c).wait_recv()
        for j in range(3):
            theirs = outs_ref.at[slots[j], pl.ds((1 - c) * r2, r2), :]
            pltpu.make_async_remote_copy(
                src_ref=theirs, dst_ref=theirs, send_sem=fwd_send.at[j], recv_sem=fwd_recv.at[j],
                device_id=sibling, device_id_type=MESH).wait_recv()
        for cp in sends + passed:
            cp.wait_send()
        mine.wait()

    dma = pltpu.SemaphoreType.DMA
    return pl.pallas_call(
        body, name="gather_in", in_specs=[ANY, ANY], out_specs=[ANY, ANY],
        out_shape=[_hbm_shape((4, R, C), w.dtype), _hbm_shape((4,) + small.shape, small.dtype)],
        scratch_shapes=[pltpu.VMEM((2, rb, C), w.dtype), dma((6,)), dma((6,)), dma((3,)), dma((3,)), dma((2,)),
                        dma((3,)), dma((2,)), dma((3 * per,))],
        compiler_params=pltpu.CompilerParams(vmem_limit_bytes=VMEM_LIMIT, collective_id=ID_GATHER_IN),
    )(w, small)


def _pair_blocks(R2):
    return _pick(R2, (256, 352, 128, 64, 32, 16))


def _pair_reduce(part, *, name):
    _, R, C = part.shape
    R2 = R // 2
    rb = _pick(R2, (512, 352, 256, 128, 64, 32, 16))
    nblk = R2 // rb
    steps = [(s, i) for s in range(4) for i in range(nblk)]
    n = len(steps)

    def body(part_ref, sums_ref, own_buf, oth_buf, rcv_buf, out_buf, own_sem, oth_sem, out_sem, send_sem, recv_sem):
        x, y, c, _ = _place()
        sibling = (x, y, 1 - c)
        _handshake([sibling])

        def fetch(t, half, buf, sem):
            s, i = steps[t]
            return pltpu.make_async_copy(part_ref.at[s, pl.ds(half * R2 + i * rb, rb), :], buf.at[t % 2], sem.at[t % 2])

        def push(t):
            return pltpu.make_async_remote_copy(
                src_ref=oth_buf.at[t % 2], dst_ref=rcv_buf.at[t % 3], send_sem=send_sem.at[t % 2],
                recv_sem=recv_sem.at[t % 2], device_id=sibling, device_id_type=MESH)

        def store(t):
            s, i = steps[t]
            return pltpu.make_async_copy(out_buf.at[t % 2], sums_ref.at[s, pl.ds(i * rb, rb), :], out_sem.at[t % 2])

        fetch(0, c, own_buf, own_sem).start()
        fetch(0, 1 - c, oth_buf, oth_sem).start()
        fetch(0, 1 - c, oth_buf, oth_sem).wait()
        push(0).start()
        for t in range(n):
            if t + 1 < n:
                fetch(t + 1, c, own_buf, own_sem).start()
                fetch(t + 1, 1 - c, oth_buf, oth_sem).start()
            fetch(t, c, own_buf, own_sem).wait()
            push(t).wait()
            if t + 1 < n:
                fetch(t + 1, 1 - c, oth_buf, oth_sem).wait()
                push(t + 1).start()
            if t >= 2:
                store(t - 2).wait()
            out_buf[t % 2] = (own_buf[t % 2].astype(F32) + rcv_buf[t % 3].astype(F32)).astype(BF16)
            store(t).start()
        for t in range(max(n - 2, 0), n):
            store(t).wait()

    buf = pltpu.VMEM((2, rb, C), BF16)
    sem2 = pltpu.SemaphoreType.DMA((2,))
    return pl.pallas_call(
        body, name=name, in_specs=[ANY], out_specs=ANY, out_shape=_hbm_shape((4, R2, C), BF16),
        scratch_shapes=[buf, buf, pltpu.VMEM((3, rb, C), BF16), buf, sem2, sem2, sem2, sem2, sem2],
        compiler_params=pltpu.CompilerParams(vmem_limit_bytes=VMEM_LIMIT, collective_id=ID_SIBLING),
    )(part)


def _chip_exchange(sums, *, name, collective_id=None):
    n = len(sums)

    def body(*refs):
        ins, outs = refs[:n], refs[n:2 * n]
        send_sem, recv_sem, local_sem = refs[2 * n:]
        x, y, c, chips = _place()
        me = 2 * x + y
        if collective_id is not None:
            _handshake([(*chip, c) for chip in chips])
        cps = []
        for w in range(n):
            lc = pltpu.make_async_copy(ins[w].at[me], outs[w].at[me], local_sem.at[w])
            lc.start()
            cps.append(lc)
            for j, chip in enumerate(chips):
                slot = 2 * chip[0] + chip[1]
                rc = pltpu.make_async_remote_copy(
                    src_ref=ins[w].at[slot], dst_ref=outs[w].at[me],
                    send_sem=send_sem.at[w * 3 + j], recv_sem=recv_sem.at[w * 3 + j],
                    device_id=(*chip, c), device_id_type=MESH)
                rc.start()
                cps.append(rc)
        for w in range(n):
            for j, chip in enumerate(chips):
                slot = 2 * chip[0] + chip[1]
                pltpu.make_async_remote_copy(
                    src_ref=ins[w].at[slot], dst_ref=outs[w].at[slot],
                    send_sem=send_sem.at[w * 3 + j], recv_sem=recv_sem.at[w * 3 + j],
                    device_id=(*chip, c), device_id_type=MESH).wait_recv()
        for w in range(n):
            cps[w * 4].wait()
            for j in range(3):
                cps[w * 4 + 1 + j].wait_send()

    return _launch_copies(
        body, sums, [_hbm_shape(s.shape, s.dtype) for s in sums],
        [pltpu.SemaphoreType.DMA((3 * n,)), pltpu.SemaphoreType.DMA((3 * n,)), pltpu.SemaphoreType.DMA((n,))],
        name=name, collective_id=collective_id)


def _sum_join(recv, *, name):
    _, R2, C = recv.shape
    rb = _pair_blocks(R2)
    nblk = R2 // rb

    def body(recv_ref, out_ref, in_buf, acc_buf, in_sem, loc_sem, send_sem, recv_sem):
        x, y, c, _ = _place()
        sibling = (x, y, 1 - c)
        _handshake([sibling])

        def fetch(t):
            return pltpu.make_async_copy(recv_ref.at[:, pl.ds(t * rb, rb), :], in_buf.at[t % 2], in_sem.at[t % 2])

        def rows(t, half):
            return out_ref.at[pl.ds(half * R2 + t * rb, rb), :]

        def put_local(t):
            return pltpu.make_async_copy(acc_buf.at[t % 2], rows(t, c), loc_sem.at[t])

        def put_remote(t, half):
            return pltpu.make_async_remote_copy(
                src_ref=acc_buf.at[t % 2], dst_ref=rows(t, half), send_sem=send_sem.at[t], recv_sem=recv_sem.at[t],
                device_id=sibling, device_id_type=MESH)

        fetch(0).start()
        for t in range(nblk):
            if t + 1 < nblk:
                fetch(t + 1).start()
            fetch(t).wait()
            if t >= 2:
                put_local(t - 2).wait()
                put_remote(t - 2, c).wait_send()
            acc = in_buf[t % 2, 0].astype(F32)
            for s in range(1, 4):
                acc = acc + in_buf[t % 2, s].astype(F32)
            acc_buf[t % 2] = acc
            put_local(t).start()
            put_remote(t, c).start()
        for t in range(max(nblk - 2, 0), nblk):
            put_local(t).wait()
            put_remote(t, c).wait_send()
        for t in range(nblk):
            put_remote(t, 1 - c).wait_recv()

    semn = pltpu.SemaphoreType.DMA((nblk,))
    return pl.pallas_call(
        body, name=name, in_specs=[ANY], out_specs=ANY, out_shape=_hbm_shape((2 * R2, C), F32),
        scratch_shapes=[pltpu.VMEM((2, 4, rb, C), BF16), pltpu.VMEM((2, rb, C), F32),
                        pltpu.SemaphoreType.DMA((2,)), semn, semn, semn],
        compiler_params=pltpu.CompilerParams(vmem_limit_bytes=VMEM_LIMIT, collective_id=ID_SIBLING),
    )(recv)


def _sum_small(pack):
    R, C = pack.shape

    def body(in_ref, out_ref, all_ref, send_sem, recv_sem):
        x, y, c, _ = _place()
        me = 4 * x + 2 * y + c
        all_ref[me] = in_ref[...]
        cps = []
        for k in range(1, 8):
            peer = (x ^ (k >> 2), y ^ ((k >> 1) & 1), c ^ (k & 1))
            cp = pltpu.make_async_remote_copy(
                src_ref=in_ref, dst_ref=all_ref.at[me], send_sem=send_sem.at[k - 1], recv_sem=recv_sem.at[k - 1],
                device_id=peer, device_id_type=MESH)
            cp.start()
            cps.append(cp)
        for k in range(1, 8):
            peer = (x ^ (k >> 2), y ^ ((k >> 1) & 1), c ^ (k & 1))
            slot = 4 * peer[0] + 2 * peer[1] + peer[2]
            pltpu.make_async_remote_copy(
                src_ref=in_ref, dst_ref=all_ref.at[slot], send_sem=send_sem.at[k - 1], recv_sem=recv_sem.at[k - 1],
                device_id=peer, device_id_type=MESH).wait_recv()
        for cp in cps:
            cp.wait_send()
        acc = all_ref[0]
        for d in range(1, 8):
            acc = acc + all_ref[d]
        out_ref[...] = acc

    return pl.pallas_call(
        body, name="sum_small", in_specs=[pl.BlockSpec(memory_space=pltpu.VMEM)],
        out_specs=pl.BlockSpec(memory_space=pltpu.VMEM), out_shape=jax.ShapeDtypeStruct((R, C), F32),
        scratch_shapes=[pltpu.VMEM((8, R, C), F32), pltpu.SemaphoreType.DMA((7,)), pltpu.SemaphoreType.DMA((7,))],
    )(pack)


ADAMW_BLOCK_BYTES = 2 * 1024 * 1024


def _adamw_block(R, C):
    padded = -(-C // 128) * 128
    rows = [rb for rb in range(8, R + 1, 8) if R % rb == 0 and rb * padded * 4 <= ADAMW_BLOCK_BYTES]
    if rows or R * padded * 4 <= ADAMW_BLOCK_BYTES:
        return (max(rows) if rows else R), C
    cols = [cb for cb in range(128, C + 1, 128) if C % cb == 0 and R * cb * 4 <= ADAMW_BLOCK_BYTES]
    return R, max(cols)


def _adamw(w, g, m, v, *, name):
    R, C = w.shape
    rb, cb = _adamw_block(R, C)
    c1 = 1.0 / (1.0 - B1 ** STEP)
    c2 = 1.0 / (1.0 - B2 ** STEP)

    def body(w_ref, g_ref, m_ref, v_ref, d_ref, nm_ref, nv_ref):
        gv = g_ref[...]
        m2 = B1 * m_ref[...] + (1.0 - B1) * gv
        v2 = B2 * v_ref[...] + (1.0 - B2) * gv * gv
        nm_ref[...] = m2
        nv_ref[...] = v2
        d_ref[...] = -LR * ((m2 * c1) / (jnp.sqrt(v2 * c2) + ADAM_EPS) + WD * w_ref[...])

    blk = pl.BlockSpec((rb, cb), lambda i, j: (i, j))
    return pl.pallas_call(
        body, name=name, grid=(R // rb, C // cb), in_specs=[blk] * 4, out_specs=[blk] * 3,
        out_shape=[jax.ShapeDtypeStruct((R, C), F32)] * 3, compiler_params=_params(2),
    )(w, g, m, v)


SMALL = (("norm1_w", D), ("norm2_w", D), ("final_norm_w", D), ("gate_b", GLA_H * DK), ("gla_norm_w", DV), ("sinks", N_Q))
PACK_W = 1024


def _pack_small(vals, w2, loss):
    rows = []
    for name, width in SMALL:
        v = vals[name].reshape(-1)
        rows.append(jnp.pad(v, (0, (-width) % PACK_W)).reshape(-1, PACK_W))
    rows.append(w2)
    rows.append(jnp.broadcast_to(loss.reshape(1, 1), (1, PACK_W)))
    pack = jnp.concatenate(rows, axis=0)
    return jnp.pad(pack, ((0, 32 - pack.shape[0]), (0, 0)))


def _unpack_small(pack):
    out, r = {}, 0
    for name, width in SMALL:
        nr = -(-width // PACK_W)
        out[name] = pack[r:r + nr].reshape(-1)[:width]
        r += nr
    out["w2"] = pack[r:r + RANK]
    out["loss"] = pack[r + RANK, 0]
    return out


def kernel(x, norm1_w, w_in, gla_gate_w2, gla_gate_b, attn_sinks, gla_norm_w, w_out, norm2_w, w_ffn_gate, w_ffn_up, w_ffn_down, final_norm_w, loss_target, m_norm1_w, m_w_in, m_gla_gate_w2, m_gla_gate_b, m_attn_sinks, m_gla_norm_w, m_w_out, m_norm2_w, m_w_ffn_gate, m_w_ffn_up, m_w_ffn_down, m_final_norm_w, v_norm1_w, v_w_in, v_gla_gate_w2, v_gla_gate_b, v_attn_sinks, v_gla_norm_w, v_w_out, v_norm2_w, v_w_ffn_gate, v_w_ffn_up, v_w_ffn_down, v_final_norm_w):
    chip = 2 * lax.axis_index("x") + lax.axis_index("y")
    w_in_s, w_out_s, w_gate_s, w_up_s, w_down_s, w2_s = (
        w_in[0], w_out[0], w_ffn_gate[0], w_ffn_up[0], w_ffn_down[0], gla_gate_w2[0])
    CS = SHARD_COLS

    g_in, g_w2 = _gather_staged(w_in_s.astype(BF16), w2_s)
    g_out, = _gather_shards([w_out_s.astype(BF16)], name="gather_out", collective_id=1, after=[g_w2])
    g_gate, g_up = _gather_shards([w_gate_s.astype(BF16), w_up_s.astype(BF16)], name="gather_gate_up",
                                  collective_id=7, after=[g_w2])
    g_down, = _gather_shards([w_down_s.astype(BF16)], name="gather_down", collective_id=8, after=[g_w2])
    w_main, w_lr = _w_in_to_main(g_in)
    w2_full = jnp.transpose(g_w2, (1, 0, 2)).reshape(RANK, GLA_H * DK)
    p = dict(
        norm1_w=norm1_w, norm2_w=norm2_w, final_norm_w=final_norm_w.reshape(1, D), gate_b=gla_gate_b,
        gla_norm_w=gla_norm_w, sinks=jnp.pad(attn_sinks, ((0, 0), (0, 128 - N_Q))),
        w_in_main=w_main, w_in_lr=w_lr,
        w2p=jnp.pad(w2_full, ((0, LRP - RANK), (0, 0))).astype(BF16),
        w_out=g_out.reshape(D, D),
        w_gate=g_gate, w_up=g_up, w_down=g_down.reshape(FFN, D),
    )

    tags = ["w_in", "w_out", "w_gate", "w_up", "w_down"]
    recv = {}

    def reduce_part(tag, part):
        sums = _pair_reduce(part, name="pair_reduce_" + tag)
        recv[tag], = _chip_exchange([sums], name="chip_exchange_" + tag, collective_id=2 + tags.index(tag))

    loss_blk, dx, g = _local_step(x[0], loss_target[0], p, reduce_part)
    reduce_part("w_in", _main_to_shards(g["w_in_main"], g["w_in_lr"]))
    big = [_sum_join(recv[tg], name="sum_join_" + tg) for tg in tags]

    small = _unpack_small(_sum_small(_pack_small(g, g["w2"], loss_blk[0, 0])))
    loss = small["loss"]
    g_w2_mine = lax.dynamic_slice_in_dim(small["w2"], chip * (GLA_H * DK // 4), GLA_H * DK // 4, axis=1)

    grads = dict(
        norm1_w=small["norm1_w"].reshape(1, D), w_in=big[0], gla_gate_w2=g_w2_mine,
        gla_gate_b=small["gate_b"].reshape(1, -1), attn_sinks=small["sinks"].reshape(1, N_Q),
        gla_norm_w=small["gla_norm_w"].reshape(1, DV), w_out=big[1], norm2_w=small["norm2_w"].reshape(1, D),
        w_ffn_gate=big[2], w_ffn_up=big[3], w_ffn_down=big[4], final_norm_w=small["final_norm_w"].reshape(1, D))
    def lin(a):
        return jnp.transpose(a.reshape(D, CS))

    def unlin(a):
        return jnp.transpose(a)

    grads["w_in"] = lin(big[0])
    weights = dict(
        norm1_w=(norm1_w, m_norm1_w, v_norm1_w), w_in=(lin(w_in), lin(m_w_in), lin(v_w_in)),
        gla_gate_w2=(w2_s, m_gla_gate_w2[0], v_gla_gate_w2[0]), gla_gate_b=(gla_gate_b, m_gla_gate_b, v_gla_gate_b),
        attn_sinks=(attn_sinks, m_attn_sinks, v_attn_sinks), gla_norm_w=(gla_norm_w, m_gla_norm_w, v_gla_norm_w),
        w_out=(w_out_s, m_w_out[0], v_w_out[0]), norm2_w=(norm2_w, m_norm2_w, v_norm2_w),
        w_ffn_gate=(w_gate_s, m_w_ffn_gate[0], v_w_ffn_gate[0]), w_ffn_up=(w_up_s, m_w_ffn_up[0], v_w_ffn_up[0]),
        w_ffn_down=(w_down_s, m_w_ffn_down[0], v_w_ffn_down[0]),
        final_norm_w=(final_norm_w.reshape(1, D), m_final_norm_w.reshape(1, D), v_final_norm_w.reshape(1, D)))
    names = ["norm1_w", "w_in", "gla_gate_w2", "gla_gate_b", "attn_sinks", "gla_norm_w", "w_out", "norm2_w",
             "w_ffn_gate", "w_ffn_up", "w_ffn_down", "final_norm_w"]
    lead = {"norm1_w": False, "gla_gate_b": False, "attn_sinks": False, "gla_norm_w": False, "norm2_w": False}
    g_out_l, d_out, m_out, v_out = [], [], [], []
    for nm in names:
        w, m, v = weights[nm]
        gr = grads[nm]
        dl, nmn, nvn = _adamw(w, gr, m, v, name="adamw_" + nm)
        if nm == "w_in":
            gr, dl, nmn, nvn, w = big[0], unlin(dl), unlin(nmn), unlin(nvn), w_in_s
        if nm == "final_norm_w":
            shape = (D,)
        elif nm in lead:
            shape = w.shape
        else:
            shape = (1,) + w.shape
        g_out_l.append(gr.reshape(shape))
        d_out.append(dl.reshape(shape))
        m_out.append(nmn.reshape(shape))
        v_out.append(nvn.reshape(shape))
    return (loss, dx[None], *g_out_l, *d_out, *m_out, *v_out)
```

```python
import functools

import jax
import jax.numpy as jnp
from jax import lax
from jax.experimental import pallas as pl
from jax.experimental.pallas import tpu as pltpu
from jax.experimental.pallas import tpu_sc as plsc

F32 = jnp.float32
BF16 = jnp.bfloat16

D = 2048
HEAD_DIM = 64
N_Q = 32
N_KV = 4
GROUP = 8
WINDOW = 128
GLA_H = 4
DK = 256
DV = 512
RANK = 16
CHUNK = 64
FFN = 5632
EPS = 1e-6
MASK_VALUE = -1e30
GATE_NORM = 16.0
LR, B1, B2, ADAM_EPS, WD, STEP = 0.001, 0.9, 0.999, 1e-08, 0.01, 10

MAIN = 12800
O_MERGE, O_GLA, O_ATTN = 0, 6144, 10240
W_MERGE, W_GLA, W_ATTN = 3 * DV, 2 * DK + DV, D + 2 * N_KV * HEAD_DIM
LR_AT = 6656
LRP = 128


def _main_pieces():
    o_gq, o_gk, o_gv, o_gr, o_ga, o_gb = 2560, 3584, 4608, LR_AT + RANK, LR_AT + RANK + D, LR_AT + RANK + 2 * D
    pieces = []
    for h in range(GLA_H):
        pieces += [(o_gr + DV * h, DV), (o_ga + DV * h, DV), (o_gb + DV * h, DV)]
    for h in range(GLA_H):
        pieces += [(o_gq + DK * h, DK), (o_gk + DK * h, DK), (o_gv + DV * h, DV)]
    pieces.append((0, W_ATTN))
    return pieces


SHARD_COLS = 3204

VMEM_LIMIT = 56 * 1024 * 1024
MESH = pl.DeviceIdType.MESH
ANY = pl.BlockSpec(memory_space=pl.ANY)


def _params(ngrid):
    return pltpu.CompilerParams(dimension_semantics=("arbitrary",) * ngrid, vmem_limit_bytes=VMEM_LIMIT)


def _shard_segments(lo, hi):
    segs = []
    while lo < hi:
        s = lo // SHARD_COLS
        e = min(hi, (s + 1) * SHARD_COLS)
        segs.append((s, lo - s * SHARD_COLS, e - s * SHARD_COLS))
        lo = e
    return segs


def _lanes(pieces):
    return pieces[0] if len(pieces) == 1 else jnp.concatenate(pieces, axis=1)


def _w_in_to_main(shards):
    rb = 256

    def body(g_ref, main_ref, lr_ref):
        at = 0
        for a, w in _main_pieces():
            main_ref[:, at:at + w] = _lanes([g_ref[s, :, lo:hi] for s, lo, hi in _shard_segments(a, a + w)])
            at += w
        lr = [g_ref[s, :, lo:hi] for s, lo, hi in _shard_segments(LR_AT, LR_AT + RANK)]
        lr_ref[...] = _lanes(lr + [jnp.zeros((rb, LRP - RANK), lr_ref.dtype)])

    return pl.pallas_call(
        body, name="w_in_to_main", grid=(D // rb,),
        in_specs=[pl.BlockSpec((4, rb, SHARD_COLS), lambda i: (0, i, 0))],
        out_specs=[pl.BlockSpec((rb, MAIN), lambda i: (i, 0)), pl.BlockSpec((rb, LRP), lambda i: (i, 0))],
        out_shape=[jax.ShapeDtypeStruct((D, MAIN), shards.dtype), jax.ShapeDtypeStruct((D, LRP), shards.dtype)],
        compiler_params=_params(1),
    )(shards)


def _main_to_shards(g_main, g_lr):
    rb = 256
    at, sources = 0, [(LR_AT, RANK, 1, 0)]
    for a, w in _main_pieces():
        sources.append((a, w, 0, at))
        at += w
    sources.sort()

    def body(main_ref, lr_ref, out_ref):
        refs = (main_ref, lr_ref)
        for s in range(4):
            lo, hi = s * SHARD_COLS, (s + 1) * SHARD_COLS
            pieces = []
            for a, w, which, src in sources:
                b, e = max(a, lo), min(a + w, hi)
                if b < e:
                    pieces.append(refs[which][:, src + b - a:src + e - a])
            out_ref[s] = _lanes(pieces)

    return pl.pallas_call(
        body, name="main_to_shards", grid=(D // rb,),
        in_specs=[pl.BlockSpec((rb, MAIN), lambda i: (i, 0)), pl.BlockSpec((rb, LRP), lambda i: (i, 0))],
        out_specs=pl.BlockSpec((4, rb, SHARD_COLS), lambda i: (0, i, 0)),
        out_shape=jax.ShapeDtypeStruct((4, D, SHARD_COLS), g_main.dtype), compiler_params=_params(1),
    )(g_main, g_lr)


def _pick(n, cands):
    for c in cands:
        if n % c == 0:
            return c
    return n


def _sigmoid(x):
    return 1.0 / (1.0 + jnp.exp(-x))


def _dot(a, b, ca, cb):
    return lax.dot_general(a, b, (((ca,), (cb,)), ((), ())), preferred_element_type=F32)


def _matmul(a, b, *, name, ta=False, tb=False, add=None, out_dtype=F32, bm=None, bj=None, bc=None,
            b_slots=False, out_slots=False):
    C, M = a.shape if ta else a.shape[::-1]
    if b_slots:
        if tb:
            J, bc = b.shape[1], b.shape[2]
            assert b.shape[0] * bc == C
        else:
            bj = b.shape[2]
            J = b.shape[0] * bj
            assert b.shape[1] == C
    else:
        J = b.shape[0] if tb else b.shape[1]
        assert (b.shape[1] if tb else b.shape[0]) == C
    bm = bm or _pick(M, (1024, 512, 256, 128) if C <= FFN else (512, 256, 128))
    bj = bj or _pick(J, (1280, 1024, 512, 256, 128) if C <= D else (512, 256, 128))
    bc = bc or (C if C <= FFN else _pick(C, (6400,)))
    nm, nj, nc = M // bm, J // bj, C // bc
    has_add = add is not None

    def body(*refs):
        a_ref, b_ref = refs[0], refs[1]
        add_ref = refs[2] if has_add else None
        o_ref = refs[3] if has_add else refs[2]
        p = _dot(a_ref[...].astype(BF16), b_ref[...].astype(BF16), 0 if ta else 1, 1 if tb else 0)

        def finish(acc):
            if has_add:
                acc = acc + add_ref[...]
            o_ref[...] = acc.astype(o_ref.dtype)

        if nc == 1:
            finish(p)
        else:
            acc_ref = refs[-1]
            c = pl.program_id(2)

            @pl.when(c == 0)
            def _():
                acc_ref[...] = p

            @pl.when(c > 0)
            def _():
                acc_ref[...] += p

            @pl.when(c == nc - 1)
            def _():
                finish(acc_ref[...])

    a_spec = pl.BlockSpec((bc, bm), lambda m, j, c: (c, m)) if ta else pl.BlockSpec((bm, bc), lambda m, j, c: (m, c))
    if b_slots:
        b_spec = (pl.BlockSpec((None, bj, bc), lambda m, j, c: (c, j, 0)) if tb
                  else pl.BlockSpec((None, bc, bj), lambda m, j, c: (j, c, 0)))
    else:
        b_spec = (pl.BlockSpec((bj, bc), lambda m, j, c: (j, c)) if tb
                  else pl.BlockSpec((bc, bj), lambda m, j, c: (c, j)))
    if out_slots:
        assert not has_add
        o_spec = pl.BlockSpec((None, bm, bj), lambda m, j, c: (j, m, 0))
        out_shape = jax.ShapeDtypeStruct((nj, M, bj), out_dtype)
    else:
        o_spec = pl.BlockSpec((bm, bj), lambda m, j, c: (m, j))
        out_shape = jax.ShapeDtypeStruct((M, J), out_dtype)
    in_specs = [a_spec, b_spec] + ([o_spec] if has_add else [])
    args = (a, b) + ((add,) if has_add else ())
    return pl.pallas_call(
        body, name=name, grid=(nm, nj, nc), in_specs=in_specs, out_specs=o_spec,
        out_shape=out_shape,
        scratch_shapes=[pltpu.VMEM((bm, bj), F32)] if nc > 1 else [],
        compiler_params=_params(3),
    )(*args)


def _rows(T):
    return _pick(T, (256, 128))


def _rms_fwd(x, w, *, name):
    T = x.shape[0]
    rb = _rows(T)

    def body(x_ref, w_ref, u_ref, ut_ref):
        xv = x_ref[...]
        r = lax.rsqrt(jnp.mean(xv * xv, axis=-1, keepdims=True) + EPS)
        u = xv * r * w_ref[...]
        u_ref[...] = u.astype(BF16)
        ut_ref[...] = u.T.astype(BF16)

    return pl.pallas_call(
        body, name=name, grid=(T // rb,),
        in_specs=[pl.BlockSpec((rb, D), lambda i: (i, 0)), pl.BlockSpec((1, D), lambda i: (0, 0))],
        out_specs=[pl.BlockSpec((rb, D), lambda i: (i, 0)), pl.BlockSpec((D, rb), lambda i: (0, i))],
        out_shape=[jax.ShapeDtypeStruct((T, D), BF16), jax.ShapeDtypeStruct((D, T), BF16)], compiler_params=_params(1),
    )(x, w)


def _rms_bwd(dy, h, w, resid, *, name):
    T = h.shape[0]
    rb = _rows(T)

    def body(dy_ref, h_ref, w_ref, res_ref, dh_ref, dhb_ref, dw_ref):
        @pl.when(pl.program_id(0) == 0)
        def _():
            dw_ref[...] = jnp.zeros_like(dw_ref)

        hv = h_ref[...]
        r = lax.rsqrt(jnp.mean(hv * hv, axis=-1, keepdims=True) + EPS)
        hn = hv * r
        dyv = dy_ref[...]
        dw_ref[...] += jnp.sum(dyv * hn, axis=0, keepdims=True)
        t = dyv * w_ref[...]
        dh = res_ref[...] + r * (t - hn * jnp.mean(t * hn, axis=-1, keepdims=True))
        dh_ref[...] = dh
        dhb_ref[...] = dh.astype(BF16)

    blk = pl.BlockSpec((rb, D), lambda i: (i, 0))
    vec = pl.BlockSpec((1, D), lambda i: (0, 0))
    return pl.pallas_call(
        body, name=name, grid=(T // rb,), in_specs=[blk, blk, vec, blk], out_specs=[blk, blk, vec],
        out_shape=[jax.ShapeDtypeStruct((T, D), F32), jax.ShapeDtypeStruct((T, D), BF16),
                   jax.ShapeDtypeStruct((1, D), F32)],
        compiler_params=_params(1),
    )(dy, h, w, resid)


def _final_loss(h, w, target):
    T = h.shape[0]
    rb = _rows(T)

    def body(h_ref, w_ref, t_ref, loss_ref, dh_ref, dhb_ref, dw_ref):
        @pl.when(pl.program_id(0) == 0)
        def _():
            dw_ref[...] = jnp.zeros_like(dw_ref)
            loss_ref[...] = jnp.zeros_like(loss_ref)

        hv = h_ref[...]
        wv = w_ref[...]
        r = lax.rsqrt(jnp.mean(hv * hv, axis=-1, keepdims=True) + EPS)
        hn = hv * r
        e = hn * wv - t_ref[...]
        row = jnp.sum(e * e, axis=-1, keepdims=True) * (0.5 / D)
        loss_ref[...] += jnp.broadcast_to(jnp.sum(row, axis=0, keepdims=True), loss_ref.shape)
        dy = e * (1.0 / D)
        dw_ref[...] += jnp.sum(dy * hn, axis=0, keepdims=True)
        t = dy * wv
        dh = r * (t - hn * jnp.mean(t * hn, axis=-1, keepdims=True))
        dh_ref[...] = dh
        dhb_ref[...] = dh.astype(BF16)

    blk = pl.BlockSpec((rb, D), lambda i: (i, 0))
    vec = pl.BlockSpec((1, D), lambda i: (0, 0))
    return pl.pallas_call(
        body, name="final_loss", grid=(T // rb,), in_specs=[blk, vec, blk],
        out_specs=[pl.BlockSpec((8, 128), lambda i: (0, 0)), blk, blk, vec],
        out_shape=[jax.ShapeDtypeStruct((8, 128), F32), jax.ShapeDtypeStruct((T, D), F32),
                   jax.ShapeDtypeStruct((T, D), BF16), jax.ShapeDtypeStruct((1, D), F32)],
        compiler_params=_params(1),
    )(h, w, target)


PANEL = FFN // 4


def _ffn_fwd(v2, w_gate, w_up):
    T = v2.shape[0]
    bm = _pick(T, (512, 256))

    def body(a_ref, bg_ref, bu_ref, gate_ref, up_ref, ff_ref, fft_ref):
        a = a_ref[...]
        g = _dot(a, bg_ref[...], 1, 0)
        u = _dot(a, bu_ref[...], 1, 0)
        gate_ref[...] = g
        up_ref[...] = u
        ff = g * _sigmoid(g) * u
        ff_ref[...] = ff.astype(BF16)
        fft_ref[...] = ff.T.astype(BF16)

    panel = pl.BlockSpec((None, D, PANEL), lambda j, m: (j, 0, 0))
    out = pl.BlockSpec((bm, PANEL), lambda j, m: (m, j))
    return pl.pallas_call(
        body, name="ffn_fwd", grid=(4, T // bm),
        in_specs=[pl.BlockSpec((bm, D), lambda j, m: (m, 0)), panel, panel],
        out_specs=[out, out, out, pl.BlockSpec((PANEL, bm), lambda j, m: (j, m))],
        out_shape=[jax.ShapeDtypeStruct((T, FFN), F32), jax.ShapeDtypeStruct((T, FFN), F32),
                   jax.ShapeDtypeStruct((T, FFN), BF16), jax.ShapeDtypeStruct((FFN, T), BF16)],
        compiler_params=_params(2),
    )(v2, w_gate, w_up)


def _ffn_bwd_hidden(dh2, w_down, gate, up):
    T = dh2.shape[0]
    bm = _pick(T, (512, 256))

    def body(a_ref, b_ref, g_ref, u_ref, dg_ref, du_ref):
        d = _dot(a_ref[...], b_ref[...], 1, 1)
        g = g_ref[...]
        sg = _sigmoid(g)
        du_ref[...] = (d * g * sg).astype(BF16)
        dg_ref[...] = (d * u_ref[...] * sg * (1.0 + g * (1.0 - sg))).astype(BF16)

    blk = pl.BlockSpec((bm, PANEL), lambda j, m: (m, j))
    return pl.pallas_call(
        body, name="ffn_bwd_hidden", grid=(4, T // bm),
        in_specs=[pl.BlockSpec((bm, D), lambda j, m: (m, 0)), pl.BlockSpec((PANEL, D), lambda j, m: (j, 0)), blk, blk],
        out_specs=[blk, blk], out_shape=[jax.ShapeDtypeStruct((T, FFN), BF16)] * 2, compiler_params=_params(2),
    )(dh2, w_down, gate, up)


def _ffn_bwd_input(dgate, dup, w_gate, w_up):
    T = dgate.shape[0]
    bm = _pick(T, (1024, 512, 256))
    bj = 1024

    def body(ag_ref, au_ref, bg_ref, bu_ref, o_ref, acc_ref):
        c = pl.program_id(2)
        p = _dot(ag_ref[...], bg_ref[...], 1, 1) + _dot(au_ref[...], bu_ref[...], 1, 1)

        @pl.when(c == 0)
        def _():
            acc_ref[...] = p

        @pl.when(c > 0)
        def _():
            acc_ref[...] += p

        @pl.when(c == 3)
        def _():
            o_ref[...] = acc_ref[...]

    a = pl.BlockSpec((bm, PANEL), lambda m, j, c: (m, c))
    b = pl.BlockSpec((None, bj, PANEL), lambda m, j, c: (c, j, 0))
    return pl.pallas_call(
        body, name="ffn_bwd_input", grid=(T // bm, D // bj, 4), in_specs=[a, a, b, b],
        out_specs=pl.BlockSpec((bm, bj), lambda m, j, c: (m, j)), out_shape=jax.ShapeDtypeStruct((T, D), F32),
        scratch_shapes=[pltpu.VMEM((bm, bj), F32)], compiler_params=_params(3),
    )(dgate, dup, w_gate, w_up)


def _colsum(x, *, name):
    T, W = x.shape
    rb = _rows(T)

    def body(x_ref, o_ref):
        @pl.when(pl.program_id(0) == 0)
        def _():
            o_ref[...] = jnp.zeros_like(o_ref)

        o_ref[...] += jnp.sum(x_ref[...], axis=0, keepdims=True)

    return pl.pallas_call(
        body, name=name, grid=(T // rb,), in_specs=[pl.BlockSpec((rb, W), lambda i: (i, 0))],
        out_specs=pl.BlockSpec((1, W), lambda i: (0, 0)), out_shape=jax.ShapeDtypeStruct((1, W), F32),
        compiler_params=_params(1),
    )(x)


def _merge_fwd(attn_o, gla_raw, proj, gla_norm_w):
    T = attn_o.shape[0]
    rb = _rows(T)

    def body(a_ref, g_ref, gr_ref, ga_ref, gb_ref, w_ref, o_ref, ot_ref):
        graw = g_ref[...]
        r = lax.rsqrt(jnp.mean(graw * graw, axis=-1, keepdims=True) + EPS)
        gr = gr_ref[...]
        go = graw * r * w_ref[...] * (gr * _sigmoid(gr))
        merged = _sigmoid(ga_ref[...]) * a_ref[...] + _sigmoid(gb_ref[...]) * go
        o_ref[...] = merged.astype(BF16)
        ot_ref[...] = merged.T.astype(BF16)

    def sec(k):
        return pl.BlockSpec((rb, DV), lambda i, h: (i, O_MERGE // DV + 3 * h + k))

    blk = pl.BlockSpec((rb, DV), lambda i, h: (i, h))
    return pl.pallas_call(
        body, name="merge_fwd", grid=(T // rb, GLA_H),
        in_specs=[blk, blk, sec(0), sec(1), sec(2), pl.BlockSpec((1, DV), lambda i, h: (0, 0))],
        out_specs=[blk, pl.BlockSpec((DV, rb), lambda i, h: (h, i))],
        out_shape=[jax.ShapeDtypeStruct((T, D), BF16), jax.ShapeDtypeStruct((D, T), BF16)], compiler_params=_params(2),
    )(attn_o, gla_raw, proj, proj, proj, gla_norm_w)


def _merge_bwd(dm, attn_o, gla_raw, proj, gla_norm_w):
    T = attn_o.shape[0]
    rb = _rows(T)

    def body(dm_ref, a_ref, g_ref, gr_ref, ga_ref, gb_ref, w_ref, da_ref, dg_ref, dp_ref, dw_ref):
        @pl.when((pl.program_id(0) == 0) & (pl.program_id(1) == 0))
        def _():
            dw_ref[...] = jnp.zeros_like(dw_ref)

        dmv = dm_ref[...]
        av = a_ref[...]
        graw = g_ref[...]
        gr = gr_ref[...]
        wv = w_ref[...]
        sa = _sigmoid(ga_ref[...])
        sb = _sigmoid(gb_ref[...])
        r = lax.rsqrt(jnp.mean(graw * graw, axis=-1, keepdims=True) + EPS)
        gnh = graw * r
        gn = gnh * wv
        sr = _sigmoid(gr)
        sl = gr * sr
        go = gn * sl
        da_ref[...] = dmv * sa
        dgo = dmv * sb
        dp_ref[:, 0:DV] = (dgo * gn * sr * (1.0 + gr * (1.0 - sr))).astype(BF16)
        dp_ref[:, DV:2 * DV] = (dmv * av * sa * (1.0 - sa)).astype(BF16)
        dp_ref[:, 2 * DV:3 * DV] = (dmv * go * sb * (1.0 - sb)).astype(BF16)
        dgn = dgo * sl
        dw_ref[...] += jnp.sum(dgn * gnh, axis=0, keepdims=True)
        t = dgn * wv
        dg_ref[...] = r * (t - gnh * jnp.mean(t * gnh, axis=-1, keepdims=True))

    def sec(k):
        return pl.BlockSpec((rb, DV), lambda i, h: (i, O_MERGE // DV + 3 * h + k))

    blk = pl.BlockSpec((rb, DV), lambda i, h: (i, h))
    vec = pl.BlockSpec((1, DV), lambda i, h: (0, 0))
    return pl.pallas_call(
        body, name="merge_bwd", grid=(T // rb, GLA_H),
        in_specs=[blk, blk, blk, sec(0), sec(1), sec(2), vec],
        out_specs=[blk, blk, pl.BlockSpec((rb, W_MERGE), lambda i, h: (i, O_MERGE // W_MERGE + h)), vec],
        out_shape=[jax.ShapeDtypeStruct((T, D), F32), jax.ShapeDtypeStruct((T, D), F32),
                   jax.ShapeDtypeStruct((T, MAIN), BF16), jax.ShapeDtypeStruct((1, DV), F32)],
        compiler_params=_params(2),
    )(dm, attn_o, gla_raw, proj, proj, proj, gla_norm_w)


def _attn_band(n):
    qi = lax.broadcasted_iota(jnp.int32, (GROUP * WINDOW, WINDOW), 0) & (WINDOW - 1)
    kj = lax.broadcasted_iota(jnp.int32, (GROUP * WINDOW, WINDOW), 1)
    cur = kj <= qi
    return cur, cur | (n > 0)


def _stack_heads(ref, h, scale=1.0):
    rows = jnp.concatenate(
        [ref[:, (h * GROUP + g) * HEAD_DIM:(h * GROUP + g + 1) * HEAD_DIM] for g in range(GROUP)], axis=0)
    return (rows * scale).astype(BF16) if scale != 1.0 else rows.astype(BF16)


def _stack_sinks(s_ref, h):
    return jnp.concatenate(
        [jnp.broadcast_to(s_ref[:, h * GROUP + g:h * GROUP + g + 1], (WINDOW, 1)) for g in range(GROUP)], axis=0)


def _attn_probs(qs, kp, kc, band, sink):
    cur, live = band
    s = jnp.where(cur, _dot(qs, kc, 1, 1), _dot(qs, kp, 1, 1))
    s = jnp.where(live, s, MASK_VALUE)
    m = jnp.maximum(jnp.max(s, axis=-1, keepdims=True), sink)
    e = jnp.exp(s - m)
    es = jnp.exp(sink - m)
    inv = 1.0 / (jnp.sum(e, axis=-1, keepdims=True) + es)
    return e * inv, es * inv


def _unfold(cur, x):
    mine = jnp.where(cur, x, 0.0)
    return mine.astype(BF16), (x - mine).astype(BF16)


def _attn_specs(nb, rev):
    def at(n):
        return (nb - 1 - n) if rev else n

    kcol, vcol = (O_ATTN + D) // 256, (O_ATTN + D) // 256 + 1
    q = pl.BlockSpec((WINDOW, D), lambda n: (at(n), O_ATTN // D))
    kc = pl.BlockSpec((WINDOW, 256), lambda n: (at(n), kcol))
    kp = pl.BlockSpec((WINDOW, 256), lambda n: (jnp.maximum(at(n) - 1, 0), kcol))
    vc = pl.BlockSpec((WINDOW, 256), lambda n: (at(n), vcol))
    vp = pl.BlockSpec((WINDOW, 256), lambda n: (jnp.maximum(at(n) - 1, 0), vcol))
    sk = pl.BlockSpec((1, 128), lambda n: (0, 0))
    o = pl.BlockSpec((WINDOW, D), lambda n: (at(n), 0))
    return q, kc, kp, vc, vp, sk, o


def _attn_fwd(proj, sinks):
    T = proj.shape[0]
    nb = T // WINDOW

    def body(q_ref, kc_ref, kp_ref, vc_ref, vp_ref, s_ref, o_ref):
        band = _attn_band(pl.program_id(0))
        for h in range(N_KV):
            hs = slice(h * HEAD_DIM, (h + 1) * HEAD_DIM)
            p, _ = _attn_probs(_stack_heads(q_ref, h, HEAD_DIM ** -0.5), kp_ref[:, hs].astype(BF16),
                               kc_ref[:, hs].astype(BF16), band, _stack_sinks(s_ref, h))
            p_cur, p_prev = _unfold(band[0], p)
            o = _dot(p_cur, vc_ref[:, hs].astype(BF16), 1, 0) + _dot(p_prev, vp_ref[:, hs].astype(BF16), 1, 0)
            for g in range(GROUP):
                hg = h * GROUP + g
                o_ref[:, hg * HEAD_DIM:(hg + 1) * HEAD_DIM] = o[g * WINDOW:(g + 1) * WINDOW]

    q, kc, kp, vc, vp, sk, o = _attn_specs(nb, False)
    return pl.pallas_call(
        body, name="attn_fwd", grid=(nb,), in_specs=[q, kc, kp, vc, vp, sk], out_specs=o,
        out_shape=jax.ShapeDtypeStruct((T, D), F32), compiler_params=_params(1),
    )(proj, proj, proj, proj, proj, sinks)


def _attn_bwd(proj, sinks, d_o, dproj):
    T = proj.shape[0]
    nb = T // WINDOW
    kat, vat = D, D + N_KV * HEAD_DIM

    def body(q_ref, kc_ref, kp_ref, vc_ref, vp_ref, s_ref, do_ref, _, dp_ref, ds_ref, ck_ref, cv_ref):
        i = pl.program_id(0)
        n = nb - 1 - i

        @pl.when(i == 0)
        def _():
            ck_ref[...] = jnp.zeros_like(ck_ref)
            cv_ref[...] = jnp.zeros_like(cv_ref)
            ds_ref[...] = jnp.zeros_like(ds_ref)

        band = _attn_band(n)
        for h in range(N_KV):
            hs = slice(h * HEAD_DIM, (h + 1) * HEAD_DIM)
            kp, kc = kp_ref[:, hs].astype(BF16), kc_ref[:, hs].astype(BF16)
            vp, vc = vp_ref[:, hs].astype(BF16), vc_ref[:, hs].astype(BF16)
            qs = _stack_heads(q_ref, h, HEAD_DIM ** -0.5)
            do = _stack_heads(do_ref, h)
            p, ps = _attn_probs(qs, kp, kc, band, _stack_sinks(s_ref, h))
            dp = jnp.where(band[0], _dot(do, vc, 1, 1), _dot(do, vp, 1, 1))
            delta = jnp.sum(p * dp, axis=-1, keepdims=True)
            ds_cur, ds_prev = _unfold(band[0], p * (dp - delta))
            p_cur, p_prev = _unfold(band[0], p)
            dsink = -ps * delta
            dq = ((_dot(ds_cur, kc, 1, 0) + _dot(ds_prev, kp, 1, 0)) * (HEAD_DIM ** -0.5)).astype(BF16)
            for g in range(GROUP):
                hg = h * GROUP + g
                rows = slice(g * WINDOW, (g + 1) * WINDOW)
                ds_ref[hg:hg + 1, :] += jnp.broadcast_to(jnp.sum(dsink[rows], axis=0, keepdims=True), (1, 128))
                dp_ref[:, hg * HEAD_DIM:(hg + 1) * HEAD_DIM] = dq[rows]
            dp_ref[:, kat + h * HEAD_DIM:kat + (h + 1) * HEAD_DIM] = (_dot(ds_cur, qs, 0, 0) + ck_ref[:, hs]).astype(BF16)
            dp_ref[:, vat + h * HEAD_DIM:vat + (h + 1) * HEAD_DIM] = (_dot(p_cur, do, 0, 0) + cv_ref[:, hs]).astype(BF16)
            ck_ref[:, hs] = _dot(ds_prev, qs, 0, 0)
            cv_ref[:, hs] = _dot(p_prev, do, 0, 0)

    q, kc, kp, vc, vp, sk, o = _attn_specs(nb, True)
    return pl.pallas_call(
        body, name="attn_bwd", grid=(nb,), in_specs=[q, kc, kp, vc, vp, sk, o, ANY],
        out_specs=[pl.BlockSpec((WINDOW, W_ATTN), lambda n: (nb - 1 - n, O_ATTN // W_ATTN)),
                   pl.BlockSpec((N_Q, 128), lambda n: (0, 0))],
        out_shape=[jax.ShapeDtypeStruct((T, MAIN), BF16), jax.ShapeDtypeStruct((N_Q, 128), F32)],
        scratch_shapes=[pltpu.VMEM((WINDOW, 256), F32), pltpu.VMEM((WINDOW, 256), F32)],
        input_output_aliases={7: 0}, compiler_params=_params(1),
    )(proj, proj, proj, proj, proj, sinks, d_o, dproj)


def _split3(x):
    hi = x.astype(BF16)
    r1 = x - hi.astype(F32)
    mid = r1.astype(BF16)
    lo = (r1 - mid.astype(F32)).astype(BF16)
    return hi, mid, lo


def _tri_sum(tri, x):
    hi, mid, lo = _split3(x)
    return _dot(tri, hi, 1, 0) + _dot(tri, mid, 1, 0) + _dot(tri, lo, 1, 0)


def _chunk_masks(rb):
    r = lax.broadcasted_iota(jnp.int32, (rb, rb), 0)
    c = lax.broadcasted_iota(jnp.int32, (rb, rb), 1)
    same = (r ^ c) < CHUNK
    return same & (r >= c), same & (r <= c)


def _per_chunk_rows(x, per, pick):
    return jnp.concatenate(
        [jnp.broadcast_to(pick(x[i * CHUNK:(i + 1) * CHUNK]), (CHUNK, x.shape[1])) for i in range(per)], axis=0)


def _gla_block(q, k, lr, w2, b, lower):
    per = q.shape[0] // CHUNK
    logit = _dot(lr.astype(BF16), w2.astype(BF16), 1, 0) + b
    la = (jnp.minimum(logit, 0.0) - jnp.log(1.0 + jnp.exp(-jnp.abs(logit)))) * (1.0 / GATE_NORM)
    g = _tri_sum(lower, la)
    gl = _per_chunk_rows(g, per, lambda c: c[CHUNK - 1:CHUNK])
    eg = jnp.exp(g)
    qd = q * (DK ** -0.5) * eg
    ki = k * jnp.exp(-g)
    ke = k * jnp.exp(gl - g)
    return logit, g, gl, eg, qd, ki, ke


def _gla_rows(T):
    return _pick(T, (256, 128, 64))


def _gla_fwd(proj, proj_lr, w2p, gate_b):
    T = proj.shape[0]
    rb = _gla_rows(T)
    per = rb // CHUNK

    def body(q_ref, k_ref, v_ref, lr_ref, w2_ref, b_ref, o_ref, st_ref, s_scr):
        @pl.when(pl.program_id(1) == 0)
        def _():
            s_scr[...] = jnp.zeros_like(s_scr)

        low, _ = _chunk_masks(rb)
        _, _, gl, _, qd, ki, ke = _gla_block(q_ref[...], k_ref[...], lr_ref[...], w2_ref[...], b_ref[...],
                                             low.astype(BF16))
        v = v_ref[...].astype(BF16)
        qdb, keb = qd.astype(BF16), ke.astype(BF16)
        att = jnp.where(low, _dot(qdb, ki.astype(BF16), 1, 1), 0.0).astype(BF16)
        o_intra = _dot(att, v, 1, 0)
        st = s_scr[...]
        for i in range(per):
            rows = slice(i * CHUNK, (i + 1) * CHUNK)
            st_ref[0, i] = st
            o_ref[rows, :] = o_intra[rows] + _dot(qdb[rows], st.astype(BF16), 1, 1)
            st = st * jnp.exp(gl[i * CHUNK:i * CHUNK + 1]) + _dot(v[rows], keb[rows], 0, 0)
        s_scr[...] = st

    return pl.pallas_call(
        body, name="gla_fwd", grid=(GLA_H, T // rb),
        in_specs=[pl.BlockSpec((rb, DK), lambda h, n: (n, (O_GLA + W_GLA * h) // DK)),
                  pl.BlockSpec((rb, DK), lambda h, n: (n, (O_GLA + W_GLA * h) // DK + 1)),
                  pl.BlockSpec((rb, DV), lambda h, n: (n, (O_GLA + W_GLA * h) // DV + 1)),
                  pl.BlockSpec((rb, LRP), lambda h, n: (n, 0)),
                  pl.BlockSpec((LRP, DK), lambda h, n: (0, h)),
                  pl.BlockSpec((1, DK), lambda h, n: (0, h))],
        out_specs=[pl.BlockSpec((rb, DV), lambda h, n: (n, h)),
                   pl.BlockSpec((1, per, DV, DK), lambda h, n: (h, n, 0, 0))],
        out_shape=[jax.ShapeDtypeStruct((T, GLA_H * DV), F32),
                   jax.ShapeDtypeStruct((GLA_H, T // CHUNK, DV, DK), F32)],
        scratch_shapes=[pltpu.VMEM((DV, DK), F32)], compiler_params=_params(2),
    )(proj, proj, proj, proj_lr, w2p, gate_b)


def _gla_bwd(proj, proj_lr, w2p, gate_b, states, d_o, dproj):
    T = proj.shape[0]
    rb = _gla_rows(T)
    per = rb // CHUNK
    nblk = T // rb

    def body(q_ref, k_ref, v_ref, lr_ref, w2_ref, b_ref, st_ref, do_ref, _, dp_ref, dl_ref, ds_scr):
        @pl.when(pl.program_id(1) == 0)
        def _():
            ds_scr[...] = jnp.zeros_like(ds_scr)

        low, up = _chunk_masks(rb)
        logit, g, gl, eg, qd, ki, ke = _gla_block(q_ref[...], k_ref[...], lr_ref[...], w2_ref[...], b_ref[...],
                                                  low.astype(BF16))
        v = v_ref[...].astype(BF16)
        do = do_ref[...].astype(BF16)
        qdb, kib, keb = qd.astype(BF16), ki.astype(BF16), ke.astype(BF16)
        att = jnp.where(low, _dot(qdb, kib, 1, 1), 0.0).astype(BF16)
        datt = jnp.where(low, _dot(do, v, 1, 1), 0.0).astype(BF16)
        dv_intra = _dot(att, do, 0, 0)
        dqd_intra = _dot(datt, kib, 1, 0)
        dki = _dot(datt, qdb, 0, 0)
        ds = ds_scr[...]
        dqd, dke, dgl = [None] * per, [None] * per, [None] * per
        for i in reversed(range(per)):
            rows = slice(i * CHUNK, (i + 1) * CHUNK)
            sp = st_ref[0, i]
            dsb = ds.astype(BF16)
            dp_ref[rows, 2 * DK:] = (dv_intra[rows] + _dot(keb[rows], dsb, 1, 1)).astype(BF16)
            dqd[i] = dqd_intra[rows] + _dot(do[rows], sp.astype(BF16), 1, 0)
            dke[i] = _dot(v[rows], dsb, 1, 0)
            decay = jnp.exp(gl[i * CHUNK:i * CHUNK + 1])
            dgl[i] = (jnp.sum(dke[i] * ke[rows], axis=0, keepdims=True)
                      + jnp.sum(ds * sp, axis=0, keepdims=True) * decay)
            ds = ds * decay + _dot(do[rows], qdb[rows], 0, 0)
        ds_scr[...] = ds
        dqd = jnp.concatenate(dqd, axis=0)
        dke = jnp.concatenate(dke, axis=0)
        dgl = jnp.concatenate([jnp.broadcast_to(d, (CHUNK, DK)) for d in dgl], axis=0)
        dp_ref[:, 0:DK] = (dqd * (DK ** -0.5) * eg).astype(BF16)
        dp_ref[:, DK:2 * DK] = (dki * jnp.exp(-g) + dke * jnp.exp(gl - g)).astype(BF16)
        dg = dqd * qd - dki * ki - dke * ke
        dla = _tri_sum(up.astype(BF16), dg) + dgl
        dl_ref[...] = dla * (1.0 / GATE_NORM) * (1.0 - _sigmoid(logit))

    def rev(n):
        return nblk - 1 - n

    return pl.pallas_call(
        body, name="gla_bwd", grid=(GLA_H, nblk),
        in_specs=[pl.BlockSpec((rb, DK), lambda h, n: (rev(n), (O_GLA + W_GLA * h) // DK)),
                  pl.BlockSpec((rb, DK), lambda h, n: (rev(n), (O_GLA + W_GLA * h) // DK + 1)),
                  pl.BlockSpec((rb, DV), lambda h, n: (rev(n), (O_GLA + W_GLA * h) // DV + 1)),
                  pl.BlockSpec((rb, LRP), lambda h, n: (rev(n), 0)),
                  pl.BlockSpec((LRP, DK), lambda h, n: (0, h)),
                  pl.BlockSpec((1, DK), lambda h, n: (0, h)),
                  pl.BlockSpec((1, per, DV, DK), lambda h, n: (h, rev(n), 0, 0)),
                  pl.BlockSpec((rb, DV), lambda h, n: (rev(n), h)),
                  ANY],
        out_specs=[pl.BlockSpec((rb, W_GLA), lambda h, n: (rev(n), O_GLA // W_GLA + h)),
                   pl.BlockSpec((rb, DK), lambda h, n: (rev(n), h))],
        out_shape=[jax.ShapeDtypeStruct((T, MAIN), BF16), jax.ShapeDtypeStruct((T, GLA_H * DK), F32)],
        scratch_shapes=[pltpu.VMEM((DV, DK), F32)], input_output_aliases={8: 0}, compiler_params=_params(2),
    )(proj, proj, proj, proj_lr, w2p, gate_b, states, d_o, dproj)


def _local_step(x, target, p, reduce_part):
    u, u_t = _rms_fwd(x, p["norm1_w"], name="rms1_fwd")
    proj = _matmul(u, p["w_in_main"], name="mm_proj")
    proj_lr = _matmul(u, p["w_in_lr"], name="mm_proj_lr")
    attn_o = _attn_fwd(proj, p["sinks"])
    gla_raw, states = _gla_fwd(proj, proj_lr, p["w2p"], p["gate_b"])
    merged, merged_t = _merge_fwd(attn_o, gla_raw, proj, p["gla_norm_w"])
    h1 = _matmul(merged, p["w_out"], add=x, name="mm_out")
    v2, v2_t = _rms_fwd(h1, p["norm2_w"], name="rms2_fwd")
    gate, up, ff, ff_t = _ffn_fwd(v2, p["w_gate"], p["w_up"])
    h2 = _matmul(ff, p["w_down"], add=h1, name="mm_down")
    loss, dh2, dh2_b, g_final = _final_loss(h2, p["final_norm_w"], target)

    dgate, dup = _ffn_bwd_hidden(dh2_b, p["w_down"], gate, up)
    reduce_part("w_down", _matmul(ff_t, dh2_b, out_dtype=BF16, bj=D, name="mm_gdown").reshape(4, FFN // 4, D))
    reduce_part("w_gate", _matmul(v2_t, dgate, out_dtype=BF16, bj=PANEL, out_slots=True, name="mm_ggate"))
    reduce_part("w_up", _matmul(v2_t, dup, out_dtype=BF16, bj=PANEL, out_slots=True, name="mm_gup"))
    dv2 = _ffn_bwd_input(dgate, dup, p["w_gate"], p["w_up"])
    dh1, dh1_b, g_norm2 = _rms_bwd(dv2, h1, p["norm2_w"], dh2, name="rms2_bwd")
    dmerged = _matmul(dh1_b, p["w_out"], tb=True, bj=D, name="mm_dmerged")
    reduce_part("w_out", _matmul(merged_t, dh1_b, out_dtype=BF16, bj=D, name="mm_gout").reshape(4, D // 4, D))
    d_attn, d_gla, dproj, g_gla_norm = _merge_bwd(dmerged, attn_o, gla_raw, proj, p["gla_norm_w"])
    dproj, dlogit = _gla_bwd(proj, proj_lr, p["w2p"], p["gate_b"], states, d_gla, dproj)
    dproj, g_sinks = _attn_bwd(proj, p["sinks"], d_attn, dproj)
    g_gate_b = _colsum(dlogit, name="colsum_gate_b")
    g_w2 = _matmul(proj_lr, dlogit, ta=True, name="mm_gw2")
    dproj_lr = _matmul(dlogit, p["w2p"], tb=True, out_dtype=BF16, name="mm_dlr")
    g_in_main = _matmul(u_t, dproj, out_dtype=BF16, name="mm_gin")
    g_in_lr = _matmul(u_t, dproj_lr, out_dtype=BF16, name="mm_gin_lr")
    du = _matmul(dproj, p["w_in_main"], tb=True, bm=_pick(x.shape[0], (1024, 256)), bj=1024, bc=3200, name="mm_du")
    du = _matmul(dproj_lr, p["w_in_lr"], tb=True, add=du, name="mm_du_lr")
    dx, _, g_norm1 = _rms_bwd(du, x, p["norm1_w"], dh1, name="rms1_bwd")
    grads = dict(norm1_w=g_norm1, w_in_main=g_in_main, w_in_lr=g_in_lr, w2=g_w2[:RANK], gate_b=g_gate_b,
                 sinks=g_sinks[:, 0].reshape(1, N_Q), gla_norm_w=g_gla_norm, norm2_w=g_norm2, final_norm_w=g_final)
    return loss, dx, grads


def _place():
    x, y, c = lax.axis_index("x"), lax.axis_index("y"), lax.axis_index("c")
    chips = [(1 - x, y), (x, 1 - y), (1 - x, 1 - y)]
    return x, y, c, chips


def _hbm_shape(s, dt):
    return jax.ShapeDtypeStruct(s, dt)


ID_SIBLING = 9
ID_GATHER_IN = 10


def _handshake(peers):
    barrier = pltpu.get_barrier_semaphore()
    for peer in peers:
        pl.semaphore_signal(barrier, inc=1, device_id=peer, device_id_type=MESH)
    pl.semaphore_wait(barrier, len(peers))


def _launch_copies(body, args, out_shape, sems, *, name, collective_id=None):
    if collective_id is None:
        return pl.pallas_call(
            body, name=name, in_specs=[ANY] * len(args), out_specs=[ANY] * len(out_shape), out_shape=out_shape,
            scratch_shapes=sems)(*args)
    return pl.kernel(
        body, name=name, out_type=out_shape, mesh=plsc.ScalarSubcoreMesh(axis_name="sequencer", num_cores=1),
        scratch_types=sems, compiler_params=pltpu.CompilerParams(collective_id=collective_id))(*args)


def _gather_shards(shards, *, name, collective_id=None, after=()):
    n = len(shards)
    first_out = n + len(after)

    def body(*refs):
        ins, outs = refs[:n], refs[first_out:first_out + n]
        ici_send, ici_recv, d2d_send, d2d_recv, local_sem = refs[first_out + n:]
        x, y, c, chips = _place()
        me = 2 * x + y
        sibling = (x, y, 1 - c)
        if collective_id is not None:
            _handshake([sibling] + [(*chip, c) for chip in chips])

        def half(w, slot, hc):
            r2 = shards[w].shape[0] // 2
            return outs[w].at[slot, pl.ds(hc * r2, r2), :]

        locals_ = [pltpu.make_async_copy(ins[w], outs[w].at[me], local_sem.at[w]) for w in range(n)]
        for cp in locals_:
            cp.start()
        sends = []
        for w in range(n):
            r2 = shards[w].shape[0] // 2
            for j, chip in enumerate(chips):
                cp = pltpu.make_async_remote_copy(
                    src_ref=ins[w].at[pl.ds(c * r2, r2), :], dst_ref=half(w, me, c),
                    send_sem=ici_send.at[w * 3 + j], recv_sem=ici_recv.at[w * 3 + j],
                    device_id=(*chip, c), device_id_type=MESH)
                cp.start()
                sends.append(cp)
        for w in range(n):
            for j, chip in enumerate(chips):
                slot = 2 * chip[0] + chip[1]
                got = half(w, slot, c)
                pltpu.make_async_remote_copy(
                    src_ref=got, dst_ref=got, send_sem=ici_send.at[w * 3 + j], recv_sem=ici_recv.at[w * 3 + j],
                    device_id=(*chip, c), device_id_type=MESH).wait_recv()
                cp = pltpu.make_async_remote_copy(
                    src_ref=got, dst_ref=got, send_sem=d2d_send.at[w * 3 + j], recv_sem=d2d_recv.at[w * 3 + j],
                    device_id=sibling, device_id_type=MESH)
                cp.start()
                sends.append(cp)
        for w in range(n):
            for j, chip in enumerate(chips):
                slot = 2 * chip[0] + chip[1]
                got = half(w, slot, 1 - c)
                pltpu.make_async_remote_copy(
                    src_ref=got, dst_ref=got, send_sem=d2d_send.at[w * 3 + j], recv_sem=d2d_recv.at[w * 3 + j],
                    device_id=sibling, device_id_type=MESH).wait_recv()
        for cp in sends:
            cp.wait_send()
        for cp in locals_:
            cp.wait()

    return _launch_copies(
        body, list(shards) + list(after), [_hbm_shape((4,) + s.shape, s.dtype) for s in shards],
        [pltpu.SemaphoreType.DMA((3 * n,)), pltpu.SemaphoreType.DMA((3 * n,)), pltpu.SemaphoreType.DMA((3 * n,)),
         pltpu.SemaphoreType.DMA((3 * n,)), pltpu.SemaphoreType.DMA((n,))],
        name=name, collective_id=collective_id)


def _gather_staged(w, small):
    R, C = w.shape
    R2 = R // 2
    rb = 256
    per = R2 // rb
    r2 = small.shape[0] // 2

    def body(w_ref, s_ref, out_ref, outs_ref, stage, ici_send, ici_recv, fwd_send, fwd_recv, in_sem, put_sem,
             push_send, push_recv):
        x, y, c, chips = _place()
        me = 2 * x + y
        sibling = (x, y, 1 - c)
        slots = [2 * chip[0] + chip[1] for chip in chips]
        _handshake([sibling] + [(*chip, c) for chip in chips])

        def ici(j, src, dst, k):
            return pltpu.make_async_remote_copy(
                src_ref=src, dst_ref=dst, send_sem=ici_send.at[k], recv_sem=ici_recv.at[k],
                device_id=(*chips[j], c), device_id_type=MESH)

        sends = []
        for j in range(3):
            sends.append(ici(j, w_ref.at[pl.ds(c * R2, R2), :], out_ref.at[me, pl.ds(c * R2, R2), :], j))
            sends.append(ici(j, s_ref.at[pl.ds(c * r2, r2), :], outs_ref.at[me, pl.ds(c * r2, r2), :], 3 + j))
        for cp in sends:
            cp.start()
        mine = pltpu.make_async_copy(s_ref, outs_ref.at[me], put_sem.at[2])
        mine.start()

        def own(k, to_vmem):
            rows = pl.ds(k * rb, rb)
            if to_vmem:
                return pltpu.make_async_copy(w_ref.at[rows, :], stage.at[k % 2], in_sem.at[k % 2])
            return pltpu.make_async_copy(stage.at[k % 2], out_ref.at[me, rows, :], put_sem.at[k % 2])

        for k in range(R // rb):
            if k >= 2:
                own(k - 2, False).wait()
            own(k, True).start()
            own(k, True).wait()
            own(k, False).start()
        for k in range(R // rb - 2, R // rb):
            own(k, False).wait()

        def block(j, k, hc):
            return out_ref.at[slots[j], pl.ds(hc * R2 + k * rb, rb), :]

        def push(t, hc):
            j, k = divmod(t, per)
            return pltpu.make_async_remote_copy(
                src_ref=stage.at[t % 2], dst_ref=block(j, k, hc), send_sem=push_send.at[t % 2],
                recv_sem=push_recv.at[t], device_id=sibling, device_id_type=MESH)

        passed = []
        for j in range(3):
            ici(j, w_ref.at[pl.ds(c * R2, R2), :], out_ref.at[slots[j], pl.ds(c * R2, R2), :], j).wait_recv()
            for k in range(per):
                t = j * per + k
                if t >= 2:
                    push(t - 2, c).wait_send()
                fetch = pltpu.make_async_copy(block(j, k, c), stage.at[t % 2], in_sem.at[t % 2])
                fetch.start()
                fetch.wait()
                push(t, c).start()
            got = outs_ref.at[slots[j], pl.ds(c * r2, r2), :]
            ici(j, got, got, 3 + j).wait_recv()
            fw = pltpu.make_async_remote_copy(
                src_ref=got, dst_ref=got, send_sem=fwd_send.at[j], recv_sem=fwd_recv.at[j],
                device_id=sibling, device_id_type=MESH)
            fw.start()
            passed.append(fw)
        for t in range(3 * per - 2, 3 * per):
            push(t, c).wait_send()
        for t in range(3 * per):
            push(t, 1 - c).wait_recv()
        for j in range(3):
            theirs = outs_ref.at[slots[j], pl.ds((1 - c) * r2, r2), :]
            pltpu.make_async_remote_copy(
                src_ref=theirs, dst_ref=theirs, send_sem=fwd_send.at[j], recv_sem=fwd_recv.at[j],
                device_id=sibling, device_id_type=MESH).wait_recv()
        for cp in sends + passed:
            cp.wait_send()
        mine.wait()

    dma = pltpu.SemaphoreType.DMA
    return pl.pallas_call(
        body, name="gather_in", in_specs=[ANY, ANY], out_specs=[ANY, ANY],
        out_shape=[_hbm_shape((4, R, C), w.dtype), _hbm_shape((4,) + small.shape, small.dtype)],
        scratch_shapes=[pltpu.VMEM((2, rb, C), w.dtype), dma((6,)), dma((6,)), dma((3,)), dma((3,)), dma((2,)),
                        dma((3,)), dma((2,)), dma((3 * per,))],
        compiler_params=pltpu.CompilerParams(vmem_limit_bytes=VMEM_LIMIT, collective_id=ID_GATHER_IN),
    )(w, small)


def _pair_blocks(R2):
    return _pick(R2, (256, 352, 128, 64, 32, 16))


def _pair_reduce(part, *, name):
    _, R, C = part.shape
    R2 = R // 2
    rb = _pick(R2, (512, 352, 256, 128, 64, 32, 16))
    nblk = R2 // rb
    steps = [(s, i) for s in range(4) for i in range(nblk)]
    n = len(steps)

    def body(part_ref, sums_ref, own_buf, oth_buf, rcv_buf, out_buf, own_sem, oth_sem, out_sem, send_sem, recv_sem):
        x, y, c, _ = _place()
        sibling = (x, y, 1 - c)
        _handshake([sibling])

        def fetch(t, half, buf, sem):
            s, i = steps[t]
            return pltpu.make_async_copy(part_ref.at[s, pl.ds(half * R2 + i * rb, rb), :], buf.at[t % 2], sem.at[t % 2])

        def push(t):
            return pltpu.make_async_remote_copy(
                src_ref=oth_buf.at[t % 2], dst_ref=rcv_buf.at[t % 3], send_sem=send_sem.at[t % 2],
                recv_sem=recv_sem.at[t % 2], device_id=sibling, device_id_type=MESH)

        def store(t):
            s, i = steps[t]
            return pltpu.make_async_copy(out_buf.at[t % 2], sums_ref.at[s, pl.ds(i * rb, rb), :], out_sem.at[t % 2])

        fetch(0, c, own_buf, own_sem).start()
        fetch(0, 1 - c, oth_buf, oth_sem).start()
        fetch(0, 1 - c, oth_buf, oth_sem).wait()
        push(0).start()
        for t in range(n):
            if t + 1 < n:
                fetch(t + 1, c, own_buf, own_sem).start()
                fetch(t + 1, 1 - c, oth_buf, oth_sem).start()
            fetch(t, c, own_buf, own_sem).wait()
            push(t).wait()
            if t + 1 < n:
                fetch(t + 1, 1 - c, oth_buf, oth_sem).wait()
                push(t + 1).start()
            if t >= 2:
                store(t - 2).wait()
            out_buf[t % 2] = (own_buf[t % 2].astype(F32) + rcv_buf[t % 3].astype(F32)).astype(BF16)
            store(t).start()
        for t in range(max(n - 2, 0), n):
            store(t).wait()

    buf = pltpu.VMEM((2, rb, C), BF16)
    sem2 = pltpu.SemaphoreType.DMA((2,))
    return pl.pallas_call(
        body, name=name, in_specs=[ANY], out_specs=ANY, out_shape=_hbm_shape((4, R2, C), BF16),
        scratch_shapes=[buf, buf, pltpu.VMEM((3, rb, C), BF16), buf, sem2, sem2, sem2, sem2, sem2],
        compiler_params=pltpu.CompilerParams(vmem_limit_bytes=VMEM_LIMIT, collective_id=ID_SIBLING),
    )(part)


def _chip_exchange(sums, *, name, collective_id=None):
    n = len(sums)

    def body(*refs):
        ins, outs = refs[:n], refs[n:2 * n]
        send_sem, recv_sem, local_sem = refs[2 * n:]
        x, y, c, chips = _place()
        me = 2 * x + y
        if collective_id is not None:
            _handshake([(*chip, c) for chip in chips])
        cps = []
        for w in range(n):
            lc = pltpu.make_async_copy(ins[w].at[me], outs[w].at[me], local_sem.at[w])
            lc.start()
            cps.append(lc)
            for j, chip in enumerate(chips):
                slot = 2 * chip[0] + chip[1]
                rc = pltpu.make_async_remote_copy(
                    src_ref=ins[w].at[slot], dst_ref=outs[w].at[me],
                    send_sem=send_sem.at[w * 3 + j], recv_sem=recv_sem.at[w * 3 + j],
                    device_id=(*chip, c), device_id_type=MESH)
                rc.start()
                cps.append(rc)
        for w in range(n):
            for j, chip in enumerate(chips):
                slot = 2 * chip[0] + chip[1]
                pltpu.make_async_remote_copy(
                    src_ref=ins[w].at[slot], dst_ref=outs[w].at[slot],
                    send_sem=send_sem.at[w * 3 + j], recv_sem=recv_sem.at[w * 3 + j],
                    device_id=(*chip, c), device_id_type=MESH).wait_recv()
        for w in range(n):
            cps[w * 4].wait()
            for j in range(3):
                cps[w * 4 + 1 + j].wait_send()

    return _launch_copies(
        body, sums, [_hbm_shape(s.shape, s.dtype) for s in sums],
        [pltpu.SemaphoreType.DMA((3 * n,)), pltpu.SemaphoreType.DMA((3 * n,)), pltpu.SemaphoreType.DMA((n,))],
        name=name, collective_id=collective_id)


def _sum_join(recv, *, name):
    _, R2, C = recv.shape
    rb = _pair_blocks(R2)
    nblk = R2 // rb

    def body(recv_ref, out_ref, in_buf, acc_buf, in_sem, loc_sem, send_sem, recv_sem):
        x, y, c, _ = _place()
        sibling = (x, y, 1 - c)
        _handshake([sibling])

        def fetch(t):
            return pltpu.make_async_copy(recv_ref.at[:, pl.ds(t * rb, rb), :], in_buf.at[t % 2], in_sem.at[t % 2])

        def rows(t, half):
            return out_ref.at[pl.ds(half * R2 + t * rb, rb), :]

        def put_local(t):
            return pltpu.make_async_copy(acc_buf.at[t % 2], rows(t, c), loc_sem.at[t])

        def put_remote(t, half):
            return pltpu.make_async_remote_copy(
                src_ref=acc_buf.at[t % 2], dst_ref=rows(t, half), send_sem=send_sem.at[t], recv_sem=recv_sem.at[t],
                device_id=sibling, device_id_type=MESH)

        fetch(0).start()
        for t in range(nblk):
            if t + 1 < nblk:
                fetch(t + 1).start()
            fetch(t).wait()
            if t >= 2:
                put_local(t - 2).wait()
                put_remote(t - 2, c).wait_send()
            acc = in_buf[t % 2, 0].astype(F32)
            for s in range(1, 4):
                acc = acc + in_buf[t % 2, s].astype(F32)
            acc_buf[t % 2] = acc
            put_local(t).start()
            put_remote(t, c).start()
        for t in range(max(nblk - 2, 0), nblk):
            put_local(t).wait()
            put_remote(t, c).wait_send()
        for t in range(nblk):
            put_remote(t, 1 - c).wait_recv()

    semn = pltpu.SemaphoreType.DMA((nblk,))
    return pl.pallas_call(
        body, name=name, in_specs=[ANY], out_specs=ANY, out_shape=_hbm_shape((2 * R2, C), F32),
        scratch_shapes=[pltpu.VMEM((2, 4, rb, C), BF16), pltpu.VMEM((2, rb, C), F32),
                        pltpu.SemaphoreType.DMA((2,)), semn, semn, semn],
        compiler_params=pltpu.CompilerParams(vmem_limit_bytes=VMEM_LIMIT, collective_id=ID_SIBLING),
    )(recv)


def _sum_small(pack):
    R, C = pack.shape

    def body(in_ref, out_ref, all_ref, send_sem, recv_sem):
        x, y, c, _ = _place()
        me = 4 * x + 2 * y + c
        all_ref[me] = in_ref[...]
        cps = []
        for k in range(1, 8):
            peer = (x ^ (k >> 2), y ^ ((k >> 1) & 1), c ^ (k & 1))
            cp = pltpu.make_async_remote_copy(
                src_ref=in_ref, dst_ref=all_ref.at[me], send_sem=send_sem.at[k - 1], recv_sem=recv_sem.at[k - 1],
                device_id=peer, device_id_type=MESH)
            cp.start()
            cps.append(cp)
        for k in range(1, 8):
            peer = (x ^ (k >> 2), y ^ ((k >> 1) & 1), c ^ (k & 1))
            slot = 4 * peer[0] + 2 * peer[1] + peer[2]
            pltpu.make_async_remote_copy(
                src_ref=in_ref, dst_ref=all_ref.at[slot], send_sem=send_sem.at[k - 1], recv_sem=recv_sem.at[k - 1],
                device_id=peer, device_id_type=MESH).wait_recv()
        for cp in cps:
            cp.wait_send()
        acc = all_ref[0]
        for d in range(1, 8):
            acc = acc + all_ref[d]
        out_ref[...] = acc

    return pl.pallas_call(
        body, name="sum_small", in_specs=[pl.BlockSpec(memory_space=pltpu.VMEM)],
        out_specs=pl.BlockSpec(memory_space=pltpu.VMEM), out_shape=jax.ShapeDtypeStruct((R, C), F32),
        scratch_shapes=[pltpu.VMEM((8, R, C), F32), pltpu.SemaphoreType.DMA((7,)), pltpu.SemaphoreType.DMA((7,))],
    )(pack)


ADAMW_BLOCK_BYTES = 2 * 1024 * 1024


def _adamw_block(R, C):
    padded = -(-C // 128) * 128
    rows = [rb for rb in range(8, R + 1, 8) if R % rb == 0 and rb * padded * 4 <= ADAMW_BLOCK_BYTES]
    if rows or R * padded * 4 <= ADAMW_BLOCK_BYTES:
        return (max(rows) if rows else R), C
    cols = [cb for cb in range(128, C + 1, 128) if C % cb == 0 and R * cb * 4 <= ADAMW_BLOCK_BYTES]
    return R, max(cols)


def _adamw(w, g, m, v, *, name):
    R, C = w.shape
    rb, cb = _adamw_block(R, C)
    c1 = 1.0 / (1.0 - B1 ** STEP)
    c2 = 1.0 / (1.0 - B2 ** STEP)

    def body(w_ref, g_ref, m_ref, v_ref, d_ref, nm_ref, nv_ref):
        gv = g_ref[...]
        m2 = B1 * m_ref[...] + (1.0 - B1) * gv
        v2 = B2 * v_ref[...] + (1.0 - B2) * gv * gv
        nm_ref[...] = m2
        nv_ref[...] = v2
        d_ref[...] = -LR * ((m2 * c1) / (jnp.sqrt(v2 * c2) + ADAM_EPS) + WD * w_ref[...])

    blk = pl.BlockSpec((rb, cb), lambda i, j: (i, j))
    return pl.pallas_call(
        body, name=name, grid=(R // rb, C // cb), in_specs=[blk] * 4, out_specs=[blk] * 3,
        out_shape=[jax.ShapeDtypeStruct((R, C), F32)] * 3, compiler_params=_params(2),
    )(w, g, m, v)


SMALL = (("norm1_w", D), ("norm2_w", D), ("final_norm_w", D), ("gate_b", GLA_H * DK), ("gla_norm_w", DV), ("sinks", N_Q))
PACK_W = 1024


def _pack_small(vals, w2, loss):
    rows = []
    for name, width in SMALL:
        v = vals[name].reshape(-1)
        rows.append(jnp.pad(v, (0, (-width) % PACK_W)).reshape(-1, PACK_W))
    rows.append(w2)
    rows.append(jnp.broadcast_to(loss.reshape(1, 1), (1, PACK_W)))
    pack = jnp.concatenate(rows, axis=0)
    return jnp.pad(pack, ((0, 32 - pack.shape[0]), (0, 0)))


def _unpack_small(pack):
    out, r = {}, 0
    for name, width in SMALL:
        nr = -(-width // PACK_W)
        out[name] = pack[r:r + nr].reshape(-1)[:width]
        r += nr
    out["w2"] = pack[r:r + RANK]
    out["loss"] = pack[r + RANK, 0]
    return out


def kernel(x, norm1_w, w_in, gla_gate_w2, gla_gate_b, attn_sinks, gla_norm_w, w_out, norm2_w, w_ffn_gate, w_ffn_up, w_ffn_down, final_norm_w, loss_target, m_norm1_w, m_w_in, m_gla_gate_w2, m_gla_gate_b, m_attn_sinks, m_gla_norm_w, m_w_out, m_norm2_w, m_w_ffn_gate, m_w_ffn_up, m_w_ffn_down, m_final_norm_w, v_norm1_w, v_w_in, v_gla_gate_w2, v_gla_gate_b, v_attn_sinks, v_gla_norm_w, v_w_out, v_norm2_w, v_w_ffn_gate, v_w_ffn_up, v_w_ffn_down, v_final_norm_w):
    chip = 2 * lax.axis_index("x") + lax.axis_index("y")
    w_in_s, w_out_s, w_gate_s, w_up_s, w_down_s, w2_s = (
        w_in[0], w_out[0], w_ffn_gate[0], w_ffn_up[0], w_ffn_down[0], gla_gate_w2[0])
    CS = SHARD_COLS

    g_in, g_w2 = _gather_staged(w_in_s.astype(BF16), w2_s)
    g_out, = _gather_shards([w_out_s.astype(BF16)], name="gather_out", collective_id=1, after=[g_w2])
    g_gate, g_up = _gather_shards([w_gate_s.astype(BF16), w_up_s.astype(BF16)], name="gather_gate_up",
                                  collective_id=7, after=[g_w2])
    g_down, = _gather_shards([w_down_s.astype(BF16)], name="gather_down", collective_id=8, after=[g_w2])
    w_main, w_lr = _w_in_to_main(g_in)
    w2_full = jnp.transpose(g_w2, (1, 0, 2)).reshape(RANK, GLA_H * DK)
    p = dict(
        norm1_w=norm1_w, norm2_w=norm2_w, final_norm_w=final_norm_w.reshape(1, D), gate_b=gla_gate_b,
        gla_norm_w=gla_norm_w, sinks=jnp.pad(attn_sinks, ((0, 0), (0, 128 - N_Q))),
        w_in_main=w_main, w_in_lr=w_lr,
        w2p=jnp.pad(w2_full, ((0, LRP - RANK), (0, 0))).astype(BF16),
        w_out=g_out.reshape(D, D),
        w_gate=g_gate, w_up=g_up, w_down=g_down.reshape(FFN, D),
    )

    tags = ["w_in", "w_out", "w_gate", "w_up", "w_down"]
    recv = {}

    def reduce_part(tag, part):
        sums = _pair_reduce(part, name="pair_reduce_" + tag)
        recv[tag], = _chip_exchange([sums], name="chip_exchange_" + tag, collective_id=2 + tags.index(tag))

    loss_blk, dx, g = _local_step(x[0], loss_target[0], p, reduce_part)
    reduce_part("w_in", _main_to_shards(g["w_in_main"], g["w_in_lr"]))
    big = [_sum_join(recv[tg], name="sum_join_" + tg) for tg in tags]

    small = _unpack_small(_sum_small(_pack_small(g, g["w2"], loss_blk[0, 0])))
    loss = small["loss"]
    g_w2_mine = lax.dynamic_slice_in_dim(small["w2"], chip * (GLA_H * DK // 4), GLA_H * DK // 4, axis=1)

    grads = dict(
        norm1_w=small["norm1_w"].reshape(1, D), w_in=big[0], gla_gate_w2=g_w2_mine,
        gla_gate_b=small["gate_b"].reshape(1, -1), attn_sinks=small["sinks"].reshape(1, N_Q),
        gla_norm_w=small["gla_norm_w"].reshape(1, DV), w_out=big[1], norm2_w=small["norm2_w"].reshape(1, D),
        w_ffn_gate=big[2], w_ffn_up=big[3], w_ffn_down=big[4], final_norm_w=small["final_norm_w"].reshape(1, D))
    def lin(a):
        return jnp.transpose(a.reshape(D, CS))

    def unlin(a):
        return jnp.transpose(a)

    grads["w_in"] = lin(big[0])
    weights = dict(
        norm1_w=(norm1_w, m_norm1_w, v_norm1_w), w_in=(lin(w_in), lin(m_w_in), lin(v_w_in)),
        gla_gate_w2=(w2_s, m_gla_gate_w2[0], v_gla_gate_w2[0]), gla_gate_b=(gla_gate_b, m_gla_gate_b, v_gla_gate_b),
        attn_sinks=(attn_sinks, m_attn_sinks, v_attn_sinks), gla_norm_w=(gla_norm_w, m_gla_norm_w, v_gla_norm_w),
        w_out=(w_out_s, m_w_out[0], v_w_out[0]), norm2_w=(norm2_w, m_norm2_w, v_norm2_w),
        w_ffn_gate=(w_gate_s, m_w_ffn_gate[0], v_w_ffn_gate[0]), w_ffn_up=(w_up_s, m_w_ffn_up[0], v_w_ffn_up[0]),
        w_ffn_down=(w_down_s, m_w_ffn_down[0], v_w_ffn_down[0]),
        final_norm_w=(final_norm_w.reshape(1, D), m_final_norm_w.reshape(1, D), v_final_norm_w.reshape(1, D)))
    names = ["norm1_w", "w_in", "gla_gate_w2", "gla_gate_b", "attn_sinks", "gla_norm_w", "w_out", "norm2_w",
             "w_ffn_gate", "w_ffn_up", "w_ffn_down", "final_norm_w"]
    lead = {"norm1_w": False, "gla_gate_b": False, "attn_sinks": False, "gla_norm_w": False, "norm2_w": False}
    g_out_l, d_out, m_out, v_out = [], [], [], []
    for nm in names:
        w, m, v = weights[nm]
        gr = grads[nm]
        dl, nmn, nvn = _adamw(w, gr, m, v, name="adamw_" + nm)
        if nm == "w_in":
            gr, dl, nmn, nvn, w = big[0], unlin(dl), unlin(nmn), unlin(nvn), w_in_s
        if nm == "final_norm_w":
            shape = (D,)
        elif nm in lead:
            shape = w.shape
        else:
            shape = (1,) + w.shape
        g_out_l.append(gr.reshape(shape))
        d_out.append(dl.reshape(shape))
        m_out.append(nmn.reshape(shape))
        v_out.append(nvn.reshape(shape))
    return (loss, dx[None], *g_out_l, *d_out, *m_out, *v_out)
```

```python
import functools

import jax
import jax.numpy as jnp
from jax import lax
from jax.experimental import pallas as pl
from jax.experimental.pallas import tpu as pltpu
from jax.experimental.pallas import tpu_sc as plsc

F32 = jnp.float32
BF16 = jnp.bfloat16

D = 2048
HEAD_DIM = 64
N_Q = 32
N_KV = 4
GROUP = 8
WINDOW = 128
GLA_H = 4
DK = 256
DV = 512
RANK = 16
CHUNK = 64
FFN = 5632
EPS = 1e-6
MASK_VALUE = -1e30
GATE_NORM = 16.0
LR, B1, B2, ADAM_EPS, WD, STEP = 0.001, 0.9, 0.999, 1e-08, 0.01, 10

MAIN = 12800
O_MERGE, O_GLA, O_ATTN = 0, 6144, 10240
W_MERGE, W_GLA, W_ATTN = 3 * DV, 2 * DK + DV, D + 2 * N_KV * HEAD_DIM
LR_AT = 6656
LRP = 128


def _main_pieces():
    o_gq, o_gk, o_gv, o_gr, o_ga, o_gb = 2560, 3584, 4608, LR_AT + RANK, LR_AT + RANK + D, LR_AT + RANK + 2 * D
    pieces = []
    for h in range(GLA_H):
        pieces += [(o_gr + DV * h, DV), (o_ga + DV * h, DV), (o_gb + DV * h, DV)]
    for h in range(GLA_H):
        pieces += [(o_gq + DK * h, DK), (o_gk + DK * h, DK), (o_gv + DV * h, DV)]
    pieces.append((0, W_ATTN))
    return pieces


SHARD_COLS = 3204

VMEM_LIMIT = 56 * 1024 * 1024
MESH = pl.DeviceIdType.MESH
ANY = pl.BlockSpec(memory_space=pl.ANY)


def _params(ngrid):
    return pltpu.CompilerParams(dimension_semantics=("arbitrary",) * ngrid, vmem_limit_bytes=VMEM_LIMIT)


def _shard_segments(lo, hi):
    segs = []
    while lo < hi:
        s = lo // SHARD_COLS
        e = min(hi, (s + 1) * SHARD_COLS)
        segs.append((s, lo - s * SHARD_COLS, e - s * SHARD_COLS))
        lo = e
    return segs


def _lanes(pieces):
    return pieces[0] if len(pieces) == 1 else jnp.concatenate(pieces, axis=1)


def _w_in_to_main(shards):
    rb = 256

    def body(g_ref, main_ref, lr_ref):
        at = 0
        for a, w in _main_pieces():
            main_ref[:, at:at + w] = _lanes([g_ref[s, :, lo:hi] for s, lo, hi in _shard_segments(a, a + w)])
            at += w
        lr = [g_ref[s, :, lo:hi] for s, lo, hi in _shard_segments(LR_AT, LR_AT + RANK)]
        lr_ref[...] = _lanes(lr + [jnp.zeros((rb, LRP - RANK), lr_ref.dtype)])

    return pl.pallas_call(
        body, name="w_in_to_main", grid=(D // rb,),
        in_specs=[pl.BlockSpec((4, rb, SHARD_COLS), lambda i: (0, i, 0))],
        out_specs=[pl.BlockSpec((rb, MAIN), lambda i: (i, 0)), pl.BlockSpec((rb, LRP), lambda i: (i, 0))],
        out_shape=[jax.ShapeDtypeStruct((D, MAIN), shards.dtype), jax.ShapeDtypeStruct((D, LRP), shards.dtype)],
        compiler_params=_params(1),
    )(shards)


def _main_to_shards(g_main, g_lr):
    rb = 256
    at, sources = 0, [(LR_AT, RANK, 1, 0)]
    for a, w in _main_pieces():
        sources.append((a, w, 0, at))
        at += w
    sources.sort()

    def body(main_ref, lr_ref, out_ref):
        refs = (main_ref, lr_ref)
        for s in range(4):
            lo, hi = s * SHARD_COLS, (s + 1) * SHARD_COLS
            pieces = []
            for a, w, which, src in sources:
                b, e = max(a, lo), min(a + w, hi)
                if b < e:
                    pieces.append(refs[which][:, src + b - a:src + e - a])
            out_ref[s] = _lanes(pieces)

    return pl.pallas_call(
        body, name="main_to_shards", grid=(D // rb,),
        in_specs=[pl.BlockSpec((rb, MAIN), lambda i: (i, 0)), pl.BlockSpec((rb, LRP), lambda i: (i, 0))],
        out_specs=pl.BlockSpec((4, rb, SHARD_COLS), lambda i: (0, i, 0)),
        out_shape=jax.ShapeDtypeStruct((4, D, SHARD_COLS), g_main.dtype), compiler_params=_params(1),
    )(g_main, g_lr)


def _pick(n, cands):
    for c in cands:
        if n % c == 0:
            return c
    return n


def _sigmoid(x):
    return 1.0 / (1.0 + jnp.exp(-x))


def _dot(a, b, ca, cb):
    return lax.dot_general(a, b, (((ca,), (cb,)), ((), ())), preferred_element_type=F32)


def _matmul(a, b, *, name, ta=False, tb=False, add=None, out_dtype=F32, bm=None, bj=None, bc=None,
            b_slots=False, out_slots=False):
    C, M = a.shape if ta else a.shape[::-1]
    if b_slots:
        if tb:
            J, bc = b.shape[1], b.shape[2]
            assert b.shape[0] * bc == C
        else:
            bj = b.shape[2]
            J = b.shape[0] * bj
            assert b.shape[1] == C
    else:
        J = b.shape[0] if tb else b.shape[1]
        assert (b.shape[1] if tb else b.shape[0]) == C
    bm = bm or _pick(M, (1024, 512, 256, 128) if C <= FFN else (512, 256, 128))
    bj = bj or _pick(J, (1280, 1024, 512, 256, 128) if C <= D else (512, 256, 128))
    bc = bc or (C if C <= FFN else _pick(C, (6400,)))
    nm, nj, nc = M // bm, J // bj, C // bc
    has_add = add is not None

    def body(*refs):
        a_ref, b_ref = refs[0], refs[1]
        add_ref = refs[2] if has_add else None
        o_ref = refs[3] if has_add else refs[2]
        p = _dot(a_ref[...].astype(BF16), b_ref[...].astype(BF16), 0 if ta else 1, 1 if tb else 0)

        def finish(acc):
            if has_add:
                acc = acc + add_ref[...]
            o_ref[...] = acc.astype(o_ref.dtype)

        if nc == 1:
            finish(p)
        else:
            acc_ref = refs[-1]
            c = pl.program_id(2)

            @pl.when(c == 0)
            def _():
                acc_ref[...] = p

            @pl.when(c > 0)
            def _():
                acc_ref[...] += p

            @pl.when(c == nc - 1)
            def _():
                finish(acc_ref[...])

    a_spec = pl.BlockSpec((bc, bm), lambda m, j, c: (c, m)) if ta else pl.BlockSpec((bm, bc), lambda m, j, c: (m, c))
    if b_slots:
        b_spec = (pl.BlockSpec((None, bj, bc), lambda m, j, c: (c, j, 0)) if tb
                  else pl.BlockSpec((None, bc, bj), lambda m, j, c: (j, c, 0)))
    else:
        b_spec = (pl.BlockSpec((bj, bc), lambda m, j, c: (j, c)) if tb
                  else pl.BlockSpec((bc, bj), lambda m, j, c: (c, j)))
    if out_slots:
        assert not has_add
        o_spec = pl.BlockSpec((None, bm, bj), lambda m, j, c: (j, m, 0))
        out_shape = jax.ShapeDtypeStruct((nj, M, bj), out_dtype)
    else:
        o_spec = pl.BlockSpec((bm, bj), lambda m, j, c: (m, j))
        out_shape = jax.ShapeDtypeStruct((M, J), out_dtype)
    in_specs = [a_spec, b_spec] + ([o_spec] if has_add else [])
    args = (a, b) + ((add,) if has_add else ())
    return pl.pallas_call(
        body, name=name, grid=(nm, nj, nc), in_specs=in_specs, out_specs=o_spec,
        out_shape=out_shape,
        scratch_shapes=[pltpu.VMEM((bm, bj), F32)] if nc > 1 else [],
        compiler_params=_params(3),
    )(*args)


def _rows(T):
    return _pick(T, (256, 128))


def _rms_fwd(x, w, *, name):
    T = x.shape[0]
    rb = _rows(T)

    def body(x_ref, w_ref, u_ref, ut_ref):
        xv = x_ref[...]
        r = lax.rsqrt(jnp.mean(xv * xv, axis=-1, keepdims=True) + EPS)
        u = xv * r * w_ref[...]
        u_ref[...] = u.astype(BF16)
        ut_ref[...] = u.T.astype(BF16)

    return pl.pallas_call(
        body, name=name, grid=(T // rb,),
        in_specs=[pl.BlockSpec((rb, D), lambda i: (i, 0)), pl.BlockSpec((1, D), lambda i: (0, 0))],
        out_specs=[pl.BlockSpec((rb, D), lambda i: (i, 0)), pl.BlockSpec((D, rb), lambda i: (0, i))],
        out_shape=[jax.ShapeDtypeStruct((T, D), BF16), jax.ShapeDtypeStruct((D, T), BF16)], compiler_params=_params(1),
    )(x, w)


def _out_proj(merged, w_out, x, norm_w):
    T = merged.shape[0]
    bm = _pick(T, (512, 256))

    def body(a_ref, b_ref, x_ref, w_ref, h_ref, v_ref, vt_ref):
        h = _dot(a_ref[...], b_ref[...], 1, 0) + x_ref[...]
        h_ref[...] = h
        r = lax.rsqrt(jnp.mean(h * h, axis=-1, keepdims=True) + EPS)
        v2 = h * r * w_ref[...]
        v_ref[...] = v2.astype(BF16)
        vt_ref[...] = v2.T.astype(BF16)

    blk = pl.BlockSpec((bm, D), lambda i: (i, 0))
    return pl.pallas_call(
        body, name="out_proj", grid=(T // bm,),
        in_specs=[blk, pl.BlockSpec((D, D), lambda i: (0, 0)), blk, pl.BlockSpec((1, D), lambda i: (0, 0))],
        out_specs=[blk, blk, pl.BlockSpec((D, bm), lambda i: (0, i))],
        out_shape=[jax.ShapeDtypeStruct((T, D), F32), jax.ShapeDtypeStruct((T, D), BF16),
                   jax.ShapeDtypeStruct((D, T), BF16)],
        compiler_params=_params(1),
    )(merged, w_out, x, norm_w)


def _rms_bwd(dy, h, w, resid, *, name):
    T = h.shape[0]
    rb = _rows(T)

    def body(dy_ref, h_ref, w_ref, res_ref, dh_ref, dhb_ref, dw_ref):
        @pl.when(pl.program_id(0) == 0)
        def _():
            dw_ref[...] = jnp.zeros_like(dw_ref)

        hv = h_ref[...]
        r = lax.rsqrt(jnp.mean(hv * hv, axis=-1, keepdims=True) + EPS)
        hn = hv * r
        dyv = dy_ref[...]
        dw_ref[...] += jnp.sum(dyv * hn, axis=0, keepdims=True)
        t = dyv * w_ref[...]
        dh = res_ref[...] + r * (t - hn * jnp.mean(t * hn, axis=-1, keepdims=True))
        dh_ref[...] = dh
        dhb_ref[...] = dh.astype(BF16)

    blk = pl.BlockSpec((rb, D), lambda i: (i, 0))
    vec = pl.BlockSpec((1, D), lambda i: (0, 0))
    return pl.pallas_call(
        body, name=name, grid=(T // rb,), in_specs=[blk, blk, vec, blk], out_specs=[blk, blk, vec],
        out_shape=[jax.ShapeDtypeStruct((T, D), F32), jax.ShapeDtypeStruct((T, D), BF16),
                   jax.ShapeDtypeStruct((1, D), F32)],
        compiler_params=_params(1),
    )(dy, h, w, resid)


def _final_loss(h, w, target):
    T = h.shape[0]
    rb = _rows(T)

    def body(h_ref, w_ref, t_ref, loss_ref, dh_ref, dhb_ref, dw_ref):
        @pl.when(pl.program_id(0) == 0)
        def _():
            dw_ref[...] = jnp.zeros_like(dw_ref)
            loss_ref[...] = jnp.zeros_like(loss_ref)

        hv = h_ref[...]
        wv = w_ref[...]
        r = lax.rsqrt(jnp.mean(hv * hv, axis=-1, keepdims=True) + EPS)
        hn = hv * r
        e = hn * wv - t_ref[...]
        row = jnp.sum(e * e, axis=-1, keepdims=True) * (0.5 / D)
        loss_ref[...] += jnp.broadcast_to(jnp.sum(row, axis=0, keepdims=True), loss_ref.shape)
        dy = e * (1.0 / D)
        dw_ref[...] += jnp.sum(dy * hn, axis=0, keepdims=True)
        t = dy * wv
        dh = r * (t - hn * jnp.mean(t * hn, axis=-1, keepdims=True))
        dh_ref[...] = dh
        dhb_ref[...] = dh.astype(BF16)

    blk = pl.BlockSpec((rb, D), lambda i: (i, 0))
    vec = pl.BlockSpec((1, D), lambda i: (0, 0))
    return pl.pallas_call(
        body, name="final_loss", grid=(T // rb,), in_specs=[blk, vec, blk],
        out_specs=[pl.BlockSpec((8, 128), lambda i: (0, 0)), blk, blk, vec],
        out_shape=[jax.ShapeDtypeStruct((8, 128), F32), jax.ShapeDtypeStruct((T, D), F32),
                   jax.ShapeDtypeStruct((T, D), BF16), jax.ShapeDtypeStruct((1, D), F32)],
        compiler_params=_params(1),
    )(h, w, target)


PANEL = FFN // 4


def _ffn_fwd(v2, w_gate, w_up):
    T = v2.shape[0]
    bm = _pick(T, (512, 256))

    def body(a_ref, bg_ref, bu_ref, gate_ref, up_ref, ff_ref, fft_ref):
        a = a_ref[...]
        g = _dot(a, bg_ref[...], 1, 0)
        u = _dot(a, bu_ref[...], 1, 0)
        gate_ref[...] = g
        up_ref[...] = u
        ff = g * _sigmoid(g) * u
        ff_ref[...] = ff.astype(BF16)
        fft_ref[...] = ff.T.astype(BF16)

    panel = pl.BlockSpec((None, D, PANEL), lambda j, m: (j, 0, 0))
    out = pl.BlockSpec((bm, PANEL), lambda j, m: (m, j))
    return pl.pallas_call(
        body, name="ffn_fwd", grid=(4, T // bm),
        in_specs=[pl.BlockSpec((bm, D), lambda j, m: (m, 0)), panel, panel],
        out_specs=[out, out, out, pl.BlockSpec((PANEL, bm), lambda j, m: (j, m))],
        out_shape=[jax.ShapeDtypeStruct((T, FFN), F32), jax.ShapeDtypeStruct((T, FFN), F32),
                   jax.ShapeDtypeStruct((T, FFN), BF16), jax.ShapeDtypeStruct((FFN, T), BF16)],
        compiler_params=_params(2),
    )(v2, w_gate, w_up)


def _ffn_bwd_hidden(dh2, w_down, gate, up):
    T = dh2.shape[0]
    bm = _pick(T, (512, 256))

    def body(a_ref, b_ref, g_ref, u_ref, dg_ref, du_ref):
        d = _dot(a_ref[...], b_ref[...], 1, 1)
        g = g_ref[...]
        sg = _sigmoid(g)
        du_ref[...] = (d * g * sg).astype(BF16)
        dg_ref[...] = (d * u_ref[...] * sg * (1.0 + g * (1.0 - sg))).astype(BF16)

    blk = pl.BlockSpec((bm, PANEL), lambda j, m: (m, j))
    return pl.pallas_call(
        body, name="ffn_bwd_hidden", grid=(4, T // bm),
        in_specs=[pl.BlockSpec((bm, D), lambda j, m: (m, 0)), pl.BlockSpec((PANEL, D), lambda j, m: (j, 0)), blk, blk],
        out_specs=[blk, blk], out_shape=[jax.ShapeDtypeStruct((T, FFN), BF16)] * 2, compiler_params=_params(2),
    )(dh2, w_down, gate, up)


def _ffn_bwd_input(dgate, dup, w_gate, w_up):
    T = dgate.shape[0]
    bm = _pick(T, (1024, 512, 256))
    bj = 1024

    def body(ag_ref, au_ref, bg_ref, bu_ref, o_ref, acc_ref):
        c = pl.program_id(2)
        p = _dot(ag_ref[...], bg_ref[...], 1, 1) + _dot(au_ref[...], bu_ref[...], 1, 1)

        @pl.when(c == 0)
        def _():
            acc_ref[...] = p

        @pl.when(c > 0)
        def _():
            acc_ref[...] += p

        @pl.when(c == 3)
        def _():
            o_ref[...] = acc_ref[...]

    a = pl.BlockSpec((bm, PANEL), lambda m, j, c: (m, c))
    b = pl.BlockSpec((None, bj, PANEL), lambda m, j, c: (c, j, 0))
    return pl.pallas_call(
        body, name="ffn_bwd_input", grid=(T // bm, D // bj, 4), in_specs=[a, a, b, b],
        out_specs=pl.BlockSpec((bm, bj), lambda m, j, c: (m, j)), out_shape=jax.ShapeDtypeStruct((T, D), F32),
        scratch_shapes=[pltpu.VMEM((bm, bj), F32)], compiler_params=_params(3),
    )(dgate, dup, w_gate, w_up)


def _colsum(x, *, name):
    T, W = x.shape
    rb = _rows(T)

    def body(x_ref, o_ref):
        @pl.when(pl.program_id(0) == 0)
        def _():
            o_ref[...] = jnp.zeros_like(o_ref)

        o_ref[...] += jnp.sum(x_ref[...], axis=0, keepdims=True)

    return pl.pallas_call(
        body, name=name, grid=(T // rb,), in_specs=[pl.BlockSpec((rb, W), lambda i: (i, 0))],
        out_specs=pl.BlockSpec((1, W), lambda i: (0, 0)), out_shape=jax.ShapeDtypeStruct((1, W), F32),
        compiler_params=_params(1),
    )(x)


def _merge_fwd(attn_o, gla_raw, proj, gla_norm_w):
    T = attn_o.shape[0]
    rb = _rows(T)

    def body(a_ref, g_ref, gr_ref, ga_ref, gb_ref, w_ref, o_ref, ot_ref):
        graw = g_ref[...]
        r = lax.rsqrt(jnp.mean(graw * graw, axis=-1, keepdims=True) + EPS)
        gr = gr_ref[...]
        go = graw * r * w_ref[...] * (gr * _sigmoid(gr))
        merged = _sigmoid(ga_ref[...]) * a_ref[...] + _sigmoid(gb_ref[...]) * go
        o_ref[...] = merged.astype(BF16)
        ot_ref[...] = merged.T.astype(BF16)

    def sec(k):
        return pl.BlockSpec((rb, DV), lambda i, h: (i, O_MERGE // DV + 3 * h + k))

    blk = pl.BlockSpec((rb, DV), lambda i, h: (i, h))
    return pl.pallas_call(
        body, name="merge_fwd", grid=(T // rb, GLA_H),
        in_specs=[blk, blk, sec(0), sec(1), sec(2), pl.BlockSpec((1, DV), lambda i, h: (0, 0))],
        out_specs=[blk, pl.BlockSpec((DV, rb), lambda i, h: (h, i))],
        out_shape=[jax.ShapeDtypeStruct((T, D), BF16), jax.ShapeDtypeStruct((D, T), BF16)], compiler_params=_params(2),
    )(attn_o, gla_raw, proj, proj, proj, gla_norm_w)


def _merge_bwd(dm, attn_o, gla_raw, proj, gla_norm_w):
    T = attn_o.shape[0]
    rb = _rows(T)

    def body(dm_ref, a_ref, g_ref, gr_ref, ga_ref, gb_ref, w_ref, da_ref, dg_ref, dp_ref, dw_ref):
        @pl.when((pl.program_id(0) == 0) & (pl.program_id(1) == 0))
        def _():
            dw_ref[...] = jnp.zeros_like(dw_ref)

        dmv = dm_ref[...]
        av = a_ref[...]
        graw = g_ref[...]
        gr = gr_ref[...]
        wv = w_ref[...]
        sa = _sigmoid(ga_ref[...])
        sb = _sigmoid(gb_ref[...])
        r = lax.rsqrt(jnp.mean(graw * graw, axis=-1, keepdims=True) + EPS)
        gnh = graw * r
        gn = gnh * wv
        sr = _sigmoid(gr)
        sl = gr * sr
        go = gn * sl
        da_ref[...] = dmv * sa
        dgo = dmv * sb
        dp_ref[:, 0:DV] = (dgo * gn * sr * (1.0 + gr * (1.0 - sr))).astype(BF16)
        dp_ref[:, DV:2 * DV] = (dmv * av * sa * (1.0 - sa)).astype(BF16)
        dp_ref[:, 2 * DV:3 * DV] = (dmv * go * sb * (1.0 - sb)).astype(BF16)
        dgn = dgo * sl
        dw_ref[...] += jnp.sum(dgn * gnh, axis=0, keepdims=True)
        t = dgn * wv
        dg_ref[...] = r * (t - gnh * jnp.mean(t * gnh, axis=-1, keepdims=True))

    def sec(k):
        return pl.BlockSpec((rb, DV), lambda i, h: (i, O_MERGE // DV + 3 * h + k))

    blk = pl.BlockSpec((rb, DV), lambda i, h: (i, h))
    vec = pl.BlockSpec((1, DV), lambda i, h: (0, 0))
    return pl.pallas_call(
        body, name="merge_bwd", grid=(T // rb, GLA_H),
        in_specs=[blk, blk, blk, sec(0), sec(1), sec(2), vec],
        out_specs=[blk, blk, pl.BlockSpec((rb, W_MERGE), lambda i, h: (i, O_MERGE // W_MERGE + h)), vec],
        out_shape=[jax.ShapeDtypeStruct((T, D), F32), jax.ShapeDtypeStruct((T, D), F32),
                   jax.ShapeDtypeStruct((T, MAIN), BF16), jax.ShapeDtypeStruct((1, DV), F32)],
        compiler_params=_params(2),
    )(dm, attn_o, gla_raw, proj, proj, proj, gla_norm_w)


def _attn_band(n):
    qi = lax.broadcasted_iota(jnp.int32, (GROUP * WINDOW, WINDOW), 0) & (WINDOW - 1)
    kj = lax.broadcasted_iota(jnp.int32, (GROUP * WINDOW, WINDOW), 1)
    cur = kj <= qi
    return cur, cur | (n > 0)


def _stack_heads(ref, h, scale=1.0):
    rows = jnp.concatenate(
        [ref[:, (h * GROUP + g) * HEAD_DIM:(h * GROUP + g + 1) * HEAD_DIM] for g in range(GROUP)], axis=0)
    return (rows * scale).astype(BF16) if scale != 1.0 else rows.astype(BF16)


def _stack_sinks(s_ref, h):
    return jnp.concatenate(
        [jnp.broadcast_to(s_ref[:, h * GROUP + g:h * GROUP + g + 1], (WINDOW, 1)) for g in range(GROUP)], axis=0)


def _attn_probs(qs, kp, kc, band, sink):
    cur, live = band
    s = jnp.where(cur, _dot(qs, kc, 1, 1), _dot(qs, kp, 1, 1))
    s = jnp.where(live, s, MASK_VALUE)
    m = jnp.maximum(jnp.max(s, axis=-1, keepdims=True), sink)
    e = jnp.exp(s - m)
    es = jnp.exp(sink - m)
    inv = 1.0 / (jnp.sum(e, axis=-1, keepdims=True) + es)
    return e * inv, es * inv


def _unfold(cur, x):
    mine = jnp.where(cur, x, 0.0)
    return mine.astype(BF16), (x - mine).astype(BF16)


def _attn_specs(nb, rev):
    def at(n):
        return (nb - 1 - n) if rev else n

    kcol, vcol = (O_ATTN + D) // 256, (O_ATTN + D) // 256 + 1
    q = pl.BlockSpec((WINDOW, D), lambda n: (at(n), O_ATTN // D))
    kc = pl.BlockSpec((WINDOW, 256), lambda n: (at(n), kcol))
    kp = pl.BlockSpec((WINDOW, 256), lambda n: (jnp.maximum(at(n) - 1, 0), kcol))
    vc = pl.BlockSpec((WINDOW, 256), lambda n: (at(n), vcol))
    vp = pl.BlockSpec((WINDOW, 256), lambda n: (jnp.maximum(at(n) - 1, 0), vcol))
    sk = pl.BlockSpec((1, 128), lambda n: (0, 0))
    o = pl.BlockSpec((WINDOW, D), lambda n: (at(n), 0))
    return q, kc, kp, vc, vp, sk, o


def _attn_fwd(proj, sinks):
    T = proj.shape[0]
    nb = T // WINDOW

    def body(q_ref, kc_ref, kp_ref, vc_ref, vp_ref, s_ref, o_ref):
        band = _attn_band(pl.program_id(0))
        for h in range(N_KV):
            hs = slice(h * HEAD_DIM, (h + 1) * HEAD_DIM)
            p, _ = _attn_probs(_stack_heads(q_ref, h, HEAD_DIM ** -0.5), kp_ref[:, hs].astype(BF16),
                               kc_ref[:, hs].astype(BF16), band, _stack_sinks(s_ref, h))
            p_cur, p_prev = _unfold(band[0], p)
            o = _dot(p_cur, vc_ref[:, hs].astype(BF16), 1, 0) + _dot(p_prev, vp_ref[:, hs].astype(BF16), 1, 0)
            for g in range(GROUP):
                hg = h * GROUP + g
                o_ref[:, hg * HEAD_DIM:(hg + 1) * HEAD_DIM] = o[g * WINDOW:(g + 1) * WINDOW]

    q, kc, kp, vc, vp, sk, o = _attn_specs(nb, False)
    return pl.pallas_call(
        body, name="attn_fwd", grid=(nb,), in_specs=[q, kc, kp, vc, vp, sk], out_specs=o,
        out_shape=jax.ShapeDtypeStruct((T, D), F32), compiler_params=_params(1),
    )(proj, proj, proj, proj, proj, sinks)


def _attn_bwd(proj, sinks, d_o, dproj):
    T = proj.shape[0]
    nb = T // WINDOW
    kat, vat = D, D + N_KV * HEAD_DIM

    def body(q_ref, kc_ref, kp_ref, vc_ref, vp_ref, s_ref, do_ref, _, dp_ref, ds_ref, ck_ref, cv_ref):
        i = pl.program_id(0)
        n = nb - 1 - i

        @pl.when(i == 0)
        def _():
            ck_ref[...] = jnp.zeros_like(ck_ref)
            cv_ref[...] = jnp.zeros_like(cv_ref)
            ds_ref[...] = jnp.zeros_like(ds_ref)

        band = _attn_band(n)
        for h in range(N_KV):
            hs = slice(h * HEAD_DIM, (h + 1) * HEAD_DIM)
            kp, kc = kp_ref[:, hs].astype(BF16), kc_ref[:, hs].astype(BF16)
            vp, vc = vp_ref[:, hs].astype(BF16), vc_ref[:, hs].astype(BF16)
            qs = _stack_heads(q_ref, h, HEAD_DIM ** -0.5)
            do = _stack_heads(do_ref, h)
            p, ps = _attn_probs(qs, kp, kc, band, _stack_sinks(s_ref, h))
            dp = jnp.where(band[0], _dot(do, vc, 1, 1), _dot(do, vp, 1, 1))
            delta = jnp.sum(p * dp, axis=-1, keepdims=True)
            ds_cur, ds_prev = _unfold(band[0], p * (dp - delta))
            p_cur, p_prev = _unfold(band[0], p)
            dsink = -ps * delta
            dq = ((_dot(ds_cur, kc, 1, 0) + _dot(ds_prev, kp, 1, 0)) * (HEAD_DIM ** -0.5)).astype(BF16)
            for g in range(GROUP):
                hg = h * GROUP + g
                rows = slice(g * WINDOW, (g + 1) * WINDOW)
                ds_ref[hg:hg + 1, :] += jnp.broadcast_to(jnp.sum(dsink[rows], axis=0, keepdims=True), (1, 128))
                dp_ref[:, hg * HEAD_DIM:(hg + 1) * HEAD_DIM] = dq[rows]
            dp_ref[:, kat + h * HEAD_DIM:kat + (h + 1) * HEAD_DIM] = (_dot(ds_cur, qs, 0, 0) + ck_ref[:, hs]).astype(BF16)
            dp_ref[:, vat + h * HEAD_DIM:vat + (h + 1) * HEAD_DIM] = (_dot(p_cur, do, 0, 0) + cv_ref[:, hs]).astype(BF16)
            ck_ref[:, hs] = _dot(ds_prev, qs, 0, 0)
            cv_ref[:, hs] = _dot(p_prev, do, 0, 0)

    q, kc, kp, vc, vp, sk, o = _attn_specs(nb, True)
    return pl.pallas_call(
        body, name="attn_bwd", grid=(nb,), in_specs=[q, kc, kp, vc, vp, sk, o, ANY],
        out_specs=[pl.BlockSpec((WINDOW, W_ATTN), lambda n: (nb - 1 - n, O_ATTN // W_ATTN)),
                   pl.BlockSpec((N_Q, 128), lambda n: (0, 0))],
        out_shape=[jax.ShapeDtypeStruct((T, MAIN), BF16), jax.ShapeDtypeStruct((N_Q, 128), F32)],
        scratch_shapes=[pltpu.VMEM((WINDOW, 256), F32), pltpu.VMEM((WINDOW, 256), F32)],
        input_output_aliases={7: 0}, compiler_params=_params(1),
    )(proj, proj, proj, proj, proj, sinks, d_o, dproj)


def _split3(x):
    hi = x.astype(BF16)
    r1 = x - hi.astype(F32)
    mid = r1.astype(BF16)
    lo = (r1 - mid.astype(F32)).astype(BF16)
    return hi, mid, lo


def _tri_sum(tri, x):
    hi, mid, lo = _split3(x)
    return _dot(tri, hi, 1, 0) + _dot(tri, mid, 1, 0) + _dot(tri, lo, 1, 0)


def _chunk_masks(rb):
    r = lax.broadcasted_iota(jnp.int32, (rb, rb), 0)
    c = lax.broadcasted_iota(jnp.int32, (rb, rb), 1)
    same = (r ^ c) < CHUNK
    return same & (r >= c), same & (r <= c)


def _per_chunk_rows(x, per, pick):
    return jnp.concatenate(
        [jnp.broadcast_to(pick(x[i * CHUNK:(i + 1) * CHUNK]), (CHUNK, x.shape[1])) for i in range(per)], axis=0)


def _gla_block(q, k, lr, w2, b, lower):
    per = q.shape[0] // CHUNK
    logit = _dot(lr.astype(BF16), w2.astype(BF16), 1, 0) + b
    la = (jnp.minimum(logit, 0.0) - jnp.log(1.0 + jnp.exp(-jnp.abs(logit)))) * (1.0 / GATE_NORM)
    g = _tri_sum(lower, la)
    gl = _per_chunk_rows(g, per, lambda c: c[CHUNK - 1:CHUNK])
    eg = jnp.exp(g)
    qd = q * (DK ** -0.5) * eg
    ki = k * jnp.exp(-g)
    ke = k * jnp.exp(gl - g)
    return logit, g, gl, eg, qd, ki, ke


def _gla_rows(T, forward):
    return _pick(T, (512, 256, 128, 64) if forward else (256, 128, 64))


def _gla_fwd(proj, proj_lr, w2p, gate_b):
    T = proj.shape[0]
    rb = _gla_rows(T, True)
    per = rb // CHUNK

    def body(q_ref, k_ref, v_ref, lr_ref, w2_ref, b_ref, o_ref, st_ref, s_scr):
        @pl.when(pl.program_id(1) == 0)
        def _():
            s_scr[...] = jnp.zeros_like(s_scr)

        low, _ = _chunk_masks(rb)
        _, _, gl, _, qd, ki, ke = _gla_block(q_ref[...], k_ref[...], lr_ref[...], w2_ref[...], b_ref[...],
                                             low.astype(BF16))
        v = v_ref[...].astype(BF16)
        qdb, keb = qd.astype(BF16), ke.astype(BF16)
        att = jnp.where(low, _dot(qdb, ki.astype(BF16), 1, 1), 0.0).astype(BF16)
        o_intra = _dot(att, v, 1, 0)
        st = s_scr[...]
        for i in range(per):
            rows = slice(i * CHUNK, (i + 1) * CHUNK)
            st_ref[0, i] = st
            o_ref[rows, :] = o_intra[rows] + _dot(qdb[rows], st.astype(BF16), 1, 1)
            st = st * jnp.exp(gl[i * CHUNK:i * CHUNK + 1]) + _dot(v[rows], keb[rows], 0, 0)
        s_scr[...] = st

    return pl.pallas_call(
        body, name="gla_fwd", grid=(GLA_H, T // rb),
        in_specs=[pl.BlockSpec((rb, DK), lambda h, n: (n, (O_GLA + W_GLA * h) // DK)),
                  pl.BlockSpec((rb, DK), lambda h, n: (n, (O_GLA + W_GLA * h) // DK + 1)),
                  pl.BlockSpec((rb, DV), lambda h, n: (n, (O_GLA + W_GLA * h) // DV + 1)),
                  pl.BlockSpec((rb, LRP), lambda h, n: (n, 0)),
                  pl.BlockSpec((LRP, DK), lambda h, n: (0, h)),
                  pl.BlockSpec((1, DK), lambda h, n: (0, h))],
        out_specs=[pl.BlockSpec((rb, DV), lambda h, n: (n, h)),
                   pl.BlockSpec((1, per, DV, DK), lambda h, n: (h, n, 0, 0))],
        out_shape=[jax.ShapeDtypeStruct((T, GLA_H * DV), F32),
                   jax.ShapeDtypeStruct((GLA_H, T // CHUNK, DV, DK), F32)],
        scratch_shapes=[pltpu.VMEM((DV, DK), F32)], compiler_params=_params(2),
    )(proj, proj, proj, proj_lr, w2p, gate_b)


def _gla_bwd(proj, proj_lr, w2p, gate_b, states, d_o, dproj):
    T = proj.shape[0]
    rb = _gla_rows(T, False)
    per = rb // CHUNK
    nblk = T // rb

    def body(q_ref, k_ref, v_ref, lr_ref, w2_ref, b_ref, st_ref, do_ref, _, dp_ref, dl_ref, ds_scr):
        @pl.when(pl.program_id(1) == 0)
        def _():
            ds_scr[...] = jnp.zeros_like(ds_scr)

        low, up = _chunk_masks(rb)
        logit, g, gl, eg, qd, ki, ke = _gla_block(q_ref[...], k_ref[...], lr_ref[...], w2_ref[...], b_ref[...],
                                                  low.astype(BF16))
        v = v_ref[...].astype(BF16)
        do = do_ref[...].astype(BF16)
        qdb, kib, keb = qd.astype(BF16), ki.astype(BF16), ke.astype(BF16)
        att = jnp.where(low, _dot(qdb, kib, 1, 1), 0.0).astype(BF16)
        datt = jnp.where(low, _dot(do, v, 1, 1), 0.0).astype(BF16)
        dv_intra = _dot(att, do, 0, 0)
        dqd_intra = _dot(datt, kib, 1, 0)
        dki = _dot(datt, qdb, 0, 0)
        ds = ds_scr[...]
        dqd, dke, dgl = [None] * per, [None] * per, [None] * per
        for i in reversed(range(per)):
            rows = slice(i * CHUNK, (i + 1) * CHUNK)
            sp = st_ref[0, i]
            dsb = ds.astype(BF16)
            dp_ref[rows, 2 * DK:] = (dv_intra[rows] + _dot(keb[rows], dsb, 1, 1)).astype(BF16)
            dqd[i] = dqd_intra[rows] + _dot(do[rows], sp.astype(BF16), 1, 0)
            dke[i] = _dot(v[rows], dsb, 1, 0)
            decay = jnp.exp(gl[i * CHUNK:i * CHUNK + 1])
            dgl[i] = (jnp.sum(dke[i] * ke[rows], axis=0, keepdims=True)
                      + jnp.sum(ds * sp, axis=0, keepdims=True) * decay)
            ds = ds * decay + _dot(do[rows], qdb[rows], 0, 0)
        ds_scr[...] = ds
        dqd = jnp.concatenate(dqd, axis=0)
        dke = jnp.concatenate(dke, axis=0)
        dgl = jnp.concatenate([jnp.broadcast_to(d, (CHUNK, DK)) for d in dgl], axis=0)
        dp_ref[:, 0:DK] = (dqd * (DK ** -0.5) * eg).astype(BF16)
        dp_ref[:, DK:2 * DK] = (dki * jnp.exp(-g) + dke * jnp.exp(gl - g)).astype(BF16)
        dg = dqd * qd - dki * ki - dke * ke
        dla = _tri_sum(up.astype(BF16), dg) + dgl
        dl_ref[...] = dla * (1.0 / GATE_NORM) * (1.0 - _sigmoid(logit))

    def rev(n):
        return nblk - 1 - n

    return pl.pallas_call(
        body, name="gla_bwd", grid=(GLA_H, nblk),
        in_specs=[pl.BlockSpec((rb, DK), lambda h, n: (rev(n), (O_GLA + W_GLA * h) // DK)),
                  pl.BlockSpec((rb, DK), lambda h, n: (rev(n), (O_GLA + W_GLA * h) // DK + 1)),
                  pl.BlockSpec((rb, DV), lambda h, n: (rev(n), (O_GLA + W_GLA * h) // DV + 1)),
                  pl.BlockSpec((rb, LRP), lambda h, n: (rev(n), 0)),
                  pl.BlockSpec((LRP, DK), lambda h, n: (0, h)),
                  pl.BlockSpec((1, DK), lambda h, n: (0, h)),
                  pl.BlockSpec((1, per, DV, DK), lambda h, n: (h, rev(n), 0, 0)),
                  pl.BlockSpec((rb, DV), lambda h, n: (rev(n), h)),
                  ANY],
        out_specs=[pl.BlockSpec((rb, W_GLA), lambda h, n: (rev(n), O_GLA // W_GLA + h)),
                   pl.BlockSpec((rb, DK), lambda h, n: (rev(n), h))],
        out_shape=[jax.ShapeDtypeStruct((T, MAIN), BF16), jax.ShapeDtypeStruct((T, GLA_H * DK), F32)],
        scratch_shapes=[pltpu.VMEM((DV, DK), F32)], input_output_aliases={8: 0}, compiler_params=_params(2),
    )(proj, proj, proj, proj_lr, w2p, gate_b, states, d_o, dproj)


def _local_step(x, target, p, reduce_part):
    u, u_t = _rms_fwd(x, p["norm1_w"], name="rms1_fwd")
    proj = _matmul(u, p["w_in_main"], name="mm_proj")
    proj_lr = _matmul(u, p["w_in_lr"], name="mm_proj_lr")
    attn_o = _attn_fwd(proj, p["sinks"])
    gla_raw, states = _gla_fwd(proj, proj_lr, p["w2p"], p["gate_b"])
    merged, merged_t = _merge_fwd(attn_o, gla_raw, proj, p["gla_norm_w"])
    h1, v2, v2_t = _out_proj(merged, p["w_out"], x, p["norm2_w"])
    gate, up, ff, ff_t = _ffn_fwd(v2, p["w_gate"], p["w_up"])
    h2 = _matmul(ff, p["w_down"], add=h1, name="mm_down")
    loss, dh2, dh2_b, g_final = _final_loss(h2, p["final_norm_w"], target)

    dgate, dup = _ffn_bwd_hidden(dh2_b, p["w_down"], gate, up)
    reduce_part("w_down", _matmul(ff_t, dh2_b, out_dtype=BF16, bj=D, name="mm_gdown").reshape(4, FFN // 4, D))
    reduce_part("w_gate", _matmul(v2_t, dgate, out_dtype=BF16, bj=PANEL, out_slots=True, name="mm_ggate"))
    reduce_part("w_up", _matmul(v2_t, dup, out_dtype=BF16, bj=PANEL, out_slots=True, name="mm_gup"))
    dv2 = _ffn_bwd_input(dgate, dup, p["w_gate"], p["w_up"])
    dh1, dh1_b, g_norm2 = _rms_bwd(dv2, h1, p["norm2_w"], dh2, name="rms2_bwd")
    dmerged = _matmul(dh1_b, p["w_out"], tb=True, bj=D, name="mm_dmerged")
    reduce_part("w_out", _matmul(merged_t, dh1_b, out_dtype=BF16, bj=D, name="mm_gout").reshape(4, D // 4, D))
    d_attn, d_gla, dproj, g_gla_norm = _merge_bwd(dmerged, attn_o, gla_raw, proj, p["gla_norm_w"])
    dproj, dlogit = _gla_bwd(proj, proj_lr, p["w2p"], p["gate_b"], states, d_gla, dproj)
    dproj, g_sinks = _attn_bwd(proj, p["sinks"], d_attn, dproj)
    g_gate_b = _colsum(dlogit, name="colsum_gate_b")
    g_w2 = _matmul(proj_lr, dlogit, ta=True, name="mm_gw2")
    dproj_lr = _matmul(dlogit, p["w2p"], tb=True, out_dtype=BF16, name="mm_dlr")
    g_in_main = _matmul(u_t, dproj, out_dtype=BF16, name="mm_gin")
    g_in_lr = _matmul(u_t, dproj_lr, out_dtype=BF16, name="mm_gin_lr")
    du = _matmul(dproj, p["w_in_main"], tb=True, bm=_pick(x.shape[0], (1024, 256)), bj=1024, bc=3200, name="mm_du")
    du = _matmul(dproj_lr, p["w_in_lr"], tb=True, add=du, name="mm_du_lr")
    dx, _, g_norm1 = _rms_bwd(du, x, p["norm1_w"], dh1, name="rms1_bwd")
    grads = dict(norm1_w=g_norm1, w_in_main=g_in_main, w_in_lr=g_in_lr, w2=g_w2[:RANK], gate_b=g_gate_b,
                 sinks=g_sinks[:, 0].reshape(1, N_Q), gla_norm_w=g_gla_norm, norm2_w=g_norm2, final_norm_w=g_final)
    return loss, dx, grads


def _place():
    x, y, c = lax.axis_index("x"), lax.axis_index("y"), lax.axis_index("c")
    chips = [(1 - x, y), (x, 1 - y), (1 - x, 1 - y)]
    return x, y, c, chips


def _hbm_shape(s, dt):
    return jax.ShapeDtypeStruct(s, dt)


ID_SIBLING = 9
ID_GATHER_IN = 10


def _handshake(peers):
    barrier = pltpu.get_barrier_semaphore()
    for peer in peers:
        pl.semaphore_signal(barrier, inc=1, device_id=peer, device_id_type=MESH)
    pl.semaphore_wait(barrier, len(peers))


def _launch_copies(body, args, out_shape, sems, *, name, collective_id=None):
    if collective_id is None:
        return pl.pallas_call(
            body, name=name, in_specs=[ANY] * len(args), out_specs=[ANY] * len(out_shape), out_shape=out_shape,
            scratch_shapes=sems)(*args)
    return pl.kernel(
        body, name=name, out_type=out_shape, mesh=plsc.ScalarSubcoreMesh(axis_name="sequencer", num_cores=1),
        scratch_types=sems, compiler_params=pltpu.CompilerParams(collective_id=collective_id))(*args)


def _gather_shards(shards, *, name, collective_id=None, after=()):
    n = len(shards)
    first_out = n + len(after)

    def body(*refs):
        ins, outs = refs[:n], refs[first_out:first_out + n]
        ici_send, ici_recv, d2d_send, d2d_recv, local_sem = refs[first_out + n:]
        x, y, c, chips = _place()
        me = 2 * x + y
        sibling = (x, y, 1 - c)
        if collective_id is not None:
            _handshake([sibling] + [(*chip, c) for chip in chips])

        def half(w, slot, hc):
            r2 = shards[w].shape[0] // 2
            return outs[w].at[slot, pl.ds(hc * r2, r2), :]

        locals_ = [pltpu.make_async_copy(ins[w], outs[w].at[me], local_sem.at[w]) for w in range(n)]
        for cp in locals_:
            cp.start()
        sends = []
        for w in range(n):
            r2 = shards[w].shape[0] // 2
            for j, chip in enumerate(chips):
                cp = pltpu.make_async_remote_copy(
                    src_ref=ins[w].at[pl.ds(c * r2, r2), :], dst_ref=half(w, me, c),
                    send_sem=ici_send.at[w * 3 + j], recv_sem=ici_recv.at[w * 3 + j],
                    device_id=(*chip, c), device_id_type=MESH)
                cp.start()
                sends.append(cp)
        for w in range(n):
            for j, chip in enumerate(chips):
                slot = 2 * chip[0] + chip[1]
                got = half(w, slot, c)
                pltpu.make_async_remote_copy(
                    src_ref=got, dst_ref=got, send_sem=ici_send.at[w * 3 + j], recv_sem=ici_recv.at[w * 3 + j],
                    device_id=(*chip, c), device_id_type=MESH).wait_recv()
                cp = pltpu.make_async_remote_copy(
                    src_ref=got, dst_ref=got, send_sem=d2d_send.at[w * 3 + j], recv_sem=d2d_recv.at[w * 3 + j],
                    device_id=sibling, device_id_type=MESH)
                cp.start()
                sends.append(cp)
        for w in range(n):
            for j, chip in enumerate(chips):
                slot = 2 * chip[0] + chip[1]
                got = half(w, slot, 1 - c)
                pltpu.make_async_remote_copy(
                    src_ref=got, dst_ref=got, send_sem=d2d_send.at[w * 3 + j], recv_sem=d2d_recv.at[w * 3 + j],
                    device_id=sibling, device_id_type=MESH).wait_recv()
        for cp in sends:
            cp.wait_send()
        for cp in locals_:
            cp.wait()

    return _launch_copies(
        body, list(shards) + list(after), [_hbm_shape((4,) + s.shape, s.dtype) for s in shards],
        [pltpu.SemaphoreType.DMA((3 * n,)), pltpu.SemaphoreType.DMA((3 * n,)), pltpu.SemaphoreType.DMA((3 * n,)),
         pltpu.SemaphoreType.DMA((3 * n,)), pltpu.SemaphoreType.DMA((n,))],
        name=name, collective_id=collective_id)


def _gather_staged(w, small):
    R, C = w.shape
    R2 = R // 2
    rb = 256
    per = R2 // rb
    r2 = small.shape[0] // 2

    def body(w_ref, s_ref, out_ref, outs_ref, stage, ici_send, ici_recv, fwd_send, fwd_recv, in_sem, put_sem,
             push_send, push_recv):
        x, y, c, chips = _place()
        me = 2 * x + y
        sibling = (x, y, 1 - c)
        slots = [2 * chip[0] + chip[1] for chip in chips]
        _handshake([sibling] + [(*chip, c) for chip in chips])

        def ici(j, src, dst, k):
            return pltpu.make_async_remote_copy(
                src_ref=src, dst_ref=dst, send_sem=ici_send.at[k], recv_sem=ici_recv.at[k],
                device_id=(*chips[j], c), device_id_type=MESH)

        sends = []
        for j in range(3):
            sends.append(ici(j, w_ref.at[pl.ds(c * R2, R2), :], out_ref.at[me, pl.ds(c * R2, R2), :], j))
            sends.append(ici(j, s_ref.at[pl.ds(c * r2, r2), :], outs_ref.at[me, pl.ds(c * r2, r2), :], 3 + j))
        for cp in sends:
            cp.start()
        mine = pltpu.make_async_copy(s_ref, outs_ref.at[me], put_sem.at[2])
        mine.start()

        def own(k, to_vmem):
            rows = pl.ds(k * rb, rb)
            if to_vmem:
                return pltpu.make_async_copy(w_ref.at[rows, :], stage.at[k % 2], in_sem.at[k % 2])
            return pltpu.make_async_copy(stage.at[k % 2], out_ref.at[me, rows, :], put_sem.at[k % 2])

        for k in range(R // rb):
            if k >= 2:
                own(k - 2, False).wait()
            own(k, True).start()
            own(k, True).wait()
            own(k, False).start()
        for k in range(R // rb - 2, R // rb):
            own(k, False).wait()

        def block(j, k, hc):
            return out_ref.at[slots[j], pl.ds(hc * R2 + k * rb, rb), :]

        def push(t, hc):
            j, k = divmod(t, per)
            return pltpu.make_async_remote_copy(
                src_ref=stage.at[t % 2], dst_ref=block(j, k, hc), send_sem=push_send.at[t % 2],
                recv_sem=push_recv.at[t], device_id=sibling, device_id_type=MESH)

        passed = []
        for j in range(3):
            ici(j, w_ref.at[pl.ds(c * R2, R2), :], out_ref.at[slots[j], pl.ds(c * R2, R2), :], j).wait_recv()
            for k in range(per):
                t = j * per + k
                if t >= 2:
                    push(t - 2, c).wait_send()
                fetch = pltpu.make_async_copy(block(j, k, c), stage.at[t % 2], in_sem.at[t % 2])
                fetch.start()
                fetch.wait()
                push(t, c).start()
            got = outs_ref.at[slots[j], pl.ds(c * r2, r2), :]
            ici(j, got, got, 3 + j).wait_recv()
            fw = pltpu.make_async_remote_copy(
                src_ref=got, dst_ref=got, send_sem=fwd_send.at[j], recv_sem=fwd_recv.at[j],
                device_id=sibling, device_id_type=MESH)
            fw.start()
            passed.append(fw)
        for t in range(3 * per - 2, 3 * per):
            push(t, c).wait_send()
        for t in range(3 * per):
            push(t, 1 - c).wait_recv()
        for j in range(3):
            theirs = outs_ref.at[slots[j], pl.ds((1 - c) * r2, r2), :]
            pltpu.make_async_remote_copy(
                src_ref=theirs, dst_ref=theirs, send_sem=fwd_send.at[j], recv_sem=fwd_recv.at[j],
                device_id=sibling, device_id_type=MESH).wait_recv()
        for cp in sends + passed:
            cp.wait_send()
        mine.wait()

    dma = pltpu.SemaphoreType.DMA
    return pl.pallas_call(
        body, name="gather_in", in_specs=[ANY, ANY], out_specs=[ANY, ANY],
        out_shape=[_hbm_shape((4, R, C), w.dtype), _hbm_shape((4,) + small.shape, small.dtype)],
        scratch_shapes=[pltpu.VMEM((2, rb, C), w.dtype), dma((6,)), dma((6,)), dma((3,)), dma((3,)), dma((2,)),
                        dma((3,)), dma((2,)), dma((3 * per,))],
        compiler_params=pltpu.CompilerParams(vmem_limit_bytes=VMEM_LIMIT, collective_id=ID_GATHER_IN),
    )(w, small)


def _pair_blocks(R2):
    return _pick(R2, (256, 352, 128, 64, 32, 16))


def _pair_reduce(part, *, name):
    _, R, C = part.shape
    R2 = R // 2
    rb = _pick(R2, (512, 352, 256, 128, 64, 32, 16))
    nblk = R2 // rb
    steps = [(s, i) for s in range(4) for i in range(nblk)]
    n = len(steps)

    def body(part_ref, sums_ref, own_buf, oth_buf, rcv_buf, out_buf, own_sem, oth_sem, out_sem, send_sem, recv_sem):
        x, y, c, _ = _place()
        sibling = (x, y, 1 - c)
        _handshake([sibling])

        def fetch(t, half, buf, sem):
            s, i = steps[t]
            return pltpu.make_async_copy(part_ref.at[s, pl.ds(half * R2 + i * rb, rb), :], buf.at[t % 2], sem.at[t % 2])

        def push(t):
            return pltpu.make_async_remote_copy(
                src_ref=oth_buf.at[t % 2], dst_ref=rcv_buf.at[t % 3], send_sem=send_sem.at[t % 2],
                recv_sem=recv_sem.at[t % 2], device_id=sibling, device_id_type=MESH)

        def store(t):
            s, i = steps[t]
            return pltpu.make_async_copy(out_buf.at[t % 2], sums_ref.at[s, pl.ds(i * rb, rb), :], out_sem.at[t % 2])

        fetch(0, c, own_buf, own_sem).start()
        fetch(0, 1 - c, oth_buf, oth_sem).start()
        fetch(0, 1 - c, oth_buf, oth_sem).wait()
        push(0).start()
        for t in range(n):
            if t + 1 < n:
                fetch(t + 1, c, own_buf, own_sem).start()
                fetch(t + 1, 1 - c, oth_buf, oth_sem).start()
            fetch(t, c, own_buf, own_sem).wait()
            push(t).wait()
            if t + 1 < n:
                fetch(t + 1, 1 - c, oth_buf, oth_sem).wait()
                push(t + 1).start()
            if t >= 2:
                store(t - 2).wait()
            out_buf[t % 2] = (own_buf[t % 2].astype(F32) + rcv_buf[t % 3].astype(F32)).astype(BF16)
            store(t).start()
        for t in range(max(n - 2, 0), n):
            store(t).wait()

    buf = pltpu.VMEM((2, rb, C), BF16)
    sem2 = pltpu.SemaphoreType.DMA((2,))
    return pl.pallas_call(
        body, name=name, in_specs=[ANY], out_specs=ANY, out_shape=_hbm_shape((4, R2, C), BF16),
        scratch_shapes=[buf, buf, pltpu.VMEM((3, rb, C), BF16), buf, sem2, sem2, sem2, sem2, sem2],
        compiler_params=pltpu.CompilerParams(vmem_limit_bytes=VMEM_LIMIT, collective_id=ID_SIBLING),
    )(part)


def _chip_exchange(sums, *, name, collective_id=None):
    n = len(sums)

    def body(*refs):
        ins, outs = refs[:n], refs[n:2 * n]
        send_sem, recv_sem, local_sem = refs[2 * n:]
        x, y, c, chips = _place()
        me = 2 * x + y
        if collective_id is not None:
            _handshake([(*chip, c) for chip in chips])
        cps = []
        for w in range(n):
            lc = pltpu.make_async_copy(ins[w].at[me], outs[w].at[me], local_sem.at[w])
            lc.start()
            cps.append(lc)
            for j, chip in enumerate(chips):
                slot = 2 * chip[0] + chip[1]
                rc = pltpu.make_async_remote_copy(
                    src_ref=ins[w].at[slot], dst_ref=outs[w].at[me],
                    send_sem=send_sem.at[w * 3 + j], recv_sem=recv_sem.at[w * 3 + j],
                    device_id=(*chip, c), device_id_type=MESH)
                rc.start()
                cps.append(rc)
        for w in range(n):
            for j, chip in enumerate(chips):
                slot = 2 * chip[0] + chip[1]
                pltpu.make_async_remote_copy(
                    src_ref=ins[w].at[slot], dst_ref=outs[w].at[slot],
                    send_sem=send_sem.at[w * 3 + j], recv_sem=recv_sem.at[w * 3 + j],
                    device_id=(*chip, c), device_id_type=MESH).wait_recv()
        for w in range(n):
            cps[w * 4].wait()
            for j in range(3):
                cps[w * 4 + 1 + j].wait_send()

    return _launch_copies(
        body, sums, [_hbm_shape(s.shape, s.dtype) for s in sums],
        [pltpu.SemaphoreType.DMA((3 * n,)), pltpu.SemaphoreType.DMA((3 * n,)), pltpu.SemaphoreType.DMA((n,))],
        name=name, collective_id=collective_id)


def _sum_join(recv, *, name):
    _, R2, C = recv.shape
    rb = _pair_blocks(R2)
    nblk = R2 // rb

    def body(recv_ref, out_ref, in_buf, acc_buf, in_sem, loc_sem, send_sem, recv_sem):
        x, y, c, _ = _place()
        sibling = (x, y, 1 - c)
        _handshake([sibling])

        def fetch(t):
            return pltpu.make_async_copy(recv_ref.at[:, pl.ds(t * rb, rb), :], in_buf.at[t % 2], in_sem.at[t % 2])

        def rows(t, half):
            return out_ref.at[pl.ds(half * R2 + t * rb, rb), :]

        def put_local(t):
            return pltpu.make_async_copy(acc_buf.at[t % 2], rows(t, c), loc_sem.at[t])

        def put_remote(t, half):
            return pltpu.make_async_remote_copy(
                src_ref=acc_buf.at[t % 2], dst_ref=rows(t, half), send_sem=send_sem.at[t], recv_sem=recv_sem.at[t],
                device_id=sibling, device_id_type=MESH)

        fetch(0).start()
        for t in range(nblk):
            if t + 1 < nblk:
                fetch(t + 1).start()
            fetch(t).wait()
            if t >= 2:
                put_local(t - 2).wait()
                put_remote(t - 2, c).wait_send()
            acc = in_buf[t % 2, 0].astype(F32)
            for s in range(1, 4):
                acc = acc + in_buf[t % 2, s].astype(F32)
            acc_buf[t % 2] = acc
            put_local(t).start()
            put_remote(t, c).start()
        for t in range(max(nblk - 2, 0), nblk):
            put_local(t).wait()
            put_remote(t, c).wait_send()
        for t in range(nblk):
            put_remote(t, 1 - c).wait_recv()

    semn = pltpu.SemaphoreType.DMA((nblk,))
    return pl.pallas_call(
        body, name=name, in_specs=[ANY], out_specs=ANY, out_shape=_hbm_shape((2 * R2, C), F32),
        scratch_shapes=[pltpu.VMEM((2, 4, rb, C), BF16), pltpu.VMEM((2, rb, C), F32),
                        pltpu.SemaphoreType.DMA((2,)), semn, semn, semn],
        compiler_params=pltpu.CompilerParams(vmem_limit_bytes=VMEM_LIMIT, collective_id=ID_SIBLING),
    )(recv)


def _sum_small(pack):
    R, C = pack.shape

    def body(in_ref, out_ref, all_ref, send_sem, recv_sem):
        x, y, c, _ = _place()
        me = 4 * x + 2 * y + c
        all_ref[me] = in_ref[...]
        cps = []
        for k in range(1, 8):
            peer = (x ^ (k >> 2), y ^ ((k >> 1) & 1), c ^ (k & 1))
            cp = pltpu.make_async_remote_copy(
                src_ref=in_ref, dst_ref=all_ref.at[me], send_sem=send_sem.at[k - 1], recv_sem=recv_sem.at[k - 1],
                device_id=peer, device_id_type=MESH)
            cp.start()
            cps.append(cp)
        for k in range(1, 8):
            peer = (x ^ (k >> 2), y ^ ((k >> 1) & 1), c ^ (k & 1))
            slot = 4 * peer[0] + 2 * peer[1] + peer[2]
            pltpu.make_async_remote_copy(
                src_ref=in_ref, dst_ref=all_ref.at[slot], send_sem=send_sem.at[k - 1], recv_sem=recv_sem.at[k - 1],
                device_id=peer, device_id_type=MESH).wait_recv()
        for cp in cps:
            cp.wait_send()
        acc = all_ref[0]
        for d in range(1, 8):
            acc = acc + all_ref[d]
        out_ref[...] = acc

    return pl.pallas_call(
        body, name="sum_small", in_specs=[pl.BlockSpec(memory_space=pltpu.VMEM)],
        out_specs=pl.BlockSpec(memory_space=pltpu.VMEM), out_shape=jax.ShapeDtypeStruct((R, C), F32),
        scratch_shapes=[pltpu.VMEM((8, R, C), F32), pltpu.SemaphoreType.DMA((7,)), pltpu.SemaphoreType.DMA((7,))],
    )(pack)


ADAMW_BLOCK_BYTES = 2 * 1024 * 1024


def _adamw_block(R, C):
    padded = -(-C // 128) * 128
    rows = [rb for rb in range(8, R + 1, 8) if R % rb == 0 and rb * padded * 4 <= ADAMW_BLOCK_BYTES]
    if rows or R * padded * 4 <= ADAMW_BLOCK_BYTES:
        return (max(rows) if rows else R), C
    cols = [cb for cb in range(128, C + 1, 128) if C % cb == 0 and R * cb * 4 <= ADAMW_BLOCK_BYTES]
    return R, max(cols)


def _adamw(w, g, m, v, *, name):
    R, C = w.shape
    rb, cb = _adamw_block(R, C)
    c1 = 1.0 / (1.0 - B1 ** STEP)
    c2 = 1.0 / (1.0 - B2 ** STEP)

    def body(w_ref, g_ref, m_ref, v_ref, d_ref, nm_ref, nv_ref):
        gv = g_ref[...]
        m2 = B1 * m_ref[...] + (1.0 - B1) * gv
        v2 = B2 * v_ref[...] + (1.0 - B2) * gv * gv
        nm_ref[...] = m2
        nv_ref[...] = v2
        d_ref[...] = -LR * ((m2 * c1) / (jnp.sqrt(v2 * c2) + ADAM_EPS) + WD * w_ref[...])

    blk = pl.BlockSpec((rb, cb), lambda i, j: (i, j))
    return pl.pallas_call(
        body, name=name, grid=(R // rb, C // cb), in_specs=[blk] * 4, out_specs=[blk] * 3,
        out_shape=[jax.ShapeDtypeStruct((R, C), F32)] * 3, compiler_params=_params(2),
    )(w, g, m, v)


SMALL = (("norm1_w", D), ("norm2_w", D), ("final_norm_w", D), ("gate_b", GLA_H * DK), ("gla_norm_w", DV), ("sinks", N_Q))
PACK_W = 1024


def _pack_small(vals, w2, loss):
    rows = []
    for name, width in SMALL:
        v = vals[name].reshape(-1)
        rows.append(jnp.pad(v, (0, (-width) % PACK_W)).reshape(-1, PACK_W))
    rows.append(w2)
    rows.append(jnp.broadcast_to(loss.reshape(1, 1), (1, PACK_W)))
    pack = jnp.concatenate(rows, axis=0)
    return jnp.pad(pack, ((0, 32 - pack.shape[0]), (0, 0)))


def _unpack_small(pack):
    out, r = {}, 0
    for name, width in SMALL:
        nr = -(-width // PACK_W)
        out[name] = pack[r:r + nr].reshape(-1)[:width]
        r += nr
    out["w2"] = pack[r:r + RANK]
    out["loss"] = pack[r + RANK, 0]
    return out


def kernel(x, norm1_w, w_in, gla_gate_w2, gla_gate_b, attn_sinks, gla_norm_w, w_out, norm2_w, w_ffn_gate, w_ffn_up, w_ffn_down, final_norm_w, loss_target, m_norm1_w, m_w_in, m_gla_gate_w2, m_gla_gate_b, m_attn_sinks, m_gla_norm_w, m_w_out, m_norm2_w, m_w_ffn_gate, m_w_ffn_up, m_w_ffn_down, m_final_norm_w, v_norm1_w, v_w_in, v_gla_gate_w2, v_gla_gate_b, v_attn_sinks, v_gla_norm_w, v_w_out, v_norm2_w, v_w_ffn_gate, v_w_ffn_up, v_w_ffn_down, v_final_norm_w):
    chip = 2 * lax.axis_index("x") + lax.axis_index("y")
    w_in_s, w_out_s, w_gate_s, w_up_s, w_down_s, w2_s = (
        w_in[0], w_out[0], w_ffn_gate[0], w_ffn_up[0], w_ffn_down[0], gla_gate_w2[0])
    CS = SHARD_COLS

    g_in, g_w2 = _gather_staged(w_in_s.astype(BF16), w2_s)
    g_out, = _gather_shards([w_out_s.astype(BF16)], name="gather_out", collective_id=1, after=[g_w2])
    g_gate, g_up = _gather_shards([w_gate_s.astype(BF16), w_up_s.astype(BF16)], name="gather_gate_up",
                                  collective_id=7, after=[g_w2])
    g_down, = _gather_shards([w_down_s.astype(BF16)], name="gather_down", collective_id=8, after=[g_w2])
    w_main, w_lr = _w_in_to_main(g_in)
    w2_full = jnp.transpose(g_w2, (1, 0, 2)).reshape(RANK, GLA_H * DK)
    p = dict(
        norm1_w=norm1_w, norm2_w=norm2_w, final_norm_w=final_norm_w.reshape(1, D), gate_b=gla_gate_b,
        gla_norm_w=gla_norm_w, sinks=jnp.pad(attn_sinks, ((0, 0), (0, 128 - N_Q))),
        w_in_main=w_main, w_in_lr=w_lr,
        w2p=jnp.pad(w2_full, ((0, LRP - RANK), (0, 0))).astype(BF16),
        w_out=g_out.reshape(D, D),
        w_gate=g_gate, w_up=g_up, w_down=g_down.reshape(FFN, D),
    )

    tags = ["w_in", "w_out", "w_gate", "w_up", "w_down"]
    recv = {}

    def reduce_part(tag, part):
        sums = _pair_reduce(part, name="pair_reduce_" + tag)
        recv[tag], = _chip_exchange([sums], name="chip_exchange_" + tag, collective_id=2 + tags.index(tag))

    done = {}
    shard = {"w_out": (w_out_s, m_w_out[0], v_w_out[0]), "w_gate": (w_gate_s, m_w_ffn_gate[0], v_w_ffn_gate[0]),
             "w_up": (w_up_s, m_w_ffn_up[0], v_w_ffn_up[0]), "w_down": (w_down_s, m_w_ffn_down[0], v_w_ffn_down[0])}

    def finish(tag):
        grad = _sum_join(recv[tag], name="sum_join_" + tag)
        done[tag] = (grad, *_adamw(shard[tag][0], grad, shard[tag][1], shard[tag][2], name="adamw_" + tag))

    loss_blk, dx, g = _local_step(x[0], loss_target[0], p, reduce_part)
    reduce_part("w_in", _main_to_shards(g["w_in_main"], g["w_in_lr"]))
    for tag in ("w_down", "w_gate", "w_up", "w_out"):
        finish(tag)
    g_in = _sum_join(recv["w_in"], name="sum_join_w_in")

    small = _unpack_small(_sum_small(_pack_small(g, g["w2"], loss_blk[0, 0])))
    loss = small["loss"]
    g_w2_mine = lax.dynamic_slice_in_dim(small["w2"], chip * (GLA_H * DK // 4), GLA_H * DK // 4, axis=1)

    grads = dict(
        norm1_w=small["norm1_w"].reshape(1, D), w_in=g_in, gla_gate_w2=g_w2_mine,
        gla_gate_b=small["gate_b"].reshape(1, -1), attn_sinks=small["sinks"].reshape(1, N_Q),
        gla_norm_w=small["gla_norm_w"].reshape(1, DV), norm2_w=small["norm2_w"].reshape(1, D),
        final_norm_w=small["final_norm_w"].reshape(1, D))
    def lin(a):
        return jnp.transpose(a.reshape(D, CS))

    def unlin(a):
        return jnp.transpose(a)

    grads["w_in"] = lin(g_in)
    weights = dict(
        norm1_w=(norm1_w, m_norm1_w, v_norm1_w), w_in=(lin(w_in), lin(m_w_in), lin(v_w_in)),
        gla_gate_w2=(w2_s, m_gla_gate_w2[0], v_gla_gate_w2[0]), gla_gate_b=(gla_gate_b, m_gla_gate_b, v_gla_gate_b),
        attn_sinks=(attn_sinks, m_attn_sinks, v_attn_sinks), gla_norm_w=(gla_norm_w, m_gla_norm_w, v_gla_norm_w),
        norm2_w=(norm2_w, m_norm2_w, v_norm2_w),
        final_norm_w=(final_norm_w.reshape(1, D), m_final_norm_w.reshape(1, D), v_final_norm_w.reshape(1, D)))
    names = ["norm1_w", "w_in", "gla_gate_w2", "gla_gate_b", "attn_sinks", "gla_norm_w", "w_out", "norm2_w",
             "w_ffn_gate", "w_ffn_up", "w_ffn_down", "final_norm_w"]
    lead = {"norm1_w": False, "gla_gate_b": False, "attn_sinks": False, "gla_norm_w": False, "norm2_w": False}
    g_out_l, d_out, m_out, v_out = [], [], [], []
    early = {"w_out": "w_out", "w_ffn_gate": "w_gate", "w_ffn_up": "w_up", "w_ffn_down": "w_down"}
    for nm in names:
        if nm in early:
            gr, dl, nmn, nvn = done[early[nm]]
            w = shard[early[nm]][0]
        else:
            w, m, v = weights[nm]
            gr = grads[nm]
            dl, nmn, nvn = _adamw(w, gr, m, v, name="adamw_" + nm)
        if nm == "w_in":
            gr, dl, nmn, nvn, w = g_in, unlin(dl), unlin(nmn), unlin(nvn), w_in_s
        if nm == "final_norm_w":
            shape = (D,)
        elif nm in lead:
            shape = w.shape
        else:
            shape = (1,) + w.shape
        g_out_l.append(gr.reshape(shape))
        d_out.append(dl.reshape(shape))
        m_out.append(nmn.reshape(shape))
        v_out.append(nvn.reshape(shape))
    return (loss, dx[None], *g_out_l, *d_out, *m_out, *v_out)
```

```python
import functools

import jax
import jax.numpy as jnp
from jax import lax
from jax.experimental import pallas as pl
from jax.experimental.pallas import tpu as pltpu
from jax.experimental.pallas import tpu_sc as plsc

F32 = jnp.float32
BF16 = jnp.bfloat16

D = 2048
HEAD_DIM = 64
N_Q = 32
N_KV = 4
GROUP = 8
WINDOW = 128
GLA_H = 4
DK = 256
DV = 512
RANK = 16
CHUNK = 64
FFN = 5632
EPS = 1e-6
MASK_VALUE = -1e30
GATE_NORM = 16.0
LR, B1, B2, ADAM_EPS, WD, STEP = 0.001, 0.9, 0.999, 1e-08, 0.01, 10

MAIN = 12800
O_MERGE, O_GLA, O_ATTN = 0, 6144, 10240
W_MERGE, W_GLA, W_ATTN = 3 * DV, 2 * DK + DV, D + 2 * N_KV * HEAD_DIM
LR_AT = 6656
LRP = 128


def _main_pieces():
    o_gq, o_gk, o_gv, o_gr, o_ga, o_gb = 2560, 3584, 4608, LR_AT + RANK, LR_AT + RANK + D, LR_AT + RANK + 2 * D
    pieces = []
    for h in range(GLA_H):
        pieces += [(o_gr + DV * h, DV), (o_ga + DV * h, DV), (o_gb + DV * h, DV)]
    for h in range(GLA_H):
        pieces += [(o_gq + DK * h, DK), (o_gk + DK * h, DK), (o_gv + DV * h, DV)]
    pieces.append((0, W_ATTN))
    return pieces


SHARD_COLS = 3204

VMEM_LIMIT = 56 * 1024 * 1024
MESH = pl.DeviceIdType.MESH
ANY = pl.BlockSpec(memory_space=pl.ANY)


def _params(ngrid):
    return pltpu.CompilerParams(dimension_semantics=("arbitrary",) * ngrid, vmem_limit_bytes=VMEM_LIMIT)


def _shard_segments(lo, hi):
    segs = []
    while lo < hi:
        s = lo // SHARD_COLS
        e = min(hi, (s + 1) * SHARD_COLS)
        segs.append((s, lo - s * SHARD_COLS, e - s * SHARD_COLS))
        lo = e
    return segs


def _lanes(pieces):
    return pieces[0] if len(pieces) == 1 else jnp.concatenate(pieces, axis=1)


def _w_in_to_main(shards):
    rb = 256

    def body(g_ref, main_ref, lr_ref):
        at = 0
        for a, w in _main_pieces():
            main_ref[:, at:at + w] = _lanes([g_ref[s, :, lo:hi] for s, lo, hi in _shard_segments(a, a + w)])
            at += w
        lr = [g_ref[s, :, lo:hi] for s, lo, hi in _shard_segments(LR_AT, LR_AT + RANK)]
        lr_ref[...] = _lanes(lr + [jnp.zeros((rb, LRP - RANK), lr_ref.dtype)])

    return pl.pallas_call(
        body, name="w_in_to_main", grid=(D // rb,),
        in_specs=[pl.BlockSpec((4, rb, SHARD_COLS), lambda i: (0, i, 0))],
        out_specs=[pl.BlockSpec((rb, MAIN), lambda i: (i, 0)), pl.BlockSpec((rb, LRP), lambda i: (i, 0))],
        out_shape=[jax.ShapeDtypeStruct((D, MAIN), shards.dtype), jax.ShapeDtypeStruct((D, LRP), shards.dtype)],
        compiler_params=_params(1),
    )(shards)


def _main_to_shards(g_main, g_lr):
    rb = 256
    at, sources = 0, [(LR_AT, RANK, 1, 0)]
    for a, w in _main_pieces():
        sources.append((a, w, 0, at))
        at += w
    sources.sort()

    def body(main_ref, lr_ref, out_ref):
        refs = (main_ref, lr_ref)
        for s in range(4):
            lo, hi = s * SHARD_COLS, (s + 1) * SHARD_COLS
            pieces = []
            for a, w, which, src in sources:
                b, e = max(a, lo), min(a + w, hi)
                if b < e:
                    pieces.append(refs[which][:, src + b - a:src + e - a])
            out_ref[s] = _lanes(pieces)

    return pl.pallas_call(
        body, name="main_to_shards", grid=(D // rb,),
        in_specs=[pl.BlockSpec((rb, MAIN), lambda i: (i, 0)), pl.BlockSpec((rb, LRP), lambda i: (i, 0))],
        out_specs=pl.BlockSpec((4, rb, SHARD_COLS), lambda i: (0, i, 0)),
        out_shape=jax.ShapeDtypeStruct((4, D, SHARD_COLS), g_main.dtype), compiler_params=_params(1),
    )(g_main, g_lr)


def _pick(n, cands):
    for c in cands:
        if n % c == 0:
            return c
    return n


def _sigmoid(x):
    return 1.0 / (1.0 + jnp.exp(-x))


def _dot(a, b, ca, cb):
    return lax.dot_general(a, b, (((ca,), (cb,)), ((), ())), preferred_element_type=F32)


def _matmul(a, b, *, name, ta=False, tb=False, add=None, out_dtype=F32, bm=None, bj=None, bc=None,
            b_slots=False, out_slots=False):
    C, M = a.shape if ta else a.shape[::-1]
    if b_slots:
        if tb:
            J, bc = b.shape[1], b.shape[2]
            assert b.shape[0] * bc == C
        else:
            bj = b.shape[2]
            J = b.shape[0] * bj
            assert b.shape[1] == C
    else:
        J = b.shape[0] if tb else b.shape[1]
        assert (b.shape[1] if tb else b.shape[0]) == C
    bm = bm or _pick(M, (1024, 512, 256, 128) if C <= FFN else (512, 256, 128))
    bj = bj or _pick(J, (1280, 1024, 512, 256, 128) if C <= D else (512, 256, 128))
    bc = bc or (C if C <= FFN else _pick(C, (6400,)))
    nm, nj, nc = M // bm, J // bj, C // bc
    has_add = add is not None

    def body(*refs):
        a_ref, b_ref = refs[0], refs[1]
        add_ref = refs[2] if has_add else None
        o_ref = refs[3] if has_add else refs[2]
        p = _dot(a_ref[...].astype(BF16), b_ref[...].astype(BF16), 0 if ta else 1, 1 if tb else 0)

        def finish(acc):
            if has_add:
                acc = acc + add_ref[...]
            o_ref[...] = acc.astype(o_ref.dtype)

        if nc == 1:
            finish(p)
        else:
            acc_ref = refs[-1]
            c = pl.program_id(2)

            @pl.when(c == 0)
            def _():
                acc_ref[...] = p

            @pl.when(c > 0)
            def _():
                acc_ref[...] += p

            @pl.when(c == nc - 1)
            def _():
                finish(acc_ref[...])

    a_spec = pl.BlockSpec((bc, bm), lambda m, j, c: (c, m)) if ta else pl.BlockSpec((bm, bc), lambda m, j, c: (m, c))
    if b_slots:
        b_spec = (pl.BlockSpec((None, bj, bc), lambda m, j, c: (c, j, 0)) if tb
                  else pl.BlockSpec((None, bc, bj), lambda m, j, c: (j, c, 0)))
    else:
        b_spec = (pl.BlockSpec((bj, bc), lambda m, j, c: (j, c)) if tb
                  else pl.BlockSpec((bc, bj), lambda m, j, c: (c, j)))
    if out_slots:
        assert not has_add
        o_spec = pl.BlockSpec((None, bm, bj), lambda m, j, c: (j, m, 0))
        out_shape = jax.ShapeDtypeStruct((nj, M, bj), out_dtype)
    else:
        o_spec = pl.BlockSpec((bm, bj), lambda m, j, c: (m, j))
        out_shape = jax.ShapeDtypeStruct((M, J), out_dtype)
    in_specs = [a_spec, b_spec] + ([o_spec] if has_add else [])
    args = (a, b) + ((add,) if has_add else ())
    return pl.pallas_call(
        body, name=name, grid=(nm, nj, nc), in_specs=in_specs, out_specs=o_spec,
        out_shape=out_shape,
        scratch_shapes=[pltpu.VMEM((bm, bj), F32)] if nc > 1 else [],
        compiler_params=_params(3),
    )(*args)


def _rows(T):
    return _pick(T, (256, 128))


def _rms_fwd(x, w, *, name):
    T = x.shape[0]
    rb = _rows(T)

    def body(x_ref, w_ref, u_ref, ut_ref):
        xv = x_ref[...]
        r = lax.rsqrt(jnp.mean(xv * xv, axis=-1, keepdims=True) + EPS)
        u = xv * r * w_ref[...]
        u_ref[...] = u.astype(BF16)
        ut_ref[...] = u.T.astype(BF16)

    return pl.pallas_call(
        body, name=name, grid=(T // rb,),
        in_specs=[pl.BlockSpec((rb, D), lambda i: (i, 0)), pl.BlockSpec((1, D), lambda i: (0, 0))],
        out_specs=[pl.BlockSpec((rb, D), lambda i: (i, 0)), pl.BlockSpec((D, rb), lambda i: (0, i))],
        out_shape=[jax.ShapeDtypeStruct((T, D), BF16), jax.ShapeDtypeStruct((D, T), BF16)], compiler_params=_params(1),
    )(x, w)


def _out_proj(merged, w_out, x, norm_w):
    T = merged.shape[0]
    bm = _pick(T, (512, 256))

    def body(a_ref, b_ref, x_ref, w_ref, h_ref, v_ref, vt_ref):
        h = _dot(a_ref[...], b_ref[...], 1, 0) + x_ref[...]
        h_ref[...] = h
        r = lax.rsqrt(jnp.mean(h * h, axis=-1, keepdims=True) + EPS)
        v2 = h * r * w_ref[...]
        v_ref[...] = v2.astype(BF16)
        vt_ref[...] = v2.T.astype(BF16)

    blk = pl.BlockSpec((bm, D), lambda i: (i, 0))
    return pl.pallas_call(
        body, name="out_proj", grid=(T // bm,),
        in_specs=[blk, pl.BlockSpec((D, D), lambda i: (0, 0)), blk, pl.BlockSpec((1, D), lambda i: (0, 0))],
        out_specs=[blk, blk, pl.BlockSpec((D, bm), lambda i: (0, i))],
        out_shape=[jax.ShapeDtypeStruct((T, D), F32), jax.ShapeDtypeStruct((T, D), BF16),
                   jax.ShapeDtypeStruct((D, T), BF16)],
        compiler_params=_params(1),
    )(merged, w_out, x, norm_w)


def _rms_bwd(dy, h, w, resid, *, name):
    T = h.shape[0]
    rb = _rows(T)

    def body(dy_ref, h_ref, w_ref, res_ref, dh_ref, dhb_ref, dw_ref):
        @pl.when(pl.program_id(0) == 0)
        def _():
            dw_ref[...] = jnp.zeros_like(dw_ref)

        hv = h_ref[...]
        r = lax.rsqrt(jnp.mean(hv * hv, axis=-1, keepdims=True) + EPS)
        hn = hv * r
        dyv = dy_ref[...]
        dw_ref[...] += jnp.sum(dyv * hn, axis=0, keepdims=True)
        t = dyv * w_ref[...]
        dh = res_ref[...] + r * (t - hn * jnp.mean(t * hn, axis=-1, keepdims=True))
        dh_ref[...] = dh
        dhb_ref[...] = dh.astype(BF16)

    blk = pl.BlockSpec((rb, D), lambda i: (i, 0))
    vec = pl.BlockSpec((1, D), lambda i: (0, 0))
    return pl.pallas_call(
        body, name=name, grid=(T // rb,), in_specs=[blk, blk, vec, blk], out_specs=[blk, blk, vec],
        out_shape=[jax.ShapeDtypeStruct((T, D), F32), jax.ShapeDtypeStruct((T, D), BF16),
                   jax.ShapeDtypeStruct((1, D), F32)],
        compiler_params=_params(1),
    )(dy, h, w, resid)


def _rms_bwd_input(du_main, dproj_lr, w_in_lr, x, w, resid):
    T = x.shape[0]
    rb = _rows(T)

    def body(du_ref, dl_ref, wl_ref, x_ref, w_ref, res_ref, dx_ref, dw_ref):
        @pl.when(pl.program_id(0) == 0)
        def _():
            dw_ref[...] = jnp.zeros_like(dw_ref)

        xv = x_ref[...]
        r = lax.rsqrt(jnp.mean(xv * xv, axis=-1, keepdims=True) + EPS)
        xn = xv * r
        du = du_ref[...] + _dot(dl_ref[...], wl_ref[...], 1, 1)
        dw_ref[...] += jnp.sum(du * xn, axis=0, keepdims=True)
        t = du * w_ref[...]
        dx_ref[...] = res_ref[...] + r * (t - xn * jnp.mean(t * xn, axis=-1, keepdims=True))

    blk = pl.BlockSpec((rb, D), lambda i: (i, 0))
    vec = pl.BlockSpec((1, D), lambda i: (0, 0))
    return pl.pallas_call(
        body, name="rms1_bwd", grid=(T // rb,),
        in_specs=[blk, pl.BlockSpec((rb, LRP), lambda i: (i, 0)), pl.BlockSpec((D, LRP), lambda i: (0, 0)), blk, vec,
                  blk],
        out_specs=[blk, vec], out_shape=[jax.ShapeDtypeStruct((T, D), F32), jax.ShapeDtypeStruct((1, D), F32)],
        compiler_params=_params(1),
    )(du_main, dproj_lr, w_in_lr, x, w, resid)


def _final_loss(h, w, target):
    T = h.shape[0]
    rb = _rows(T)

    def body(h_ref, w_ref, t_ref, loss_ref, dh_ref, dhb_ref, dw_ref):
        @pl.when(pl.program_id(0) == 0)
        def _():
            dw_ref[...] = jnp.zeros_like(dw_ref)
            loss_ref[...] = jnp.zeros_like(loss_ref)

        hv = h_ref[...]
        wv = w_ref[...]
        r = lax.rsqrt(jnp.mean(hv * hv, axis=-1, keepdims=True) + EPS)
        hn = hv * r
        e = hn * wv - t_ref[...]
        row = jnp.sum(e * e, axis=-1, keepdims=True) * (0.5 / D)
        loss_ref[...] += jnp.broadcast_to(jnp.sum(row, axis=0, keepdims=True), loss_ref.shape)
        dy = e * (1.0 / D)
        dw_ref[...] += jnp.sum(dy * hn, axis=0, keepdims=True)
        t = dy * wv
        dh = r * (t - hn * jnp.mean(t * hn, axis=-1, keepdims=True))
        dh_ref[...] = dh
        dhb_ref[...] = dh.astype(BF16)

    blk = pl.BlockSpec((rb, D), lambda i: (i, 0))
    vec = pl.BlockSpec((1, D), lambda i: (0, 0))
    return pl.pallas_call(
        body, name="final_loss", grid=(T // rb,), in_specs=[blk, vec, blk],
        out_specs=[pl.BlockSpec((8, 128), lambda i: (0, 0)), blk, blk, vec],
        out_shape=[jax.ShapeDtypeStruct((8, 128), F32), jax.ShapeDtypeStruct((T, D), F32),
                   jax.ShapeDtypeStruct((T, D), BF16), jax.ShapeDtypeStruct((1, D), F32)],
        compiler_params=_params(1),
    )(h, w, target)


PANEL = FFN // 4


def _ffn_fwd(v2, w_gate, w_up):
    T = v2.shape[0]
    bm = _pick(T, (512, 256))

    def body(a_ref, bg_ref, bu_ref, gate_ref, up_ref, ff_ref, fft_ref):
        a = a_ref[...]
        g = _dot(a, bg_ref[...], 1, 0)
        u = _dot(a, bu_ref[...], 1, 0)
        gate_ref[...] = g
        up_ref[...] = u
        ff = g * _sigmoid(g) * u
        ff_ref[...] = ff.astype(BF16)
        fft_ref[...] = ff.T.astype(BF16)

    panel = pl.BlockSpec((None, D, PANEL), lambda j, m: (j, 0, 0))
    out = pl.BlockSpec((bm, PANEL), lambda j, m: (m, j))
    return pl.pallas_call(
        body, name="ffn_fwd", grid=(4, T // bm),
        in_specs=[pl.BlockSpec((bm, D), lambda j, m: (m, 0)), panel, panel],
        out_specs=[out, out, out, pl.BlockSpec((PANEL, bm), lambda j, m: (j, m))],
        out_shape=[jax.ShapeDtypeStruct((T, FFN), F32), jax.ShapeDtypeStruct((T, FFN), F32),
                   jax.ShapeDtypeStruct((T, FFN), BF16), jax.ShapeDtypeStruct((FFN, T), BF16)],
        compiler_params=_params(2),
    )(v2, w_gate, w_up)


def _ffn_bwd_hidden(dh2, w_down, gate, up):
    T = dh2.shape[0]
    bm = _pick(T, (512, 256))

    def body(a_ref, b_ref, g_ref, u_ref, dg_ref, du_ref):
        d = _dot(a_ref[...], b_ref[...], 1, 1)
        g = g_ref[...]
        sg = _sigmoid(g)
        du_ref[...] = (d * g * sg).astype(BF16)
        dg_ref[...] = (d * u_ref[...] * sg * (1.0 + g * (1.0 - sg))).astype(BF16)

    blk = pl.BlockSpec((bm, PANEL), lambda j, m: (m, j))
    return pl.pallas_call(
        body, name="ffn_bwd_hidden", grid=(4, T // bm),
        in_specs=[pl.BlockSpec((bm, D), lambda j, m: (m, 0)), pl.BlockSpec((PANEL, D), lambda j, m: (j, 0)), blk, blk],
        out_specs=[blk, blk], out_shape=[jax.ShapeDtypeStruct((T, FFN), BF16)] * 2, compiler_params=_params(2),
    )(dh2, w_down, gate, up)


def _ffn_bwd_input(dgate, dup, w_gate, w_up):
    T = dgate.shape[0]
    bm = _pick(T, (1024, 512, 256))
    bj = 1024

    def body(ag_ref, au_ref, bg_ref, bu_ref, o_ref, acc_ref):
        c = pl.program_id(2)
        p = _dot(ag_ref[...], bg_ref[...], 1, 1) + _dot(au_ref[...], bu_ref[...], 1, 1)

        @pl.when(c == 0)
        def _():
            acc_ref[...] = p

        @pl.when(c > 0)
        def _():
            acc_ref[...] += p

        @pl.when(c == 3)
        def _():
            o_ref[...] = acc_ref[...]

    a = pl.BlockSpec((bm, PANEL), lambda m, j, c: (m, c))
    b = pl.BlockSpec((None, bj, PANEL), lambda m, j, c: (c, j, 0))
    return pl.pallas_call(
        body, name="ffn_bwd_input", grid=(T // bm, D // bj, 4), in_specs=[a, a, b, b],
        out_specs=pl.BlockSpec((bm, bj), lambda m, j, c: (m, j)), out_shape=jax.ShapeDtypeStruct((T, D), F32),
        scratch_shapes=[pltpu.VMEM((bm, bj), F32)], compiler_params=_params(3),
    )(dgate, dup, w_gate, w_up)


def _colsum(x, *, name):
    T, W = x.shape
    rb = _rows(T)

    def body(x_ref, o_ref):
        @pl.when(pl.program_id(0) == 0)
        def _():
            o_ref[...] = jnp.zeros_like(o_ref)

        o_ref[...] += jnp.sum(x_ref[...], axis=0, keepdims=True)

    return pl.pallas_call(
        body, name=name, grid=(T // rb,), in_specs=[pl.BlockSpec((rb, W), lambda i: (i, 0))],
        out_specs=pl.BlockSpec((1, W), lambda i: (0, 0)), out_shape=jax.ShapeDtypeStruct((1, W), F32),
        compiler_params=_params(1),
    )(x)


def _merge_fwd(attn_o, gla_raw, proj, gla_norm_w):
    T = attn_o.shape[0]
    rb = _rows(T)

    def body(a_ref, g_ref, gr_ref, ga_ref, gb_ref, w_ref, o_ref, ot_ref):
        graw = g_ref[...]
        r = lax.rsqrt(jnp.mean(graw * graw, axis=-1, keepdims=True) + EPS)
        gr = gr_ref[...]
        go = graw * r * w_ref[...] * (gr * _sigmoid(gr))
        merged = _sigmoid(ga_ref[...]) * a_ref[...] + _sigmoid(gb_ref[...]) * go
        o_ref[...] = merged.astype(BF16)
        ot_ref[...] = merged.T.astype(BF16)

    def sec(k):
        return pl.BlockSpec((rb, DV), lambda i, h: (i, O_MERGE // DV + 3 * h + k))

    blk = pl.BlockSpec((rb, DV), lambda i, h: (i, h))
    return pl.pallas_call(
        body, name="merge_fwd", grid=(T // rb, GLA_H),
        in_specs=[blk, blk, sec(0), sec(1), sec(2), pl.BlockSpec((1, DV), lambda i, h: (0, 0))],
        out_specs=[blk, pl.BlockSpec((DV, rb), lambda i, h: (h, i))],
        out_shape=[jax.ShapeDtypeStruct((T, D), BF16), jax.ShapeDtypeStruct((D, T), BF16)], compiler_params=_params(2),
    )(attn_o, gla_raw, proj, proj, proj, gla_norm_w)


def _merge_bwd(dm, attn_o, gla_raw, proj, gla_norm_w):
    T = attn_o.shape[0]
    rb = _rows(T)

    def body(dm_ref, a_ref, g_ref, gr_ref, ga_ref, gb_ref, w_ref, da_ref, dg_ref, dp_ref, dw_ref):
        @pl.when((pl.program_id(0) == 0) & (pl.program_id(1) == 0))
        def _():
            dw_ref[...] = jnp.zeros_like(dw_ref)

        dmv = dm_ref[...]
        av = a_ref[...]
        graw = g_ref[...]
        gr = gr_ref[...]
        wv = w_ref[...]
        sa = _sigmoid(ga_ref[...])
        sb = _sigmoid(gb_ref[...])
        r = lax.rsqrt(jnp.mean(graw * graw, axis=-1, keepdims=True) + EPS)
        gnh = graw * r
        gn = gnh * wv
        sr = _sigmoid(gr)
        sl = gr * sr
        go = gn * sl
        da_ref[...] = dmv * sa
        dgo = dmv * sb
        dp_ref[:, 0:DV] = (dgo * gn * sr * (1.0 + gr * (1.0 - sr))).astype(BF16)
        dp_ref[:, DV:2 * DV] = (dmv * av * sa * (1.0 - sa)).astype(BF16)
        dp_ref[:, 2 * DV:3 * DV] = (dmv * go * sb * (1.0 - sb)).astype(BF16)
        dgn = dgo * sl
        dw_ref[...] += jnp.sum(dgn * gnh, axis=0, keepdims=True)
        t = dgn * wv
        dg_ref[...] = r * (t - gnh * jnp.mean(t * gnh, axis=-1, keepdims=True))

    def sec(k):
        return pl.BlockSpec((rb, DV), lambda i, h: (i, O_MERGE // DV + 3 * h + k))

    blk = pl.BlockSpec((rb, DV), lambda i, h: (i, h))
    vec = pl.BlockSpec((1, DV), lambda i, h: (0, 0))
    return pl.pallas_call(
        body, name="merge_bwd", grid=(T // rb, GLA_H),
        in_specs=[blk, blk, blk, sec(0), sec(1), sec(2), vec],
        out_specs=[blk, blk, pl.BlockSpec((rb, W_MERGE), lambda i, h: (i, O_MERGE // W_MERGE + h)), vec],
        out_shape=[jax.ShapeDtypeStruct((T, D), F32), jax.ShapeDtypeStruct((T, D), F32),
                   jax.ShapeDtypeStruct((T, MAIN), BF16), jax.ShapeDtypeStruct((1, DV), F32)],
        compiler_params=_params(2),
    )(dm, attn_o, gla_raw, proj, proj, proj, gla_norm_w)


def _attn_band(n):
    qi = lax.broadcasted_iota(jnp.int32, (GROUP * WINDOW, WINDOW), 0) & (WINDOW - 1)
    kj = lax.broadcasted_iota(jnp.int32, (GROUP * WINDOW, WINDOW), 1)
    cur = kj <= qi
    return cur, cur | (n > 0)


def _stack_heads(ref, h, scale=1.0):
    rows = jnp.concatenate(
        [ref[:, (h * GROUP + g) * HEAD_DIM:(h * GROUP + g + 1) * HEAD_DIM] for g in range(GROUP)], axis=0)
    return (rows * scale).astype(BF16) if scale != 1.0 else rows.astype(BF16)


def _stack_sinks(s_ref, h):
    return jnp.concatenate(
        [jnp.broadcast_to(s_ref[:, h * GROUP + g:h * GROUP + g + 1], (WINDOW, 1)) for g in range(GROUP)], axis=0)


def _attn_probs(qs, kp, kc, band, sink):
    cur, live = band
    s = jnp.where(cur, _dot(qs, kc, 1, 1), _dot(qs, kp, 1, 1))
    s = jnp.where(live, s, MASK_VALUE)
    m = jnp.maximum(jnp.max(s, axis=-1, keepdims=True), sink)
    e = jnp.exp(s - m)
    es = jnp.exp(sink - m)
    inv = 1.0 / (jnp.sum(e, axis=-1, keepdims=True) + es)
    return e * inv, es * inv


def _unfold(cur, x):
    mine = jnp.where(cur, x, 0.0)
    return mine.astype(BF16), (x - mine).astype(BF16)


def _attn_specs(nb, rev):
    def at(n):
        return (nb - 1 - n) if rev else n

    kcol, vcol = (O_ATTN + D) // 256, (O_ATTN + D) // 256 + 1
    q = pl.BlockSpec((WINDOW, D), lambda n: (at(n), O_ATTN // D))
    kc = pl.BlockSpec((WINDOW, 256), lambda n: (at(n), kcol))
    kp = pl.BlockSpec((WINDOW, 256), lambda n: (jnp.maximum(at(n) - 1, 0), kcol))
    vc = pl.BlockSpec((WINDOW, 256), lambda n: (at(n), vcol))
    vp = pl.BlockSpec((WINDOW, 256), lambda n: (jnp.maximum(at(n) - 1, 0), vcol))
    sk = pl.BlockSpec((1, 128), lambda n: (0, 0))
    o = pl.BlockSpec((WINDOW, D), lambda n: (at(n), 0))
    return q, kc, kp, vc, vp, sk, o


def _attn_fwd(proj, sinks):
    T = proj.shape[0]
    nb = T // WINDOW

    def body(q_ref, kc_ref, kp_ref, vc_ref, vp_ref, s_ref, o_ref):
        band = _attn_band(pl.program_id(0))
        for h in range(N_KV):
            hs = slice(h * HEAD_DIM, (h + 1) * HEAD_DIM)
            p, _ = _attn_probs(_stack_heads(q_ref, h, HEAD_DIM ** -0.5), kp_ref[:, hs].astype(BF16),
                               kc_ref[:, hs].astype(BF16), band, _stack_sinks(s_ref, h))
            p_cur, p_prev = _unfold(band[0], p)
            o = _dot(p_cur, vc_ref[:, hs].astype(BF16), 1, 0) + _dot(p_prev, vp_ref[:, hs].astype(BF16), 1, 0)
            for g in range(GROUP):
                hg = h * GROUP + g
                o_ref[:, hg * HEAD_DIM:(hg + 1) * HEAD_DIM] = o[g * WINDOW:(g + 1) * WINDOW]

    q, kc, kp, vc, vp, sk, o = _attn_specs(nb, False)
    return pl.pallas_call(
        body, name="attn_fwd", grid=(nb,), in_specs=[q, kc, kp, vc, vp, sk], out_specs=o,
        out_shape=jax.ShapeDtypeStruct((T, D), F32), compiler_params=_params(1),
    )(proj, proj, proj, proj, proj, sinks)


def _attn_bwd(proj, sinks, d_o, dproj):
    T = proj.shape[0]
    nb = T // WINDOW
    kat, vat = D, D + N_KV * HEAD_DIM

    def body(q_ref, kc_ref, kp_ref, vc_ref, vp_ref, s_ref, do_ref, _, dp_ref, ds_ref, ck_ref, cv_ref):
        i = pl.program_id(0)
        n = nb - 1 - i

        @pl.when(i == 0)
        def _():
            ck_ref[...] = jnp.zeros_like(ck_ref)
            cv_ref[...] = jnp.zeros_like(cv_ref)
            ds_ref[...] = jnp.zeros_like(ds_ref)

        band = _attn_band(n)
        for h in range(N_KV):
            hs = slice(h * HEAD_DIM, (h + 1) * HEAD_DIM)
            kp, kc = kp_ref[:, hs].astype(BF16), kc_ref[:, hs].astype(BF16)
            vp, vc = vp_ref[:, hs].astype(BF16), vc_ref[:, hs].astype(BF16)
            qs = _stack_heads(q_ref, h, HEAD_DIM ** -0.5)
            do = _stack_heads(do_ref, h)
            p, ps = _attn_probs(qs, kp, kc, band, _stack_sinks(s_ref, h))
            dp = jnp.where(band[0], _dot(do, vc, 1, 1), _dot(do, vp, 1, 1))
            delta = jnp.sum(p * dp, axis=-1, keepdims=True)
            ds_cur, ds_prev = _unfold(band[0], p * (dp - delta))
            p_cur, p_prev = _unfold(band[0], p)
            dsink = -ps * delta
            dq = ((_dot(ds_cur, kc, 1, 0) + _dot(ds_prev, kp, 1, 0)) * (HEAD_DIM ** -0.5)).astype(BF16)
            for g in range(GROUP):
                hg = h * GROUP + g
                rows = slice(g * WINDOW, (g + 1) * WINDOW)
                ds_ref[hg:hg + 1, :] += jnp.broadcast_to(jnp.sum(dsink[rows], axis=0, keepdims=True), (1, 128))
                dp_ref[:, hg * HEAD_DIM:(hg + 1) * HEAD_DIM] = dq[rows]
            dp_ref[:, kat + h * HEAD_DIM:kat + (h + 1) * HEAD_DIM] = (_dot(ds_cur, qs, 0, 0) + ck_ref[:, hs]).astype(BF16)
            dp_ref[:, vat + h * HEAD_DIM:vat + (h + 1) * HEAD_DIM] = (_dot(p_cur, do, 0, 0) + cv_ref[:, hs]).astype(BF16)
            ck_ref[:, hs] = _dot(ds_prev, qs, 0, 0)
            cv_ref[:, hs] = _dot(p_prev, do, 0, 0)

    q, kc, kp, vc, vp, sk, o = _attn_specs(nb, True)
    return pl.pallas_call(
        body, name="attn_bwd", grid=(nb,), in_specs=[q, kc, kp, vc, vp, sk, o, ANY],
        out_specs=[pl.BlockSpec((WINDOW, W_ATTN), lambda n: (nb - 1 - n, O_ATTN // W_ATTN)),
                   pl.BlockSpec((N_Q, 128), lambda n: (0, 0))],
        out_shape=[jax.ShapeDtypeStruct((T, MAIN), BF16), jax.ShapeDtypeStruct((N_Q, 128), F32)],
        scratch_shapes=[pltpu.VMEM((WINDOW, 256), F32), pltpu.VMEM((WINDOW, 256), F32)],
        input_output_aliases={7: 0}, compiler_params=_params(1),
    )(proj, proj, proj, proj, proj, sinks, d_o, dproj)


def _split3(x):
    hi = x.astype(BF16)
    r1 = x - hi.astype(F32)
    mid = r1.astype(BF16)
    lo = (r1 - mid.astype(F32)).astype(BF16)
    return hi, mid, lo


def _tri_sum(tri, x):
    hi, mid, lo = _split3(x)
    return _dot(tri, hi, 1, 0) + _dot(tri, mid, 1, 0) + _dot(tri, lo, 1, 0)


def _chunk_masks(rb):
    r = lax.broadcasted_iota(jnp.int32, (rb, rb), 0)
    c = lax.broadcasted_iota(jnp.int32, (rb, rb), 1)
    same = (r ^ c) < CHUNK
    return same & (r >= c), same & (r <= c)


def _per_chunk_rows(x, per, pick):
    return jnp.concatenate(
        [jnp.broadcast_to(pick(x[i * CHUNK:(i + 1) * CHUNK]), (CHUNK, x.shape[1])) for i in range(per)], axis=0)


def _gla_block(q, k, lr, w2, b, lower):
    per = q.shape[0] // CHUNK
    logit = _dot(lr.astype(BF16), w2.astype(BF16), 1, 0) + b
    la = (jnp.minimum(logit, 0.0) - jnp.log(1.0 + jnp.exp(-jnp.abs(logit)))) * (1.0 / GATE_NORM)
    g = _tri_sum(lower, la)
    gl = _per_chunk_rows(g, per, lambda c: c[CHUNK - 1:CHUNK])
    eg = jnp.exp(g)
    qd = q * (DK ** -0.5) * eg
    ki = k * jnp.exp(-g)
    ke = k * jnp.exp(gl - g)
    return logit, g, gl, eg, qd, ki, ke


def _gla_rows(T, forward):
    return _pick(T, (512, 256, 128, 64) if forward else (256, 128, 64))


def _gla_fwd(proj, proj_lr, w2p, gate_b):
    T = proj.shape[0]
    rb = _gla_rows(T, True)
    per = rb // CHUNK

    def body(q_ref, k_ref, v_ref, lr_ref, w2_ref, b_ref, o_ref, st_ref, s_scr):
        @pl.when(pl.program_id(1) == 0)
        def _():
            s_scr[...] = jnp.zeros_like(s_scr)

        low, _ = _chunk_masks(rb)
        _, _, gl, _, qd, ki, ke = _gla_block(q_ref[...], k_ref[...], lr_ref[...], w2_ref[...], b_ref[...],
                                             low.astype(BF16))
        v = v_ref[...].astype(BF16)
        qdb, keb = qd.astype(BF16), ke.astype(BF16)
        att = jnp.where(low, _dot(qdb, ki.astype(BF16), 1, 1), 0.0).astype(BF16)
        o_intra = _dot(att, v, 1, 0)
        st = s_scr[...]
        for i in range(per):
            rows = slice(i * CHUNK, (i + 1) * CHUNK)
            st_ref[0, i] = st
            o_ref[rows, :] = o_intra[rows] + _dot(qdb[rows], st.astype(BF16), 1, 1)
            st = st * jnp.exp(gl[i * CHUNK:i * CHUNK + 1]) + _dot(v[rows], keb[rows], 0, 0)
        s_scr[...] = st

    return pl.pallas_call(
        body, name="gla_fwd", grid=(GLA_H, T // rb),
        in_specs=[pl.BlockSpec((rb, DK), lambda h, n: (n, (O_GLA + W_GLA * h) // DK)),
                  pl.BlockSpec((rb, DK), lambda h, n: (n, (O_GLA + W_GLA * h) // DK + 1)),
                  pl.BlockSpec((rb, DV), lambda h, n: (n, (O_GLA + W_GLA * h) // DV + 1)),
                  pl.BlockSpec((rb, LRP), lambda h, n: (n, 0)),
                  pl.BlockSpec((LRP, DK), lambda h, n: (0, h)),
                  pl.BlockSpec((1, DK), lambda h, n: (0, h))],
        out_specs=[pl.BlockSpec((rb, DV), lambda h, n: (n, h)),
                   pl.BlockSpec((1, per, DV, DK), lambda h, n: (h, n, 0, 0))],
        out_shape=[jax.ShapeDtypeStruct((T, GLA_H * DV), F32),
                   jax.ShapeDtypeStruct((GLA_H, T // CHUNK, DV, DK), F32)],
        scratch_shapes=[pltpu.VMEM((DV, DK), F32)], compiler_params=_params(2),
    )(proj, proj, proj, proj_lr, w2p, gate_b)


def _gla_bwd(proj, proj_lr, w2p, gate_b, states, d_o, dproj):
    T = proj.shape[0]
    rb = _gla_rows(T, False)
    per = rb // CHUNK
    nblk = T // rb

    def body(q_ref, k_ref, v_ref, lr_ref, w2_ref, b_ref, st_ref, do_ref, _, dp_ref, dl_ref, ds_scr):
        @pl.when(pl.program_id(1) == 0)
        def _():
            ds_scr[...] = jnp.zeros_like(ds_scr)

        low, up = _chunk_masks(rb)
        logit, g, gl, eg, qd, ki, ke = _gla_block(q_ref[...], k_ref[...], lr_ref[...], w2_ref[...], b_ref[...],
                                                  low.astype(BF16))
        v = v_ref[...].astype(BF16)
        do = do_ref[...].astype(BF16)
        qdb, kib, keb = qd.astype(BF16), ki.astype(BF16), ke.astype(BF16)
        att = jnp.where(low, _dot(qdb, kib, 1, 1), 0.0).astype(BF16)
        datt = jnp.where(low, _dot(do, v, 1, 1), 0.0).astype(BF16)
        dv_intra = _dot(att, do, 0, 0)
        dqd_intra = _dot(datt, kib, 1, 0)
        dki = _dot(datt, qdb, 0, 0)
        ds = ds_scr[...]
        dqd, dke, dgl = [None] * per, [None] * per, [None] * per
        for i in reversed(range(per)):
            rows = slice(i * CHUNK, (i + 1) * CHUNK)
            sp = st_ref[0, i]
            dsb = ds.astype(BF16)
            dp_ref[rows, 2 * DK:] = (dv_intra[rows] + _dot(keb[rows], dsb, 1, 1)).astype(BF16)
            dqd[i] = dqd_intra[rows] + _dot(do[rows], sp.astype(BF16), 1, 0)
            dke[i] = _dot(v[rows], dsb, 1, 0)
            decay = jnp.exp(gl[i * CHUNK:i * CHUNK + 1])
            dgl[i] = (jnp.sum(dke[i] * ke[rows], axis=0, keepdims=True)
                      + jnp.sum(ds * sp, axis=0, keepdims=True) * decay)
            ds = ds * decay + _dot(do[rows], qdb[rows], 0, 0)
        ds_scr[...] = ds
        dqd = jnp.concatenate(dqd, axis=0)
        dke = jnp.concatenate(dke, axis=0)
        dgl = jnp.concatenate([jnp.broadcast_to(d, (CHUNK, DK)) for d in dgl], axis=0)
        dp_ref[:, 0:DK] = (dqd * (DK ** -0.5) * eg).astype(BF16)
        dp_ref[:, DK:2 * DK] = (dki * jnp.exp(-g) + dke * jnp.exp(gl - g)).astype(BF16)
        dg = dqd * qd - dki * ki - dke * ke
        dla = _tri_sum(up.astype(BF16), dg) + dgl
        dl_ref[...] = dla * (1.0 / GATE_NORM) * (1.0 - _sigmoid(logit))

    def rev(n):
        return nblk - 1 - n

    return pl.pallas_call(
        body, name="gla_bwd", grid=(GLA_H, nblk),
        in_specs=[pl.BlockSpec((rb, DK), lambda h, n: (rev(n), (O_GLA + W_GLA * h) // DK)),
                  pl.BlockSpec((rb, DK), lambda h, n: (rev(n), (O_GLA + W_GLA * h) // DK + 1)),
                  pl.BlockSpec((rb, DV), lambda h, n: (rev(n), (O_GLA + W_GLA * h) // DV + 1)),
                  pl.BlockSpec((rb, LRP), lambda h, n: (rev(n), 0)),
                  pl.BlockSpec((LRP, DK), lambda h, n: (0, h)),
                  pl.BlockSpec((1, DK), lambda h, n: (0, h)),
                  pl.BlockSpec((1, per, DV, DK), lambda h, n: (h, rev(n), 0, 0)),
                  pl.BlockSpec((rb, DV), lambda h, n: (rev(n), h)),
                  ANY],
        out_specs=[pl.BlockSpec((rb, W_GLA), lambda h, n: (rev(n), O_GLA // W_GLA + h)),
                   pl.BlockSpec((rb, DK), lambda h, n: (rev(n), h))],
        out_shape=[jax.ShapeDtypeStruct((T, MAIN), BF16), jax.ShapeDtypeStruct((T, GLA_H * DK), F32)],
        scratch_shapes=[pltpu.VMEM((DV, DK), F32)], input_output_aliases={8: 0}, compiler_params=_params(2),
    )(proj, proj, proj, proj_lr, w2p, gate_b, states, d_o, dproj)


def _local_step(x, target, p, reduce_part):
    u, u_t = _rms_fwd(x, p["norm1_w"], name="rms1_fwd")
    whole = _pick(x.shape[0], (2048, 256))
    proj = _matmul(u, p["w_in_main"], bm=whole, name="mm_proj")
    proj_lr = _matmul(u, p["w_in_lr"], name="mm_proj_lr")
    attn_o = _attn_fwd(proj, p["sinks"])
    gla_raw, states = _gla_fwd(proj, proj_lr, p["w2p"], p["gate_b"])
    merged, merged_t = _merge_fwd(attn_o, gla_raw, proj, p["gla_norm_w"])
    h1, v2, v2_t = _out_proj(merged, p["w_out"], x, p["norm2_w"])
    gate, up, ff, ff_t = _ffn_fwd(v2, p["w_gate"], p["w_up"])
    h2 = _matmul(ff, p["w_down"], add=h1, name="mm_down")
    loss, dh2, dh2_b, g_final = _final_loss(h2, p["final_norm_w"], target)

    dgate, dup = _ffn_bwd_hidden(dh2_b, p["w_down"], gate, up)
    reduce_part("w_down", _matmul(ff_t, dh2_b, out_dtype=BF16, bj=D, name="mm_gdown").reshape(4, FFN // 4, D))
    reduce_part("w_gate", _matmul(v2_t, dgate, out_dtype=BF16, bj=PANEL, out_slots=True, name="mm_ggate"))
    reduce_part("w_up", _matmul(v2_t, dup, out_dtype=BF16, bj=PANEL, out_slots=True, name="mm_gup"))
    dv2 = _ffn_bwd_input(dgate, dup, p["w_gate"], p["w_up"])
    dh1, dh1_b, g_norm2 = _rms_bwd(dv2, h1, p["norm2_w"], dh2, name="rms2_bwd")
    dmerged = _matmul(dh1_b, p["w_out"], tb=True, bj=D, name="mm_dmerged")
    reduce_part("w_out", _matmul(merged_t, dh1_b, out_dtype=BF16, bj=D, name="mm_gout").reshape(4, D // 4, D))
    d_attn, d_gla, dproj, g_gla_norm = _merge_bwd(dmerged, attn_o, gla_raw, proj, p["gla_norm_w"])
    dproj, dlogit = _gla_bwd(proj, proj_lr, p["w2p"], p["gate_b"], states, d_gla, dproj)
    dproj, g_sinks = _attn_bwd(proj, p["sinks"], d_attn, dproj)
    g_gate_b = _colsum(dlogit, name="colsum_gate_b")
    g_w2 = _matmul(proj_lr, dlogit, ta=True, name="mm_gw2")
    dproj_lr = _matmul(dlogit, p["w2p"], tb=True, out_dtype=BF16, name="mm_dlr")
    g_in_main = _matmul(u_t, dproj, out_dtype=BF16, bm=D, name="mm_gin")
    g_in_lr = _matmul(u_t, dproj_lr, out_dtype=BF16, name="mm_gin_lr")
    du = _matmul(dproj, p["w_in_main"], tb=True, bm=_pick(x.shape[0], (1024, 256)), bj=1024, bc=3200, name="mm_du")
    dx, g_norm1 = _rms_bwd_input(du, dproj_lr, p["w_in_lr"], x, p["norm1_w"], dh1)
    grads = dict(norm1_w=g_norm1, w_in_main=g_in_main, w_in_lr=g_in_lr, w2=g_w2[:RANK], gate_b=g_gate_b,
                 sinks=g_sinks[:, 0].reshape(1, N_Q), gla_norm_w=g_gla_norm, norm2_w=g_norm2, final_norm_w=g_final)
    return loss, dx, grads


def _place():
    x, y, c = lax.axis_index("x"), lax.axis_index("y"), lax.axis_index("c")
    chips = [(1 - x, y), (x, 1 - y), (1 - x, 1 - y)]
    return x, y, c, chips


def _hbm_shape(s, dt):
    return jax.ShapeDtypeStruct(s, dt)


ID_SIBLING = 9
ID_GATHER_IN = 10


def _handshake(peers):
    barrier = pltpu.get_barrier_semaphore()
    for peer in peers:
        pl.semaphore_signal(barrier, inc=1, device_id=peer, device_id_type=MESH)
    pl.semaphore_wait(barrier, len(peers))


def _launch_copies(body, args, out_shape, sems, *, name, collective_id=None):
    if collective_id is None:
        return pl.pallas_call(
            body, name=name, in_specs=[ANY] * len(args), out_specs=[ANY] * len(out_shape), out_shape=out_shape,
            scratch_shapes=sems)(*args)
    return pl.kernel(
        body, name=name, out_type=out_shape, mesh=plsc.ScalarSubcoreMesh(axis_name="sequencer", num_cores=1),
        scratch_types=sems, compiler_params=pltpu.CompilerParams(collective_id=collective_id))(*args)


def _gather_shards(shards, *, name, collective_id=None, after=()):
    n = len(shards)
    first_out = n + len(after)

    def body(*refs):
        ins, outs = refs[:n], refs[first_out:first_out + n]
        ici_send, ici_recv, d2d_send, d2d_recv, local_sem = refs[first_out + n:]
        x, y, c, chips = _place()
        me = 2 * x + y
        sibling = (x, y, 1 - c)
        if collective_id is not None:
            _handshake([sibling] + [(*chip, c) for chip in chips])

        def half(w, slot, hc):
            r2 = shards[w].shape[0] // 2
            return outs[w].at[slot, pl.ds(hc * r2, r2), :]

        locals_ = [pltpu.make_async_copy(ins[w], outs[w].at[me], local_sem.at[w]) for w in range(n)]
        for cp in locals_:
            cp.start()
        sends = []
        for w in range(n):
            r2 = shards[w].shape[0] // 2
            for j, chip in enumerate(chips):
                cp = pltpu.make_async_remote_copy(
                    src_ref=ins[w].at[pl.ds(c * r2, r2), :], dst_ref=half(w, me, c),
                    send_sem=ici_send.at[w * 3 + j], recv_sem=ici_recv.at[w * 3 + j],
                    device_id=(*chip, c), device_id_type=MESH)
                cp.start()
                sends.append(cp)
        for w in range(n):
            for j, chip in enumerate(chips):
                slot = 2 * chip[0] + chip[1]
                got = half(w, slot, c)
                pltpu.make_async_remote_copy(
                    src_ref=got, dst_ref=got, send_sem=ici_send.at[w * 3 + j], recv_sem=ici_recv.at[w * 3 + j],
                    device_id=(*chip, c), device_id_type=MESH).wait_recv()
                cp = pltpu.make_async_remote_copy(
                    src_ref=got, dst_ref=got, send_sem=d2d_send.at[w * 3 + j], recv_sem=d2d_recv.at[w * 3 + j],
                    device_id=sibling, device_id_type=MESH)
                cp.start()
                sends.append(cp)
        for w in range(n):
            for j, chip in enumerate(chips):
                slot = 2 * chip[0] + chip[1]
                got = half(w, slot, 1 - c)
                pltpu.make_async_remote_copy(
                    src_ref=got, dst_ref=got, send_sem=d2d_send.at[w * 3 + j], recv_sem=d2d_recv.at[w * 3 + j],
                    device_id=sibling, device_id_type=MESH).wait_recv()
        for cp in sends:
            cp.wait_send()
        for cp in locals_:
            cp.wait()

    return _launch_copies(
        body, list(shards) + list(after), [_hbm_shape((4,) + s.shape, s.dtype) for s in shards],
        [pltpu.SemaphoreType.DMA((3 * n,)), pltpu.SemaphoreType.DMA((3 * n,)), pltpu.SemaphoreType.DMA((3 * n,)),
         pltpu.SemaphoreType.DMA((3 * n,)), pltpu.SemaphoreType.DMA((n,))],
        name=name, collective_id=collective_id)


def _gather_staged(w, small):
    R, C = w.shape
    R2 = R // 2
    rb = 256
    per = R2 // rb
    r2 = small.shape[0] // 2

    def body(w_ref, s_ref, out_ref, outs_ref, stage, ici_send, ici_recv, fwd_send, fwd_recv, in_sem, put_sem,
             push_send, push_recv):
        x, y, c, chips = _place()
        me = 2 * x + y
        sibling = (x, y, 1 - c)
        slots = [2 * chip[0] + chip[1] for chip in chips]
        _handshake([sibling] + [(*chip, c) for chip in chips])

        def ici(j, src, dst, k):
            return pltpu.make_async_remote_copy(
                src_ref=src, dst_ref=dst, send_sem=ici_send.at[k], recv_sem=ici_recv.at[k],
                device_id=(*chips[j], c), device_id_type=MESH)

        sends = []
        for j in range(3):
            sends.append(ici(j, w_ref.at[pl.ds(c * R2, R2), :], out_ref.at[me, pl.ds(c * R2, R2), :], j))
            sends.append(ici(j, s_ref.at[pl.ds(c * r2, r2), :], outs_ref.at[me, pl.ds(c * r2, r2), :], 3 + j))
        for cp in sends:
            cp.start()
        mine = pltpu.make_async_copy(s_ref, outs_ref.at[me], put_sem.at[2])
        mine.start()

        def own(k, to_vmem):
            rows = pl.ds(k * rb, rb)
            if to_vmem:
                return pltpu.make_async_copy(w_ref.at[rows, :], stage.at[k % 2], in_sem.at[k % 2])
            return pltpu.make_async_copy(stage.at[k % 2], out_ref.at[me, rows, :], put_sem.at[k % 2])

        for k in range(R // rb):
            if k >= 2:
                own(k - 2, False).wait()
            own(k, True).start()
            own(k, True).wait()
            own(k, False).start()
        for k in range(R // rb - 2, R // rb):
            own(k, False).wait()

        def block(j, k, hc):
            return out_ref.at[slots[j], pl.ds(hc * R2 + k * rb, rb), :]

        def push(t, hc):
            j, k = divmod(t, per)
            return pltpu.make_async_remote_copy(
                src_ref=stage.at[t % 2], dst_ref=block(j, k, hc), send_sem=push_send.at[t % 2],
                recv_sem=push_recv.at[t], device_id=sibling, device_id_type=MESH)

        passed = []
        for j in range(3):
            ici(j, w_ref.at[pl.ds(c * R2, R2), :], out_ref.at[slots[j], pl.ds(c * R2, R2), :], j).wait_recv()
            for k in range(per):
                t = j * per + k
                if t >= 2:
                    push(t - 2, c).wait_send()
                fetch = pltpu.make_async_copy(block(j, k, c), stage.at[t % 2], in_sem.at[t % 2])
                fetch.start()
                fetch.wait()
                push(t, c).start()
            got = outs_ref.at[slots[j], pl.ds(c * r2, r2), :]
            ici(j, got, got, 3 + j).wait_recv()
            fw = pltpu.make_async_remote_copy(
                src_ref=got, dst_ref=got, send_sem=fwd_send.at[j], recv_sem=fwd_recv.at[j],
                device_id=sibling, device_id_type=MESH)
            fw.start()
            passed.append(fw)
        for t in range(3 * per - 2, 3 * per):
            push(t, c).wait_send()
        for t in range(3 * per):
            push(t, 1 - c).wait_recv()
        for j in range(3):
            theirs = outs_ref.at[slots[j], pl.ds((1 - c) * r2, r2), :]
            pltpu.make_async_remote_copy(
                src_ref=theirs, dst_ref=theirs, send_sem=fwd_send.at[j], recv_sem=fwd_recv.at[j],
                device_id=sibling, device_id_type=MESH).wait_recv()
        for cp in sends + passed:
            cp.wait_send()
        mine.wait()

    dma = pltpu.SemaphoreType.DMA
    return pl.pallas_call(
        body, name="gather_in", in_specs=[ANY, ANY], out_specs=[ANY, ANY],
        out_shape=[_hbm_shape((4, R, C), w.dtype), _hbm_shape((4,) + small.shape, small.dtype)],
        scratch_shapes=[pltpu.VMEM((2, rb, C), w.dtype), dma((6,)), dma((6,)), dma((3,)), dma((3,)), dma((2,)),
                        dma((3,)), dma((2,)), dma((3 * per,))],
        compiler_params=pltpu.CompilerParams(vmem_limit_bytes=VMEM_LIMIT, collective_id=ID_GATHER_IN),
    )(w, small)


def _pair_blocks(R2):
    return _pick(R2, (256, 352, 128, 64, 32, 16))


def _pair_reduce(part, *, name):
    _, R, C = part.shape
    R2 = R // 2
    rb = _pick(R2, (512, 352, 256, 128, 64, 32, 16))
    nblk = R2 // rb
    steps = [(s, i) for s in range(4) for i in range(nblk)]
    n = len(steps)

    def body(part_ref, sums_ref, own_buf, oth_buf, rcv_buf, out_buf, own_sem, oth_sem, out_sem, send_sem, recv_sem):
        x, y, c, _ = _place()
        sibling = (x, y, 1 - c)
        _handshake([sibling])

        def fetch(t, half, buf, sem):
            s, i = steps[t]
            return pltpu.make_async_copy(part_ref.at[s, pl.ds(half * R2 + i * rb, rb), :], buf.at[t % 2], sem.at[t % 2])

        def push(t):
            return pltpu.make_async_remote_copy(
                src_ref=oth_buf.at[t % 2], dst_ref=rcv_buf.at[t % 3], send_sem=send_sem.at[t % 2],
                recv_sem=recv_sem.at[t % 2], device_id=sibling, device_id_type=MESH)

        def store(t):
            s, i = steps[t]
            return pltpu.make_async_copy(out_buf.at[t % 2], sums_ref.at[s, pl.ds(i * rb, rb), :], out_sem.at[t % 2])

        fetch(0, c, own_buf, own_sem).start()
        fetch(0, 1 - c, oth_buf, oth_sem).start()
        fetch(0, 1 - c, oth_buf, oth_sem).wait()
        push(0).start()
        for t in range(n):
            if t + 1 < n:
                fetch(t + 1, c, own_buf, own_sem).start()
                fetch(t + 1, 1 - c, oth_buf, oth_sem).start()
            fetch(t, c, own_buf, own_sem).wait()
            push(t).wait()
            if t + 1 < n:
                fetch(t + 1, 1 - c, oth_buf, oth_sem).wait()
                push(t + 1).start()
            if t >= 2:
                store(t - 2).wait()
            out_buf[t % 2] = (own_buf[t % 2].astype(F32) + rcv_buf[t % 3].astype(F32)).astype(BF16)
            store(t).start()
        for t in range(max(n - 2, 0), n):
            store(t).wait()

    buf = pltpu.VMEM((2, rb, C), BF16)
    sem2 = pltpu.SemaphoreType.DMA((2,))
    return pl.pallas_call(
        body, name=name, in_specs=[ANY], out_specs=ANY, out_shape=_hbm_shape((4, R2, C), BF16),
        scratch_shapes=[buf, buf, pltpu.VMEM((3, rb, C), BF16), buf, sem2, sem2, sem2, sem2, sem2],
        compiler_params=pltpu.CompilerParams(vmem_limit_bytes=VMEM_LIMIT, collective_id=ID_SIBLING),
    )(part)


def _chip_exchange(sums, *, name, collective_id=None):
    n = len(sums)

    def body(*refs):
        ins, outs = refs[:n], refs[n:2 * n]
        send_sem, recv_sem, local_sem = refs[2 * n:]
        x, y, c, chips = _place()
        me = 2 * x + y
        if collective_id is not None:
            _handshake([(*chip, c) for chip in chips])
        cps = []
        for w in range(n):
            lc = pltpu.make_async_copy(ins[w].at[me], outs[w].at[me], local_sem.at[w])
            lc.start()
            cps.append(lc)
            for j, chip in enumerate(chips):
                slot = 2 * chip[0] + chip[1]
                rc = pltpu.make_async_remote_copy(
                    src_ref=ins[w].at[slot], dst_ref=outs[w].at[me],
                    send_sem=send_sem.at[w * 3 + j], recv_sem=recv_sem.at[w * 3 + j],
                    device_id=(*chip, c), device_id_type=MESH)
                rc.start()
                cps.append(rc)
        for w in range(n):
            for j, chip in enumerate(chips):
                slot = 2 * chip[0] + chip[1]
                pltpu.make_async_remote_copy(
                    src_ref=ins[w].at[slot], dst_ref=outs[w].at[slot],
                    send_sem=send_sem.at[w * 3 + j], recv_sem=recv_sem.at[w * 3 + j],
                    device_id=(*chip, c), device_id_type=MESH).wait_recv()
        for w in range(n):
            cps[w * 4].wait()
            for j in range(3):
                cps[w * 4 + 1 + j].wait_send()

    return _launch_copies(
        body, sums, [_hbm_shape(s.shape, s.dtype) for s in sums],
        [pltpu.SemaphoreType.DMA((3 * n,)), pltpu.SemaphoreType.DMA((3 * n,)), pltpu.SemaphoreType.DMA((n,))],
        name=name, collective_id=collective_id)


def _sum_join(recv, *, name):
    _, R2, C = recv.shape
    rb = _pair_blocks(R2)
    nblk = R2 // rb

    def body(recv_ref, out_ref, in_buf, acc_buf, in_sem, loc_sem, send_sem, recv_sem):
        x, y, c, _ = _place()
        sibling = (x, y, 1 - c)
        _handshake([sibling])

        def fetch(t):
            return pltpu.make_async_copy(recv_ref.at[:, pl.ds(t * rb, rb), :], in_buf.at[t % 2], in_sem.at[t % 2])

        def rows(t, half):
            return out_ref.at[pl.ds(half * R2 + t * rb, rb), :]

        def put_local(t):
            return pltpu.make_async_copy(acc_buf.at[t % 2], rows(t, c), loc_sem.at[t])

        def put_remote(t, half):
            return pltpu.make_async_remote_copy(
                src_ref=acc_buf.at[t % 2], dst_ref=rows(t, half), send_sem=send_sem.at[t], recv_sem=recv_sem.at[t],
                device_id=sibling, device_id_type=MESH)

        fetch(0).start()
        for t in range(nblk):
            if t + 1 < nblk:
                fetch(t + 1).start()
            fetch(t).wait()
            if t >= 2:
                put_local(t - 2).wait()
                put_remote(t - 2, c).wait_send()
            acc = in_buf[t % 2, 0].astype(F32)
            for s in range(1, 4):
                acc = acc + in_buf[t % 2, s].astype(F32)
            acc_buf[t % 2] = acc
            put_local(t).start()
            put_remote(t, c).start()
        for t in range(max(nblk - 2, 0), nblk):
            put_local(t).wait()
            put_remote(t, c).wait_send()
        for t in range(nblk):
            put_remote(t, 1 - c).wait_recv()

    semn = pltpu.SemaphoreType.DMA((nblk,))
    return pl.pallas_call(
        body, name=name, in_specs=[ANY], out_specs=ANY, out_shape=_hbm_shape((2 * R2, C), F32),
        scratch_shapes=[pltpu.VMEM((2, 4, rb, C), BF16), pltpu.VMEM((2, rb, C), F32),
                        pltpu.SemaphoreType.DMA((2,)), semn, semn, semn],
        compiler_params=pltpu.CompilerParams(vmem_limit_bytes=VMEM_LIMIT, collective_id=ID_SIBLING),
    )(recv)


def _sum_small(pack):
    R, C = pack.shape

    def body(in_ref, out_ref, all_ref, send_sem, recv_sem):
        x, y, c, _ = _place()
        me = 4 * x + 2 * y + c
        all_ref[me] = in_ref[...]
        cps = []
        for k in range(1, 8):
            peer = (x ^ (k >> 2), y ^ ((k >> 1) & 1), c ^ (k & 1))
            cp = pltpu.make_async_remote_copy(
                src_ref=in_ref, dst_ref=all_ref.at[me], send_sem=send_sem.at[k - 1], recv_sem=recv_sem.at[k - 1],
                device_id=peer, device_id_type=MESH)
            cp.start()
            cps.append(cp)
        for k in range(1, 8):
            peer = (x ^ (k >> 2), y ^ ((k >> 1) & 1), c ^ (k & 1))
            slot = 4 * peer[0] + 2 * peer[1] + peer[2]
            pltpu.make_async_remote_copy(
                src_ref=in_ref, dst_ref=all_ref.at[slot], send_sem=send_sem.at[k - 1], recv_sem=recv_sem.at[k - 1],
                device_id=peer, device_id_type=MESH).wait_recv()
        for cp in cps:
            cp.wait_send()
        acc = all_ref[0]
        for d in range(1, 8):
            acc = acc + all_ref[d]
        out_ref[...] = acc

    return pl.pallas_call(
        body, name="sum_small", in_specs=[pl.BlockSpec(memory_space=pltpu.VMEM)],
        out_specs=pl.BlockSpec(memory_space=pltpu.VMEM), out_shape=jax.ShapeDtypeStruct((R, C), F32),
        scratch_shapes=[pltpu.VMEM((8, R, C), F32), pltpu.SemaphoreType.DMA((7,)), pltpu.SemaphoreType.DMA((7,))],
    )(pack)


ADAMW_BLOCK_BYTES = 2 * 1024 * 1024


def _adamw_block(R, C):
    padded = -(-C // 128) * 128
    rows = [rb for rb in range(8, R + 1, 8) if R % rb == 0 and rb * padded * 4 <= ADAMW_BLOCK_BYTES]
    if rows or R * padded * 4 <= ADAMW_BLOCK_BYTES:
        return (max(rows) if rows else R), C
    cols = [cb for cb in range(128, C + 1, 128) if C % cb == 0 and R * cb * 4 <= ADAMW_BLOCK_BYTES]
    return R, max(cols)


def _adamw(w, g, m, v, *, name, pass_grad=False):
    R, C = w.shape
    rb, cb = _adamw_block(R, C)
    c1 = 1.0 / (1.0 - B1 ** STEP)
    c2 = 1.0 / (1.0 - B2 ** STEP)

    def body(w_ref, g_ref, m_ref, v_ref, *outs):
        d_ref, nm_ref, nv_ref = outs[-3:]
        gv = g_ref[...]
        if pass_grad:
            outs[0][...] = gv
        m2 = B1 * m_ref[...] + (1.0 - B1) * gv
        v2 = B2 * v_ref[...] + (1.0 - B2) * gv * gv
        nm_ref[...] = m2
        nv_ref[...] = v2
        d_ref[...] = -LR * ((m2 * c1) / (jnp.sqrt(v2 * c2) + ADAM_EPS) + WD * w_ref[...])

    n_out = 4 if pass_grad else 3
    blk = pl.BlockSpec((rb, cb), lambda i, j: (i, j))
    return pl.pallas_call(
        body, name=name, grid=(R // rb, C // cb), in_specs=[blk] * 4, out_specs=[blk] * n_out,
        out_shape=[jax.ShapeDtypeStruct((R, C), F32)] * n_out, compiler_params=_params(2),
    )(w, g, m, v)


SMALL = (("norm1_w", D), ("norm2_w", D), ("final_norm_w", D), ("gate_b", GLA_H * DK), ("gla_norm_w", DV), ("sinks", N_Q))
PACK_W = 1024


def _pack_small(vals, w2, loss):
    rows = []
    for name, width in SMALL:
        v = vals[name].reshape(-1)
        rows.append(jnp.pad(v, (0, (-width) % PACK_W)).reshape(-1, PACK_W))
    rows.append(w2)
    rows.append(jnp.broadcast_to(loss.reshape(1, 1), (1, PACK_W)))
    pack = jnp.concatenate(rows, axis=0)
    return jnp.pad(pack, ((0, 32 - pack.shape[0]), (0, 0)))


def _unpack_small(pack):
    out, r = {}, 0
    for name, width in SMALL:
        nr = -(-width // PACK_W)
        out[name] = pack[r:r + nr].reshape(-1)[:width]
        r += nr
    out["w2"] = pack[r:r + RANK]
    out["loss"] = pack[r + RANK, 0]
    return out


def kernel(x, norm1_w, w_in, gla_gate_w2, gla_gate_b, attn_sinks, gla_norm_w, w_out, norm2_w, w_ffn_gate, w_ffn_up, w_ffn_down, final_norm_w, loss_target, m_norm1_w, m_w_in, m_gla_gate_w2, m_gla_gate_b, m_attn_sinks, m_gla_norm_w, m_w_out, m_norm2_w, m_w_ffn_gate, m_w_ffn_up, m_w_ffn_down, m_final_norm_w, v_norm1_w, v_w_in, v_gla_gate_w2, v_gla_gate_b, v_attn_sinks, v_gla_norm_w, v_w_out, v_norm2_w, v_w_ffn_gate, v_w_ffn_up, v_w_ffn_down, v_final_norm_w):
    chip = 2 * lax.axis_index("x") + lax.axis_index("y")
    w_in_s, w_out_s, w_gate_s, w_up_s, w_down_s, w2_s = (
        w_in[0], w_out[0], w_ffn_gate[0], w_ffn_up[0], w_ffn_down[0], gla_gate_w2[0])
    CS = SHARD_COLS

    g_in, g_w2 = _gather_staged(w_in_s.astype(BF16), w2_s)
    g_out, = _gather_shards([w_out_s.astype(BF16)], name="gather_out", collective_id=1, after=[g_w2])
    g_gate, g_up = _gather_shards([w_gate_s.astype(BF16), w_up_s.astype(BF16)], name="gather_gate_up",
                                  collective_id=7, after=[g_w2])
    g_down, = _gather_shards([w_down_s.astype(BF16)], name="gather_down", collective_id=8, after=[g_w2])
    w_main, w_lr = _w_in_to_main(g_in)
    w2_full = jnp.transpose(g_w2, (1, 0, 2)).reshape(RANK, GLA_H * DK)
    p = dict(
        norm1_w=norm1_w, norm2_w=norm2_w, final_norm_w=final_norm_w.reshape(1, D), gate_b=gla_gate_b,
        gla_norm_w=gla_norm_w, sinks=jnp.pad(attn_sinks, ((0, 0), (0, 128 - N_Q))),
        w_in_main=w_main, w_in_lr=w_lr,
        w2p=jnp.pad(w2_full, ((0, LRP - RANK), (0, 0))).astype(BF16),
        w_out=g_out.reshape(D, D),
        w_gate=g_gate, w_up=g_up, w_down=g_down.reshape(FFN, D),
    )

    tags = ["w_in", "w_out", "w_gate", "w_up", "w_down"]
    recv = {}

    def reduce_part(tag, part):
        sums = _pair_reduce(part, name="pair_reduce_" + tag)
        recv[tag], = _chip_exchange([sums], name="chip_exchange_" + tag, collective_id=2 + tags.index(tag))

    done = {}
    shard = {"w_out": (w_out_s, m_w_out[0], v_w_out[0]), "w_gate": (w_gate_s, m_w_ffn_gate[0], v_w_ffn_gate[0]),
             "w_up": (w_up_s, m_w_ffn_up[0], v_w_ffn_up[0]), "w_down": (w_down_s, m_w_ffn_down[0], v_w_ffn_down[0])}

    def finish(tag):
        grad = _sum_join(recv[tag], name="sum_join_" + tag)
        done[tag] = _adamw(shard[tag][0], grad, shard[tag][1], shard[tag][2], name="adamw_" + tag, pass_grad=True)

    loss_blk, dx, g = _local_step(x[0], loss_target[0], p, reduce_part)
    reduce_part("w_in", _main_to_shards(g["w_in_main"], g["w_in_lr"]))
    for tag in ("w_down", "w_gate", "w_up", "w_out"):
        finish(tag)
    g_in = _sum_join(recv["w_in"], name="sum_join_w_in")

    small = _unpack_small(_sum_small(_pack_small(g, g["w2"], loss_blk[0, 0])))
    loss = small["loss"]
    g_w2_mine = lax.dynamic_slice_in_dim(small["w2"], chip * (GLA_H * DK // 4), GLA_H * DK // 4, axis=1)

    grads = dict(
        norm1_w=small["norm1_w"].reshape(1, D), w_in=g_in, gla_gate_w2=g_w2_mine,
        gla_gate_b=small["gate_b"].reshape(1, -1), attn_sinks=small["sinks"].reshape(1, N_Q),
        gla_norm_w=small["gla_norm_w"].reshape(1, DV), norm2_w=small["norm2_w"].reshape(1, D),
        final_norm_w=small["final_norm_w"].reshape(1, D))
    def lin(a):
        return jnp.transpose(a.reshape(D, CS))

    def unlin(a):
        return jnp.transpose(a)

    grads["w_in"] = lin(g_in)
    weights = dict(
        norm1_w=(norm1_w, m_norm1_w, v_norm1_w), w_in=(lin(w_in), lin(m_w_in), lin(v_w_in)),
        gla_gate_w2=(w2_s, m_gla_gate_w2[0], v_gla_gate_w2[0]), gla_gate_b=(gla_gate_b, m_gla_gate_b, v_gla_gate_b),
        attn_sinks=(attn_sinks, m_attn_sinks, v_attn_sinks), gla_norm_w=(gla_norm_w, m_gla_norm_w, v_gla_norm_w),
        norm2_w=(norm2_w, m_norm2_w, v_norm2_w),
        final_norm_w=(final_norm_w.reshape(1, D), m_final_norm_w.reshape(1, D), v_final_norm_w.reshape(1, D)))
    names = ["norm1_w", "w_in", "gla_gate_w2", "gla_gate_b", "attn_sinks", "gla_norm_w", "w_out", "norm2_w",
             "w_ffn_gate", "w_ffn_up", "w_ffn_down", "final_norm_w"]
    lead = {"norm1_w": False, "gla_gate_b": False, "attn_sinks": False, "gla_norm_w": False, "norm2_w": False}
    g_out_l, d_out, m_out, v_out = [], [], [], []
    early = {"w_out": "w_out", "w_ffn_gate": "w_gate", "w_ffn_up": "w_up", "w_ffn_down": "w_down"}
    for nm in names:
        if nm in early:
            gr, dl, nmn, nvn = done[early[nm]]
            w = shard[early[nm]][0]
        else:
            w, m, v = weights[nm]
            gr = grads[nm]
            dl, nmn, nvn = _adamw(w, gr, m, v, name="adamw_" + nm)
        if nm == "w_in":
            gr, dl, nmn, nvn, w = g_in, unlin(dl), unlin(nmn), unlin(nvn), w_in_s
        if nm == "final_norm_w":
            shape = (D,)
        elif nm in lead:
            shape = w.shape
        else:
            shape = (1,) + w.shape
        g_out_l.append(gr.reshape(shape))
        d_out.append(dl.reshape(shape))
        m_out.append(nmn.reshape(shape))
        v_out.append(nvn.reshape(shape))
    return (loss, dx[None], *g_out_l, *d_out, *m_out, *v_out)
```

```python
import functools

import jax
import jax.numpy as jnp
from jax import lax
from jax.experimental import pallas as pl
from jax.experimental.pallas import tpu as pltpu
from jax.experimental.pallas import tpu_sc as plsc

F32 = jnp.float32
BF16 = jnp.bfloat16

D = 2048
HEAD_DIM = 64
N_Q = 32
N_KV = 4
GROUP = 8
WINDOW = 128
GLA_H = 4
DK = 256
DV = 512
RANK = 16
CHUNK = 64
FFN = 5632
EPS = 1e-6
MASK_VALUE = -1e30
GATE_NORM = 16.0
LR, B1, B2, ADAM_EPS, WD, STEP = 0.001, 0.9, 0.999, 1e-08, 0.01, 10

MAIN = 12800
O_MERGE, O_GLA, O_ATTN = 0, 6144, 10240
W_MERGE, W_GLA, W_ATTN = 3 * DV, 2 * DK + DV, D + 2 * N_KV * HEAD_DIM
LR_AT = 6656
LRP = 128


def _main_pieces():
    o_gq, o_gk, o_gv, o_gr, o_ga, o_gb = 2560, 3584, 4608, LR_AT + RANK, LR_AT + RANK + D, LR_AT + RANK + 2 * D
    pieces = []
    for h in range(GLA_H):
        pieces += [(o_gr + DV * h, DV), (o_ga + DV * h, DV), (o_gb + DV * h, DV)]
    for h in range(GLA_H):
        pieces += [(o_gq + DK * h, DK), (o_gk + DK * h, DK), (o_gv + DV * h, DV)]
    pieces.append((0, W_ATTN))
    return pieces


SHARD_COLS = 3204

VMEM_LIMIT = 56 * 1024 * 1024
MESH = pl.DeviceIdType.MESH
ANY = pl.BlockSpec(memory_space=pl.ANY)


def _params(ngrid):
    return pltpu.CompilerParams(dimension_semantics=("arbitrary",) * ngrid, vmem_limit_bytes=VMEM_LIMIT)


def _shard_segments(lo, hi):
    segs = []
    while lo < hi:
        s = lo // SHARD_COLS
        e = min(hi, (s + 1) * SHARD_COLS)
        segs.append((s, lo - s * SHARD_COLS, e - s * SHARD_COLS))
        lo = e
    return segs


def _lanes(pieces):
    return pieces[0] if len(pieces) == 1 else jnp.concatenate(pieces, axis=1)


def _w_in_to_main(shards):
    rb = 256

    def body(g_ref, main_ref, lr_ref):
        at = 0
        for a, w in _main_pieces():
            main_ref[:, at:at + w] = _lanes([g_ref[s, :, lo:hi] for s, lo, hi in _shard_segments(a, a + w)])
            at += w
        lr = [g_ref[s, :, lo:hi] for s, lo, hi in _shard_segments(LR_AT, LR_AT + RANK)]
        lr_ref[...] = _lanes(lr + [jnp.zeros((rb, LRP - RANK), lr_ref.dtype)])

    return pl.pallas_call(
        body, name="w_in_to_main", grid=(D // rb,),
        in_specs=[pl.BlockSpec((4, rb, SHARD_COLS), lambda i: (0, i, 0))],
        out_specs=[pl.BlockSpec((rb, MAIN), lambda i: (i, 0)), pl.BlockSpec((rb, LRP), lambda i: (i, 0))],
        out_shape=[jax.ShapeDtypeStruct((D, MAIN), shards.dtype), jax.ShapeDtypeStruct((D, LRP), shards.dtype)],
        compiler_params=_params(1),
    )(shards)


def _main_to_shards(g_main, g_lr):
    rb = 256
    at, sources = 0, [(LR_AT, RANK, 1, 0)]
    for a, w in _main_pieces():
        sources.append((a, w, 0, at))
        at += w
    sources.sort()

    def body(main_ref, lr_ref, out_ref):
        refs = (main_ref, lr_ref)
        for s in range(4):
            lo, hi = s * SHARD_COLS, (s + 1) * SHARD_COLS
            pieces = []
            for a, w, which, src in sources:
                b, e = max(a, lo), min(a + w, hi)
                if b < e:
                    pieces.append(refs[which][:, src + b - a:src + e - a])
            out_ref[s] = _lanes(pieces)

    return pl.pallas_call(
        body, name="main_to_shards", grid=(D // rb,),
        in_specs=[pl.BlockSpec((rb, MAIN), lambda i: (i, 0)), pl.BlockSpec((rb, LRP), lambda i: (i, 0))],
        out_specs=pl.BlockSpec((4, rb, SHARD_COLS), lambda i: (0, i, 0)),
        out_shape=jax.ShapeDtypeStruct((4, D, SHARD_COLS), g_main.dtype), compiler_params=_params(1),
    )(g_main, g_lr)


def _pick(n, cands):
    for c in cands:
        if n % c == 0:
            return c
    return n


def _sigmoid(x):
    return 1.0 / (1.0 + jnp.exp(-x))


def _dot(a, b, ca, cb):
    return lax.dot_general(a, b, (((ca,), (cb,)), ((), ())), preferred_element_type=F32)


def _matmul(a, b, *, name, ta=False, tb=False, add=None, out_dtype=F32, bm=None, bj=None, bc=None,
            b_slots=False, out_slots=False):
    C, M = a.shape if ta else a.shape[::-1]
    if b_slots:
        if tb:
            J, bc = b.shape[1], b.shape[2]
            assert b.shape[0] * bc == C
        else:
            bj = b.shape[2]
            J = b.shape[0] * bj
            assert b.shape[1] == C
    else:
        J = b.shape[0] if tb else b.shape[1]
        assert (b.shape[1] if tb else b.shape[0]) == C
    bm = bm or _pick(M, (1024, 512, 256, 128) if C <= FFN else (512, 256, 128))
    bj = bj or _pick(J, (1280, 1024, 512, 256, 128) if C <= D else (512, 256, 128))
    bc = bc or (C if C <= FFN else _pick(C, (6400,)))
    nm, nj, nc = M // bm, J // bj, C // bc
    has_add = add is not None

    def body(*refs):
        a_ref, b_ref = refs[0], refs[1]
        add_ref = refs[2] if has_add else None
        o_ref = refs[3] if has_add else refs[2]
        p = _dot(a_ref[...].astype(BF16), b_ref[...].astype(BF16), 0 if ta else 1, 1 if tb else 0)

        def finish(acc):
            if has_add:
                acc = acc + add_ref[...]
            o_ref[...] = acc.astype(o_ref.dtype)

        if nc == 1:
            finish(p)
        else:
            acc_ref = refs[-1]
            c = pl.program_id(2)

            @pl.when(c == 0)
            def _():
                acc_ref[...] = p

            @pl.when(c > 0)
            def _():
                acc_ref[...] += p

            @pl.when(c == nc - 1)
            def _():
                finish(acc_ref[...])

    a_spec = pl.BlockSpec((bc, bm), lambda m, j, c: (c, m)) if ta else pl.BlockSpec((bm, bc), lambda m, j, c: (m, c))
    if b_slots:
        b_spec = (pl.BlockSpec((None, bj, bc), lambda m, j, c: (c, j, 0)) if tb
                  else pl.BlockSpec((None, bc, bj), lambda m, j, c: (j, c, 0)))
    else:
        b_spec = (pl.BlockSpec((bj, bc), lambda m, j, c: (j, c)) if tb
                  else pl.BlockSpec((bc, bj), lambda m, j, c: (c, j)))
    if out_slots:
        assert not has_add
        o_spec = pl.BlockSpec((None, bm, bj), lambda m, j, c: (j, m, 0))
        out_shape = jax.ShapeDtypeStruct((nj, M, bj), out_dtype)
    else:
        o_spec = pl.BlockSpec((bm, bj), lambda m, j, c: (m, j))
        out_shape = jax.ShapeDtypeStruct((M, J), out_dtype)
    in_specs = [a_spec, b_spec] + ([o_spec] if has_add else [])
    args = (a, b) + ((add,) if has_add else ())
    return pl.pallas_call(
        body, name=name, grid=(nm, nj, nc), in_specs=in_specs, out_specs=o_spec,
        out_shape=out_shape,
        scratch_shapes=[pltpu.VMEM((bm, bj), F32)] if nc > 1 else [],
        compiler_params=_params(3),
    )(*args)


def _rows(T):
    return _pick(T, (256, 128))


def _rms_fwd(x, w, *, name):
    T = x.shape[0]
    rb = _rows(T)

    def body(x_ref, w_ref, u_ref, ut_ref):
        xv = x_ref[...]
        r = lax.rsqrt(jnp.mean(xv * xv, axis=-1, keepdims=True) + EPS)
        u = xv * r * w_ref[...]
        u_ref[...] = u.astype(BF16)
        ut_ref[...] = u.T.astype(BF16)

    return pl.pallas_call(
        body, name=name, grid=(T // rb,),
        in_specs=[pl.BlockSpec((rb, D), lambda i: (i, 0)), pl.BlockSpec((1, D), lambda i: (0, 0))],
        out_specs=[pl.BlockSpec((rb, D), lambda i: (i, 0)), pl.BlockSpec((D, rb), lambda i: (0, i))],
        out_shape=[jax.ShapeDtypeStruct((T, D), BF16), jax.ShapeDtypeStruct((D, T), BF16)], compiler_params=_params(1),
    )(x, w)


def _out_proj(merged, w_out, x, norm_w):
    T = merged.shape[0]
    bm = _pick(T, (512, 256))

    def body(a_ref, b_ref, x_ref, w_ref, h_ref, v_ref, vt_ref):
        h = _dot(a_ref[...], b_ref[...], 1, 0) + x_ref[...]
        h_ref[...] = h
        r = lax.rsqrt(jnp.mean(h * h, axis=-1, keepdims=True) + EPS)
        v2 = h * r * w_ref[...]
        v_ref[...] = v2.astype(BF16)
        vt_ref[...] = v2.T.astype(BF16)

    blk = pl.BlockSpec((bm, D), lambda i: (i, 0))
    return pl.pallas_call(
        body, name="out_proj", grid=(T // bm,),
        in_specs=[blk, pl.BlockSpec((D, D), lambda i: (0, 0)), blk, pl.BlockSpec((1, D), lambda i: (0, 0))],
        out_specs=[blk, blk, pl.BlockSpec((D, bm), lambda i: (0, i))],
        out_shape=[jax.ShapeDtypeStruct((T, D), F32), jax.ShapeDtypeStruct((T, D), BF16),
                   jax.ShapeDtypeStruct((D, T), BF16)],
        compiler_params=_params(1),
    )(merged, w_out, x, norm_w)


def _rms_bwd(dy, h, w, resid, *, name):
    T = h.shape[0]
    rb = _rows(T)

    def body(dy_ref, h_ref, w_ref, res_ref, dh_ref, dhb_ref, dw_ref):
        @pl.when(pl.program_id(0) == 0)
        def _():
            dw_ref[...] = jnp.zeros_like(dw_ref)

        hv = h_ref[...]
        r = lax.rsqrt(jnp.mean(hv * hv, axis=-1, keepdims=True) + EPS)
        hn = hv * r
        dyv = dy_ref[...]
        dw_ref[...] += jnp.sum(dyv * hn, axis=0, keepdims=True)
        t = dyv * w_ref[...]
        dh = res_ref[...] + r * (t - hn * jnp.mean(t * hn, axis=-1, keepdims=True))
        dh_ref[...] = dh
        dhb_ref[...] = dh.astype(BF16)

    blk = pl.BlockSpec((rb, D), lambda i: (i, 0))
    vec = pl.BlockSpec((1, D), lambda i: (0, 0))
    return pl.pallas_call(
        body, name=name, grid=(T // rb,), in_specs=[blk, blk, vec, blk], out_specs=[blk, blk, vec],
        out_shape=[jax.ShapeDtypeStruct((T, D), F32), jax.ShapeDtypeStruct((T, D), BF16),
                   jax.ShapeDtypeStruct((1, D), F32)],
        compiler_params=_params(1),
    )(dy, h, w, resid)


def _rms_bwd_input(du_main, dproj_lr, w_in_lr, x, w, resid):
    T = x.shape[0]
    rb = _rows(T)

    def body(du_ref, dl_ref, wl_ref, x_ref, w_ref, res_ref, dx_ref, dw_ref):
        @pl.when(pl.program_id(0) == 0)
        def _():
            dw_ref[...] = jnp.zeros_like(dw_ref)

        xv = x_ref[...]
        r = lax.rsqrt(jnp.mean(xv * xv, axis=-1, keepdims=True) + EPS)
        xn = xv * r
        du = du_ref[...] + _dot(dl_ref[...], wl_ref[...], 1, 1)
        dw_ref[...] += jnp.sum(du * xn, axis=0, keepdims=True)
        t = du * w_ref[...]
        dx_ref[...] = res_ref[...] + r * (t - xn * jnp.mean(t * xn, axis=-1, keepdims=True))

    blk = pl.BlockSpec((rb, D), lambda i: (i, 0))
    vec = pl.BlockSpec((1, D), lambda i: (0, 0))
    return pl.pallas_call(
        body, name="rms1_bwd", grid=(T // rb,),
        in_specs=[blk, pl.BlockSpec((rb, LRP), lambda i: (i, 0)), pl.BlockSpec((D, LRP), lambda i: (0, 0)), blk, vec,
                  blk],
        out_specs=[blk, vec], out_shape=[jax.ShapeDtypeStruct((T, D), F32), jax.ShapeDtypeStruct((1, D), F32)],
        compiler_params=_params(1),
    )(du_main, dproj_lr, w_in_lr, x, w, resid)


def _final_loss(h, w, target):
    T = h.shape[0]
    rb = _rows(T)

    def body(h_ref, w_ref, t_ref, loss_ref, dh_ref, dhb_ref, dw_ref):
        @pl.when(pl.program_id(0) == 0)
        def _():
            dw_ref[...] = jnp.zeros_like(dw_ref)
            loss_ref[...] = jnp.zeros_like(loss_ref)

        hv = h_ref[...]
        wv = w_ref[...]
        r = lax.rsqrt(jnp.mean(hv * hv, axis=-1, keepdims=True) + EPS)
        hn = hv * r
        e = hn * wv - t_ref[...]
        row = jnp.sum(e * e, axis=-1, keepdims=True) * (0.5 / D)
        loss_ref[...] += jnp.broadcast_to(jnp.sum(row, axis=0, keepdims=True), loss_ref.shape)
        dy = e * (1.0 / D)
        dw_ref[...] += jnp.sum(dy * hn, axis=0, keepdims=True)
        t = dy * wv
        dh = r * (t - hn * jnp.mean(t * hn, axis=-1, keepdims=True))
        dh_ref[...] = dh
        dhb_ref[...] = dh.astype(BF16)

    blk = pl.BlockSpec((rb, D), lambda i: (i, 0))
    vec = pl.BlockSpec((1, D), lambda i: (0, 0))
    return pl.pallas_call(
        body, name="final_loss", grid=(T // rb,), in_specs=[blk, vec, blk],
        out_specs=[pl.BlockSpec((8, 128), lambda i: (0, 0)), blk, blk, vec],
        out_shape=[jax.ShapeDtypeStruct((8, 128), F32), jax.ShapeDtypeStruct((T, D), F32),
                   jax.ShapeDtypeStruct((T, D), BF16), jax.ShapeDtypeStruct((1, D), F32)],
        compiler_params=_params(1),
    )(h, w, target)


PANEL = FFN // 4


def _ffn_fwd(v2, w_gate, w_up):
    T = v2.shape[0]
    bm = _pick(T, (512, 256))

    def body(a_ref, bg_ref, bu_ref, gate_ref, up_ref, ff_ref, fft_ref):
        a = a_ref[...]
        g = _dot(a, bg_ref[...], 1, 0)
        u = _dot(a, bu_ref[...], 1, 0)
        gate_ref[...] = g
        up_ref[...] = u
        ff = g * _sigmoid(g) * u
        ff_ref[...] = ff.astype(BF16)
        fft_ref[...] = ff.T.astype(BF16)

    panel = pl.BlockSpec((None, D, PANEL), lambda j, m: (j, 0, 0))
    out = pl.BlockSpec((bm, PANEL), lambda j, m: (m, j))
    return pl.pallas_call(
        body, name="ffn_fwd", grid=(4, T // bm),
        in_specs=[pl.BlockSpec((bm, D), lambda j, m: (m, 0)), panel, panel],
        out_specs=[out, out, out, pl.BlockSpec((PANEL, bm), lambda j, m: (j, m))],
        out_shape=[jax.ShapeDtypeStruct((T, FFN), F32), jax.ShapeDtypeStruct((T, FFN), F32),
                   jax.ShapeDtypeStruct((T, FFN), BF16), jax.ShapeDtypeStruct((FFN, T), BF16)],
        compiler_params=_params(2),
    )(v2, w_gate, w_up)


def _ffn_bwd_hidden(dh2, w_down, gate, up):
    T = dh2.shape[0]
    bm = _pick(T, (512, 256))

    def body(a_ref, b_ref, g_ref, u_ref, dg_ref, du_ref):
        d = _dot(a_ref[...], b_ref[...], 1, 1)
        g = g_ref[...]
        sg = _sigmoid(g)
        du_ref[...] = (d * g * sg).astype(BF16)
        dg_ref[...] = (d * u_ref[...] * sg * (1.0 + g * (1.0 - sg))).astype(BF16)

    blk = pl.BlockSpec((bm, PANEL), lambda j, m: (m, j))
    return pl.pallas_call(
        body, name="ffn_bwd_hidden", grid=(4, T // bm),
        in_specs=[pl.BlockSpec((bm, D), lambda j, m: (m, 0)), pl.BlockSpec((PANEL, D), lambda j, m: (j, 0)), blk, blk],
        out_specs=[blk, blk], out_shape=[jax.ShapeDtypeStruct((T, FFN), BF16)] * 2, compiler_params=_params(2),
    )(dh2, w_down, gate, up)


def _ffn_bwd_input(dgate, dup, w_gate, w_up):
    T = dgate.shape[0]
    bm = _pick(T, (1024, 512, 256))
    bj = 1024

    def body(ag_ref, au_ref, bg_ref, bu_ref, o_ref, acc_ref):
        c = pl.program_id(2)
        p = _dot(ag_ref[...], bg_ref[...], 1, 1) + _dot(au_ref[...], bu_ref[...], 1, 1)

        @pl.when(c == 0)
        def _():
            acc_ref[...] = p

        @pl.when(c > 0)
        def _():
            acc_ref[...] += p

        @pl.when(c == 3)
        def _():
            o_ref[...] = acc_ref[...]

    a = pl.BlockSpec((bm, PANEL), lambda m, j, c: (m, c))
    b = pl.BlockSpec((None, bj, PANEL), lambda m, j, c: (c, j, 0))
    return pl.pallas_call(
        body, name="ffn_bwd_input", grid=(T // bm, D // bj, 4), in_specs=[a, a, b, b],
        out_specs=pl.BlockSpec((bm, bj), lambda m, j, c: (m, j)), out_shape=jax.ShapeDtypeStruct((T, D), F32),
        scratch_shapes=[pltpu.VMEM((bm, bj), F32)], compiler_params=_params(3),
    )(dgate, dup, w_gate, w_up)


def _colsum(x, *, name):
    T, W = x.shape
    rb = _rows(T)

    def body(x_ref, o_ref):
        @pl.when(pl.program_id(0) == 0)
        def _():
            o_ref[...] = jnp.zeros_like(o_ref)

        o_ref[...] += jnp.sum(x_ref[...], axis=0, keepdims=True)

    return pl.pallas_call(
        body, name=name, grid=(T // rb,), in_specs=[pl.BlockSpec((rb, W), lambda i: (i, 0))],
        out_specs=pl.BlockSpec((1, W), lambda i: (0, 0)), out_shape=jax.ShapeDtypeStruct((1, W), F32),
        compiler_params=_params(1),
    )(x)


def _merge_fwd(attn_o, gla_raw, proj, gla_norm_w):
    T = attn_o.shape[0]
    rb = _rows(T)

    def body(a_ref, g_ref, gr_ref, ga_ref, gb_ref, w_ref, o_ref, ot_ref):
        graw = g_ref[...]
        r = lax.rsqrt(jnp.mean(graw * graw, axis=-1, keepdims=True) + EPS)
        gr = gr_ref[...]
        go = graw * r * w_ref[...] * (gr * _sigmoid(gr))
        merged = _sigmoid(ga_ref[...]) * a_ref[...] + _sigmoid(gb_ref[...]) * go
        o_ref[...] = merged.astype(BF16)
        ot_ref[...] = merged.T.astype(BF16)

    def sec(k):
        return pl.BlockSpec((rb, DV), lambda i, h: (i, O_MERGE // DV + 3 * h + k))

    blk = pl.BlockSpec((rb, DV), lambda i, h: (i, h))
    return pl.pallas_call(
        body, name="merge_fwd", grid=(T // rb, GLA_H),
        in_specs=[blk, blk, sec(0), sec(1), sec(2), pl.BlockSpec((1, DV), lambda i, h: (0, 0))],
        out_specs=[blk, pl.BlockSpec((DV, rb), lambda i, h: (h, i))],
        out_shape=[jax.ShapeDtypeStruct((T, D), BF16), jax.ShapeDtypeStruct((D, T), BF16)], compiler_params=_params(2),
    )(attn_o, gla_raw, proj, proj, proj, gla_norm_w)


def _merge_bwd(dm, attn_o, gla_raw, proj, gla_norm_w):
    T = attn_o.shape[0]
    rb = _rows(T)

    def body(dm_ref, a_ref, g_ref, gr_ref, ga_ref, gb_ref, w_ref, da_ref, dg_ref, dp_ref, dw_ref):
        @pl.when((pl.program_id(0) == 0) & (pl.program_id(1) == 0))
        def _():
            dw_ref[...] = jnp.zeros_like(dw_ref)

        dmv = dm_ref[...]
        av = a_ref[...]
        graw = g_ref[...]
        gr = gr_ref[...]
        wv = w_ref[...]
        sa = _sigmoid(ga_ref[...])
        sb = _sigmoid(gb_ref[...])
        r = lax.rsqrt(jnp.mean(graw * graw, axis=-1, keepdims=True) + EPS)
        gnh = graw * r
        gn = gnh * wv
        sr = _sigmoid(gr)
        sl = gr * sr
        go = gn * sl
        da_ref[...] = (dmv * sa).astype(BF16)
        dgo = dmv * sb
        dp_ref[:, 0:DV] = (dgo * gn * sr * (1.0 + gr * (1.0 - sr))).astype(BF16)
        dp_ref[:, DV:2 * DV] = (dmv * av * sa * (1.0 - sa)).astype(BF16)
        dp_ref[:, 2 * DV:3 * DV] = (dmv * go * sb * (1.0 - sb)).astype(BF16)
        dgn = dgo * sl
        dw_ref[...] += jnp.sum(dgn * gnh, axis=0, keepdims=True)
        t = dgn * wv
        dg_ref[...] = (r * (t - gnh * jnp.mean(t * gnh, axis=-1, keepdims=True))).astype(BF16)

    def sec(k):
        return pl.BlockSpec((rb, DV), lambda i, h: (i, O_MERGE // DV + 3 * h + k))

    blk = pl.BlockSpec((rb, DV), lambda i, h: (i, h))
    vec = pl.BlockSpec((1, DV), lambda i, h: (0, 0))
    return pl.pallas_call(
        body, name="merge_bwd", grid=(T // rb, GLA_H),
        in_specs=[blk, blk, blk, sec(0), sec(1), sec(2), vec],
        out_specs=[blk, blk, pl.BlockSpec((rb, W_MERGE), lambda i, h: (i, O_MERGE // W_MERGE + h)), vec],
        out_shape=[jax.ShapeDtypeStruct((T, D), BF16), jax.ShapeDtypeStruct((T, D), BF16),
                   jax.ShapeDtypeStruct((T, MAIN), BF16), jax.ShapeDtypeStruct((1, DV), F32)],
        compiler_params=_params(2),
    )(dm, attn_o, gla_raw, proj, proj, proj, gla_norm_w)


def _attn_band(n):
    qi = lax.broadcasted_iota(jnp.int32, (GROUP * WINDOW, WINDOW), 0) & (WINDOW - 1)
    kj = lax.broadcasted_iota(jnp.int32, (GROUP * WINDOW, WINDOW), 1)
    cur = kj <= qi
    return cur, cur | (n > 0)


def _stack_heads(ref, h, scale=1.0):
    rows = jnp.concatenate(
        [ref[:, (h * GROUP + g) * HEAD_DIM:(h * GROUP + g + 1) * HEAD_DIM] for g in range(GROUP)], axis=0)
    return (rows * scale).astype(BF16) if scale != 1.0 else rows.astype(BF16)


def _stack_sinks(s_ref, h):
    return jnp.concatenate(
        [jnp.broadcast_to(s_ref[:, h * GROUP + g:h * GROUP + g + 1], (WINDOW, 1)) for g in range(GROUP)], axis=0)


def _attn_probs(qs, kp, kc, band, sink):
    cur, live = band
    s = jnp.where(cur, _dot(qs, kc, 1, 1), _dot(qs, kp, 1, 1))
    s = jnp.where(live, s, MASK_VALUE)
    m = jnp.maximum(jnp.max(s, axis=-1, keepdims=True), sink)
    e = jnp.exp(s - m)
    es = jnp.exp(sink - m)
    inv = 1.0 / (jnp.sum(e, axis=-1, keepdims=True) + es)
    return e * inv, es * inv


def _unfold(cur, x):
    mine = jnp.where(cur, x, 0.0)
    return mine.astype(BF16), (x - mine).astype(BF16)


def _attn_specs(nb, rev):
    def at(n):
        return (nb - 1 - n) if rev else n

    kcol, vcol = (O_ATTN + D) // 256, (O_ATTN + D) // 256 + 1
    q = pl.BlockSpec((WINDOW, D), lambda n: (at(n), O_ATTN // D))
    kc = pl.BlockSpec((WINDOW, 256), lambda n: (at(n), kcol))
    kp = pl.BlockSpec((WINDOW, 256), lambda n: (jnp.maximum(at(n) - 1, 0), kcol))
    vc = pl.BlockSpec((WINDOW, 256), lambda n: (at(n), vcol))
    vp = pl.BlockSpec((WINDOW, 256), lambda n: (jnp.maximum(at(n) - 1, 0), vcol))
    sk = pl.BlockSpec((1, 128), lambda n: (0, 0))
    o = pl.BlockSpec((WINDOW, D), lambda n: (at(n), 0))
    return q, kc, kp, vc, vp, sk, o


def _attn_fwd(proj, sinks):
    T = proj.shape[0]
    nb = T // WINDOW

    def body(q_ref, kc_ref, kp_ref, vc_ref, vp_ref, s_ref, o_ref):
        band = _attn_band(pl.program_id(0))
        for h in range(N_KV):
            hs = slice(h * HEAD_DIM, (h + 1) * HEAD_DIM)
            p, _ = _attn_probs(_stack_heads(q_ref, h, HEAD_DIM ** -0.5), kp_ref[:, hs].astype(BF16),
                               kc_ref[:, hs].astype(BF16), band, _stack_sinks(s_ref, h))
            p_cur, p_prev = _unfold(band[0], p)
            o = _dot(p_cur, vc_ref[:, hs].astype(BF16), 1, 0) + _dot(p_prev, vp_ref[:, hs].astype(BF16), 1, 0)
            for g in range(GROUP):
                hg = h * GROUP + g
                o_ref[:, hg * HEAD_DIM:(hg + 1) * HEAD_DIM] = o[g * WINDOW:(g + 1) * WINDOW]

    q, kc, kp, vc, vp, sk, o = _attn_specs(nb, False)
    return pl.pallas_call(
        body, name="attn_fwd", grid=(nb,), in_specs=[q, kc, kp, vc, vp, sk], out_specs=o,
        out_shape=jax.ShapeDtypeStruct((T, D), F32), compiler_params=_params(1),
    )(proj, proj, proj, proj, proj, sinks)


def _attn_bwd(proj, sinks, d_o, dproj):
    T = proj.shape[0]
    nb = T // WINDOW
    kat, vat = D, D + N_KV * HEAD_DIM

    def body(q_ref, kc_ref, kp_ref, vc_ref, vp_ref, s_ref, do_ref, _, dp_ref, ds_ref, ck_ref, cv_ref):
        i = pl.program_id(0)
        n = nb - 1 - i

        @pl.when(i == 0)
        def _():
            ck_ref[...] = jnp.zeros_like(ck_ref)
            cv_ref[...] = jnp.zeros_like(cv_ref)
            ds_ref[...] = jnp.zeros_like(ds_ref)

        band = _attn_band(n)
        for h in range(N_KV):
            hs = slice(h * HEAD_DIM, (h + 1) * HEAD_DIM)
            kp, kc = kp_ref[:, hs].astype(BF16), kc_ref[:, hs].astype(BF16)
            vp, vc = vp_ref[:, hs].astype(BF16), vc_ref[:, hs].astype(BF16)
            qs = _stack_heads(q_ref, h, HEAD_DIM ** -0.5)
            do = _stack_heads(do_ref, h)
            p, ps = _attn_probs(qs, kp, kc, band, _stack_sinks(s_ref, h))
            dp = jnp.where(band[0], _dot(do, vc, 1, 1), _dot(do, vp, 1, 1))
            delta = jnp.sum(p * dp, axis=-1, keepdims=True)
            ds_cur, ds_prev = _unfold(band[0], p * (dp - delta))
            p_cur, p_prev = _unfold(band[0], p)
            dsink = -ps * delta
            dq = ((_dot(ds_cur, kc, 1, 0) + _dot(ds_prev, kp, 1, 0)) * (HEAD_DIM ** -0.5)).astype(BF16)
            for g in range(GROUP):
                hg = h * GROUP + g
                rows = slice(g * WINDOW, (g + 1) * WINDOW)
                ds_ref[hg:hg + 1, :] += jnp.broadcast_to(jnp.sum(dsink[rows], axis=0, keepdims=True), (1, 128))
                dp_ref[:, hg * HEAD_DIM:(hg + 1) * HEAD_DIM] = dq[rows]
            dp_ref[:, kat + h * HEAD_DIM:kat + (h + 1) * HEAD_DIM] = (_dot(ds_cur, qs, 0, 0) + ck_ref[:, hs]).astype(BF16)
            dp_ref[:, vat + h * HEAD_DIM:vat + (h + 1) * HEAD_DIM] = (_dot(p_cur, do, 0, 0) + cv_ref[:, hs]).astype(BF16)
            ck_ref[:, hs] = _dot(ds_prev, qs, 0, 0)
            cv_ref[:, hs] = _dot(p_prev, do, 0, 0)

    q, kc, kp, vc, vp, sk, o = _attn_specs(nb, True)
    return pl.pallas_call(
        body, name="attn_bwd", grid=(nb,), in_specs=[q, kc, kp, vc, vp, sk, o, ANY],
        out_specs=[pl.BlockSpec((WINDOW, W_ATTN), lambda n: (nb - 1 - n, O_ATTN // W_ATTN)),
                   pl.BlockSpec((N_Q, 128), lambda n: (0, 0))],
        out_shape=[jax.ShapeDtypeStruct((T, MAIN), BF16), jax.ShapeDtypeStruct((N_Q, 128), F32)],
        scratch_shapes=[pltpu.VMEM((WINDOW, 256), F32), pltpu.VMEM((WINDOW, 256), F32)],
        input_output_aliases={7: 0}, compiler_params=_params(1),
    )(proj, proj, proj, proj, proj, sinks, d_o, dproj)


def _split3(x):
    hi = x.astype(BF16)
    r1 = x - hi.astype(F32)
    mid = r1.astype(BF16)
    lo = (r1 - mid.astype(F32)).astype(BF16)
    return hi, mid, lo


def _tri_sum(tri, x):
    hi, mid, lo = _split3(x)
    return _dot(tri, hi, 1, 0) + _dot(tri, mid, 1, 0) + _dot(tri, lo, 1, 0)


def _chunk_masks(rb):
    r = lax.broadcasted_iota(jnp.int32, (rb, rb), 0)
    c = lax.broadcasted_iota(jnp.int32, (rb, rb), 1)
    same = (r ^ c) < CHUNK
    return same & (r >= c), same & (r <= c)


def _per_chunk_rows(x, per, pick):
    return jnp.concatenate(
        [jnp.broadcast_to(pick(x[i * CHUNK:(i + 1) * CHUNK]), (CHUNK, x.shape[1])) for i in range(per)], axis=0)


def _gla_block(q, k, lr, w2, b, lower):
    per = q.shape[0] // CHUNK
    logit = _dot(lr.astype(BF16), w2.astype(BF16), 1, 0) + b
    la = (jnp.minimum(logit, 0.0) - jnp.log(1.0 + jnp.exp(-jnp.abs(logit)))) * (1.0 / GATE_NORM)
    g = _tri_sum(lower, la)
    gl = _per_chunk_rows(g, per, lambda c: c[CHUNK - 1:CHUNK])
    eg = jnp.exp(g)
    qd = q * (DK ** -0.5) * eg
    ki = k * jnp.exp(-g)
    ke = k * jnp.exp(gl - g)
    return logit, g, gl, eg, qd, ki, ke


def _gla_rows(T, forward):
    return _pick(T, (512, 256, 128, 64) if forward else (256, 128, 64))


def _gla_fwd(proj, proj_lr, w2p, gate_b):
    T = proj.shape[0]
    rb = _gla_rows(T, True)
    per = rb // CHUNK

    def body(q_ref, k_ref, v_ref, lr_ref, w2_ref, b_ref, o_ref, st_ref, s_scr):
        @pl.when(pl.program_id(1) == 0)
        def _():
            s_scr[...] = jnp.zeros_like(s_scr)

        low, _ = _chunk_masks(rb)
        _, _, gl, _, qd, ki, ke = _gla_block(q_ref[...], k_ref[...], lr_ref[...], w2_ref[...], b_ref[...],
                                             low.astype(BF16))
        v = v_ref[...].astype(BF16)
        qdb, keb = qd.astype(BF16), ke.astype(BF16)
        att = jnp.where(low, _dot(qdb, ki.astype(BF16), 1, 1), 0.0).astype(BF16)
        o_intra = _dot(att, v, 1, 0)
        st = s_scr[...]
        for i in range(per):
            rows = slice(i * CHUNK, (i + 1) * CHUNK)
            st_ref[0, i] = st
            o_ref[rows, :] = o_intra[rows] + _dot(qdb[rows], st.astype(BF16), 1, 1)
            st = st * jnp.exp(gl[i * CHUNK:i * CHUNK + 1]) + _dot(v[rows], keb[rows], 0, 0)
        s_scr[...] = st

    return pl.pallas_call(
        body, name="gla_fwd", grid=(GLA_H, T // rb),
        in_specs=[pl.BlockSpec((rb, DK), lambda h, n: (n, (O_GLA + W_GLA * h) // DK)),
                  pl.BlockSpec((rb, DK), lambda h, n: (n, (O_GLA + W_GLA * h) // DK + 1)),
                  pl.BlockSpec((rb, DV), lambda h, n: (n, (O_GLA + W_GLA * h) // DV + 1)),
                  pl.BlockSpec((rb, LRP), lambda h, n: (n, 0)),
                  pl.BlockSpec((LRP, DK), lambda h, n: (0, h)),
                  pl.BlockSpec((1, DK), lambda h, n: (0, h))],
        out_specs=[pl.BlockSpec((rb, DV), lambda h, n: (n, h)),
                   pl.BlockSpec((1, per, DV, DK), lambda h, n: (h, n, 0, 0))],
        out_shape=[jax.ShapeDtypeStruct((T, GLA_H * DV), F32),
                   jax.ShapeDtypeStruct((GLA_H, T // CHUNK, DV, DK), F32)],
        scratch_shapes=[pltpu.VMEM((DV, DK), F32)], compiler_params=_params(2),
    )(proj, proj, proj, proj_lr, w2p, gate_b)


def _gla_bwd(proj, proj_lr, w2p, gate_b, states, d_o, dproj):
    T = proj.shape[0]
    rb = _gla_rows(T, False)
    per = rb // CHUNK
    nblk = T // rb

    def body(q_ref, k_ref, v_ref, lr_ref, w2_ref, b_ref, st_ref, do_ref, _, dp_ref, dl_ref, ds_scr):
        @pl.when(pl.program_id(1) == 0)
        def _():
            ds_scr[...] = jnp.zeros_like(ds_scr)

        low, up = _chunk_masks(rb)
        logit, g, gl, eg, qd, ki, ke = _gla_block(q_ref[...], k_ref[...], lr_ref[...], w2_ref[...], b_ref[...],
                                                  low.astype(BF16))
        v = v_ref[...].astype(BF16)
        do = do_ref[...].astype(BF16)
        qdb, kib, keb = qd.astype(BF16), ki.astype(BF16), ke.astype(BF16)
        att = jnp.where(low, _dot(qdb, kib, 1, 1), 0.0).astype(BF16)
        datt = jnp.where(low, _dot(do, v, 1, 1), 0.0).astype(BF16)
        dv_intra = _dot(att, do, 0, 0)
        dqd_intra = _dot(datt, kib, 1, 0)
        dki = _dot(datt, qdb, 0, 0)
        ds = ds_scr[...]
        dqd, dke, dgl = [None] * per, [None] * per, [None] * per
        for i in reversed(range(per)):
            rows = slice(i * CHUNK, (i + 1) * CHUNK)
            sp = st_ref[0, i]
            dsb = ds.astype(BF16)
            dp_ref[rows, 2 * DK:] = (dv_intra[rows] + _dot(keb[rows], dsb, 1, 1)).astype(BF16)
            dqd[i] = dqd_intra[rows] + _dot(do[rows], sp.astype(BF16), 1, 0)
            dke[i] = _dot(v[rows], dsb, 1, 0)
            decay = jnp.exp(gl[i * CHUNK:i * CHUNK + 1])
            dgl[i] = (jnp.sum(dke[i] * ke[rows], axis=0, keepdims=True)
                      + jnp.sum(ds * sp, axis=0, keepdims=True) * decay)
            ds = ds * decay + _dot(do[rows], qdb[rows], 0, 0)
        ds_scr[...] = ds
        dqd = jnp.concatenate(dqd, axis=0)
        dke = jnp.concatenate(dke, axis=0)
        dgl = jnp.concatenate([jnp.broadcast_to(d, (CHUNK, DK)) for d in dgl], axis=0)
        dp_ref[:, 0:DK] = (dqd * (DK ** -0.5) * eg).astype(BF16)
        dp_ref[:, DK:2 * DK] = (dki * jnp.exp(-g) + dke * jnp.exp(gl - g)).astype(BF16)
        dg = dqd * qd - dki * ki - dke * ke
        dla = _tri_sum(up.astype(BF16), dg) + dgl
        dl_ref[...] = dla * (1.0 / GATE_NORM) * (1.0 - _sigmoid(logit))

    def rev(n):
        return nblk - 1 - n

    return pl.pallas_call(
        body, name="gla_bwd", grid=(GLA_H, nblk),
        in_specs=[pl.BlockSpec((rb, DK), lambda h, n: (rev(n), (O_GLA + W_GLA * h) // DK)),
                  pl.BlockSpec((rb, DK), lambda h, n: (rev(n), (O_GLA + W_GLA * h) // DK + 1)),
                  pl.BlockSpec((rb, DV), lambda h, n: (rev(n), (O_GLA + W_GLA * h) // DV + 1)),
                  pl.BlockSpec((rb, LRP), lambda h, n: (rev(n), 0)),
                  pl.BlockSpec((LRP, DK), lambda h, n: (0, h)),
                  pl.BlockSpec((1, DK), lambda h, n: (0, h)),
                  pl.BlockSpec((1, per, DV, DK), lambda h, n: (h, rev(n), 0, 0)),
                  pl.BlockSpec((rb, DV), lambda h, n: (rev(n), h)),
                  ANY],
        out_specs=[pl.BlockSpec((rb, W_GLA), lambda h, n: (rev(n), O_GLA // W_GLA + h)),
                   pl.BlockSpec((rb, DK), lambda h, n: (rev(n), h))],
        out_shape=[jax.ShapeDtypeStruct((T, MAIN), BF16), jax.ShapeDtypeStruct((T, GLA_H * DK), F32)],
        scratch_shapes=[pltpu.VMEM((DV, DK), F32)], input_output_aliases={8: 0}, compiler_params=_params(2),
    )(proj, proj, proj, proj_lr, w2p, gate_b, states, d_o, dproj)


def _local_step(x, target, p, reduce_part):
    u, u_t = _rms_fwd(x, p["norm1_w"], name="rms1_fwd")
    whole = _pick(x.shape[0], (2048, 256))
    proj = _matmul(u, p["w_in_main"], bm=whole, name="mm_proj")
    proj_lr = _matmul(u, p["w_in_lr"], name="mm_proj_lr")
    attn_o = _attn_fwd(proj, p["sinks"])
    gla_raw, states = _gla_fwd(proj, proj_lr, p["w2p"], p["gate_b"])
    merged, merged_t = _merge_fwd(attn_o, gla_raw, proj, p["gla_norm_w"])
    h1, v2, v2_t = _out_proj(merged, p["w_out"], x, p["norm2_w"])
    gate, up, ff, ff_t = _ffn_fwd(v2, p["w_gate"], p["w_up"])
    h2 = _matmul(ff, p["w_down"], add=h1, name="mm_down")
    loss, dh2, dh2_b, g_final = _final_loss(h2, p["final_norm_w"], target)

    dgate, dup = _ffn_bwd_hidden(dh2_b, p["w_down"], gate, up)
    reduce_part("w_down", _matmul(ff_t, dh2_b, out_dtype=BF16, bj=D, name="mm_gdown").reshape(4, FFN // 4, D))
    reduce_part("w_gate", _matmul(v2_t, dgate, out_dtype=BF16, bj=PANEL, out_slots=True, name="mm_ggate"))
    reduce_part("w_up", _matmul(v2_t, dup, out_dtype=BF16, bj=PANEL, out_slots=True, name="mm_gup"))
    dv2 = _ffn_bwd_input(dgate, dup, p["w_gate"], p["w_up"])
    dh1, dh1_b, g_norm2 = _rms_bwd(dv2, h1, p["norm2_w"], dh2, name="rms2_bwd")
    dmerged = _matmul(dh1_b, p["w_out"], tb=True, bj=D, name="mm_dmerged")
    reduce_part("w_out", _matmul(merged_t, dh1_b, out_dtype=BF16, bj=D, name="mm_gout").reshape(4, D // 4, D))
    d_attn, d_gla, dproj, g_gla_norm = _merge_bwd(dmerged, attn_o, gla_raw, proj, p["gla_norm_w"])
    dproj, dlogit = _gla_bwd(proj, proj_lr, p["w2p"], p["gate_b"], states, d_gla, dproj)
    dproj, g_sinks = _attn_bwd(proj, p["sinks"], d_attn, dproj)
    g_gate_b = _colsum(dlogit, name="colsum_gate_b")
    g_w2 = _matmul(proj_lr, dlogit, ta=True, name="mm_gw2")
    dproj_lr = _matmul(dlogit, p["w2p"], tb=True, out_dtype=BF16, name="mm_dlr")
    g_in_main = _matmul(u_t, dproj, out_dtype=BF16, bm=D, name="mm_gin")
    g_in_lr = _matmul(u_t, dproj_lr, out_dtype=BF16, name="mm_gin_lr")
    du = _matmul(dproj, p["w_in_main"], tb=True, bm=_pick(x.shape[0], (1024, 256)), bj=1024, bc=3200, name="mm_du")
    dx, g_norm1 = _rms_bwd_input(du, dproj_lr, p["w_in_lr"], x, p["norm1_w"], dh1)
    grads = dict(norm1_w=g_norm1, w_in_main=g_in_main, w_in_lr=g_in_lr, w2=g_w2[:RANK], gate_b=g_gate_b,
                 sinks=g_sinks[:, 0].reshape(1, N_Q), gla_norm_w=g_gla_norm, norm2_w=g_norm2, final_norm_w=g_final)
    return loss, dx, grads


def _place():
    x, y, c = lax.axis_index("x"), lax.axis_index("y"), lax.axis_index("c")
    chips = [(1 - x, y), (x, 1 - y), (1 - x, 1 - y)]
    return x, y, c, chips


def _hbm_shape(s, dt):
    return jax.ShapeDtypeStruct(s, dt)


ID_SIBLING = 9
ID_GATHER_IN = 10


def _handshake(peers):
    barrier = pltpu.get_barrier_semaphore()
    for peer in peers:
        pl.semaphore_signal(barrier, inc=1, device_id=peer, device_id_type=MESH)
    pl.semaphore_wait(barrier, len(peers))


def _launch_copies(body, args, out_shape, sems, *, name, collective_id=None):
    if collective_id is None:
        return pl.pallas_call(
            body, name=name, in_specs=[ANY] * len(args), out_specs=[ANY] * len(out_shape), out_shape=out_shape,
            scratch_shapes=sems)(*args)
    return pl.kernel(
        body, name=name, out_type=out_shape, mesh=plsc.ScalarSubcoreMesh(axis_name="sequencer", num_cores=1),
        scratch_types=sems, compiler_params=pltpu.CompilerParams(collective_id=collective_id))(*args)


def _gather_shards(shards, *, name, collective_id=None, after=()):
    n = len(shards)
    first_out = n + len(after)

    def body(*refs):
        ins, outs = refs[:n], refs[first_out:first_out + n]
        ici_send, ici_recv, d2d_send, d2d_recv, local_sem = refs[first_out + n:]
        x, y, c, chips = _place()
        me = 2 * x + y
        sibling = (x, y, 1 - c)
        if collective_id is not None:
            _handshake([sibling] + [(*chip, c) for chip in chips])

        def half(w, slot, hc):
            r2 = shards[w].shape[0] // 2
            return outs[w].at[slot, pl.ds(hc * r2, r2), :]

        locals_ = [pltpu.make_async_copy(ins[w], outs[w].at[me], local_sem.at[w]) for w in range(n)]
        for cp in locals_:
            cp.start()
        sends = []
        for w in range(n):
            r2 = shards[w].shape[0] // 2
            for j, chip in enumerate(chips):
                cp = pltpu.make_async_remote_copy(
                    src_ref=ins[w].at[pl.ds(c * r2, r2), :], dst_ref=half(w, me, c),
                    send_sem=ici_send.at[w * 3 + j], recv_sem=ici_recv.at[w * 3 + j],
                    device_id=(*chip, c), device_id_type=MESH)
                cp.start()
                sends.append(cp)
        for w in range(n):
            for j, chip in enumerate(chips):
                slot = 2 * chip[0] + chip[1]
                got = half(w, slot, c)
                pltpu.make_async_remote_copy(
                    src_ref=got, dst_ref=got, send_sem=ici_send.at[w * 3 + j], recv_sem=ici_recv.at[w * 3 + j],
                    device_id=(*chip, c), device_id_type=MESH).wait_recv()
                cp = pltpu.make_async_remote_copy(
                    src_ref=got, dst_ref=got, send_sem=d2d_send.at[w * 3 + j], recv_sem=d2d_recv.at[w * 3 + j],
                    device_id=sibling, device_id_type=MESH)
                cp.start()
                sends.append(cp)
        for w in range(n):
            for j, chip in enumerate(chips):
                slot = 2 * chip[0] + chip[1]
                got = half(w, slot, 1 - c)
                pltpu.make_async_remote_copy(
                    src_ref=got, dst_ref=got, send_sem=d2d_send.at[w * 3 + j], recv_sem=d2d_recv.at[w * 3 + j],
                    device_id=sibling, device_id_type=MESH).wait_recv()
        for cp in sends:
            cp.wait_send()
        for cp in locals_:
            cp.wait()

    return _launch_copies(
        body, list(shards) + list(after), [_hbm_shape((4,) + s.shape, s.dtype) for s in shards],
        [pltpu.SemaphoreType.DMA((3 * n,)), pltpu.SemaphoreType.DMA((3 * n,)), pltpu.SemaphoreType.DMA((3 * n,)),
         pltpu.SemaphoreType.DMA((3 * n,)), pltpu.SemaphoreType.DMA((n,))],
        name=name, collective_id=collective_id)


def _gather_staged(w, small):
    R, C = w.shape
    R2 = R // 2
    rb = 256
    per = R2 // rb
    r2 = small.shape[0] // 2

    def body(w_ref, s_ref, out_ref, outs_ref, stage, ici_send, ici_recv, fwd_send, fwd_recv, in_sem, put_sem,
             push_send, push_recv):
        x, y, c, chips = _place()
        me = 2 * x + y
        sibling = (x, y, 1 - c)
        slots = [2 * chip[0] + chip[1] for chip in chips]
        _handshake([sibling] + [(*chip, c) for chip in chips])

        def ici(j, src, dst, k):
            return pltpu.make_async_remote_copy(
                src_ref=src, dst_ref=dst, send_sem=ici_send.at[k], recv_sem=ici_recv.at[k],
                device_id=(*chips[j], c), device_id_type=MESH)

        sends = []
        for j in range(3):
            sends.append(ici(j, w_ref.at[pl.ds(c * R2, R2), :], out_ref.at[me, pl.ds(c * R2, R2), :], j))
            sends.append(ici(j, s_ref.at[pl.ds(c * r2, r2), :], outs_ref.at[me, pl.ds(c * r2, r2), :], 3 + j))
        for cp in sends:
            cp.start()
        mine = pltpu.make_async_copy(s_ref, outs_ref.at[me], put_sem.at[2])
        mine.start()

        def own(k, to_vmem):
            rows = pl.ds(k * rb, rb)
            if to_vmem:
                return pltpu.make_async_copy(w_ref.at[rows, :], stage.at[k % 2], in_sem.at[k % 2])
            return pltpu.make_async_copy(stage.at[k % 2], out_ref.at[me, rows, :], put_sem.at[k % 2])

        for k in range(R // rb):
            if k >= 2:
                own(k - 2, False).wait()
            own(k, True).start()
            own(k, True).wait()
            own(k, False).start()
        for k in range(R // rb - 2, R // rb):
            own(k, False).wait()

        def block(j, k, hc):
            return out_ref.at[slots[j], pl.ds(hc * R2 + k * rb, rb), :]

        def push(t, hc):
            j, k = divmod(t, per)
            return pltpu.make_async_remote_copy(
                src_ref=stage.at[t % 2], dst_ref=block(j, k, hc), send_sem=push_send.at[t % 2],
                recv_sem=push_recv.at[t], device_id=sibling, device_id_type=MESH)

        passed = []
        for j in range(3):
            ici(j, w_ref.at[pl.ds(c * R2, R2), :], out_ref.at[slots[j], pl.ds(c * R2, R2), :], j).wait_recv()
            for k in range(per):
                t = j * per + k
                if t >= 2:
                    push(t - 2, c).wait_send()
                fetch = pltpu.make_async_copy(block(j, k, c), stage.at[t % 2], in_sem.at[t % 2])
                fetch.start()
                fetch.wait()
                push(t, c).start()
            got = outs_ref.at[slots[j], pl.ds(c * r2, r2), :]
            ici(j, got, got, 3 + j).wait_recv()
            fw = pltpu.make_async_remote_copy(
                src_ref=got, dst_ref=got, send_sem=fwd_send.at[j], recv_sem=fwd_recv.at[j],
                device_id=sibling, device_id_type=MESH)
            fw.start()
            passed.append(fw)
        for t in range(3 * per - 2, 3 * per):
            push(t, c).wait_send()
        for t in range(3 * per):
            push(t, 1 - c).wait_recv()
        for j in range(3):
            theirs = outs_ref.at[slots[j], pl.ds((1 - c) * r2, r2), :]
            pltpu.make_async_remote_copy(
                src_ref=theirs, dst_ref=theirs, send_sem=fwd_send.at[j], recv_sem=fwd_recv.at[j],
                device_id=sibling, device_id_type=MESH).wait_recv()
        for cp in sends + passed:
            cp.wait_send()
        mine.wait()

    dma = pltpu.SemaphoreType.DMA
    return pl.pallas_call(
        body, name="gather_in", in_specs=[ANY, ANY], out_specs=[ANY, ANY],
        out_shape=[_hbm_shape((4, R, C), w.dtype), _hbm_shape((4,) + small.shape, small.dtype)],
        scratch_shapes=[pltpu.VMEM((2, rb, C), w.dtype), dma((6,)), dma((6,)), dma((3,)), dma((3,)), dma((2,)),
                        dma((3,)), dma((2,)), dma((3 * per,))],
        compiler_params=pltpu.CompilerParams(vmem_limit_bytes=VMEM_LIMIT, collective_id=ID_GATHER_IN),
    )(w, small)


def _pair_blocks(R2):
    return _pick(R2, (256, 352, 128, 64, 32, 16))


def _pair_reduce(part, *, name):
    _, R, C = part.shape
    R2 = R // 2
    rb = _pick(R2, (512, 352, 256, 128, 64, 32, 16))
    nblk = R2 // rb
    steps = [(s, i) for s in range(4) for i in range(nblk)]
    n = len(steps)

    def body(part_ref, sums_ref, own_buf, oth_buf, rcv_buf, out_buf, own_sem, oth_sem, out_sem, send_sem, recv_sem):
        x, y, c, _ = _place()
        sibling = (x, y, 1 - c)
        _handshake([sibling])

        def fetch(t, half, buf, sem):
            s, i = steps[t]
            return pltpu.make_async_copy(part_ref.at[s, pl.ds(half * R2 + i * rb, rb), :], buf.at[t % 3], sem.at[t % 3])

        def push(t):
            return pltpu.make_async_remote_copy(
                src_ref=oth_buf.at[t % 3], dst_ref=rcv_buf.at[t % 4], send_sem=send_sem.at[t % 4],
                recv_sem=recv_sem.at[t % 4], device_id=sibling, device_id_type=MESH)

        def store(t):
            s, i = steps[t]
            return pltpu.make_async_copy(out_buf.at[t % 2], sums_ref.at[s, pl.ds(i * rb, rb), :], out_sem.at[t % 2])

        for t in range(min(2, n)):
            fetch(t, c, own_buf, own_sem).start()
            fetch(t, 1 - c, oth_buf, oth_sem).start()
        fetch(0, 1 - c, oth_buf, oth_sem).wait()
        push(0).start()
        for t in range(n):
            if t + 2 < n:
                fetch(t + 2, c, own_buf, own_sem).start()
                fetch(t + 2, 1 - c, oth_buf, oth_sem).start()
            if t + 1 < n:
                fetch(t + 1, 1 - c, oth_buf, oth_sem).wait()
                push(t + 1).start()
            fetch(t, c, own_buf, own_sem).wait()
            push(t).wait()
            if t >= 2:
                store(t - 2).wait()
            out_buf[t % 2] = (own_buf[t % 3].astype(F32) + rcv_buf[t % 4].astype(F32)).astype(BF16)
            store(t).start()
        for t in range(max(n - 2, 0), n):
            store(t).wait()

    def buf(k):
        return pltpu.VMEM((k, rb, C), BF16)

    def sems(k):
        return pltpu.SemaphoreType.DMA((k,))

    return pl.pallas_call(
        body, name=name, in_specs=[ANY], out_specs=ANY, out_shape=_hbm_shape((4, R2, C), BF16),
        scratch_shapes=[buf(3), buf(3), buf(4), buf(2), sems(3), sems(3), sems(2), sems(4), sems(4)],
        compiler_params=pltpu.CompilerParams(vmem_limit_bytes=VMEM_LIMIT, collective_id=ID_SIBLING),
    )(part)


def _chip_exchange(sums, *, name, collective_id=None):
    n = len(sums)

    def body(*refs):
        ins, outs = refs[:n], refs[n:2 * n]
        send_sem, recv_sem, local_sem = refs[2 * n:]
        x, y, c, chips = _place()
        me = 2 * x + y
        if collective_id is not None:
            _handshake([(*chip, c) for chip in chips])
        cps = []
        for w in range(n):
            lc = pltpu.make_async_copy(ins[w].at[me], outs[w].at[me], local_sem.at[w])
            lc.start()
            cps.append(lc)
            for j, chip in enumerate(chips):
                slot = 2 * chip[0] + chip[1]
                rc = pltpu.make_async_remote_copy(
                    src_ref=ins[w].at[slot], dst_ref=outs[w].at[me],
                    send_sem=send_sem.at[w * 3 + j], recv_sem=recv_sem.at[w * 3 + j],
                    device_id=(*chip, c), device_id_type=MESH)
                rc.start()
                cps.append(rc)
        for w in range(n):
            for j, chip in enumerate(chips):
                slot = 2 * chip[0] + chip[1]
                pltpu.make_async_remote_copy(
                    src_ref=ins[w].at[slot], dst_ref=outs[w].at[slot],
                    send_sem=send_sem.at[w * 3 + j], recv_sem=recv_sem.at[w * 3 + j],
                    device_id=(*chip, c), device_id_type=MESH).wait_recv()
        for w in range(n):
            cps[w * 4].wait()
            for j in range(3):
                cps[w * 4 + 1 + j].wait_send()

    return _launch_copies(
        body, sums, [_hbm_shape(s.shape, s.dtype) for s in sums],
        [pltpu.SemaphoreType.DMA((3 * n,)), pltpu.SemaphoreType.DMA((3 * n,)), pltpu.SemaphoreType.DMA((n,))],
        name=name, collective_id=collective_id)


def _sum_join(recv, *, name):
    _, R2, C = recv.shape
    rb = _pair_blocks(R2)
    nblk = R2 // rb

    def body(recv_ref, out_ref, in_buf, acc_buf, in_sem, loc_sem, send_sem, recv_sem):
        x, y, c, _ = _place()
        sibling = (x, y, 1 - c)
        _handshake([sibling])

        def fetch(t):
            return pltpu.make_async_copy(recv_ref.at[:, pl.ds(t * rb, rb), :], in_buf.at[t % 2], in_sem.at[t % 2])

        def rows(t, half):
            return out_ref.at[pl.ds(half * R2 + t * rb, rb), :]

        def put_local(t):
            return pltpu.make_async_copy(acc_buf.at[t % 2], rows(t, c), loc_sem.at[t])

        def put_remote(t, half):
            return pltpu.make_async_remote_copy(
                src_ref=acc_buf.at[t % 2], dst_ref=rows(t, half), send_sem=send_sem.at[t], recv_sem=recv_sem.at[t],
                device_id=sibling, device_id_type=MESH)

        fetch(0).start()
        for t in range(nblk):
            if t + 1 < nblk:
                fetch(t + 1).start()
            fetch(t).wait()
            if t >= 2:
                put_local(t - 2).wait()
                put_remote(t - 2, c).wait_send()
            acc = in_buf[t % 2, 0].astype(F32)
            for s in range(1, 4):
                acc = acc + in_buf[t % 2, s].astype(F32)
            acc_buf[t % 2] = acc
            put_local(t).start()
            put_remote(t, c).start()
        for t in range(max(nblk - 2, 0), nblk):
            put_local(t).wait()
            put_remote(t, c).wait_send()
        for t in range(nblk):
            put_remote(t, 1 - c).wait_recv()

    semn = pltpu.SemaphoreType.DMA((nblk,))
    return pl.pallas_call(
        body, name=name, in_specs=[ANY], out_specs=ANY, out_shape=_hbm_shape((2 * R2, C), F32),
        scratch_shapes=[pltpu.VMEM((2, 4, rb, C), BF16), pltpu.VMEM((2, rb, C), F32),
                        pltpu.SemaphoreType.DMA((2,)), semn, semn, semn],
        compiler_params=pltpu.CompilerParams(vmem_limit_bytes=VMEM_LIMIT, collective_id=ID_SIBLING),
    )(recv)


def _sum_small(pack):
    R, C = pack.shape

    def body(in_ref, out_ref, all_ref, send_sem, recv_sem):
        x, y, c, _ = _place()
        me = 4 * x + 2 * y + c
        all_ref[me] = in_ref[...]
        cps = []
        for k in range(1, 8):
            peer = (x ^ (k >> 2), y ^ ((k >> 1) & 1), c ^ (k & 1))
            cp = pltpu.make_async_remote_copy(
                src_ref=in_ref, dst_ref=all_ref.at[me], send_sem=send_sem.at[k - 1], recv_sem=recv_sem.at[k - 1],
                device_id=peer, device_id_type=MESH)
            cp.start()
            cps.append(cp)
        for k in range(1, 8):
            peer = (x ^ (k >> 2), y ^ ((k >> 1) & 1), c ^ (k & 1))
            slot = 4 * peer[0] + 2 * peer[1] + peer[2]
            pltpu.make_async_remote_copy(
                src_ref=in_ref, dst_ref=all_ref.at[slot], send_sem=send_sem.at[k - 1], recv_sem=recv_sem.at[k - 1],
                device_id=peer, device_id_type=MESH).wait_recv()
        for cp in cps:
            cp.wait_send()
        acc = all_ref[0]
        for d in range(1, 8):
            acc = acc + all_ref[d]
        out_ref[...] = acc

    return pl.pallas_call(
        body, name="sum_small", in_specs=[pl.BlockSpec(memory_space=pltpu.VMEM)],
        out_specs=pl.BlockSpec(memory_space=pltpu.VMEM), out_shape=jax.ShapeDtypeStruct((R, C), F32),
        scratch_shapes=[pltpu.VMEM((8, R, C), F32), pltpu.SemaphoreType.DMA((7,)), pltpu.SemaphoreType.DMA((7,))],
    )(pack)


ADAMW_BLOCK_BYTES = 2 * 1024 * 1024


def _adamw_block(R, C):
    padded = -(-C // 128) * 128
    rows = [rb for rb in range(8, R + 1, 8) if R % rb == 0 and rb * padded * 4 <= ADAMW_BLOCK_BYTES]
    if rows or R * padded * 4 <= ADAMW_BLOCK_BYTES:
        return (max(rows) if rows else R), C
    cols = [cb for cb in range(128, C + 1, 128) if C % cb == 0 and R * cb * 4 <= ADAMW_BLOCK_BYTES]
    return R, max(cols)


def _adamw(w, g, m, v, *, name, pass_grad=False):
    R, C = w.shape
    rb, cb = _adamw_block(R, C)
    c1 = 1.0 / (1.0 - B1 ** STEP)
    c2 = 1.0 / (1.0 - B2 ** STEP)

    def body(w_ref, g_ref, m_ref, v_ref, *outs):
        d_ref, nm_ref, nv_ref = outs[-3:]
        gv = g_ref[...]
        if pass_grad:
            outs[0][...] = gv
        m2 = B1 * m_ref[...] + (1.0 - B1) * gv
        v2 = B2 * v_ref[...] + (1.0 - B2) * gv * gv
        nm_ref[...] = m2
        nv_ref[...] = v2
        d_ref[...] = -LR * ((m2 * c1) / (jnp.sqrt(v2 * c2) + ADAM_EPS) + WD * w_ref[...])

    n_out = 4 if pass_grad else 3
    blk = pl.BlockSpec((rb, cb), lambda i, j: (i, j))
    return pl.pallas_call(
        body, name=name, grid=(R // rb, C // cb), in_specs=[blk] * 4, out_specs=[blk] * n_out,
        out_shape=[jax.ShapeDtypeStruct((R, C), F32)] * n_out, compiler_params=_params(2),
    )(w, g, m, v)


SMALL = (("norm1_w", D), ("norm2_w", D), ("final_norm_w", D), ("gate_b", GLA_H * DK), ("gla_norm_w", DV), ("sinks", N_Q))
PACK_W = 1024


def _pack_small(vals, w2, loss):
    rows = []
    for name, width in SMALL:
        v = vals[name].reshape(-1)
        rows.append(jnp.pad(v, (0, (-width) % PACK_W)).reshape(-1, PACK_W))
    rows.append(w2)
    rows.append(jnp.broadcast_to(loss.reshape(1, 1), (1, PACK_W)))
    pack = jnp.concatenate(rows, axis=0)
    return jnp.pad(pack, ((0, 32 - pack.shape[0]), (0, 0)))


def _unpack_small(pack):
    out, r = {}, 0
    for name, width in SMALL:
        nr = -(-width // PACK_W)
        out[name] = pack[r:r + nr].reshape(-1)[:width]
        r += nr
    out["w2"] = pack[r:r + RANK]
    out["loss"] = pack[r + RANK, 0]
    return out


def kernel(x, norm1_w, w_in, gla_gate_w2, gla_gate_b, attn_sinks, gla_norm_w, w_out, norm2_w, w_ffn_gate, w_ffn_up, w_ffn_down, final_norm_w, loss_target, m_norm1_w, m_w_in, m_gla_gate_w2, m_gla_gate_b, m_attn_sinks, m_gla_norm_w, m_w_out, m_norm2_w, m_w_ffn_gate, m_w_ffn_up, m_w_ffn_down, m_final_norm_w, v_norm1_w, v_w_in, v_gla_gate_w2, v_gla_gate_b, v_attn_sinks, v_gla_norm_w, v_w_out, v_norm2_w, v_w_ffn_gate, v_w_ffn_up, v_w_ffn_down, v_final_norm_w):
    chip = 2 * lax.axis_index("x") + lax.axis_index("y")
    w_in_s, w_out_s, w_gate_s, w_up_s, w_down_s, w2_s = (
        w_in[0], w_out[0], w_ffn_gate[0], w_ffn_up[0], w_ffn_down[0], gla_gate_w2[0])
    CS = SHARD_COLS

    g_in, g_w2 = _gather_staged(w_in_s.astype(BF16), w2_s)
    g_out, = _gather_shards([w_out_s.astype(BF16)], name="gather_out", collective_id=1, after=[g_w2])
    g_gate, g_up = _gather_shards([w_gate_s.astype(BF16), w_up_s.astype(BF16)], name="gather_gate_up",
                                  collective_id=7, after=[g_w2])
    g_down, = _gather_shards([w_down_s.astype(BF16)], name="gather_down", collective_id=8, after=[g_w2])
    w_main, w_lr = _w_in_to_main(g_in)
    w2_full = jnp.transpose(g_w2, (1, 0, 2)).reshape(RANK, GLA_H * DK)
    p = dict(
        norm1_w=norm1_w, norm2_w=norm2_w, final_norm_w=final_norm_w.reshape(1, D), gate_b=gla_gate_b,
        gla_norm_w=gla_norm_w, sinks=jnp.pad(attn_sinks, ((0, 0), (0, 128 - N_Q))),
        w_in_main=w_main, w_in_lr=w_lr,
        w2p=jnp.pad(w2_full, ((0, LRP - RANK), (0, 0))).astype(BF16),
        w_out=g_out.reshape(D, D),
        w_gate=g_gate, w_up=g_up, w_down=g_down.reshape(FFN, D),
    )

    tags = ["w_in", "w_out", "w_gate", "w_up", "w_down"]
    recv = {}

    def reduce_part(tag, part):
        sums = _pair_reduce(part, name="pair_reduce_" + tag)
        recv[tag], = _chip_exchange([sums], name="chip_exchange_" + tag, collective_id=2 + tags.index(tag))

    done = {}
    shard = {"w_out": (w_out_s, m_w_out[0], v_w_out[0]), "w_gate": (w_gate_s, m_w_ffn_gate[0], v_w_ffn_gate[0]),
             "w_up": (w_up_s, m_w_ffn_up[0], v_w_ffn_up[0]), "w_down": (w_down_s, m_w_ffn_down[0], v_w_ffn_down[0])}

    def finish(tag):
        grad = _sum_join(recv[tag], name="sum_join_" + tag)
        done[tag] = _adamw(shard[tag][0], grad, shard[tag][1], shard[tag][2], name="adamw_" + tag, pass_grad=True)

    loss_blk, dx, g = _local_step(x[0], loss_target[0], p, reduce_part)
    reduce_part("w_in", _main_to_shards(g["w_in_main"], g["w_in_lr"]))
    for tag in ("w_down", "w_gate", "w_up", "w_out"):
        finish(tag)
    g_in = _sum_join(recv["w_in"], name="sum_join_w_in")

    small = _unpack_small(_sum_small(_pack_small(g, g["w2"], loss_blk[0, 0])))
    loss = small["loss"]
    g_w2_mine = lax.dynamic_slice_in_dim(small["w2"], chip * (GLA_H * DK // 4), GLA_H * DK // 4, axis=1)

    grads = dict(
        norm1_w=small["norm1_w"].reshape(1, D), w_in=g_in, gla_gate_w2=g_w2_mine,
        gla_gate_b=small["gate_b"].reshape(1, -1), attn_sinks=small["sinks"].reshape(1, N_Q),
        gla_norm_w=small["gla_norm_w"].reshape(1, DV), norm2_w=small["norm2_w"].reshape(1, D),
        final_norm_w=small["final_norm_w"].reshape(1, D))
    def lin(a):
        return jnp.transpose(a.reshape(D, CS))

    def unlin(a):
        return jnp.transpose(a)

    grads["w_in"] = lin(g_in)
    weights = dict(
        norm1_w=(norm1_w, m_norm1_w, v_norm1_w), w_in=(lin(w_in), lin(m_w_in), lin(v_w_in)),
        gla_gate_w2=(w2_s, m_gla_gate_w2[0], v_gla_gate_w2[0]), gla_gate_b=(gla_gate_b, m_gla_gate_b, v_gla_gate_b),
        attn_sinks=(attn_sinks, m_attn_sinks, v_attn_sinks), gla_norm_w=(gla_norm_w, m_gla_norm_w, v_gla_norm_w),
        norm2_w=(norm2_w, m_norm2_w, v_norm2_w),
        final_norm_w=(final_norm_w.reshape(1, D), m_final_norm_w.reshape(1, D), v_final_norm_w.reshape(1, D)))
    names = ["norm1_w", "w_in", "gla_gate_w2", "gla_gate_b", "attn_sinks", "gla_norm_w", "w_out", "norm2_w",
             "w_ffn_gate", "w_ffn_up", "w_ffn_down", "final_norm_w"]
    lead = {"norm1_w": False, "gla_gate_b": False, "attn_sinks": False, "gla_norm_w": False, "norm2_w": False}
    g_out_l, d_out, m_out, v_out = [], [], [], []
    early = {"w_out": "w_out", "w_ffn_gate": "w_gate", "w_ffn_up": "w_up", "w_ffn_down": "w_down"}
    for nm in names:
        if nm in early:
            gr, dl, nmn, nvn = done[early[nm]]
            w = shard[early[nm]][0]
        else:
            w, m, v = weights[nm]
            gr = grads[nm]
            dl, nmn, nvn = _adamw(w, gr, m, v, name="adamw_" + nm)
        if nm == "w_in":
            gr, dl, nmn, nvn, w = g_in, unlin(dl), unlin(nmn), unlin(nvn), w_in_s
        if nm == "final_norm_w":
            shape = (D,)
        elif nm in lead:
            shape = w.shape
        else:
            shape = (1,) + w.shape
        g_out_l.append(gr.reshape(shape))
        d_out.append(dl.reshape(shape))
        m_out.append(nmn.reshape(shape))
        v_out.append(nvn.reshape(shape))
    return (loss, dx[None], *g_out_l, *d_out, *m_out, *v_out)
```

```python
import functools

import jax
import jax.numpy as jnp
from jax import lax
from jax.experimental import pallas as pl
from jax.experimental.pallas import tpu as pltpu
from jax.experimental.pallas import tpu_sc as plsc

F32 = jnp.float32
BF16 = jnp.bfloat16

D = 2048
HEAD_DIM = 64
N_Q = 32
N_KV = 4
GROUP = 8
WINDOW = 128
GLA_H = 4
DK = 256
DV = 512
RANK = 16
CHUNK = 64
FFN = 5632
EPS = 1e-6
MASK_VALUE = -1e30
GATE_NORM = 16.0
LR, B1, B2, ADAM_EPS, WD, STEP = 0.001, 0.9, 0.999, 1e-08, 0.01, 10

MAIN = 12800
O_MERGE, O_GLA, O_ATTN = 0, 6144, 10240
W_MERGE, W_GLA, W_ATTN = 3 * DV, 2 * DK + DV, D + 2 * N_KV * HEAD_DIM
LR_AT = 6656
LRP = 128


def _main_pieces():
    o_gq, o_gk, o_gv, o_gr, o_ga, o_gb = 2560, 3584, 4608, LR_AT + RANK, LR_AT + RANK + D, LR_AT + RANK + 2 * D
    pieces = []
    for h in range(GLA_H):
        pieces += [(o_gr + DV * h, DV), (o_ga + DV * h, DV), (o_gb + DV * h, DV)]
    for h in range(GLA_H):
        pieces += [(o_gq + DK * h, DK), (o_gk + DK * h, DK), (o_gv + DV * h, DV)]
    pieces.append((0, W_ATTN))
    return pieces


SHARD_COLS = 3204

VMEM_LIMIT = 56 * 1024 * 1024
MESH = pl.DeviceIdType.MESH
ANY = pl.BlockSpec(memory_space=pl.ANY)


def _params(ngrid):
    return pltpu.CompilerParams(dimension_semantics=("arbitrary",) * ngrid, vmem_limit_bytes=VMEM_LIMIT)


def _shard_segments(lo, hi):
    segs = []
    while lo < hi:
        s = lo // SHARD_COLS
        e = min(hi, (s + 1) * SHARD_COLS)
        segs.append((s, lo - s * SHARD_COLS, e - s * SHARD_COLS))
        lo = e
    return segs


def _lanes(pieces):
    return pieces[0] if len(pieces) == 1 else jnp.concatenate(pieces, axis=1)


def _w_in_to_main(shards):
    rb = 256

    def body(g_ref, main_ref, lr_ref):
        at = 0
        for a, w in _main_pieces():
            main_ref[:, at:at + w] = _lanes([g_ref[s, :, lo:hi] for s, lo, hi in _shard_segments(a, a + w)])
            at += w
        lr = [g_ref[s, :, lo:hi] for s, lo, hi in _shard_segments(LR_AT, LR_AT + RANK)]
        lr_ref[...] = _lanes(lr + [jnp.zeros((rb, LRP - RANK), lr_ref.dtype)])

    return pl.pallas_call(
        body, name="w_in_to_main", grid=(D // rb,),
        in_specs=[pl.BlockSpec((4, rb, SHARD_COLS), lambda i: (0, i, 0))],
        out_specs=[pl.BlockSpec((rb, MAIN), lambda i: (i, 0)), pl.BlockSpec((rb, LRP), lambda i: (i, 0))],
        out_shape=[jax.ShapeDtypeStruct((D, MAIN), shards.dtype), jax.ShapeDtypeStruct((D, LRP), shards.dtype)],
        compiler_params=_params(1),
    )(shards)


def _main_to_shards(g_main, g_lr):
    rb = 256
    at, sources = 0, [(LR_AT, RANK, 1, 0)]
    for a, w in _main_pieces():
        sources.append((a, w, 0, at))
        at += w
    sources.sort()

    def body(main_ref, lr_ref, out_ref):
        refs = (main_ref, lr_ref)
        for s in range(4):
            lo, hi = s * SHARD_COLS, (s + 1) * SHARD_COLS
            pieces = []
            for a, w, which, src in sources:
                b, e = max(a, lo), min(a + w, hi)
                if b < e:
                    pieces.append(refs[which][:, src + b - a:src + e - a])
            out_ref[s] = _lanes(pieces)

    return pl.pallas_call(
        body, name="main_to_shards", grid=(D // rb,),
        in_specs=[pl.BlockSpec((rb, MAIN), lambda i: (i, 0)), pl.BlockSpec((rb, LRP), lambda i: (i, 0))],
        out_specs=pl.BlockSpec((4, rb, SHARD_COLS), lambda i: (0, i, 0)),
        out_shape=jax.ShapeDtypeStruct((4, D, SHARD_COLS), g_main.dtype), compiler_params=_params(1),
    )(g_main, g_lr)


def _pick(n, cands):
    for c in cands:
        if n % c == 0:
            return c
    return n


def _sigmoid(x):
    return 1.0 / (1.0 + jnp.exp(-x))


def _dot(a, b, ca, cb):
    return lax.dot_general(a, b, (((ca,), (cb,)), ((), ())), preferred_element_type=F32)


def _matmul(a, b, *, name, ta=False, tb=False, add=None, out_dtype=F32, bm=None, bj=None, bc=None,
            b_slots=False, out_slots=False):
    C, M = a.shape if ta else a.shape[::-1]
    if b_slots:
        if tb:
            J, bc = b.shape[1], b.shape[2]
            assert b.shape[0] * bc == C
        else:
            bj = b.shape[2]
            J = b.shape[0] * bj
            assert b.shape[1] == C
    else:
        J = b.shape[0] if tb else b.shape[1]
        assert (b.shape[1] if tb else b.shape[0]) == C
    bm = bm or _pick(M, (1024, 512, 256, 128) if C <= FFN else (512, 256, 128))
    bj = bj or _pick(J, (1280, 1024, 512, 256, 128) if C <= D else (512, 256, 128))
    bc = bc or (C if C <= FFN else _pick(C, (6400,)))
    nm, nj, nc = M // bm, J // bj, C // bc
    has_add = add is not None

    def body(*refs):
        a_ref, b_ref = refs[0], refs[1]
        add_ref = refs[2] if has_add else None
        o_ref = refs[3] if has_add else refs[2]
        p = _dot(a_ref[...].astype(BF16), b_ref[...].astype(BF16), 0 if ta else 1, 1 if tb else 0)

        def finish(acc):
            if has_add:
                acc = acc + add_ref[...]
            o_ref[...] = acc.astype(o_ref.dtype)

        if nc == 1:
            finish(p)
        else:
            acc_ref = refs[-1]
            c = pl.program_id(2)

            @pl.when(c == 0)
            def _():
                acc_ref[...] = p

            @pl.when(c > 0)
            def _():
                acc_ref[...] += p

            @pl.when(c == nc - 1)
            def _():
                finish(acc_ref[...])

    a_spec = pl.BlockSpec((bc, bm), lambda m, j, c: (c, m)) if ta else pl.BlockSpec((bm, bc), lambda m, j, c: (m, c))
    if b_slots:
        b_spec = (pl.BlockSpec((None, bj, bc), lambda m, j, c: (c, j, 0)) if tb
                  else pl.BlockSpec((None, bc, bj), lambda m, j, c: (j, c, 0)))
    else:
        b_spec = (pl.BlockSpec((bj, bc), lambda m, j, c: (j, c)) if tb
                  else pl.BlockSpec((bc, bj), lambda m, j, c: (c, j)))
    if out_slots:
        assert not has_add
        o_spec = pl.BlockSpec((None, bm, bj), lambda m, j, c: (j, m, 0))
        out_shape = jax.ShapeDtypeStruct((nj, M, bj), out_dtype)
    else:
        o_spec = pl.BlockSpec((bm, bj), lambda m, j, c: (m, j))
        out_shape = jax.ShapeDtypeStruct((M, J), out_dtype)
    in_specs = [a_spec, b_spec] + ([o_spec] if has_add else [])
    args = (a, b) + ((add,) if has_add else ())
    return pl.pallas_call(
        body, name=name, grid=(nm, nj, nc), in_specs=in_specs, out_specs=o_spec,
        out_shape=out_shape,
        scratch_shapes=[pltpu.VMEM((bm, bj), F32)] if nc > 1 else [],
        compiler_params=_params(3),
    )(*args)


def _rows(T):
    return _pick(T, (256, 128))


def _rms_fwd(x, w, *, name):
    T = x.shape[0]
    rb = _rows(T)

    def body(x_ref, w_ref, u_ref, ut_ref):
        xv = x_ref[...]
        r = lax.rsqrt(jnp.mean(xv * xv, axis=-1, keepdims=True) + EPS)
        u = xv * r * w_ref[...]
        u_ref[...] = u.astype(BF16)
        ut_ref[...] = u.T.astype(BF16)

    return pl.pallas_call(
        body, name=name, grid=(T // rb,),
        in_specs=[pl.BlockSpec((rb, D), lambda i: (i, 0)), pl.BlockSpec((1, D), lambda i: (0, 0))],
        out_specs=[pl.BlockSpec((rb, D), lambda i: (i, 0)), pl.BlockSpec((D, rb), lambda i: (0, i))],
        out_shape=[jax.ShapeDtypeStruct((T, D), BF16), jax.ShapeDtypeStruct((D, T), BF16)], compiler_params=_params(1),
    )(x, w)


def _out_proj(merged, w_out, x, norm_w):
    T = merged.shape[0]
    bm = _pick(T, (512, 256))

    def body(a_ref, b_ref, x_ref, w_ref, h_ref, v_ref, vt_ref):
        h = _dot(a_ref[...], b_ref[...], 1, 0) + x_ref[...]
        h_ref[...] = h
        r = lax.rsqrt(jnp.mean(h * h, axis=-1, keepdims=True) + EPS)
        v2 = h * r * w_ref[...]
        v_ref[...] = v2.astype(BF16)
        vt_ref[...] = v2.T.astype(BF16)

    blk = pl.BlockSpec((bm, D), lambda i: (i, 0))
    return pl.pallas_call(
        body, name="out_proj", grid=(T // bm,),
        in_specs=[blk, pl.BlockSpec((D, D), lambda i: (0, 0)), blk, pl.BlockSpec((1, D), lambda i: (0, 0))],
        out_specs=[blk, blk, pl.BlockSpec((D, bm), lambda i: (0, i))],
        out_shape=[jax.ShapeDtypeStruct((T, D), F32), jax.ShapeDtypeStruct((T, D), BF16),
                   jax.ShapeDtypeStruct((D, T), BF16)],
        compiler_params=_params(1),
    )(merged, w_out, x, norm_w)


def _rms_bwd(dy, h, w, resid, *, name):
    T = h.shape[0]
    rb = _rows(T)

    def body(dy_ref, h_ref, w_ref, res_ref, dh_ref, dhb_ref, dw_ref):
        @pl.when(pl.program_id(0) == 0)
        def _():
            dw_ref[...] = jnp.zeros_like(dw_ref)

        hv = h_ref[...]
        r = lax.rsqrt(jnp.mean(hv * hv, axis=-1, keepdims=True) + EPS)
        hn = hv * r
        dyv = dy_ref[...]
        dw_ref[...] += jnp.sum(dyv * hn, axis=0, keepdims=True)
        t = dyv * w_ref[...]
        dh = res_ref[...] + r * (t - hn * jnp.mean(t * hn, axis=-1, keepdims=True))
        dh_ref[...] = dh
        dhb_ref[...] = dh.astype(BF16)

    blk = pl.BlockSpec((rb, D), lambda i: (i, 0))
    vec = pl.BlockSpec((1, D), lambda i: (0, 0))
    return pl.pallas_call(
        body, name=name, grid=(T // rb,), in_specs=[blk, blk, vec, blk], out_specs=[blk, blk, vec],
        out_shape=[jax.ShapeDtypeStruct((T, D), F32), jax.ShapeDtypeStruct((T, D), BF16),
                   jax.ShapeDtypeStruct((1, D), F32)],
        compiler_params=_params(1),
    )(dy, h, w, resid)


def _rms_bwd_input(du_main, dproj_lr, w_in_lr, x, w, resid):
    T = x.shape[0]
    rb = _rows(T)

    def body(du_ref, dl_ref, wl_ref, x_ref, w_ref, res_ref, dx_ref, dw_ref):
        @pl.when(pl.program_id(0) == 0)
        def _():
            dw_ref[...] = jnp.zeros_like(dw_ref)

        xv = x_ref[...]
        r = lax.rsqrt(jnp.mean(xv * xv, axis=-1, keepdims=True) + EPS)
        xn = xv * r
        du = du_ref[...] + _dot(dl_ref[...], wl_ref[...], 1, 1)
        dw_ref[...] += jnp.sum(du * xn, axis=0, keepdims=True)
        t = du * w_ref[...]
        dx_ref[...] = res_ref[...] + r * (t - xn * jnp.mean(t * xn, axis=-1, keepdims=True))

    blk = pl.BlockSpec((rb, D), lambda i: (i, 0))
    vec = pl.BlockSpec((1, D), lambda i: (0, 0))
    return pl.pallas_call(
        body, name="rms1_bwd", grid=(T // rb,),
        in_specs=[blk, pl.BlockSpec((rb, LRP), lambda i: (i, 0)), pl.BlockSpec((D, LRP), lambda i: (0, 0)), blk, vec,
                  blk],
        out_specs=[blk, vec], out_shape=[jax.ShapeDtypeStruct((T, D), F32), jax.ShapeDtypeStruct((1, D), F32)],
        compiler_params=_params(1),
    )(du_main, dproj_lr, w_in_lr, x, w, resid)


def _final_loss(h, w, target):
    T = h.shape[0]
    rb = _rows(T)

    def body(h_ref, w_ref, t_ref, loss_ref, dh_ref, dhb_ref, dw_ref):
        @pl.when(pl.program_id(0) == 0)
        def _():
            dw_ref[...] = jnp.zeros_like(dw_ref)
            loss_ref[...] = jnp.zeros_like(loss_ref)

        hv = h_ref[...]
        wv = w_ref[...]
        r = lax.rsqrt(jnp.mean(hv * hv, axis=-1, keepdims=True) + EPS)
        hn = hv * r
        e = hn * wv - t_ref[...]
        row = jnp.sum(e * e, axis=-1, keepdims=True) * (0.5 / D)
        loss_ref[...] += jnp.broadcast_to(jnp.sum(row, axis=0, keepdims=True), loss_ref.shape)
        dy = e * (1.0 / D)
        dw_ref[...] += jnp.sum(dy * hn, axis=0, keepdims=True)
        t = dy * wv
        dh = r * (t - hn * jnp.mean(t * hn, axis=-1, keepdims=True))
        dh_ref[...] = dh
        dhb_ref[...] = dh.astype(BF16)

    blk = pl.BlockSpec((rb, D), lambda i: (i, 0))
    vec = pl.BlockSpec((1, D), lambda i: (0, 0))
    return pl.pallas_call(
        body, name="final_loss", grid=(T // rb,), in_specs=[blk, vec, blk],
        out_specs=[pl.BlockSpec((8, 128), lambda i: (0, 0)), blk, blk, vec],
        out_shape=[jax.ShapeDtypeStruct((8, 128), F32), jax.ShapeDtypeStruct((T, D), F32),
                   jax.ShapeDtypeStruct((T, D), BF16), jax.ShapeDtypeStruct((1, D), F32)],
        compiler_params=_params(1),
    )(h, w, target)


PANEL = FFN // 4


def _ffn_fwd(v2, w_gate, w_up):
    T = v2.shape[0]
    bm = _pick(T, (512, 256))

    def body(a_ref, bg_ref, bu_ref, gate_ref, up_ref, ff_ref, fft_ref):
        a = a_ref[...]
        g = _dot(a, bg_ref[...], 1, 0)
        u = _dot(a, bu_ref[...], 1, 0)
        gate_ref[...] = g
        up_ref[...] = u
        ff = g * _sigmoid(g) * u
        ff_ref[...] = ff.astype(BF16)
        fft_ref[...] = ff.T.astype(BF16)

    panel = pl.BlockSpec((None, D, PANEL), lambda j, m: (j, 0, 0))
    out = pl.BlockSpec((bm, PANEL), lambda j, m: (m, j))
    return pl.pallas_call(
        body, name="ffn_fwd", grid=(4, T // bm),
        in_specs=[pl.BlockSpec((bm, D), lambda j, m: (m, 0)), panel, panel],
        out_specs=[out, out, out, pl.BlockSpec((PANEL, bm), lambda j, m: (j, m))],
        out_shape=[jax.ShapeDtypeStruct((T, FFN), F32), jax.ShapeDtypeStruct((T, FFN), F32),
                   jax.ShapeDtypeStruct((T, FFN), BF16), jax.ShapeDtypeStruct((FFN, T), BF16)],
        compiler_params=_params(2),
    )(v2, w_gate, w_up)


def _ffn_bwd_hidden(dh2, w_down, gate, up):
    T = dh2.shape[0]
    bm = _pick(T, (512, 256))

    def body(a_ref, b_ref, g_ref, u_ref, dg_ref, du_ref):
        d = _dot(a_ref[...], b_ref[...], 1, 1)
        g = g_ref[...]
        sg = _sigmoid(g)
        du_ref[...] = (d * g * sg).astype(BF16)
        dg_ref[...] = (d * u_ref[...] * sg * (1.0 + g * (1.0 - sg))).astype(BF16)

    blk = pl.BlockSpec((bm, PANEL), lambda j, m: (m, j))
    return pl.pallas_call(
        body, name="ffn_bwd_hidden", grid=(4, T // bm),
        in_specs=[pl.BlockSpec((bm, D), lambda j, m: (m, 0)), pl.BlockSpec((PANEL, D), lambda j, m: (j, 0)), blk, blk],
        out_specs=[blk, blk], out_shape=[jax.ShapeDtypeStruct((T, FFN), BF16)] * 2, compiler_params=_params(2),
    )(dh2, w_down, gate, up)


def _ffn_bwd_input(dgate, dup, w_gate, w_up):
    T = dgate.shape[0]
    bm = _pick(T, (1024, 512, 256))
    bj = 1024

    def body(ag_ref, au_ref, bg_ref, bu_ref, o_ref, acc_ref):
        c = pl.program_id(2)
        p = _dot(ag_ref[...], bg_ref[...], 1, 1) + _dot(au_ref[...], bu_ref[...], 1, 1)

        @pl.when(c == 0)
        def _():
            acc_ref[...] = p

        @pl.when(c > 0)
        def _():
            acc_ref[...] += p

        @pl.when(c == 3)
        def _():
            o_ref[...] = acc_ref[...]

    a = pl.BlockSpec((bm, PANEL), lambda m, j, c: (m, c))
    b = pl.BlockSpec((None, bj, PANEL), lambda m, j, c: (c, j, 0))
    return pl.pallas_call(
        body, name="ffn_bwd_input", grid=(T // bm, D // bj, 4), in_specs=[a, a, b, b],
        out_specs=pl.BlockSpec((bm, bj), lambda m, j, c: (m, j)), out_shape=jax.ShapeDtypeStruct((T, D), F32),
        scratch_shapes=[pltpu.VMEM((bm, bj), F32)], compiler_params=_params(3),
    )(dgate, dup, w_gate, w_up)


def _colsum(x, *, name):
    T, W = x.shape
    rb = _rows(T)

    def body(x_ref, o_ref):
        @pl.when(pl.program_id(0) == 0)
        def _():
            o_ref[...] = jnp.zeros_like(o_ref)

        o_ref[...] += jnp.sum(x_ref[...], axis=0, keepdims=True)

    return pl.pallas_call(
        body, name=name, grid=(T // rb,), in_specs=[pl.BlockSpec((rb, W), lambda i: (i, 0))],
        out_specs=pl.BlockSpec((1, W), lambda i: (0, 0)), out_shape=jax.ShapeDtypeStruct((1, W), F32),
        compiler_params=_params(1),
    )(x)


def _merge_fwd(attn_o, gla_raw, proj, gla_norm_w):
    T = attn_o.shape[0]
    rb = _rows(T)

    def body(a_ref, g_ref, gr_ref, ga_ref, gb_ref, w_ref, o_ref, ot_ref):
        graw = g_ref[...]
        r = lax.rsqrt(jnp.mean(graw * graw, axis=-1, keepdims=True) + EPS)
        gr = gr_ref[...]
        go = graw * r * w_ref[...] * (gr * _sigmoid(gr))
        merged = _sigmoid(ga_ref[...]) * a_ref[...] + _sigmoid(gb_ref[...]) * go
        o_ref[...] = merged.astype(BF16)
        ot_ref[...] = merged.T.astype(BF16)

    def sec(k):
        return pl.BlockSpec((rb, DV), lambda i, h: (i, O_MERGE // DV + 3 * h + k))

    blk = pl.BlockSpec((rb, DV), lambda i, h: (i, h))
    return pl.pallas_call(
        body, name="merge_fwd", grid=(T // rb, GLA_H),
        in_specs=[blk, blk, sec(0), sec(1), sec(2), pl.BlockSpec((1, DV), lambda i, h: (0, 0))],
        out_specs=[blk, pl.BlockSpec((DV, rb), lambda i, h: (h, i))],
        out_shape=[jax.ShapeDtypeStruct((T, D), BF16), jax.ShapeDtypeStruct((D, T), BF16)], compiler_params=_params(2),
    )(attn_o, gla_raw, proj, proj, proj, gla_norm_w)


def _merge_bwd(dm, attn_o, gla_raw, proj, gla_norm_w):
    T = attn_o.shape[0]
    rb = _rows(T)

    def body(dm_ref, a_ref, g_ref, gr_ref, ga_ref, gb_ref, w_ref, da_ref, dg_ref, dp_ref, dw_ref):
        @pl.when((pl.program_id(0) == 0) & (pl.program_id(1) == 0))
        def _():
            dw_ref[...] = jnp.zeros_like(dw_ref)

        dmv = dm_ref[...]
        av = a_ref[...]
        graw = g_ref[...]
        gr = gr_ref[...]
        wv = w_ref[...]
        sa = _sigmoid(ga_ref[...])
        sb = _sigmoid(gb_ref[...])
        r = lax.rsqrt(jnp.mean(graw * graw, axis=-1, keepdims=True) + EPS)
        gnh = graw * r
        gn = gnh * wv
        sr = _sigmoid(gr)
        sl = gr * sr
        go = gn * sl
        da_ref[...] = (dmv * sa).astype(BF16)
        dgo = dmv * sb
        dp_ref[:, 0:DV] = (dgo * gn * sr * (1.0 + gr * (1.0 - sr))).astype(BF16)
        dp_ref[:, DV:2 * DV] = (dmv * av * sa * (1.0 - sa)).astype(BF16)
        dp_ref[:, 2 * DV:3 * DV] = (dmv * go * sb * (1.0 - sb)).astype(BF16)
        dgn = dgo * sl
        dw_ref[...] += jnp.sum(dgn * gnh, axis=0, keepdims=True)
        t = dgn * wv
        dg_ref[...] = (r * (t - gnh * jnp.mean(t * gnh, axis=-1, keepdims=True))).astype(BF16)

    def sec(k):
        return pl.BlockSpec((rb, DV), lambda i, h: (i, O_MERGE // DV + 3 * h + k))

    blk = pl.BlockSpec((rb, DV), lambda i, h: (i, h))
    vec = pl.BlockSpec((1, DV), lambda i, h: (0, 0))
    return pl.pallas_call(
        body, name="merge_bwd", grid=(T // rb, GLA_H),
        in_specs=[blk, blk, blk, sec(0), sec(1), sec(2), vec],
        out_specs=[blk, blk, pl.BlockSpec((rb, W_MERGE), lambda i, h: (i, O_MERGE // W_MERGE + h)), vec],
        out_shape=[jax.ShapeDtypeStruct((T, D), BF16), jax.ShapeDtypeStruct((T, D), BF16),
                   jax.ShapeDtypeStruct((T, MAIN), BF16), jax.ShapeDtypeStruct((1, DV), F32)],
        compiler_params=_params(2),
    )(dm, attn_o, gla_raw, proj, proj, proj, gla_norm_w)


def _attn_band(n):
    qi = lax.broadcasted_iota(jnp.int32, (GROUP * WINDOW, WINDOW), 0) & (WINDOW - 1)
    kj = lax.broadcasted_iota(jnp.int32, (GROUP * WINDOW, WINDOW), 1)
    cur = kj <= qi
    return cur, cur | (n > 0)


def _stack_heads(ref, h, scale=1.0):
    rows = jnp.concatenate(
        [ref[:, (h * GROUP + g) * HEAD_DIM:(h * GROUP + g + 1) * HEAD_DIM] for g in range(GROUP)], axis=0)
    return (rows * scale).astype(BF16) if scale != 1.0 else rows.astype(BF16)


def _stack_sinks(s_ref, h):
    return jnp.concatenate(
        [jnp.broadcast_to(s_ref[:, h * GROUP + g:h * GROUP + g + 1], (WINDOW, 1)) for g in range(GROUP)], axis=0)


def _attn_probs(qs, kp, kc, band, sink):
    cur, live = band
    s = jnp.where(cur, _dot(qs, kc, 1, 1), _dot(qs, kp, 1, 1))
    s = jnp.where(live, s, MASK_VALUE)
    m = jnp.maximum(jnp.max(s, axis=-1, keepdims=True), sink)
    e = jnp.exp(s - m)
    es = jnp.exp(sink - m)
    inv = 1.0 / (jnp.sum(e, axis=-1, keepdims=True) + es)
    return e * inv, es * inv


def _unfold(cur, x):
    mine = jnp.where(cur, x, 0.0)
    return mine.astype(BF16), (x - mine).astype(BF16)


def _attn_specs(nb, rev):
    def at(n):
        return (nb - 1 - n) if rev else n

    kcol, vcol = (O_ATTN + D) // 256, (O_ATTN + D) // 256 + 1
    q = pl.BlockSpec((WINDOW, D), lambda n: (at(n), O_ATTN // D))
    kc = pl.BlockSpec((WINDOW, 256), lambda n: (at(n), kcol))
    kp = pl.BlockSpec((WINDOW, 256), lambda n: (jnp.maximum(at(n) - 1, 0), kcol))
    vc = pl.BlockSpec((WINDOW, 256), lambda n: (at(n), vcol))
    vp = pl.BlockSpec((WINDOW, 256), lambda n: (jnp.maximum(at(n) - 1, 0), vcol))
    sk = pl.BlockSpec((1, 128), lambda n: (0, 0))
    o = pl.BlockSpec((WINDOW, D), lambda n: (at(n), 0))
    return q, kc, kp, vc, vp, sk, o


def _attn_fwd(proj, sinks):
    T = proj.shape[0]
    nb = T // WINDOW

    def body(q_ref, kc_ref, kp_ref, vc_ref, vp_ref, s_ref, o_ref):
        band = _attn_band(pl.program_id(0))
        for h in range(N_KV):
            hs = slice(h * HEAD_DIM, (h + 1) * HEAD_DIM)
            p, _ = _attn_probs(_stack_heads(q_ref, h, HEAD_DIM ** -0.5), kp_ref[:, hs].astype(BF16),
                               kc_ref[:, hs].astype(BF16), band, _stack_sinks(s_ref, h))
            p_cur, p_prev = _unfold(band[0], p)
            o = _dot(p_cur, vc_ref[:, hs].astype(BF16), 1, 0) + _dot(p_prev, vp_ref[:, hs].astype(BF16), 1, 0)
            for g in range(GROUP):
                hg = h * GROUP + g
                o_ref[:, hg * HEAD_DIM:(hg + 1) * HEAD_DIM] = o[g * WINDOW:(g + 1) * WINDOW]

    q, kc, kp, vc, vp, sk, o = _attn_specs(nb, False)
    return pl.pallas_call(
        body, name="attn_fwd", grid=(nb,), in_specs=[q, kc, kp, vc, vp, sk], out_specs=o,
        out_shape=jax.ShapeDtypeStruct((T, D), F32), compiler_params=_params(1),
    )(proj, proj, proj, proj, proj, sinks)


def _attn_bwd(proj, sinks, d_o, dproj):
    T = proj.shape[0]
    nb = T // WINDOW
    kat, vat = D, D + N_KV * HEAD_DIM

    def body(q_ref, kc_ref, kp_ref, vc_ref, vp_ref, s_ref, do_ref, _, dp_ref, ds_ref, ck_ref, cv_ref):
        i = pl.program_id(0)
        n = nb - 1 - i

        @pl.when(i == 0)
        def _():
            ck_ref[...] = jnp.zeros_like(ck_ref)
            cv_ref[...] = jnp.zeros_like(cv_ref)
            ds_ref[...] = jnp.zeros_like(ds_ref)

        band = _attn_band(n)
        for h in range(N_KV):
            hs = slice(h * HEAD_DIM, (h + 1) * HEAD_DIM)
            kp, kc = kp_ref[:, hs].astype(BF16), kc_ref[:, hs].astype(BF16)
            vp, vc = vp_ref[:, hs].astype(BF16), vc_ref[:, hs].astype(BF16)
            qs = _stack_heads(q_ref, h, HEAD_DIM ** -0.5)
            do = _stack_heads(do_ref, h)
            p, ps = _attn_probs(qs, kp, kc, band, _stack_sinks(s_ref, h))
            dp = jnp.where(band[0], _dot(do, vc, 1, 1), _dot(do, vp, 1, 1))
            delta = jnp.sum(p * dp, axis=-1, keepdims=True)
            ds_cur, ds_prev = _unfold(band[0], p * (dp - delta))
            p_cur, p_prev = _unfold(band[0], p)
            dsink = -ps * delta
            dq = ((_dot(ds_cur, kc, 1, 0) + _dot(ds_prev, kp, 1, 0)) * (HEAD_DIM ** -0.5)).astype(BF16)
            for g in range(GROUP):
                hg = h * GROUP + g
                rows = slice(g * WINDOW, (g + 1) * WINDOW)
                ds_ref[hg:hg + 1, :] += jnp.broadcast_to(jnp.sum(dsink[rows], axis=0, keepdims=True), (1, 128))
                dp_ref[:, hg * HEAD_DIM:(hg + 1) * HEAD_DIM] = dq[rows]
            dp_ref[:, kat + h * HEAD_DIM:kat + (h + 1) * HEAD_DIM] = (_dot(ds_cur, qs, 0, 0) + ck_ref[:, hs]).astype(BF16)
            dp_ref[:, vat + h * HEAD_DIM:vat + (h + 1) * HEAD_DIM] = (_dot(p_cur, do, 0, 0) + cv_ref[:, hs]).astype(BF16)
            ck_ref[:, hs] = _dot(ds_prev, qs, 0, 0)
            cv_ref[:, hs] = _dot(p_prev, do, 0, 0)

    q, kc, kp, vc, vp, sk, o = _attn_specs(nb, True)
    return pl.pallas_call(
        body, name="attn_bwd", grid=(nb,), in_specs=[q, kc, kp, vc, vp, sk, o, ANY],
        out_specs=[pl.BlockSpec((WINDOW, W_ATTN), lambda n: (nb - 1 - n, O_ATTN // W_ATTN)),
                   pl.BlockSpec((N_Q, 128), lambda n: (0, 0))],
        out_shape=[jax.ShapeDtypeStruct((T, MAIN), BF16), jax.ShapeDtypeStruct((N_Q, 128), F32)],
        scratch_shapes=[pltpu.VMEM((WINDOW, 256), F32), pltpu.VMEM((WINDOW, 256), F32)],
        input_output_aliases={7: 0}, compiler_params=_params(1),
    )(proj, proj, proj, proj, proj, sinks, d_o, dproj)


def _split3(x):
    hi = x.astype(BF16)
    r1 = x - hi.astype(F32)
    mid = r1.astype(BF16)
    lo = (r1 - mid.astype(F32)).astype(BF16)
    return hi, mid, lo


def _tri_sum(tri, x):
    hi, mid, lo = _split3(x)
    return _dot(tri, hi, 1, 0) + _dot(tri, mid, 1, 0) + _dot(tri, lo, 1, 0)


def _chunk_masks(rb):
    r = lax.broadcasted_iota(jnp.int32, (rb, rb), 0)
    c = lax.broadcasted_iota(jnp.int32, (rb, rb), 1)
    same = (r ^ c) < CHUNK
    return same & (r >= c), same & (r <= c)


def _per_chunk_rows(x, per, pick):
    return jnp.concatenate(
        [jnp.broadcast_to(pick(x[i * CHUNK:(i + 1) * CHUNK]), (CHUNK, x.shape[1])) for i in range(per)], axis=0)


def _gla_block(q, k, lr, w2, b, lower):
    per = q.shape[0] // CHUNK
    logit = _dot(lr.astype(BF16), w2.astype(BF16), 1, 0) + b
    la = (jnp.minimum(logit, 0.0) - jnp.log(1.0 + jnp.exp(-jnp.abs(logit)))) * (1.0 / GATE_NORM)
    g = _tri_sum(lower, la)
    gl = _per_chunk_rows(g, per, lambda c: c[CHUNK - 1:CHUNK])
    eg = jnp.exp(g)
    qd = q * (DK ** -0.5) * eg
    ki = k * jnp.exp(-g)
    ke = k * jnp.exp(gl - g)
    return logit, g, gl, eg, qd, ki, ke


def _gla_rows(T, forward):
    return _pick(T, (512, 256, 128, 64) if forward else (256, 128, 64))


def _gla_fwd(proj, proj_lr, w2p, gate_b):
    T = proj.shape[0]
    rb = _gla_rows(T, True)
    per = rb // CHUNK

    def body(q_ref, k_ref, v_ref, lr_ref, w2_ref, b_ref, o_ref, st_ref, s_scr):
        @pl.when(pl.program_id(1) == 0)
        def _():
            s_scr[...] = jnp.zeros_like(s_scr)

        low, _ = _chunk_masks(rb)
        _, _, gl, _, qd, ki, ke = _gla_block(q_ref[...], k_ref[...], lr_ref[...], w2_ref[...], b_ref[...],
                                             low.astype(BF16))
        v = v_ref[...].astype(BF16)
        qdb, keb = qd.astype(BF16), ke.astype(BF16)
        att = jnp.where(low, _dot(qdb, ki.astype(BF16), 1, 1), 0.0).astype(BF16)
        o_intra = _dot(att, v, 1, 0)
        st = s_scr[...]
        for i in range(per):
            rows = slice(i * CHUNK, (i + 1) * CHUNK)
            st_ref[0, i] = st
            o_ref[rows, :] = o_intra[rows] + _dot(qdb[rows], st.astype(BF16), 1, 1)
            st = st * jnp.exp(gl[i * CHUNK:i * CHUNK + 1]) + _dot(v[rows], keb[rows], 0, 0)
        s_scr[...] = st

    return pl.pallas_call(
        body, name="gla_fwd", grid=(GLA_H, T // rb),
        in_specs=[pl.BlockSpec((rb, DK), lambda h, n: (n, (O_GLA + W_GLA * h) // DK)),
                  pl.BlockSpec((rb, DK), lambda h, n: (n, (O_GLA + W_GLA * h) // DK + 1)),
                  pl.BlockSpec((rb, DV), lambda h, n: (n, (O_GLA + W_GLA * h) // DV + 1)),
                  pl.BlockSpec((rb, LRP), lambda h, n: (n, 0)),
                  pl.BlockSpec((LRP, DK), lambda h, n: (0, h)),
                  pl.BlockSpec((1, DK), lambda h, n: (0, h))],
        out_specs=[pl.BlockSpec((rb, DV), lambda h, n: (n, h)),
                   pl.BlockSpec((1, per, DV, DK), lambda h, n: (h, n, 0, 0))],
        out_shape=[jax.ShapeDtypeStruct((T, GLA_H * DV), F32),
                   jax.ShapeDtypeStruct((GLA_H, T // CHUNK, DV, DK), F32)],
        scratch_shapes=[pltpu.VMEM((DV, DK), F32)], compiler_params=_params(2),
    )(proj, proj, proj, proj_lr, w2p, gate_b)


def _gla_bwd(proj, proj_lr, w2p, gate_b, states, d_o, dproj):
    T = proj.shape[0]
    rb = _gla_rows(T, False)
    per = rb // CHUNK
    nblk = T // rb

    def body(q_ref, k_ref, v_ref, lr_ref, w2_ref, b_ref, st_ref, do_ref, _, dp_ref, dl_ref, ds_scr):
        @pl.when(pl.program_id(1) == 0)
        def _():
            ds_scr[...] = jnp.zeros_like(ds_scr)

        low, up = _chunk_masks(rb)
        logit, g, gl, eg, qd, ki, ke = _gla_block(q_ref[...], k_ref[...], lr_ref[...], w2_ref[...], b_ref[...],
                                                  low.astype(BF16))
        v = v_ref[...].astype(BF16)
        do = do_ref[...].astype(BF16)
        qdb, kib, keb = qd.astype(BF16), ki.astype(BF16), ke.astype(BF16)
        att = jnp.where(low, _dot(qdb, kib, 1, 1), 0.0).astype(BF16)
        datt = jnp.where(low, _dot(do, v, 1, 1), 0.0).astype(BF16)
        dv_intra = _dot(att, do, 0, 0)
        dqd_intra = _dot(datt, kib, 1, 0)
        dki = _dot(datt, qdb, 0, 0)
        ds = ds_scr[...]
        dqd, dke, dgl = [None] * per, [None] * per, [None] * per
        for i in reversed(range(per)):
            rows = slice(i * CHUNK, (i + 1) * CHUNK)
            sp = st_ref[0, i]
            dsb = ds.astype(BF16)
            dp_ref[rows, 2 * DK:] = (dv_intra[rows] + _dot(keb[rows], dsb, 1, 1)).astype(BF16)
            dqd[i] = dqd_intra[rows] + _dot(do[rows], sp.astype(BF16), 1, 0)
            dke[i] = _dot(v[rows], dsb, 1, 0)
            decay = jnp.exp(gl[i * CHUNK:i * CHUNK + 1])
            dgl[i] = (jnp.sum(dke[i] * ke[rows], axis=0, keepdims=True)
                      + jnp.sum(ds * sp, axis=0, keepdims=True) * decay)
            ds = ds * decay + _dot(do[rows], qdb[rows], 0, 0)
        ds_scr[...] = ds
        dqd = jnp.concatenate(dqd, axis=0)
        dke = jnp.concatenate(dke, axis=0)
        dgl = jnp.concatenate([jnp.broadcast_to(d, (CHUNK, DK)) for d in dgl], axis=0)
        dp_ref[:, 0:DK] = (dqd * (DK ** -0.5) * eg).astype(BF16)
        dp_ref[:, DK:2 * DK] = (dki * jnp.exp(-g) + dke * jnp.exp(gl - g)).astype(BF16)
        dg = dqd * qd - dki * ki - dke * ke
        dla = _tri_sum(up.astype(BF16), dg) + dgl
        dl_ref[...] = dla * (1.0 / GATE_NORM) * (1.0 - _sigmoid(logit))

    def rev(n):
        return nblk - 1 - n

    return pl.pallas_call(
        body, name="gla_bwd", grid=(GLA_H, nblk),
        in_specs=[pl.BlockSpec((rb, DK), lambda h, n: (rev(n), (O_GLA + W_GLA * h) // DK)),
                  pl.BlockSpec((rb, DK), lambda h, n: (rev(n), (O_GLA + W_GLA * h) // DK + 1)),
                  pl.BlockSpec((rb, DV), lambda h, n: (rev(n), (O_GLA + W_GLA * h) // DV + 1)),
                  pl.BlockSpec((rb, LRP), lambda h, n: (rev(n), 0)),
                  pl.BlockSpec((LRP, DK), lambda h, n: (0, h)),
                  pl.BlockSpec((1, DK), lambda h, n: (0, h)),
                  pl.BlockSpec((1, per, DV, DK), lambda h, n: (h, rev(n), 0, 0)),
                  pl.BlockSpec((rb, DV), lambda h, n: (rev(n), h)),
                  ANY],
        out_specs=[pl.BlockSpec((rb, W_GLA), lambda h, n: (rev(n), O_GLA // W_GLA + h)),
                   pl.BlockSpec((rb, DK), lambda h, n: (rev(n), h))],
        out_shape=[jax.ShapeDtypeStruct((T, MAIN), BF16), jax.ShapeDtypeStruct((T, GLA_H * DK), F32)],
        scratch_shapes=[pltpu.VMEM((DV, DK), F32)], input_output_aliases={8: 0}, compiler_params=_params(2),
    )(proj, proj, proj, proj_lr, w2p, gate_b, states, d_o, dproj)


def _after(values, tokens):
    return lax.optimization_barrier((values, tokens))[0]


def _local_step(x, target, p, reduce_part, reduce_ffn):
    u, u_t = _rms_fwd(x, p["norm1_w"], name="rms1_fwd")
    whole = _pick(x.shape[0], (2048, 256))
    proj = _matmul(u, p["w_in_main"], bm=whole, name="mm_proj")
    proj_lr = _matmul(u, p["w_in_lr"], name="mm_proj_lr")
    attn_o = _attn_fwd(proj, p["sinks"])
    gla_raw, states = _gla_fwd(proj, proj_lr, p["w2p"], p["gate_b"])
    merged, merged_t = _merge_fwd(attn_o, gla_raw, proj, p["gla_norm_w"])
    h1, v2, v2_t = _out_proj(merged, p["w_out"], x, p["norm2_w"])
    gate, up, ff, ff_t = _ffn_fwd(v2, p["w_gate"], p["w_up"])
    h2 = _matmul(ff, p["w_down"], add=h1, name="mm_down")
    loss, dh2, dh2_b, g_final = _final_loss(h2, p["final_norm_w"], target)

    dgate, dup = _ffn_bwd_hidden(dh2_b, p["w_down"], gate, up)
    sent = [reduce_ffn("w_down", ff_t, dh2_b), reduce_ffn("w_gate", v2_t, dgate), reduce_ffn("w_up", v2_t, dup)]
    dgate, dup = _after((dgate, dup), sent)
    dv2 = _ffn_bwd_input(dgate, dup, p["w_gate"], p["w_up"])
    dh1, dh1_b, g_norm2 = _rms_bwd(dv2, h1, p["norm2_w"], dh2, name="rms2_bwd")
    dmerged = _matmul(dh1_b, p["w_out"], tb=True, bj=D, name="mm_dmerged")
    sent = reduce_part("w_out", _matmul(merged_t, dh1_b, out_dtype=BF16, bj=D, name="mm_gout").reshape(4, D // 4, D))
    dmerged = _after(dmerged, sent)
    d_attn, d_gla, dproj, g_gla_norm = _merge_bwd(dmerged, attn_o, gla_raw, proj, p["gla_norm_w"])
    dproj, dlogit = _gla_bwd(proj, proj_lr, p["w2p"], p["gate_b"], states, d_gla, dproj)
    dproj, g_sinks = _attn_bwd(proj, p["sinks"], d_attn, dproj)
    g_gate_b = _colsum(dlogit, name="colsum_gate_b")
    g_w2 = _matmul(proj_lr, dlogit, ta=True, name="mm_gw2")
    dproj_lr = _matmul(dlogit, p["w2p"], tb=True, out_dtype=BF16, name="mm_dlr")
    g_in_main = _matmul(u_t, dproj, out_dtype=BF16, bm=D, name="mm_gin")
    g_in_lr = _matmul(u_t, dproj_lr, out_dtype=BF16, name="mm_gin_lr")
    du = _matmul(dproj, p["w_in_main"], tb=True, bm=_pick(x.shape[0], (1024, 256)), bj=1024, bc=3200, name="mm_du")
    dx, g_norm1 = _rms_bwd_input(du, dproj_lr, p["w_in_lr"], x, p["norm1_w"], dh1)
    grads = dict(norm1_w=g_norm1, w_in_main=g_in_main, w_in_lr=g_in_lr, w2=g_w2[:RANK], gate_b=g_gate_b,
                 sinks=g_sinks[:, 0].reshape(1, N_Q), gla_norm_w=g_gla_norm, norm2_w=g_norm2, final_norm_w=g_final)
    return loss, dx, grads


def _place():
    x, y, c = lax.axis_index("x"), lax.axis_index("y"), lax.axis_index("c")
    chips = [(1 - x, y), (x, 1 - y), (1 - x, 1 - y)]
    return x, y, c, chips


def _hbm_shape(s, dt):
    return jax.ShapeDtypeStruct(s, dt)


ID_SIBLING = 9
ID_GATHER_IN = 10


def _handshake(peers):
    barrier = pltpu.get_barrier_semaphore()
    for peer in peers:
        pl.semaphore_signal(barrier, inc=1, device_id=peer, device_id_type=MESH)
    pl.semaphore_wait(barrier, len(peers))


def _launch_copies(body, args, out_shape, sems, *, name, collective_id=None):
    if collective_id is None:
        return pl.pallas_call(
            body, name=name, in_specs=[ANY] * len(args), out_specs=[ANY] * len(out_shape), out_shape=out_shape,
            scratch_shapes=sems)(*args)
    return pl.kernel(
        body, name=name, out_type=out_shape, mesh=plsc.ScalarSubcoreMesh(axis_name="sequencer", num_cores=1),
        scratch_types=sems, compiler_params=pltpu.CompilerParams(collective_id=collective_id))(*args)


def _gather_shards(shards, *, name, collective_id=None, after=()):
    n = len(shards)
    first_out = n + len(after)

    def body(*refs):
        ins, outs = refs[:n], refs[first_out:first_out + n]
        ici_send, ici_recv, d2d_send, d2d_recv, local_sem = refs[first_out + n:]
        x, y, c, chips = _place()
        me = 2 * x + y
        sibling = (x, y, 1 - c)
        if collective_id is not None:
            _handshake([sibling] + [(*chip, c) for chip in chips])

        def half(w, slot, hc):
            r2 = shards[w].shape[0] // 2
            return outs[w].at[slot, pl.ds(hc * r2, r2), :]

        locals_ = [pltpu.make_async_copy(ins[w], outs[w].at[me], local_sem.at[w]) for w in range(n)]
        for cp in locals_:
            cp.start()
        sends = []
        for w in range(n):
            r2 = shards[w].shape[0] // 2
            for j, chip in enumerate(chips):
                cp = pltpu.make_async_remote_copy(
                    src_ref=ins[w].at[pl.ds(c * r2, r2), :], dst_ref=half(w, me, c),
                    send_sem=ici_send.at[w * 3 + j], recv_sem=ici_recv.at[w * 3 + j],
                    device_id=(*chip, c), device_id_type=MESH)
                cp.start()
                sends.append(cp)
        for w in range(n):
            for j, chip in enumerate(chips):
                slot = 2 * chip[0] + chip[1]
                got = half(w, slot, c)
                pltpu.make_async_remote_copy(
                    src_ref=got, dst_ref=got, send_sem=ici_send.at[w * 3 + j], recv_sem=ici_recv.at[w * 3 + j],
                    device_id=(*chip, c), device_id_type=MESH).wait_recv()
                cp = pltpu.make_async_remote_copy(
                    src_ref=got, dst_ref=got, send_sem=d2d_send.at[w * 3 + j], recv_sem=d2d_recv.at[w * 3 + j],
                    device_id=sibling, device_id_type=MESH)
                cp.start()
                sends.append(cp)
        for w in range(n):
            for j, chip in enumerate(chips):
                slot = 2 * chip[0] + chip[1]
                got = half(w, slot, 1 - c)
                pltpu.make_async_remote_copy(
                    src_ref=got, dst_ref=got, send_sem=d2d_send.at[w * 3 + j], recv_sem=d2d_recv.at[w * 3 + j],
                    device_id=sibling, device_id_type=MESH).wait_recv()
        for cp in sends:
            cp.wait_send()
        for cp in locals_:
            cp.wait()

    return _launch_copies(
        body, list(shards) + list(after), [_hbm_shape((4,) + s.shape, s.dtype) for s in shards],
        [pltpu.SemaphoreType.DMA((3 * n,)), pltpu.SemaphoreType.DMA((3 * n,)), pltpu.SemaphoreType.DMA((3 * n,)),
         pltpu.SemaphoreType.DMA((3 * n,)), pltpu.SemaphoreType.DMA((n,))],
        name=name, collective_id=collective_id)


def _gather_staged(w, small):
    R, C = w.shape
    R2 = R // 2
    rb = 256
    per = R2 // rb
    r2 = small.shape[0] // 2

    def body(w_ref, s_ref, out_ref, outs_ref, stage, ici_send, ici_recv, fwd_send, fwd_recv, in_sem, put_sem,
             push_send, push_recv):
        x, y, c, chips = _place()
        me = 2 * x + y
        sibling = (x, y, 1 - c)
        slots = [2 * chip[0] + chip[1] for chip in chips]
        _handshake([sibling] + [(*chip, c) for chip in chips])

        def ici(j, src, dst, k):
            return pltpu.make_async_remote_copy(
                src_ref=src, dst_ref=dst, send_sem=ici_send.at[k], recv_sem=ici_recv.at[k],
                device_id=(*chips[j], c), device_id_type=MESH)

        sends = []
        for j in range(3):
            sends.append(ici(j, w_ref.at[pl.ds(c * R2, R2), :], out_ref.at[me, pl.ds(c * R2, R2), :], j))
            sends.append(ici(j, s_ref.at[pl.ds(c * r2, r2), :], outs_ref.at[me, pl.ds(c * r2, r2), :], 3 + j))
        for cp in sends:
            cp.start()
        mine = pltpu.make_async_copy(s_ref, outs_ref.at[me], put_sem.at[2])
        mine.start()

        def own(k, to_vmem):
            rows = pl.ds(k * rb, rb)
            if to_vmem:
                return pltpu.make_async_copy(w_ref.at[rows, :], stage.at[k % 2], in_sem.at[k % 2])
            return pltpu.make_async_copy(stage.at[k % 2], out_ref.at[me, rows, :], put_sem.at[k % 2])

        for k in range(R // rb):
            if k >= 2:
                own(k - 2, False).wait()
            own(k, True).start()
            own(k, True).wait()
            own(k, False).start()
        for k in range(R // rb - 2, R // rb):
            own(k, False).wait()

        def block(j, k, hc):
            return out_ref.at[slots[j], pl.ds(hc * R2 + k * rb, rb), :]

        def push(t, hc):
            j, k = divmod(t, per)
            return pltpu.make_async_remote_copy(
                src_ref=stage.at[t % 2], dst_ref=block(j, k, hc), send_sem=push_send.at[t % 2],
                recv_sem=push_recv.at[t], device_id=sibling, device_id_type=MESH)

        passed = []
        for j in range(3):
            ici(j, w_ref.at[pl.ds(c * R2, R2), :], out_ref.at[slots[j], pl.ds(c * R2, R2), :], j).wait_recv()
            for k in range(per):
                t = j * per + k
                if t >= 2:
                    push(t - 2, c).wait_send()
                fetch = pltpu.make_async_copy(block(j, k, c), stage.at[t % 2], in_sem.at[t % 2])
                fetch.start()
                fetch.wait()
                push(t, c).start()
            got = outs_ref.at[slots[j], pl.ds(c * r2, r2), :]
            ici(j, got, got, 3 + j).wait_recv()
            fw = pltpu.make_async_remote_copy(
                src_ref=got, dst_ref=got, send_sem=fwd_send.at[j], recv_sem=fwd_recv.at[j],
                device_id=sibling, device_id_type=MESH)
            fw.start()
            passed.append(fw)
        for t in range(3 * per - 2, 3 * per):
            push(t, c).wait_send()
        for t in range(3 * per):
            push(t, 1 - c).wait_recv()
        for j in range(3):
            theirs = outs_ref.at[slots[j], pl.ds((1 - c) * r2, r2), :]
            pltpu.make_async_remote_copy(
                src_ref=theirs, dst_ref=theirs, send_sem=fwd_send.at[j], recv_sem=fwd_recv.at[j],
                device_id=sibling, device_id_type=MESH).wait_recv()
        for cp in sends + passed:
            cp.wait_send()
        mine.wait()

    dma = pltpu.SemaphoreType.DMA
    return pl.pallas_call(
        body, name="gather_in", in_specs=[ANY, ANY], out_specs=[ANY, ANY],
        out_shape=[_hbm_shape((4, R, C), w.dtype), _hbm_shape((4,) + small.shape, small.dtype)],
        scratch_shapes=[pltpu.VMEM((2, rb, C), w.dtype), dma((6,)), dma((6,)), dma((3,)), dma((3,)), dma((2,)),
                        dma((3,)), dma((2,)), dma((3 * per,))],
        compiler_params=pltpu.CompilerParams(vmem_limit_bytes=VMEM_LIMIT, collective_id=ID_GATHER_IN),
    )(w, small)


def _pair_blocks(R2):
    return _pick(R2, (256, 352, 128, 64, 32, 16))


def _grad_matmul_split(a, b, *, by_rows, name):
    M, T = a.shape
    J = b.shape[1]
    if by_rows:
        R, C, bm = M // 4, J, 352
        per = R // 2 // bm
        n = M // bm

        def where(t):
            return t // (2 * per), (t // per) % 2, t % per

        a_spec = pl.BlockSpec((bm, T), lambda t: (t, 0))
        b_spec = pl.BlockSpec((T, C), lambda t: (0, 0))
    else:
        R, C, bm = M, J // 4, M // 2
        per, n = 1, 8

        def where(t):
            return t // 2, t % 2, 0

        a_spec = pl.BlockSpec((bm, T), lambda t: (t % 2, 0))
        b_spec = pl.BlockSpec((T, C), lambda t: (0, t // 2))
    R2 = R // 2

    def body(a_ref, b_ref, mine_ref, got_ref, tile, loc_sem, snd_sem, rcv_sem):
        x, y, c, _ = _place()
        sibling = (x, y, 1 - c)
        t = pl.program_id(0)

        @pl.when(t == 0)
        def _():
            _handshake([sibling])

        def keep(u):
            s, _, i = where(u)
            return pltpu.make_async_copy(tile.at[u % 2], mine_ref.at[s, pl.ds(i * bm, bm), :], loc_sem.at[u % 2])

        def give(u):
            s, _, i = where(u)
            return pltpu.make_async_remote_copy(
                src_ref=tile.at[u % 2], dst_ref=got_ref.at[s, pl.ds(i * bm, bm), :], send_sem=snd_sem.at[u % 2],
                recv_sem=rcv_sem, device_id=sibling, device_id_type=MESH)

        def settle(u):
            @pl.when(where(u)[1] == c)
            def _():
                keep(u).wait()

            @pl.when(where(u)[1] != c)
            def _():
                give(u).wait_send()

        @pl.when(t >= 2)
        def _():
            settle(t - 2)

        tile[t % 2] = _dot(a_ref[...], b_ref[...], 1, 0).astype(BF16)

        @pl.when(where(t)[1] == c)
        def _():
            keep(t).start()

        @pl.when(where(t)[1] != c)
        def _():
            give(t).start()

        @pl.when(t == n - 1)
        def _():
            settle(t - 1)
            settle(t)
            pltpu.make_async_remote_copy(
                src_ref=got_ref, dst_ref=got_ref, send_sem=snd_sem.at[0], recv_sem=rcv_sem,
                device_id=sibling, device_id_type=MESH).wait_recv()

    half = _hbm_shape((4, R2, C), BF16)
    return pl.pallas_call(
        body, name=name, grid=(n,), in_specs=[a_spec, b_spec], out_specs=[ANY, ANY], out_shape=[half, half],
        scratch_shapes=[pltpu.VMEM((2, bm, C), BF16), pltpu.SemaphoreType.DMA((2,)), pltpu.SemaphoreType.DMA((2,)),
                        pltpu.SemaphoreType.DMA],
        compiler_params=pltpu.CompilerParams(
            dimension_semantics=("arbitrary",), vmem_limit_bytes=VMEM_LIMIT, collective_id=ID_SIBLING),
    )(a, b)


def _pair_add(mine, got, *, name):
    _, R2, C = mine.shape
    rb = _pick(R2, (512, 352, 256))

    def body(a_ref, b_ref, o_ref):
        o_ref[...] = (a_ref[...].astype(F32) + b_ref[...].astype(F32)).astype(BF16)

    blk = pl.BlockSpec((1, rb, C), lambda s, i: (s, i, 0))
    return pl.pallas_call(
        body, name=name, grid=(4, R2 // rb), in_specs=[blk, blk], out_specs=blk,
        out_shape=jax.ShapeDtypeStruct(mine.shape, BF16), compiler_params=_params(2),
    )(mine, got)


def _pair_reduce(part, *, name):
    _, R, C = part.shape
    R2 = R // 2
    rb = _pick(R2, (512, 352, 256, 128, 64, 32, 16))
    nblk = R2 // rb
    steps = [(s, i) for s in range(4) for i in range(nblk)]
    n = len(steps)

    def body(part_ref, sums_ref, own_buf, oth_buf, rcv_buf, out_buf, own_sem, oth_sem, out_sem, send_sem, recv_sem):
        x, y, c, _ = _place()
        sibling = (x, y, 1 - c)
        _handshake([sibling])

        def fetch(t, half, buf, sem):
            s, i = steps[t]
            return pltpu.make_async_copy(part_ref.at[s, pl.ds(half * R2 + i * rb, rb), :], buf.at[t % 3], sem.at[t % 3])

        def push(t):
            return pltpu.make_async_remote_copy(
                src_ref=oth_buf.at[t % 3], dst_ref=rcv_buf.at[t % 4], send_sem=send_sem.at[t % 4],
                recv_sem=recv_sem.at[t % 4], device_id=sibling, device_id_type=MESH)

        def store(t):
            s, i = steps[t]
            return pltpu.make_async_copy(out_buf.at[t % 2], sums_ref.at[s, pl.ds(i * rb, rb), :], out_sem.at[t % 2])

        for t in range(min(2, n)):
            fetch(t, c, own_buf, own_sem).start()
            fetch(t, 1 - c, oth_buf, oth_sem).start()
        fetch(0, 1 - c, oth_buf, oth_sem).wait()
        push(0).start()
        for t in range(n):
            if t + 2 < n:
                fetch(t + 2, c, own_buf, own_sem).start()
                fetch(t + 2, 1 - c, oth_buf, oth_sem).start()
            if t + 1 < n:
                fetch(t + 1, 1 - c, oth_buf, oth_sem).wait()
                push(t + 1).start()
            fetch(t, c, own_buf, own_sem).wait()
            push(t).wait()
            if t >= 2:
                store(t - 2).wait()
            out_buf[t % 2] = (own_buf[t % 3].astype(F32) + rcv_buf[t % 4].astype(F32)).astype(BF16)
            store(t).start()
        for t in range(max(n - 2, 0), n):
            store(t).wait()

    def buf(k):
        return pltpu.VMEM((k, rb, C), BF16)

    def sems(k):
        return pltpu.SemaphoreType.DMA((k,))

    return pl.pallas_call(
        body, name=name, in_specs=[ANY], out_specs=ANY, out_shape=_hbm_shape((4, R2, C), BF16),
        scratch_shapes=[buf(3), buf(3), buf(4), buf(2), sems(3), sems(3), sems(2), sems(4), sems(4)],
        compiler_params=pltpu.CompilerParams(vmem_limit_bytes=VMEM_LIMIT, collective_id=ID_SIBLING),
    )(part)


def _chip_exchange(sums, *, name, collective_id=None):
    n = len(sums)

    def body(*refs):
        ins, outs = refs[:n], refs[n:2 * n]
        send_sem, recv_sem, local_sem = refs[2 * n:]
        x, y, c, chips = _place()
        me = 2 * x + y
        if collective_id is not None:
            _handshake([(*chip, c) for chip in chips])
        cps = []
        for w in range(n):
            lc = pltpu.make_async_copy(ins[w].at[me], outs[w].at[me], local_sem.at[w])
            lc.start()
            cps.append(lc)
            for j, chip in enumerate(chips):
                slot = 2 * chip[0] + chip[1]
                rc = pltpu.make_async_remote_copy(
                    src_ref=ins[w].at[slot], dst_ref=outs[w].at[me],
                    send_sem=send_sem.at[w * 3 + j], recv_sem=recv_sem.at[w * 3 + j],
                    device_id=(*chip, c), device_id_type=MESH)
                rc.start()
                cps.append(rc)
        for w in range(n):
            for j, chip in enumerate(chips):
                slot = 2 * chip[0] + chip[1]
                pltpu.make_async_remote_copy(
                    src_ref=ins[w].at[slot], dst_ref=outs[w].at[slot],
                    send_sem=send_sem.at[w * 3 + j], recv_sem=recv_sem.at[w * 3 + j],
                    device_id=(*chip, c), device_id_type=MESH).wait_recv()
        for w in range(n):
            cps[w * 4].wait()
            for j in range(3):
                cps[w * 4 + 1 + j].wait_send()

    return _launch_copies(
        body, sums, [_hbm_shape(s.shape, s.dtype) for s in sums],
        [pltpu.SemaphoreType.DMA((3 * n,)), pltpu.SemaphoreType.DMA((3 * n,)), pltpu.SemaphoreType.DMA((n,))],
        name=name, collective_id=collective_id)


def _sum_join(recv, *, name):
    _, R2, C = recv.shape
    rb = _pair_blocks(R2)
    nblk = R2 // rb

    def body(recv_ref, out_ref, in_buf, acc_buf, in_sem, loc_sem, send_sem, recv_sem):
        x, y, c, _ = _place()
        sibling = (x, y, 1 - c)
        _handshake([sibling])

        def fetch(t):
            return pltpu.make_async_copy(recv_ref.at[:, pl.ds(t * rb, rb), :], in_buf.at[t % 2], in_sem.at[t % 2])

        def rows(t, half):
            return out_ref.at[pl.ds(half * R2 + t * rb, rb), :]

        def put_local(t):
            return pltpu.make_async_copy(acc_buf.at[t % 2], rows(t, c), loc_sem.at[t])

        def put_remote(t, half):
            return pltpu.make_async_remote_copy(
                src_ref=acc_buf.at[t % 2], dst_ref=rows(t, half), send_sem=send_sem.at[t], recv_sem=recv_sem.at[t],
                device_id=sibling, device_id_type=MESH)

        fetch(0).start()
        for t in range(nblk):
            if t + 1 < nblk:
                fetch(t + 1).start()
            fetch(t).wait()
            if t >= 2:
                put_local(t - 2).wait()
                put_remote(t - 2, c).wait_send()
            acc = in_buf[t % 2, 0].astype(F32)
            for s in range(1, 4):
                acc = acc + in_buf[t % 2, s].astype(F32)
            acc_buf[t % 2] = acc
            put_local(t).start()
            put_remote(t, c).start()
        for t in range(max(nblk - 2, 0), nblk):
            put_local(t).wait()
            put_remote(t, c).wait_send()
        for t in range(nblk):
            put_remote(t, 1 - c).wait_recv()

    semn = pltpu.SemaphoreType.DMA((nblk,))
    return pl.pallas_call(
        body, name=name, in_specs=[ANY], out_specs=ANY, out_shape=_hbm_shape((2 * R2, C), F32),
        scratch_shapes=[pltpu.VMEM((2, 4, rb, C), BF16), pltpu.VMEM((2, rb, C), F32),
                        pltpu.SemaphoreType.DMA((2,)), semn, semn, semn],
        compiler_params=pltpu.CompilerParams(vmem_limit_bytes=VMEM_LIMIT, collective_id=ID_SIBLING),
    )(recv)


def _sum_small(pack):
    R, C = pack.shape

    def body(in_ref, out_ref, all_ref, send_sem, recv_sem):
        x, y, c, _ = _place()
        me = 4 * x + 2 * y + c
        all_ref[me] = in_ref[...]
        cps = []
        for k in range(1, 8):
            peer = (x ^ (k >> 2), y ^ ((k >> 1) & 1), c ^ (k & 1))
            cp = pltpu.make_async_remote_copy(
                src_ref=in_ref, dst_ref=all_ref.at[me], send_sem=send_sem.at[k - 1], recv_sem=recv_sem.at[k - 1],
                device_id=peer, device_id_type=MESH)
            cp.start()
            cps.append(cp)
        for k in range(1, 8):
            peer = (x ^ (k >> 2), y ^ ((k >> 1) & 1), c ^ (k & 1))
            slot = 4 * peer[0] + 2 * peer[1] + peer[2]
            pltpu.make_async_remote_copy(
                src_ref=in_ref, dst_ref=all_ref.at[slot], send_sem=send_sem.at[k - 1], recv_sem=recv_sem.at[k - 1],
                device_id=peer, device_id_type=MESH).wait_recv()
        for cp in cps:
            cp.wait_send()
        acc = all_ref[0]
        for d in range(1, 8):
            acc = acc + all_ref[d]
        out_ref[...] = acc

    return pl.pallas_call(
        body, name="sum_small", in_specs=[pl.BlockSpec(memory_space=pltpu.VMEM)],
        out_specs=pl.BlockSpec(memory_space=pltpu.VMEM), out_shape=jax.ShapeDtypeStruct((R, C), F32),
        scratch_shapes=[pltpu.VMEM((8, R, C), F32), pltpu.SemaphoreType.DMA((7,)), pltpu.SemaphoreType.DMA((7,))],
    )(pack)


ADAMW_BLOCK_BYTES = 2 * 1024 * 1024


def _adamw_block(R, C):
    padded = -(-C // 128) * 128
    rows = [rb for rb in range(8, R + 1, 8) if R % rb == 0 and rb * padded * 4 <= ADAMW_BLOCK_BYTES]
    if rows or R * padded * 4 <= ADAMW_BLOCK_BYTES:
        return (max(rows) if rows else R), C
    cols = [cb for cb in range(128, C + 1, 128) if C % cb == 0 and R * cb * 4 <= ADAMW_BLOCK_BYTES]
    return R, max(cols)


def _adamw(w, g, m, v, *, name, pass_grad=False):
    R, C = w.shape
    rb, cb = _adamw_block(R, C)
    c1 = 1.0 / (1.0 - B1 ** STEP)
    c2 = 1.0 / (1.0 - B2 ** STEP)

    def body(w_ref, g_ref, m_ref, v_ref, *outs):
        d_ref, nm_ref, nv_ref = outs[-3:]
        gv = g_ref[...]
        if pass_grad:
            outs[0][...] = gv
        m2 = B1 * m_ref[...] + (1.0 - B1) * gv
        v2 = B2 * v_ref[...] + (1.0 - B2) * gv * gv
        nm_ref[...] = m2
        nv_ref[...] = v2
        d_ref[...] = -LR * ((m2 * c1) / (jnp.sqrt(v2 * c2) + ADAM_EPS) + WD * w_ref[...])

    n_out = 4 if pass_grad else 3
    blk = pl.BlockSpec((rb, cb), lambda i, j: (i, j))
    return pl.pallas_call(
        body, name=name, grid=(R // rb, C // cb), in_specs=[blk] * 4, out_specs=[blk] * n_out,
        out_shape=[jax.ShapeDtypeStruct((R, C), F32)] * n_out, compiler_params=_params(2),
    )(w, g, m, v)


SMALL = (("norm1_w", D), ("norm2_w", D), ("final_norm_w", D), ("gate_b", GLA_H * DK), ("gla_norm_w", DV), ("sinks", N_Q))
PACK_W = 1024


def _pack_small(vals, w2, loss):
    rows = []
    for name, width in SMALL:
        v = vals[name].reshape(-1)
        rows.append(jnp.pad(v, (0, (-width) % PACK_W)).reshape(-1, PACK_W))
    rows.append(w2)
    rows.append(jnp.broadcast_to(loss.reshape(1, 1), (1, PACK_W)))
    pack = jnp.concatenate(rows, axis=0)
    return jnp.pad(pack, ((0, 32 - pack.shape[0]), (0, 0)))


def _unpack_small(pack):
    out, r = {}, 0
    for name, width in SMALL:
        nr = -(-width // PACK_W)
        out[name] = pack[r:r + nr].reshape(-1)[:width]
        r += nr
    out["w2"] = pack[r:r + RANK]
    out["loss"] = pack[r + RANK, 0]
    return out


def kernel(x, norm1_w, w_in, gla_gate_w2, gla_gate_b, attn_sinks, gla_norm_w, w_out, norm2_w, w_ffn_gate, w_ffn_up, w_ffn_down, final_norm_w, loss_target, m_norm1_w, m_w_in, m_gla_gate_w2, m_gla_gate_b, m_attn_sinks, m_gla_norm_w, m_w_out, m_norm2_w, m_w_ffn_gate, m_w_ffn_up, m_w_ffn_down, m_final_norm_w, v_norm1_w, v_w_in, v_gla_gate_w2, v_gla_gate_b, v_attn_sinks, v_gla_norm_w, v_w_out, v_norm2_w, v_w_ffn_gate, v_w_ffn_up, v_w_ffn_down, v_final_norm_w):
    chip = 2 * lax.axis_index("x") + lax.axis_index("y")
    w_in_s, w_out_s, w_gate_s, w_up_s, w_down_s, w2_s = (
        w_in[0], w_out[0], w_ffn_gate[0], w_ffn_up[0], w_ffn_down[0], gla_gate_w2[0])
    CS = SHARD_COLS

    g_in, g_w2 = _gather_staged(w_in_s.astype(BF16), w2_s)
    g_out, = _gather_shards([w_out_s.astype(BF16)], name="gather_out", collective_id=1, after=[g_w2])
    g_gate, g_up = _gather_shards([w_gate_s.astype(BF16), w_up_s.astype(BF16)], name="gather_gate_up",
                                  collective_id=7, after=[g_w2])
    g_down, = _gather_shards([w_down_s.astype(BF16)], name="gather_down", collective_id=8, after=[g_w2])
    w_main, w_lr = _w_in_to_main(g_in)
    w2_full = jnp.transpose(g_w2, (1, 0, 2)).reshape(RANK, GLA_H * DK)
    p = dict(
        norm1_w=norm1_w, norm2_w=norm2_w, final_norm_w=final_norm_w.reshape(1, D), gate_b=gla_gate_b,
        gla_norm_w=gla_norm_w, sinks=jnp.pad(attn_sinks, ((0, 0), (0, 128 - N_Q))),
        w_in_main=w_main, w_in_lr=w_lr,
        w2p=jnp.pad(w2_full, ((0, LRP - RANK), (0, 0))).astype(BF16),
        w_out=g_out.reshape(D, D),
        w_gate=g_gate, w_up=g_up, w_down=g_down.reshape(FFN, D),
    )

    tags = ["w_in", "w_out", "w_gate", "w_up", "w_down"]
    recv = {}

    def exchange(tag, sums):
        recv[tag], = _chip_exchange([sums], name="chip_exchange_" + tag, collective_id=2 + tags.index(tag))
        return sums

    def reduce_part(tag, part):
        return exchange(tag, _pair_reduce(part, name="pair_reduce_" + tag))

    def reduce_ffn(tag, a, b):
        mine, got = _grad_matmul_split(a, b, by_rows=(tag == "w_down"), name="mm_grad_" + tag)
        return exchange(tag, _pair_add(mine, got, name="pair_add_" + tag))

    done = {}
    shard = {"w_out": (w_out_s, m_w_out[0], v_w_out[0]), "w_gate": (w_gate_s, m_w_ffn_gate[0], v_w_ffn_gate[0]),
             "w_up": (w_up_s, m_w_ffn_up[0], v_w_ffn_up[0]), "w_down": (w_down_s, m_w_ffn_down[0], v_w_ffn_down[0])}

    def finish(tag):
        grad = _sum_join(recv[tag], name="sum_join_" + tag)
        done[tag] = _adamw(shard[tag][0], grad, shard[tag][1], shard[tag][2], name="adamw_" + tag, pass_grad=True)

    loss_blk, dx, g = _local_step(x[0], loss_target[0], p, reduce_part, reduce_ffn)
    reduce_part("w_in", _main_to_shards(g["w_in_main"], g["w_in_lr"]))
    for tag in ("w_down", "w_gate", "w_up", "w_out"):
        finish(tag)
    behind = [dx] + [done[tag][1] for tag in ("w_down", "w_gate", "w_up", "w_out")]
    g_in = _sum_join(_after(recv["w_in"], behind), name="sum_join_w_in")

    small = _unpack_small(_sum_small(_pack_small(g, g["w2"], loss_blk[0, 0])))
    loss = small["loss"]
    g_w2_mine = lax.dynamic_slice_in_dim(small["w2"], chip * (GLA_H * DK // 4), GLA_H * DK // 4, axis=1)

    grads = dict(
        norm1_w=small["norm1_w"].reshape(1, D), w_in=g_in, gla_gate_w2=g_w2_mine,
        gla_gate_b=small["gate_b"].reshape(1, -1), attn_sinks=small["sinks"].reshape(1, N_Q),
        gla_norm_w=small["gla_norm_w"].reshape(1, DV), norm2_w=small["norm2_w"].reshape(1, D),
        final_norm_w=small["final_norm_w"].reshape(1, D))
    def lin(a):
        return jnp.transpose(a.reshape(D, CS))

    def unlin(a):
        return jnp.transpose(a)

    grads["w_in"] = lin(g_in)
    weights = dict(
        norm1_w=(norm1_w, m_norm1_w, v_norm1_w), w_in=(lin(w_in), lin(m_w_in), lin(v_w_in)),
        gla_gate_w2=(w2_s, m_gla_gate_w2[0], v_gla_gate_w2[0]), gla_gate_b=(gla_gate_b, m_gla_gate_b, v_gla_gate_b),
        attn_sinks=(attn_sinks, m_attn_sinks, v_attn_sinks), gla_norm_w=(gla_norm_w, m_gla_norm_w, v_gla_norm_w),
        norm2_w=(norm2_w, m_norm2_w, v_norm2_w),
        final_norm_w=(final_norm_w.reshape(1, D), m_final_norm_w.reshape(1, D), v_final_norm_w.reshape(1, D)))
    names = ["norm1_w", "w_in", "gla_gate_w2", "gla_gate_b", "attn_sinks", "gla_norm_w", "w_out", "norm2_w",
             "w_ffn_gate", "w_ffn_up", "w_ffn_down", "final_norm_w"]
    lead = {"norm1_w": False, "gla_gate_b": False, "attn_sinks": False, "gla_norm_w": False, "norm2_w": False}
    g_out_l, d_out, m_out, v_out = [], [], [], []
    early = {"w_out": "w_out", "w_ffn_gate": "w_gate", "w_ffn_up": "w_up", "w_ffn_down": "w_down"}
    for nm in names:
        if nm in early:
            gr, dl, nmn, nvn = done[early[nm]]
            w = shard[early[nm]][0]
        else:
            w, m, v = weights[nm]
            gr = grads[nm]
            dl, nmn, nvn = _adamw(w, gr, m, v, name="adamw_" + nm)
        if nm == "w_in":
            gr, dl, nmn, nvn, w = g_in, unlin(dl), unlin(nmn), unlin(nvn), w_in_s
        if nm == "final_norm_w":
            shape = (D,)
        elif nm in lead:
            shape = w.shape
        else:
            shape = (1,) + w.shape
        g_out_l.append(gr.reshape(shape))
        d_out.append(dl.reshape(shape))
        m_out.append(nmn.reshape(shape))
        v_out.append(nvn.reshape(shape))
    return (loss, dx[None], *g_out_l, *d_out, *m_out, *v_out)
```

```python
import functools

import jax
import jax.numpy as jnp
from jax import lax
from jax.experimental import pallas as pl
from jax.experimental.pallas import tpu as pltpu
from jax.experimental.pallas import tpu_sc as plsc

F32 = jnp.float32
BF16 = jnp.bfloat16

D = 2048
HEAD_DIM = 64
N_Q = 32
N_KV = 4
GROUP = 8
WINDOW = 128
GLA_H = 4
DK = 256
DV = 512
RANK = 16
CHUNK = 64
FFN = 5632
EPS = 1e-6
MASK_VALUE = -1e30
GATE_NORM = 16.0
LR, B1, B2, ADAM_EPS, WD, STEP = 0.001, 0.9, 0.999, 1e-08, 0.01, 10

MAIN = 12800
O_MERGE, O_GLA, O_ATTN = 0, 6144, 10240
W_MERGE, W_GLA, W_ATTN = 3 * DV, 2 * DK + DV, D + 2 * N_KV * HEAD_DIM
LR_AT = 6656
LRP = 128


def _main_pieces():
    o_gq, o_gk, o_gv, o_gr, o_ga, o_gb = 2560, 3584, 4608, LR_AT + RANK, LR_AT + RANK + D, LR_AT + RANK + 2 * D
    pieces = []
    for h in range(GLA_H):
        pieces += [(o_gr + DV * h, DV), (o_ga + DV * h, DV), (o_gb + DV * h, DV)]
    for h in range(GLA_H):
        pieces += [(o_gq + DK * h, DK), (o_gk + DK * h, DK), (o_gv + DV * h, DV)]
    pieces.append((0, W_ATTN))
    return pieces


SHARD_COLS = 3204

VMEM_LIMIT = 56 * 1024 * 1024
MESH = pl.DeviceIdType.MESH
ANY = pl.BlockSpec(memory_space=pl.ANY)


def _params(ngrid):
    return pltpu.CompilerParams(dimension_semantics=("arbitrary",) * ngrid, vmem_limit_bytes=VMEM_LIMIT)


def _shard_segments(lo, hi):
    segs = []
    while lo < hi:
        s = lo // SHARD_COLS
        e = min(hi, (s + 1) * SHARD_COLS)
        segs.append((s, lo - s * SHARD_COLS, e - s * SHARD_COLS))
        lo = e
    return segs


def _lanes(pieces):
    return pieces[0] if len(pieces) == 1 else jnp.concatenate(pieces, axis=1)


def _w_in_to_main(shards):
    rb = 256

    def body(g_ref, main_ref, lr_ref):
        at = 0
        for a, w in _main_pieces():
            main_ref[:, at:at + w] = _lanes([g_ref[s, :, lo:hi] for s, lo, hi in _shard_segments(a, a + w)])
            at += w
        lr = [g_ref[s, :, lo:hi] for s, lo, hi in _shard_segments(LR_AT, LR_AT + RANK)]
        lr_ref[...] = _lanes(lr + [jnp.zeros((rb, LRP - RANK), lr_ref.dtype)])

    return pl.pallas_call(
        body, name="w_in_to_main", grid=(D // rb,),
        in_specs=[pl.BlockSpec((4, rb, SHARD_COLS), lambda i: (0, i, 0))],
        out_specs=[pl.BlockSpec((rb, MAIN), lambda i: (i, 0)), pl.BlockSpec((rb, LRP), lambda i: (i, 0))],
        out_shape=[jax.ShapeDtypeStruct((D, MAIN), shards.dtype), jax.ShapeDtypeStruct((D, LRP), shards.dtype)],
        compiler_params=_params(1),
    )(shards)


def _main_to_shards(g_main, g_lr):
    rb = 256
    at, sources = 0, [(LR_AT, RANK, 1, 0)]
    for a, w in _main_pieces():
        sources.append((a, w, 0, at))
        at += w
    sources.sort()

    def body(main_ref, lr_ref, out_ref):
        refs = (main_ref, lr_ref)
        for s in range(4):
            lo, hi = s * SHARD_COLS, (s + 1) * SHARD_COLS
            pieces = []
            for a, w, which, src in sources:
                b, e = max(a, lo), min(a + w, hi)
                if b < e:
                    pieces.append(refs[which][:, src + b - a:src + e - a])
            out_ref[s] = _lanes(pieces)

    return pl.pallas_call(
        body, name="main_to_shards", grid=(D // rb,),
        in_specs=[pl.BlockSpec((rb, MAIN), lambda i: (i, 0)), pl.BlockSpec((rb, LRP), lambda i: (i, 0))],
        out_specs=pl.BlockSpec((4, rb, SHARD_COLS), lambda i: (0, i, 0)),
        out_shape=jax.ShapeDtypeStruct((4, D, SHARD_COLS), g_main.dtype), compiler_params=_params(1),
    )(g_main, g_lr)


def _pick(n, cands):
    for c in cands:
        if n % c == 0:
            return c
    return n


def _sigmoid(x):
    return 1.0 / (1.0 + jnp.exp(-x))


def _dot(a, b, ca, cb):
    return lax.dot_general(a, b, (((ca,), (cb,)), ((), ())), preferred_element_type=F32)


def _matmul(a, b, *, name, ta=False, tb=False, add=None, out_dtype=F32, bm=None, bj=None, bc=None,
            b_slots=False, out_slots=False):
    C, M = a.shape if ta else a.shape[::-1]
    if b_slots:
        if tb:
            J, bc = b.shape[1], b.shape[2]
            assert b.shape[0] * bc == C
        else:
            bj = b.shape[2]
            J = b.shape[0] * bj
            assert b.shape[1] == C
    else:
        J = b.shape[0] if tb else b.shape[1]
        assert (b.shape[1] if tb else b.shape[0]) == C
    bm = bm or _pick(M, (1024, 512, 256, 128) if C <= FFN else (512, 256, 128))
    bj = bj or _pick(J, (1280, 1024, 512, 256, 128) if C <= D else (512, 256, 128))
    bc = bc or (C if C <= FFN else _pick(C, (6400,)))
    nm, nj, nc = M // bm, J // bj, C // bc
    has_add = add is not None

    def body(*refs):
        a_ref, b_ref = refs[0], refs[1]
        add_ref = refs[2] if has_add else None
        o_ref = refs[3] if has_add else refs[2]
        p = _dot(a_ref[...].astype(BF16), b_ref[...].astype(BF16), 0 if ta else 1, 1 if tb else 0)

        def finish(acc):
            if has_add:
                acc = acc + add_ref[...]
            o_ref[...] = acc.astype(o_ref.dtype)

        if nc == 1:
            finish(p)
        else:
            acc_ref = refs[-1]
            c = pl.program_id(2)

            @pl.when(c == 0)
            def _():
                acc_ref[...] = p

            @pl.when(c > 0)
            def _():
                acc_ref[...] += p

            @pl.when(c == nc - 1)
            def _():
                finish(acc_ref[...])

    a_spec = pl.BlockSpec((bc, bm), lambda m, j, c: (c, m)) if ta else pl.BlockSpec((bm, bc), lambda m, j, c: (m, c))
    if b_slots:
        b_spec = (pl.BlockSpec((None, bj, bc), lambda m, j, c: (c, j, 0)) if tb
                  else pl.BlockSpec((None, bc, bj), lambda m, j, c: (j, c, 0)))
    else:
        b_spec = (pl.BlockSpec((bj, bc), lambda m, j, c: (j, c)) if tb
                  else pl.BlockSpec((bc, bj), lambda m, j, c: (c, j)))
    if out_slots:
        assert not has_add
        o_spec = pl.BlockSpec((None, bm, bj), lambda m, j, c: (j, m, 0))
        out_shape = jax.ShapeDtypeStruct((nj, M, bj), out_dtype)
    else:
        o_spec = pl.BlockSpec((bm, bj), lambda m, j, c: (m, j))
        out_shape = jax.ShapeDtypeStruct((M, J), out_dtype)
    in_specs = [a_spec, b_spec] + ([o_spec] if has_add else [])
    args = (a, b) + ((add,) if has_add else ())
    return pl.pallas_call(
        body, name=name, grid=(nm, nj, nc), in_specs=in_specs, out_specs=o_spec,
        out_shape=out_shape,
        scratch_shapes=[pltpu.VMEM((bm, bj), F32)] if nc > 1 else [],
        compiler_params=_params(3),
    )(*args)


def _rows(T):
    return _pick(T, (256, 128))


def _rms_fwd(x, w, *, name):
    T = x.shape[0]
    rb = _rows(T)

    def body(x_ref, w_ref, u_ref, ut_ref):
        xv = x_ref[...]
        r = lax.rsqrt(jnp.mean(xv * xv, axis=-1, keepdims=True) + EPS)
        u = xv * r * w_ref[...]
        u_ref[...] = u.astype(BF16)
        ut_ref[...] = u.T.astype(BF16)

    return pl.pallas_call(
        body, name=name, grid=(T // rb,),
        in_specs=[pl.BlockSpec((rb, D), lambda i: (i, 0)), pl.BlockSpec((1, D), lambda i: (0, 0))],
        out_specs=[pl.BlockSpec((rb, D), lambda i: (i, 0)), pl.BlockSpec((D, rb), lambda i: (0, i))],
        out_shape=[jax.ShapeDtypeStruct((T, D), BF16), jax.ShapeDtypeStruct((D, T), BF16)], compiler_params=_params(1),
    )(x, w)


def _out_proj(merged, w_out, x, norm_w):
    T = merged.shape[0]
    bm = _pick(T, (512, 256))

    def body(a_ref, b_ref, x_ref, w_ref, h_ref, v_ref, vt_ref):
        h = _dot(a_ref[...], b_ref[...], 1, 0) + x_ref[...]
        h_ref[...] = h
        r = lax.rsqrt(jnp.mean(h * h, axis=-1, keepdims=True) + EPS)
        v2 = h * r * w_ref[...]
        v_ref[...] = v2.astype(BF16)
        vt_ref[...] = v2.T.astype(BF16)

    blk = pl.BlockSpec((bm, D), lambda i: (i, 0))
    return pl.pallas_call(
        body, name="out_proj", grid=(T // bm,),
        in_specs=[blk, pl.BlockSpec((D, D), lambda i: (0, 0)), blk, pl.BlockSpec((1, D), lambda i: (0, 0))],
        out_specs=[blk, blk, pl.BlockSpec((D, bm), lambda i: (0, i))],
        out_shape=[jax.ShapeDtypeStruct((T, D), F32), jax.ShapeDtypeStruct((T, D), BF16),
                   jax.ShapeDtypeStruct((D, T), BF16)],
        compiler_params=_params(1),
    )(merged, w_out, x, norm_w)


def _rms_bwd(dy, h, w, resid, *, name):
    T = h.shape[0]
    rb = _rows(T)

    def body(dy_ref, h_ref, w_ref, res_ref, dh_ref, dhb_ref, dw_ref):
        @pl.when(pl.program_id(0) == 0)
        def _():
            dw_ref[...] = jnp.zeros_like(dw_ref)

        hv = h_ref[...]
        r = lax.rsqrt(jnp.mean(hv * hv, axis=-1, keepdims=True) + EPS)
        hn = hv * r
        dyv = dy_ref[...]
        dw_ref[...] += jnp.sum(dyv * hn, axis=0, keepdims=True)
        t = dyv * w_ref[...]
        dh = res_ref[...] + r * (t - hn * jnp.mean(t * hn, axis=-1, keepdims=True))
        dh_ref[...] = dh
        dhb_ref[...] = dh.astype(BF16)

    blk = pl.BlockSpec((rb, D), lambda i: (i, 0))
    vec = pl.BlockSpec((1, D), lambda i: (0, 0))
    return pl.pallas_call(
        body, name=name, grid=(T // rb,), in_specs=[blk, blk, vec, blk], out_specs=[blk, blk, vec],
        out_shape=[jax.ShapeDtypeStruct((T, D), F32), jax.ShapeDtypeStruct((T, D), BF16),
                   jax.ShapeDtypeStruct((1, D), F32)],
        compiler_params=_params(1),
    )(dy, h, w, resid)


def _rms_bwd_input(du_main, dproj_lr, w_in_lr, x, w, resid):
    T = x.shape[0]
    rb = _rows(T)

    def body(du_ref, dl_ref, wl_ref, x_ref, w_ref, res_ref, dx_ref, dw_ref):
        @pl.when(pl.program_id(0) == 0)
        def _():
            dw_ref[...] = jnp.zeros_like(dw_ref)

        xv = x_ref[...]
        r = lax.rsqrt(jnp.mean(xv * xv, axis=-1, keepdims=True) + EPS)
        xn = xv * r
        du = du_ref[...] + _dot(dl_ref[...], wl_ref[...], 1, 1)
        dw_ref[...] += jnp.sum(du * xn, axis=0, keepdims=True)
        t = du * w_ref[...]
        dx_ref[...] = res_ref[...] + r * (t - xn * jnp.mean(t * xn, axis=-1, keepdims=True))

    blk = pl.BlockSpec((rb, D), lambda i: (i, 0))
    vec = pl.BlockSpec((1, D), lambda i: (0, 0))
    return pl.pallas_call(
        body, name="rms1_bwd", grid=(T // rb,),
        in_specs=[blk, pl.BlockSpec((rb, LRP), lambda i: (i, 0)), pl.BlockSpec((D, LRP), lambda i: (0, 0)), blk, vec,
                  blk],
        out_specs=[blk, vec], out_shape=[jax.ShapeDtypeStruct((T, D), F32), jax.ShapeDtypeStruct((1, D), F32)],
        compiler_params=_params(1),
    )(du_main, dproj_lr, w_in_lr, x, w, resid)


def _final_loss(h, w, target):
    T = h.shape[0]
    rb = _rows(T)

    def body(h_ref, w_ref, t_ref, loss_ref, dh_ref, dhb_ref, dw_ref):
        @pl.when(pl.program_id(0) == 0)
        def _():
            dw_ref[...] = jnp.zeros_like(dw_ref)
            loss_ref[...] = jnp.zeros_like(loss_ref)

        hv = h_ref[...]
        wv = w_ref[...]
        r = lax.rsqrt(jnp.mean(hv * hv, axis=-1, keepdims=True) + EPS)
        hn = hv * r
        e = hn * wv - t_ref[...]
        row = jnp.sum(e * e, axis=-1, keepdims=True) * (0.5 / D)
        loss_ref[...] += jnp.broadcast_to(jnp.sum(row, axis=0, keepdims=True), loss_ref.shape)
        dy = e * (1.0 / D)
        dw_ref[...] += jnp.sum(dy * hn, axis=0, keepdims=True)
        t = dy * wv
        dh = r * (t - hn * jnp.mean(t * hn, axis=-1, keepdims=True))
        dh_ref[...] = dh
        dhb_ref[...] = dh.astype(BF16)

    blk = pl.BlockSpec((rb, D), lambda i: (i, 0))
    vec = pl.BlockSpec((1, D), lambda i: (0, 0))
    return pl.pallas_call(
        body, name="final_loss", grid=(T // rb,), in_specs=[blk, vec, blk],
        out_specs=[pl.BlockSpec((8, 128), lambda i: (0, 0)), blk, blk, vec],
        out_shape=[jax.ShapeDtypeStruct((8, 128), F32), jax.ShapeDtypeStruct((T, D), F32),
                   jax.ShapeDtypeStruct((T, D), BF16), jax.ShapeDtypeStruct((1, D), F32)],
        compiler_params=_params(1),
    )(h, w, target)


PANEL = FFN // 4


def _ffn_fwd(v2, w_gate, w_up):
    T = v2.shape[0]
    bm = _pick(T, (512, 256))

    def body(a_ref, bg_ref, bu_ref, gate_ref, up_ref, ff_ref, fft_ref):
        a = a_ref[...]
        g = _dot(a, bg_ref[...], 1, 0)
        u = _dot(a, bu_ref[...], 1, 0)
        gate_ref[...] = g
        up_ref[...] = u
        ff = g * _sigmoid(g) * u
        ff_ref[...] = ff.astype(BF16)
        fft_ref[...] = ff.T.astype(BF16)

    panel = pl.BlockSpec((None, D, PANEL), lambda j, m: (j, 0, 0))
    out = pl.BlockSpec((bm, PANEL), lambda j, m: (m, j))
    return pl.pallas_call(
        body, name="ffn_fwd", grid=(4, T // bm),
        in_specs=[pl.BlockSpec((bm, D), lambda j, m: (m, 0)), panel, panel],
        out_specs=[out, out, out, pl.BlockSpec((PANEL, bm), lambda j, m: (j, m))],
        out_shape=[jax.ShapeDtypeStruct((T, FFN), F32), jax.ShapeDtypeStruct((T, FFN), F32),
                   jax.ShapeDtypeStruct((T, FFN), BF16), jax.ShapeDtypeStruct((FFN, T), BF16)],
        compiler_params=_params(2),
    )(v2, w_gate, w_up)


def _ffn_bwd_hidden(dh2, w_down, gate, up):
    T = dh2.shape[0]
    bm = _pick(T, (512, 256))

    def body(a_ref, b_ref, g_ref, u_ref, dg_ref, du_ref):
        d = _dot(a_ref[...], b_ref[...], 1, 1)
        g = g_ref[...]
        sg = _sigmoid(g)
        du_ref[...] = (d * g * sg).astype(BF16)
        dg_ref[...] = (d * u_ref[...] * sg * (1.0 + g * (1.0 - sg))).astype(BF16)

    blk = pl.BlockSpec((bm, PANEL), lambda j, m: (m, j))
    return pl.pallas_call(
        body, name="ffn_bwd_hidden", grid=(4, T // bm),
        in_specs=[pl.BlockSpec((bm, D), lambda j, m: (m, 0)), pl.BlockSpec((PANEL, D), lambda j, m: (j, 0)), blk, blk],
        out_specs=[blk, blk], out_shape=[jax.ShapeDtypeStruct((T, FFN), BF16)] * 2, compiler_params=_params(2),
    )(dh2, w_down, gate, up)


def _ffn_bwd_input(dgate, dup, w_gate, w_up):
    T = dgate.shape[0]
    bm = _pick(T, (1024, 512, 256))
    bj = 1024

    def body(ag_ref, au_ref, bg_ref, bu_ref, o_ref, acc_ref):
        c = pl.program_id(2)
        p = _dot(ag_ref[...], bg_ref[...], 1, 1) + _dot(au_ref[...], bu_ref[...], 1, 1)

        @pl.when(c == 0)
        def _():
            acc_ref[...] = p

        @pl.when(c > 0)
        def _():
            acc_ref[...] += p

        @pl.when(c == 3)
        def _():
            o_ref[...] = acc_ref[...]

    a = pl.BlockSpec((bm, PANEL), lambda m, j, c: (m, c))
    b = pl.BlockSpec((None, bj, PANEL), lambda m, j, c: (c, j, 0))
    return pl.pallas_call(
        body, name="ffn_bwd_input", grid=(T // bm, D // bj, 4), in_specs=[a, a, b, b],
        out_specs=pl.BlockSpec((bm, bj), lambda m, j, c: (m, j)), out_shape=jax.ShapeDtypeStruct((T, D), F32),
        scratch_shapes=[pltpu.VMEM((bm, bj), F32)], compiler_params=_params(3),
    )(dgate, dup, w_gate, w_up)


def _colsum(x, *, name):
    T, W = x.shape
    rb = _rows(T)

    def body(x_ref, o_ref):
        @pl.when(pl.program_id(0) == 0)
        def _():
            o_ref[...] = jnp.zeros_like(o_ref)

        o_ref[...] += jnp.sum(x_ref[...], axis=0, keepdims=True)

    return pl.pallas_call(
        body, name=name, grid=(T // rb,), in_specs=[pl.BlockSpec((rb, W), lambda i: (i, 0))],
        out_specs=pl.BlockSpec((1, W), lambda i: (0, 0)), out_shape=jax.ShapeDtypeStruct((1, W), F32),
        compiler_params=_params(1),
    )(x)


def _merge_fwd(attn_o, gla_raw, proj, gla_norm_w):
    T = attn_o.shape[0]
    rb = _rows(T)

    def body(a_ref, g_ref, gr_ref, ga_ref, gb_ref, w_ref, o_ref, ot_ref):
        graw = g_ref[...]
        r = lax.rsqrt(jnp.mean(graw * graw, axis=-1, keepdims=True) + EPS)
        gr = gr_ref[...]
        go = graw * r * w_ref[...] * (gr * _sigmoid(gr))
        merged = _sigmoid(ga_ref[...]) * a_ref[...] + _sigmoid(gb_ref[...]) * go
        o_ref[...] = merged.astype(BF16)
        ot_ref[...] = merged.T.astype(BF16)

    def sec(k):
        return pl.BlockSpec((rb, DV), lambda i, h: (i, O_MERGE // DV + 3 * h + k))

    blk = pl.BlockSpec((rb, DV), lambda i, h: (i, h))
    return pl.pallas_call(
        body, name="merge_fwd", grid=(T // rb, GLA_H),
        in_specs=[blk, blk, sec(0), sec(1), sec(2), pl.BlockSpec((1, DV), lambda i, h: (0, 0))],
        out_specs=[blk, pl.BlockSpec((DV, rb), lambda i, h: (h, i))],
        out_shape=[jax.ShapeDtypeStruct((T, D), BF16), jax.ShapeDtypeStruct((D, T), BF16)], compiler_params=_params(2),
    )(attn_o, gla_raw, proj, proj, proj, gla_norm_w)


def _merge_bwd(dm, attn_o, gla_raw, proj, gla_norm_w):
    T = attn_o.shape[0]
    rb = _rows(T)

    def body(dm_ref, a_ref, g_ref, gr_ref, ga_ref, gb_ref, w_ref, da_ref, dg_ref, dp_ref, dw_ref):
        @pl.when((pl.program_id(0) == 0) & (pl.program_id(1) == 0))
        def _():
            dw_ref[...] = jnp.zeros_like(dw_ref)

        dmv = dm_ref[...]
        av = a_ref[...]
        graw = g_ref[...]
        gr = gr_ref[...]
        wv = w_ref[...]
        sa = _sigmoid(ga_ref[...])
        sb = _sigmoid(gb_ref[...])
        r = lax.rsqrt(jnp.mean(graw * graw, axis=-1, keepdims=True) + EPS)
        gnh = graw * r
        gn = gnh * wv
        sr = _sigmoid(gr)
        sl = gr * sr
        go = gn * sl
        da_ref[...] = (dmv * sa).astype(BF16)
        dgo = dmv * sb
        dp_ref[:, 0:DV] = (dgo * gn * sr * (1.0 + gr * (1.0 - sr))).astype(BF16)
        dp_ref[:, DV:2 * DV] = (dmv * av * sa * (1.0 - sa)).astype(BF16)
        dp_ref[:, 2 * DV:3 * DV] = (dmv * go * sb * (1.0 - sb)).astype(BF16)
        dgn = dgo * sl
        dw_ref[...] += jnp.sum(dgn * gnh, axis=0, keepdims=True)
        t = dgn * wv
        dg_ref[...] = (r * (t - gnh * jnp.mean(t * gnh, axis=-1, keepdims=True))).astype(BF16)

    def sec(k):
        return pl.BlockSpec((rb, DV), lambda i, h: (i, O_MERGE // DV + 3 * h + k))

    blk = pl.BlockSpec((rb, DV), lambda i, h: (i, h))
    vec = pl.BlockSpec((1, DV), lambda i, h: (0, 0))
    return pl.pallas_call(
        body, name="merge_bwd", grid=(T // rb, GLA_H),
        in_specs=[blk, blk, blk, sec(0), sec(1), sec(2), vec],
        out_specs=[blk, blk, pl.BlockSpec((rb, W_MERGE), lambda i, h: (i, O_MERGE // W_MERGE + h)), vec],
        out_shape=[jax.ShapeDtypeStruct((T, D), BF16), jax.ShapeDtypeStruct((T, D), BF16),
                   jax.ShapeDtypeStruct((T, MAIN), BF16), jax.ShapeDtypeStruct((1, DV), F32)],
        compiler_params=_params(2),
    )(dm, attn_o, gla_raw, proj, proj, proj, gla_norm_w)


def _attn_band(n):
    qi = lax.broadcasted_iota(jnp.int32, (GROUP * WINDOW, WINDOW), 0) & (WINDOW - 1)
    kj = lax.broadcasted_iota(jnp.int32, (GROUP * WINDOW, WINDOW), 1)
    cur = kj <= qi
    return cur, cur | (n > 0)


def _stack_heads(ref, h, scale=1.0):
    rows = jnp.concatenate(
        [ref[:, (h * GROUP + g) * HEAD_DIM:(h * GROUP + g + 1) * HEAD_DIM] for g in range(GROUP)], axis=0)
    return (rows * scale).astype(BF16) if scale != 1.0 else rows.astype(BF16)


def _stack_sinks(s_ref, h):
    return jnp.concatenate(
        [jnp.broadcast_to(s_ref[:, h * GROUP + g:h * GROUP + g + 1], (WINDOW, 1)) for g in range(GROUP)], axis=0)


def _attn_probs(qs, kp, kc, band, sink):
    cur, live = band
    s = jnp.where(cur, _dot(qs, kc, 1, 1), _dot(qs, kp, 1, 1))
    s = jnp.where(live, s, MASK_VALUE)
    m = jnp.maximum(jnp.max(s, axis=-1, keepdims=True), sink)
    e = jnp.exp(s - m)
    es = jnp.exp(sink - m)
    inv = 1.0 / (jnp.sum(e, axis=-1, keepdims=True) + es)
    return e * inv, es * inv


def _unfold(cur, x):
    mine = jnp.where(cur, x, 0.0)
    return mine.astype(BF16), (x - mine).astype(BF16)


def _attn_specs(nb, rev):
    def at(n):
        return (nb - 1 - n) if rev else n

    kcol, vcol = (O_ATTN + D) // 256, (O_ATTN + D) // 256 + 1
    q = pl.BlockSpec((WINDOW, D), lambda n: (at(n), O_ATTN // D))
    kc = pl.BlockSpec((WINDOW, 256), lambda n: (at(n), kcol))
    kp = pl.BlockSpec((WINDOW, 256), lambda n: (jnp.maximum(at(n) - 1, 0), kcol))
    vc = pl.BlockSpec((WINDOW, 256), lambda n: (at(n), vcol))
    vp = pl.BlockSpec((WINDOW, 256), lambda n: (jnp.maximum(at(n) - 1, 0), vcol))
    sk = pl.BlockSpec((1, 128), lambda n: (0, 0))
    o = pl.BlockSpec((WINDOW, D), lambda n: (at(n), 0))
    return q, kc, kp, vc, vp, sk, o


def _attn_fwd(proj, sinks):
    T = proj.shape[0]
    nb = T // WINDOW

    def body(q_ref, kc_ref, kp_ref, vc_ref, vp_ref, s_ref, o_ref):
        band = _attn_band(pl.program_id(0))
        for h in range(N_KV):
            hs = slice(h * HEAD_DIM, (h + 1) * HEAD_DIM)
            p, _ = _attn_probs(_stack_heads(q_ref, h, HEAD_DIM ** -0.5), kp_ref[:, hs].astype(BF16),
                               kc_ref[:, hs].astype(BF16), band, _stack_sinks(s_ref, h))
            p_cur, p_prev = _unfold(band[0], p)
            o = _dot(p_cur, vc_ref[:, hs].astype(BF16), 1, 0) + _dot(p_prev, vp_ref[:, hs].astype(BF16), 1, 0)
            for g in range(GROUP):
                hg = h * GROUP + g
                o_ref[:, hg * HEAD_DIM:(hg + 1) * HEAD_DIM] = o[g * WINDOW:(g + 1) * WINDOW]

    q, kc, kp, vc, vp, sk, o = _attn_specs(nb, False)
    return pl.pallas_call(
        body, name="attn_fwd", grid=(nb,), in_specs=[q, kc, kp, vc, vp, sk], out_specs=o,
        out_shape=jax.ShapeDtypeStruct((T, D), F32), compiler_params=_params(1),
    )(proj, proj, proj, proj, proj, sinks)


def _attn_bwd(proj, sinks, d_o, dproj):
    T = proj.shape[0]
    nb = T // WINDOW
    kat, vat = D, D + N_KV * HEAD_DIM

    def body(q_ref, kc_ref, kp_ref, vc_ref, vp_ref, s_ref, do_ref, _, dp_ref, ds_ref, ck_ref, cv_ref):
        i = pl.program_id(0)
        n = nb - 1 - i

        @pl.when(i == 0)
        def _():
            ck_ref[...] = jnp.zeros_like(ck_ref)
            cv_ref[...] = jnp.zeros_like(cv_ref)
            ds_ref[...] = jnp.zeros_like(ds_ref)

        band = _attn_band(n)
        for h in range(N_KV):
            hs = slice(h * HEAD_DIM, (h + 1) * HEAD_DIM)
            kp, kc = kp_ref[:, hs].astype(BF16), kc_ref[:, hs].astype(BF16)
            vp, vc = vp_ref[:, hs].astype(BF16), vc_ref[:, hs].astype(BF16)
            qs = _stack_heads(q_ref, h, HEAD_DIM ** -0.5)
            do = _stack_heads(do_ref, h)
            p, ps = _attn_probs(qs, kp, kc, band, _stack_sinks(s_ref, h))
            dp = jnp.where(band[0], _dot(do, vc, 1, 1), _dot(do, vp, 1, 1))
            delta = jnp.sum(p * dp, axis=-1, keepdims=True)
            ds_cur, ds_prev = _unfold(band[0], p * (dp - delta))
            p_cur, p_prev = _unfold(band[0], p)
            dsink = -ps * delta
            dq = ((_dot(ds_cur, kc, 1, 0) + _dot(ds_prev, kp, 1, 0)) * (HEAD_DIM ** -0.5)).astype(BF16)
            for g in range(GROUP):
                hg = h * GROUP + g
                rows = slice(g * WINDOW, (g + 1) * WINDOW)
                ds_ref[hg:hg + 1, :] += jnp.broadcast_to(jnp.sum(dsink[rows], axis=0, keepdims=True), (1, 128))
                dp_ref[:, hg * HEAD_DIM:(hg + 1) * HEAD_DIM] = dq[rows]
            dp_ref[:, kat + h * HEAD_DIM:kat + (h + 1) * HEAD_DIM] = (_dot(ds_cur, qs, 0, 0) + ck_ref[:, hs]).astype(BF16)
            dp_ref[:, vat + h * HEAD_DIM:vat + (h + 1) * HEAD_DIM] = (_dot(p_cur, do, 0, 0) + cv_ref[:, hs]).astype(BF16)
            ck_ref[:, hs] = _dot(ds_prev, qs, 0, 0)
            cv_ref[:, hs] = _dot(p_prev, do, 0, 0)

    q, kc, kp, vc, vp, sk, o = _attn_specs(nb, True)
    return pl.pallas_call(
        body, name="attn_bwd", grid=(nb,), in_specs=[q, kc, kp, vc, vp, sk, o, ANY],
        out_specs=[pl.BlockSpec((WINDOW, W_ATTN), lambda n: (nb - 1 - n, O_ATTN // W_ATTN)),
                   pl.BlockSpec((N_Q, 128), lambda n: (0, 0))],
        out_shape=[jax.ShapeDtypeStruct((T, MAIN), BF16), jax.ShapeDtypeStruct((N_Q, 128), F32)],
        scratch_shapes=[pltpu.VMEM((WINDOW, 256), F32), pltpu.VMEM((WINDOW, 256), F32)],
        input_output_aliases={7: 0}, compiler_params=_params(1),
    )(proj, proj, proj, proj, proj, sinks, d_o, dproj)


def _split3(x):
    hi = x.astype(BF16)
    r1 = x - hi.astype(F32)
    mid = r1.astype(BF16)
    lo = (r1 - mid.astype(F32)).astype(BF16)
    return hi, mid, lo


def _tri_sum(tri, x):
    hi, mid, lo = _split3(x)
    return _dot(tri, hi, 1, 0) + _dot(tri, mid, 1, 0) + _dot(tri, lo, 1, 0)


def _chunk_masks(rb):
    r = lax.broadcasted_iota(jnp.int32, (rb, rb), 0)
    c = lax.broadcasted_iota(jnp.int32, (rb, rb), 1)
    same = (r ^ c) < CHUNK
    return same & (r >= c), same & (r <= c)


def _per_chunk_rows(x, per, pick):
    return jnp.concatenate(
        [jnp.broadcast_to(pick(x[i * CHUNK:(i + 1) * CHUNK]), (CHUNK, x.shape[1])) for i in range(per)], axis=0)


def _gla_block(q, k, lr, w2, b, lower):
    per = q.shape[0] // CHUNK
    logit = _dot(lr.astype(BF16), w2.astype(BF16), 1, 0) + b
    la = (jnp.minimum(logit, 0.0) - jnp.log(1.0 + jnp.exp(-jnp.abs(logit)))) * (1.0 / GATE_NORM)
    g = _tri_sum(lower, la)
    gl = _per_chunk_rows(g, per, lambda c: c[CHUNK - 1:CHUNK])
    eg = jnp.exp(g)
    qd = q * (DK ** -0.5) * eg
    ki = k * jnp.exp(-g)
    ke = k * jnp.exp(gl - g)
    return logit, g, gl, eg, qd, ki, ke


def _gla_rows(T, forward):
    return _pick(T, (512, 256, 128, 64) if forward else (256, 128, 64))


def _gla_fwd(proj, proj_lr, w2p, gate_b):
    T = proj.shape[0]
    rb = _gla_rows(T, True)
    per = rb // CHUNK

    def body(q_ref, k_ref, v_ref, lr_ref, w2_ref, b_ref, o_ref, st_ref, s_scr):
        @pl.when(pl.program_id(1) == 0)
        def _():
            s_scr[...] = jnp.zeros_like(s_scr)

        low, _ = _chunk_masks(rb)
        _, _, gl, _, qd, ki, ke = _gla_block(q_ref[...], k_ref[...], lr_ref[...], w2_ref[...], b_ref[...],
                                             low.astype(BF16))
        v = v_ref[...].astype(BF16)
        qdb, keb = qd.astype(BF16), ke.astype(BF16)
        att = jnp.where(low, _dot(qdb, ki.astype(BF16), 1, 1), 0.0).astype(BF16)
        o_intra = _dot(att, v, 1, 0)
        st = s_scr[...]
        for i in range(per):
            rows = slice(i * CHUNK, (i + 1) * CHUNK)
            st_ref[0, i] = st
            o_ref[rows, :] = o_intra[rows] + _dot(qdb[rows], st.astype(BF16), 1, 1)
            st = st * jnp.exp(gl[i * CHUNK:i * CHUNK + 1]) + _dot(v[rows], keb[rows], 0, 0)
        s_scr[...] = st

    return pl.pallas_call(
        body, name="gla_fwd", grid=(GLA_H, T // rb),
        in_specs=[pl.BlockSpec((rb, DK), lambda h, n: (n, (O_GLA + W_GLA * h) // DK)),
                  pl.BlockSpec((rb, DK), lambda h, n: (n, (O_GLA + W_GLA * h) // DK + 1)),
                  pl.BlockSpec((rb, DV), lambda h, n: (n, (O_GLA + W_GLA * h) // DV + 1)),
                  pl.BlockSpec((rb, LRP), lambda h, n: (n, 0)),
                  pl.BlockSpec((LRP, DK), lambda h, n: (0, h)),
                  pl.BlockSpec((1, DK), lambda h, n: (0, h))],
        out_specs=[pl.BlockSpec((rb, DV), lambda h, n: (n, h)),
                   pl.BlockSpec((1, per, DV, DK), lambda h, n: (h, n, 0, 0))],
        out_shape=[jax.ShapeDtypeStruct((T, GLA_H * DV), F32),
                   jax.ShapeDtypeStruct((GLA_H, T // CHUNK, DV, DK), F32)],
        scratch_shapes=[pltpu.VMEM((DV, DK), F32)], compiler_params=_params(2),
    )(proj, proj, proj, proj_lr, w2p, gate_b)


def _gla_bwd(proj, proj_lr, w2p, gate_b, states, d_o, dproj):
    T = proj.shape[0]
    rb = _gla_rows(T, False)
    per = rb // CHUNK
    nblk = T // rb

    def body(q_ref, k_ref, v_ref, lr_ref, w2_ref, b_ref, st_ref, do_ref, _, dp_ref, dl_ref, ds_scr):
        @pl.when(pl.program_id(1) == 0)
        def _():
            ds_scr[...] = jnp.zeros_like(ds_scr)

        low, up = _chunk_masks(rb)
        logit, g, gl, eg, qd, ki, ke = _gla_block(q_ref[...], k_ref[...], lr_ref[...], w2_ref[...], b_ref[...],
                                                  low.astype(BF16))
        v = v_ref[...].astype(BF16)
        do = do_ref[...].astype(BF16)
        qdb, kib, keb = qd.astype(BF16), ki.astype(BF16), ke.astype(BF16)
        att = jnp.where(low, _dot(qdb, kib, 1, 1), 0.0).astype(BF16)
        datt = jnp.where(low, _dot(do, v, 1, 1), 0.0).astype(BF16)
        dv_intra = _dot(att, do, 0, 0)
        dqd_intra = _dot(datt, kib, 1, 0)
        dki = _dot(datt, qdb, 0, 0)
        ds = ds_scr[...]
        dqd, dke, dgl = [None] * per, [None] * per, [None] * per
        for i in reversed(range(per)):
            rows = slice(i * CHUNK, (i + 1) * CHUNK)
            sp = st_ref[0, i]
            dsb = ds.astype(BF16)
            dp_ref[rows, 2 * DK:] = (dv_intra[rows] + _dot(keb[rows], dsb, 1, 1)).astype(BF16)
            dqd[i] = dqd_intra[rows] + _dot(do[rows], sp.astype(BF16), 1, 0)
            dke[i] = _dot(v[rows], dsb, 1, 0)
            decay = jnp.exp(gl[i * CHUNK:i * CHUNK + 1])
            dgl[i] = (jnp.sum(dke[i] * ke[rows], axis=0, keepdims=True)
                      + jnp.sum(ds * sp, axis=0, keepdims=True) * decay)
            ds = ds * decay + _dot(do[rows], qdb[rows], 0, 0)
        ds_scr[...] = ds
        dqd = jnp.concatenate(dqd, axis=0)
        dke = jnp.concatenate(dke, axis=0)
        dgl = jnp.concatenate([jnp.broadcast_to(d, (CHUNK, DK)) for d in dgl], axis=0)
        dp_ref[:, 0:DK] = (dqd * (DK ** -0.5) * eg).astype(BF16)
        dp_ref[:, DK:2 * DK] = (dki * jnp.exp(-g) + dke * jnp.exp(gl - g)).astype(BF16)
        dg = dqd * qd - dki * ki - dke * ke
        dla = _tri_sum(up.astype(BF16), dg) + dgl
        dl_ref[...] = dla * (1.0 / GATE_NORM) * (1.0 - _sigmoid(logit))

    def rev(n):
        return nblk - 1 - n

    return pl.pallas_call(
        body, name="gla_bwd", grid=(GLA_H, nblk),
        in_specs=[pl.BlockSpec((rb, DK), lambda h, n: (rev(n), (O_GLA + W_GLA * h) // DK)),
                  pl.BlockSpec((rb, DK), lambda h, n: (rev(n), (O_GLA + W_GLA * h) // DK + 1)),
                  pl.BlockSpec((rb, DV), lambda h, n: (rev(n), (O_GLA + W_GLA * h) // DV + 1)),
                  pl.BlockSpec((rb, LRP), lambda h, n: (rev(n), 0)),
                  pl.BlockSpec((LRP, DK), lambda h, n: (0, h)),
                  pl.BlockSpec((1, DK), lambda h, n: (0, h)),
                  pl.BlockSpec((1, per, DV, DK), lambda h, n: (h, rev(n), 0, 0)),
                  pl.BlockSpec((rb, DV), lambda h, n: (rev(n), h)),
                  ANY],
        out_specs=[pl.BlockSpec((rb, W_GLA), lambda h, n: (rev(n), O_GLA // W_GLA + h)),
                   pl.BlockSpec((rb, DK), lambda h, n: (rev(n), h))],
        out_shape=[jax.ShapeDtypeStruct((T, MAIN), BF16), jax.ShapeDtypeStruct((T, GLA_H * DK), F32)],
        scratch_shapes=[pltpu.VMEM((DV, DK), F32)], input_output_aliases={8: 0}, compiler_params=_params(2),
    )(proj, proj, proj, proj_lr, w2p, gate_b, states, d_o, dproj)


def _after(values, tokens):
    return lax.optimization_barrier((values, tokens))[0]


def _local_step(x, target, p, reduce_part, reduce_ffn):
    u, u_t = _rms_fwd(x, p["norm1_w"], name="rms1_fwd")
    whole = _pick(x.shape[0], (2048, 256))
    proj = _matmul(u, p["w_in_main"], bm=whole, name="mm_proj")
    proj_lr = _matmul(u, p["w_in_lr"], name="mm_proj_lr")
    attn_o = _attn_fwd(proj, p["sinks"])
    gla_raw, states = _gla_fwd(proj, proj_lr, p["w2p"], p["gate_b"])
    merged, merged_t = _merge_fwd(attn_o, gla_raw, proj, p["gla_norm_w"])
    h1, v2, v2_t = _out_proj(merged, p["w_out"], x, p["norm2_w"])
    gate, up, ff, ff_t = _ffn_fwd(v2, p["w_gate"], p["w_up"])
    h2 = _matmul(ff, p["w_down"], add=h1, name="mm_down")
    loss, dh2, dh2_b, g_final = _final_loss(h2, p["final_norm_w"], target)

    dgate, dup = _ffn_bwd_hidden(dh2_b, p["w_down"], gate, up)
    sent = [reduce_ffn("w_down", ff_t, dh2_b), reduce_ffn("w_gate", v2_t, dgate), reduce_ffn("w_up", v2_t, dup)]
    dgate, dup = _after((dgate, dup), sent)
    dv2 = _ffn_bwd_input(dgate, dup, p["w_gate"], p["w_up"])
    dh1, dh1_b, g_norm2 = _rms_bwd(dv2, h1, p["norm2_w"], dh2, name="rms2_bwd")
    dmerged = _matmul(dh1_b, p["w_out"], tb=True, bj=D, name="mm_dmerged")
    sent = reduce_part("w_out", _matmul(merged_t, dh1_b, out_dtype=BF16, bj=D, name="mm_gout").reshape(4, D // 4, D))
    dmerged = _after(dmerged, sent)
    d_attn, d_gla, dproj, g_gla_norm = _merge_bwd(dmerged, attn_o, gla_raw, proj, p["gla_norm_w"])
    dproj, dlogit = _gla_bwd(proj, proj_lr, p["w2p"], p["gate_b"], states, d_gla, dproj)
    dproj, g_sinks = _attn_bwd(proj, p["sinks"], d_attn, dproj)
    g_gate_b = _colsum(dlogit, name="colsum_gate_b")
    g_w2 = _matmul(proj_lr, dlogit, ta=True, name="mm_gw2")
    dproj_lr = _matmul(dlogit, p["w2p"], tb=True, out_dtype=BF16, name="mm_dlr")
    g_in_main = _matmul(u_t, dproj, out_dtype=BF16, bm=D, name="mm_gin")
    g_in_lr = _matmul(u_t, dproj_lr, out_dtype=BF16, name="mm_gin_lr")
    du = _matmul(dproj, p["w_in_main"], tb=True, bm=_pick(x.shape[0], (1024, 256)), bj=1024, bc=3200, name="mm_du")
    dx, g_norm1 = _rms_bwd_input(du, dproj_lr, p["w_in_lr"], x, p["norm1_w"], dh1)
    grads = dict(norm1_w=g_norm1, w_in_main=g_in_main, w_in_lr=g_in_lr, w2=g_w2[:RANK], gate_b=g_gate_b,
                 sinks=g_sinks[:, 0].reshape(1, N_Q), gla_norm_w=g_gla_norm, norm2_w=g_norm2, final_norm_w=g_final)
    return loss, dx, grads


def _place():
    x, y, c = lax.axis_index("x"), lax.axis_index("y"), lax.axis_index("c")
    chips = [(1 - x, y), (x, 1 - y), (1 - x, 1 - y)]
    return x, y, c, chips


def _hbm_shape(s, dt):
    return jax.ShapeDtypeStruct(s, dt)


ID_SIBLING = 9
ID_GATHER_IN = 10


def _handshake(peers):
    barrier = pltpu.get_barrier_semaphore()
    for peer in peers:
        pl.semaphore_signal(barrier, inc=1, device_id=peer, device_id_type=MESH)
    pl.semaphore_wait(barrier, len(peers))


def _launch_copies(body, args, out_shape, sems, *, name, collective_id=None):
    if collective_id is None:
        return pl.pallas_call(
            body, name=name, in_specs=[ANY] * len(args), out_specs=[ANY] * len(out_shape), out_shape=out_shape,
            scratch_shapes=sems)(*args)
    return pl.kernel(
        body, name=name, out_type=out_shape, mesh=plsc.ScalarSubcoreMesh(axis_name="sequencer", num_cores=1),
        scratch_types=sems, compiler_params=pltpu.CompilerParams(collective_id=collective_id))(*args)


def _gather_shards(shards, *, name, collective_id=None, after=()):
    n = len(shards)
    first_out = n + len(after)

    def body(*refs):
        ins, outs = refs[:n], refs[first_out:first_out + n]
        ici_send, ici_recv, d2d_send, d2d_recv, local_sem = refs[first_out + n:]
        x, y, c, chips = _place()
        me = 2 * x + y
        sibling = (x, y, 1 - c)
        if collective_id is not None:
            _handshake([sibling] + [(*chip, c) for chip in chips])

        def half(w, slot, hc):
            r2 = shards[w].shape[0] // 2
            return outs[w].at[slot, pl.ds(hc * r2, r2), :]

        locals_ = [pltpu.make_async_copy(ins[w], outs[w].at[me], local_sem.at[w]) for w in range(n)]
        for cp in locals_:
            cp.start()
        sends = []
        for w in range(n):
            r2 = shards[w].shape[0] // 2
            for j, chip in enumerate(chips):
                cp = pltpu.make_async_remote_copy(
                    src_ref=ins[w].at[pl.ds(c * r2, r2), :], dst_ref=half(w, me, c),
                    send_sem=ici_send.at[w * 3 + j], recv_sem=ici_recv.at[w * 3 + j],
                    device_id=(*chip, c), device_id_type=MESH)
                cp.start()
                sends.append(cp)
        for w in range(n):
            for j, chip in enumerate(chips):
                slot = 2 * chip[0] + chip[1]
                got = half(w, slot, c)
                pltpu.make_async_remote_copy(
                    src_ref=got, dst_ref=got, send_sem=ici_send.at[w * 3 + j], recv_sem=ici_recv.at[w * 3 + j],
                    device_id=(*chip, c), device_id_type=MESH).wait_recv()
                cp = pltpu.make_async_remote_copy(
                    src_ref=got, dst_ref=got, send_sem=d2d_send.at[w * 3 + j], recv_sem=d2d_recv.at[w * 3 + j],
                    device_id=sibling, device_id_type=MESH)
                cp.start()
                sends.append(cp)
        for w in range(n):
            for j, chip in enumerate(chips):
                slot = 2 * chip[0] + chip[1]
                got = half(w, slot, 1 - c)
                pltpu.make_async_remote_copy(
                    src_ref=got, dst_ref=got, send_sem=d2d_send.at[w * 3 + j], recv_sem=d2d_recv.at[w * 3 + j],
                    device_id=sibling, device_id_type=MESH).wait_recv()
        for cp in sends:
            cp.wait_send()
        for cp in locals_:
            cp.wait()

    return _launch_copies(
        body, list(shards) + list(after), [_hbm_shape((4,) + s.shape, s.dtype) for s in shards],
        [pltpu.SemaphoreType.DMA((3 * n,)), pltpu.SemaphoreType.DMA((3 * n,)), pltpu.SemaphoreType.DMA((3 * n,)),
         pltpu.SemaphoreType.DMA((3 * n,)), pltpu.SemaphoreType.DMA((n,))],
        name=name, collective_id=collective_id)


def _gather_staged(w, small):
    R, C = w.shape
    R2 = R // 2
    rb = 256
    per = R2 // rb
    r2 = small.shape[0] // 2

    def body(w_ref, s_ref, out_ref, outs_ref, stage, ici_send, ici_recv, fwd_send, fwd_recv, in_sem, put_sem,
             push_send, push_recv):
        x, y, c, chips = _place()
        me = 2 * x + y
        sibling = (x, y, 1 - c)
        slots = [2 * chip[0] + chip[1] for chip in chips]
        _handshake([sibling] + [(*chip, c) for chip in chips])

        def ici(j, src, dst, k):
            return pltpu.make_async_remote_copy(
                src_ref=src, dst_ref=dst, send_sem=ici_send.at[k], recv_sem=ici_recv.at[k],
                device_id=(*chips[j], c), device_id_type=MESH)

        sends = []
        for j in range(3):
            sends.append(ici(j, w_ref.at[pl.ds(c * R2, R2), :], out_ref.at[me, pl.ds(c * R2, R2), :], j))
            sends.append(ici(j, s_ref.at[pl.ds(c * r2, r2), :], outs_ref.at[me, pl.ds(c * r2, r2), :], 3 + j))
        for cp in sends:
            cp.start()
        mine = pltpu.make_async_copy(s_ref, outs_ref.at[me], put_sem.at[2])
        mine.start()

        def own(k, to_vmem):
            rows = pl.ds(k * rb, rb)
            if to_vmem:
                return pltpu.make_async_copy(w_ref.at[rows, :], stage.at[k % 2], in_sem.at[k % 2])
            return pltpu.make_async_copy(stage.at[k % 2], out_ref.at[me, rows, :], put_sem.at[k % 2])

        for k in range(R // rb):
            if k >= 2:
                own(k - 2, False).wait()
            own(k, True).start()
            own(k, True).wait()
            own(k, False).start()
        for k in range(R // rb - 2, R // rb):
            own(k, False).wait()

        def block(j, k, hc):
            return out_ref.at[slots[j], pl.ds(hc * R2 + k * rb, rb), :]

        def push(t, hc):
            j, k = divmod(t, per)
            return pltpu.make_async_remote_copy(
                src_ref=stage.at[t % 2], dst_ref=block(j, k, hc), send_sem=push_send.at[t % 2],
                recv_sem=push_recv.at[t], device_id=sibling, device_id_type=MESH)

        passed = []
        for j in range(3):
            ici(j, w_ref.at[pl.ds(c * R2, R2), :], out_ref.at[slots[j], pl.ds(c * R2, R2), :], j).wait_recv()
            for k in range(per):
                t = j * per + k
                if t >= 2:
                    push(t - 2, c).wait_send()
                fetch = pltpu.make_async_copy(block(j, k, c), stage.at[t % 2], in_sem.at[t % 2])
                fetch.start()
                fetch.wait()
                push(t, c).start()
            got = outs_ref.at[slots[j], pl.ds(c * r2, r2), :]
            ici(j, got, got, 3 + j).wait_recv()
            fw = pltpu.make_async_remote_copy(
                src_ref=got, dst_ref=got, send_sem=fwd_send.at[j], recv_sem=fwd_recv.at[j],
                device_id=sibling, device_id_type=MESH)
            fw.start()
            passed.append(fw)
        for t in range(3 * per - 2, 3 * per):
            push(t, c).wait_send()
        for t in range(3 * per):
            push(t, 1 - c).wait_recv()
        for j in range(3):
            theirs = outs_ref.at[slots[j], pl.ds((1 - c) * r2, r2), :]
            pltpu.make_async_remote_copy(
                src_ref=theirs, dst_ref=theirs, send_sem=fwd_send.at[j], recv_sem=fwd_recv.at[j],
                device_id=sibling, device_id_type=MESH).wait_recv()
        for cp in sends + passed:
            cp.wait_send()
        mine.wait()

    dma = pltpu.SemaphoreType.DMA
    return pl.pallas_call(
        body, name="gather_in", in_specs=[ANY, ANY], out_specs=[ANY, ANY],
        out_shape=[_hbm_shape((4, R, C), w.dtype), _hbm_shape((4,) + small.shape, small.dtype)],
        scratch_shapes=[pltpu.VMEM((2, rb, C), w.dtype), dma((6,)), dma((6,)), dma((3,)), dma((3,)), dma((2,)),
                        dma((3,)), dma((2,)), dma((3 * per,))],
        compiler_params=pltpu.CompilerParams(vmem_limit_bytes=VMEM_LIMIT, collective_id=ID_GATHER_IN),
    )(w, small)


def _pair_blocks(R2):
    return _pick(R2, (256, 352, 128, 64, 32, 16))


def _grad_matmul_split(a, b, *, by_rows, name):
    M, T = a.shape
    J = b.shape[1]
    if by_rows:
        R, C, bm = M // 4, J, 352
        per = R // 2 // bm
        n = M // bm

        def where(t):
            return t // (2 * per), (t // per) % 2, t % per

        a_spec = pl.BlockSpec((bm, T), lambda t: (t, 0))
        b_spec = pl.BlockSpec((T, C), lambda t: (0, 0))
    else:
        R, C, bm = M, J // 4, M // 2
        per, n = 1, 8

        def where(t):
            return t // 2, t % 2, 0

        a_spec = pl.BlockSpec((bm, T), lambda t: (t % 2, 0))
        b_spec = pl.BlockSpec((T, C), lambda t: (0, t // 2))
    R2 = R // 2

    def body(a_ref, b_ref, mine_ref, got_ref, tile, loc_sem, snd_sem, rcv_sem):
        x, y, c, _ = _place()
        sibling = (x, y, 1 - c)
        t = pl.program_id(0)

        @pl.when(t == 0)
        def _():
            _handshake([sibling])

        def keep(u):
            s, _, i = where(u)
            return pltpu.make_async_copy(tile.at[u % 2], mine_ref.at[s, pl.ds(i * bm, bm), :], loc_sem.at[u % 2])

        def give(u):
            s, _, i = where(u)
            return pltpu.make_async_remote_copy(
                src_ref=tile.at[u % 2], dst_ref=got_ref.at[s, pl.ds(i * bm, bm), :], send_sem=snd_sem.at[u % 2],
                recv_sem=rcv_sem, device_id=sibling, device_id_type=MESH)

        def settle(u):
            @pl.when(where(u)[1] == c)
            def _():
                keep(u).wait()

            @pl.when(where(u)[1] != c)
            def _():
                give(u).wait_send()

        @pl.when(t >= 2)
        def _():
            settle(t - 2)

        tile[t % 2] = _dot(a_ref[...], b_ref[...], 1, 0).astype(BF16)

        @pl.when(where(t)[1] == c)
        def _():
            keep(t).start()

        @pl.when(where(t)[1] != c)
        def _():
            give(t).start()

        @pl.when(t == n - 1)
        def _():
            settle(t - 1)
            settle(t)
            pltpu.make_async_remote_copy(
                src_ref=got_ref, dst_ref=got_ref, send_sem=snd_sem.at[0], recv_sem=rcv_sem,
                device_id=sibling, device_id_type=MESH).wait_recv()

    half = _hbm_shape((4, R2, C), BF16)
    return pl.pallas_call(
        body, name=name, grid=(n,), in_specs=[a_spec, b_spec], out_specs=[ANY, ANY], out_shape=[half, half],
        scratch_shapes=[pltpu.VMEM((2, bm, C), BF16), pltpu.SemaphoreType.DMA((2,)), pltpu.SemaphoreType.DMA((2,)),
                        pltpu.SemaphoreType.DMA],
        compiler_params=pltpu.CompilerParams(
            dimension_semantics=("arbitrary",), vmem_limit_bytes=VMEM_LIMIT, collective_id=ID_SIBLING),
    )(a, b)


def _pair_add(mine, got, *, name):
    _, R2, C = mine.shape
    rb = _pick(R2, (512, 352, 256))

    def body(a_ref, b_ref, o_ref):
        o_ref[...] = (a_ref[...].astype(F32) + b_ref[...].astype(F32)).astype(BF16)

    blk = pl.BlockSpec((1, rb, C), lambda s, i: (s, i, 0))
    return pl.pallas_call(
        body, name=name, grid=(4, R2 // rb), in_specs=[blk, blk], out_specs=blk,
        out_shape=jax.ShapeDtypeStruct(mine.shape, BF16), compiler_params=_params(2),
    )(mine, got)


def _pair_reduce(part, *, name):
    _, R, C = part.shape
    R2 = R // 2
    rb = _pick(R2, (512, 352, 256, 128, 64, 32, 16))
    nblk = R2 // rb
    steps = [(s, i) for s in range(4) for i in range(nblk)]
    n = len(steps)

    def body(part_ref, sums_ref, own_buf, oth_buf, rcv_buf, out_buf, own_sem, oth_sem, out_sem, send_sem, recv_sem):
        x, y, c, _ = _place()
        sibling = (x, y, 1 - c)
        _handshake([sibling])

        def fetch(t, half, buf, sem):
            s, i = steps[t]
            return pltpu.make_async_copy(part_ref.at[s, pl.ds(half * R2 + i * rb, rb), :], buf.at[t % 3], sem.at[t % 3])

        def push(t):
            return pltpu.make_async_remote_copy(
                src_ref=oth_buf.at[t % 3], dst_ref=rcv_buf.at[t % 4], send_sem=send_sem.at[t % 4],
                recv_sem=recv_sem.at[t % 4], device_id=sibling, device_id_type=MESH)

        def store(t):
            s, i = steps[t]
            return pltpu.make_async_copy(out_buf.at[t % 2], sums_ref.at[s, pl.ds(i * rb, rb), :], out_sem.at[t % 2])

        for t in range(min(2, n)):
            fetch(t, c, own_buf, own_sem).start()
            fetch(t, 1 - c, oth_buf, oth_sem).start()
        fetch(0, 1 - c, oth_buf, oth_sem).wait()
        push(0).start()
        for t in range(n):
            if t + 2 < n:
                fetch(t + 2, c, own_buf, own_sem).start()
                fetch(t + 2, 1 - c, oth_buf, oth_sem).start()
            if t + 1 < n:
                fetch(t + 1, 1 - c, oth_buf, oth_sem).wait()
                push(t + 1).start()
            fetch(t, c, own_buf, own_sem).wait()
            push(t).wait()
            if t >= 2:
                store(t - 2).wait()
            out_buf[t % 2] = (own_buf[t % 3].astype(F32) + rcv_buf[t % 4].astype(F32)).astype(BF16)
            store(t).start()
        for t in range(max(n - 2, 0), n):
            store(t).wait()

    def buf(k):
        return pltpu.VMEM((k, rb, C), BF16)

    def sems(k):
        return pltpu.SemaphoreType.DMA((k,))

    return pl.pallas_call(
        body, name=name, in_specs=[ANY], out_specs=ANY, out_shape=_hbm_shape((4, R2, C), BF16),
        scratch_shapes=[buf(3), buf(3), buf(4), buf(2), sems(3), sems(3), sems(2), sems(4), sems(4)],
        compiler_params=pltpu.CompilerParams(vmem_limit_bytes=VMEM_LIMIT, collective_id=ID_SIBLING),
    )(part)


def _chip_exchange(sums, *, name, collective_id=None):
    n = len(sums)

    def body(*refs):
        ins, outs = refs[:n], refs[n:2 * n]
        send_sem, recv_sem, local_sem = refs[2 * n:]
        x, y, c, chips = _place()
        me = 2 * x + y
        if collective_id is not None:
            _handshake([(*chip, c) for chip in chips])
        cps = []
        for w in range(n):
            lc = pltpu.make_async_copy(ins[w].at[me], outs[w].at[me], local_sem.at[w])
            lc.start()
            cps.append(lc)
            for j, chip in enumerate(chips):
                slot = 2 * chip[0] + chip[1]
                rc = pltpu.make_async_remote_copy(
                    src_ref=ins[w].at[slot], dst_ref=outs[w].at[me],
                    send_sem=send_sem.at[w * 3 + j], recv_sem=recv_sem.at[w * 3 + j],
                    device_id=(*chip, c), device_id_type=MESH)
                rc.start()
                cps.append(rc)
        for w in range(n):
            for j, chip in enumerate(chips):
                slot = 2 * chip[0] + chip[1]
                pltpu.make_async_remote_copy(
                    src_ref=ins[w].at[slot], dst_ref=outs[w].at[slot],
                    send_sem=send_sem.at[w * 3 + j], recv_sem=recv_sem.at[w * 3 + j],
                    device_id=(*chip, c), device_id_type=MESH).wait_recv()
        for w in range(n):
            cps[w * 4].wait()
            for j in range(3):
                cps[w * 4 + 1 + j].wait_send()

    return _launch_copies(
        body, sums, [_hbm_shape(s.shape, s.dtype) for s in sums],
        [pltpu.SemaphoreType.DMA((3 * n,)), pltpu.SemaphoreType.DMA((3 * n,)), pltpu.SemaphoreType.DMA((n,))],
        name=name, collective_id=collective_id)


def _sum_join(recv, *, name):
    _, R2, C = recv.shape
    rb = _pair_blocks(R2)
    nblk = R2 // rb

    def body(recv_ref, out_ref, in_buf, acc_buf, in_sem, loc_sem, send_sem, recv_sem):
        x, y, c, _ = _place()
        sibling = (x, y, 1 - c)
        _handshake([sibling])

        def fetch(t):
            return pltpu.make_async_copy(recv_ref.at[:, pl.ds(t * rb, rb), :], in_buf.at[t % 2], in_sem.at[t % 2])

        def rows(t, half):
            return out_ref.at[pl.ds(half * R2 + t * rb, rb), :]

        def put_local(t):
            return pltpu.make_async_copy(acc_buf.at[t % 2], rows(t, c), loc_sem.at[t])

        def put_remote(t, half):
            return pltpu.make_async_remote_copy(
                src_ref=acc_buf.at[t % 2], dst_ref=rows(t, half), send_sem=send_sem.at[t], recv_sem=recv_sem.at[t],
                device_id=sibling, device_id_type=MESH)

        fetch(0).start()
        for t in range(nblk):
            if t + 1 < nblk:
                fetch(t + 1).start()
            fetch(t).wait()
            if t >= 2:
                put_local(t - 2).wait()
                put_remote(t - 2, c).wait_send()
            acc = in_buf[t % 2, 0].astype(F32)
            for s in range(1, 4):
                acc = acc + in_buf[t % 2, s].astype(F32)
            acc_buf[t % 2] = acc
            put_local(t).start()
            put_remote(t, c).start()
        for t in range(max(nblk - 2, 0), nblk):
            put_local(t).wait()
            put_remote(t, c).wait_send()
        for t in range(nblk):
            put_remote(t, 1 - c).wait_recv()

    semn = pltpu.SemaphoreType.DMA((nblk,))
    return pl.pallas_call(
        body, name=name, in_specs=[ANY], out_specs=ANY, out_shape=_hbm_shape((2 * R2, C), F32),
        scratch_shapes=[pltpu.VMEM((2, 4, rb, C), BF16), pltpu.VMEM((2, rb, C), F32),
                        pltpu.SemaphoreType.DMA((2,)), semn, semn, semn],
        compiler_params=pltpu.CompilerParams(vmem_limit_bytes=VMEM_LIMIT, collective_id=ID_SIBLING),
    )(recv)


def _sum_small(pack):
    R, C = pack.shape

    def body(in_ref, out_ref, all_ref, send_sem, recv_sem):
        x, y, c, _ = _place()
        me = 4 * x + 2 * y + c
        all_ref[me] = in_ref[...]
        cps = []
        for k in range(1, 8):
            peer = (x ^ (k >> 2), y ^ ((k >> 1) & 1), c ^ (k & 1))
            cp = pltpu.make_async_remote_copy(
                src_ref=in_ref, dst_ref=all_ref.at[me], send_sem=send_sem.at[k - 1], recv_sem=recv_sem.at[k - 1],
                device_id=peer, device_id_type=MESH)
            cp.start()
            cps.append(cp)
        for k in range(1, 8):
            peer = (x ^ (k >> 2), y ^ ((k >> 1) & 1), c ^ (k & 1))
            slot = 4 * peer[0] + 2 * peer[1] + peer[2]
            pltpu.make_async_remote_copy(
                src_ref=in_ref, dst_ref=all_ref.at[slot], send_sem=send_sem.at[k - 1], recv_sem=recv_sem.at[k - 1],
                device_id=peer, device_id_type=MESH).wait_recv()
        for cp in cps:
            cp.wait_send()
        acc = all_ref[0]
        for d in range(1, 8):
            acc = acc + all_ref[d]
        out_ref[...] = acc

    return pl.pallas_call(
        body, name="sum_small", in_specs=[pl.BlockSpec(memory_space=pltpu.VMEM)],
        out_specs=pl.BlockSpec(memory_space=pltpu.VMEM), out_shape=jax.ShapeDtypeStruct((R, C), F32),
        scratch_shapes=[pltpu.VMEM((8, R, C), F32), pltpu.SemaphoreType.DMA((7,)), pltpu.SemaphoreType.DMA((7,))],
    )(pack)


ADAMW_BLOCK_BYTES = 2 * 1024 * 1024


def _adamw_block(R, C):
    padded = -(-C // 128) * 128
    rows = [rb for rb in range(8, R + 1, 8) if R % rb == 0 and rb * padded * 4 <= ADAMW_BLOCK_BYTES]
    if rows or R * padded * 4 <= ADAMW_BLOCK_BYTES:
        return (max(rows) if rows else R), C
    cols = [cb for cb in range(128, C + 1, 128) if C % cb == 0 and R * cb * 4 <= ADAMW_BLOCK_BYTES]
    return R, max(cols)


def _adamw(w, g, m, v, *, name, pass_grad=False):
    R, C = w.shape
    rb, cb = _adamw_block(R, C)
    c1 = 1.0 / (1.0 - B1 ** STEP)
    c2 = 1.0 / (1.0 - B2 ** STEP)

    def body(w_ref, g_ref, m_ref, v_ref, *outs):
        d_ref, nm_ref, nv_ref = outs[-3:]
        gv = g_ref[...]
        if pass_grad:
            outs[0][...] = gv
        m2 = B1 * m_ref[...] + (1.0 - B1) * gv
        v2 = B2 * v_ref[...] + (1.0 - B2) * gv * gv
        nm_ref[...] = m2
        nv_ref[...] = v2
        d_ref[...] = -LR * ((m2 * c1) / (jnp.sqrt(v2 * c2) + ADAM_EPS) + WD * w_ref[...])

    n_out = 4 if pass_grad else 3
    blk = pl.BlockSpec((rb, cb), lambda i, j: (i, j))
    return pl.pallas_call(
        body, name=name, grid=(R // rb, C // cb), in_specs=[blk] * 4, out_specs=[blk] * n_out,
        out_shape=[jax.ShapeDtypeStruct((R, C), F32)] * n_out, compiler_params=_params(2),
    )(w, g, m, v)


SMALL = (("norm1_w", D), ("norm2_w", D), ("final_norm_w", D), ("gate_b", GLA_H * DK), ("gla_norm_w", DV), ("sinks", N_Q))
PACK_W = 1024


def _pack_small(vals, w2, loss):
    rows = []
    for name, width in SMALL:
        v = vals[name].reshape(-1)
        rows.append(jnp.pad(v, (0, (-width) % PACK_W)).reshape(-1, PACK_W))
    rows.append(w2)
    rows.append(jnp.broadcast_to(loss.reshape(1, 1), (1, PACK_W)))
    pack = jnp.concatenate(rows, axis=0)
    return jnp.pad(pack, ((0, 32 - pack.shape[0]), (0, 0)))


def _unpack_small(pack):
    out, r = {}, 0
    for name, width in SMALL:
        nr = -(-width // PACK_W)
        out[name] = pack[r:r + nr].reshape(-1)[:width]
        r += nr
    out["w2"] = pack[r:r + RANK]
    out["loss"] = pack[r + RANK, 0]
    return out


def kernel(x, norm1_w, w_in, gla_gate_w2, gla_gate_b, attn_sinks, gla_norm_w, w_out, norm2_w, w_ffn_gate, w_ffn_up, w_ffn_down, final_norm_w, loss_target, m_norm1_w, m_w_in, m_gla_gate_w2, m_gla_gate_b, m_attn_sinks, m_gla_norm_w, m_w_out, m_norm2_w, m_w_ffn_gate, m_w_ffn_up, m_w_ffn_down, m_final_norm_w, v_norm1_w, v_w_in, v_gla_gate_w2, v_gla_gate_b, v_attn_sinks, v_gla_norm_w, v_w_out, v_norm2_w, v_w_ffn_gate, v_w_ffn_up, v_w_ffn_down, v_final_norm_w):
    chip = 2 * lax.axis_index("x") + lax.axis_index("y")
    w_in_s, w_out_s, w_gate_s, w_up_s, w_down_s, w2_s = (
        w_in[0], w_out[0], w_ffn_gate[0], w_ffn_up[0], w_ffn_down[0], gla_gate_w2[0])
    CS = SHARD_COLS

    g_in, g_w2 = _gather_staged(w_in_s.astype(BF16), w2_s)
    g_out, = _gather_shards([w_out_s.astype(BF16)], name="gather_out", collective_id=1, after=[g_w2])
    g_gate, g_up = _gather_shards([w_gate_s.astype(BF16), w_up_s.astype(BF16)], name="gather_gate_up",
                                  collective_id=7, after=[g_w2])
    g_down, = _gather_shards([w_down_s.astype(BF16)], name="gather_down", collective_id=8, after=[g_w2])
    w_in_t = lax.optimization_barrier(jnp.transpose(w_in_s))
    m_in_t = lax.optimization_barrier(jnp.transpose(m_w_in[0]))
    v_in_t = lax.optimization_barrier(jnp.transpose(v_w_in[0]))
    w_main, w_lr = _w_in_to_main(_after(g_in, (w_in_t, m_in_t, v_in_t)))
    w2_full = jnp.transpose(g_w2, (1, 0, 2)).reshape(RANK, GLA_H * DK)
    p = dict(
        norm1_w=norm1_w, norm2_w=norm2_w, final_norm_w=final_norm_w.reshape(1, D), gate_b=gla_gate_b,
        gla_norm_w=gla_norm_w, sinks=jnp.pad(attn_sinks, ((0, 0), (0, 128 - N_Q))),
        w_in_main=w_main, w_in_lr=w_lr,
        w2p=jnp.pad(w2_full, ((0, LRP - RANK), (0, 0))).astype(BF16),
        w_out=g_out.reshape(D, D),
        w_gate=g_gate, w_up=g_up, w_down=g_down.reshape(FFN, D),
    )

    tags = ["w_in", "w_out", "w_gate", "w_up", "w_down"]
    recv = {}

    def exchange(tag, sums):
        recv[tag], = _chip_exchange([sums], name="chip_exchange_" + tag, collective_id=2 + tags.index(tag))
        return sums

    def reduce_part(tag, part):
        return exchange(tag, _pair_reduce(part, name="pair_reduce_" + tag))

    def reduce_ffn(tag, a, b):
        mine, got = _grad_matmul_split(a, b, by_rows=(tag == "w_down"), name="mm_grad_" + tag)
        return exchange(tag, _pair_add(mine, got, name="pair_add_" + tag))

    done = {}
    shard = {"w_out": (w_out_s, m_w_out[0], v_w_out[0]), "w_gate": (w_gate_s, m_w_ffn_gate[0], v_w_ffn_gate[0]),
             "w_up": (w_up_s, m_w_ffn_up[0], v_w_ffn_up[0]), "w_down": (w_down_s, m_w_ffn_down[0], v_w_ffn_down[0])}

    def finish(tag):
        grad = _sum_join(recv[tag], name="sum_join_" + tag)
        done[tag] = _adamw(shard[tag][0], grad, shard[tag][1], shard[tag][2], name="adamw_" + tag, pass_grad=True)

    loss_blk, dx, g = _local_step(x[0], loss_target[0], p, reduce_part, reduce_ffn)
    reduce_part("w_in", _main_to_shards(g["w_in_main"], g["w_in_lr"]))
    for tag in ("w_down", "w_gate", "w_up", "w_out"):
        finish(tag)
    behind = [dx] + [done[tag][1] for tag in ("w_down", "w_gate", "w_up", "w_out")]
    g_in = _sum_join(_after(recv["w_in"], behind), name="sum_join_w_in")

    small = _unpack_small(_sum_small(_pack_small(g, g["w2"], loss_blk[0, 0])))
    loss = small["loss"]
    g_w2_mine = lax.dynamic_slice_in_dim(small["w2"], chip * (GLA_H * DK // 4), GLA_H * DK // 4, axis=1)

    grads = dict(
        norm1_w=small["norm1_w"].reshape(1, D), w_in=g_in, gla_gate_w2=g_w2_mine,
        gla_gate_b=small["gate_b"].reshape(1, -1), attn_sinks=small["sinks"].reshape(1, N_Q),
        gla_norm_w=small["gla_norm_w"].reshape(1, DV), norm2_w=small["norm2_w"].reshape(1, D),
        final_norm_w=small["final_norm_w"].reshape(1, D))
    def lin(a):
        return jnp.transpose(a.reshape(D, CS))

    def unlin(a):
        return jnp.transpose(a)

    grads["w_in"] = lin(g_in)
    weights = dict(
        norm1_w=(norm1_w, m_norm1_w, v_norm1_w), w_in=(w_in_t, m_in_t, v_in_t),
        gla_gate_w2=(w2_s, m_gla_gate_w2[0], v_gla_gate_w2[0]), gla_gate_b=(gla_gate_b, m_gla_gate_b, v_gla_gate_b),
        attn_sinks=(attn_sinks, m_attn_sinks, v_attn_sinks), gla_norm_w=(gla_norm_w, m_gla_norm_w, v_gla_norm_w),
        norm2_w=(norm2_w, m_norm2_w, v_norm2_w),
        final_norm_w=(final_norm_w.reshape(1, D), m_final_norm_w.reshape(1, D), v_final_norm_w.reshape(1, D)))
    names = ["norm1_w", "w_in", "gla_gate_w2", "gla_gate_b", "attn_sinks", "gla_norm_w", "w_out", "norm2_w",
             "w_ffn_gate", "w_ffn_up", "w_ffn_down", "final_norm_w"]
    lead = {"norm1_w": False, "gla_gate_b": False, "attn_sinks": False, "gla_norm_w": False, "norm2_w": False}
    g_out_l, d_out, m_out, v_out = [], [], [], []
    early = {"w_out": "w_out", "w_ffn_gate": "w_gate", "w_ffn_up": "w_up", "w_ffn_down": "w_down"}
    for nm in names:
        if nm in early:
            gr, dl, nmn, nvn = done[early[nm]]
            w = shard[early[nm]][0]
        else:
            w, m, v = weights[nm]
            gr = grads[nm]
            dl, nmn, nvn = _adamw(w, gr, m, v, name="adamw_" + nm)
        if nm == "w_in":
            gr, dl, nmn, nvn, w = g_in, unlin(dl), unlin(nmn), unlin(nvn), w_in_s
        if nm == "final_norm_w":
            shape = (D,)
        elif nm in lead:
            shape = w.shape
        else:
            shape = (1,) + w.shape
        g_out_l.append(gr.reshape(shape))
        d_out.append(dl.reshape(shape))
        m_out.append(nmn.reshape(shape))
        v_out.append(nvn.reshape(shape))
    return (loss, dx[None], *g_out_l, *d_out, *m_out, *v_out)
```

```python
import functools

import jax
import jax.numpy as jnp
from jax import lax
from jax.experimental import pallas as pl
from jax.experimental.pallas import tpu as pltpu
from jax.experimental.pallas import tpu_sc as plsc

F32 = jnp.float32
BF16 = jnp.bfloat16

D = 2048
HEAD_DIM = 64
N_Q = 32
N_KV = 4
GROUP = 8
WINDOW = 128
GLA_H = 4
DK = 256
DV = 512
RANK = 16
CHUNK = 64
FFN = 5632
EPS = 1e-6
MASK_VALUE = -1e30
GATE_NORM = 16.0
LR, B1, B2, ADAM_EPS, WD, STEP = 0.001, 0.9, 0.999, 1e-08, 0.01, 10

MAIN = 12800
O_MERGE, O_GLA, O_ATTN = 0, 6144, 10240
W_MERGE, W_GLA, W_ATTN = 3 * DV, 2 * DK + DV, D + 2 * N_KV * HEAD_DIM
LR_AT = 6656
LRP = 128


def _main_pieces():
    o_gq, o_gk, o_gv, o_gr, o_ga, o_gb = 2560, 3584, 4608, LR_AT + RANK, LR_AT + RANK + D, LR_AT + RANK + 2 * D
    pieces = []
    for h in range(GLA_H):
        pieces += [(o_gr + DV * h, DV), (o_ga + DV * h, DV), (o_gb + DV * h, DV)]
    for h in range(GLA_H):
        pieces += [(o_gq + DK * h, DK), (o_gk + DK * h, DK), (o_gv + DV * h, DV)]
    pieces.append((0, W_ATTN))
    return pieces


SHARD_COLS = 3204

VMEM_LIMIT = 56 * 1024 * 1024
MESH = pl.DeviceIdType.MESH
ANY = pl.BlockSpec(memory_space=pl.ANY)


def _params(ngrid):
    return pltpu.CompilerParams(dimension_semantics=("arbitrary",) * ngrid, vmem_limit_bytes=VMEM_LIMIT)


def _shard_segments(lo, hi):
    segs = []
    while lo < hi:
        s = lo // SHARD_COLS
        e = min(hi, (s + 1) * SHARD_COLS)
        segs.append((s, lo - s * SHARD_COLS, e - s * SHARD_COLS))
        lo = e
    return segs


def _lanes(pieces):
    return pieces[0] if len(pieces) == 1 else jnp.concatenate(pieces, axis=1)


def _w_in_to_main(shards):
    rb = 256

    def body(g_ref, main_ref, lr_ref):
        at = 0
        for a, w in _main_pieces():
            main_ref[:, at:at + w] = _lanes([g_ref[s, :, lo:hi] for s, lo, hi in _shard_segments(a, a + w)])
            at += w
        lr = [g_ref[s, :, lo:hi] for s, lo, hi in _shard_segments(LR_AT, LR_AT + RANK)]
        lr_ref[...] = _lanes(lr + [jnp.zeros((rb, LRP - RANK), lr_ref.dtype)])

    return pl.pallas_call(
        body, name="w_in_to_main", grid=(D // rb,),
        in_specs=[pl.BlockSpec((4, rb, SHARD_COLS), lambda i: (0, i, 0))],
        out_specs=[pl.BlockSpec((rb, MAIN), lambda i: (i, 0)), pl.BlockSpec((rb, LRP), lambda i: (i, 0))],
        out_shape=[jax.ShapeDtypeStruct((D, MAIN), shards.dtype), jax.ShapeDtypeStruct((D, LRP), shards.dtype)],
        compiler_params=_params(1),
    )(shards)


def _main_to_shards(g_main, g_lr):
    rb = 256
    rows = g_main.shape[0]
    at, sources = 0, [(LR_AT, RANK, 1, 0)]
    for a, w in _main_pieces():
        sources.append((a, w, 0, at))
        at += w
    sources.sort()

    def body(main_ref, lr_ref, out_ref):
        refs = (main_ref, lr_ref)
        for s in range(4):
            lo, hi = s * SHARD_COLS, (s + 1) * SHARD_COLS
            pieces = []
            for a, w, which, src in sources:
                b, e = max(a, lo), min(a + w, hi)
                if b < e:
                    pieces.append(refs[which][:, src + b - a:src + e - a])
            out_ref[s] = _lanes(pieces)

    return pl.pallas_call(
        body, name="main_to_shards", grid=(rows // rb,),
        in_specs=[pl.BlockSpec((rb, MAIN), lambda i: (i, 0)), pl.BlockSpec((rb, LRP), lambda i: (i, 0))],
        out_specs=pl.BlockSpec((4, rb, SHARD_COLS), lambda i: (0, i, 0)),
        out_shape=jax.ShapeDtypeStruct((4, rows, SHARD_COLS), g_main.dtype), compiler_params=_params(1),
    )(g_main, g_lr)


def _pick(n, cands):
    for c in cands:
        if n % c == 0:
            return c
    return n


def _sigmoid(x):
    return 1.0 / (1.0 + jnp.exp(-x))


def _dot(a, b, ca, cb):
    return lax.dot_general(a, b, (((ca,), (cb,)), ((), ())), preferred_element_type=F32)


def _matmul(a, b, *, name, ta=False, tb=False, add=None, out_dtype=F32, bm=None, bj=None, bc=None,
            b_slots=False, out_slots=False):
    C, M = a.shape if ta else a.shape[::-1]
    if b_slots:
        if tb:
            J, bc = b.shape[1], b.shape[2]
            assert b.shape[0] * bc == C
        else:
            bj = b.shape[2]
            J = b.shape[0] * bj
            assert b.shape[1] == C
    else:
        J = b.shape[0] if tb else b.shape[1]
        assert (b.shape[1] if tb else b.shape[0]) == C
    bm = bm or _pick(M, (1024, 512, 256, 128) if C <= FFN else (512, 256, 128))
    bj = bj or _pick(J, (1280, 1024, 512, 256, 128) if C <= D else (512, 256, 128))
    bc = bc or (C if C <= FFN else _pick(C, (6400,)))
    nm, nj, nc = M // bm, J // bj, C // bc
    has_add = add is not None

    def body(*refs):
        a_ref, b_ref = refs[0], refs[1]
        add_ref = refs[2] if has_add else None
        o_ref = refs[3] if has_add else refs[2]
        p = _dot(a_ref[...].astype(BF16), b_ref[...].astype(BF16), 0 if ta else 1, 1 if tb else 0)

        def finish(acc):
            if has_add:
                acc = acc + add_ref[...]
            o_ref[...] = acc.astype(o_ref.dtype)

        if nc == 1:
            finish(p)
        else:
            acc_ref = refs[-1]
            c = pl.program_id(2)

            @pl.when(c == 0)
            def _():
                acc_ref[...] = p

            @pl.when(c > 0)
            def _():
                acc_ref[...] += p

            @pl.when(c == nc - 1)
            def _():
                finish(acc_ref[...])

    a_spec = pl.BlockSpec((bc, bm), lambda m, j, c: (c, m)) if ta else pl.BlockSpec((bm, bc), lambda m, j, c: (m, c))
    if b_slots:
        b_spec = (pl.BlockSpec((None, bj, bc), lambda m, j, c: (c, j, 0)) if tb
                  else pl.BlockSpec((None, bc, bj), lambda m, j, c: (j, c, 0)))
    else:
        b_spec = (pl.BlockSpec((bj, bc), lambda m, j, c: (j, c)) if tb
                  else pl.BlockSpec((bc, bj), lambda m, j, c: (c, j)))
    if out_slots:
        assert not has_add
        o_spec = pl.BlockSpec((None, bm, bj), lambda m, j, c: (j, m, 0))
        out_shape = jax.ShapeDtypeStruct((nj, M, bj), out_dtype)
    else:
        o_spec = pl.BlockSpec((bm, bj), lambda m, j, c: (m, j))
        out_shape = jax.ShapeDtypeStruct((M, J), out_dtype)
    in_specs = [a_spec, b_spec] + ([o_spec] if has_add else [])
    args = (a, b) + ((add,) if has_add else ())
    return pl.pallas_call(
        body, name=name, grid=(nm, nj, nc), in_specs=in_specs, out_specs=o_spec,
        out_shape=out_shape,
        scratch_shapes=[pltpu.VMEM((bm, bj), F32)] if nc > 1 else [],
        compiler_params=_params(3),
    )(*args)


def _rows(T):
    return _pick(T, (256, 128))


def _rms_fwd(x, w, *, name):
    T = x.shape[0]
    rb = _rows(T)

    def body(x_ref, w_ref, u_ref, ut_ref):
        xv = x_ref[...]
        r = lax.rsqrt(jnp.mean(xv * xv, axis=-1, keepdims=True) + EPS)
        u = xv * r * w_ref[...]
        u_ref[...] = u.astype(BF16)
        ut_ref[...] = u.T.astype(BF16)

    return pl.pallas_call(
        body, name=name, grid=(T // rb,),
        in_specs=[pl.BlockSpec((rb, D), lambda i: (i, 0)), pl.BlockSpec((1, D), lambda i: (0, 0))],
        out_specs=[pl.BlockSpec((rb, D), lambda i: (i, 0)), pl.BlockSpec((D, rb), lambda i: (0, i))],
        out_shape=[jax.ShapeDtypeStruct((T, D), BF16), jax.ShapeDtypeStruct((D, T), BF16)], compiler_params=_params(1),
    )(x, w)


def _out_proj(merged, w_out, x, norm_w):
    T = merged.shape[0]
    bm = _pick(T, (512, 256))

    def body(a_ref, b_ref, x_ref, w_ref, h_ref, v_ref, vt_ref):
        h = _dot(a_ref[...], b_ref[...], 1, 0) + x_ref[...]
        h_ref[...] = h
        r = lax.rsqrt(jnp.mean(h * h, axis=-1, keepdims=True) + EPS)
        v2 = h * r * w_ref[...]
        v_ref[...] = v2.astype(BF16)
        vt_ref[...] = v2.T.astype(BF16)

    blk = pl.BlockSpec((bm, D), lambda i: (i, 0))
    return pl.pallas_call(
        body, name="out_proj", grid=(T // bm,),
        in_specs=[blk, pl.BlockSpec((D, D), lambda i: (0, 0)), blk, pl.BlockSpec((1, D), lambda i: (0, 0))],
        out_specs=[blk, blk, pl.BlockSpec((D, bm), lambda i: (0, i))],
        out_shape=[jax.ShapeDtypeStruct((T, D), F32), jax.ShapeDtypeStruct((T, D), BF16),
                   jax.ShapeDtypeStruct((D, T), BF16)],
        compiler_params=_params(1),
    )(merged, w_out, x, norm_w)


def _rms_bwd(dy, h, w, resid, *, name):
    T = h.shape[0]
    rb = _rows(T)

    def body(dy_ref, h_ref, w_ref, res_ref, dh_ref, dhb_ref, dw_ref):
        @pl.when(pl.program_id(0) == 0)
        def _():
            dw_ref[...] = jnp.zeros_like(dw_ref)

        hv = h_ref[...]
        r = lax.rsqrt(jnp.mean(hv * hv, axis=-1, keepdims=True) + EPS)
        hn = hv * r
        dyv = dy_ref[...]
        dw_ref[...] += jnp.sum(dyv * hn, axis=0, keepdims=True)
        t = dyv * w_ref[...]
        dh = res_ref[...] + r * (t - hn * jnp.mean(t * hn, axis=-1, keepdims=True))
        dh_ref[...] = dh
        dhb_ref[...] = dh.astype(BF16)

    blk = pl.BlockSpec((rb, D), lambda i: (i, 0))
    vec = pl.BlockSpec((1, D), lambda i: (0, 0))
    return pl.pallas_call(
        body, name=name, grid=(T // rb,), in_specs=[blk, blk, vec, blk], out_specs=[blk, blk, vec],
        out_shape=[jax.ShapeDtypeStruct((T, D), F32), jax.ShapeDtypeStruct((T, D), BF16),
                   jax.ShapeDtypeStruct((1, D), F32)],
        compiler_params=_params(1),
    )(dy, h, w, resid)


def _rms_bwd_input(du_main, dproj_lr, w_in_lr, x, w, resid):
    T = x.shape[0]
    rb = _rows(T)

    def body(du_ref, dl_ref, wl_ref, x_ref, w_ref, res_ref, dx_ref, dw_ref):
        @pl.when(pl.program_id(0) == 0)
        def _():
            dw_ref[...] = jnp.zeros_like(dw_ref)

        xv = x_ref[...]
        r = lax.rsqrt(jnp.mean(xv * xv, axis=-1, keepdims=True) + EPS)
        xn = xv * r
        du = du_ref[...] + _dot(dl_ref[...], wl_ref[...], 1, 1)
        dw_ref[...] += jnp.sum(du * xn, axis=0, keepdims=True)
        t = du * w_ref[...]
        dx_ref[...] = res_ref[...] + r * (t - xn * jnp.mean(t * xn, axis=-1, keepdims=True))

    blk = pl.BlockSpec((rb, D), lambda i: (i, 0))
    vec = pl.BlockSpec((1, D), lambda i: (0, 0))
    return pl.pallas_call(
        body, name="rms1_bwd", grid=(T // rb,),
        in_specs=[blk, pl.BlockSpec((rb, LRP), lambda i: (i, 0)), pl.BlockSpec((D, LRP), lambda i: (0, 0)), blk, vec,
                  blk],
        out_specs=[blk, vec], out_shape=[jax.ShapeDtypeStruct((T, D), F32), jax.ShapeDtypeStruct((1, D), F32)],
        compiler_params=_params(1),
    )(du_main, dproj_lr, w_in_lr, x, w, resid)


def _final_loss(h, w, target):
    T = h.shape[0]
    rb = _rows(T)

    def body(h_ref, w_ref, t_ref, loss_ref, dh_ref, dhb_ref, dw_ref):
        @pl.when(pl.program_id(0) == 0)
        def _():
            dw_ref[...] = jnp.zeros_like(dw_ref)
            loss_ref[...] = jnp.zeros_like(loss_ref)

        hv = h_ref[...]
        wv = w_ref[...]
        r = lax.rsqrt(jnp.mean(hv * hv, axis=-1, keepdims=True) + EPS)
        hn = hv * r
        e = hn * wv - t_ref[...]
        row = jnp.sum(e * e, axis=-1, keepdims=True) * (0.5 / D)
        loss_ref[...] += jnp.broadcast_to(jnp.sum(row, axis=0, keepdims=True), loss_ref.shape)
        dy = e * (1.0 / D)
        dw_ref[...] += jnp.sum(dy * hn, axis=0, keepdims=True)
        t = dy * wv
        dh = r * (t - hn * jnp.mean(t * hn, axis=-1, keepdims=True))
        dh_ref[...] = dh
        dhb_ref[...] = dh.astype(BF16)

    blk = pl.BlockSpec((rb, D), lambda i: (i, 0))
    vec = pl.BlockSpec((1, D), lambda i: (0, 0))
    return pl.pallas_call(
        body, name="final_loss", grid=(T // rb,), in_specs=[blk, vec, blk],
        out_specs=[pl.BlockSpec((8, 128), lambda i: (0, 0)), blk, blk, vec],
        out_shape=[jax.ShapeDtypeStruct((8, 128), F32), jax.ShapeDtypeStruct((T, D), F32),
                   jax.ShapeDtypeStruct((T, D), BF16), jax.ShapeDtypeStruct((1, D), F32)],
        compiler_params=_params(1),
    )(h, w, target)


PANEL = FFN // 4


def _ffn_fwd(v2, w_gate, w_up):
    T = v2.shape[0]
    bm = _pick(T, (512, 256))

    def body(a_ref, bg_ref, bu_ref, gate_ref, up_ref, ff_ref, fft_ref):
        a = a_ref[...]
        g = _dot(a, bg_ref[...], 1, 0)
        u = _dot(a, bu_ref[...], 1, 0)
        gate_ref[...] = g
        up_ref[...] = u
        ff = g * _sigmoid(g) * u
        ff_ref[...] = ff.astype(BF16)
        fft_ref[...] = ff.T.astype(BF16)

    panel = pl.BlockSpec((None, D, PANEL), lambda j, m: (j, 0, 0))
    out = pl.BlockSpec((bm, PANEL), lambda j, m: (m, j))
    return pl.pallas_call(
        body, name="ffn_fwd", grid=(4, T // bm),
        in_specs=[pl.BlockSpec((bm, D), lambda j, m: (m, 0)), panel, panel],
        out_specs=[out, out, out, pl.BlockSpec((PANEL, bm), lambda j, m: (j, m))],
        out_shape=[jax.ShapeDtypeStruct((T, FFN), F32), jax.ShapeDtypeStruct((T, FFN), F32),
                   jax.ShapeDtypeStruct((T, FFN), BF16), jax.ShapeDtypeStruct((FFN, T), BF16)],
        compiler_params=_params(2),
    )(v2, w_gate, w_up)


def _ffn_bwd_hidden(dh2, w_down, gate, up):
    T = dh2.shape[0]
    bm = _pick(T, (512, 256))

    def body(a_ref, b_ref, g_ref, u_ref, dg_ref, du_ref):
        d = _dot(a_ref[...], b_ref[...], 1, 1)
        g = g_ref[...]
        sg = _sigmoid(g)
        du_ref[...] = (d * g * sg).astype(BF16)
        dg_ref[...] = (d * u_ref[...] * sg * (1.0 + g * (1.0 - sg))).astype(BF16)

    blk = pl.BlockSpec((bm, PANEL), lambda j, m: (m, j))
    return pl.pallas_call(
        body, name="ffn_bwd_hidden", grid=(4, T // bm),
        in_specs=[pl.BlockSpec((bm, D), lambda j, m: (m, 0)), pl.BlockSpec((PANEL, D), lambda j, m: (j, 0)), blk, blk],
        out_specs=[blk, blk], out_shape=[jax.ShapeDtypeStruct((T, FFN), BF16)] * 2, compiler_params=_params(2),
    )(dh2, w_down, gate, up)


def _ffn_bwd_input(dgate, dup, w_gate, w_up):
    T = dgate.shape[0]
    bm = _pick(T, (1024, 512, 256))
    bj = 1024

    def body(ag_ref, au_ref, bg_ref, bu_ref, o_ref, acc_ref):
        c = pl.program_id(2)
        p = _dot(ag_ref[...], bg_ref[...], 1, 1) + _dot(au_ref[...], bu_ref[...], 1, 1)

        @pl.when(c == 0)
        def _():
            acc_ref[...] = p

        @pl.when(c > 0)
        def _():
            acc_ref[...] += p

        @pl.when(c == 3)
        def _():
            o_ref[...] = acc_ref[...]

    a = pl.BlockSpec((bm, PANEL), lambda m, j, c: (m, c))
    b = pl.BlockSpec((None, bj, PANEL), lambda m, j, c: (c, j, 0))
    return pl.pallas_call(
        body, name="ffn_bwd_input", grid=(T // bm, D // bj, 4), in_specs=[a, a, b, b],
        out_specs=pl.BlockSpec((bm, bj), lambda m, j, c: (m, j)), out_shape=jax.ShapeDtypeStruct((T, D), F32),
        scratch_shapes=[pltpu.VMEM((bm, bj), F32)], compiler_params=_params(3),
    )(dgate, dup, w_gate, w_up)


def _colsum(x, *, name):
    T, W = x.shape
    rb = _rows(T)

    def body(x_ref, o_ref):
        @pl.when(pl.program_id(0) == 0)
        def _():
            o_ref[...] = jnp.zeros_like(o_ref)

        o_ref[...] += jnp.sum(x_ref[...], axis=0, keepdims=True)

    return pl.pallas_call(
        body, name=name, grid=(T // rb,), in_specs=[pl.BlockSpec((rb, W), lambda i: (i, 0))],
        out_specs=pl.BlockSpec((1, W), lambda i: (0, 0)), out_shape=jax.ShapeDtypeStruct((1, W), F32),
        compiler_params=_params(1),
    )(x)


def _merge_fwd(attn_o, gla_raw, proj, gla_norm_w):
    T = attn_o.shape[0]
    rb = _rows(T)

    def body(a_ref, g_ref, gr_ref, ga_ref, gb_ref, w_ref, o_ref, ot_ref):
        graw = g_ref[...]
        r = lax.rsqrt(jnp.mean(graw * graw, axis=-1, keepdims=True) + EPS)
        gr = gr_ref[...]
        go = graw * r * w_ref[...] * (gr * _sigmoid(gr))
        merged = _sigmoid(ga_ref[...]) * a_ref[...] + _sigmoid(gb_ref[...]) * go
        o_ref[...] = merged.astype(BF16)
        ot_ref[...] = merged.T.astype(BF16)

    def sec(k):
        return pl.BlockSpec((rb, DV), lambda i, h: (i, O_MERGE // DV + 3 * h + k))

    blk = pl.BlockSpec((rb, DV), lambda i, h: (i, h))
    return pl.pallas_call(
        body, name="merge_fwd", grid=(T // rb, GLA_H),
        in_specs=[blk, blk, sec(0), sec(1), sec(2), pl.BlockSpec((1, DV), lambda i, h: (0, 0))],
        out_specs=[blk, pl.BlockSpec((DV, rb), lambda i, h: (h, i))],
        out_shape=[jax.ShapeDtypeStruct((T, D), BF16), jax.ShapeDtypeStruct((D, T), BF16)], compiler_params=_params(2),
    )(attn_o, gla_raw, proj, proj, proj, gla_norm_w)


def _merge_bwd(dm, attn_o, gla_raw, proj, gla_norm_w):
    T = attn_o.shape[0]
    rb = _rows(T)

    def body(dm_ref, a_ref, g_ref, gr_ref, ga_ref, gb_ref, w_ref, da_ref, dg_ref, dp_ref, dw_ref):
        @pl.when((pl.program_id(0) == 0) & (pl.program_id(1) == 0))
        def _():
            dw_ref[...] = jnp.zeros_like(dw_ref)

        dmv = dm_ref[...]
        av = a_ref[...]
        graw = g_ref[...]
        gr = gr_ref[...]
        wv = w_ref[...]
        sa = _sigmoid(ga_ref[...])
        sb = _sigmoid(gb_ref[...])
        r = lax.rsqrt(jnp.mean(graw * graw, axis=-1, keepdims=True) + EPS)
        gnh = graw * r
        gn = gnh * wv
        sr = _sigmoid(gr)
        sl = gr * sr
        go = gn * sl
        da_ref[...] = (dmv * sa).astype(BF16)
        dgo = dmv * sb
        dp_ref[:, 0:DV] = (dgo * gn * sr * (1.0 + gr * (1.0 - sr))).astype(BF16)
        dp_ref[:, DV:2 * DV] = (dmv * av * sa * (1.0 - sa)).astype(BF16)
        dp_ref[:, 2 * DV:3 * DV] = (dmv * go * sb * (1.0 - sb)).astype(BF16)
        dgn = dgo * sl
        dw_ref[...] += jnp.sum(dgn * gnh, axis=0, keepdims=True)
        t = dgn * wv
        dg_ref[...] = (r * (t - gnh * jnp.mean(t * gnh, axis=-1, keepdims=True))).astype(BF16)

    def sec(k):
        return pl.BlockSpec((rb, DV), lambda i, h: (i, O_MERGE // DV + 3 * h + k))

    blk = pl.BlockSpec((rb, DV), lambda i, h: (i, h))
    vec = pl.BlockSpec((1, DV), lambda i, h: (0, 0))
    return pl.pallas_call(
        body, name="merge_bwd", grid=(T // rb, GLA_H),
        in_specs=[blk, blk, blk, sec(0), sec(1), sec(2), vec],
        out_specs=[blk, blk, pl.BlockSpec((rb, W_MERGE), lambda i, h: (i, O_MERGE // W_MERGE + h)), vec],
        out_shape=[jax.ShapeDtypeStruct((T, D), BF16), jax.ShapeDtypeStruct((T, D), BF16),
                   jax.ShapeDtypeStruct((T, MAIN), BF16), jax.ShapeDtypeStruct((1, DV), F32)],
        compiler_params=_params(2),
    )(dm, attn_o, gla_raw, proj, proj, proj, gla_norm_w)


def _attn_band(n):
    qi = lax.broadcasted_iota(jnp.int32, (GROUP * WINDOW, WINDOW), 0) & (WINDOW - 1)
    kj = lax.broadcasted_iota(jnp.int32, (GROUP * WINDOW, WINDOW), 1)
    cur = kj <= qi
    return cur, cur | (n > 0)


def _stack_heads(ref, h, scale=1.0):
    rows = jnp.concatenate(
        [ref[:, (h * GROUP + g) * HEAD_DIM:(h * GROUP + g + 1) * HEAD_DIM] for g in range(GROUP)], axis=0)
    return (rows * scale).astype(BF16) if scale != 1.0 else rows.astype(BF16)


def _stack_sinks(s_ref, h):
    return jnp.concatenate(
        [jnp.broadcast_to(s_ref[:, h * GROUP + g:h * GROUP + g + 1], (WINDOW, 1)) for g in range(GROUP)], axis=0)


def _attn_probs(qs, kp, kc, band, sink):
    cur, live = band
    s = jnp.where(cur, _dot(qs, kc, 1, 1), _dot(qs, kp, 1, 1))
    s = jnp.where(live, s, MASK_VALUE)
    m = jnp.maximum(jnp.max(s, axis=-1, keepdims=True), sink)
    e = jnp.exp(s - m)
    es = jnp.exp(sink - m)
    inv = 1.0 / (jnp.sum(e, axis=-1, keepdims=True) + es)
    return e * inv, es * inv


def _unfold(cur, x):
    mine = jnp.where(cur, x, 0.0)
    return mine.astype(BF16), (x - mine).astype(BF16)


def _attn_specs(nb, rev):
    def at(n):
        return (nb - 1 - n) if rev else n

    kcol, vcol = (O_ATTN + D) // 256, (O_ATTN + D) // 256 + 1
    q = pl.BlockSpec((WINDOW, D), lambda n: (at(n), O_ATTN // D))
    kc = pl.BlockSpec((WINDOW, 256), lambda n: (at(n), kcol))
    kp = pl.BlockSpec((WINDOW, 256), lambda n: (jnp.maximum(at(n) - 1, 0), kcol))
    vc = pl.BlockSpec((WINDOW, 256), lambda n: (at(n), vcol))
    vp = pl.BlockSpec((WINDOW, 256), lambda n: (jnp.maximum(at(n) - 1, 0), vcol))
    sk = pl.BlockSpec((1, 128), lambda n: (0, 0))
    o = pl.BlockSpec((WINDOW, D), lambda n: (at(n), 0))
    return q, kc, kp, vc, vp, sk, o


def _attn_fwd(proj, sinks):
    T = proj.shape[0]
    nb = T // WINDOW

    def body(q_ref, kc_ref, kp_ref, vc_ref, vp_ref, s_ref, o_ref):
        band = _attn_band(pl.program_id(0))
        for h in range(N_KV):
            hs = slice(h * HEAD_DIM, (h + 1) * HEAD_DIM)
            p, _ = _attn_probs(_stack_heads(q_ref, h, HEAD_DIM ** -0.5), kp_ref[:, hs].astype(BF16),
                               kc_ref[:, hs].astype(BF16), band, _stack_sinks(s_ref, h))
            p_cur, p_prev = _unfold(band[0], p)
            o = _dot(p_cur, vc_ref[:, hs].astype(BF16), 1, 0) + _dot(p_prev, vp_ref[:, hs].astype(BF16), 1, 0)
            for g in range(GROUP):
                hg = h * GROUP + g
                o_ref[:, hg * HEAD_DIM:(hg + 1) * HEAD_DIM] = o[g * WINDOW:(g + 1) * WINDOW]

    q, kc, kp, vc, vp, sk, o = _attn_specs(nb, False)
    return pl.pallas_call(
        body, name="attn_fwd", grid=(nb,), in_specs=[q, kc, kp, vc, vp, sk], out_specs=o,
        out_shape=jax.ShapeDtypeStruct((T, D), F32), compiler_params=_params(1),
    )(proj, proj, proj, proj, proj, sinks)


def _attn_bwd(proj, sinks, d_o, dproj):
    T = proj.shape[0]
    nb = T // WINDOW
    kat, vat = D, D + N_KV * HEAD_DIM

    def body(q_ref, kc_ref, kp_ref, vc_ref, vp_ref, s_ref, do_ref, _, dp_ref, ds_ref, ck_ref, cv_ref):
        i = pl.program_id(0)
        n = nb - 1 - i

        @pl.when(i == 0)
        def _():
            ck_ref[...] = jnp.zeros_like(ck_ref)
            cv_ref[...] = jnp.zeros_like(cv_ref)
            ds_ref[...] = jnp.zeros_like(ds_ref)

        band = _attn_band(n)
        for h in range(N_KV):
            hs = slice(h * HEAD_DIM, (h + 1) * HEAD_DIM)
            kp, kc = kp_ref[:, hs].astype(BF16), kc_ref[:, hs].astype(BF16)
            vp, vc = vp_ref[:, hs].astype(BF16), vc_ref[:, hs].astype(BF16)
            qs = _stack_heads(q_ref, h, HEAD_DIM ** -0.5)
            do = _stack_heads(do_ref, h)
            p, ps = _attn_probs(qs, kp, kc, band, _stack_sinks(s_ref, h))
            dp = jnp.where(band[0], _dot(do, vc, 1, 1), _dot(do, vp, 1, 1))
            delta = jnp.sum(p * dp, axis=-1, keepdims=True)
            ds_cur, ds_prev = _unfold(band[0], p * (dp - delta))
            p_cur, p_prev = _unfold(band[0], p)
            dsink = -ps * delta
            dq = ((_dot(ds_cur, kc, 1, 0) + _dot(ds_prev, kp, 1, 0)) * (HEAD_DIM ** -0.5)).astype(BF16)
            for g in range(GROUP):
                hg = h * GROUP + g
                rows = slice(g * WINDOW, (g + 1) * WINDOW)
                ds_ref[hg:hg + 1, :] += jnp.broadcast_to(jnp.sum(dsink[rows], axis=0, keepdims=True), (1, 128))
                dp_ref[:, hg * HEAD_DIM:(hg + 1) * HEAD_DIM] = dq[rows]
            dp_ref[:, kat + h * HEAD_DIM:kat + (h + 1) * HEAD_DIM] = (_dot(ds_cur, qs, 0, 0) + ck_ref[:, hs]).astype(BF16)
            dp_ref[:, vat + h * HEAD_DIM:vat + (h + 1) * HEAD_DIM] = (_dot(p_cur, do, 0, 0) + cv_ref[:, hs]).astype(BF16)
            ck_ref[:, hs] = _dot(ds_prev, qs, 0, 0)
            cv_ref[:, hs] = _dot(p_prev, do, 0, 0)

    q, kc, kp, vc, vp, sk, o = _attn_specs(nb, True)
    return pl.pallas_call(
        body, name="attn_bwd", grid=(nb,), in_specs=[q, kc, kp, vc, vp, sk, o, ANY],
        out_specs=[pl.BlockSpec((WINDOW, W_ATTN), lambda n: (nb - 1 - n, O_ATTN // W_ATTN)),
                   pl.BlockSpec((N_Q, 128), lambda n: (0, 0))],
        out_shape=[jax.ShapeDtypeStruct((T, MAIN), BF16), jax.ShapeDtypeStruct((N_Q, 128), F32)],
        scratch_shapes=[pltpu.VMEM((WINDOW, 256), F32), pltpu.VMEM((WINDOW, 256), F32)],
        input_output_aliases={7: 0}, compiler_params=_params(1),
    )(proj, proj, proj, proj, proj, sinks, d_o, dproj)


def _split3(x):
    hi = x.astype(BF16)
    r1 = x - hi.astype(F32)
    mid = r1.astype(BF16)
    lo = (r1 - mid.astype(F32)).astype(BF16)
    return hi, mid, lo


def _tri_sum(tri, x):
    hi, mid, lo = _split3(x)
    return _dot(tri, hi, 1, 0) + _dot(tri, mid, 1, 0) + _dot(tri, lo, 1, 0)


def _chunk_masks(rb):
    r = lax.broadcasted_iota(jnp.int32, (rb, rb), 0)
    c = lax.broadcasted_iota(jnp.int32, (rb, rb), 1)
    same = (r ^ c) < CHUNK
    return same & (r >= c), same & (r <= c)


def _per_chunk_rows(x, per, pick):
    return jnp.concatenate(
        [jnp.broadcast_to(pick(x[i * CHUNK:(i + 1) * CHUNK]), (CHUNK, x.shape[1])) for i in range(per)], axis=0)


def _gla_block(q, k, lr, w2, b, lower):
    per = q.shape[0] // CHUNK
    logit = _dot(lr.astype(BF16), w2.astype(BF16), 1, 0) + b
    la = (jnp.minimum(logit, 0.0) - jnp.log(1.0 + jnp.exp(-jnp.abs(logit)))) * (1.0 / GATE_NORM)
    g = _tri_sum(lower, la)
    gl = _per_chunk_rows(g, per, lambda c: c[CHUNK - 1:CHUNK])
    eg = jnp.exp(g)
    qd = q * (DK ** -0.5) * eg
    ki = k * jnp.exp(-g)
    ke = k * jnp.exp(gl - g)
    return logit, g, gl, eg, qd, ki, ke


def _gla_rows(T, forward):
    return _pick(T, (512, 256, 128, 64) if forward else (256, 128, 64))


def _gla_fwd(proj, proj_lr, w2p, gate_b):
    T = proj.shape[0]
    rb = _gla_rows(T, True)
    per = rb // CHUNK

    def body(q_ref, k_ref, v_ref, lr_ref, w2_ref, b_ref, o_ref, st_ref, s_scr):
        @pl.when(pl.program_id(1) == 0)
        def _():
            s_scr[...] = jnp.zeros_like(s_scr)

        low, _ = _chunk_masks(rb)
        _, _, gl, _, qd, ki, ke = _gla_block(q_ref[...], k_ref[...], lr_ref[...], w2_ref[...], b_ref[...],
                                             low.astype(BF16))
        v = v_ref[...].astype(BF16)
        qdb, keb = qd.astype(BF16), ke.astype(BF16)
        att = jnp.where(low, _dot(qdb, ki.astype(BF16), 1, 1), 0.0).astype(BF16)
        o_intra = _dot(att, v, 1, 0)
        st = s_scr[...]
        for i in range(per):
            rows = slice(i * CHUNK, (i + 1) * CHUNK)
            st_ref[0, i] = st
            o_ref[rows, :] = o_intra[rows] + _dot(qdb[rows], st.astype(BF16), 1, 1)
            st = st * jnp.exp(gl[i * CHUNK:i * CHUNK + 1]) + _dot(v[rows], keb[rows], 0, 0)
        s_scr[...] = st

    return pl.pallas_call(
        body, name="gla_fwd", grid=(GLA_H, T // rb),
        in_specs=[pl.BlockSpec((rb, DK), lambda h, n: (n, (O_GLA + W_GLA * h) // DK)),
                  pl.BlockSpec((rb, DK), lambda h, n: (n, (O_GLA + W_GLA * h) // DK + 1)),
                  pl.BlockSpec((rb, DV), lambda h, n: (n, (O_GLA + W_GLA * h) // DV + 1)),
                  pl.BlockSpec((rb, LRP), lambda h, n: (n, 0)),
                  pl.BlockSpec((LRP, DK), lambda h, n: (0, h)),
                  pl.BlockSpec((1, DK), lambda h, n: (0, h))],
        out_specs=[pl.BlockSpec((rb, DV), lambda h, n: (n, h)),
                   pl.BlockSpec((1, per, DV, DK), lambda h, n: (h, n, 0, 0))],
        out_shape=[jax.ShapeDtypeStruct((T, GLA_H * DV), F32),
                   jax.ShapeDtypeStruct((GLA_H, T // CHUNK, DV, DK), F32)],
        scratch_shapes=[pltpu.VMEM((DV, DK), F32)], compiler_params=_params(2),
    )(proj, proj, proj, proj_lr, w2p, gate_b)


def _gla_bwd(proj, proj_lr, w2p, gate_b, states, d_o, dproj):
    T = proj.shape[0]
    rb = _gla_rows(T, False)
    per = rb // CHUNK
    nblk = T // rb

    def body(q_ref, k_ref, v_ref, lr_ref, w2_ref, b_ref, st_ref, do_ref, _, dp_ref, dl_ref, ds_scr):
        @pl.when(pl.program_id(1) == 0)
        def _():
            ds_scr[...] = jnp.zeros_like(ds_scr)

        low, up = _chunk_masks(rb)
        logit, g, gl, eg, qd, ki, ke = _gla_block(q_ref[...], k_ref[...], lr_ref[...], w2_ref[...], b_ref[...],
                                                  low.astype(BF16))
        v = v_ref[...].astype(BF16)
        do = do_ref[...].astype(BF16)
        qdb, kib, keb = qd.astype(BF16), ki.astype(BF16), ke.astype(BF16)
        att = jnp.where(low, _dot(qdb, kib, 1, 1), 0.0).astype(BF16)
        datt = jnp.where(low, _dot(do, v, 1, 1), 0.0).astype(BF16)
        dv_intra = _dot(att, do, 0, 0)
        dqd_intra = _dot(datt, kib, 1, 0)
        dki = _dot(datt, qdb, 0, 0)
        ds = ds_scr[...]
        dqd, dke, dgl = [None] * per, [None] * per, [None] * per
        for i in reversed(range(per)):
            rows = slice(i * CHUNK, (i + 1) * CHUNK)
            sp = st_ref[0, i]
            dsb = ds.astype(BF16)
            dp_ref[rows, 2 * DK:] = (dv_intra[rows] + _dot(keb[rows], dsb, 1, 1)).astype(BF16)
            dqd[i] = dqd_intra[rows] + _dot(do[rows], sp.astype(BF16), 1, 0)
            dke[i] = _dot(v[rows], dsb, 1, 0)
            decay = jnp.exp(gl[i * CHUNK:i * CHUNK + 1])
            dgl[i] = (jnp.sum(dke[i] * ke[rows], axis=0, keepdims=True)
                      + jnp.sum(ds * sp, axis=0, keepdims=True) * decay)
            ds = ds * decay + _dot(do[rows], qdb[rows], 0, 0)
        ds_scr[...] = ds
        dqd = jnp.concatenate(dqd, axis=0)
        dke = jnp.concatenate(dke, axis=0)
        dgl = jnp.concatenate([jnp.broadcast_to(d, (CHUNK, DK)) for d in dgl], axis=0)
        dp_ref[:, 0:DK] = (dqd * (DK ** -0.5) * eg).astype(BF16)
        dp_ref[:, DK:2 * DK] = (dki * jnp.exp(-g) + dke * jnp.exp(gl - g)).astype(BF16)
        dg = dqd * qd - dki * ki - dke * ke
        dla = _tri_sum(up.astype(BF16), dg) + dgl
        dl_ref[...] = dla * (1.0 / GATE_NORM) * (1.0 - _sigmoid(logit))

    def rev(n):
        return nblk - 1 - n

    return pl.pallas_call(
        body, name="gla_bwd", grid=(GLA_H, nblk),
        in_specs=[pl.BlockSpec((rb, DK), lambda h, n: (rev(n), (O_GLA + W_GLA * h) // DK)),
                  pl.BlockSpec((rb, DK), lambda h, n: (rev(n), (O_GLA + W_GLA * h) // DK + 1)),
                  pl.BlockSpec((rb, DV), lambda h, n: (rev(n), (O_GLA + W_GLA * h) // DV + 1)),
                  pl.BlockSpec((rb, LRP), lambda h, n: (rev(n), 0)),
                  pl.BlockSpec((LRP, DK), lambda h, n: (0, h)),
                  pl.BlockSpec((1, DK), lambda h, n: (0, h)),
                  pl.BlockSpec((1, per, DV, DK), lambda h, n: (h, rev(n), 0, 0)),
                  pl.BlockSpec((rb, DV), lambda h, n: (rev(n), h)),
                  ANY],
        out_specs=[pl.BlockSpec((rb, W_GLA), lambda h, n: (rev(n), O_GLA // W_GLA + h)),
                   pl.BlockSpec((rb, DK), lambda h, n: (rev(n), h))],
        out_shape=[jax.ShapeDtypeStruct((T, MAIN), BF16), jax.ShapeDtypeStruct((T, GLA_H * DK), F32)],
        scratch_shapes=[pltpu.VMEM((DV, DK), F32)], input_output_aliases={8: 0}, compiler_params=_params(2),
    )(proj, proj, proj, proj_lr, w2p, gate_b, states, d_o, dproj)


def _after(values, tokens):
    return lax.optimization_barrier((values, tokens))[0]


def _local_step(x, target, p, reduce_part, reduce_ffn, reduce_in):
    u, u_t = _rms_fwd(x, p["norm1_w"], name="rms1_fwd")
    whole = _pick(x.shape[0], (2048, 256))
    proj = _matmul(u, p["w_in_main"], bm=whole, name="mm_proj")
    proj_lr = _matmul(u, p["w_in_lr"], name="mm_proj_lr")
    attn_o = _attn_fwd(proj, p["sinks"])
    gla_raw, states = _gla_fwd(proj, proj_lr, p["w2p"], p["gate_b"])
    merged, merged_t = _merge_fwd(attn_o, gla_raw, proj, p["gla_norm_w"])
    h1, v2, v2_t = _out_proj(merged, p["w_out"], x, p["norm2_w"])
    gate, up, ff, ff_t = _ffn_fwd(v2, p["w_gate"], p["w_up"])
    h2 = _matmul(ff, p["w_down"], add=h1, name="mm_down")
    loss, dh2, dh2_b, g_final = _final_loss(h2, p["final_norm_w"], target)

    dgate, dup = _ffn_bwd_hidden(dh2_b, p["w_down"], gate, up)
    sent = [reduce_ffn("w_down", ff_t, dh2_b), reduce_ffn("w_gate", v2_t, dgate), reduce_ffn("w_up", v2_t, dup)]
    dgate, dup = _after((dgate, dup), sent)
    dv2 = _ffn_bwd_input(dgate, dup, p["w_gate"], p["w_up"])
    dh1, dh1_b, g_norm2 = _rms_bwd(dv2, h1, p["norm2_w"], dh2, name="rms2_bwd")
    dmerged = _matmul(dh1_b, p["w_out"], tb=True, bj=D, name="mm_dmerged")
    sent = reduce_part("w_out", _matmul(merged_t, dh1_b, out_dtype=BF16, bj=D, name="mm_gout").reshape(4, D // 4, D))
    dmerged = _after(dmerged, sent)
    d_attn, d_gla, dproj, g_gla_norm = _merge_bwd(dmerged, attn_o, gla_raw, proj, p["gla_norm_w"])
    dproj, dlogit = _gla_bwd(proj, proj_lr, p["w2p"], p["gate_b"], states, d_gla, dproj)
    dproj, g_sinks = _attn_bwd(proj, p["sinks"], d_attn, dproj)
    g_gate_b = _colsum(dlogit, name="colsum_gate_b")
    g_w2 = _matmul(proj_lr, dlogit, ta=True, name="mm_gw2")
    dproj_lr = _matmul(dlogit, p["w2p"], tb=True, out_dtype=BF16, name="mm_dlr")
    reduce_in(u_t, dproj, dproj_lr)
    du = _matmul(dproj, p["w_in_main"], tb=True, bm=_pick(x.shape[0], (1024, 256)), bj=1024, bc=3200, name="mm_du")
    dx, g_norm1 = _rms_bwd_input(du, dproj_lr, p["w_in_lr"], x, p["norm1_w"], dh1)
    grads = dict(norm1_w=g_norm1, w2=g_w2[:RANK], gate_b=g_gate_b,
                 sinks=g_sinks[:, 0].reshape(1, N_Q), gla_norm_w=g_gla_norm, norm2_w=g_norm2, final_norm_w=g_final)
    return loss, dx, grads


def _place():
    x, y, c = lax.axis_index("x"), lax.axis_index("y"), lax.axis_index("c")
    chips = [(1 - x, y), (x, 1 - y), (1 - x, 1 - y)]
    return x, y, c, chips


def _hbm_shape(s, dt):
    return jax.ShapeDtypeStruct(s, dt)


ID_SIBLING = 9
ID_GATHER_IN = 10


def _handshake(peers):
    barrier = pltpu.get_barrier_semaphore()
    for peer in peers:
        pl.semaphore_signal(barrier, inc=1, device_id=peer, device_id_type=MESH)
    pl.semaphore_wait(barrier, len(peers))


def _launch_copies(body, args, out_shape, sems, *, name, collective_id=None):
    if collective_id is None:
        return pl.pallas_call(
            body, name=name, in_specs=[ANY] * len(args), out_specs=[ANY] * len(out_shape), out_shape=out_shape,
            scratch_shapes=sems)(*args)
    return pl.kernel(
        body, name=name, out_type=out_shape, mesh=plsc.ScalarSubcoreMesh(axis_name="sequencer", num_cores=1),
        scratch_types=sems, compiler_params=pltpu.CompilerParams(collective_id=collective_id))(*args)


def _gather_shards(shards, *, name, collective_id=None, after=()):
    n = len(shards)
    first_out = n + len(after)

    def body(*refs):
        ins, outs = refs[:n], refs[first_out:first_out + n]
        ici_send, ici_recv, d2d_send, d2d_recv, local_sem = refs[first_out + n:]
        x, y, c, chips = _place()
        me = 2 * x + y
        sibling = (x, y, 1 - c)
        if collective_id is not None:
            _handshake([sibling] + [(*chip, c) for chip in chips])

        def half(w, slot, hc):
            r2 = shards[w].shape[0] // 2
            return outs[w].at[slot, pl.ds(hc * r2, r2), :]

        locals_ = [pltpu.make_async_copy(ins[w], outs[w].at[me], local_sem.at[w]) for w in range(n)]
        for cp in locals_:
            cp.start()
        sends = []
        for w in range(n):
            r2 = shards[w].shape[0] // 2
            for j, chip in enumerate(chips):
                cp = pltpu.make_async_remote_copy(
                    src_ref=ins[w].at[pl.ds(c * r2, r2), :], dst_ref=half(w, me, c),
                    send_sem=ici_send.at[w * 3 + j], recv_sem=ici_recv.at[w * 3 + j],
                    device_id=(*chip, c), device_id_type=MESH)
                cp.start()
                sends.append(cp)
        for w in range(n):
            for j, chip in enumerate(chips):
                slot = 2 * chip[0] + chip[1]
                got = half(w, slot, c)
                pltpu.make_async_remote_copy(
                    src_ref=got, dst_ref=got, send_sem=ici_send.at[w * 3 + j], recv_sem=ici_recv.at[w * 3 + j],
                    device_id=(*chip, c), device_id_type=MESH).wait_recv()
                cp = pltpu.make_async_remote_copy(
                    src_ref=got, dst_ref=got, send_sem=d2d_send.at[w * 3 + j], recv_sem=d2d_recv.at[w * 3 + j],
                    device_id=sibling, device_id_type=MESH)
                cp.start()
                sends.append(cp)
        for w in range(n):
            for j, chip in enumerate(chips):
                slot = 2 * chip[0] + chip[1]
                got = half(w, slot, 1 - c)
                pltpu.make_async_remote_copy(
                    src_ref=got, dst_ref=got, send_sem=d2d_send.at[w * 3 + j], recv_sem=d2d_recv.at[w * 3 + j],
                    device_id=sibling, device_id_type=MESH).wait_recv()
        for cp in sends:
            cp.wait_send()
        for cp in locals_:
            cp.wait()

    return _launch_copies(
        body, list(shards) + list(after), [_hbm_shape((4,) + s.shape, s.dtype) for s in shards],
        [pltpu.SemaphoreType.DMA((3 * n,)), pltpu.SemaphoreType.DMA((3 * n,)), pltpu.SemaphoreType.DMA((3 * n,)),
         pltpu.SemaphoreType.DMA((3 * n,)), pltpu.SemaphoreType.DMA((n,))],
        name=name, collective_id=collective_id)


def _gather_staged(w, small):
    R, C = w.shape
    R2 = R // 2
    rb = 256
    per = R2 // rb
    r2 = small.shape[0] // 2

    def body(w_ref, s_ref, out_ref, outs_ref, stage, ici_send, ici_recv, fwd_send, fwd_recv, in_sem, put_sem,
             push_send, push_recv):
        x, y, c, chips = _place()
        me = 2 * x + y
        sibling = (x, y, 1 - c)
        slots = [2 * chip[0] + chip[1] for chip in chips]
        _handshake([sibling] + [(*chip, c) for chip in chips])

        def ici(j, src, dst, k):
            return pltpu.make_async_remote_copy(
                src_ref=src, dst_ref=dst, send_sem=ici_send.at[k], recv_sem=ici_recv.at[k],
                device_id=(*chips[j], c), device_id_type=MESH)

        sends = []
        for j in range(3):
            sends.append(ici(j, w_ref.at[pl.ds(c * R2, R2), :], out_ref.at[me, pl.ds(c * R2, R2), :], j))
            sends.append(ici(j, s_ref.at[pl.ds(c * r2, r2), :], outs_ref.at[me, pl.ds(c * r2, r2), :], 3 + j))
        for cp in sends:
            cp.start()
        mine = pltpu.make_async_copy(s_ref, outs_ref.at[me], put_sem.at[2])
        mine.start()

        def own(k, to_vmem):
            rows = pl.ds(k * rb, rb)
            if to_vmem:
                return pltpu.make_async_copy(w_ref.at[rows, :], stage.at[k % 2], in_sem.at[k % 2])
            return pltpu.make_async_copy(stage.at[k % 2], out_ref.at[me, rows, :], put_sem.at[k % 2])

        for k in range(R // rb):
            if k >= 2:
                own(k - 2, False).wait()
            own(k, True).start()
            own(k, True).wait()
            own(k, False).start()
        for k in range(R // rb - 2, R // rb):
            own(k, False).wait()

        def block(j, k, hc):
            return out_ref.at[slots[j], pl.ds(hc * R2 + k * rb, rb), :]

        def push(t, hc):
            j, k = divmod(t, per)
            return pltpu.make_async_remote_copy(
                src_ref=stage.at[t % 2], dst_ref=block(j, k, hc), send_sem=push_send.at[t % 2],
                recv_sem=push_recv.at[t], device_id=sibling, device_id_type=MESH)

        passed = []
        for j in range(3):
            ici(j, w_ref.at[pl.ds(c * R2, R2), :], out_ref.at[slots[j], pl.ds(c * R2, R2), :], j).wait_recv()
            for k in range(per):
                t = j * per + k
                if t >= 2:
                    push(t - 2, c).wait_send()
                fetch = pltpu.make_async_copy(block(j, k, c), stage.at[t % 2], in_sem.at[t % 2])
                fetch.start()
                fetch.wait()
                push(t, c).start()
            got = outs_ref.at[slots[j], pl.ds(c * r2, r2), :]
            ici(j, got, got, 3 + j).wait_recv()
            fw = pltpu.make_async_remote_copy(
                src_ref=got, dst_ref=got, send_sem=fwd_send.at[j], recv_sem=fwd_recv.at[j],
                device_id=sibling, device_id_type=MESH)
            fw.start()
            passed.append(fw)
        for t in range(3 * per - 2, 3 * per):
            push(t, c).wait_send()
        for t in range(3 * per):
            push(t, 1 - c).wait_recv()
        for j in range(3):
            theirs = outs_ref.at[slots[j], pl.ds((1 - c) * r2, r2), :]
            pltpu.make_async_remote_copy(
                src_ref=theirs, dst_ref=theirs, send_sem=fwd_send.at[j], recv_sem=fwd_recv.at[j],
                device_id=sibling, device_id_type=MESH).wait_recv()
        for cp in sends + passed:
            cp.wait_send()
        mine.wait()

    dma = pltpu.SemaphoreType.DMA
    return pl.pallas_call(
        body, name="gather_in", in_specs=[ANY, ANY], out_specs=[ANY, ANY],
        out_shape=[_hbm_shape((4, R, C), w.dtype), _hbm_shape((4,) + small.shape, small.dtype)],
        scratch_shapes=[pltpu.VMEM((2, rb, C), w.dtype), dma((6,)), dma((6,)), dma((3,)), dma((3,)), dma((2,)),
                        dma((3,)), dma((2,)), dma((3 * per,))],
        compiler_params=pltpu.CompilerParams(vmem_limit_bytes=VMEM_LIMIT, collective_id=ID_GATHER_IN),
    )(w, small)


def _pair_blocks(R2):
    return _pick(R2, (256, 352, 128, 64, 32, 16))


def _grad_matmul_split(a, b, *, cut, name):
    M, T = a.shape
    J = b.shape[1]
    bw = None
    if cut == "rows":
        S, R, C, bm = 4, M // 4, J, 352
        per = R // 2 // bm
        n = M // bm

        def where(t):
            return t // (2 * per), (t // per) % 2, t % per, 0

        a_spec = pl.BlockSpec((bm, T), lambda t: (t, 0))
        b_spec = pl.BlockSpec((T, C), lambda t: (0, 0))
    elif cut == "panels":
        S, R, C, bm = 4, M, J // 4, M // 2
        n = 8

        def where(t):
            return t // 2, t % 2, 0, 0

        a_spec = pl.BlockSpec((bm, T), lambda t: (t % 2, 0))
        b_spec = pl.BlockSpec((T, C), lambda t: (0, t // 2))
    else:
        S, R, C, bm, bw = 1, M, J, M // 2, 1280
        n = 2 * (J // bw)

        def where(t):
            return 0, t % 2, 0, t // 2

        a_spec = pl.BlockSpec((bm, T), lambda t: (t % 2, 0))
        b_spec = pl.BlockSpec((T, bw), lambda t: (0, t // 2))
    R2 = R // 2
    tw = bw or C

    def body(a_ref, b_ref, mine_ref, got_ref, tile, loc_sem, snd_sem, rcv_sem):
        x, y, c, _ = _place()
        sibling = (x, y, 1 - c)
        t = pl.program_id(0)

        @pl.when(t == 0)
        def _():
            _handshake([sibling])

        def place(ref, u):
            s, _, i, j = where(u)
            return ref.at[s, pl.ds(i * bm, bm), pl.ds(j * tw, tw)]

        def keep(u):
            return pltpu.make_async_copy(tile.at[u % 2], place(mine_ref, u), loc_sem.at[u % 2])

        def give(u):
            return pltpu.make_async_remote_copy(
                src_ref=tile.at[u % 2], dst_ref=place(got_ref, u), send_sem=snd_sem.at[u % 2],
                recv_sem=rcv_sem, device_id=sibling, device_id_type=MESH)

        def settle(u):
            @pl.when(where(u)[1] == c)
            def _():
                keep(u).wait()

            @pl.when(where(u)[1] != c)
            def _():
                give(u).wait_send()

        @pl.when(t >= 2)
        def _():
            settle(t - 2)

        tile[t % 2] = _dot(a_ref[...], b_ref[...], 1, 0).astype(BF16)

        @pl.when(where(t)[1] == c)
        def _():
            keep(t).start()

        @pl.when(where(t)[1] != c)
        def _():
            give(t).start()

        @pl.when(t == n - 1)
        def _():
            settle(t - 1)
            settle(t)
            pltpu.make_async_remote_copy(
                src_ref=got_ref, dst_ref=got_ref, send_sem=snd_sem.at[0], recv_sem=rcv_sem,
                device_id=sibling, device_id_type=MESH).wait_recv()

    half = _hbm_shape((S, R2, C), BF16)
    return pl.pallas_call(
        body, name=name, grid=(n,), in_specs=[a_spec, b_spec], out_specs=[ANY, ANY], out_shape=[half, half],
        scratch_shapes=[pltpu.VMEM((2, bm, tw), BF16), pltpu.SemaphoreType.DMA((2,)), pltpu.SemaphoreType.DMA((2,)),
                        pltpu.SemaphoreType.DMA],
        compiler_params=pltpu.CompilerParams(
            dimension_semantics=("arbitrary",), vmem_limit_bytes=VMEM_LIMIT, collective_id=ID_SIBLING),
    )(a, b)


def _pair_add(mine, got, *, name):
    S, R2, C = mine.shape
    rb = _pick(R2, (512, 352, 256))
    cb = _pick(C, (2560,)) if C > 4096 else C

    def body(a_ref, b_ref, o_ref):
        o_ref[...] = (a_ref[...].astype(F32) + b_ref[...].astype(F32)).astype(BF16)

    blk = pl.BlockSpec((1, rb, cb), lambda s, i, j: (s, i, j))
    return pl.pallas_call(
        body, name=name, grid=(S, R2 // rb, C // cb), in_specs=[blk, blk], out_specs=blk,
        out_shape=jax.ShapeDtypeStruct(mine.shape, BF16), compiler_params=_params(3),
    )(mine, got)


def _pair_reduce(part, *, name):
    S, R, C = part.shape
    R2 = R // 2
    rb = _pick(R2, (512, 352, 256, 128, 64, 32, 16))
    nblk = R2 // rb
    steps = [(s, i) for s in range(S) for i in range(nblk)]
    n = len(steps)

    def body(part_ref, sums_ref, own_buf, oth_buf, rcv_buf, out_buf, own_sem, oth_sem, out_sem, send_sem, recv_sem):
        x, y, c, _ = _place()
        sibling = (x, y, 1 - c)
        _handshake([sibling])

        def fetch(t, half, buf, sem):
            s, i = steps[t]
            return pltpu.make_async_copy(part_ref.at[s, pl.ds(half * R2 + i * rb, rb), :], buf.at[t % 3], sem.at[t % 3])

        def push(t):
            return pltpu.make_async_remote_copy(
                src_ref=oth_buf.at[t % 3], dst_ref=rcv_buf.at[t % 4], send_sem=send_sem.at[t % 4],
                recv_sem=recv_sem.at[t % 4], device_id=sibling, device_id_type=MESH)

        def store(t):
            s, i = steps[t]
            return pltpu.make_async_copy(out_buf.at[t % 2], sums_ref.at[s, pl.ds(i * rb, rb), :], out_sem.at[t % 2])

        for t in range(min(2, n)):
            fetch(t, c, own_buf, own_sem).start()
            fetch(t, 1 - c, oth_buf, oth_sem).start()
        fetch(0, 1 - c, oth_buf, oth_sem).wait()
        push(0).start()
        for t in range(n):
            if t + 2 < n:
                fetch(t + 2, c, own_buf, own_sem).start()
                fetch(t + 2, 1 - c, oth_buf, oth_sem).start()
            if t + 1 < n:
                fetch(t + 1, 1 - c, oth_buf, oth_sem).wait()
                push(t + 1).start()
            fetch(t, c, own_buf, own_sem).wait()
            push(t).wait()
            if t >= 2:
                store(t - 2).wait()
            out_buf[t % 2] = (own_buf[t % 3].astype(F32) + rcv_buf[t % 4].astype(F32)).astype(BF16)
            store(t).start()
        for t in range(max(n - 2, 0), n):
            store(t).wait()

    def buf(k):
        return pltpu.VMEM((k, rb, C), BF16)

    def sems(k):
        return pltpu.SemaphoreType.DMA((k,))

    return pl.pallas_call(
        body, name=name, in_specs=[ANY], out_specs=ANY, out_shape=_hbm_shape((S, R2, C), BF16),
        scratch_shapes=[buf(3), buf(3), buf(4), buf(2), sems(3), sems(3), sems(2), sems(4), sems(4)],
        compiler_params=pltpu.CompilerParams(vmem_limit_bytes=VMEM_LIMIT, collective_id=ID_SIBLING),
    )(part)


def _chip_exchange(sums, *, name, collective_id=None):
    n = len(sums)

    def body(*refs):
        ins, outs = refs[:n], refs[n:2 * n]
        send_sem, recv_sem, local_sem = refs[2 * n:]
        x, y, c, chips = _place()
        me = 2 * x + y
        if collective_id is not None:
            _handshake([(*chip, c) for chip in chips])
        cps = []
        for w in range(n):
            lc = pltpu.make_async_copy(ins[w].at[me], outs[w].at[me], local_sem.at[w])
            lc.start()
            cps.append(lc)
            for j, chip in enumerate(chips):
                slot = 2 * chip[0] + chip[1]
                rc = pltpu.make_async_remote_copy(
                    src_ref=ins[w].at[slot], dst_ref=outs[w].at[me],
                    send_sem=send_sem.at[w * 3 + j], recv_sem=recv_sem.at[w * 3 + j],
                    device_id=(*chip, c), device_id_type=MESH)
                rc.start()
                cps.append(rc)
        for w in range(n):
            for j, chip in enumerate(chips):
                slot = 2 * chip[0] + chip[1]
                pltpu.make_async_remote_copy(
                    src_ref=ins[w].at[slot], dst_ref=outs[w].at[slot],
                    send_sem=send_sem.at[w * 3 + j], recv_sem=recv_sem.at[w * 3 + j],
                    device_id=(*chip, c), device_id_type=MESH).wait_recv()
        for w in range(n):
            cps[w * 4].wait()
            for j in range(3):
                cps[w * 4 + 1 + j].wait_send()

    return _launch_copies(
        body, sums, [_hbm_shape(s.shape, s.dtype) for s in sums],
        [pltpu.SemaphoreType.DMA((3 * n,)), pltpu.SemaphoreType.DMA((3 * n,)), pltpu.SemaphoreType.DMA((n,))],
        name=name, collective_id=collective_id)


def _sum_join(recv, *, name):
    _, R2, C = recv.shape
    rb = _pair_blocks(R2)
    nblk = R2 // rb

    def body(recv_ref, out_ref, in_buf, acc_buf, in_sem, loc_sem, send_sem, recv_sem):
        x, y, c, _ = _place()
        sibling = (x, y, 1 - c)
        _handshake([sibling])

        def fetch(t):
            return pltpu.make_async_copy(recv_ref.at[:, pl.ds(t * rb, rb), :], in_buf.at[t % 2], in_sem.at[t % 2])

        def rows(t, half):
            return out_ref.at[pl.ds(half * R2 + t * rb, rb), :]

        def put_local(t):
            return pltpu.make_async_copy(acc_buf.at[t % 2], rows(t, c), loc_sem.at[t])

        def put_remote(t, half):
            return pltpu.make_async_remote_copy(
                src_ref=acc_buf.at[t % 2], dst_ref=rows(t, half), send_sem=send_sem.at[t], recv_sem=recv_sem.at[t],
                device_id=sibling, device_id_type=MESH)

        fetch(0).start()
        for t in range(nblk):
            if t + 1 < nblk:
                fetch(t + 1).start()
            fetch(t).wait()
            if t >= 2:
                put_local(t - 2).wait()
                put_remote(t - 2, c).wait_send()
            acc = in_buf[t % 2, 0].astype(F32)
            for s in range(1, 4):
                acc = acc + in_buf[t % 2, s].astype(F32)
            acc_buf[t % 2] = acc
            put_local(t).start()
            put_remote(t, c).start()
        for t in range(max(nblk - 2, 0), nblk):
            put_local(t).wait()
            put_remote(t, c).wait_send()
        for t in range(nblk):
            put_remote(t, 1 - c).wait_recv()

    semn = pltpu.SemaphoreType.DMA((nblk,))
    return pl.pallas_call(
        body, name=name, in_specs=[ANY], out_specs=ANY, out_shape=_hbm_shape((2 * R2, C), F32),
        scratch_shapes=[pltpu.VMEM((2, 4, rb, C), BF16), pltpu.VMEM((2, rb, C), F32),
                        pltpu.SemaphoreType.DMA((2,)), semn, semn, semn],
        compiler_params=pltpu.CompilerParams(vmem_limit_bytes=VMEM_LIMIT, collective_id=ID_SIBLING),
    )(recv)


def _sum_small(pack):
    R, C = pack.shape

    def body(in_ref, out_ref, all_ref, send_sem, recv_sem):
        x, y, c, _ = _place()
        me = 4 * x + 2 * y + c
        all_ref[me] = in_ref[...]
        cps = []
        for k in range(1, 8):
            peer = (x ^ (k >> 2), y ^ ((k >> 1) & 1), c ^ (k & 1))
            cp = pltpu.make_async_remote_copy(
                src_ref=in_ref, dst_ref=all_ref.at[me], send_sem=send_sem.at[k - 1], recv_sem=recv_sem.at[k - 1],
                device_id=peer, device_id_type=MESH)
            cp.start()
            cps.append(cp)
        for k in range(1, 8):
            peer = (x ^ (k >> 2), y ^ ((k >> 1) & 1), c ^ (k & 1))
            slot = 4 * peer[0] + 2 * peer[1] + peer[2]
            pltpu.make_async_remote_copy(
                src_ref=in_ref, dst_ref=all_ref.at[slot], send_sem=send_sem.at[k - 1], recv_sem=recv_sem.at[k - 1],
                device_id=peer, device_id_type=MESH).wait_recv()
        for cp in cps:
            cp.wait_send()
        acc = all_ref[0]
        for d in range(1, 8):
            acc = acc + all_ref[d]
        out_ref[...] = acc

    return pl.pallas_call(
        body, name="sum_small", in_specs=[pl.BlockSpec(memory_space=pltpu.VMEM)],
        out_specs=pl.BlockSpec(memory_space=pltpu.VMEM), out_shape=jax.ShapeDtypeStruct((R, C), F32),
        scratch_shapes=[pltpu.VMEM((8, R, C), F32), pltpu.SemaphoreType.DMA((7,)), pltpu.SemaphoreType.DMA((7,))],
    )(pack)


ADAMW_BLOCK_BYTES = 2 * 1024 * 1024


def _adamw_block(R, C):
    padded = -(-C // 128) * 128
    rows = [rb for rb in range(8, R + 1, 8) if R % rb == 0 and rb * padded * 4 <= ADAMW_BLOCK_BYTES]
    if rows or R * padded * 4 <= ADAMW_BLOCK_BYTES:
        return (max(rows) if rows else R), C
    cols = [cb for cb in range(128, C + 1, 128) if C % cb == 0 and R * cb * 4 <= ADAMW_BLOCK_BYTES]
    return R, max(cols)


def _adamw(w, g, m, v, *, name, pass_grad=False):
    R, C = w.shape
    rb, cb = _adamw_block(R, C)
    c1 = 1.0 / (1.0 - B1 ** STEP)
    c2 = 1.0 / (1.0 - B2 ** STEP)

    def body(w_ref, g_ref, m_ref, v_ref, *outs):
        d_ref, nm_ref, nv_ref = outs[-3:]
        gv = g_ref[...]
        if pass_grad:
            outs[0][...] = gv
        m2 = B1 * m_ref[...] + (1.0 - B1) * gv
        v2 = B2 * v_ref[...] + (1.0 - B2) * gv * gv
        nm_ref[...] = m2
        nv_ref[...] = v2
        d_ref[...] = -LR * ((m2 * c1) / (jnp.sqrt(v2 * c2) + ADAM_EPS) + WD * w_ref[...])

    n_out = 4 if pass_grad else 3
    blk = pl.BlockSpec((rb, cb), lambda i, j: (i, j))
    return pl.pallas_call(
        body, name=name, grid=(R // rb, C // cb), in_specs=[blk] * 4, out_specs=[blk] * n_out,
        out_shape=[jax.ShapeDtypeStruct((R, C), F32)] * n_out, compiler_params=_params(2),
    )(w, g, m, v)


SMALL = (("norm1_w", D), ("norm2_w", D), ("final_norm_w", D), ("gate_b", GLA_H * DK), ("gla_norm_w", DV), ("sinks", N_Q))
PACK_W = 1024


def _pack_small(vals, w2, loss):
    rows = []
    for name, width in SMALL:
        v = vals[name].reshape(-1)
        rows.append(jnp.pad(v, (0, (-width) % PACK_W)).reshape(-1, PACK_W))
    rows.append(w2)
    rows.append(jnp.broadcast_to(loss.reshape(1, 1), (1, PACK_W)))
    pack = jnp.concatenate(rows, axis=0)
    return jnp.pad(pack, ((0, 32 - pack.shape[0]), (0, 0)))


def _unpack_small(pack):
    out, r = {}, 0
    for name, width in SMALL:
        nr = -(-width // PACK_W)
        out[name] = pack[r:r + nr].reshape(-1)[:width]
        r += nr
    out["w2"] = pack[r:r + RANK]
    out["loss"] = pack[r + RANK, 0]
    return out


def kernel(x, norm1_w, w_in, gla_gate_w2, gla_gate_b, attn_sinks, gla_norm_w, w_out, norm2_w, w_ffn_gate, w_ffn_up, w_ffn_down, final_norm_w, loss_target, m_norm1_w, m_w_in, m_gla_gate_w2, m_gla_gate_b, m_attn_sinks, m_gla_norm_w, m_w_out, m_norm2_w, m_w_ffn_gate, m_w_ffn_up, m_w_ffn_down, m_final_norm_w, v_norm1_w, v_w_in, v_gla_gate_w2, v_gla_gate_b, v_attn_sinks, v_gla_norm_w, v_w_out, v_norm2_w, v_w_ffn_gate, v_w_ffn_up, v_w_ffn_down, v_final_norm_w):
    chip = 2 * lax.axis_index("x") + lax.axis_index("y")
    w_in_s, w_out_s, w_gate_s, w_up_s, w_down_s, w2_s = (
        w_in[0], w_out[0], w_ffn_gate[0], w_ffn_up[0], w_ffn_down[0], gla_gate_w2[0])
    CS = SHARD_COLS

    g_in, g_w2 = _gather_staged(w_in_s.astype(BF16), w2_s)
    g_out, = _gather_shards([w_out_s.astype(BF16)], name="gather_out", collective_id=1, after=[g_w2])
    g_gate, g_up = _gather_shards([w_gate_s.astype(BF16), w_up_s.astype(BF16)], name="gather_gate_up",
                                  collective_id=7, after=[g_w2])
    g_down, = _gather_shards([w_down_s.astype(BF16)], name="gather_down", collective_id=8, after=[g_w2])
    w_in_t = lax.optimization_barrier(jnp.transpose(w_in_s))
    m_in_t = lax.optimization_barrier(jnp.transpose(m_w_in[0]))
    v_in_t = lax.optimization_barrier(jnp.transpose(v_w_in[0]))
    w_main, w_lr = _w_in_to_main(_after(g_in, (w_in_t, m_in_t, v_in_t)))
    w2_full = jnp.transpose(g_w2, (1, 0, 2)).reshape(RANK, GLA_H * DK)
    p = dict(
        norm1_w=norm1_w, norm2_w=norm2_w, final_norm_w=final_norm_w.reshape(1, D), gate_b=gla_gate_b,
        gla_norm_w=gla_norm_w, sinks=jnp.pad(attn_sinks, ((0, 0), (0, 128 - N_Q))),
        w_in_main=w_main, w_in_lr=w_lr,
        w2p=jnp.pad(w2_full, ((0, LRP - RANK), (0, 0))).astype(BF16),
        w_out=g_out.reshape(D, D),
        w_gate=g_gate, w_up=g_up, w_down=g_down.reshape(FFN, D),
    )

    tags = ["w_in", "w_out", "w_gate", "w_up", "w_down"]
    recv = {}

    def exchange(tag, sums):
        recv[tag], = _chip_exchange([sums], name="chip_exchange_" + tag, collective_id=2 + tags.index(tag))
        return sums

    def reduce_part(tag, part):
        return exchange(tag, _pair_reduce(part, name="pair_reduce_" + tag))

    def reduce_ffn(tag, a, b):
        mine, got = _grad_matmul_split(a, b, cut="rows" if tag == "w_down" else "panels", name="mm_grad_" + tag)
        return exchange(tag, _pair_add(mine, got, name="pair_add_" + tag))

    def reduce_in(u_t, dproj, dproj_lr):
        mine, got = _grad_matmul_split(u_t, dproj, cut="whole", name="mm_grad_w_in")
        sum_main, = _pair_add(mine, got, name="pair_add_w_in")
        g_lr = _matmul(u_t, dproj_lr, out_dtype=BF16, name="mm_gin_lr")
        sum_lr, = _pair_reduce(g_lr[None], name="pair_reduce_w_in_lr")
        exchange("w_in", _main_to_shards(sum_main, sum_lr))

    done = {}
    shard = {"w_out": (w_out_s, m_w_out[0], v_w_out[0]), "w_gate": (w_gate_s, m_w_ffn_gate[0], v_w_ffn_gate[0]),
             "w_up": (w_up_s, m_w_ffn_up[0], v_w_ffn_up[0]), "w_down": (w_down_s, m_w_ffn_down[0], v_w_ffn_down[0])}

    def finish(tag):
        grad = _sum_join(recv[tag], name="sum_join_" + tag)
        done[tag] = _adamw(shard[tag][0], grad, shard[tag][1], shard[tag][2], name="adamw_" + tag, pass_grad=True)

    loss_blk, dx, g = _local_step(x[0], loss_target[0], p, reduce_part, reduce_ffn, reduce_in)
    for tag in ("w_down", "w_gate", "w_up", "w_out"):
        finish(tag)
    behind = [dx] + [done[tag][1] for tag in ("w_down", "w_gate", "w_up", "w_out")]
    g_in = _sum_join(_after(recv["w_in"], behind), name="sum_join_w_in")

    small = _unpack_small(_sum_small(_pack_small(g, g["w2"], loss_blk[0, 0])))
    loss = small["loss"]
    g_w2_mine = lax.dynamic_slice_in_dim(small["w2"], chip * (GLA_H * DK // 4), GLA_H * DK // 4, axis=1)

    grads = dict(
        norm1_w=small["norm1_w"].reshape(1, D), w_in=g_in, gla_gate_w2=g_w2_mine,
        gla_gate_b=small["gate_b"].reshape(1, -1), attn_sinks=small["sinks"].reshape(1, N_Q),
        gla_norm_w=small["gla_norm_w"].reshape(1, DV), norm2_w=small["norm2_w"].reshape(1, D),
        final_norm_w=small["final_norm_w"].reshape(1, D))
    def lin(a):
        return jnp.transpose(a.reshape(D, CS))

    def unlin(a):
        return jnp.transpose(a)

    grads["w_in"] = lin(g_in)
    weights = dict(
        norm1_w=(norm1_w, m_norm1_w, v_norm1_w), w_in=(w_in_t, m_in_t, v_in_t),
        gla_gate_w2=(w2_s, m_gla_gate_w2[0], v_gla_gate_w2[0]), gla_gate_b=(gla_gate_b, m_gla_gate_b, v_gla_gate_b),
        attn_sinks=(attn_sinks, m_attn_sinks, v_attn_sinks), gla_norm_w=(gla_norm_w, m_gla_norm_w, v_gla_norm_w),
        norm2_w=(norm2_w, m_norm2_w, v_norm2_w),
        final_norm_w=(final_norm_w.reshape(1, D), m_final_norm_w.reshape(1, D), v_final_norm_w.reshape(1, D)))
    names = ["norm1_w", "w_in", "gla_gate_w2", "gla_gate_b", "attn_sinks", "gla_norm_w", "w_out", "norm2_w",
             "w_ffn_gate", "w_ffn_up", "w_ffn_down", "final_norm_w"]
    lead = {"norm1_w": False, "gla_gate_b": False, "attn_sinks": False, "gla_norm_w": False, "norm2_w": False}
    g_out_l, d_out, m_out, v_out = [], [], [], []
    early = {"w_out": "w_out", "w_ffn_gate": "w_gate", "w_ffn_up": "w_up", "w_ffn_down": "w_down"}
    for nm in names:
        if nm in early:
            gr, dl, nmn, nvn = done[early[nm]]
            w = shard[early[nm]][0]
        else:
            w, m, v = weights[nm]
            gr = grads[nm]
            dl, nmn, nvn = _adamw(w, gr, m, v, name="adamw_" + nm)
        if nm == "w_in":
            gr, dl, nmn, nvn, w = g_in, unlin(dl), unlin(nmn), unlin(nvn), w_in_s
        if nm == "final_norm_w":
            shape = (D,)
        elif nm in lead:
            shape = w.shape
        else:
            shape = (1,) + w.shape
        g_out_l.append(gr.reshape(shape))
        d_out.append(dl.reshape(shape))
        m_out.append(nmn.reshape(shape))
        v_out.append(nvn.reshape(shape))
    return (loss, dx[None], *g_out_l, *d_out, *m_out, *v_out)
```

```python
import functools

import jax
import jax.numpy as jnp
from jax import lax
from jax.experimental import pallas as pl
from jax.experimental.pallas import tpu as pltpu
from jax.experimental.pallas import tpu_sc as plsc

F32 = jnp.float32
BF16 = jnp.bfloat16

D = 2048
HEAD_DIM = 64
N_Q = 32
N_KV = 4
GROUP = 8
WINDOW = 128
GLA_H = 4
DK = 256
DV = 512
RANK = 16
CHUNK = 64
FFN = 5632
EPS = 1e-6
MASK_VALUE = -1e30
GATE_NORM = 16.0
LR, B1, B2, ADAM_EPS, WD, STEP = 0.001, 0.9, 0.999, 1e-08, 0.01, 10

MAIN = 12800
O_MERGE, O_GLA, O_ATTN = 0, 6144, 10240
W_MERGE, W_GLA, W_ATTN = 3 * DV, 2 * DK + DV, D + 2 * N_KV * HEAD_DIM
LR_AT = 6656
LRP = 128


def _main_pieces():
    o_gq, o_gk, o_gv, o_gr, o_ga, o_gb = 2560, 3584, 4608, LR_AT + RANK, LR_AT + RANK + D, LR_AT + RANK + 2 * D
    pieces = []
    for h in range(GLA_H):
        pieces += [(o_gr + DV * h, DV), (o_ga + DV * h, DV), (o_gb + DV * h, DV)]
    for h in range(GLA_H):
        pieces += [(o_gq + DK * h, DK), (o_gk + DK * h, DK), (o_gv + DV * h, DV)]
    pieces.append((0, W_ATTN))
    return pieces


SHARD_COLS = 3204

VMEM_LIMIT = 56 * 1024 * 1024
MESH = pl.DeviceIdType.MESH
ANY = pl.BlockSpec(memory_space=pl.ANY)


def _params(ngrid):
    return pltpu.CompilerParams(dimension_semantics=("arbitrary",) * ngrid, vmem_limit_bytes=VMEM_LIMIT)


def _shard_segments(lo, hi):
    segs = []
    while lo < hi:
        s = lo // SHARD_COLS
        e = min(hi, (s + 1) * SHARD_COLS)
        segs.append((s, lo - s * SHARD_COLS, e - s * SHARD_COLS))
        lo = e
    return segs


def _lanes(pieces):
    return pieces[0] if len(pieces) == 1 else jnp.concatenate(pieces, axis=1)


def _w_in_to_main(shards):
    rb = 256

    def body(g_ref, main_ref, lr_ref):
        at = 0
        for a, w in _main_pieces():
            main_ref[:, at:at + w] = _lanes([g_ref[s, :, lo:hi] for s, lo, hi in _shard_segments(a, a + w)])
            at += w
        lr = [g_ref[s, :, lo:hi] for s, lo, hi in _shard_segments(LR_AT, LR_AT + RANK)]
        lr_ref[...] = _lanes(lr + [jnp.zeros((rb, LRP - RANK), lr_ref.dtype)])

    return pl.pallas_call(
        body, name="w_in_to_main", grid=(D // rb,),
        in_specs=[pl.BlockSpec((4, rb, SHARD_COLS), lambda i: (0, i, 0))],
        out_specs=[pl.BlockSpec((rb, MAIN), lambda i: (i, 0)), pl.BlockSpec((rb, LRP), lambda i: (i, 0))],
        out_shape=[jax.ShapeDtypeStruct((D, MAIN), shards.dtype), jax.ShapeDtypeStruct((D, LRP), shards.dtype)],
        compiler_params=_params(1),
    )(shards)


def _main_to_shards(g_main, g_lr):
    rb = 256
    rows = g_main.shape[0]
    at, sources = 0, [(LR_AT, RANK, 1, 0)]
    for a, w in _main_pieces():
        sources.append((a, w, 0, at))
        at += w
    sources.sort()

    def body(main_ref, lr_ref, out_ref):
        refs = (main_ref, lr_ref)
        for s in range(4):
            lo, hi = s * SHARD_COLS, (s + 1) * SHARD_COLS
            pieces = []
            for a, w, which, src in sources:
                b, e = max(a, lo), min(a + w, hi)
                if b < e:
                    pieces.append(refs[which][:, src + b - a:src + e - a])
            out_ref[s] = _lanes(pieces)

    return pl.pallas_call(
        body, name="main_to_shards", grid=(rows // rb,),
        in_specs=[pl.BlockSpec((rb, MAIN), lambda i: (i, 0)), pl.BlockSpec((rb, LRP), lambda i: (i, 0))],
        out_specs=pl.BlockSpec((4, rb, SHARD_COLS), lambda i: (0, i, 0)),
        out_shape=jax.ShapeDtypeStruct((4, rows, SHARD_COLS), g_main.dtype), compiler_params=_params(1),
    )(g_main, g_lr)


def _pick(n, cands):
    for c in cands:
        if n % c == 0:
            return c
    return n


def _sigmoid(x):
    return 1.0 / (1.0 + jnp.exp(-x))


def _dot(a, b, ca, cb):
    return lax.dot_general(a, b, (((ca,), (cb,)), ((), ())), preferred_element_type=F32)


def _matmul(a, b, *, name, ta=False, tb=False, add=None, out_dtype=F32, bm=None, bj=None, bc=None,
            b_slots=False, out_slots=False):
    C, M = a.shape if ta else a.shape[::-1]
    if b_slots:
        if tb:
            J, bc = b.shape[1], b.shape[2]
            assert b.shape[0] * bc == C
        else:
            bj = b.shape[2]
            J = b.shape[0] * bj
            assert b.shape[1] == C
    else:
        J = b.shape[0] if tb else b.shape[1]
        assert (b.shape[1] if tb else b.shape[0]) == C
    bm = bm or _pick(M, (1024, 512, 256, 128) if C <= FFN else (512, 256, 128))
    bj = bj or _pick(J, (1280, 1024, 512, 256, 128) if C <= D else (512, 256, 128))
    bc = bc or (C if C <= FFN else _pick(C, (6400,)))
    nm, nj, nc = M // bm, J // bj, C // bc
    has_add = add is not None

    def body(*refs):
        a_ref, b_ref = refs[0], refs[1]
        add_ref = refs[2] if has_add else None
        o_ref = refs[3] if has_add else refs[2]
        p = _dot(a_ref[...].astype(BF16), b_ref[...].astype(BF16), 0 if ta else 1, 1 if tb else 0)

        def finish(acc):
            if has_add:
                acc = acc + add_ref[...]
            o_ref[...] = acc.astype(o_ref.dtype)

        if nc == 1:
            finish(p)
        else:
            acc_ref = refs[-1]
            c = pl.program_id(2)

            @pl.when(c == 0)
            def _():
                acc_ref[...] = p

            @pl.when(c > 0)
            def _():
                acc_ref[...] += p

            @pl.when(c == nc - 1)
            def _():
                finish(acc_ref[...])

    a_spec = pl.BlockSpec((bc, bm), lambda m, j, c: (c, m)) if ta else pl.BlockSpec((bm, bc), lambda m, j, c: (m, c))
    if b_slots:
        b_spec = (pl.BlockSpec((None, bj, bc), lambda m, j, c: (c, j, 0)) if tb
                  else pl.BlockSpec((None, bc, bj), lambda m, j, c: (j, c, 0)))
    else:
        b_spec = (pl.BlockSpec((bj, bc), lambda m, j, c: (j, c)) if tb
                  else pl.BlockSpec((bc, bj), lambda m, j, c: (c, j)))
    if out_slots:
        assert not has_add
        o_spec = pl.BlockSpec((None, bm, bj), lambda m, j, c: (j, m, 0))
        out_shape = jax.ShapeDtypeStruct((nj, M, bj), out_dtype)
    else:
        o_spec = pl.BlockSpec((bm, bj), lambda m, j, c: (m, j))
        out_shape = jax.ShapeDtypeStruct((M, J), out_dtype)
    in_specs = [a_spec, b_spec] + ([o_spec] if has_add else [])
    args = (a, b) + ((add,) if has_add else ())
    return pl.pallas_call(
        body, name=name, grid=(nm, nj, nc), in_specs=in_specs, out_specs=o_spec,
        out_shape=out_shape,
        scratch_shapes=[pltpu.VMEM((bm, bj), F32)] if nc > 1 else [],
        compiler_params=_params(3),
    )(*args)


def _rows(T):
    return _pick(T, (256, 128))


def _rms_fwd(x, w, *, name):
    T = x.shape[0]
    rb = _rows(T)

    def body(x_ref, w_ref, u_ref, ut_ref):
        xv = x_ref[...]
        r = lax.rsqrt(jnp.mean(xv * xv, axis=-1, keepdims=True) + EPS)
        u = xv * r * w_ref[...]
        u_ref[...] = u.astype(BF16)
        ut_ref[...] = u.T.astype(BF16)

    return pl.pallas_call(
        body, name=name, grid=(T // rb,),
        in_specs=[pl.BlockSpec((rb, D), lambda i: (i, 0)), pl.BlockSpec((1, D), lambda i: (0, 0))],
        out_specs=[pl.BlockSpec((rb, D), lambda i: (i, 0)), pl.BlockSpec((D, rb), lambda i: (0, i))],
        out_shape=[jax.ShapeDtypeStruct((T, D), BF16), jax.ShapeDtypeStruct((D, T), BF16)], compiler_params=_params(1),
    )(x, w)


def _out_proj(merged, w_out, x, norm_w):
    T = merged.shape[0]
    bm = _pick(T, (512, 256))

    def body(a_ref, b_ref, x_ref, w_ref, h_ref, v_ref, vt_ref):
        h = _dot(a_ref[...], b_ref[...], 1, 0) + x_ref[...]
        h_ref[...] = h
        r = lax.rsqrt(jnp.mean(h * h, axis=-1, keepdims=True) + EPS)
        v2 = h * r * w_ref[...]
        v_ref[...] = v2.astype(BF16)
        vt_ref[...] = v2.T.astype(BF16)

    blk = pl.BlockSpec((bm, D), lambda i: (i, 0))
    return pl.pallas_call(
        body, name="out_proj", grid=(T // bm,),
        in_specs=[blk, pl.BlockSpec((D, D), lambda i: (0, 0)), blk, pl.BlockSpec((1, D), lambda i: (0, 0))],
        out_specs=[blk, blk, pl.BlockSpec((D, bm), lambda i: (0, i))],
        out_shape=[jax.ShapeDtypeStruct((T, D), F32), jax.ShapeDtypeStruct((T, D), BF16),
                   jax.ShapeDtypeStruct((D, T), BF16)],
        compiler_params=_params(1),
    )(merged, w_out, x, norm_w)


def _rms_bwd(dy, h, w, resid, *, name):
    T = h.shape[0]
    rb = _rows(T)

    def body(dy_ref, h_ref, w_ref, res_ref, dh_ref, dhb_ref, dw_ref):
        @pl.when(pl.program_id(0) == 0)
        def _():
            dw_ref[...] = jnp.zeros_like(dw_ref)

        hv = h_ref[...]
        r = lax.rsqrt(jnp.mean(hv * hv, axis=-1, keepdims=True) + EPS)
        hn = hv * r
        dyv = dy_ref[...]
        dw_ref[...] += jnp.sum(dyv * hn, axis=0, keepdims=True)
        t = dyv * w_ref[...]
        dh = res_ref[...] + r * (t - hn * jnp.mean(t * hn, axis=-1, keepdims=True))
        dh_ref[...] = dh
        dhb_ref[...] = dh.astype(BF16)

    blk = pl.BlockSpec((rb, D), lambda i: (i, 0))
    vec = pl.BlockSpec((1, D), lambda i: (0, 0))
    return pl.pallas_call(
        body, name=name, grid=(T // rb,), in_specs=[blk, blk, vec, blk], out_specs=[blk, blk, vec],
        out_shape=[jax.ShapeDtypeStruct((T, D), F32), jax.ShapeDtypeStruct((T, D), BF16),
                   jax.ShapeDtypeStruct((1, D), F32)],
        compiler_params=_params(1),
    )(dy, h, w, resid)


def _rms_bwd_input(du_main, dproj_lr, w_in_lr, x, w, resid):
    T = x.shape[0]
    rb = _rows(T)

    def body(du_ref, dl_ref, wl_ref, x_ref, w_ref, res_ref, dx_ref, dw_ref):
        @pl.when(pl.program_id(0) == 0)
        def _():
            dw_ref[...] = jnp.zeros_like(dw_ref)

        xv = x_ref[...]
        r = lax.rsqrt(jnp.mean(xv * xv, axis=-1, keepdims=True) + EPS)
        xn = xv * r
        du = du_ref[...] + _dot(dl_ref[...], wl_ref[...], 1, 1)
        dw_ref[...] += jnp.sum(du * xn, axis=0, keepdims=True)
        t = du * w_ref[...]
        dx_ref[...] = res_ref[...] + r * (t - xn * jnp.mean(t * xn, axis=-1, keepdims=True))

    blk = pl.BlockSpec((rb, D), lambda i: (i, 0))
    vec = pl.BlockSpec((1, D), lambda i: (0, 0))
    return pl.pallas_call(
        body, name="rms1_bwd", grid=(T // rb,),
        in_specs=[blk, pl.BlockSpec((rb, LRP), lambda i: (i, 0)), pl.BlockSpec((D, LRP), lambda i: (0, 0)), blk, vec,
                  blk],
        out_specs=[blk, vec], out_shape=[jax.ShapeDtypeStruct((T, D), F32), jax.ShapeDtypeStruct((1, D), F32)],
        compiler_params=_params(1),
    )(du_main, dproj_lr, w_in_lr, x, w, resid)


def _final_loss(h, w, target):
    T = h.shape[0]
    rb = _rows(T)

    def body(h_ref, w_ref, t_ref, loss_ref, dh_ref, dhb_ref, dw_ref):
        @pl.when(pl.program_id(0) == 0)
        def _():
            dw_ref[...] = jnp.zeros_like(dw_ref)
            loss_ref[...] = jnp.zeros_like(loss_ref)

        hv = h_ref[...]
        wv = w_ref[...]
        r = lax.rsqrt(jnp.mean(hv * hv, axis=-1, keepdims=True) + EPS)
        hn = hv * r
        e = hn * wv - t_ref[...]
        row = jnp.sum(e * e, axis=-1, keepdims=True) * (0.5 / D)
        loss_ref[...] += jnp.broadcast_to(jnp.sum(row, axis=0, keepdims=True), loss_ref.shape)
        dy = e * (1.0 / D)
        dw_ref[...] += jnp.sum(dy * hn, axis=0, keepdims=True)
        t = dy * wv
        dh = r * (t - hn * jnp.mean(t * hn, axis=-1, keepdims=True))
        dh_ref[...] = dh
        dhb_ref[...] = dh.astype(BF16)

    blk = pl.BlockSpec((rb, D), lambda i: (i, 0))
    vec = pl.BlockSpec((1, D), lambda i: (0, 0))
    return pl.pallas_call(
        body, name="final_loss", grid=(T // rb,), in_specs=[blk, vec, blk],
        out_specs=[pl.BlockSpec((8, 128), lambda i: (0, 0)), blk, blk, vec],
        out_shape=[jax.ShapeDtypeStruct((8, 128), F32), jax.ShapeDtypeStruct((T, D), F32),
                   jax.ShapeDtypeStruct((T, D), BF16), jax.ShapeDtypeStruct((1, D), F32)],
        compiler_params=_params(1),
    )(h, w, target)


PANEL = FFN // 4


def _ffn_fwd(v2, w_gate, w_up):
    T = v2.shape[0]
    bm = _pick(T, (512, 256))

    def body(a_ref, bg_ref, bu_ref, gate_ref, up_ref, ff_ref, fft_ref):
        a = a_ref[...]
        g = _dot(a, bg_ref[...], 1, 0)
        u = _dot(a, bu_ref[...], 1, 0)
        gate_ref[...] = g
        up_ref[...] = u
        ff = g * _sigmoid(g) * u
        ff_ref[...] = ff.astype(BF16)
        fft_ref[...] = ff.T.astype(BF16)

    panel = pl.BlockSpec((None, D, PANEL), lambda j, m: (j, 0, 0))
    out = pl.BlockSpec((bm, PANEL), lambda j, m: (m, j))
    return pl.pallas_call(
        body, name="ffn_fwd", grid=(4, T // bm),
        in_specs=[pl.BlockSpec((bm, D), lambda j, m: (m, 0)), panel, panel],
        out_specs=[out, out, out, pl.BlockSpec((PANEL, bm), lambda j, m: (j, m))],
        out_shape=[jax.ShapeDtypeStruct((T, FFN), F32), jax.ShapeDtypeStruct((T, FFN), F32),
                   jax.ShapeDtypeStruct((T, FFN), BF16), jax.ShapeDtypeStruct((FFN, T), BF16)],
        compiler_params=_params(2),
    )(v2, w_gate, w_up)


def _ffn_bwd_hidden(dh2, w_down, gate, up):
    T = dh2.shape[0]
    bm = _pick(T, (512, 256))

    def body(a_ref, b_ref, g_ref, u_ref, dg_ref, du_ref):
        d = _dot(a_ref[...], b_ref[...], 1, 1)
        g = g_ref[...]
        sg = _sigmoid(g)
        du_ref[...] = (d * g * sg).astype(BF16)
        dg_ref[...] = (d * u_ref[...] * sg * (1.0 + g * (1.0 - sg))).astype(BF16)

    blk = pl.BlockSpec((bm, PANEL), lambda j, m: (m, j))
    return pl.pallas_call(
        body, name="ffn_bwd_hidden", grid=(4, T // bm),
        in_specs=[pl.BlockSpec((bm, D), lambda j, m: (m, 0)), pl.BlockSpec((PANEL, D), lambda j, m: (j, 0)), blk, blk],
        out_specs=[blk, blk], out_shape=[jax.ShapeDtypeStruct((T, FFN), BF16)] * 2, compiler_params=_params(2),
    )(dh2, w_down, gate, up)


def _ffn_bwd_input(dgate, dup, w_gate, w_up):
    T = dgate.shape[0]
    bm = _pick(T, (1024, 512, 256))
    bj = 1024

    def body(ag_ref, au_ref, bg_ref, bu_ref, o_ref, acc_ref):
        c = pl.program_id(2)
        p = _dot(ag_ref[...], bg_ref[...], 1, 1) + _dot(au_ref[...], bu_ref[...], 1, 1)

        @pl.when(c == 0)
        def _():
            acc_ref[...] = p

        @pl.when(c > 0)
        def _():
            acc_ref[...] += p

        @pl.when(c == 3)
        def _():
            o_ref[...] = acc_ref[...]

    a = pl.BlockSpec((bm, PANEL), lambda m, j, c: (m, c))
    b = pl.BlockSpec((None, bj, PANEL), lambda m, j, c: (c, j, 0))
    return pl.pallas_call(
        body, name="ffn_bwd_input", grid=(T // bm, D // bj, 4), in_specs=[a, a, b, b],
        out_specs=pl.BlockSpec((bm, bj), lambda m, j, c: (m, j)), out_shape=jax.ShapeDtypeStruct((T, D), F32),
        scratch_shapes=[pltpu.VMEM((bm, bj), F32)], compiler_params=_params(3),
    )(dgate, dup, w_gate, w_up)


def _colsum(x, *, name):
    T, W = x.shape
    rb = _rows(T)

    def body(x_ref, o_ref):
        @pl.when(pl.program_id(0) == 0)
        def _():
            o_ref[...] = jnp.zeros_like(o_ref)

        o_ref[...] += jnp.sum(x_ref[...], axis=0, keepdims=True)

    return pl.pallas_call(
        body, name=name, grid=(T // rb,), in_specs=[pl.BlockSpec((rb, W), lambda i: (i, 0))],
        out_specs=pl.BlockSpec((1, W), lambda i: (0, 0)), out_shape=jax.ShapeDtypeStruct((1, W), F32),
        compiler_params=_params(1),
    )(x)


def _merge_fwd(attn_o, gla_raw, proj, gla_norm_w):
    T = attn_o.shape[0]
    rb = _rows(T)

    def body(a_ref, g_ref, gr_ref, ga_ref, gb_ref, w_ref, o_ref, ot_ref):
        graw = g_ref[...]
        r = lax.rsqrt(jnp.mean(graw * graw, axis=-1, keepdims=True) + EPS)
        gr = gr_ref[...]
        go = graw * r * w_ref[...] * (gr * _sigmoid(gr))
        merged = _sigmoid(ga_ref[...]) * a_ref[...] + _sigmoid(gb_ref[...]) * go
        o_ref[...] = merged.astype(BF16)
        ot_ref[...] = merged.T.astype(BF16)

    def sec(k):
        return pl.BlockSpec((rb, DV), lambda i, h: (i, O_MERGE // DV + 3 * h + k))

    blk = pl.BlockSpec((rb, DV), lambda i, h: (i, h))
    return pl.pallas_call(
        body, name="merge_fwd", grid=(T // rb, GLA_H),
        in_specs=[blk, blk, sec(0), sec(1), sec(2), pl.BlockSpec((1, DV), lambda i, h: (0, 0))],
        out_specs=[blk, pl.BlockSpec((DV, rb), lambda i, h: (h, i))],
        out_shape=[jax.ShapeDtypeStruct((T, D), BF16), jax.ShapeDtypeStruct((D, T), BF16)], compiler_params=_params(2),
    )(attn_o, gla_raw, proj, proj, proj, gla_norm_w)


def _merge_bwd(dm, attn_o, gla_raw, proj, gla_norm_w):
    T = attn_o.shape[0]
    rb = _rows(T)

    def body(dm_ref, a_ref, g_ref, gr_ref, ga_ref, gb_ref, w_ref, da_ref, dg_ref, dp_ref, dw_ref):
        @pl.when((pl.program_id(0) == 0) & (pl.program_id(1) == 0))
        def _():
            dw_ref[...] = jnp.zeros_like(dw_ref)

        dmv = dm_ref[...]
        av = a_ref[...]
        graw = g_ref[...]
        gr = gr_ref[...]
        wv = w_ref[...]
        sa = _sigmoid(ga_ref[...])
        sb = _sigmoid(gb_ref[...])
        r = lax.rsqrt(jnp.mean(graw * graw, axis=-1, keepdims=True) + EPS)
        gnh = graw * r
        gn = gnh * wv
        sr = _sigmoid(gr)
        sl = gr * sr
        go = gn * sl
        da_ref[...] = (dmv * sa).astype(BF16)
        dgo = dmv * sb
        dp_ref[:, 0:DV] = (dgo * gn * sr * (1.0 + gr * (1.0 - sr))).astype(BF16)
        dp_ref[:, DV:2 * DV] = (dmv * av * sa * (1.0 - sa)).astype(BF16)
        dp_ref[:, 2 * DV:3 * DV] = (dmv * go * sb * (1.0 - sb)).astype(BF16)
        dgn = dgo * sl
        dw_ref[...] += jnp.sum(dgn * gnh, axis=0, keepdims=True)
        t = dgn * wv
        dg_ref[...] = (r * (t - gnh * jnp.mean(t * gnh, axis=-1, keepdims=True))).astype(BF16)

    def sec(k):
        return pl.BlockSpec((rb, DV), lambda i, h: (i, O_MERGE // DV + 3 * h + k))

    blk = pl.BlockSpec((rb, DV), lambda i, h: (i, h))
    vec = pl.BlockSpec((1, DV), lambda i, h: (0, 0))
    return pl.pallas_call(
        body, name="merge_bwd", grid=(T // rb, GLA_H),
        in_specs=[blk, blk, blk, sec(0), sec(1), sec(2), vec],
        out_specs=[blk, blk, pl.BlockSpec((rb, W_MERGE), lambda i, h: (i, O_MERGE // W_MERGE + h)), vec],
        out_shape=[jax.ShapeDtypeStruct((T, D), BF16), jax.ShapeDtypeStruct((T, D), BF16),
                   jax.ShapeDtypeStruct((T, MAIN), BF16), jax.ShapeDtypeStruct((1, DV), F32)],
        compiler_params=_params(2),
    )(dm, attn_o, gla_raw, proj, proj, proj, gla_norm_w)


def _attn_band(n):
    qi = lax.broadcasted_iota(jnp.int32, (GROUP * WINDOW, WINDOW), 0) & (WINDOW - 1)
    kj = lax.broadcasted_iota(jnp.int32, (GROUP * WINDOW, WINDOW), 1)
    cur = kj <= qi
    return cur, cur | (n > 0)


def _stack_heads(ref, h, scale=1.0):
    rows = jnp.concatenate(
        [ref[:, (h * GROUP + g) * HEAD_DIM:(h * GROUP + g + 1) * HEAD_DIM] for g in range(GROUP)], axis=0)
    return (rows * scale).astype(BF16) if scale != 1.0 else rows.astype(BF16)


def _stack_sinks(s_ref, h):
    return jnp.concatenate(
        [jnp.broadcast_to(s_ref[:, h * GROUP + g:h * GROUP + g + 1], (WINDOW, 1)) for g in range(GROUP)], axis=0)


def _attn_probs(qs, kp, kc, band, sink):
    cur, live = band
    s = jnp.where(cur, _dot(qs, kc, 1, 1), _dot(qs, kp, 1, 1))
    s = jnp.where(live, s, MASK_VALUE)
    m = jnp.maximum(jnp.max(s, axis=-1, keepdims=True), sink)
    e = jnp.exp(s - m)
    es = jnp.exp(sink - m)
    inv = 1.0 / (jnp.sum(e, axis=-1, keepdims=True) + es)
    return e * inv, es * inv


def _unfold(cur, x):
    mine = jnp.where(cur, x, 0.0)
    return mine.astype(BF16), (x - mine).astype(BF16)


def _attn_specs(nb, rev):
    def at(n):
        return (nb - 1 - n) if rev else n

    kcol, vcol = (O_ATTN + D) // 256, (O_ATTN + D) // 256 + 1
    q = pl.BlockSpec((WINDOW, D), lambda n: (at(n), O_ATTN // D))
    kc = pl.BlockSpec((WINDOW, 256), lambda n: (at(n), kcol))
    kp = pl.BlockSpec((WINDOW, 256), lambda n: (jnp.maximum(at(n) - 1, 0), kcol))
    vc = pl.BlockSpec((WINDOW, 256), lambda n: (at(n), vcol))
    vp = pl.BlockSpec((WINDOW, 256), lambda n: (jnp.maximum(at(n) - 1, 0), vcol))
    sk = pl.BlockSpec((1, 128), lambda n: (0, 0))
    o = pl.BlockSpec((WINDOW, D), lambda n: (at(n), 0))
    return q, kc, kp, vc, vp, sk, o


def _attn_fwd(proj, sinks):
    T = proj.shape[0]
    nb = T // WINDOW

    def body(q_ref, kc_ref, kp_ref, vc_ref, vp_ref, s_ref, o_ref):
        band = _attn_band(pl.program_id(0))
        for h in range(N_KV):
            hs = slice(h * HEAD_DIM, (h + 1) * HEAD_DIM)
            p, _ = _attn_probs(_stack_heads(q_ref, h, HEAD_DIM ** -0.5), kp_ref[:, hs].astype(BF16),
                               kc_ref[:, hs].astype(BF16), band, _stack_sinks(s_ref, h))
            p_cur, p_prev = _unfold(band[0], p)
            o = _dot(p_cur, vc_ref[:, hs].astype(BF16), 1, 0) + _dot(p_prev, vp_ref[:, hs].astype(BF16), 1, 0)
            for g in range(GROUP):
                hg = h * GROUP + g
                o_ref[:, hg * HEAD_DIM:(hg + 1) * HEAD_DIM] = o[g * WINDOW:(g + 1) * WINDOW]

    q, kc, kp, vc, vp, sk, o = _attn_specs(nb, False)
    return pl.pallas_call(
        body, name="attn_fwd", grid=(nb,), in_specs=[q, kc, kp, vc, vp, sk], out_specs=o,
        out_shape=jax.ShapeDtypeStruct((T, D), F32), compiler_params=_params(1),
    )(proj, proj, proj, proj, proj, sinks)


def _attn_bwd(proj, sinks, d_o, dproj):
    T = proj.shape[0]
    nb = T // WINDOW
    kat, vat = D, D + N_KV * HEAD_DIM

    def body(q_ref, kc_ref, kp_ref, vc_ref, vp_ref, s_ref, do_ref, _, dp_ref, ds_ref, ck_ref, cv_ref):
        i = pl.program_id(0)
        n = nb - 1 - i

        @pl.when(i == 0)
        def _():
            ck_ref[...] = jnp.zeros_like(ck_ref)
            cv_ref[...] = jnp.zeros_like(cv_ref)
            ds_ref[...] = jnp.zeros_like(ds_ref)

        band = _attn_band(n)
        for h in range(N_KV):
            hs = slice(h * HEAD_DIM, (h + 1) * HEAD_DIM)
            kp, kc = kp_ref[:, hs].astype(BF16), kc_ref[:, hs].astype(BF16)
            vp, vc = vp_ref[:, hs].astype(BF16), vc_ref[:, hs].astype(BF16)
            qs = _stack_heads(q_ref, h, HEAD_DIM ** -0.5)
            do = _stack_heads(do_ref, h)
            p, ps = _attn_probs(qs, kp, kc, band, _stack_sinks(s_ref, h))
            dp = jnp.where(band[0], _dot(do, vc, 1, 1), _dot(do, vp, 1, 1))
            delta = jnp.sum(p * dp, axis=-1, keepdims=True)
            ds_cur, ds_prev = _unfold(band[0], p * (dp - delta))
            p_cur, p_prev = _unfold(band[0], p)
            dsink = -ps * delta
            dq = ((_dot(ds_cur, kc, 1, 0) + _dot(ds_prev, kp, 1, 0)) * (HEAD_DIM ** -0.5)).astype(BF16)
            for g in range(GROUP):
                hg = h * GROUP + g
                rows = slice(g * WINDOW, (g + 1) * WINDOW)
                ds_ref[hg:hg + 1, :] += jnp.broadcast_to(jnp.sum(dsink[rows], axis=0, keepdims=True), (1, 128))
                dp_ref[:, hg * HEAD_DIM:(hg + 1) * HEAD_DIM] = dq[rows]
            dp_ref[:, kat + h * HEAD_DIM:kat + (h + 1) * HEAD_DIM] = (_dot(ds_cur, qs, 0, 0) + ck_ref[:, hs]).astype(BF16)
            dp_ref[:, vat + h * HEAD_DIM:vat + (h + 1) * HEAD_DIM] = (_dot(p_cur, do, 0, 0) + cv_ref[:, hs]).astype(BF16)
            ck_ref[:, hs] = _dot(ds_prev, qs, 0, 0)
            cv_ref[:, hs] = _dot(p_prev, do, 0, 0)

    q, kc, kp, vc, vp, sk, o = _attn_specs(nb, True)
    return pl.pallas_call(
        body, name="attn_bwd", grid=(nb,), in_specs=[q, kc, kp, vc, vp, sk, o, ANY],
        out_specs=[pl.BlockSpec((WINDOW, W_ATTN), lambda n: (nb - 1 - n, O_ATTN // W_ATTN)),
                   pl.BlockSpec((N_Q, 128), lambda n: (0, 0))],
        out_shape=[jax.ShapeDtypeStruct((T, MAIN), BF16), jax.ShapeDtypeStruct((N_Q, 128), F32)],
        scratch_shapes=[pltpu.VMEM((WINDOW, 256), F32), pltpu.VMEM((WINDOW, 256), F32)],
        input_output_aliases={7: 0}, compiler_params=_params(1),
    )(proj, proj, proj, proj, proj, sinks, d_o, dproj)


def _split3(x):
    hi = x.astype(BF16)
    r1 = x - hi.astype(F32)
    mid = r1.astype(BF16)
    lo = (r1 - mid.astype(F32)).astype(BF16)
    return hi, mid, lo


def _tri_sum(tri, x):
    hi, mid, lo = _split3(x)
    return _dot(tri, hi, 1, 0) + _dot(tri, mid, 1, 0) + _dot(tri, lo, 1, 0)


def _chunk_masks(rb):
    r = lax.broadcasted_iota(jnp.int32, (rb, rb), 0)
    c = lax.broadcasted_iota(jnp.int32, (rb, rb), 1)
    same = (r ^ c) < CHUNK
    return same & (r >= c), same & (r <= c)


def _per_chunk_rows(x, per, pick):
    return jnp.concatenate(
        [jnp.broadcast_to(pick(x[i * CHUNK:(i + 1) * CHUNK]), (CHUNK, x.shape[1])) for i in range(per)], axis=0)


def _gla_block(q, k, lr, w2, b, lower):
    per = q.shape[0] // CHUNK
    logit = _dot(lr.astype(BF16), w2.astype(BF16), 1, 0) + b
    la = (jnp.minimum(logit, 0.0) - jnp.log(1.0 + jnp.exp(-jnp.abs(logit)))) * (1.0 / GATE_NORM)
    g = _tri_sum(lower, la)
    gl = _per_chunk_rows(g, per, lambda c: c[CHUNK - 1:CHUNK])
    eg = jnp.exp(g)
    qd = q * (DK ** -0.5) * eg
    ki = k * jnp.exp(-g)
    ke = k * jnp.exp(gl - g)
    return logit, g, gl, eg, qd, ki, ke


def _gla_rows(T, forward):
    return _pick(T, (512, 256, 128, 64) if forward else (256, 128, 64))


def _gla_fwd(proj, proj_lr, w2p, gate_b):
    T = proj.shape[0]
    rb = _gla_rows(T, True)
    per = rb // CHUNK

    def body(q_ref, k_ref, v_ref, lr_ref, w2_ref, b_ref, o_ref, st_ref, s_scr):
        @pl.when(pl.program_id(1) == 0)
        def _():
            s_scr[...] = jnp.zeros_like(s_scr)

        low, _ = _chunk_masks(rb)
        _, _, gl, _, qd, ki, ke = _gla_block(q_ref[...], k_ref[...], lr_ref[...], w2_ref[...], b_ref[...],
                                             low.astype(BF16))
        v = v_ref[...].astype(BF16)
        qdb, keb = qd.astype(BF16), ke.astype(BF16)
        att = jnp.where(low, _dot(qdb, ki.astype(BF16), 1, 1), 0.0).astype(BF16)
        o_intra = _dot(att, v, 1, 0)
        st = s_scr[...]
        for i in range(per):
            rows = slice(i * CHUNK, (i + 1) * CHUNK)
            st_ref[0, i] = st
            o_ref[rows, :] = o_intra[rows] + _dot(qdb[rows], st.astype(BF16), 1, 1)
            st = st * jnp.exp(gl[i * CHUNK:i * CHUNK + 1]) + _dot(v[rows], keb[rows], 0, 0)
        s_scr[...] = st

    return pl.pallas_call(
        body, name="gla_fwd", grid=(GLA_H, T // rb),
        in_specs=[pl.BlockSpec((rb, DK), lambda h, n: (n, (O_GLA + W_GLA * h) // DK)),
                  pl.BlockSpec((rb, DK), lambda h, n: (n, (O_GLA + W_GLA * h) // DK + 1)),
                  pl.BlockSpec((rb, DV), lambda h, n: (n, (O_GLA + W_GLA * h) // DV + 1)),
                  pl.BlockSpec((rb, LRP), lambda h, n: (n, 0)),
                  pl.BlockSpec((LRP, DK), lambda h, n: (0, h)),
                  pl.BlockSpec((1, DK), lambda h, n: (0, h))],
        out_specs=[pl.BlockSpec((rb, DV), lambda h, n: (n, h)),
                   pl.BlockSpec((1, per, DV, DK), lambda h, n: (h, n, 0, 0))],
        out_shape=[jax.ShapeDtypeStruct((T, GLA_H * DV), F32),
                   jax.ShapeDtypeStruct((GLA_H, T // CHUNK, DV, DK), F32)],
        scratch_shapes=[pltpu.VMEM((DV, DK), F32)], compiler_params=_params(2),
    )(proj, proj, proj, proj_lr, w2p, gate_b)


def _gla_bwd(proj, proj_lr, w2p, gate_b, states, d_o, dproj):
    T = proj.shape[0]
    rb = _gla_rows(T, False)
    per = rb // CHUNK
    nblk = T // rb

    def body(q_ref, k_ref, v_ref, lr_ref, w2_ref, b_ref, st_ref, do_ref, _, dp_ref, dl_ref, ds_scr):
        @pl.when(pl.program_id(1) == 0)
        def _():
            ds_scr[...] = jnp.zeros_like(ds_scr)

        low, up = _chunk_masks(rb)
        logit, g, gl, eg, qd, ki, ke = _gla_block(q_ref[...], k_ref[...], lr_ref[...], w2_ref[...], b_ref[...],
                                                  low.astype(BF16))
        v = v_ref[...].astype(BF16)
        do = do_ref[...].astype(BF16)
        qdb, kib, keb = qd.astype(BF16), ki.astype(BF16), ke.astype(BF16)
        att = jnp.where(low, _dot(qdb, kib, 1, 1), 0.0).astype(BF16)
        datt = jnp.where(low, _dot(do, v, 1, 1), 0.0).astype(BF16)
        dv_intra = _dot(att, do, 0, 0)
        dqd_intra = _dot(datt, kib, 1, 0)
        dki = _dot(datt, qdb, 0, 0)
        ds = ds_scr[...]
        dqd, dke, dgl = [None] * per, [None] * per, [None] * per
        for i in reversed(range(per)):
            rows = slice(i * CHUNK, (i + 1) * CHUNK)
            sp = st_ref[0, i]
            dsb = ds.astype(BF16)
            dp_ref[rows, 2 * DK:] = (dv_intra[rows] + _dot(keb[rows], dsb, 1, 1)).astype(BF16)
            dqd[i] = dqd_intra[rows] + _dot(do[rows], sp.astype(BF16), 1, 0)
            dke[i] = _dot(v[rows], dsb, 1, 0)
            decay = jnp.exp(gl[i * CHUNK:i * CHUNK + 1])
            dgl[i] = (jnp.sum(dke[i] * ke[rows], axis=0, keepdims=True)
                      + jnp.sum(ds * sp, axis=0, keepdims=True) * decay)
            ds = ds * decay + _dot(do[rows], qdb[rows], 0, 0)
        ds_scr[...] = ds
        dqd = jnp.concatenate(dqd, axis=0)
        dke = jnp.concatenate(dke, axis=0)
        dgl = jnp.concatenate([jnp.broadcast_to(d, (CHUNK, DK)) for d in dgl], axis=0)
        dp_ref[:, 0:DK] = (dqd * (DK ** -0.5) * eg).astype(BF16)
        dp_ref[:, DK:2 * DK] = (dki * jnp.exp(-g) + dke * jnp.exp(gl - g)).astype(BF16)
        dg = dqd * qd - dki * ki - dke * ke
        dla = _tri_sum(up.astype(BF16), dg) + dgl
        dl_ref[...] = dla * (1.0 / GATE_NORM) * (1.0 - _sigmoid(logit))

    def rev(n):
        return nblk - 1 - n

    return pl.pallas_call(
        body, name="gla_bwd", grid=(GLA_H, nblk),
        in_specs=[pl.BlockSpec((rb, DK), lambda h, n: (rev(n), (O_GLA + W_GLA * h) // DK)),
                  pl.BlockSpec((rb, DK), lambda h, n: (rev(n), (O_GLA + W_GLA * h) // DK + 1)),
                  pl.BlockSpec((rb, DV), lambda h, n: (rev(n), (O_GLA + W_GLA * h) // DV + 1)),
                  pl.BlockSpec((rb, LRP), lambda h, n: (rev(n), 0)),
                  pl.BlockSpec((LRP, DK), lambda h, n: (0, h)),
                  pl.BlockSpec((1, DK), lambda h, n: (0, h)),
                  pl.BlockSpec((1, per, DV, DK), lambda h, n: (h, rev(n), 0, 0)),
                  pl.BlockSpec((rb, DV), lambda h, n: (rev(n), h)),
                  ANY],
        out_specs=[pl.BlockSpec((rb, W_GLA), lambda h, n: (rev(n), O_GLA // W_GLA + h)),
                   pl.BlockSpec((rb, DK), lambda h, n: (rev(n), h))],
        out_shape=[jax.ShapeDtypeStruct((T, MAIN), BF16), jax.ShapeDtypeStruct((T, GLA_H * DK), F32)],
        scratch_shapes=[pltpu.VMEM((DV, DK), F32)], input_output_aliases={8: 0}, compiler_params=_params(2),
    )(proj, proj, proj, proj_lr, w2p, gate_b, states, d_o, dproj)


def _after(values, tokens):
    return lax.optimization_barrier((values, tokens))[0]


def _local_step(x, target, p, reduce_part, reduce_ffn, reduce_in):
    u, u_t = _rms_fwd(x, p["norm1_w"], name="rms1_fwd")
    whole = _pick(x.shape[0], (2048, 256))
    proj = _matmul(u, p["w_in_main"], bm=whole, name="mm_proj")
    proj_lr = _matmul(u, p["w_in_lr"], name="mm_proj_lr")
    attn_o = _attn_fwd(proj, p["sinks"])
    gla_raw, states = _gla_fwd(proj, proj_lr, p["w2p"], p["gate_b"])
    merged, merged_t = _merge_fwd(attn_o, gla_raw, proj, p["gla_norm_w"])
    h1, v2, v2_t = _out_proj(merged, p["w_out"], x, p["norm2_w"])
    gate, up, ff, ff_t = _ffn_fwd(v2, p["w_gate"], p["w_up"])
    h2 = _matmul(ff, p["w_down"], add=h1, name="mm_down")
    loss, dh2, dh2_b, g_final = _final_loss(h2, p["final_norm_w"], target)

    dgate, dup = _ffn_bwd_hidden(dh2_b, p["w_down"], gate, up)
    sent = [reduce_ffn("w_down", ff_t, dh2_b), reduce_ffn("w_gate", v2_t, dgate), reduce_ffn("w_up", v2_t, dup)]
    dgate, dup = _after((dgate, dup), sent)
    dv2 = _ffn_bwd_input(dgate, dup, p["w_gate"], p["w_up"])
    dh1, dh1_b, g_norm2 = _rms_bwd(dv2, h1, p["norm2_w"], dh2, name="rms2_bwd")
    dmerged = _matmul(dh1_b, p["w_out"], tb=True, bj=D, name="mm_dmerged")
    sent = reduce_part("w_out", _matmul(merged_t, dh1_b, out_dtype=BF16, bj=D, name="mm_gout").reshape(4, D // 4, D))
    dmerged = _after(dmerged, sent)
    d_attn, d_gla, dproj, g_gla_norm = _merge_bwd(dmerged, attn_o, gla_raw, proj, p["gla_norm_w"])
    dproj, dlogit = _gla_bwd(proj, proj_lr, p["w2p"], p["gate_b"], states, d_gla, dproj)
    dproj, g_sinks = _attn_bwd(proj, p["sinks"], d_attn, dproj)
    g_gate_b = _colsum(dlogit, name="colsum_gate_b")
    g_w2 = _matmul(proj_lr, dlogit, ta=True, name="mm_gw2")
    dproj_lr = _matmul(dlogit, p["w2p"], tb=True, out_dtype=BF16, name="mm_dlr")
    reduce_in(u_t, dproj, dproj_lr)
    du = _matmul(dproj, p["w_in_main"], tb=True, bm=_pick(x.shape[0], (1024, 256)), bj=1024, bc=3200, name="mm_du")
    dx, g_norm1 = _rms_bwd_input(du, dproj_lr, p["w_in_lr"], x, p["norm1_w"], dh1)
    grads = dict(norm1_w=g_norm1, w2=g_w2[:RANK], gate_b=g_gate_b,
                 sinks=g_sinks[:, 0].reshape(1, N_Q), gla_norm_w=g_gla_norm, norm2_w=g_norm2, final_norm_w=g_final)
    return loss, dx, grads


def _place():
    x, y, c = lax.axis_index("x"), lax.axis_index("y"), lax.axis_index("c")
    chips = [(1 - x, y), (x, 1 - y), (1 - x, 1 - y)]
    return x, y, c, chips


def _hbm_shape(s, dt):
    return jax.ShapeDtypeStruct(s, dt)


ID_SIBLING = 9
ID_GATHER_IN = 10


def _handshake(peers):
    barrier = pltpu.get_barrier_semaphore()
    for peer in peers:
        pl.semaphore_signal(barrier, inc=1, device_id=peer, device_id_type=MESH)
    pl.semaphore_wait(barrier, len(peers))


def _launch_copies(body, args, out_shape, sems, *, name, collective_id=None):
    if collective_id is None:
        return pl.pallas_call(
            body, name=name, in_specs=[ANY] * len(args), out_specs=[ANY] * len(out_shape), out_shape=out_shape,
            scratch_shapes=sems)(*args)
    return pl.kernel(
        body, name=name, out_type=out_shape, mesh=plsc.ScalarSubcoreMesh(axis_name="sequencer", num_cores=1),
        scratch_types=sems, compiler_params=pltpu.CompilerParams(collective_id=collective_id))(*args)


def _gather_shards(shards, *, name, collective_id=None, after=()):
    n = len(shards)
    first_out = n + len(after)

    def body(*refs):
        ins, outs = refs[:n], refs[first_out:first_out + n]
        ici_send, ici_recv, d2d_send, d2d_recv, local_sem = refs[first_out + n:]
        x, y, c, chips = _place()
        me = 2 * x + y
        sibling = (x, y, 1 - c)
        if collective_id is not None:
            _handshake([sibling] + [(*chip, c) for chip in chips])

        def half(w, slot, hc):
            r2 = shards[w].shape[0] // 2
            return outs[w].at[slot, pl.ds(hc * r2, r2), :]

        locals_ = [pltpu.make_async_copy(ins[w], outs[w].at[me], local_sem.at[w]) for w in range(n)]
        for cp in locals_:
            cp.start()
        sends = []
        for w in range(n):
            r2 = shards[w].shape[0] // 2
            for j, chip in enumerate(chips):
                cp = pltpu.make_async_remote_copy(
                    src_ref=ins[w].at[pl.ds(c * r2, r2), :], dst_ref=half(w, me, c),
                    send_sem=ici_send.at[w * 3 + j], recv_sem=ici_recv.at[w * 3 + j],
                    device_id=(*chip, c), device_id_type=MESH)
                cp.start()
                sends.append(cp)
        for w in range(n):
            for j, chip in enumerate(chips):
                slot = 2 * chip[0] + chip[1]
                got = half(w, slot, c)
                pltpu.make_async_remote_copy(
                    src_ref=got, dst_ref=got, send_sem=ici_send.at[w * 3 + j], recv_sem=ici_recv.at[w * 3 + j],
                    device_id=(*chip, c), device_id_type=MESH).wait_recv()
                cp = pltpu.make_async_remote_copy(
                    src_ref=got, dst_ref=got, send_sem=d2d_send.at[w * 3 + j], recv_sem=d2d_recv.at[w * 3 + j],
                    device_id=sibling, device_id_type=MESH)
                cp.start()
                sends.append(cp)
        for w in range(n):
            for j, chip in enumerate(chips):
                slot = 2 * chip[0] + chip[1]
                got = half(w, slot, 1 - c)
                pltpu.make_async_remote_copy(
                    src_ref=got, dst_ref=got, send_sem=d2d_send.at[w * 3 + j], recv_sem=d2d_recv.at[w * 3 + j],
                    device_id=sibling, device_id_type=MESH).wait_recv()
        for cp in sends:
            cp.wait_send()
        for cp in locals_:
            cp.wait()

    return _launch_copies(
        body, list(shards) + list(after), [_hbm_shape((4,) + s.shape, s.dtype) for s in shards],
        [pltpu.SemaphoreType.DMA((3 * n,)), pltpu.SemaphoreType.DMA((3 * n,)), pltpu.SemaphoreType.DMA((3 * n,)),
         pltpu.SemaphoreType.DMA((3 * n,)), pltpu.SemaphoreType.DMA((n,))],
        name=name, collective_id=collective_id)


def _gather_staged(w, small):
    R, C = w.shape
    R2 = R // 2
    rb = 256
    per = R2 // rb
    assert per % 2 == 0
    r2 = small.shape[0] // 2
    order = [(j, k) for k in range(per) for j in range(2)] + [(2, k) for k in range(per)]

    def body(w_ref, s_ref, out_ref, outs_ref, stage, big_send, big_recv, dia_send, dia_recv, small_send, small_recv,
             fwd_send, fwd_recv, in_sem, put_sem, push_send, push_recv):
        x, y, c, chips = _place()
        me = 2 * x + y
        sibling = (x, y, 1 - c)
        slots = [2 * chip[0] + chip[1] for chip in chips]
        _handshake([sibling] + [(*chip, c) for chip in chips])

        def block(j, k, hc):
            return out_ref.at[slots[j], pl.ds(hc * R2 + k * rb, rb), :]

        def big(j, k):
            return pltpu.make_async_remote_copy(
                src_ref=w_ref.at[pl.ds(c * R2 + k * rb, rb), :], dst_ref=out_ref.at[me, pl.ds(c * R2 + k * rb, rb), :],
                send_sem=big_send.at[j * per + k], recv_sem=big_recv.at[j * per + k],
                device_id=(*chips[j], c), device_id_type=MESH)

        def passes_on(j, k):
            return (j == 0) == (k < per // 2)

        def onward(j, k, to):
            return pltpu.make_async_remote_copy(
                src_ref=block(j, k, c), dst_ref=block(j, k, c), send_sem=dia_send.at[k], recv_sem=dia_recv.at[k],
                device_id=(*chips[to], c), device_id_type=MESH)

        def little(j):
            return pltpu.make_async_remote_copy(
                src_ref=s_ref.at[pl.ds(c * r2, r2), :], dst_ref=outs_ref.at[me, pl.ds(c * r2, r2), :],
                send_sem=small_send.at[j], recv_sem=small_recv.at[j], device_id=(*chips[j], c), device_id_type=MESH)

        sends = [big(j, k) for k in range(per) for j in range(2)] + [little(j) for j in range(3)]
        for cp in sends:
            cp.start()
        mine = pltpu.make_async_copy(s_ref, outs_ref.at[me], put_sem.at[2])
        mine.start()

        def own(k, to_vmem):
            rows = pl.ds(k * rb, rb)
            if to_vmem:
                return pltpu.make_async_copy(w_ref.at[rows, :], stage.at[k % 2], in_sem.at[k % 2])
            return pltpu.make_async_copy(stage.at[k % 2], out_ref.at[me, rows, :], put_sem.at[k % 2])

        for k in range(R // rb):
            if k >= 2:
                own(k - 2, False).wait()
            own(k, True).start()
            own(k, True).wait()
            own(k, False).start()
        for k in range(R // rb - 2, R // rb):
            own(k, False).wait()

        def push(t, hc):
            j, k = order[t]
            return pltpu.make_async_remote_copy(
                src_ref=stage.at[t % 2], dst_ref=block(j, k, hc), send_sem=push_send.at[t % 2],
                recv_sem=push_recv.at[t], device_id=sibling, device_id_type=MESH)

        passed = []
        for t, (j, k) in enumerate(order):
            if j == 2:
                onward(2, k, 0).wait_recv()
            else:
                big(j, k).wait_recv()
                if passes_on(j, k):
                    passed.append(onward(j, k, 1 - j))
                    passed[-1].start()
            if t >= 2:
                push(t - 2, c).wait_send()
            fetch = pltpu.make_async_copy(block(j, k, c), stage.at[t % 2], in_sem.at[t % 2])
            fetch.start()
            fetch.wait()
            push(t, c).start()
        for j in range(3):
            got = outs_ref.at[slots[j], pl.ds(c * r2, r2), :]
            little(j).wait_recv()
            fw = pltpu.make_async_remote_copy(
                src_ref=got, dst_ref=got, send_sem=fwd_send.at[j], recv_sem=fwd_recv.at[j],
                device_id=sibling, device_id_type=MESH)
            fw.start()
            passed.append(fw)
        for t in range(len(order) - 2, len(order)):
            push(t, c).wait_send()
        for t in range(len(order)):
            push(t, 1 - c).wait_recv()
        for j in range(3):
            theirs = outs_ref.at[slots[j], pl.ds((1 - c) * r2, r2), :]
            pltpu.make_async_remote_copy(
                src_ref=theirs, dst_ref=theirs, send_sem=fwd_send.at[j], recv_sem=fwd_recv.at[j],
                device_id=sibling, device_id_type=MESH).wait_recv()
        for cp in sends + passed:
            cp.wait_send()
        mine.wait()

    dma = pltpu.SemaphoreType.DMA
    return pl.pallas_call(
        body, name="gather_in", in_specs=[ANY, ANY], out_specs=[ANY, ANY],
        out_shape=[_hbm_shape((4, R, C), w.dtype), _hbm_shape((4,) + small.shape, small.dtype)],
        scratch_shapes=[pltpu.VMEM((2, rb, C), w.dtype), dma((2 * per,)), dma((2 * per,)), dma((per,)), dma((per,)),
                        dma((3,)), dma((3,)), dma((3,)), dma((3,)), dma((2,)), dma((3,)), dma((2,)),
                        dma((len(order),))],
        compiler_params=pltpu.CompilerParams(vmem_limit_bytes=VMEM_LIMIT, collective_id=ID_GATHER_IN),
    )(w, small)


def _pair_blocks(R2):
    return _pick(R2, (256, 352, 128, 64, 32, 16))


def _grad_matmul_split(a, b, *, cut, name):
    M, T = a.shape
    J = b.shape[1]
    bw = None
    if cut == "rows":
        S, R, C, bm = 4, M // 4, J, 352
        per = R // 2 // bm
        n = M // bm

        def where(t):
            return t // (2 * per), (t // per) % 2, t % per, 0

        a_spec = pl.BlockSpec((bm, T), lambda t: (t, 0))
        b_spec = pl.BlockSpec((T, C), lambda t: (0, 0))
    elif cut == "panels":
        S, R, C, bm = 4, M, J // 4, M // 2
        n = 8

        def where(t):
            return t // 2, t % 2, 0, 0

        a_spec = pl.BlockSpec((bm, T), lambda t: (t % 2, 0))
        b_spec = pl.BlockSpec((T, C), lambda t: (0, t // 2))
    else:
        S, R, C, bm, bw = 1, M, J, M // 2, 1280
        n = 2 * (J // bw)

        def where(t):
            return 0, t % 2, 0, t // 2

        a_spec = pl.BlockSpec((bm, T), lambda t: (t % 2, 0))
        b_spec = pl.BlockSpec((T, bw), lambda t: (0, t // 2))
    R2 = R // 2
    tw = bw or C

    def body(a_ref, b_ref, mine_ref, got_ref, tile, loc_sem, snd_sem, rcv_sem):
        x, y, c, _ = _place()
        sibling = (x, y, 1 - c)
        t = pl.program_id(0)

        @pl.when(t == 0)
        def _():
            _handshake([sibling])

        def place(ref, u):
            s, _, i, j = where(u)
            return ref.at[s, pl.ds(i * bm, bm), pl.ds(j * tw, tw)]

        def keep(u):
            return pltpu.make_async_copy(tile.at[u % 2], place(mine_ref, u), loc_sem.at[u % 2])

        def give(u):
            return pltpu.make_async_remote_copy(
                src_ref=tile.at[u % 2], dst_ref=place(got_ref, u), send_sem=snd_sem.at[u % 2],
                recv_sem=rcv_sem, device_id=sibling, device_id_type=MESH)

        def settle(u):
            @pl.when(where(u)[1] == c)
            def _():
                keep(u).wait()

            @pl.when(where(u)[1] != c)
            def _():
                give(u).wait_send()

        @pl.when(t >= 2)
        def _():
            settle(t - 2)

        tile[t % 2] = _dot(a_ref[...], b_ref[...], 1, 0).astype(BF16)

        @pl.when(where(t)[1] == c)
        def _():
            keep(t).start()

        @pl.when(where(t)[1] != c)
        def _():
            give(t).start()

        @pl.when(t == n - 1)
        def _():
            settle(t - 1)
            settle(t)
            pltpu.make_async_remote_copy(
                src_ref=got_ref, dst_ref=got_ref, send_sem=snd_sem.at[0], recv_sem=rcv_sem,
                device_id=sibling, device_id_type=MESH).wait_recv()

    half = _hbm_shape((S, R2, C), BF16)
    return pl.pallas_call(
        body, name=name, grid=(n,), in_specs=[a_spec, b_spec], out_specs=[ANY, ANY], out_shape=[half, half],
        scratch_shapes=[pltpu.VMEM((2, bm, tw), BF16), pltpu.SemaphoreType.DMA((2,)), pltpu.SemaphoreType.DMA((2,)),
                        pltpu.SemaphoreType.DMA],
        compiler_params=pltpu.CompilerParams(
            dimension_semantics=("arbitrary",), vmem_limit_bytes=VMEM_LIMIT, collective_id=ID_SIBLING),
    )(a, b)


def _pair_add(mine, got, *, name):
    S, R2, C = mine.shape
    rb = _pick(R2, (512, 352, 256))
    cb = _pick(C, (2560,)) if C > 4096 else C

    def body(a_ref, b_ref, o_ref):
        o_ref[...] = (a_ref[...].astype(F32) + b_ref[...].astype(F32)).astype(BF16)

    blk = pl.BlockSpec((1, rb, cb), lambda s, i, j: (s, i, j))
    return pl.pallas_call(
        body, name=name, grid=(S, R2 // rb, C // cb), in_specs=[blk, blk], out_specs=blk,
        out_shape=jax.ShapeDtypeStruct(mine.shape, BF16), compiler_params=_params(3),
    )(mine, got)


def _pair_reduce(part, *, name):
    S, R, C = part.shape
    R2 = R // 2
    rb = _pick(R2, (512, 352, 256, 128, 64, 32, 16))
    nblk = R2 // rb
    steps = [(s, i) for s in range(S) for i in range(nblk)]
    n = len(steps)

    def body(part_ref, sums_ref, own_buf, oth_buf, rcv_buf, out_buf, own_sem, oth_sem, out_sem, send_sem, recv_sem):
        x, y, c, _ = _place()
        sibling = (x, y, 1 - c)
        _handshake([sibling])

        def fetch(t, half, buf, sem):
            s, i = steps[t]
            return pltpu.make_async_copy(part_ref.at[s, pl.ds(half * R2 + i * rb, rb), :], buf.at[t % 3], sem.at[t % 3])

        def push(t):
            return pltpu.make_async_remote_copy(
                src_ref=oth_buf.at[t % 3], dst_ref=rcv_buf.at[t % 4], send_sem=send_sem.at[t % 4],
                recv_sem=recv_sem.at[t % 4], device_id=sibling, device_id_type=MESH)

        def store(t):
            s, i = steps[t]
            return pltpu.make_async_copy(out_buf.at[t % 2], sums_ref.at[s, pl.ds(i * rb, rb), :], out_sem.at[t % 2])

        for t in range(min(2, n)):
            fetch(t, c, own_buf, own_sem).start()
            fetch(t, 1 - c, oth_buf, oth_sem).start()
        fetch(0, 1 - c, oth_buf, oth_sem).wait()
        push(0).start()
        for t in range(n):
            if t + 2 < n:
                fetch(t + 2, c, own_buf, own_sem).start()
                fetch(t + 2, 1 - c, oth_buf, oth_sem).start()
            if t + 1 < n:
                fetch(t + 1, 1 - c, oth_buf, oth_sem).wait()
                push(t + 1).start()
            fetch(t, c, own_buf, own_sem).wait()
            push(t).wait()
            if t >= 2:
                store(t - 2).wait()
            out_buf[t % 2] = (own_buf[t % 3].astype(F32) + rcv_buf[t % 4].astype(F32)).astype(BF16)
            store(t).start()
        for t in range(max(n - 2, 0), n):
            store(t).wait()

    def buf(k):
        return pltpu.VMEM((k, rb, C), BF16)

    def sems(k):
        return pltpu.SemaphoreType.DMA((k,))

    return pl.pallas_call(
        body, name=name, in_specs=[ANY], out_specs=ANY, out_shape=_hbm_shape((S, R2, C), BF16),
        scratch_shapes=[buf(3), buf(3), buf(4), buf(2), sems(3), sems(3), sems(2), sems(4), sems(4)],
        compiler_params=pltpu.CompilerParams(vmem_limit_bytes=VMEM_LIMIT, collective_id=ID_SIBLING),
    )(part)


def _chip_exchange(sums, *, name, collective_id=None):
    n = len(sums)

    def body(*refs):
        ins, outs = refs[:n], refs[n:2 * n]
        send_sem, recv_sem, local_sem = refs[2 * n:]
        x, y, c, chips = _place()
        me = 2 * x + y
        if collective_id is not None:
            _handshake([(*chip, c) for chip in chips])
        cps = []
        for w in range(n):
            lc = pltpu.make_async_copy(ins[w].at[me], outs[w].at[me], local_sem.at[w])
            lc.start()
            cps.append(lc)
            for j, chip in enumerate(chips):
                slot = 2 * chip[0] + chip[1]
                rc = pltpu.make_async_remote_copy(
                    src_ref=ins[w].at[slot], dst_ref=outs[w].at[me],
                    send_sem=send_sem.at[w * 3 + j], recv_sem=recv_sem.at[w * 3 + j],
                    device_id=(*chip, c), device_id_type=MESH)
                rc.start()
                cps.append(rc)
        for w in range(n):
            for j, chip in enumerate(chips):
                slot = 2 * chip[0] + chip[1]
                pltpu.make_async_remote_copy(
                    src_ref=ins[w].at[slot], dst_ref=outs[w].at[slot],
                    send_sem=send_sem.at[w * 3 + j], recv_sem=recv_sem.at[w * 3 + j],
                    device_id=(*chip, c), device_id_type=MESH).wait_recv()
        for w in range(n):
            cps[w * 4].wait()
            for j in range(3):
                cps[w * 4 + 1 + j].wait_send()

    return _launch_copies(
        body, sums, [_hbm_shape(s.shape, s.dtype) for s in sums],
        [pltpu.SemaphoreType.DMA((3 * n,)), pltpu.SemaphoreType.DMA((3 * n,)), pltpu.SemaphoreType.DMA((n,))],
        name=name, collective_id=collective_id)


def _sum_join(recv, *, name):
    _, R2, C = recv.shape
    rb = _pair_blocks(R2)
    nblk = R2 // rb

    def body(recv_ref, out_ref, in_buf, acc_buf, in_sem, loc_sem, send_sem, recv_sem):
        x, y, c, _ = _place()
        sibling = (x, y, 1 - c)
        _handshake([sibling])

        def fetch(t):
            return pltpu.make_async_copy(recv_ref.at[:, pl.ds(t * rb, rb), :], in_buf.at[t % 2], in_sem.at[t % 2])

        def rows(t, half):
            return out_ref.at[pl.ds(half * R2 + t * rb, rb), :]

        def put_local(t):
            return pltpu.make_async_copy(acc_buf.at[t % 2], rows(t, c), loc_sem.at[t])

        def put_remote(t, half):
            return pltpu.make_async_remote_copy(
                src_ref=acc_buf.at[t % 2], dst_ref=rows(t, half), send_sem=send_sem.at[t], recv_sem=recv_sem.at[t],
                device_id=sibling, device_id_type=MESH)

        fetch(0).start()
        for t in range(nblk):
            if t + 1 < nblk:
                fetch(t + 1).start()
            fetch(t).wait()
            if t >= 2:
                put_local(t - 2).wait()
                put_remote(t - 2, c).wait_send()
            acc = in_buf[t % 2, 0].astype(F32)
            for s in range(1, 4):
                acc = acc + in_buf[t % 2, s].astype(F32)
            acc_buf[t % 2] = acc
            put_local(t).start()
            put_remote(t, c).start()
        for t in range(max(nblk - 2, 0), nblk):
            put_local(t).wait()
            put_remote(t, c).wait_send()
        for t in range(nblk):
            put_remote(t, 1 - c).wait_recv()

    semn = pltpu.SemaphoreType.DMA((nblk,))
    return pl.pallas_call(
        body, name=name, in_specs=[ANY], out_specs=ANY, out_shape=_hbm_shape((2 * R2, C), F32),
        scratch_shapes=[pltpu.VMEM((2, 4, rb, C), BF16), pltpu.VMEM((2, rb, C), F32),
                        pltpu.SemaphoreType.DMA((2,)), semn, semn, semn],
        compiler_params=pltpu.CompilerParams(vmem_limit_bytes=VMEM_LIMIT, collective_id=ID_SIBLING),
    )(recv)


def _sum_small(pack):
    R, C = pack.shape

    def body(in_ref, out_ref, all_ref, send_sem, recv_sem):
        x, y, c, _ = _place()
        me = 4 * x + 2 * y + c
        all_ref[me] = in_ref[...]
        cps = []
        for k in range(1, 8):
            peer = (x ^ (k >> 2), y ^ ((k >> 1) & 1), c ^ (k & 1))
            cp = pltpu.make_async_remote_copy(
                src_ref=in_ref, dst_ref=all_ref.at[me], send_sem=send_sem.at[k - 1], recv_sem=recv_sem.at[k - 1],
                device_id=peer, device_id_type=MESH)
            cp.start()
            cps.append(cp)
        for k in range(1, 8):
            peer = (x ^ (k >> 2), y ^ ((k >> 1) & 1), c ^ (k & 1))
            slot = 4 * peer[0] + 2 * peer[1] + peer[2]
            pltpu.make_async_remote_copy(
                src_ref=in_ref, dst_ref=all_ref.at[slot], send_sem=send_sem.at[k - 1], recv_sem=recv_sem.at[k - 1],
                device_id=peer, device_id_type=MESH).wait_recv()
        for cp in cps:
            cp.wait_send()
        acc = all_ref[0]
        for d in range(1, 8):
            acc = acc + all_ref[d]
        out_ref[...] = acc

    return pl.pallas_call(
        body, name="sum_small", in_specs=[pl.BlockSpec(memory_space=pltpu.VMEM)],
        out_specs=pl.BlockSpec(memory_space=pltpu.VMEM), out_shape=jax.ShapeDtypeStruct((R, C), F32),
        scratch_shapes=[pltpu.VMEM((8, R, C), F32), pltpu.SemaphoreType.DMA((7,)), pltpu.SemaphoreType.DMA((7,))],
    )(pack)


ADAMW_BLOCK_BYTES = 2 * 1024 * 1024


def _adamw_block(R, C):
    padded = -(-C // 128) * 128
    rows = [rb for rb in range(8, R + 1, 8) if R % rb == 0 and rb * padded * 4 <= ADAMW_BLOCK_BYTES]
    if rows or R * padded * 4 <= ADAMW_BLOCK_BYTES:
        return (max(rows) if rows else R), C
    cols = [cb for cb in range(128, C + 1, 128) if C % cb == 0 and R * cb * 4 <= ADAMW_BLOCK_BYTES]
    return R, max(cols)


def _adamw(w, g, m, v, *, name, pass_grad=False):
    R, C = w.shape
    rb, cb = _adamw_block(R, C)
    c1 = 1.0 / (1.0 - B1 ** STEP)
    c2 = 1.0 / (1.0 - B2 ** STEP)

    def body(w_ref, g_ref, m_ref, v_ref, *outs):
        d_ref, nm_ref, nv_ref = outs[-3:]
        gv = g_ref[...]
        if pass_grad:
            outs[0][...] = gv
        m2 = B1 * m_ref[...] + (1.0 - B1) * gv
        v2 = B2 * v_ref[...] + (1.0 - B2) * gv * gv
        nm_ref[...] = m2
        nv_ref[...] = v2
        d_ref[...] = -LR * ((m2 * c1) / (jnp.sqrt(v2 * c2) + ADAM_EPS) + WD * w_ref[...])

    n_out = 4 if pass_grad else 3
    blk = pl.BlockSpec((rb, cb), lambda i, j: (i, j))
    return pl.pallas_call(
        body, name=name, grid=(R // rb, C // cb), in_specs=[blk] * 4, out_specs=[blk] * n_out,
        out_shape=[jax.ShapeDtypeStruct((R, C), F32)] * n_out, compiler_params=_params(2),
    )(w, g, m, v)


SMALL = (("norm1_w", D), ("norm2_w", D), ("final_norm_w", D), ("gate_b", GLA_H * DK), ("gla_norm_w", DV), ("sinks", N_Q))
PACK_W = 1024


def _pack_small(vals, w2, loss):
    rows = []
    for name, width in SMALL:
        v = vals[name].reshape(-1)
        rows.append(jnp.pad(v, (0, (-width) % PACK_W)).reshape(-1, PACK_W))
    rows.append(w2)
    rows.append(jnp.broadcast_to(loss.reshape(1, 1), (1, PACK_W)))
    pack = jnp.concatenate(rows, axis=0)
    return jnp.pad(pack, ((0, 32 - pack.shape[0]), (0, 0)))


def _unpack_small(pack):
    out, r = {}, 0
    for name, width in SMALL:
        nr = -(-width // PACK_W)
        out[name] = pack[r:r + nr].reshape(-1)[:width]
        r += nr
    out["w2"] = pack[r:r + RANK]
    out["loss"] = pack[r + RANK, 0]
    return out


def kernel(x, norm1_w, w_in, gla_gate_w2, gla_gate_b, attn_sinks, gla_norm_w, w_out, norm2_w, w_ffn_gate, w_ffn_up, w_ffn_down, final_norm_w, loss_target, m_norm1_w, m_w_in, m_gla_gate_w2, m_gla_gate_b, m_attn_sinks, m_gla_norm_w, m_w_out, m_norm2_w, m_w_ffn_gate, m_w_ffn_up, m_w_ffn_down, m_final_norm_w, v_norm1_w, v_w_in, v_gla_gate_w2, v_gla_gate_b, v_attn_sinks, v_gla_norm_w, v_w_out, v_norm2_w, v_w_ffn_gate, v_w_ffn_up, v_w_ffn_down, v_final_norm_w):
    chip = 2 * lax.axis_index("x") + lax.axis_index("y")
    w_in_s, w_out_s, w_gate_s, w_up_s, w_down_s, w2_s = (
        w_in[0], w_out[0], w_ffn_gate[0], w_ffn_up[0], w_ffn_down[0], gla_gate_w2[0])
    CS = SHARD_COLS

    g_in, g_w2 = _gather_staged(w_in_s.astype(BF16), w2_s)
    g_out, = _gather_shards([w_out_s.astype(BF16)], name="gather_out", collective_id=1, after=[g_w2])
    g_gate, g_up = _gather_shards([w_gate_s.astype(BF16), w_up_s.astype(BF16)], name="gather_gate_up",
                                  collective_id=7, after=[g_w2])
    g_down, = _gather_shards([w_down_s.astype(BF16)], name="gather_down", collective_id=8, after=[g_w2])
    w_in_t = lax.optimization_barrier(jnp.transpose(w_in_s))
    m_in_t = lax.optimization_barrier(jnp.transpose(m_w_in[0]))
    v_in_t = lax.optimization_barrier(jnp.transpose(v_w_in[0]))
    w_main, w_lr = _w_in_to_main(_after(g_in, (w_in_t, m_in_t, v_in_t)))
    w2_full = jnp.transpose(g_w2, (1, 0, 2)).reshape(RANK, GLA_H * DK)
    p = dict(
        norm1_w=norm1_w, norm2_w=norm2_w, final_norm_w=final_norm_w.reshape(1, D), gate_b=gla_gate_b,
        gla_norm_w=gla_norm_w, sinks=jnp.pad(attn_sinks, ((0, 0), (0, 128 - N_Q))),
        w_in_main=w_main, w_in_lr=w_lr,
        w2p=jnp.pad(w2_full, ((0, LRP - RANK), (0, 0))).astype(BF16),
        w_out=g_out.reshape(D, D),
        w_gate=g_gate, w_up=g_up, w_down=g_down.reshape(FFN, D),
    )

    tags = ["w_in", "w_out", "w_gate", "w_up", "w_down"]
    recv = {}

    def exchange(tag, sums):
        recv[tag], = _chip_exchange([sums], name="chip_exchange_" + tag, collective_id=2 + tags.index(tag))
        return sums

    def reduce_part(tag, part):
        return exchange(tag, _pair_reduce(part, name="pair_reduce_" + tag))

    def reduce_ffn(tag, a, b):
        mine, got = _grad_matmul_split(a, b, cut="rows" if tag == "w_down" else "panels", name="mm_grad_" + tag)
        return exchange(tag, _pair_add(mine, got, name="pair_add_" + tag))

    def reduce_in(u_t, dproj, dproj_lr):
        mine, got = _grad_matmul_split(u_t, dproj, cut="whole", name="mm_grad_w_in")
        sum_main, = _pair_add(mine, got, name="pair_add_w_in")
        g_lr = _matmul(u_t, dproj_lr, out_dtype=BF16, name="mm_gin_lr")
        sum_lr, = _pair_reduce(g_lr[None], name="pair_reduce_w_in_lr")
        exchange("w_in", _main_to_shards(sum_main, sum_lr))

    done = {}
    shard = {"w_out": (w_out_s, m_w_out[0], v_w_out[0]), "w_gate": (w_gate_s, m_w_ffn_gate[0], v_w_ffn_gate[0]),
             "w_up": (w_up_s, m_w_ffn_up[0], v_w_ffn_up[0]), "w_down": (w_down_s, m_w_ffn_down[0], v_w_ffn_down[0])}

    def finish(tag):
        grad = _sum_join(recv[tag], name="sum_join_" + tag)
        done[tag] = _adamw(shard[tag][0], grad, shard[tag][1], shard[tag][2], name="adamw_" + tag, pass_grad=True)

    loss_blk, dx, g = _local_step(x[0], loss_target[0], p, reduce_part, reduce_ffn, reduce_in)
    for tag in ("w_down", "w_gate", "w_up", "w_out"):
        finish(tag)
    behind = [dx] + [done[tag][1] for tag in ("w_down", "w_gate", "w_up", "w_out")]
    g_in = _sum_join(_after(recv["w_in"], behind), name="sum_join_w_in")

    small = _unpack_small(_sum_small(_pack_small(g, g["w2"], loss_blk[0, 0])))
    loss = small["loss"]
    g_w2_mine = lax.dynamic_slice_in_dim(small["w2"], chip * (GLA_H * DK // 4), GLA_H * DK // 4, axis=1)

    grads = dict(
        norm1_w=small["norm1_w"].reshape(1, D), w_in=g_in, gla_gate_w2=g_w2_mine,
        gla_gate_b=small["gate_b"].reshape(1, -1), attn_sinks=small["sinks"].reshape(1, N_Q),
        gla_norm_w=small["gla_norm_w"].reshape(1, DV), norm2_w=small["norm2_w"].reshape(1, D),
        final_norm_w=small["final_norm_w"].reshape(1, D))
    def lin(a):
        return jnp.transpose(a.reshape(D, CS))

    def unlin(a):
        return jnp.transpose(a)

    grads["w_in"] = lin(g_in)
    weights = dict(
        norm1_w=(norm1_w, m_norm1_w, v_norm1_w), w_in=(w_in_t, m_in_t, v_in_t),
        gla_gate_w2=(w2_s, m_gla_gate_w2[0], v_gla_gate_w2[0]), gla_gate_b=(gla_gate_b, m_gla_gate_b, v_gla_gate_b),
        attn_sinks=(attn_sinks, m_attn_sinks, v_attn_sinks), gla_norm_w=(gla_norm_w, m_gla_norm_w, v_gla_norm_w),
        norm2_w=(norm2_w, m_norm2_w, v_norm2_w),
        final_norm_w=(final_norm_w.reshape(1, D), m_final_norm_w.reshape(1, D), v_final_norm_w.reshape(1, D)))
    names = ["norm1_w", "w_in", "gla_gate_w2", "gla_gate_b", "attn_sinks", "gla_norm_w", "w_out", "norm2_w",
             "w_ffn_gate", "w_ffn_up", "w_ffn_down", "final_norm_w"]
    lead = {"norm1_w": False, "gla_gate_b": False, "attn_sinks": False, "gla_norm_w": False, "norm2_w": False}
    g_out_l, d_out, m_out, v_out = [], [], [], []
    early = {"w_out": "w_out", "w_ffn_gate": "w_gate", "w_ffn_up": "w_up", "w_ffn_down": "w_down"}
    for nm in names:
        if nm in early:
            gr, dl, nmn, nvn = done[early[nm]]
            w = shard[early[nm]][0]
        else:
            w, m, v = weights[nm]
            gr = grads[nm]
            dl, nmn, nvn = _adamw(w, gr, m, v, name="adamw_" + nm)
        if nm == "w_in":
            gr, dl, nmn, nvn, w = g_in, unlin(dl), unlin(nmn), unlin(nvn), w_in_s
        if nm == "final_norm_w":
            shape = (D,)
        elif nm in lead:
            shape = w.shape
        else:
            shape = (1,) + w.shape
        g_out_l.append(gr.reshape(shape))
        d_out.append(dl.reshape(shape))
        m_out.append(nmn.reshape(shape))
        v_out.append(nvn.reshape(shape))
    return (loss, dx[None], *g_out_l, *d_out, *m_out, *v_out)
```

```python
import functools

import jax
import jax.numpy as jnp
from jax import lax
from jax.experimental import pallas as pl
from jax.experimental.pallas import tpu as pltpu
from jax.experimental.pallas import tpu_sc as plsc

F32 = jnp.float32
BF16 = jnp.bfloat16

D = 2048
HEAD_DIM = 64
N_Q = 32
N_KV = 4
GROUP = 8
WINDOW = 128
GLA_H = 4
DK = 256
DV = 512
RANK = 16
CHUNK = 64
FFN = 5632
EPS = 1e-6
MASK_VALUE = -1e30
GATE_NORM = 16.0
LR, B1, B2, ADAM_EPS, WD, STEP = 0.001, 0.9, 0.999, 1e-08, 0.01, 10

MAIN = 12800
O_MERGE, O_GLA, O_ATTN = 0, 6144, 10240
W_MERGE, W_GLA, W_ATTN = 3 * DV, 2 * DK + DV, D + 2 * N_KV * HEAD_DIM
LR_AT = 6656
LRP = 128


def _main_pieces():
    o_gq, o_gk, o_gv, o_gr, o_ga, o_gb = 2560, 3584, 4608, LR_AT + RANK, LR_AT + RANK + D, LR_AT + RANK + 2 * D
    pieces = []
    for h in range(GLA_H):
        pieces += [(o_gr + DV * h, DV), (o_ga + DV * h, DV), (o_gb + DV * h, DV)]
    for h in range(GLA_H):
        pieces += [(o_gq + DK * h, DK), (o_gk + DK * h, DK), (o_gv + DV * h, DV)]
    pieces.append((0, W_ATTN))
    return pieces


SHARD_COLS = 3204

VMEM_LIMIT = 56 * 1024 * 1024
MESH = pl.DeviceIdType.MESH
ANY = pl.BlockSpec(memory_space=pl.ANY)


def _params(ngrid):
    return pltpu.CompilerParams(dimension_semantics=("arbitrary",) * ngrid, vmem_limit_bytes=VMEM_LIMIT)


def _shard_segments(lo, hi):
    segs = []
    while lo < hi:
        s = lo // SHARD_COLS
        e = min(hi, (s + 1) * SHARD_COLS)
        segs.append((s, lo - s * SHARD_COLS, e - s * SHARD_COLS))
        lo = e
    return segs


def _lanes(pieces):
    return pieces[0] if len(pieces) == 1 else jnp.concatenate(pieces, axis=1)


def _w_in_to_main(shards):
    rb = 256

    def body(g_ref, main_ref, lr_ref):
        at = 0
        for a, w in _main_pieces():
            main_ref[:, at:at + w] = _lanes([g_ref[s, :, lo:hi] for s, lo, hi in _shard_segments(a, a + w)])
            at += w
        lr = [g_ref[s, :, lo:hi] for s, lo, hi in _shard_segments(LR_AT, LR_AT + RANK)]
        lr_ref[...] = _lanes(lr + [jnp.zeros((rb, LRP - RANK), lr_ref.dtype)])

    return pl.pallas_call(
        body, name="w_in_to_main", grid=(D // rb,),
        in_specs=[pl.BlockSpec((4, rb, SHARD_COLS), lambda i: (0, i, 0))],
        out_specs=[pl.BlockSpec((rb, MAIN), lambda i: (i, 0)), pl.BlockSpec((rb, LRP), lambda i: (i, 0))],
        out_shape=[jax.ShapeDtypeStruct((D, MAIN), shards.dtype), jax.ShapeDtypeStruct((D, LRP), shards.dtype)],
        compiler_params=_params(1),
    )(shards)


def _main_to_shards(g_main, g_lr):
    rb = 256
    rows = g_main.shape[0]
    at, sources = 0, [(LR_AT, RANK, 1, 0)]
    for a, w in _main_pieces():
        sources.append((a, w, 0, at))
        at += w
    sources.sort()

    def body(main_ref, lr_ref, out_ref):
        refs = (main_ref, lr_ref)
        for s in range(4):
            lo, hi = s * SHARD_COLS, (s + 1) * SHARD_COLS
            pieces = []
            for a, w, which, src in sources:
                b, e = max(a, lo), min(a + w, hi)
                if b < e:
                    pieces.append(refs[which][:, src + b - a:src + e - a])
            out_ref[s] = _lanes(pieces)

    return pl.pallas_call(
        body, name="main_to_shards", grid=(rows // rb,),
        in_specs=[pl.BlockSpec((rb, MAIN), lambda i: (i, 0)), pl.BlockSpec((rb, LRP), lambda i: (i, 0))],
        out_specs=pl.BlockSpec((4, rb, SHARD_COLS), lambda i: (0, i, 0)),
        out_shape=jax.ShapeDtypeStruct((4, rows, SHARD_COLS), g_main.dtype), compiler_params=_params(1),
    )(g_main, g_lr)


def _pick(n, cands):
    for c in cands:
        if n % c == 0:
            return c
    return n


def _sigmoid(x):
    return 1.0 / (1.0 + jnp.exp(-x))


def _dot(a, b, ca, cb):
    return lax.dot_general(a, b, (((ca,), (cb,)), ((), ())), preferred_element_type=F32)


def _matmul(a, b, *, name, ta=False, tb=False, add=None, out_dtype=F32, bm=None, bj=None, bc=None,
            b_slots=False, out_slots=False):
    C, M = a.shape if ta else a.shape[::-1]
    if b_slots:
        if tb:
            J, bc = b.shape[1], b.shape[2]
            assert b.shape[0] * bc == C
        else:
            bj = b.shape[2]
            J = b.shape[0] * bj
            assert b.shape[1] == C
    else:
        J = b.shape[0] if tb else b.shape[1]
        assert (b.shape[1] if tb else b.shape[0]) == C
    bm = bm or _pick(M, (1024, 512, 256, 128) if C <= FFN else (512, 256, 128))
    bj = bj or _pick(J, (1280, 1024, 512, 256, 128) if C <= D else (512, 256, 128))
    bc = bc or (C if C <= FFN else _pick(C, (6400,)))
    nm, nj, nc = M // bm, J // bj, C // bc
    has_add = add is not None

    def body(*refs):
        a_ref, b_ref = refs[0], refs[1]
        add_ref = refs[2] if has_add else None
        o_ref = refs[3] if has_add else refs[2]
        p = _dot(a_ref[...].astype(BF16), b_ref[...].astype(BF16), 0 if ta else 1, 1 if tb else 0)

        def finish(acc):
            if has_add:
                acc = acc + add_ref[...]
            o_ref[...] = acc.astype(o_ref.dtype)

        if nc == 1:
            finish(p)
        else:
            acc_ref = refs[-1]
            c = pl.program_id(2)

            @pl.when(c == 0)
            def _():
                acc_ref[...] = p

            @pl.when(c > 0)
            def _():
                acc_ref[...] += p

            @pl.when(c == nc - 1)
            def _():
                finish(acc_ref[...])

    a_spec = pl.BlockSpec((bc, bm), lambda m, j, c: (c, m)) if ta else pl.BlockSpec((bm, bc), lambda m, j, c: (m, c))
    if b_slots:
        b_spec = (pl.BlockSpec((None, bj, bc), lambda m, j, c: (c, j, 0)) if tb
                  else pl.BlockSpec((None, bc, bj), lambda m, j, c: (j, c, 0)))
    else:
        b_spec = (pl.BlockSpec((bj, bc), lambda m, j, c: (j, c)) if tb
                  else pl.BlockSpec((bc, bj), lambda m, j, c: (c, j)))
    if out_slots:
        assert not has_add
        o_spec = pl.BlockSpec((None, bm, bj), lambda m, j, c: (j, m, 0))
        out_shape = jax.ShapeDtypeStruct((nj, M, bj), out_dtype)
    else:
        o_spec = pl.BlockSpec((bm, bj), lambda m, j, c: (m, j))
        out_shape = jax.ShapeDtypeStruct((M, J), out_dtype)
    in_specs = [a_spec, b_spec] + ([o_spec] if has_add else [])
    args = (a, b) + ((add,) if has_add else ())
    return pl.pallas_call(
        body, name=name, grid=(nm, nj, nc), in_specs=in_specs, out_specs=o_spec,
        out_shape=out_shape,
        scratch_shapes=[pltpu.VMEM((bm, bj), F32)] if nc > 1 else [],
        compiler_params=_params(3),
    )(*args)


def _rows(T):
    return _pick(T, (256, 128))


def _rms_fwd(x, w, *, name):
    T = x.shape[0]
    rb = _rows(T)

    def body(x_ref, w_ref, u_ref, ut_ref):
        xv = x_ref[...]
        r = lax.rsqrt(jnp.mean(xv * xv, axis=-1, keepdims=True) + EPS)
        u = xv * r * w_ref[...]
        u_ref[...] = u.astype(BF16)
        ut_ref[...] = u.T.astype(BF16)

    return pl.pallas_call(
        body, name=name, grid=(T // rb,),
        in_specs=[pl.BlockSpec((rb, D), lambda i: (i, 0)), pl.BlockSpec((1, D), lambda i: (0, 0))],
        out_specs=[pl.BlockSpec((rb, D), lambda i: (i, 0)), pl.BlockSpec((D, rb), lambda i: (0, i))],
        out_shape=[jax.ShapeDtypeStruct((T, D), BF16), jax.ShapeDtypeStruct((D, T), BF16)], compiler_params=_params(1),
    )(x, w)


def _out_proj(merged, w_out, x, norm_w):
    T = merged.shape[0]
    bm = _pick(T, (512, 256))

    def body(a_ref, b_ref, x_ref, w_ref, h_ref, v_ref, vt_ref):
        h = _dot(a_ref[...], b_ref[...], 1, 0) + x_ref[...]
        h_ref[...] = h
        r = lax.rsqrt(jnp.mean(h * h, axis=-1, keepdims=True) + EPS)
        v2 = h * r * w_ref[...]
        v_ref[...] = v2.astype(BF16)
        vt_ref[...] = v2.T.astype(BF16)

    blk = pl.BlockSpec((bm, D), lambda i: (i, 0))
    return pl.pallas_call(
        body, name="out_proj", grid=(T // bm,),
        in_specs=[blk, pl.BlockSpec((D, D), lambda i: (0, 0)), blk, pl.BlockSpec((1, D), lambda i: (0, 0))],
        out_specs=[blk, blk, pl.BlockSpec((D, bm), lambda i: (0, i))],
        out_shape=[jax.ShapeDtypeStruct((T, D), F32), jax.ShapeDtypeStruct((T, D), BF16),
                   jax.ShapeDtypeStruct((D, T), BF16)],
        compiler_params=_params(1),
    )(merged, w_out, x, norm_w)


def _rms_bwd(dy, h, w, resid, *, name):
    T = h.shape[0]
    rb = _rows(T)

    def body(dy_ref, h_ref, w_ref, res_ref, dh_ref, dhb_ref, dw_ref):
        @pl.when(pl.program_id(0) == 0)
        def _():
            dw_ref[...] = jnp.zeros_like(dw_ref)

        hv = h_ref[...]
        r = lax.rsqrt(jnp.mean(hv * hv, axis=-1, keepdims=True) + EPS)
        hn = hv * r
        dyv = dy_ref[...]
        dw_ref[...] += jnp.sum(dyv * hn, axis=0, keepdims=True)
        t = dyv * w_ref[...]
        dh = res_ref[...] + r * (t - hn * jnp.mean(t * hn, axis=-1, keepdims=True))
        dh_ref[...] = dh
        dhb_ref[...] = dh.astype(BF16)

    blk = pl.BlockSpec((rb, D), lambda i: (i, 0))
    vec = pl.BlockSpec((1, D), lambda i: (0, 0))
    return pl.pallas_call(
        body, name=name, grid=(T // rb,), in_specs=[blk, blk, vec, blk], out_specs=[blk, blk, vec],
        out_shape=[jax.ShapeDtypeStruct((T, D), F32), jax.ShapeDtypeStruct((T, D), BF16),
                   jax.ShapeDtypeStruct((1, D), F32)],
        compiler_params=_params(1),
    )(dy, h, w, resid)


def _rms_bwd_input(du_main, dproj_lr, w_in_lr, x, w, resid):
    T = x.shape[0]
    rb = _rows(T)

    def body(du_ref, dl_ref, wl_ref, x_ref, w_ref, res_ref, dx_ref, dw_ref):
        @pl.when(pl.program_id(0) == 0)
        def _():
            dw_ref[...] = jnp.zeros_like(dw_ref)

        xv = x_ref[...]
        r = lax.rsqrt(jnp.mean(xv * xv, axis=-1, keepdims=True) + EPS)
        xn = xv * r
        du = du_ref[...] + _dot(dl_ref[...], wl_ref[...], 1, 1)
        dw_ref[...] += jnp.sum(du * xn, axis=0, keepdims=True)
        t = du * w_ref[...]
        dx_ref[...] = res_ref[...] + r * (t - xn * jnp.mean(t * xn, axis=-1, keepdims=True))

    blk = pl.BlockSpec((rb, D), lambda i: (i, 0))
    vec = pl.BlockSpec((1, D), lambda i: (0, 0))
    return pl.pallas_call(
        body, name="rms1_bwd", grid=(T // rb,),
        in_specs=[blk, pl.BlockSpec((rb, LRP), lambda i: (i, 0)), pl.BlockSpec((D, LRP), lambda i: (0, 0)), blk, vec,
                  blk],
        out_specs=[blk, vec], out_shape=[jax.ShapeDtypeStruct((T, D), F32), jax.ShapeDtypeStruct((1, D), F32)],
        compiler_params=_params(1),
    )(du_main, dproj_lr, w_in_lr, x, w, resid)


def _final_loss(h, w, target):
    T = h.shape[0]
    rb = _rows(T)

    def body(h_ref, w_ref, t_ref, loss_ref, dh_ref, dhb_ref, dw_ref):
        @pl.when(pl.program_id(0) == 0)
        def _():
            dw_ref[...] = jnp.zeros_like(dw_ref)
            loss_ref[...] = jnp.zeros_like(loss_ref)

        hv = h_ref[...]
        wv = w_ref[...]
        r = lax.rsqrt(jnp.mean(hv * hv, axis=-1, keepdims=True) + EPS)
        hn = hv * r
        e = hn * wv - t_ref[...]
        row = jnp.sum(e * e, axis=-1, keepdims=True) * (0.5 / D)
        loss_ref[...] += jnp.broadcast_to(jnp.sum(row, axis=0, keepdims=True), loss_ref.shape)
        dy = e * (1.0 / D)
        dw_ref[...] += jnp.sum(dy * hn, axis=0, keepdims=True)
        t = dy * wv
        dh = r * (t - hn * jnp.mean(t * hn, axis=-1, keepdims=True))
        dh_ref[...] = dh
        dhb_ref[...] = dh.astype(BF16)

    blk = pl.BlockSpec((rb, D), lambda i: (i, 0))
    vec = pl.BlockSpec((1, D), lambda i: (0, 0))
    return pl.pallas_call(
        body, name="final_loss", grid=(T // rb,), in_specs=[blk, vec, blk],
        out_specs=[pl.BlockSpec((8, 128), lambda i: (0, 0)), blk, blk, vec],
        out_shape=[jax.ShapeDtypeStruct((8, 128), F32), jax.ShapeDtypeStruct((T, D), F32),
                   jax.ShapeDtypeStruct((T, D), BF16), jax.ShapeDtypeStruct((1, D), F32)],
        compiler_params=_params(1),
    )(h, w, target)


PANEL = FFN // 4


def _ffn_fwd(v2, w_gate, w_up):
    T = v2.shape[0]
    bm = _pick(T, (512, 256))

    def body(a_ref, bg_ref, bu_ref, gate_ref, up_ref, ff_ref, fft_ref):
        a = a_ref[...]
        g = _dot(a, bg_ref[...], 1, 0)
        u = _dot(a, bu_ref[...], 1, 0)
        gate_ref[...] = g
        up_ref[...] = u
        ff = g * _sigmoid(g) * u
        ff_ref[...] = ff.astype(BF16)
        fft_ref[...] = ff.T.astype(BF16)

    panel = pl.BlockSpec((None, D, PANEL), lambda j, m: (j, 0, 0))
    out = pl.BlockSpec((bm, PANEL), lambda j, m: (m, j))
    return pl.pallas_call(
        body, name="ffn_fwd", grid=(4, T // bm),
        in_specs=[pl.BlockSpec((bm, D), lambda j, m: (m, 0)), panel, panel],
        out_specs=[out, out, out, pl.BlockSpec((PANEL, bm), lambda j, m: (j, m))],
        out_shape=[jax.ShapeDtypeStruct((T, FFN), F32), jax.ShapeDtypeStruct((T, FFN), F32),
                   jax.ShapeDtypeStruct((T, FFN), BF16), jax.ShapeDtypeStruct((FFN, T), BF16)],
        compiler_params=_params(2),
    )(v2, w_gate, w_up)


def _ffn_bwd_hidden(dh2, w_down, gate, up):
    T = dh2.shape[0]
    bm = _pick(T, (512, 256))

    def body(a_ref, b_ref, g_ref, u_ref, dg_ref, du_ref):
        d = _dot(a_ref[...], b_ref[...], 1, 1)
        g = g_ref[...]
        sg = _sigmoid(g)
        du_ref[...] = (d * g * sg).astype(BF16)
        dg_ref[...] = (d * u_ref[...] * sg * (1.0 + g * (1.0 - sg))).astype(BF16)

    blk = pl.BlockSpec((bm, PANEL), lambda j, m: (m, j))
    return pl.pallas_call(
        body, name="ffn_bwd_hidden", grid=(4, T // bm),
        in_specs=[pl.BlockSpec((bm, D), lambda j, m: (m, 0)), pl.BlockSpec((PANEL, D), lambda j, m: (j, 0)), blk, blk],
        out_specs=[blk, blk], out_shape=[jax.ShapeDtypeStruct((T, FFN), BF16)] * 2, compiler_params=_params(2),
    )(dh2, w_down, gate, up)


def _ffn_bwd_input(dgate, dup, w_gate, w_up):
    T = dgate.shape[0]
    bm = _pick(T, (1024, 512, 256))
    bj = 1024

    def body(ag_ref, au_ref, bg_ref, bu_ref, o_ref, acc_ref):
        c = pl.program_id(2)
        p = _dot(ag_ref[...], bg_ref[...], 1, 1) + _dot(au_ref[...], bu_ref[...], 1, 1)

        @pl.when(c == 0)
        def _():
            acc_ref[...] = p

        @pl.when(c > 0)
        def _():
            acc_ref[...] += p

        @pl.when(c == 3)
        def _():
            o_ref[...] = acc_ref[...]

    a = pl.BlockSpec((bm, PANEL), lambda m, j, c: (m, c))
    b = pl.BlockSpec((None, bj, PANEL), lambda m, j, c: (c, j, 0))
    return pl.pallas_call(
        body, name="ffn_bwd_input", grid=(T // bm, D // bj, 4), in_specs=[a, a, b, b],
        out_specs=pl.BlockSpec((bm, bj), lambda m, j, c: (m, j)), out_shape=jax.ShapeDtypeStruct((T, D), F32),
        scratch_shapes=[pltpu.VMEM((bm, bj), F32)], compiler_params=_params(3),
    )(dgate, dup, w_gate, w_up)


def _colsum(x, *, name):
    T, W = x.shape
    rb = _rows(T)

    def body(x_ref, o_ref):
        @pl.when(pl.program_id(0) == 0)
        def _():
            o_ref[...] = jnp.zeros_like(o_ref)

        o_ref[...] += jnp.sum(x_ref[...], axis=0, keepdims=True)

    return pl.pallas_call(
        body, name=name, grid=(T // rb,), in_specs=[pl.BlockSpec((rb, W), lambda i: (i, 0))],
        out_specs=pl.BlockSpec((1, W), lambda i: (0, 0)), out_shape=jax.ShapeDtypeStruct((1, W), F32),
        compiler_params=_params(1),
    )(x)


def _merge_fwd(attn_o, gla_raw, proj, gla_norm_w):
    T = attn_o.shape[0]
    rb = _rows(T)

    def body(a_ref, g_ref, gr_ref, ga_ref, gb_ref, w_ref, o_ref, ot_ref):
        graw = g_ref[...]
        r = lax.rsqrt(jnp.mean(graw * graw, axis=-1, keepdims=True) + EPS)
        gr = gr_ref[...]
        go = graw * r * w_ref[...] * (gr * _sigmoid(gr))
        merged = _sigmoid(ga_ref[...]) * a_ref[...] + _sigmoid(gb_ref[...]) * go
        o_ref[...] = merged.astype(BF16)
        ot_ref[...] = merged.T.astype(BF16)

    def sec(k):
        return pl.BlockSpec((rb, DV), lambda i, h: (i, O_MERGE // DV + 3 * h + k))

    blk = pl.BlockSpec((rb, DV), lambda i, h: (i, h))
    return pl.pallas_call(
        body, name="merge_fwd", grid=(T // rb, GLA_H),
        in_specs=[blk, blk, sec(0), sec(1), sec(2), pl.BlockSpec((1, DV), lambda i, h: (0, 0))],
        out_specs=[blk, pl.BlockSpec((DV, rb), lambda i, h: (h, i))],
        out_shape=[jax.ShapeDtypeStruct((T, D), BF16), jax.ShapeDtypeStruct((D, T), BF16)], compiler_params=_params(2),
    )(attn_o, gla_raw, proj, proj, proj, gla_norm_w)


def _merge_bwd(dm, attn_o, gla_raw, proj, gla_norm_w):
    T = attn_o.shape[0]
    rb = _rows(T)

    def body(dm_ref, a_ref, g_ref, gr_ref, ga_ref, gb_ref, w_ref, da_ref, dg_ref, dp_ref, dw_ref):
        @pl.when((pl.program_id(0) == 0) & (pl.program_id(1) == 0))
        def _():
            dw_ref[...] = jnp.zeros_like(dw_ref)

        dmv = dm_ref[...]
        av = a_ref[...]
        graw = g_ref[...]
        gr = gr_ref[...]
        wv = w_ref[...]
        sa = _sigmoid(ga_ref[...])
        sb = _sigmoid(gb_ref[...])
        r = lax.rsqrt(jnp.mean(graw * graw, axis=-1, keepdims=True) + EPS)
        gnh = graw * r
        gn = gnh * wv
        sr = _sigmoid(gr)
        sl = gr * sr
        go = gn * sl
        da_ref[...] = (dmv * sa).astype(BF16)
        dgo = dmv * sb
        dp_ref[:, 0:DV] = (dgo * gn * sr * (1.0 + gr * (1.0 - sr))).astype(BF16)
        dp_ref[:, DV:2 * DV] = (dmv * av * sa * (1.0 - sa)).astype(BF16)
        dp_ref[:, 2 * DV:3 * DV] = (dmv * go * sb * (1.0 - sb)).astype(BF16)
        dgn = dgo * sl
        dw_ref[...] += jnp.sum(dgn * gnh, axis=0, keepdims=True)
        t = dgn * wv
        dg_ref[...] = (r * (t - gnh * jnp.mean(t * gnh, axis=-1, keepdims=True))).astype(BF16)

    def sec(k):
        return pl.BlockSpec((rb, DV), lambda i, h: (i, O_MERGE // DV + 3 * h + k))

    blk = pl.BlockSpec((rb, DV), lambda i, h: (i, h))
    vec = pl.BlockSpec((1, DV), lambda i, h: (0, 0))
    return pl.pallas_call(
        body, name="merge_bwd", grid=(T // rb, GLA_H),
        in_specs=[blk, blk, blk, sec(0), sec(1), sec(2), vec],
        out_specs=[blk, blk, pl.BlockSpec((rb, W_MERGE), lambda i, h: (i, O_MERGE // W_MERGE + h)), vec],
        out_shape=[jax.ShapeDtypeStruct((T, D), BF16), jax.ShapeDtypeStruct((T, D), BF16),
                   jax.ShapeDtypeStruct((T, MAIN), BF16), jax.ShapeDtypeStruct((1, DV), F32)],
        compiler_params=_params(2),
    )(dm, attn_o, gla_raw, proj, proj, proj, gla_norm_w)


def _attn_band(n):
    qi = lax.broadcasted_iota(jnp.int32, (GROUP * WINDOW, WINDOW), 0) & (WINDOW - 1)
    kj = lax.broadcasted_iota(jnp.int32, (GROUP * WINDOW, WINDOW), 1)
    cur = kj <= qi
    return cur, cur | (n > 0)


def _stack_heads(ref, h, scale=1.0):
    rows = jnp.concatenate(
        [ref[:, (h * GROUP + g) * HEAD_DIM:(h * GROUP + g + 1) * HEAD_DIM] for g in range(GROUP)], axis=0)
    return (rows * scale).astype(BF16) if scale != 1.0 else rows.astype(BF16)


def _stack_sinks(s_ref, h):
    return jnp.concatenate(
        [jnp.broadcast_to(s_ref[:, h * GROUP + g:h * GROUP + g + 1], (WINDOW, 1)) for g in range(GROUP)], axis=0)


def _attn_probs(qs, kp, kc, band, sink):
    cur, live = band
    s = jnp.where(cur, _dot(qs, kc, 1, 1), _dot(qs, kp, 1, 1))
    s = jnp.where(live, s, MASK_VALUE)
    m = jnp.maximum(jnp.max(s, axis=-1, keepdims=True), sink)
    e = jnp.exp(s - m)
    es = jnp.exp(sink - m)
    inv = 1.0 / (jnp.sum(e, axis=-1, keepdims=True) + es)
    return e * inv, es * inv


def _unfold(cur, x):
    mine = jnp.where(cur, x, 0.0)
    return mine.astype(BF16), (x - mine).astype(BF16)


def _attn_specs(nb, rev):
    def at(n):
        return (nb - 1 - n) if rev else n

    kcol, vcol = (O_ATTN + D) // 256, (O_ATTN + D) // 256 + 1
    q = pl.BlockSpec((WINDOW, D), lambda n: (at(n), O_ATTN // D))
    kc = pl.BlockSpec((WINDOW, 256), lambda n: (at(n), kcol))
    kp = pl.BlockSpec((WINDOW, 256), lambda n: (jnp.maximum(at(n) - 1, 0), kcol))
    vc = pl.BlockSpec((WINDOW, 256), lambda n: (at(n), vcol))
    vp = pl.BlockSpec((WINDOW, 256), lambda n: (jnp.maximum(at(n) - 1, 0), vcol))
    sk = pl.BlockSpec((1, 128), lambda n: (0, 0))
    o = pl.BlockSpec((WINDOW, D), lambda n: (at(n), 0))
    return q, kc, kp, vc, vp, sk, o


def _attn_fwd(proj, sinks):
    T = proj.shape[0]
    nb = T // WINDOW

    def body(q_ref, kc_ref, kp_ref, vc_ref, vp_ref, s_ref, o_ref):
        band = _attn_band(pl.program_id(0))
        for h in range(N_KV):
            hs = slice(h * HEAD_DIM, (h + 1) * HEAD_DIM)
            p, _ = _attn_probs(_stack_heads(q_ref, h, HEAD_DIM ** -0.5), kp_ref[:, hs].astype(BF16),
                               kc_ref[:, hs].astype(BF16), band, _stack_sinks(s_ref, h))
            p_cur, p_prev = _unfold(band[0], p)
            o = _dot(p_cur, vc_ref[:, hs].astype(BF16), 1, 0) + _dot(p_prev, vp_ref[:, hs].astype(BF16), 1, 0)
            for g in range(GROUP):
                hg = h * GROUP + g
                o_ref[:, hg * HEAD_DIM:(hg + 1) * HEAD_DIM] = o[g * WINDOW:(g + 1) * WINDOW]

    q, kc, kp, vc, vp, sk, o = _attn_specs(nb, False)
    return pl.pallas_call(
        body, name="attn_fwd", grid=(nb,), in_specs=[q, kc, kp, vc, vp, sk], out_specs=o,
        out_shape=jax.ShapeDtypeStruct((T, D), F32), compiler_params=_params(1),
    )(proj, proj, proj, proj, proj, sinks)


def _attn_bwd(proj, sinks, d_o, dproj):
    T = proj.shape[0]
    nb = T // WINDOW
    kat, vat = D, D + N_KV * HEAD_DIM

    def body(q_ref, kc_ref, kp_ref, vc_ref, vp_ref, s_ref, do_ref, _, dp_ref, ds_ref, ck_ref, cv_ref):
        i = pl.program_id(0)
        n = nb - 1 - i

        @pl.when(i == 0)
        def _():
            ck_ref[...] = jnp.zeros_like(ck_ref)
            cv_ref[...] = jnp.zeros_like(cv_ref)
            ds_ref[...] = jnp.zeros_like(ds_ref)

        band = _attn_band(n)
        for h in range(N_KV):
            hs = slice(h * HEAD_DIM, (h + 1) * HEAD_DIM)
            kp, kc = kp_ref[:, hs].astype(BF16), kc_ref[:, hs].astype(BF16)
            vp, vc = vp_ref[:, hs].astype(BF16), vc_ref[:, hs].astype(BF16)
            qs = _stack_heads(q_ref, h, HEAD_DIM ** -0.5)
            do = _stack_heads(do_ref, h)
            p, ps = _attn_probs(qs, kp, kc, band, _stack_sinks(s_ref, h))
            dp = jnp.where(band[0], _dot(do, vc, 1, 1), _dot(do, vp, 1, 1))
            delta = jnp.sum(p * dp, axis=-1, keepdims=True)
            ds_cur, ds_prev = _unfold(band[0], p * (dp - delta))
            p_cur, p_prev = _unfold(band[0], p)
            dsink = -ps * delta
            dq = ((_dot(ds_cur, kc, 1, 0) + _dot(ds_prev, kp, 1, 0)) * (HEAD_DIM ** -0.5)).astype(BF16)
            for g in range(GROUP):
                hg = h * GROUP + g
                rows = slice(g * WINDOW, (g + 1) * WINDOW)
                ds_ref[hg:hg + 1, :] += jnp.broadcast_to(jnp.sum(dsink[rows], axis=0, keepdims=True), (1, 128))
                dp_ref[:, hg * HEAD_DIM:(hg + 1) * HEAD_DIM] = dq[rows]
            dp_ref[:, kat + h * HEAD_DIM:kat + (h + 1) * HEAD_DIM] = (_dot(ds_cur, qs, 0, 0) + ck_ref[:, hs]).astype(BF16)
            dp_ref[:, vat + h * HEAD_DIM:vat + (h + 1) * HEAD_DIM] = (_dot(p_cur, do, 0, 0) + cv_ref[:, hs]).astype(BF16)
            ck_ref[:, hs] = _dot(ds_prev, qs, 0, 0)
            cv_ref[:, hs] = _dot(p_prev, do, 0, 0)

    q, kc, kp, vc, vp, sk, o = _attn_specs(nb, True)
    return pl.pallas_call(
        body, name="attn_bwd", grid=(nb,), in_specs=[q, kc, kp, vc, vp, sk, o, ANY],
        out_specs=[pl.BlockSpec((WINDOW, W_ATTN), lambda n: (nb - 1 - n, O_ATTN // W_ATTN)),
                   pl.BlockSpec((N_Q, 128), lambda n: (0, 0))],
        out_shape=[jax.ShapeDtypeStruct((T, MAIN), BF16), jax.ShapeDtypeStruct((N_Q, 128), F32)],
        scratch_shapes=[pltpu.VMEM((WINDOW, 256), F32), pltpu.VMEM((WINDOW, 256), F32)],
        input_output_aliases={7: 0}, compiler_params=_params(1),
    )(proj, proj, proj, proj, proj, sinks, d_o, dproj)


def _split3(x):
    hi = x.astype(BF16)
    r1 = x - hi.astype(F32)
    mid = r1.astype(BF16)
    lo = (r1 - mid.astype(F32)).astype(BF16)
    return hi, mid, lo


def _tri_sum(tri, x):
    hi, mid, lo = _split3(x)
    return _dot(tri, hi, 1, 0) + _dot(tri, mid, 1, 0) + _dot(tri, lo, 1, 0)


def _chunk_masks(rb):
    r = lax.broadcasted_iota(jnp.int32, (rb, rb), 0)
    c = lax.broadcasted_iota(jnp.int32, (rb, rb), 1)
    same = (r ^ c) < CHUNK
    return same & (r >= c), same & (r <= c)


def _per_chunk_rows(x, per, pick):
    return jnp.concatenate(
        [jnp.broadcast_to(pick(x[i * CHUNK:(i + 1) * CHUNK]), (CHUNK, x.shape[1])) for i in range(per)], axis=0)


def _gla_block(q, k, lr, w2, b, lower):
    per = q.shape[0] // CHUNK
    logit = _dot(lr.astype(BF16), w2.astype(BF16), 1, 0) + b
    la = (jnp.minimum(logit, 0.0) - jnp.log(1.0 + jnp.exp(-jnp.abs(logit)))) * (1.0 / GATE_NORM)
    g = _tri_sum(lower, la)
    gl = _per_chunk_rows(g, per, lambda c: c[CHUNK - 1:CHUNK])
    eg = jnp.exp(g)
    qd = q * (DK ** -0.5) * eg
    ki = k * jnp.exp(-g)
    ke = k * jnp.exp(gl - g)
    return logit, g, gl, eg, qd, ki, ke


def _gla_rows(T, forward):
    return _pick(T, (512, 256, 128, 64) if forward else (256, 128, 64))


def _gla_fwd(proj, proj_lr, w2p, gate_b):
    T = proj.shape[0]
    rb = _gla_rows(T, True)
    per = rb // CHUNK

    def body(q_ref, k_ref, v_ref, lr_ref, w2_ref, b_ref, o_ref, st_ref, s_scr):
        @pl.when(pl.program_id(1) == 0)
        def _():
            s_scr[...] = jnp.zeros_like(s_scr)

        low, _ = _chunk_masks(rb)
        _, _, gl, _, qd, ki, ke = _gla_block(q_ref[...], k_ref[...], lr_ref[...], w2_ref[...], b_ref[...],
                                             low.astype(BF16))
        v = v_ref[...].astype(BF16)
        qdb, keb = qd.astype(BF16), ke.astype(BF16)
        att = jnp.where(low, _dot(qdb, ki.astype(BF16), 1, 1), 0.0).astype(BF16)
        o_intra = _dot(att, v, 1, 0)
        st = s_scr[...]
        for i in range(per):
            rows = slice(i * CHUNK, (i + 1) * CHUNK)
            st_ref[0, i] = st
            o_ref[rows, :] = o_intra[rows] + _dot(qdb[rows], st.astype(BF16), 1, 1)
            st = st * jnp.exp(gl[i * CHUNK:i * CHUNK + 1]) + _dot(v[rows], keb[rows], 0, 0)
        s_scr[...] = st

    return pl.pallas_call(
        body, name="gla_fwd", grid=(GLA_H, T // rb),
        in_specs=[pl.BlockSpec((rb, DK), lambda h, n: (n, (O_GLA + W_GLA * h) // DK)),
                  pl.BlockSpec((rb, DK), lambda h, n: (n, (O_GLA + W_GLA * h) // DK + 1)),
                  pl.BlockSpec((rb, DV), lambda h, n: (n, (O_GLA + W_GLA * h) // DV + 1)),
                  pl.BlockSpec((rb, LRP), lambda h, n: (n, 0)),
                  pl.BlockSpec((LRP, DK), lambda h, n: (0, h)),
                  pl.BlockSpec((1, DK), lambda h, n: (0, h))],
        out_specs=[pl.BlockSpec((rb, DV), lambda h, n: (n, h)),
                   pl.BlockSpec((1, per, DV, DK), lambda h, n: (h, n, 0, 0))],
        out_shape=[jax.ShapeDtypeStruct((T, GLA_H * DV), F32),
                   jax.ShapeDtypeStruct((GLA_H, T // CHUNK, DV, DK), F32)],
        scratch_shapes=[pltpu.VMEM((DV, DK), F32)], compiler_params=_params(2),
    )(proj, proj, proj, proj_lr, w2p, gate_b)


def _gla_bwd(proj, proj_lr, w2p, gate_b, states, d_o, dproj):
    T = proj.shape[0]
    rb = _gla_rows(T, False)
    per = rb // CHUNK
    nblk = T // rb

    def body(q_ref, k_ref, v_ref, lr_ref, w2_ref, b_ref, st_ref, do_ref, _, dp_ref, dl_ref, ds_scr):
        @pl.when(pl.program_id(1) == 0)
        def _():
            ds_scr[...] = jnp.zeros_like(ds_scr)

        low, up = _chunk_masks(rb)
        logit, g, gl, eg, qd, ki, ke = _gla_block(q_ref[...], k_ref[...], lr_ref[...], w2_ref[...], b_ref[...],
                                                  low.astype(BF16))
        v = v_ref[...].astype(BF16)
        do = do_ref[...].astype(BF16)
        qdb, kib, keb = qd.astype(BF16), ki.astype(BF16), ke.astype(BF16)
        att = jnp.where(low, _dot(qdb, kib, 1, 1), 0.0).astype(BF16)
        datt = jnp.where(low, _dot(do, v, 1, 1), 0.0).astype(BF16)
        dv_intra = _dot(att, do, 0, 0)
        dqd_intra = _dot(datt, kib, 1, 0)
        dki = _dot(datt, qdb, 0, 0)
        ds = ds_scr[...]
        dqd, dke, dgl = [None] * per, [None] * per, [None] * per
        for i in reversed(range(per)):
            rows = slice(i * CHUNK, (i + 1) * CHUNK)
            sp = st_ref[0, i]
            dsb = ds.astype(BF16)
            dp_ref[rows, 2 * DK:] = (dv_intra[rows] + _dot(keb[rows], dsb, 1, 1)).astype(BF16)
            dqd[i] = dqd_intra[rows] + _dot(do[rows], sp.astype(BF16), 1, 0)
            dke[i] = _dot(v[rows], dsb, 1, 0)
            decay = jnp.exp(gl[i * CHUNK:i * CHUNK + 1])
            dgl[i] = (jnp.sum(dke[i] * ke[rows], axis=0, keepdims=True)
                      + jnp.sum(ds * sp, axis=0, keepdims=True) * decay)
            ds = ds * decay + _dot(do[rows], qdb[rows], 0, 0)
        ds_scr[...] = ds
        dqd = jnp.concatenate(dqd, axis=0)
        dke = jnp.concatenate(dke, axis=0)
        dgl = jnp.concatenate([jnp.broadcast_to(d, (CHUNK, DK)) for d in dgl], axis=0)
        dp_ref[:, 0:DK] = (dqd * (DK ** -0.5) * eg).astype(BF16)
        dp_ref[:, DK:2 * DK] = (dki * jnp.exp(-g) + dke * jnp.exp(gl - g)).astype(BF16)
        dg = dqd * qd - dki * ki - dke * ke
        dla = _tri_sum(up.astype(BF16), dg) + dgl
        dl_ref[...] = dla * (1.0 / GATE_NORM) * (1.0 - _sigmoid(logit))

    def rev(n):
        return nblk - 1 - n

    return pl.pallas_call(
        body, name="gla_bwd", grid=(GLA_H, nblk),
        in_specs=[pl.BlockSpec((rb, DK), lambda h, n: (rev(n), (O_GLA + W_GLA * h) // DK)),
                  pl.BlockSpec((rb, DK), lambda h, n: (rev(n), (O_GLA + W_GLA * h) // DK + 1)),
                  pl.BlockSpec((rb, DV), lambda h, n: (rev(n), (O_GLA + W_GLA * h) // DV + 1)),
                  pl.BlockSpec((rb, LRP), lambda h, n: (rev(n), 0)),
                  pl.BlockSpec((LRP, DK), lambda h, n: (0, h)),
                  pl.BlockSpec((1, DK), lambda h, n: (0, h)),
                  pl.BlockSpec((1, per, DV, DK), lambda h, n: (h, rev(n), 0, 0)),
                  pl.BlockSpec((rb, DV), lambda h, n: (rev(n), h)),
                  ANY],
        out_specs=[pl.BlockSpec((rb, W_GLA), lambda h, n: (rev(n), O_GLA // W_GLA + h)),
                   pl.BlockSpec((rb, DK), lambda h, n: (rev(n), h))],
        out_shape=[jax.ShapeDtypeStruct((T, MAIN), BF16), jax.ShapeDtypeStruct((T, GLA_H * DK), F32)],
        scratch_shapes=[pltpu.VMEM((DV, DK), F32)], input_output_aliases={8: 0}, compiler_params=_params(2),
    )(proj, proj, proj, proj_lr, w2p, gate_b, states, d_o, dproj)


def _after(values, tokens):
    return lax.optimization_barrier((values, tokens))[0]


def _local_step(x, target, p, reduce_part, reduce_ffn, reduce_in):
    u, u_t = _rms_fwd(x, p["norm1_w"], name="rms1_fwd")
    whole = _pick(x.shape[0], (2048, 256))
    proj = _matmul(u, p["w_in_main"], bm=whole, name="mm_proj")
    proj_lr = _matmul(u, p["w_in_lr"], name="mm_proj_lr")
    attn_o = _attn_fwd(proj, p["sinks"])
    gla_raw, states = _gla_fwd(proj, proj_lr, p["w2p"], p["gate_b"])
    merged, merged_t = _merge_fwd(attn_o, gla_raw, proj, p["gla_norm_w"])
    h1, v2, v2_t = _out_proj(merged, p["w_out"], x, p["norm2_w"])
    gate, up, ff, ff_t = _ffn_fwd(v2, p["w_gate"], p["w_up"])
    h2 = _matmul(ff, p["w_down"], add=h1, name="mm_down")
    loss, dh2, dh2_b, g_final = _final_loss(h2, p["final_norm_w"], target)

    dgate, dup = _ffn_bwd_hidden(dh2_b, p["w_down"], gate, up)
    sent = [reduce_ffn("w_down", ff_t, dh2_b), reduce_ffn("w_gate", v2_t, dgate), reduce_ffn("w_up", v2_t, dup)]
    dgate, dup = _after((dgate, dup), sent)
    dv2 = _ffn_bwd_input(dgate, dup, p["w_gate"], p["w_up"])
    dh1, dh1_b, g_norm2 = _rms_bwd(dv2, h1, p["norm2_w"], dh2, name="rms2_bwd")
    dmerged = _matmul(dh1_b, p["w_out"], tb=True, bj=D, name="mm_dmerged")
    sent = reduce_part("w_out", _matmul(merged_t, dh1_b, out_dtype=BF16, bj=D, name="mm_gout").reshape(4, D // 4, D))
    dmerged = _after(dmerged, sent)
    d_attn, d_gla, dproj, g_gla_norm = _merge_bwd(dmerged, attn_o, gla_raw, proj, p["gla_norm_w"])
    dproj, dlogit = _gla_bwd(proj, proj_lr, p["w2p"], p["gate_b"], states, d_gla, dproj)
    dproj, g_sinks = _attn_bwd(proj, p["sinks"], d_attn, dproj)
    g_gate_b = _colsum(dlogit, name="colsum_gate_b")
    g_w2 = _matmul(proj_lr, dlogit, ta=True, name="mm_gw2")
    dproj_lr = _matmul(dlogit, p["w2p"], tb=True, out_dtype=BF16, name="mm_dlr")
    reduce_in(u_t, dproj, dproj_lr)
    du = _matmul(dproj, p["w_in_main"], tb=True, bm=_pick(x.shape[0], (1024, 256)), bj=1024, bc=3200, name="mm_du")
    dx, g_norm1 = _rms_bwd_input(du, dproj_lr, p["w_in_lr"], x, p["norm1_w"], dh1)
    grads = dict(norm1_w=g_norm1, w2=g_w2[:RANK], gate_b=g_gate_b,
                 sinks=g_sinks[:, 0].reshape(1, N_Q), gla_norm_w=g_gla_norm, norm2_w=g_norm2, final_norm_w=g_final)
    return loss, dx, grads


def _place():
    x, y, c = lax.axis_index("x"), lax.axis_index("y"), lax.axis_index("c")
    chips = [(1 - x, y), (x, 1 - y), (1 - x, 1 - y)]
    return x, y, c, chips


def _hbm_shape(s, dt):
    return jax.ShapeDtypeStruct(s, dt)


ID_SIBLING = 9
ID_GATHER_IN = 10


def _handshake(peers):
    barrier = pltpu.get_barrier_semaphore()
    for peer in peers:
        pl.semaphore_signal(barrier, inc=1, device_id=peer, device_id_type=MESH)
    pl.semaphore_wait(barrier, len(peers))


def _launch_copies(body, args, out_shape, sems, *, name, collective_id=None):
    if collective_id is None:
        return pl.pallas_call(
            body, name=name, in_specs=[ANY] * len(args), out_specs=[ANY] * len(out_shape), out_shape=out_shape,
            scratch_shapes=sems)(*args)
    return pl.kernel(
        body, name=name, out_type=out_shape, mesh=plsc.ScalarSubcoreMesh(axis_name="sequencer", num_cores=1),
        scratch_types=sems, compiler_params=pltpu.CompilerParams(collective_id=collective_id))(*args)


def _gather_shards(shards, *, name, collective_id=None, after=()):
    n = len(shards)
    first_out = n + len(after)

    def body(*refs):
        ins, outs = refs[:n], refs[first_out:first_out + n]
        ici_send, ici_recv, d2d_send, d2d_recv, local_sem = refs[first_out + n:]
        x, y, c, chips = _place()
        me = 2 * x + y
        sibling = (x, y, 1 - c)
        if collective_id is not None:
            _handshake([sibling] + [(*chip, c) for chip in chips])

        def half(w, slot, hc):
            r2 = shards[w].shape[0] // 2
            return outs[w].at[slot, pl.ds(hc * r2, r2), :]

        locals_ = [pltpu.make_async_copy(ins[w], outs[w].at[me], local_sem.at[w]) for w in range(n)]
        for cp in locals_:
            cp.start()
        sends = []
        for w in range(n):
            r2 = shards[w].shape[0] // 2
            for j, chip in enumerate(chips):
                cp = pltpu.make_async_remote_copy(
                    src_ref=ins[w].at[pl.ds(c * r2, r2), :], dst_ref=half(w, me, c),
                    send_sem=ici_send.at[w * 3 + j], recv_sem=ici_recv.at[w * 3 + j],
                    device_id=(*chip, c), device_id_type=MESH)
                cp.start()
                sends.append(cp)
        for w in range(n):
            for j, chip in enumerate(chips):
                slot = 2 * chip[0] + chip[1]
                got = half(w, slot, c)
                pltpu.make_async_remote_copy(
                    src_ref=got, dst_ref=got, send_sem=ici_send.at[w * 3 + j], recv_sem=ici_recv.at[w * 3 + j],
                    device_id=(*chip, c), device_id_type=MESH).wait_recv()
                cp = pltpu.make_async_remote_copy(
                    src_ref=got, dst_ref=got, send_sem=d2d_send.at[w * 3 + j], recv_sem=d2d_recv.at[w * 3 + j],
                    device_id=sibling, device_id_type=MESH)
                cp.start()
                sends.append(cp)
        for w in range(n):
            for j, chip in enumerate(chips):
                slot = 2 * chip[0] + chip[1]
                got = half(w, slot, 1 - c)
                pltpu.make_async_remote_copy(
                    src_ref=got, dst_ref=got, send_sem=d2d_send.at[w * 3 + j], recv_sem=d2d_recv.at[w * 3 + j],
                    device_id=sibling, device_id_type=MESH).wait_recv()
        for cp in sends:
            cp.wait_send()
        for cp in locals_:
            cp.wait()

    return _launch_copies(
        body, list(shards) + list(after), [_hbm_shape((4,) + s.shape, s.dtype) for s in shards],
        [pltpu.SemaphoreType.DMA((3 * n,)), pltpu.SemaphoreType.DMA((3 * n,)), pltpu.SemaphoreType.DMA((3 * n,)),
         pltpu.SemaphoreType.DMA((3 * n,)), pltpu.SemaphoreType.DMA((n,))],
        name=name, collective_id=collective_id)


def _gather_staged(w, small):
    R2, C = w.shape
    R = 2 * R2
    rb = 256
    per = R2 // rb
    assert per % 2 == 0
    r2 = small.shape[0] // 2
    order = [(j, k) for k in range(per) for j in range(2)] + [(2, k) for k in range(per)]

    def body(w_ref, s_ref, out_ref, outs_ref, stage, big_send, big_recv, dia_send, dia_recv, small_send, small_recv,
             fwd_send, fwd_recv, in_sem, put_sem, own_send, own_recv, push_send, push_recv):
        x, y, c, chips = _place()
        me = 2 * x + y
        sibling = (x, y, 1 - c)
        slots = [2 * chip[0] + chip[1] for chip in chips]
        _handshake([sibling] + [(*chip, c) for chip in chips])

        def block(j, k, hc):
            return out_ref.at[slots[j], pl.ds(hc * R2 + k * rb, rb), :]

        def big(j, k):
            return pltpu.make_async_remote_copy(
                src_ref=w_ref.at[pl.ds(k * rb, rb), :], dst_ref=out_ref.at[me, pl.ds(c * R2 + k * rb, rb), :],
                send_sem=big_send.at[j * per + k], recv_sem=big_recv.at[j * per + k],
                device_id=(*chips[j], c), device_id_type=MESH)

        def passes_on(j, k):
            return (j == 0) == (k < per // 2)

        def onward(j, k, to):
            return pltpu.make_async_remote_copy(
                src_ref=block(j, k, c), dst_ref=block(j, k, c), send_sem=dia_send.at[k], recv_sem=dia_recv.at[k],
                device_id=(*chips[to], c), device_id_type=MESH)

        def little(j):
            return pltpu.make_async_remote_copy(
                src_ref=s_ref.at[pl.ds(c * r2, r2), :], dst_ref=outs_ref.at[me, pl.ds(c * r2, r2), :],
                send_sem=small_send.at[j], recv_sem=small_recv.at[j], device_id=(*chips[j], c), device_id_type=MESH)

        sends = [big(j, k) for k in range(per) for j in range(2)] + [little(j) for j in range(3)]
        for cp in sends:
            cp.start()
        mine = pltpu.make_async_copy(s_ref, outs_ref.at[me], put_sem.at[2])
        mine.start()

        def own(k, to_vmem):
            if to_vmem:
                return pltpu.make_async_copy(w_ref.at[pl.ds(k * rb, rb), :], stage.at[k % 2], in_sem.at[k % 2])
            return pltpu.make_async_copy(stage.at[k % 2], out_ref.at[me, pl.ds(c * R2 + k * rb, rb), :], put_sem.at[k % 2])

        def own_push(k, hc):
            return pltpu.make_async_remote_copy(
                src_ref=stage.at[k % 2], dst_ref=out_ref.at[me, pl.ds(hc * R2 + k * rb, rb), :],
                send_sem=own_send.at[k % 2], recv_sem=own_recv.at[k], device_id=sibling, device_id_type=MESH)

        for k in range(per):
            if k >= 2:
                own(k - 2, False).wait()
                own_push(k - 2, c).wait_send()
            own(k, True).start()
            own(k, True).wait()
            own(k, False).start()
            own_push(k, c).start()
        for k in range(per - 2, per):
            own(k, False).wait()
            own_push(k, c).wait_send()

        def push(t, hc):
            j, k = order[t]
            return pltpu.make_async_remote_copy(
                src_ref=stage.at[t % 2], dst_ref=block(j, k, hc), send_sem=push_send.at[t % 2],
                recv_sem=push_recv.at[t], device_id=sibling, device_id_type=MESH)

        passed = []
        for t, (j, k) in enumerate(order):
            if j == 2:
                onward(2, k, 0).wait_recv()
            else:
                big(j, k).wait_recv()
                if passes_on(j, k):
                    passed.append(onward(j, k, 1 - j))
                    passed[-1].start()
            if t >= 2:
                push(t - 2, c).wait_send()
            fetch = pltpu.make_async_copy(block(j, k, c), stage.at[t % 2], in_sem.at[t % 2])
            fetch.start()
            fetch.wait()
            push(t, c).start()
        for j in range(3):
            got = outs_ref.at[slots[j], pl.ds(c * r2, r2), :]
            little(j).wait_recv()
            fw = pltpu.make_async_remote_copy(
                src_ref=got, dst_ref=got, send_sem=fwd_send.at[j], recv_sem=fwd_recv.at[j],
                device_id=sibling, device_id_type=MESH)
            fw.start()
            passed.append(fw)
        for t in range(len(order) - 2, len(order)):
            push(t, c).wait_send()
        for t in range(len(order)):
            push(t, 1 - c).wait_recv()
        for k in range(per):
            own_push(k, 1 - c).wait_recv()
        for j in range(3):
            theirs = outs_ref.at[slots[j], pl.ds((1 - c) * r2, r2), :]
            pltpu.make_async_remote_copy(
                src_ref=theirs, dst_ref=theirs, send_sem=fwd_send.at[j], recv_sem=fwd_recv.at[j],
                device_id=sibling, device_id_type=MESH).wait_recv()
        for cp in sends + passed:
            cp.wait_send()
        mine.wait()

    dma = pltpu.SemaphoreType.DMA
    return pl.pallas_call(
        body, name="gather_in", in_specs=[ANY, ANY], out_specs=[ANY, ANY],
        out_shape=[_hbm_shape((4, R, C), w.dtype), _hbm_shape((4,) + small.shape, small.dtype)],
        scratch_shapes=[pltpu.VMEM((2, rb, C), w.dtype), dma((2 * per,)), dma((2 * per,)), dma((per,)), dma((per,)),
                        dma((3,)), dma((3,)), dma((3,)), dma((3,)), dma((2,)), dma((3,)), dma((2,)), dma((per,)),
                        dma((2,)), dma((len(order),))],
        compiler_params=pltpu.CompilerParams(vmem_limit_bytes=VMEM_LIMIT, collective_id=ID_GATHER_IN),
    )(w, small)


def _pair_blocks(R2):
    return _pick(R2, (256, 352, 128, 64, 32, 16))


def _grad_matmul_split(a, b, *, cut, name):
    M, T = a.shape
    J = b.shape[1]
    bw = None
    if cut == "rows":
        S, R, C, bm = 4, M // 4, J, 352
        per = R // 2 // bm
        n = M // bm

        def where(t):
            return t // (2 * per), (t // per) % 2, t % per, 0

        a_spec = pl.BlockSpec((bm, T), lambda t: (t, 0))
        b_spec = pl.BlockSpec((T, C), lambda t: (0, 0))
    elif cut == "panels":
        S, R, C, bm = 4, M, J // 4, M // 2
        n = 8

        def where(t):
            return t // 2, t % 2, 0, 0

        a_spec = pl.BlockSpec((bm, T), lambda t: (t % 2, 0))
        b_spec = pl.BlockSpec((T, C), lambda t: (0, t // 2))
    else:
        S, R, C, bm, bw = 1, M, J, M // 2, 1280
        n = 2 * (J // bw)

        def where(t):
            return 0, t % 2, 0, t // 2

        a_spec = pl.BlockSpec((bm, T), lambda t: (t % 2, 0))
        b_spec = pl.BlockSpec((T, bw), lambda t: (0, t // 2))
    R2 = R // 2
    tw = bw or C

    def body(a_ref, b_ref, mine_ref, got_ref, tile, loc_sem, snd_sem, rcv_sem):
        x, y, c, _ = _place()
        sibling = (x, y, 1 - c)
        t = pl.program_id(0)

        @pl.when(t == 0)
        def _():
            _handshake([sibling])

        def place(ref, u):
            s, _, i, j = where(u)
            return ref.at[s, pl.ds(i * bm, bm), pl.ds(j * tw, tw)]

        def keep(u):
            return pltpu.make_async_copy(tile.at[u % 2], place(mine_ref, u), loc_sem.at[u % 2])

        def give(u):
            return pltpu.make_async_remote_copy(
                src_ref=tile.at[u % 2], dst_ref=place(got_ref, u), send_sem=snd_sem.at[u % 2],
                recv_sem=rcv_sem, device_id=sibling, device_id_type=MESH)

        def settle(u):
            @pl.when(where(u)[1] == c)
            def _():
                keep(u).wait()

            @pl.when(where(u)[1] != c)
            def _():
                give(u).wait_send()

        @pl.when(t >= 2)
        def _():
            settle(t - 2)

        tile[t % 2] = _dot(a_ref[...], b_ref[...], 1, 0).astype(BF16)

        @pl.when(where(t)[1] == c)
        def _():
            keep(t).start()

        @pl.when(where(t)[1] != c)
        def _():
            give(t).start()

        @pl.when(t == n - 1)
        def _():
            settle(t - 1)
            settle(t)
            pltpu.make_async_remote_copy(
                src_ref=got_ref, dst_ref=got_ref, send_sem=snd_sem.at[0], recv_sem=rcv_sem,
                device_id=sibling, device_id_type=MESH).wait_recv()

    half = _hbm_shape((S, R2, C), BF16)
    return pl.pallas_call(
        body, name=name, grid=(n,), in_specs=[a_spec, b_spec], out_specs=[ANY, ANY], out_shape=[half, half],
        scratch_shapes=[pltpu.VMEM((2, bm, tw), BF16), pltpu.SemaphoreType.DMA((2,)), pltpu.SemaphoreType.DMA((2,)),
                        pltpu.SemaphoreType.DMA],
        compiler_params=pltpu.CompilerParams(
            dimension_semantics=("arbitrary",), vmem_limit_bytes=VMEM_LIMIT, collective_id=ID_SIBLING),
    )(a, b)


def _pair_add(mine, got, *, name):
    S, R2, C = mine.shape
    rb = _pick(R2, (512, 352, 256))
    cb = _pick(C, (2560,)) if C > 4096 else C

    def body(a_ref, b_ref, o_ref):
        o_ref[...] = (a_ref[...].astype(F32) + b_ref[...].astype(F32)).astype(BF16)

    blk = pl.BlockSpec((1, rb, cb), lambda s, i, j: (s, i, j))
    return pl.pallas_call(
        body, name=name, grid=(S, R2 // rb, C // cb), in_specs=[blk, blk], out_specs=blk,
        out_shape=jax.ShapeDtypeStruct(mine.shape, BF16), compiler_params=_params(3),
    )(mine, got)


def _pair_reduce(part, *, name):
    S, R, C = part.shape
    R2 = R // 2
    rb = _pick(R2, (512, 352, 256, 128, 64, 32, 16))
    nblk = R2 // rb
    steps = [(s, i) for s in range(S) for i in range(nblk)]
    n = len(steps)

    def body(part_ref, sums_ref, own_buf, oth_buf, rcv_buf, out_buf, own_sem, oth_sem, out_sem, send_sem, recv_sem):
        x, y, c, _ = _place()
        sibling = (x, y, 1 - c)
        _handshake([sibling])

        def fetch(t, half, buf, sem):
            s, i = steps[t]
            return pltpu.make_async_copy(part_ref.at[s, pl.ds(half * R2 + i * rb, rb), :], buf.at[t % 3], sem.at[t % 3])

        def push(t):
            return pltpu.make_async_remote_copy(
                src_ref=oth_buf.at[t % 3], dst_ref=rcv_buf.at[t % 4], send_sem=send_sem.at[t % 4],
                recv_sem=recv_sem.at[t % 4], device_id=sibling, device_id_type=MESH)

        def store(t):
            s, i = steps[t]
            return pltpu.make_async_copy(out_buf.at[t % 2], sums_ref.at[s, pl.ds(i * rb, rb), :], out_sem.at[t % 2])

        for t in range(min(2, n)):
            fetch(t, c, own_buf, own_sem).start()
            fetch(t, 1 - c, oth_buf, oth_sem).start()
        fetch(0, 1 - c, oth_buf, oth_sem).wait()
        push(0).start()
        for t in range(n):
            if t + 2 < n:
                fetch(t + 2, c, own_buf, own_sem).start()
                fetch(t + 2, 1 - c, oth_buf, oth_sem).start()
            if t + 1 < n:
                fetch(t + 1, 1 - c, oth_buf, oth_sem).wait()
                push(t + 1).start()
            fetch(t, c, own_buf, own_sem).wait()
            push(t).wait()
            if t >= 2:
                store(t - 2).wait()
            out_buf[t % 2] = (own_buf[t % 3].astype(F32) + rcv_buf[t % 4].astype(F32)).astype(BF16)
            store(t).start()
        for t in range(max(n - 2, 0), n):
            store(t).wait()

    def buf(k):
        return pltpu.VMEM((k, rb, C), BF16)

    def sems(k):
        return pltpu.SemaphoreType.DMA((k,))

    return pl.pallas_call(
        body, name=name, in_specs=[ANY], out_specs=ANY, out_shape=_hbm_shape((S, R2, C), BF16),
        scratch_shapes=[buf(3), buf(3), buf(4), buf(2), sems(3), sems(3), sems(2), sems(4), sems(4)],
        compiler_params=pltpu.CompilerParams(vmem_limit_bytes=VMEM_LIMIT, collective_id=ID_SIBLING),
    )(part)


def _chip_exchange(sums, *, name, collective_id=None):
    n = len(sums)

    def body(*refs):
        ins, outs = refs[:n], refs[n:2 * n]
        send_sem, recv_sem, local_sem = refs[2 * n:]
        x, y, c, chips = _place()
        me = 2 * x + y
        if collective_id is not None:
            _handshake([(*chip, c) for chip in chips])
        cps = []
        for w in range(n):
            lc = pltpu.make_async_copy(ins[w].at[me], outs[w].at[me], local_sem.at[w])
            lc.start()
            cps.append(lc)
            for j, chip in enumerate(chips):
                slot = 2 * chip[0] + chip[1]
                rc = pltpu.make_async_remote_copy(
                    src_ref=ins[w].at[slot], dst_ref=outs[w].at[me],
                    send_sem=send_sem.at[w * 3 + j], recv_sem=recv_sem.at[w * 3 + j],
                    device_id=(*chip, c), device_id_type=MESH)
                rc.start()
                cps.append(rc)
        for w in range(n):
            for j, chip in enumerate(chips):
                slot = 2 * chip[0] + chip[1]
                pltpu.make_async_remote_copy(
                    src_ref=ins[w].at[slot], dst_ref=outs[w].at[slot],
                    send_sem=send_sem.at[w * 3 + j], recv_sem=recv_sem.at[w * 3 + j],
                    device_id=(*chip, c), device_id_type=MESH).wait_recv()
        for w in range(n):
            cps[w * 4].wait()
            for j in range(3):
                cps[w * 4 + 1 + j].wait_send()

    return _launch_copies(
        body, sums, [_hbm_shape(s.shape, s.dtype) for s in sums],
        [pltpu.SemaphoreType.DMA((3 * n,)), pltpu.SemaphoreType.DMA((3 * n,)), pltpu.SemaphoreType.DMA((n,))],
        name=name, collective_id=collective_id)


def _sum_join(recv, *, name):
    _, R2, C = recv.shape
    rb = _pair_blocks(R2)
    nblk = R2 // rb

    def body(recv_ref, out_ref, in_buf, acc_buf, in_sem, loc_sem, send_sem, recv_sem):
        x, y, c, _ = _place()
        sibling = (x, y, 1 - c)
        _handshake([sibling])

        def fetch(t):
            return pltpu.make_async_copy(recv_ref.at[:, pl.ds(t * rb, rb), :], in_buf.at[t % 2], in_sem.at[t % 2])

        def rows(t, half):
            return out_ref.at[pl.ds(half * R2 + t * rb, rb), :]

        def put_local(t):
            return pltpu.make_async_copy(acc_buf.at[t % 2], rows(t, c), loc_sem.at[t])

        def put_remote(t, half):
            return pltpu.make_async_remote_copy(
                src_ref=acc_buf.at[t % 2], dst_ref=rows(t, half), send_sem=send_sem.at[t], recv_sem=recv_sem.at[t],
                device_id=sibling, device_id_type=MESH)

        fetch(0).start()
        for t in range(nblk):
            if t + 1 < nblk:
                fetch(t + 1).start()
            fetch(t).wait()
            if t >= 2:
                put_local(t - 2).wait()
                put_remote(t - 2, c).wait_send()
            acc = in_buf[t % 2, 0].astype(F32)
            for s in range(1, 4):
                acc = acc + in_buf[t % 2, s].astype(F32)
            acc_buf[t % 2] = acc
            put_local(t).start()
            put_remote(t, c).start()
        for t in range(max(nblk - 2, 0), nblk):
            put_local(t).wait()
            put_remote(t, c).wait_send()
        for t in range(nblk):
            put_remote(t, 1 - c).wait_recv()

    semn = pltpu.SemaphoreType.DMA((nblk,))
    return pl.pallas_call(
        body, name=name, in_specs=[ANY], out_specs=ANY, out_shape=_hbm_shape((2 * R2, C), F32),
        scratch_shapes=[pltpu.VMEM((2, 4, rb, C), BF16), pltpu.VMEM((2, rb, C), F32),
                        pltpu.SemaphoreType.DMA((2,)), semn, semn, semn],
        compiler_params=pltpu.CompilerParams(vmem_limit_bytes=VMEM_LIMIT, collective_id=ID_SIBLING),
    )(recv)


def _sum_small(pack):
    R, C = pack.shape

    def body(in_ref, out_ref, all_ref, send_sem, recv_sem):
        x, y, c, _ = _place()
        me = 4 * x + 2 * y + c
        all_ref[me] = in_ref[...]
        cps = []
        for k in range(1, 8):
            peer = (x ^ (k >> 2), y ^ ((k >> 1) & 1), c ^ (k & 1))
            cp = pltpu.make_async_remote_copy(
                src_ref=in_ref, dst_ref=all_ref.at[me], send_sem=send_sem.at[k - 1], recv_sem=recv_sem.at[k - 1],
                device_id=peer, device_id_type=MESH)
            cp.start()
            cps.append(cp)
        for k in range(1, 8):
            peer = (x ^ (k >> 2), y ^ ((k >> 1) & 1), c ^ (k & 1))
            slot = 4 * peer[0] + 2 * peer[1] + peer[2]
            pltpu.make_async_remote_copy(
                src_ref=in_ref, dst_ref=all_ref.at[slot], send_sem=send_sem.at[k - 1], recv_sem=recv_sem.at[k - 1],
                device_id=peer, device_id_type=MESH).wait_recv()
        for cp in cps:
            cp.wait_send()
        acc = all_ref[0]
        for d in range(1, 8):
            acc = acc + all_ref[d]
        out_ref[...] = acc

    return pl.pallas_call(
        body, name="sum_small", in_specs=[pl.BlockSpec(memory_space=pltpu.VMEM)],
        out_specs=pl.BlockSpec(memory_space=pltpu.VMEM), out_shape=jax.ShapeDtypeStruct((R, C), F32),
        scratch_shapes=[pltpu.VMEM((8, R, C), F32), pltpu.SemaphoreType.DMA((7,)), pltpu.SemaphoreType.DMA((7,))],
    )(pack)


ADAMW_BLOCK_BYTES = 2 * 1024 * 1024


def _adamw_block(R, C):
    padded = -(-C // 128) * 128
    rows = [rb for rb in range(8, R + 1, 8) if R % rb == 0 and rb * padded * 4 <= ADAMW_BLOCK_BYTES]
    if rows or R * padded * 4 <= ADAMW_BLOCK_BYTES:
        return (max(rows) if rows else R), C
    cols = [cb for cb in range(128, C + 1, 128) if C % cb == 0 and R * cb * 4 <= ADAMW_BLOCK_BYTES]
    return R, max(cols)


def _adamw(w, g, m, v, *, name, pass_grad=False):
    R, C = w.shape
    rb, cb = _adamw_block(R, C)
    c1 = 1.0 / (1.0 - B1 ** STEP)
    c2 = 1.0 / (1.0 - B2 ** STEP)

    def body(w_ref, g_ref, m_ref, v_ref, *outs):
        d_ref, nm_ref, nv_ref = outs[-3:]
        gv = g_ref[...]
        if pass_grad:
            outs[0][...] = gv
        m2 = B1 * m_ref[...] + (1.0 - B1) * gv
        v2 = B2 * v_ref[...] + (1.0 - B2) * gv * gv
        nm_ref[...] = m2
        nv_ref[...] = v2
        d_ref[...] = -LR * ((m2 * c1) / (jnp.sqrt(v2 * c2) + ADAM_EPS) + WD * w_ref[...])

    n_out = 4 if pass_grad else 3
    blk = pl.BlockSpec((rb, cb), lambda i, j: (i, j))
    return pl.pallas_call(
        body, name=name, grid=(R // rb, C // cb), in_specs=[blk] * 4, out_specs=[blk] * n_out,
        out_shape=[jax.ShapeDtypeStruct((R, C), F32)] * n_out, compiler_params=_params(2),
    )(w, g, m, v)


SMALL = (("norm1_w", D), ("norm2_w", D), ("final_norm_w", D), ("gate_b", GLA_H * DK), ("gla_norm_w", DV), ("sinks", N_Q))
PACK_W = 1024


def _pack_small(vals, w2, loss):
    rows = []
    for name, width in SMALL:
        v = vals[name].reshape(-1)
        rows.append(jnp.pad(v, (0, (-width) % PACK_W)).reshape(-1, PACK_W))
    rows.append(w2)
    rows.append(jnp.broadcast_to(loss.reshape(1, 1), (1, PACK_W)))
    pack = jnp.concatenate(rows, axis=0)
    return jnp.pad(pack, ((0, 32 - pack.shape[0]), (0, 0)))


def _unpack_small(pack):
    out, r = {}, 0
    for name, width in SMALL:
        nr = -(-width // PACK_W)
        out[name] = pack[r:r + nr].reshape(-1)[:width]
        r += nr
    out["w2"] = pack[r:r + RANK]
    out["loss"] = pack[r + RANK, 0]
    return out


def kernel(x, norm1_w, w_in, gla_gate_w2, gla_gate_b, attn_sinks, gla_norm_w, w_out, norm2_w, w_ffn_gate, w_ffn_up, w_ffn_down, final_norm_w, loss_target, m_norm1_w, m_w_in, m_gla_gate_w2, m_gla_gate_b, m_attn_sinks, m_gla_norm_w, m_w_out, m_norm2_w, m_w_ffn_gate, m_w_ffn_up, m_w_ffn_down, m_final_norm_w, v_norm1_w, v_w_in, v_gla_gate_w2, v_gla_gate_b, v_attn_sinks, v_gla_norm_w, v_w_out, v_norm2_w, v_w_ffn_gate, v_w_ffn_up, v_w_ffn_down, v_final_norm_w):
    chip = 2 * lax.axis_index("x") + lax.axis_index("y")
    w_in_s, w_out_s, w_gate_s, w_up_s, w_down_s, w2_s = (
        w_in[0], w_out[0], w_ffn_gate[0], w_ffn_up[0], w_ffn_down[0], gla_gate_w2[0])
    CS = SHARD_COLS

    half = lax.dynamic_slice_in_dim(w_in_s, lax.axis_index("c") * (D // 2), D // 2, axis=0)
    g_in, g_w2 = _gather_staged(half.astype(BF16), w2_s)
    g_out, = _gather_shards([w_out_s.astype(BF16)], name="gather_out", collective_id=1, after=[g_w2])
    g_gate, g_up = _gather_shards([w_gate_s.astype(BF16), w_up_s.astype(BF16)], name="gather_gate_up",
                                  collective_id=7, after=[g_w2])
    g_down, = _gather_shards([w_down_s.astype(BF16)], name="gather_down", collective_id=8, after=[g_w2])
    w_in_t = lax.optimization_barrier(jnp.transpose(w_in_s))
    m_in_t = lax.optimization_barrier(jnp.transpose(m_w_in[0]))
    v_in_t = lax.optimization_barrier(jnp.transpose(v_w_in[0]))
    w_main, w_lr = _w_in_to_main(_after(g_in, (w_in_t, m_in_t, v_in_t)))
    w2_full = jnp.transpose(g_w2, (1, 0, 2)).reshape(RANK, GLA_H * DK)
    p = dict(
        norm1_w=norm1_w, norm2_w=norm2_w, final_norm_w=final_norm_w.reshape(1, D), gate_b=gla_gate_b,
        gla_norm_w=gla_norm_w, sinks=jnp.pad(attn_sinks, ((0, 0), (0, 128 - N_Q))),
        w_in_main=w_main, w_in_lr=w_lr,
        w2p=jnp.pad(w2_full, ((0, LRP - RANK), (0, 0))).astype(BF16),
        w_out=g_out.reshape(D, D),
        w_gate=g_gate, w_up=g_up, w_down=g_down.reshape(FFN, D),
    )

    tags = ["w_in", "w_out", "w_gate", "w_up", "w_down"]
    recv = {}

    def exchange(tag, sums):
        recv[tag], = _chip_exchange([sums], name="chip_exchange_" + tag, collective_id=2 + tags.index(tag))
        return sums

    def reduce_part(tag, part):
        return exchange(tag, _pair_reduce(part, name="pair_reduce_" + tag))

    def reduce_ffn(tag, a, b):
        mine, got = _grad_matmul_split(a, b, cut="rows" if tag == "w_down" else "panels", name="mm_grad_" + tag)
        return exchange(tag, _pair_add(mine, got, name="pair_add_" + tag))

    def reduce_in(u_t, dproj, dproj_lr):
        mine, got = _grad_matmul_split(u_t, dproj, cut="whole", name="mm_grad_w_in")
        sum_main, = _pair_add(mine, got, name="pair_add_w_in")
        g_lr = _matmul(u_t, dproj_lr, out_dtype=BF16, name="mm_gin_lr")
        sum_lr, = _pair_reduce(g_lr[None], name="pair_reduce_w_in_lr")
        exchange("w_in", _main_to_shards(sum_main, sum_lr))

    done = {}
    shard = {"w_out": (w_out_s, m_w_out[0], v_w_out[0]), "w_gate": (w_gate_s, m_w_ffn_gate[0], v_w_ffn_gate[0]),
             "w_up": (w_up_s, m_w_ffn_up[0], v_w_ffn_up[0]), "w_down": (w_down_s, m_w_ffn_down[0], v_w_ffn_down[0])}

    def finish(tag):
        grad = _sum_join(recv[tag], name="sum_join_" + tag)
        done[tag] = _adamw(shard[tag][0], grad, shard[tag][1], shard[tag][2], name="adamw_" + tag, pass_grad=True)

    loss_blk, dx, g = _local_step(x[0], loss_target[0], p, reduce_part, reduce_ffn, reduce_in)
    for tag in ("w_down", "w_gate", "w_up", "w_out"):
        finish(tag)
    behind = [dx] + [done[tag][1] for tag in ("w_down", "w_gate", "w_up", "w_out")]
    g_in = _sum_join(_after(recv["w_in"], behind), name="sum_join_w_in")

    small = _unpack_small(_sum_small(_pack_small(g, g["w2"], loss_blk[0, 0])))
    loss = small["loss"]
    g_w2_mine = lax.dynamic_slice_in_dim(small["w2"], chip * (GLA_H * DK // 4), GLA_H * DK // 4, axis=1)

    grads = dict(
        norm1_w=small["norm1_w"].reshape(1, D), w_in=g_in, gla_gate_w2=g_w2_mine,
        gla_gate_b=small["gate_b"].reshape(1, -1), attn_sinks=small["sinks"].reshape(1, N_Q),
        gla_norm_w=small["gla_norm_w"].reshape(1, DV), norm2_w=small["norm2_w"].reshape(1, D),
        final_norm_w=small["final_norm_w"].reshape(1, D))
    def lin(a):
        return jnp.transpose(a.reshape(D, CS))

    def unlin(a):
        return jnp.transpose(a)

    grads["w_in"] = lin(g_in)
    weights = dict(
        norm1_w=(norm1_w, m_norm1_w, v_norm1_w), w_in=(w_in_t, m_in_t, v_in_t),
        gla_gate_w2=(w2_s, m_gla_gate_w2[0], v_gla_gate_w2[0]), gla_gate_b=(gla_gate_b, m_gla_gate_b, v_gla_gate_b),
        attn_sinks=(attn_sinks, m_attn_sinks, v_attn_sinks), gla_norm_w=(gla_norm_w, m_gla_norm_w, v_gla_norm_w),
        norm2_w=(norm2_w, m_norm2_w, v_norm2_w),
        final_norm_w=(final_norm_w.reshape(1, D), m_final_norm_w.reshape(1, D), v_final_norm_w.reshape(1, D)))
    names = ["norm1_w", "w_in", "gla_gate_w2", "gla_gate_b", "attn_sinks", "gla_norm_w", "w_out", "norm2_w",
             "w_ffn_gate", "w_ffn_up", "w_ffn_down", "final_norm_w"]
    lead = {"norm1_w": False, "gla_gate_b": False, "attn_sinks": False, "gla_norm_w": False, "norm2_w": False}
    g_out_l, d_out, m_out, v_out = [], [], [], []
    early = {"w_out": "w_out", "w_ffn_gate": "w_gate", "w_ffn_up": "w_up", "w_ffn_down": "w_down"}
    for nm in names:
        if nm in early:
            gr, dl, nmn, nvn = done[early[nm]]
            w = shard[early[nm]][0]
        else:
            w, m, v = weights[nm]
            gr = grads[nm]
            dl, nmn, nvn = _adamw(w, gr, m, v, name="adamw_" + nm)
        if nm == "w_in":
            gr, dl, nmn, nvn, w = g_in, unlin(dl), unlin(nmn), unlin(nvn), w_in_s
        if nm == "final_norm_w":
            shape = (D,)
        elif nm in lead:
            shape = w.shape
        else:
            shape = (1,) + w.shape
        g_out_l.append(gr.reshape(shape))
        d_out.append(dl.reshape(shape))
        m_out.append(nmn.reshape(shape))
        v_out.append(nvn.reshape(shape))
    return (loss, dx[None], *g_out_l, *d_out, *m_out, *v_out)
```

```python
import functools

import jax
import jax.numpy as jnp
from jax import lax
from jax.experimental import pallas as pl
from jax.experimental.pallas import tpu as pltpu
from jax.experimental.pallas import tpu_sc as plsc

F32 = jnp.float32
BF16 = jnp.bfloat16

D = 2048
HEAD_DIM = 64
N_Q = 32
N_KV = 4
GROUP = 8
WINDOW = 128
GLA_H = 4
DK = 256
DV = 512
RANK = 16
CHUNK = 64
FFN = 5632
EPS = 1e-6
MASK_VALUE = -1e30
GATE_NORM = 16.0
LR, B1, B2, ADAM_EPS, WD, STEP = 0.001, 0.9, 0.999, 1e-08, 0.01, 10

MAIN = 12800
O_MERGE, O_GLA, O_ATTN = 0, 6144, 10240
W_MERGE, W_GLA, W_ATTN = 3 * DV, 2 * DK + DV, D + 2 * N_KV * HEAD_DIM
LR_AT = 6656
LRP = 128


def _main_pieces():
    o_gq, o_gk, o_gv, o_gr, o_ga, o_gb = 2560, 3584, 4608, LR_AT + RANK, LR_AT + RANK + D, LR_AT + RANK + 2 * D
    pieces = []
    for h in range(GLA_H):
        pieces += [(o_gr + DV * h, DV), (o_ga + DV * h, DV), (o_gb + DV * h, DV)]
    for h in range(GLA_H):
        pieces += [(o_gq + DK * h, DK), (o_gk + DK * h, DK), (o_gv + DV * h, DV)]
    pieces.append((0, W_ATTN))
    return pieces


SHARD_COLS = 3204

VMEM_LIMIT = 56 * 1024 * 1024
MESH = pl.DeviceIdType.MESH
ANY = pl.BlockSpec(memory_space=pl.ANY)


def _params(ngrid):
    return pltpu.CompilerParams(dimension_semantics=("arbitrary",) * ngrid, vmem_limit_bytes=VMEM_LIMIT)


def _shard_segments(lo, hi):
    segs = []
    while lo < hi:
        s = lo // SHARD_COLS
        e = min(hi, (s + 1) * SHARD_COLS)
        segs.append((s, lo - s * SHARD_COLS, e - s * SHARD_COLS))
        lo = e
    return segs


def _lanes(pieces):
    return pieces[0] if len(pieces) == 1 else jnp.concatenate(pieces, axis=1)


def _w_in_to_main(shards):
    rb = 256

    def body(g_ref, main_ref, lr_ref):
        at = 0
        for a, w in _main_pieces():
            main_ref[:, at:at + w] = _lanes([g_ref[s, :, lo:hi] for s, lo, hi in _shard_segments(a, a + w)])
            at += w
        lr = [g_ref[s, :, lo:hi] for s, lo, hi in _shard_segments(LR_AT, LR_AT + RANK)]
        lr_ref[...] = _lanes(lr + [jnp.zeros((rb, LRP - RANK), lr_ref.dtype)])

    return pl.pallas_call(
        body, name="w_in_to_main", grid=(D // rb,),
        in_specs=[pl.BlockSpec((4, rb, SHARD_COLS), lambda i: (0, i, 0))],
        out_specs=[pl.BlockSpec((rb, MAIN), lambda i: (i, 0)), pl.BlockSpec((rb, LRP), lambda i: (i, 0))],
        out_shape=[jax.ShapeDtypeStruct((D, MAIN), shards.dtype), jax.ShapeDtypeStruct((D, LRP), shards.dtype)],
        compiler_params=_params(1),
    )(shards)


def _main_to_shards(g_main, g_lr):
    rb = 256
    rows = g_main.shape[0]
    at, sources = 0, [(LR_AT, RANK, 1, 0)]
    for a, w in _main_pieces():
        sources.append((a, w, 0, at))
        at += w
    sources.sort()

    def body(main_ref, lr_ref, out_ref):
        refs = (main_ref, lr_ref)
        for s in range(4):
            lo, hi = s * SHARD_COLS, (s + 1) * SHARD_COLS
            pieces = []
            for a, w, which, src in sources:
                b, e = max(a, lo), min(a + w, hi)
                if b < e:
                    pieces.append(refs[which][:, src + b - a:src + e - a])
            out_ref[s] = _lanes(pieces)

    return pl.pallas_call(
        body, name="main_to_shards", grid=(rows // rb,),
        in_specs=[pl.BlockSpec((rb, MAIN), lambda i: (i, 0)), pl.BlockSpec((rb, LRP), lambda i: (i, 0))],
        out_specs=pl.BlockSpec((4, rb, SHARD_COLS), lambda i: (0, i, 0)),
        out_shape=jax.ShapeDtypeStruct((4, rows, SHARD_COLS), g_main.dtype), compiler_params=_params(1),
    )(g_main, g_lr)


def _pick(n, cands):
    for c in cands:
        if n % c == 0:
            return c
    return n


def _sigmoid(x):
    return 1.0 / (1.0 + jnp.exp(-x))


def _dot(a, b, ca, cb):
    return lax.dot_general(a, b, (((ca,), (cb,)), ((), ())), preferred_element_type=F32)


def _matmul(a, b, *, name, ta=False, tb=False, add=None, out_dtype=F32, bm=None, bj=None, bc=None,
            b_slots=False, out_slots=False):
    C, M = a.shape if ta else a.shape[::-1]
    if b_slots:
        if tb:
            J, bc = b.shape[1], b.shape[2]
            assert b.shape[0] * bc == C
        else:
            bj = b.shape[2]
            J = b.shape[0] * bj
            assert b.shape[1] == C
    else:
        J = b.shape[0] if tb else b.shape[1]
        assert (b.shape[1] if tb else b.shape[0]) == C
    bm = bm or _pick(M, (1024, 512, 256, 128) if C <= FFN else (512, 256, 128))
    bj = bj or _pick(J, (1280, 1024, 512, 256, 128) if C <= D else (512, 256, 128))
    bc = bc or (C if C <= FFN else _pick(C, (6400,)))
    nm, nj, nc = M // bm, J // bj, C // bc
    has_add = add is not None

    def body(*refs):
        a_ref, b_ref = refs[0], refs[1]
        add_ref = refs[2] if has_add else None
        o_ref = refs[3] if has_add else refs[2]
        p = _dot(a_ref[...].astype(BF16), b_ref[...].astype(BF16), 0 if ta else 1, 1 if tb else 0)

        def finish(acc):
            if has_add:
                acc = acc + add_ref[...]
            o_ref[...] = acc.astype(o_ref.dtype)

        if nc == 1:
            finish(p)
        else:
            acc_ref = refs[-1]
            c = pl.program_id(2)

            @pl.when(c == 0)
            def _():
                acc_ref[...] = p

            @pl.when(c > 0)
            def _():
                acc_ref[...] += p

            @pl.when(c == nc - 1)
            def _():
                finish(acc_ref[...])

    a_spec = pl.BlockSpec((bc, bm), lambda m, j, c: (c, m)) if ta else pl.BlockSpec((bm, bc), lambda m, j, c: (m, c))
    if b_slots:
        b_spec = (pl.BlockSpec((None, bj, bc), lambda m, j, c: (c, j, 0)) if tb
                  else pl.BlockSpec((None, bc, bj), lambda m, j, c: (j, c, 0)))
    else:
        b_spec = (pl.BlockSpec((bj, bc), lambda m, j, c: (j, c)) if tb
                  else pl.BlockSpec((bc, bj), lambda m, j, c: (c, j)))
    if out_slots:
        assert not has_add
        o_spec = pl.BlockSpec((None, bm, bj), lambda m, j, c: (j, m, 0))
        out_shape = jax.ShapeDtypeStruct((nj, M, bj), out_dtype)
    else:
        o_spec = pl.BlockSpec((bm, bj), lambda m, j, c: (m, j))
        out_shape = jax.ShapeDtypeStruct((M, J), out_dtype)
    in_specs = [a_spec, b_spec] + ([o_spec] if has_add else [])
    args = (a, b) + ((add,) if has_add else ())
    return pl.pallas_call(
        body, name=name, grid=(nm, nj, nc), in_specs=in_specs, out_specs=o_spec,
        out_shape=out_shape,
        scratch_shapes=[pltpu.VMEM((bm, bj), F32)] if nc > 1 else [],
        compiler_params=_params(3),
    )(*args)


def _rows(T):
    return _pick(T, (256, 128))


def _rms_fwd(x, w, *, name):
    T = x.shape[0]
    rb = _rows(T)

    def body(x_ref, w_ref, u_ref, ut_ref):
        xv = x_ref[...]
        r = lax.rsqrt(jnp.mean(xv * xv, axis=-1, keepdims=True) + EPS)
        u = xv * r * w_ref[...]
        u_ref[...] = u.astype(BF16)
        ut_ref[...] = u.T.astype(BF16)

    return pl.pallas_call(
        body, name=name, grid=(T // rb,),
        in_specs=[pl.BlockSpec((rb, D), lambda i: (i, 0)), pl.BlockSpec((1, D), lambda i: (0, 0))],
        out_specs=[pl.BlockSpec((rb, D), lambda i: (i, 0)), pl.BlockSpec((D, rb), lambda i: (0, i))],
        out_shape=[jax.ShapeDtypeStruct((T, D), BF16), jax.ShapeDtypeStruct((D, T), BF16)], compiler_params=_params(1),
    )(x, w)


def _out_proj(merged, w_out, x, norm_w):
    T = merged.shape[0]
    bm = _pick(T, (512, 256))

    def body(a_ref, b_ref, x_ref, w_ref, h_ref, v_ref, vt_ref):
        h = _dot(a_ref[...], b_ref[...], 1, 0) + x_ref[...]
        h_ref[...] = h
        r = lax.rsqrt(jnp.mean(h * h, axis=-1, keepdims=True) + EPS)
        v2 = h * r * w_ref[...]
        v_ref[...] = v2.astype(BF16)
        vt_ref[...] = v2.T.astype(BF16)

    blk = pl.BlockSpec((bm, D), lambda i: (i, 0))
    return pl.pallas_call(
        body, name="out_proj", grid=(T // bm,),
        in_specs=[blk, pl.BlockSpec((D, D), lambda i: (0, 0)), blk, pl.BlockSpec((1, D), lambda i: (0, 0))],
        out_specs=[blk, blk, pl.BlockSpec((D, bm), lambda i: (0, i))],
        out_shape=[jax.ShapeDtypeStruct((T, D), F32), jax.ShapeDtypeStruct((T, D), BF16),
                   jax.ShapeDtypeStruct((D, T), BF16)],
        compiler_params=_params(1),
    )(merged, w_out, x, norm_w)


def _rms_bwd(dy, h, w, resid, *, name):
    T = h.shape[0]
    rb = _rows(T)

    def body(dy_ref, h_ref, w_ref, res_ref, dh_ref, dhb_ref, dw_ref):
        @pl.when(pl.program_id(0) == 0)
        def _():
            dw_ref[...] = jnp.zeros_like(dw_ref)

        hv = h_ref[...]
        r = lax.rsqrt(jnp.mean(hv * hv, axis=-1, keepdims=True) + EPS)
        hn = hv * r
        dyv = dy_ref[...]
        dw_ref[...] += jnp.sum(dyv * hn, axis=0, keepdims=True)
        t = dyv * w_ref[...]
        dh = res_ref[...] + r * (t - hn * jnp.mean(t * hn, axis=-1, keepdims=True))
        dh_ref[...] = dh
        dhb_ref[...] = dh.astype(BF16)

    blk = pl.BlockSpec((rb, D), lambda i: (i, 0))
    vec = pl.BlockSpec((1, D), lambda i: (0, 0))
    return pl.pallas_call(
        body, name=name, grid=(T // rb,), in_specs=[blk, blk, vec, blk], out_specs=[blk, blk, vec],
        out_shape=[jax.ShapeDtypeStruct((T, D), F32), jax.ShapeDtypeStruct((T, D), BF16),
                   jax.ShapeDtypeStruct((1, D), F32)],
        compiler_params=_params(1),
    )(dy, h, w, resid)


def _rms_bwd_input(du_main, dproj_lr, w_in_lr, x, w, resid):
    T = x.shape[0]
    rb = _rows(T)

    def body(du_ref, dl_ref, wl_ref, x_ref, w_ref, res_ref, dx_ref, dw_ref):
        @pl.when(pl.program_id(0) == 0)
        def _():
            dw_ref[...] = jnp.zeros_like(dw_ref)

        xv = x_ref[...]
        r = lax.rsqrt(jnp.mean(xv * xv, axis=-1, keepdims=True) + EPS)
        xn = xv * r
        du = du_ref[...] + _dot(dl_ref[...], wl_ref[...], 1, 1)
        dw_ref[...] += jnp.sum(du * xn, axis=0, keepdims=True)
        t = du * w_ref[...]
        dx_ref[...] = res_ref[...] + r * (t - xn * jnp.mean(t * xn, axis=-1, keepdims=True))

    blk = pl.BlockSpec((rb, D), lambda i: (i, 0))
    vec = pl.BlockSpec((1, D), lambda i: (0, 0))
    return pl.pallas_call(
        body, name="rms1_bwd", grid=(T // rb,),
        in_specs=[blk, pl.BlockSpec((rb, LRP), lambda i: (i, 0)), pl.BlockSpec((D, LRP), lambda i: (0, 0)), blk, vec,
                  blk],
        out_specs=[blk, vec], out_shape=[jax.ShapeDtypeStruct((T, D), F32), jax.ShapeDtypeStruct((1, D), F32)],
        compiler_params=_params(1),
    )(du_main, dproj_lr, w_in_lr, x, w, resid)


def _final_loss(h, w, target):
    T = h.shape[0]
    rb = _rows(T)

    def body(h_ref, w_ref, t_ref, loss_ref, dh_ref, dhb_ref, dw_ref):
        @pl.when(pl.program_id(0) == 0)
        def _():
            dw_ref[...] = jnp.zeros_like(dw_ref)
            loss_ref[...] = jnp.zeros_like(loss_ref)

        hv = h_ref[...]
        wv = w_ref[...]
        r = lax.rsqrt(jnp.mean(hv * hv, axis=-1, keepdims=True) + EPS)
        hn = hv * r
        e = hn * wv - t_ref[...]
        row = jnp.sum(e * e, axis=-1, keepdims=True) * (0.5 / D)
        loss_ref[...] += jnp.broadcast_to(jnp.sum(row, axis=0, keepdims=True), loss_ref.shape)
        dy = e * (1.0 / D)
        dw_ref[...] += jnp.sum(dy * hn, axis=0, keepdims=True)
        t = dy * wv
        dh = r * (t - hn * jnp.mean(t * hn, axis=-1, keepdims=True))
        dh_ref[...] = dh
        dhb_ref[...] = dh.astype(BF16)

    blk = pl.BlockSpec((rb, D), lambda i: (i, 0))
    vec = pl.BlockSpec((1, D), lambda i: (0, 0))
    return pl.pallas_call(
        body, name="final_loss", grid=(T // rb,), in_specs=[blk, vec, blk],
        out_specs=[pl.BlockSpec((8, 128), lambda i: (0, 0)), blk, blk, vec],
        out_shape=[jax.ShapeDtypeStruct((8, 128), F32), jax.ShapeDtypeStruct((T, D), F32),
                   jax.ShapeDtypeStruct((T, D), BF16), jax.ShapeDtypeStruct((1, D), F32)],
        compiler_params=_params(1),
    )(h, w, target)


PANEL = FFN // 4


def _ffn_fwd(v2, w_gate, w_up):
    T = v2.shape[0]
    bm = _pick(T, (512, 256))

    def body(a_ref, bg_ref, bu_ref, gate_ref, up_ref, ff_ref, fft_ref):
        a = a_ref[...]
        g = _dot(a, bg_ref[...], 1, 0)
        u = _dot(a, bu_ref[...], 1, 0)
        gate_ref[...] = g
        up_ref[...] = u
        ff = g * _sigmoid(g) * u
        ff_ref[...] = ff.astype(BF16)
        fft_ref[...] = ff.T.astype(BF16)

    panel = pl.BlockSpec((None, D, PANEL), lambda j, m: (j, 0, 0))
    out = pl.BlockSpec((bm, PANEL), lambda j, m: (m, j))
    return pl.pallas_call(
        body, name="ffn_fwd", grid=(4, T // bm),
        in_specs=[pl.BlockSpec((bm, D), lambda j, m: (m, 0)), panel, panel],
        out_specs=[out, out, out, pl.BlockSpec((PANEL, bm), lambda j, m: (j, m))],
        out_shape=[jax.ShapeDtypeStruct((T, FFN), F32), jax.ShapeDtypeStruct((T, FFN), F32),
                   jax.ShapeDtypeStruct((T, FFN), BF16), jax.ShapeDtypeStruct((FFN, T), BF16)],
        compiler_params=_params(2),
    )(v2, w_gate, w_up)


def _ffn_bwd_hidden(dh2, w_down, gate, up):
    T = dh2.shape[0]
    bm = _pick(T, (512, 256))

    def body(a_ref, b_ref, g_ref, u_ref, dg_ref, du_ref):
        d = _dot(a_ref[...], b_ref[...], 1, 1)
        g = g_ref[...]
        sg = _sigmoid(g)
        du_ref[...] = (d * g * sg).astype(BF16)
        dg_ref[...] = (d * u_ref[...] * sg * (1.0 + g * (1.0 - sg))).astype(BF16)

    blk = pl.BlockSpec((bm, PANEL), lambda j, m: (m, j))
    return pl.pallas_call(
        body, name="ffn_bwd_hidden", grid=(4, T // bm),
        in_specs=[pl.BlockSpec((bm, D), lambda j, m: (m, 0)), pl.BlockSpec((PANEL, D), lambda j, m: (j, 0)), blk, blk],
        out_specs=[blk, blk], out_shape=[jax.ShapeDtypeStruct((T, FFN), BF16)] * 2, compiler_params=_params(2),
    )(dh2, w_down, gate, up)


def _ffn_bwd_input(dgate, dup, w_gate, w_up):
    T = dgate.shape[0]
    bm = _pick(T, (1024, 512, 256))
    bj = 1024

    def body(ag_ref, au_ref, bg_ref, bu_ref, o_ref, acc_ref):
        c = pl.program_id(2)
        p = _dot(ag_ref[...], bg_ref[...], 1, 1) + _dot(au_ref[...], bu_ref[...], 1, 1)

        @pl.when(c == 0)
        def _():
            acc_ref[...] = p

        @pl.when(c > 0)
        def _():
            acc_ref[...] += p

        @pl.when(c == 3)
        def _():
            o_ref[...] = acc_ref[...]

    a = pl.BlockSpec((bm, PANEL), lambda m, j, c: (m, c))
    b = pl.BlockSpec((None, bj, PANEL), lambda m, j, c: (c, j, 0))
    return pl.pallas_call(
        body, name="ffn_bwd_input", grid=(T // bm, D // bj, 4), in_specs=[a, a, b, b],
        out_specs=pl.BlockSpec((bm, bj), lambda m, j, c: (m, j)), out_shape=jax.ShapeDtypeStruct((T, D), F32),
        scratch_shapes=[pltpu.VMEM((bm, bj), F32)], compiler_params=_params(3),
    )(dgate, dup, w_gate, w_up)


def _colsum(x, *, name):
    T, W = x.shape
    rb = _rows(T)

    def body(x_ref, o_ref):
        @pl.when(pl.program_id(0) == 0)
        def _():
            o_ref[...] = jnp.zeros_like(o_ref)

        o_ref[...] += jnp.sum(x_ref[...], axis=0, keepdims=True)

    return pl.pallas_call(
        body, name=name, grid=(T // rb,), in_specs=[pl.BlockSpec((rb, W), lambda i: (i, 0))],
        out_specs=pl.BlockSpec((1, W), lambda i: (0, 0)), out_shape=jax.ShapeDtypeStruct((1, W), F32),
        compiler_params=_params(1),
    )(x)


def _merge_fwd(attn_o, gla_raw, proj, gla_norm_w):
    T = attn_o.shape[0]
    rb = _rows(T)

    def body(a_ref, g_ref, gr_ref, ga_ref, gb_ref, w_ref, o_ref, ot_ref):
        graw = g_ref[...]
        r = lax.rsqrt(jnp.mean(graw * graw, axis=-1, keepdims=True) + EPS)
        gr = gr_ref[...]
        go = graw * r * w_ref[...] * (gr * _sigmoid(gr))
        merged = _sigmoid(ga_ref[...]) * a_ref[...] + _sigmoid(gb_ref[...]) * go
        o_ref[...] = merged.astype(BF16)
        ot_ref[...] = merged.T.astype(BF16)

    def sec(k):
        return pl.BlockSpec((rb, DV), lambda i, h: (i, O_MERGE // DV + 3 * h + k))

    blk = pl.BlockSpec((rb, DV), lambda i, h: (i, h))
    return pl.pallas_call(
        body, name="merge_fwd", grid=(T // rb, GLA_H),
        in_specs=[blk, blk, sec(0), sec(1), sec(2), pl.BlockSpec((1, DV), lambda i, h: (0, 0))],
        out_specs=[blk, pl.BlockSpec((DV, rb), lambda i, h: (h, i))],
        out_shape=[jax.ShapeDtypeStruct((T, D), BF16), jax.ShapeDtypeStruct((D, T), BF16)], compiler_params=_params(2),
    )(attn_o, gla_raw, proj, proj, proj, gla_norm_w)


def _merge_bwd(dm, attn_o, gla_raw, proj, gla_norm_w):
    T = attn_o.shape[0]
    rb = _rows(T)

    def body(dm_ref, a_ref, g_ref, gr_ref, ga_ref, gb_ref, w_ref, da_ref, dg_ref, dp_ref, dw_ref):
        @pl.when((pl.program_id(0) == 0) & (pl.program_id(1) == 0))
        def _():
            dw_ref[...] = jnp.zeros_like(dw_ref)

        dmv = dm_ref[...]
        av = a_ref[...]
        graw = g_ref[...]
        gr = gr_ref[...]
        wv = w_ref[...]
        sa = _sigmoid(ga_ref[...])
        sb = _sigmoid(gb_ref[...])
        r = lax.rsqrt(jnp.mean(graw * graw, axis=-1, keepdims=True) + EPS)
        gnh = graw * r
        gn = gnh * wv
        sr = _sigmoid(gr)
        sl = gr * sr
        go = gn * sl
        da_ref[...] = (dmv * sa).astype(BF16)
        dgo = dmv * sb
        dp_ref[:, 0:DV] = (dgo * gn * sr * (1.0 + gr * (1.0 - sr))).astype(BF16)
        dp_ref[:, DV:2 * DV] = (dmv * av * sa * (1.0 - sa)).astype(BF16)
        dp_ref[:, 2 * DV:3 * DV] = (dmv * go * sb * (1.0 - sb)).astype(BF16)
        dgn = dgo * sl
        dw_ref[...] += jnp.sum(dgn * gnh, axis=0, keepdims=True)
        t = dgn * wv
        dg_ref[...] = (r * (t - gnh * jnp.mean(t * gnh, axis=-1, keepdims=True))).astype(BF16)

    def sec(k):
        return pl.BlockSpec((rb, DV), lambda i, h: (i, O_MERGE // DV + 3 * h + k))

    blk = pl.BlockSpec((rb, DV), lambda i, h: (i, h))
    vec = pl.BlockSpec((1, DV), lambda i, h: (0, 0))
    return pl.pallas_call(
        body, name="merge_bwd", grid=(T // rb, GLA_H),
        in_specs=[blk, blk, blk, sec(0), sec(1), sec(2), vec],
        out_specs=[blk, blk, pl.BlockSpec((rb, W_MERGE), lambda i, h: (i, O_MERGE // W_MERGE + h)), vec],
        out_shape=[jax.ShapeDtypeStruct((T, D), BF16), jax.ShapeDtypeStruct((T, D), BF16),
                   jax.ShapeDtypeStruct((T, MAIN), BF16), jax.ShapeDtypeStruct((1, DV), F32)],
        compiler_params=_params(2),
    )(dm, attn_o, gla_raw, proj, proj, proj, gla_norm_w)


def _attn_band(n):
    qi = lax.broadcasted_iota(jnp.int32, (GROUP * WINDOW, WINDOW), 0) & (WINDOW - 1)
    kj = lax.broadcasted_iota(jnp.int32, (GROUP * WINDOW, WINDOW), 1)
    cur = kj <= qi
    return cur, cur | (n > 0)


def _stack_heads(ref, h, scale=1.0):
    rows = jnp.concatenate(
        [ref[:, (h * GROUP + g) * HEAD_DIM:(h * GROUP + g + 1) * HEAD_DIM] for g in range(GROUP)], axis=0)
    return (rows * scale).astype(BF16) if scale != 1.0 else rows.astype(BF16)


def _stack_sinks(s_ref, h):
    return jnp.concatenate(
        [jnp.broadcast_to(s_ref[:, h * GROUP + g:h * GROUP + g + 1], (WINDOW, 1)) for g in range(GROUP)], axis=0)


def _attn_probs(qs, kp, kc, band, sink):
    cur, live = band
    s = jnp.where(cur, _dot(qs, kc, 1, 1), _dot(qs, kp, 1, 1))
    s = jnp.where(live, s, MASK_VALUE)
    m = jnp.maximum(jnp.max(s, axis=-1, keepdims=True), sink)
    e = jnp.exp(s - m)
    es = jnp.exp(sink - m)
    inv = 1.0 / (jnp.sum(e, axis=-1, keepdims=True) + es)
    return e * inv, es * inv


def _unfold(cur, x):
    mine = jnp.where(cur, x, 0.0)
    return mine.astype(BF16), (x - mine).astype(BF16)


def _attn_specs(nb, rev):
    def at(n):
        return (nb - 1 - n) if rev else n

    kcol, vcol = (O_ATTN + D) // 256, (O_ATTN + D) // 256 + 1
    q = pl.BlockSpec((WINDOW, D), lambda n: (at(n), O_ATTN // D))
    kc = pl.BlockSpec((WINDOW, 256), lambda n: (at(n), kcol))
    kp = pl.BlockSpec((WINDOW, 256), lambda n: (jnp.maximum(at(n) - 1, 0), kcol))
    vc = pl.BlockSpec((WINDOW, 256), lambda n: (at(n), vcol))
    vp = pl.BlockSpec((WINDOW, 256), lambda n: (jnp.maximum(at(n) - 1, 0), vcol))
    sk = pl.BlockSpec((1, 128), lambda n: (0, 0))
    o = pl.BlockSpec((WINDOW, D), lambda n: (at(n), 0))
    return q, kc, kp, vc, vp, sk, o


def _attn_fwd(proj, sinks):
    T = proj.shape[0]
    nb = T // WINDOW

    def body(q_ref, kc_ref, kp_ref, vc_ref, vp_ref, s_ref, o_ref):
        band = _attn_band(pl.program_id(0))
        for h in range(N_KV):
            hs = slice(h * HEAD_DIM, (h + 1) * HEAD_DIM)
            p, _ = _attn_probs(_stack_heads(q_ref, h, HEAD_DIM ** -0.5), kp_ref[:, hs].astype(BF16),
                               kc_ref[:, hs].astype(BF16), band, _stack_sinks(s_ref, h))
            p_cur, p_prev = _unfold(band[0], p)
            o = _dot(p_cur, vc_ref[:, hs].astype(BF16), 1, 0) + _dot(p_prev, vp_ref[:, hs].astype(BF16), 1, 0)
            for g in range(GROUP):
                hg = h * GROUP + g
                o_ref[:, hg * HEAD_DIM:(hg + 1) * HEAD_DIM] = o[g * WINDOW:(g + 1) * WINDOW]

    q, kc, kp, vc, vp, sk, o = _attn_specs(nb, False)
    return pl.pallas_call(
        body, name="attn_fwd", grid=(nb,), in_specs=[q, kc, kp, vc, vp, sk], out_specs=o,
        out_shape=jax.ShapeDtypeStruct((T, D), F32), compiler_params=_params(1),
    )(proj, proj, proj, proj, proj, sinks)


def _attn_bwd(proj, sinks, d_o, dproj):
    T = proj.shape[0]
    nb = T // WINDOW
    kat, vat = D, D + N_KV * HEAD_DIM

    def body(q_ref, kc_ref, kp_ref, vc_ref, vp_ref, s_ref, do_ref, _, dp_ref, ds_ref, ck_ref, cv_ref):
        i = pl.program_id(0)
        n = nb - 1 - i

        @pl.when(i == 0)
        def _():
            ck_ref[...] = jnp.zeros_like(ck_ref)
            cv_ref[...] = jnp.zeros_like(cv_ref)
            ds_ref[...] = jnp.zeros_like(ds_ref)

        band = _attn_band(n)
        for h in range(N_KV):
            hs = slice(h * HEAD_DIM, (h + 1) * HEAD_DIM)
            kp, kc = kp_ref[:, hs].astype(BF16), kc_ref[:, hs].astype(BF16)
            vp, vc = vp_ref[:, hs].astype(BF16), vc_ref[:, hs].astype(BF16)
            qs = _stack_heads(q_ref, h, HEAD_DIM ** -0.5)
            do = _stack_heads(do_ref, h)
            p, ps = _attn_probs(qs, kp, kc, band, _stack_sinks(s_ref, h))
            dp = jnp.where(band[0], _dot(do, vc, 1, 1), _dot(do, vp, 1, 1))
            delta = jnp.sum(p * dp, axis=-1, keepdims=True)
            ds_cur, ds_prev = _unfold(band[0], p * (dp - delta))
            p_cur, p_prev = _unfold(band[0], p)
            dsink = -ps * delta
            dq = ((_dot(ds_cur, kc, 1, 0) + _dot(ds_prev, kp, 1, 0)) * (HEAD_DIM ** -0.5)).astype(BF16)
            for g in range(GROUP):
                hg = h * GROUP + g
                rows = slice(g * WINDOW, (g + 1) * WINDOW)
                ds_ref[hg:hg + 1, :] += jnp.broadcast_to(jnp.sum(dsink[rows], axis=0, keepdims=True), (1, 128))
                dp_ref[:, hg * HEAD_DIM:(hg + 1) * HEAD_DIM] = dq[rows]
            dp_ref[:, kat + h * HEAD_DIM:kat + (h + 1) * HEAD_DIM] = (_dot(ds_cur, qs, 0, 0) + ck_ref[:, hs]).astype(BF16)
            dp_ref[:, vat + h * HEAD_DIM:vat + (h + 1) * HEAD_DIM] = (_dot(p_cur, do, 0, 0) + cv_ref[:, hs]).astype(BF16)
            ck_ref[:, hs] = _dot(ds_prev, qs, 0, 0)
            cv_ref[:, hs] = _dot(p_prev, do, 0, 0)

    q, kc, kp, vc, vp, sk, o = _attn_specs(nb, True)
    return pl.pallas_call(
        body, name="attn_bwd", grid=(nb,), in_specs=[q, kc, kp, vc, vp, sk, o, ANY],
        out_specs=[pl.BlockSpec((WINDOW, W_ATTN), lambda n: (nb - 1 - n, O_ATTN // W_ATTN)),
                   pl.BlockSpec((N_Q, 128), lambda n: (0, 0))],
        out_shape=[jax.ShapeDtypeStruct((T, MAIN), BF16), jax.ShapeDtypeStruct((N_Q, 128), F32)],
        scratch_shapes=[pltpu.VMEM((WINDOW, 256), F32), pltpu.VMEM((WINDOW, 256), F32)],
        input_output_aliases={7: 0}, compiler_params=_params(1),
    )(proj, proj, proj, proj, proj, sinks, d_o, dproj)


def _split3(x):
    hi = x.astype(BF16)
    r1 = x - hi.astype(F32)
    mid = r1.astype(BF16)
    lo = (r1 - mid.astype(F32)).astype(BF16)
    return hi, mid, lo


def _tri_sum(tri, x):
    hi, mid, lo = _split3(x)
    return _dot(tri, hi, 1, 0) + _dot(tri, mid, 1, 0) + _dot(tri, lo, 1, 0)


def _chunk_masks(rb):
    r = lax.broadcasted_iota(jnp.int32, (rb, rb), 0)
    c = lax.broadcasted_iota(jnp.int32, (rb, rb), 1)
    same = (r ^ c) < CHUNK
    return same & (r >= c), same & (r <= c)


def _per_chunk_rows(x, per, pick):
    return jnp.concatenate(
        [jnp.broadcast_to(pick(x[i * CHUNK:(i + 1) * CHUNK]), (CHUNK, x.shape[1])) for i in range(per)], axis=0)


def _gla_block(q, k, lr, w2, b, lower):
    per = q.shape[0] // CHUNK
    logit = _dot(lr.astype(BF16), w2.astype(BF16), 1, 0) + b
    la = (jnp.minimum(logit, 0.0) - jnp.log(1.0 + jnp.exp(-jnp.abs(logit)))) * (1.0 / GATE_NORM)
    g = _tri_sum(lower, la)
    gl = _per_chunk_rows(g, per, lambda c: c[CHUNK - 1:CHUNK])
    eg = jnp.exp(g)
    qd = q * (DK ** -0.5) * eg
    ki = k * jnp.exp(-g)
    ke = k * jnp.exp(gl - g)
    return logit, g, gl, eg, qd, ki, ke


def _gla_rows(T, forward):
    return _pick(T, (512, 256, 128, 64) if forward else (256, 128, 64))


def _gla_fwd(proj, proj_lr, w2p, gate_b):
    T = proj.shape[0]
    rb = _gla_rows(T, True)
    per = rb // CHUNK

    def body(q_ref, k_ref, v_ref, lr_ref, w2_ref, b_ref, o_ref, st_ref, s_scr):
        @pl.when(pl.program_id(1) == 0)
        def _():
            s_scr[...] = jnp.zeros_like(s_scr)

        low, _ = _chunk_masks(rb)
        _, _, gl, _, qd, ki, ke = _gla_block(q_ref[...], k_ref[...], lr_ref[...], w2_ref[...], b_ref[...],
                                             low.astype(BF16))
        v = v_ref[...].astype(BF16)
        qdb, keb = qd.astype(BF16), ke.astype(BF16)
        att = jnp.where(low, _dot(qdb, ki.astype(BF16), 1, 1), 0.0).astype(BF16)
        o_intra = _dot(att, v, 1, 0)
        st = s_scr[...]
        for i in range(per):
            rows = slice(i * CHUNK, (i + 1) * CHUNK)
            st_ref[0, i] = st
            o_ref[rows, :] = o_intra[rows] + _dot(qdb[rows], st.astype(BF16), 1, 1)
            st = st * jnp.exp(gl[i * CHUNK:i * CHUNK + 1]) + _dot(v[rows], keb[rows], 0, 0)
        s_scr[...] = st

    return pl.pallas_call(
        body, name="gla_fwd", grid=(GLA_H, T // rb),
        in_specs=[pl.BlockSpec((rb, DK), lambda h, n: (n, (O_GLA + W_GLA * h) // DK)),
                  pl.BlockSpec((rb, DK), lambda h, n: (n, (O_GLA + W_GLA * h) // DK + 1)),
                  pl.BlockSpec((rb, DV), lambda h, n: (n, (O_GLA + W_GLA * h) // DV + 1)),
                  pl.BlockSpec((rb, LRP), lambda h, n: (n, 0)),
                  pl.BlockSpec((LRP, DK), lambda h, n: (0, h)),
                  pl.BlockSpec((1, DK), lambda h, n: (0, h))],
        out_specs=[pl.BlockSpec((rb, DV), lambda h, n: (n, h)),
                   pl.BlockSpec((1, per, DV, DK), lambda h, n: (h, n, 0, 0))],
        out_shape=[jax.ShapeDtypeStruct((T, GLA_H * DV), F32),
                   jax.ShapeDtypeStruct((GLA_H, T // CHUNK, DV, DK), F32)],
        scratch_shapes=[pltpu.VMEM((DV, DK), F32)], compiler_params=_params(2),
    )(proj, proj, proj, proj_lr, w2p, gate_b)


def _gla_bwd(proj, proj_lr, w2p, gate_b, states, d_o, dproj):
    T = proj.shape[0]
    rb = _gla_rows(T, False)
    per = rb // CHUNK
    nblk = T // rb

    def body(q_ref, k_ref, v_ref, lr_ref, w2_ref, b_ref, st_ref, do_ref, _, dp_ref, dl_ref, ds_scr):
        @pl.when(pl.program_id(1) == 0)
        def _():
            ds_scr[...] = jnp.zeros_like(ds_scr)

        low, up = _chunk_masks(rb)
        logit, g, gl, eg, qd, ki, ke = _gla_block(q_ref[...], k_ref[...], lr_ref[...], w2_ref[...], b_ref[...],
                                                  low.astype(BF16))
        v = v_ref[...].astype(BF16)
        do = do_ref[...].astype(BF16)
        qdb, kib, keb = qd.astype(BF16), ki.astype(BF16), ke.astype(BF16)
        att = jnp.where(low, _dot(qdb, kib, 1, 1), 0.0).astype(BF16)
        datt = jnp.where(low, _dot(do, v, 1, 1), 0.0).astype(BF16)
        dv_intra = _dot(att, do, 0, 0)
        dqd_intra = _dot(datt, kib, 1, 0)
        dki = _dot(datt, qdb, 0, 0)
        ds = ds_scr[...]
        dqd, dke, dgl = [None] * per, [None] * per, [None] * per
        for i in reversed(range(per)):
            rows = slice(i * CHUNK, (i + 1) * CHUNK)
            sp = st_ref[0, i]
            dsb = ds.astype(BF16)
            dp_ref[rows, 2 * DK:] = (dv_intra[rows] + _dot(keb[rows], dsb, 1, 1)).astype(BF16)
            dqd[i] = dqd_intra[rows] + _dot(do[rows], sp.astype(BF16), 1, 0)
            dke[i] = _dot(v[rows], dsb, 1, 0)
            decay = jnp.exp(gl[i * CHUNK:i * CHUNK + 1])
            dgl[i] = (jnp.sum(dke[i] * ke[rows], axis=0, keepdims=True)
                      + jnp.sum(ds * sp, axis=0, keepdims=True) * decay)
            ds = ds * decay + _dot(do[rows], qdb[rows], 0, 0)
        ds_scr[...] = ds
        dqd = jnp.concatenate(dqd, axis=0)
        dke = jnp.concatenate(dke, axis=0)
        dgl = jnp.concatenate([jnp.broadcast_to(d, (CHUNK, DK)) for d in dgl], axis=0)
        dp_ref[:, 0:DK] = (dqd * (DK ** -0.5) * eg).astype(BF16)
        dp_ref[:, DK:2 * DK] = (dki * jnp.exp(-g) + dke * jnp.exp(gl - g)).astype(BF16)
        dg = dqd * qd - dki * ki - dke * ke
        dla = _tri_sum(up.astype(BF16), dg) + dgl
        dl_ref[...] = dla * (1.0 / GATE_NORM) * (1.0 - _sigmoid(logit))

    def rev(n):
        return nblk - 1 - n

    return pl.pallas_call(
        body, name="gla_bwd", grid=(GLA_H, nblk),
        in_specs=[pl.BlockSpec((rb, DK), lambda h, n: (rev(n), (O_GLA + W_GLA * h) // DK)),
                  pl.BlockSpec((rb, DK), lambda h, n: (rev(n), (O_GLA + W_GLA * h) // DK + 1)),
                  pl.BlockSpec((rb, DV), lambda h, n: (rev(n), (O_GLA + W_GLA * h) // DV + 1)),
                  pl.BlockSpec((rb, LRP), lambda h, n: (rev(n), 0)),
                  pl.BlockSpec((LRP, DK), lambda h, n: (0, h)),
                  pl.BlockSpec((1, DK), lambda h, n: (0, h)),
                  pl.BlockSpec((1, per, DV, DK), lambda h, n: (h, rev(n), 0, 0)),
                  pl.BlockSpec((rb, DV), lambda h, n: (rev(n), h)),
                  ANY],
        out_specs=[pl.BlockSpec((rb, W_GLA), lambda h, n: (rev(n), O_GLA // W_GLA + h)),
                   pl.BlockSpec((rb, DK), lambda h, n: (rev(n), h))],
        out_shape=[jax.ShapeDtypeStruct((T, MAIN), BF16), jax.ShapeDtypeStruct((T, GLA_H * DK), F32)],
        scratch_shapes=[pltpu.VMEM((DV, DK), F32)], input_output_aliases={8: 0}, compiler_params=_params(2),
    )(proj, proj, proj, proj_lr, w2p, gate_b, states, d_o, dproj)


def _after(values, tokens):
    return lax.optimization_barrier((values, tokens))[0]


def _local_step(x, target, p, reduce_part, reduce_ffn, reduce_in):
    u, u_t = _rms_fwd(x, p["norm1_w"], name="rms1_fwd")
    whole = _pick(x.shape[0], (2048, 256))
    proj = _matmul(u, p["w_in_main"], bm=whole, name="mm_proj")
    proj_lr = _matmul(u, p["w_in_lr"], name="mm_proj_lr")
    attn_o = _attn_fwd(proj, p["sinks"])
    gla_raw, states = _gla_fwd(proj, proj_lr, p["w2p"], p["gate_b"])
    merged, merged_t = _merge_fwd(attn_o, gla_raw, proj, p["gla_norm_w"])
    h1, v2, v2_t = _out_proj(merged, p["w_out"], x, p["norm2_w"])
    gate, up, ff, ff_t = _ffn_fwd(v2, p["w_gate"], p["w_up"])
    h2 = _matmul(ff, p["w_down"], add=h1, name="mm_down")
    loss, dh2, dh2_b, g_final = _final_loss(h2, p["final_norm_w"], target)

    dgate, dup = _ffn_bwd_hidden(dh2_b, p["w_down"], gate, up)
    sent = [reduce_ffn("w_down", ff_t, dh2_b), reduce_ffn("w_gate", v2_t, dgate), reduce_ffn("w_up", v2_t, dup)]
    dgate, dup = _after((dgate, dup), sent)
    dv2 = _ffn_bwd_input(dgate, dup, p["w_gate"], p["w_up"])
    dh1, dh1_b, g_norm2 = _rms_bwd(dv2, h1, p["norm2_w"], dh2, name="rms2_bwd")
    dmerged = _matmul(dh1_b, p["w_out"], tb=True, bj=D, name="mm_dmerged")
    sent = reduce_part("w_out", _matmul(merged_t, dh1_b, out_dtype=BF16, bj=D, name="mm_gout").reshape(4, D // 4, D))
    dmerged = _after(dmerged, sent)
    d_attn, d_gla, dproj, g_gla_norm = _merge_bwd(dmerged, attn_o, gla_raw, proj, p["gla_norm_w"])
    dproj, dlogit = _gla_bwd(proj, proj_lr, p["w2p"], p["gate_b"], states, d_gla, dproj)
    dproj, g_sinks = _attn_bwd(proj, p["sinks"], d_attn, dproj)
    g_gate_b = _colsum(dlogit, name="colsum_gate_b")
    g_w2 = _matmul(proj_lr, dlogit, ta=True, name="mm_gw2")
    dproj_lr = _matmul(dlogit, p["w2p"], tb=True, out_dtype=BF16, name="mm_dlr")
    reduce_in(u_t, dproj, dproj_lr)
    du = _matmul(dproj, p["w_in_main"], tb=True, bm=_pick(x.shape[0], (1024, 256)), bj=1024, bc=3200, name="mm_du")
    dx, g_norm1 = _rms_bwd_input(du, dproj_lr, p["w_in_lr"], x, p["norm1_w"], dh1)
    grads = dict(norm1_w=g_norm1, w2=g_w2[:RANK], gate_b=g_gate_b,
                 sinks=g_sinks[:, 0].reshape(1, N_Q), gla_norm_w=g_gla_norm, norm2_w=g_norm2, final_norm_w=g_final)
    return loss, dx, grads


def _place():
    x, y, c = lax.axis_index("x"), lax.axis_index("y"), lax.axis_index("c")
    chips = [(1 - x, y), (x, 1 - y), (1 - x, 1 - y)]
    return x, y, c, chips


def _hbm_shape(s, dt):
    return jax.ShapeDtypeStruct(s, dt)


ID_SIBLING = 9
ID_GATHER_IN = 10


def _handshake(peers):
    barrier = pltpu.get_barrier_semaphore()
    for peer in peers:
        pl.semaphore_signal(barrier, inc=1, device_id=peer, device_id_type=MESH)
    pl.semaphore_wait(barrier, len(peers))


def _launch_copies(body, args, out_shape, sems, *, name, collective_id=None):
    if collective_id is None:
        return pl.pallas_call(
            body, name=name, in_specs=[ANY] * len(args), out_specs=[ANY] * len(out_shape), out_shape=out_shape,
            scratch_shapes=sems)(*args)
    return pl.kernel(
        body, name=name, out_type=out_shape, mesh=plsc.ScalarSubcoreMesh(axis_name="sequencer", num_cores=1),
        scratch_types=sems, compiler_params=pltpu.CompilerParams(collective_id=collective_id))(*args)


def _gather_shards(shards, *, name, collective_id=None, after=()):
    n = len(shards)
    first_out = n + len(after)

    def body(*refs):
        ins, outs = refs[:n], refs[first_out:first_out + n]
        ici_send, ici_recv, dia_send, dia_recv, d2d_send, d2d_recv, local_sem = refs[first_out + n:]
        x, y, c, chips = _place()
        me = 2 * x + y
        sibling = (x, y, 1 - c)
        slots = [2 * chip[0] + chip[1] for chip in chips]
        if collective_id is not None:
            _handshake([sibling] + [(*chip, c) for chip in chips])

        def half(w, slot, hc):
            r2 = shards[w].shape[0] // 2
            return outs[w].at[slot, pl.ds(hc * r2, r2), :]

        def quarter(w, slot, q):
            r4 = shards[w].shape[0] // 4
            return outs[w].at[slot, pl.ds((2 * c + q) * r4, r4), :]

        def onward(w, j, q, to):
            return pltpu.make_async_remote_copy(
                src_ref=quarter(w, slots[j], q), dst_ref=quarter(w, slots[j], q), send_sem=dia_send.at[w * 2 + q],
                recv_sem=dia_recv.at[w * 2 + q], device_id=(*chips[to], c), device_id_type=MESH)

        def to_sibling(w, j):
            got = half(w, slots[j], c)
            return pltpu.make_async_remote_copy(
                src_ref=got, dst_ref=got, send_sem=d2d_send.at[w * 3 + j], recv_sem=d2d_recv.at[w * 3 + j],
                device_id=sibling, device_id_type=MESH)

        locals_ = [pltpu.make_async_copy(ins[w], outs[w].at[me], local_sem.at[w]) for w in range(n)]
        for cp in locals_:
            cp.start()
        sends = []
        for w in range(n):
            r2 = shards[w].shape[0] // 2
            for j in range(2):
                cp = pltpu.make_async_remote_copy(
                    src_ref=ins[w].at[pl.ds(c * r2, r2), :], dst_ref=half(w, me, c),
                    send_sem=ici_send.at[w * 2 + j], recv_sem=ici_recv.at[w * 2 + j],
                    device_id=(*chips[j], c), device_id_type=MESH)
                cp.start()
                sends.append(cp)
        for w in range(n):
            for j in range(2):
                got = half(w, slots[j], c)
                pltpu.make_async_remote_copy(
                    src_ref=got, dst_ref=got, send_sem=ici_send.at[w * 2 + j], recv_sem=ici_recv.at[w * 2 + j],
                    device_id=(*chips[j], c), device_id_type=MESH).wait_recv()
                for cp in (onward(w, j, j, 1 - j), to_sibling(w, j)):
                    cp.start()
                    sends.append(cp)
        for w in range(n):
            for q in range(2):
                onward(w, 2, q, 0).wait_recv()
            cp = to_sibling(w, 2)
            cp.start()
            sends.append(cp)
        for w in range(n):
            for j, chip in enumerate(chips):
                slot = 2 * chip[0] + chip[1]
                got = half(w, slot, 1 - c)
                pltpu.make_async_remote_copy(
                    src_ref=got, dst_ref=got, send_sem=d2d_send.at[w * 3 + j], recv_sem=d2d_recv.at[w * 3 + j],
                    device_id=sibling, device_id_type=MESH).wait_recv()
        for cp in sends:
            cp.wait_send()
        for cp in locals_:
            cp.wait()

    return _launch_copies(
        body, list(shards) + list(after), [_hbm_shape((4,) + s.shape, s.dtype) for s in shards],
        [pltpu.SemaphoreType.DMA((2 * n,))] * 4 + [pltpu.SemaphoreType.DMA((3 * n,))] * 2
        + [pltpu.SemaphoreType.DMA((n,))],
        name=name, collective_id=collective_id)


def _gather_staged(w, small):
    R2, C = w.shape
    R = 2 * R2
    rb = 256
    per = R2 // rb
    assert per % 2 == 0
    r2 = small.shape[0] // 2
    order = [(j, k) for k in range(per) for j in range(2)] + [(2, k) for k in range(per)]

    def body(w_ref, s_ref, out_ref, outs_ref, stage, big_send, big_recv, dia_send, dia_recv, small_send, small_recv,
             fwd_send, fwd_recv, in_sem, put_sem, own_send, own_recv, push_send, push_recv):
        x, y, c, chips = _place()
        me = 2 * x + y
        sibling = (x, y, 1 - c)
        slots = [2 * chip[0] + chip[1] for chip in chips]
        _handshake([sibling] + [(*chip, c) for chip in chips])

        def block(j, k, hc):
            return out_ref.at[slots[j], pl.ds(hc * R2 + k * rb, rb), :]

        def big(j, k):
            return pltpu.make_async_remote_copy(
                src_ref=w_ref.at[pl.ds(k * rb, rb), :], dst_ref=out_ref.at[me, pl.ds(c * R2 + k * rb, rb), :],
                send_sem=big_send.at[j * per + k], recv_sem=big_recv.at[j * per + k],
                device_id=(*chips[j], c), device_id_type=MESH)

        def passes_on(j, k):
            return (j == 0) == (k < per // 2)

        def onward(j, k, to):
            return pltpu.make_async_remote_copy(
                src_ref=block(j, k, c), dst_ref=block(j, k, c), send_sem=dia_send.at[k], recv_sem=dia_recv.at[k],
                device_id=(*chips[to], c), device_id_type=MESH)

        def little(j):
            return pltpu.make_async_remote_copy(
                src_ref=s_ref.at[pl.ds(c * r2, r2), :], dst_ref=outs_ref.at[me, pl.ds(c * r2, r2), :],
                send_sem=small_send.at[j], recv_sem=small_recv.at[j], device_id=(*chips[j], c), device_id_type=MESH)

        sends = [big(j, k) for k in range(per) for j in range(2)] + [little(j) for j in range(3)]
        for cp in sends:
            cp.start()
        mine = pltpu.make_async_copy(s_ref, outs_ref.at[me], put_sem.at[2])
        mine.start()

        def own(k, to_vmem):
            if to_vmem:
                return pltpu.make_async_copy(w_ref.at[pl.ds(k * rb, rb), :], stage.at[k % 2], in_sem.at[k % 2])
            return pltpu.make_async_copy(stage.at[k % 2], out_ref.at[me, pl.ds(c * R2 + k * rb, rb), :], put_sem.at[k % 2])

        def own_push(k, hc):
            return pltpu.make_async_remote_copy(
                src_ref=stage.at[k % 2], dst_ref=out_ref.at[me, pl.ds(hc * R2 + k * rb, rb), :],
                send_sem=own_send.at[k % 2], recv_sem=own_recv.at[k], device_id=sibling, device_id_type=MESH)

        for k in range(per):
            if k >= 2:
                own(k - 2, False).wait()
                own_push(k - 2, c).wait_send()
            own(k, True).start()
            own(k, True).wait()
            own(k, False).start()
            own_push(k, c).start()
        for k in range(per - 2, per):
            own(k, False).wait()
            own_push(k, c).wait_send()

        def push(t, hc):
            j, k = order[t]
            return pltpu.make_async_remote_copy(
                src_ref=stage.at[t % 2], dst_ref=block(j, k, hc), send_sem=push_send.at[t % 2],
                recv_sem=push_recv.at[t], device_id=sibling, device_id_type=MESH)

        passed = []
        for t, (j, k) in enumerate(order):
            if j == 2:
                onward(2, k, 0).wait_recv()
            else:
                big(j, k).wait_recv()
                if passes_on(j, k):
                    passed.append(onward(j, k, 1 - j))
                    passed[-1].start()
            if t >= 2:
                push(t - 2, c).wait_send()
            fetch = pltpu.make_async_copy(block(j, k, c), stage.at[t % 2], in_sem.at[t % 2])
            fetch.start()
            fetch.wait()
            push(t, c).start()
        for j in range(3):
            got = outs_ref.at[slots[j], pl.ds(c * r2, r2), :]
            little(j).wait_recv()
            fw = pltpu.make_async_remote_copy(
                src_ref=got, dst_ref=got, send_sem=fwd_send.at[j], recv_sem=fwd_recv.at[j],
                device_id=sibling, device_id_type=MESH)
            fw.start()
            passed.append(fw)
        for t in range(len(order) - 2, len(order)):
            push(t, c).wait_send()
        for t in range(len(order)):
            push(t, 1 - c).wait_recv()
        for k in range(per):
            own_push(k, 1 - c).wait_recv()
        for j in range(3):
            theirs = outs_ref.at[slots[j], pl.ds((1 - c) * r2, r2), :]
            pltpu.make_async_remote_copy(
                src_ref=theirs, dst_ref=theirs, send_sem=fwd_send.at[j], recv_sem=fwd_recv.at[j],
                device_id=sibling, device_id_type=MESH).wait_recv()
        for cp in sends + passed:
            cp.wait_send()
        mine.wait()

    dma = pltpu.SemaphoreType.DMA
    return pl.pallas_call(
        body, name="gather_in", in_specs=[ANY, ANY], out_specs=[ANY, ANY],
        out_shape=[_hbm_shape((4, R, C), w.dtype), _hbm_shape((4,) + small.shape, small.dtype)],
        scratch_shapes=[pltpu.VMEM((2, rb, C), w.dtype), dma((2 * per,)), dma((2 * per,)), dma((per,)), dma((per,)),
                        dma((3,)), dma((3,)), dma((3,)), dma((3,)), dma((2,)), dma((3,)), dma((2,)), dma((per,)),
                        dma((2,)), dma((len(order),))],
        compiler_params=pltpu.CompilerParams(vmem_limit_bytes=VMEM_LIMIT, collective_id=ID_GATHER_IN),
    )(w, small)


def _pair_blocks(R2):
    return _pick(R2, (256, 352, 128, 64, 32, 16))


def _grad_matmul_split(a, b, *, cut, name):
    M, T = a.shape
    J = b.shape[1]
    bw = None
    if cut == "rows":
        S, R, C, bm = 4, M // 4, J, 352
        per = R // 2 // bm
        n = M // bm

        def where(t):
            return t // (2 * per), (t // per) % 2, t % per, 0

        a_spec = pl.BlockSpec((bm, T), lambda t: (t, 0))
        b_spec = pl.BlockSpec((T, C), lambda t: (0, 0))
    elif cut == "panels":
        S, R, C, bm = 4, M, J // 4, M // 2
        n = 8

        def where(t):
            return t // 2, t % 2, 0, 0

        a_spec = pl.BlockSpec((bm, T), lambda t: (t % 2, 0))
        b_spec = pl.BlockSpec((T, C), lambda t: (0, t // 2))
    else:
        S, R, C, bm, bw = 1, M, J, M // 2, 1280
        n = 2 * (J // bw)

        def where(t):
            return 0, t % 2, 0, t // 2

        a_spec = pl.BlockSpec((bm, T), lambda t: (t % 2, 0))
        b_spec = pl.BlockSpec((T, bw), lambda t: (0, t // 2))
    R2 = R // 2
    tw = bw or C

    def body(a_ref, b_ref, mine_ref, got_ref, tile, loc_sem, snd_sem, rcv_sem):
        x, y, c, _ = _place()
        sibling = (x, y, 1 - c)
        t = pl.program_id(0)

        @pl.when(t == 0)
        def _():
            _handshake([sibling])

        def place(ref, u):
            s, _, i, j = where(u)
            return ref.at[s, pl.ds(i * bm, bm), pl.ds(j * tw, tw)]

        def keep(u):
            return pltpu.make_async_copy(tile.at[u % 2], place(mine_ref, u), loc_sem.at[u % 2])

        def give(u):
            return pltpu.make_async_remote_copy(
                src_ref=tile.at[u % 2], dst_ref=place(got_ref, u), send_sem=snd_sem.at[u % 2],
                recv_sem=rcv_sem, device_id=sibling, device_id_type=MESH)

        def settle(u):
            @pl.when(where(u)[1] == c)
            def _():
                keep(u).wait()

            @pl.when(where(u)[1] != c)
            def _():
                give(u).wait_send()

        @pl.when(t >= 2)
        def _():
            settle(t - 2)

        tile[t % 2] = _dot(a_ref[...], b_ref[...], 1, 0).astype(BF16)

        @pl.when(where(t)[1] == c)
        def _():
            keep(t).start()

        @pl.when(where(t)[1] != c)
        def _():
            give(t).start()

        @pl.when(t == n - 1)
        def _():
            settle(t - 1)
            settle(t)
            pltpu.make_async_remote_copy(
                src_ref=got_ref, dst_ref=got_ref, send_sem=snd_sem.at[0], recv_sem=rcv_sem,
                device_id=sibling, device_id_type=MESH).wait_recv()

    half = _hbm_shape((S, R2, C), BF16)
    return pl.pallas_call(
        body, name=name, grid=(n,), in_specs=[a_spec, b_spec], out_specs=[ANY, ANY], out_shape=[half, half],
        scratch_shapes=[pltpu.VMEM((2, bm, tw), BF16), pltpu.SemaphoreType.DMA((2,)), pltpu.SemaphoreType.DMA((2,)),
                        pltpu.SemaphoreType.DMA],
        compiler_params=pltpu.CompilerParams(
            dimension_semantics=("arbitrary",), vmem_limit_bytes=VMEM_LIMIT, collective_id=ID_SIBLING),
    )(a, b)


def _pair_add(mine, got, *, name):
    S, R2, C = mine.shape
    rb = _pick(R2, (512, 352, 256))
    cb = _pick(C, (2560,)) if C > 4096 else C

    def body(a_ref, b_ref, o_ref):
        o_ref[...] = (a_ref[...].astype(F32) + b_ref[...].astype(F32)).astype(BF16)

    blk = pl.BlockSpec((1, rb, cb), lambda s, i, j: (s, i, j))
    return pl.pallas_call(
        body, name=name, grid=(S, R2 // rb, C // cb), in_specs=[blk, blk], out_specs=blk,
        out_shape=jax.ShapeDtypeStruct(mine.shape, BF16), compiler_params=_params(3),
    )(mine, got)


def _pair_reduce(part, *, name):
    S, R, C = part.shape
    R2 = R // 2
    rb = _pick(R2, (512, 352, 256, 128, 64, 32, 16))
    nblk = R2 // rb
    steps = [(s, i) for s in range(S) for i in range(nblk)]
    n = len(steps)

    def body(part_ref, sums_ref, own_buf, oth_buf, rcv_buf, out_buf, own_sem, oth_sem, out_sem, send_sem, recv_sem):
        x, y, c, _ = _place()
        sibling = (x, y, 1 - c)
        _handshake([sibling])

        def fetch(t, half, buf, sem):
            s, i = steps[t]
            return pltpu.make_async_copy(part_ref.at[s, pl.ds(half * R2 + i * rb, rb), :], buf.at[t % 3], sem.at[t % 3])

        def push(t):
            return pltpu.make_async_remote_copy(
                src_ref=oth_buf.at[t % 3], dst_ref=rcv_buf.at[t % 4], send_sem=send_sem.at[t % 4],
                recv_sem=recv_sem.at[t % 4], device_id=sibling, device_id_type=MESH)

        def store(t):
            s, i = steps[t]
            return pltpu.make_async_copy(out_buf.at[t % 2], sums_ref.at[s, pl.ds(i * rb, rb), :], out_sem.at[t % 2])

        for t in range(min(2, n)):
            fetch(t, c, own_buf, own_sem).start()
            fetch(t, 1 - c, oth_buf, oth_sem).start()
        fetch(0, 1 - c, oth_buf, oth_sem).wait()
        push(0).start()
        for t in range(n):
            if t + 2 < n:
                fetch(t + 2, c, own_buf, own_sem).start()
                fetch(t + 2, 1 - c, oth_buf, oth_sem).start()
            if t + 1 < n:
                fetch(t + 1, 1 - c, oth_buf, oth_sem).wait()
                push(t + 1).start()
            fetch(t, c, own_buf, own_sem).wait()
            push(t).wait()
            if t >= 2:
                store(t - 2).wait()
            out_buf[t % 2] = (own_buf[t % 3].astype(F32) + rcv_buf[t % 4].astype(F32)).astype(BF16)
            store(t).start()
        for t in range(max(n - 2, 0), n):
            store(t).wait()

    def buf(k):
        return pltpu.VMEM((k, rb, C), BF16)

    def sems(k):
        return pltpu.SemaphoreType.DMA((k,))

    return pl.pallas_call(
        body, name=name, in_specs=[ANY], out_specs=ANY, out_shape=_hbm_shape((S, R2, C), BF16),
        scratch_shapes=[buf(3), buf(3), buf(4), buf(2), sems(3), sems(3), sems(2), sems(4), sems(4)],
        compiler_params=pltpu.CompilerParams(vmem_limit_bytes=VMEM_LIMIT, collective_id=ID_SIBLING),
    )(part)


def _chip_exchange(sums, *, name, collective_id=None):
    n = len(sums)

    def body(*refs):
        ins, outs = refs[:n], refs[n:2 * n]
        send_sem, recv_sem, local_sem = refs[2 * n:]
        x, y, c, chips = _place()
        me = 2 * x + y
        if collective_id is not None:
            _handshake([(*chip, c) for chip in chips])
        cps = []
        for w in range(n):
            lc = pltpu.make_async_copy(ins[w].at[me], outs[w].at[me], local_sem.at[w])
            lc.start()
            cps.append(lc)
            for j, chip in enumerate(chips):
                slot = 2 * chip[0] + chip[1]
                rc = pltpu.make_async_remote_copy(
                    src_ref=ins[w].at[slot], dst_ref=outs[w].at[me],
                    send_sem=send_sem.at[w * 3 + j], recv_sem=recv_sem.at[w * 3 + j],
                    device_id=(*chip, c), device_id_type=MESH)
                rc.start()
                cps.append(rc)
        for w in range(n):
            for j, chip in enumerate(chips):
                slot = 2 * chip[0] + chip[1]
                pltpu.make_async_remote_copy(
                    src_ref=ins[w].at[slot], dst_ref=outs[w].at[slot],
                    send_sem=send_sem.at[w * 3 + j], recv_sem=recv_sem.at[w * 3 + j],
                    device_id=(*chip, c), device_id_type=MESH).wait_recv()
        for w in range(n):
            cps[w * 4].wait()
            for j in range(3):
                cps[w * 4 + 1 + j].wait_send()

    return _launch_copies(
        body, sums, [_hbm_shape(s.shape, s.dtype) for s in sums],
        [pltpu.SemaphoreType.DMA((3 * n,)), pltpu.SemaphoreType.DMA((3 * n,)), pltpu.SemaphoreType.DMA((n,))],
        name=name, collective_id=collective_id)


def _sum_join(recv, *, name):
    _, R2, C = recv.shape
    rb = _pair_blocks(R2)
    nblk = R2 // rb

    def body(recv_ref, out_ref, in_buf, acc_buf, in_sem, loc_sem, send_sem, recv_sem):
        x, y, c, _ = _place()
        sibling = (x, y, 1 - c)
        _handshake([sibling])

        def fetch(t):
            return pltpu.make_async_copy(recv_ref.at[:, pl.ds(t * rb, rb), :], in_buf.at[t % 2], in_sem.at[t % 2])

        def rows(t, half):
            return out_ref.at[pl.ds(half * R2 + t * rb, rb), :]

        def put_local(t):
            return pltpu.make_async_copy(acc_buf.at[t % 2], rows(t, c), loc_sem.at[t])

        def put_remote(t, half):
            return pltpu.make_async_remote_copy(
                src_ref=acc_buf.at[t % 2], dst_ref=rows(t, half), send_sem=send_sem.at[t], recv_sem=recv_sem.at[t],
                device_id=sibling, device_id_type=MESH)

        fetch(0).start()
        for t in range(nblk):
            if t + 1 < nblk:
                fetch(t + 1).start()
            fetch(t).wait()
            if t >= 2:
                put_local(t - 2).wait()
                put_remote(t - 2, c).wait_send()
            acc = in_buf[t % 2, 0].astype(F32)
            for s in range(1, 4):
                acc = acc + in_buf[t % 2, s].astype(F32)
            acc_buf[t % 2] = acc
            put_local(t).start()
            put_remote(t, c).start()
        for t in range(max(nblk - 2, 0), nblk):
            put_local(t).wait()
            put_remote(t, c).wait_send()
        for t in range(nblk):
            put_remote(t, 1 - c).wait_recv()

    semn = pltpu.SemaphoreType.DMA((nblk,))
    return pl.pallas_call(
        body, name=name, in_specs=[ANY], out_specs=ANY, out_shape=_hbm_shape((2 * R2, C), F32),
        scratch_shapes=[pltpu.VMEM((2, 4, rb, C), BF16), pltpu.VMEM((2, rb, C), F32),
                        pltpu.SemaphoreType.DMA((2,)), semn, semn, semn],
        compiler_params=pltpu.CompilerParams(vmem_limit_bytes=VMEM_LIMIT, collective_id=ID_SIBLING),
    )(recv)


def _sum_small(pack):
    R, C = pack.shape

    def body(in_ref, out_ref, all_ref, send_sem, recv_sem):
        x, y, c, _ = _place()
        me = 4 * x + 2 * y + c
        all_ref[me] = in_ref[...]
        cps = []
        for k in range(1, 8):
            peer = (x ^ (k >> 2), y ^ ((k >> 1) & 1), c ^ (k & 1))
            cp = pltpu.make_async_remote_copy(
                src_ref=in_ref, dst_ref=all_ref.at[me], send_sem=send_sem.at[k - 1], recv_sem=recv_sem.at[k - 1],
                device_id=peer, device_id_type=MESH)
            cp.start()
            cps.append(cp)
        for k in range(1, 8):
            peer = (x ^ (k >> 2), y ^ ((k >> 1) & 1), c ^ (k & 1))
            slot = 4 * peer[0] + 2 * peer[1] + peer[2]
            pltpu.make_async_remote_copy(
                src_ref=in_ref, dst_ref=all_ref.at[slot], send_sem=send_sem.at[k - 1], recv_sem=recv_sem.at[k - 1],
                device_id=peer, device_id_type=MESH).wait_recv()
        for cp in cps:
            cp.wait_send()
        acc = all_ref[0]
        for d in range(1, 8):
            acc = acc + all_ref[d]
        out_ref[...] = acc

    return pl.pallas_call(
        body, name="sum_small", in_specs=[pl.BlockSpec(memory_space=pltpu.VMEM)],
        out_specs=pl.BlockSpec(memory_space=pltpu.VMEM), out_shape=jax.ShapeDtypeStruct((R, C), F32),
        scratch_shapes=[pltpu.VMEM((8, R, C), F32), pltpu.SemaphoreType.DMA((7,)), pltpu.SemaphoreType.DMA((7,))],
    )(pack)


ADAMW_BLOCK_BYTES = 2 * 1024 * 1024


def _adamw_block(R, C):
    padded = -(-C // 128) * 128
    rows = [rb for rb in range(8, R + 1, 8) if R % rb == 0 and rb * padded * 4 <= ADAMW_BLOCK_BYTES]
    if rows or R * padded * 4 <= ADAMW_BLOCK_BYTES:
        return (max(rows) if rows else R), C
    cols = [cb for cb in range(128, C + 1, 128) if C % cb == 0 and R * cb * 4 <= ADAMW_BLOCK_BYTES]
    return R, max(cols)


def _adamw(w, g, m, v, *, name, pass_grad=False):
    R, C = w.shape
    rb, cb = _adamw_block(R, C)
    c1 = 1.0 / (1.0 - B1 ** STEP)
    c2 = 1.0 / (1.0 - B2 ** STEP)

    def body(w_ref, g_ref, m_ref, v_ref, *outs):
        d_ref, nm_ref, nv_ref = outs[-3:]
        gv = g_ref[...]
        if pass_grad:
            outs[0][...] = gv
        m2 = B1 * m_ref[...] + (1.0 - B1) * gv
        v2 = B2 * v_ref[...] + (1.0 - B2) * gv * gv
        nm_ref[...] = m2
        nv_ref[...] = v2
        d_ref[...] = -LR * ((m2 * c1) / (jnp.sqrt(v2 * c2) + ADAM_EPS) + WD * w_ref[...])

    n_out = 4 if pass_grad else 3
    blk = pl.BlockSpec((rb, cb), lambda i, j: (i, j))
    return pl.pallas_call(
        body, name=name, grid=(R // rb, C // cb), in_specs=[blk] * 4, out_specs=[blk] * n_out,
        out_shape=[jax.ShapeDtypeStruct((R, C), F32)] * n_out, compiler_params=_params(2),
    )(w, g, m, v)


SMALL = (("norm1_w", D), ("norm2_w", D), ("final_norm_w", D), ("gate_b", GLA_H * DK), ("gla_norm_w", DV), ("sinks", N_Q))
PACK_W = 1024


def _pack_small(vals, w2, loss):
    rows = []
    for name, width in SMALL:
        v = vals[name].reshape(-1)
        rows.append(jnp.pad(v, (0, (-width) % PACK_W)).reshape(-1, PACK_W))
    rows.append(w2)
    rows.append(jnp.broadcast_to(loss.reshape(1, 1), (1, PACK_W)))
    pack = jnp.concatenate(rows, axis=0)
    return jnp.pad(pack, ((0, 32 - pack.shape[0]), (0, 0)))


def _unpack_small(pack):
    out, r = {}, 0
    for name, width in SMALL:
        nr = -(-width // PACK_W)
        out[name] = pack[r:r + nr].reshape(-1)[:width]
        r += nr
    out["w2"] = pack[r:r + RANK]
    out["loss"] = pack[r + RANK, 0]
    return out


def kernel(x, norm1_w, w_in, gla_gate_w2, gla_gate_b, attn_sinks, gla_norm_w, w_out, norm2_w, w_ffn_gate, w_ffn_up, w_ffn_down, final_norm_w, loss_target, m_norm1_w, m_w_in, m_gla_gate_w2, m_gla_gate_b, m_attn_sinks, m_gla_norm_w, m_w_out, m_norm2_w, m_w_ffn_gate, m_w_ffn_up, m_w_ffn_down, m_final_norm_w, v_norm1_w, v_w_in, v_gla_gate_w2, v_gla_gate_b, v_attn_sinks, v_gla_norm_w, v_w_out, v_norm2_w, v_w_ffn_gate, v_w_ffn_up, v_w_ffn_down, v_final_norm_w):
    chip = 2 * lax.axis_index("x") + lax.axis_index("y")
    w_in_s, w_out_s, w_gate_s, w_up_s, w_down_s, w2_s = (
        w_in[0], w_out[0], w_ffn_gate[0], w_ffn_up[0], w_ffn_down[0], gla_gate_w2[0])
    CS = SHARD_COLS

    half = lax.dynamic_slice_in_dim(w_in_s, lax.axis_index("c") * (D // 2), D // 2, axis=0)
    g_in, g_w2 = _gather_staged(half.astype(BF16), w2_s)
    g_out, = _gather_shards([w_out_s.astype(BF16)], name="gather_out", collective_id=1, after=[g_w2])
    g_gate, g_up = _gather_shards([w_gate_s.astype(BF16), w_up_s.astype(BF16)], name="gather_gate_up",
                                  collective_id=7, after=[g_w2])
    g_down, = _gather_shards([w_down_s.astype(BF16)], name="gather_down", collective_id=8, after=[g_w2])
    w_in_t = lax.optimization_barrier(jnp.transpose(w_in_s))
    m_in_t = lax.optimization_barrier(jnp.transpose(m_w_in[0]))
    v_in_t = lax.optimization_barrier(jnp.transpose(v_w_in[0]))
    w_main, w_lr = _w_in_to_main(_after(g_in, (w_in_t, m_in_t, v_in_t)))
    w2_full = jnp.transpose(g_w2, (1, 0, 2)).reshape(RANK, GLA_H * DK)
    p = dict(
        norm1_w=norm1_w, norm2_w=norm2_w, final_norm_w=final_norm_w.reshape(1, D), gate_b=gla_gate_b,
        gla_norm_w=gla_norm_w, sinks=jnp.pad(attn_sinks, ((0, 0), (0, 128 - N_Q))),
        w_in_main=w_main, w_in_lr=w_lr,
        w2p=jnp.pad(w2_full, ((0, LRP - RANK), (0, 0))).astype(BF16),
        w_out=g_out.reshape(D, D),
        w_gate=g_gate, w_up=g_up, w_down=g_down.reshape(FFN, D),
    )

    tags = ["w_in", "w_out", "w_gate", "w_up", "w_down"]
    recv = {}

    def exchange(tag, sums):
        recv[tag], = _chip_exchange([sums], name="chip_exchange_" + tag, collective_id=2 + tags.index(tag))
        return sums

    def reduce_part(tag, part):
        return exchange(tag, _pair_reduce(part, name="pair_reduce_" + tag))

    def reduce_ffn(tag, a, b):
        mine, got = _grad_matmul_split(a, b, cut="rows" if tag == "w_down" else "panels", name="mm_grad_" + tag)
        return exchange(tag, _pair_add(mine, got, name="pair_add_" + tag))

    def reduce_in(u_t, dproj, dproj_lr):
        mine, got = _grad_matmul_split(u_t, dproj, cut="whole", name="mm_grad_w_in")
        sum_main, = _pair_add(mine, got, name="pair_add_w_in")
        g_lr = _matmul(u_t, dproj_lr, out_dtype=BF16, name="mm_gin_lr")
        sum_lr, = _pair_reduce(g_lr[None], name="pair_reduce_w_in_lr")
        exchange("w_in", _main_to_shards(sum_main, sum_lr))

    done = {}
    shard = {"w_out": (w_out_s, m_w_out[0], v_w_out[0]), "w_gate": (w_gate_s, m_w_ffn_gate[0], v_w_ffn_gate[0]),
             "w_up": (w_up_s, m_w_ffn_up[0], v_w_ffn_up[0]), "w_down": (w_down_s, m_w_ffn_down[0], v_w_ffn_down[0])}

    def finish(tag):
        grad = _sum_join(recv[tag], name="sum_join_" + tag)
        done[tag] = _adamw(shard[tag][0], grad, shard[tag][1], shard[tag][2], name="adamw_" + tag, pass_grad=True)

    loss_blk, dx, g = _local_step(x[0], loss_target[0], p, reduce_part, reduce_ffn, reduce_in)
    for tag in ("w_down", "w_gate", "w_up", "w_out"):
        finish(tag)
    behind = [dx] + [done[tag][1] for tag in ("w_down", "w_gate", "w_up", "w_out")]
    g_in = _sum_join(_after(recv["w_in"], behind), name="sum_join_w_in")

    small = _unpack_small(_sum_small(_pack_small(g, g["w2"], loss_blk[0, 0])))
    loss = small["loss"]
    g_w2_mine = lax.dynamic_slice_in_dim(small["w2"], chip * (GLA_H * DK // 4), GLA_H * DK // 4, axis=1)

    grads = dict(
        norm1_w=small["norm1_w"].reshape(1, D), w_in=g_in, gla_gate_w2=g_w2_mine,
        gla_gate_b=small["gate_b"].reshape(1, -1), attn_sinks=small["sinks"].reshape(1, N_Q),
        gla_norm_w=small["gla_norm_w"].reshape(1, DV), norm2_w=small["norm2_w"].reshape(1, D),
        final_norm_w=small["final_norm_w"].reshape(1, D))
    def lin(a):
        return jnp.transpose(a.reshape(D, CS))

    def unlin(a):
        return jnp.transpose(a)

    grads["w_in"] = lin(g_in)
    weights = dict(
        norm1_w=(norm1_w, m_norm1_w, v_norm1_w), w_in=(w_in_t, m_in_t, v_in_t),
        gla_gate_w2=(w2_s, m_gla_gate_w2[0], v_gla_gate_w2[0]), gla_gate_b=(gla_gate_b, m_gla_gate_b, v_gla_gate_b),
        attn_sinks=(attn_sinks, m_attn_sinks, v_attn_sinks), gla_norm_w=(gla_norm_w, m_gla_norm_w, v_gla_norm_w),
        norm2_w=(norm2_w, m_norm2_w, v_norm2_w),
        final_norm_w=(final_norm_w.reshape(1, D), m_final_norm_w.reshape(1, D), v_final_norm_w.reshape(1, D)))
    names = ["norm1_w", "w_in", "gla_gate_w2", "gla_gate_b", "attn_sinks", "gla_norm_w", "w_out", "norm2_w",
             "w_ffn_gate", "w_ffn_up", "w_ffn_down", "final_norm_w"]
    lead = {"norm1_w": False, "gla_gate_b": False, "attn_sinks": False, "gla_norm_w": False, "norm2_w": False}
    g_out_l, d_out, m_out, v_out = [], [], [], []
    early = {"w_out": "w_out", "w_ffn_gate": "w_gate", "w_ffn_up": "w_up", "w_ffn_down": "w_down"}
    for nm in names:
        if nm in early:
            gr, dl, nmn, nvn = done[early[nm]]
            w = shard[early[nm]][0]
        else:
            w, m, v = weights[nm]
            gr = grads[nm]
            dl, nmn, nvn = _adamw(w, gr, m, v, name="adamw_" + nm)
        if nm == "w_in":
            gr, dl, nmn, nvn, w = g_in, unlin(dl), unlin(nmn), unlin(nvn), w_in_s
        if nm == "final_norm_w":
            shape = (D,)
        elif nm in lead:
            shape = w.shape
        else:
            shape = (1,) + w.shape
        g_out_l.append(gr.reshape(shape))
        d_out.append(dl.reshape(shape))
        m_out.append(nmn.reshape(shape))
        v_out.append(nvn.reshape(shape))
    return (loss, dx[None], *g_out_l, *d_out, *m_out, *v_out)
```
